```python
import math
import jax, jax.numpy as jnp
from jax import lax
import numpy as np

D_MODEL = 2048
BATCH = 8
SEQ = 4096
DEPTH = 1

CHUNK = 64
Q_BLOCK = 128
EPS = 1e-6
ROPE_THETA = 10000.0

RET_HEADS = 8
RET_QK_DIM = 128
RET_V_DIM = 256
RET_QK_W = RET_HEADS * RET_QK_DIM
RET_V_W = RET_HEADS * RET_V_DIM

MLA_HEADS = 16
Q_LORA = 512
KV_LORA = 512
QK_NOPE = 128
QK_ROPE = 64
V_HEAD = 128
MLA_QK_DIM = QK_NOPE + QK_ROPE
MLA_V_W = MLA_HEADS * V_HEAD

D_FF = 4 * D_MODEL

IN_SPLITS = (RET_QK_W, RET_QK_W, RET_V_W, RET_V_W, Q_LORA, KV_LORA, QK_ROPE, D_MODEL, D_MODEL)
N_IN = sum(IN_SPLITS)

kernel_name = "hybrid_retention_mla_gated_encoder"


def _rmsnorm(x, g):
    x32 = x.astype(jnp.float32)
    y = x32 * lax.rsqrt(jnp.mean(x32 * x32, axis=-1, keepdims=True) + EPS)
    return (y * g.astype(jnp.float32)).astype(x.dtype)


def _rope(t, positions):
    half = t.shape[-1] // 2
    inv = ROPE_THETA ** (-jnp.arange(half, dtype=jnp.float32) / half)
    ang = positions.astype(jnp.float32)[..., None] * inv
    cos = jnp.cos(ang)[:, :, None, :]
    sin = jnp.sin(ang)[:, :, None, :]
    t32 = t.astype(jnp.float32)
    t1, t2 = t32[..., :half], t32[..., half:]
    out = jnp.concatenate([t1 * cos - t2 * sin, t2 * cos + t1 * sin], axis=-1)
    return out.astype(t.dtype)


def _retention(q, k, v):
    B, S, H, dk = q.shape
    dv = v.shape[-1]
    nc = S // CHUNK
    log_gamma = jnp.log(1.0 - 2.0 ** (-5.0 - jnp.arange(H, dtype=jnp.float32)))

    def to_chunks(t):
        return t.reshape(B, nc, CHUNK, H, t.shape[-1]).transpose(1, 0, 3, 2, 4)

    pos = jnp.arange(CHUNK, dtype=jnp.float32)
    lg = log_gamma[:, None]
    intra = jnp.exp(lg[..., None] * jnp.abs(pos[:, None] - pos[None, :]))
    q_dec = jnp.exp(lg * (pos + 1.0))
    k_dec = jnp.exp(lg * (CHUNK - 1.0 - pos))
    c_dec = jnp.exp(log_gamma * CHUNK)

    def step(state, inp):
        qi, ki, vi = inp
        s = jnp.einsum('bhnd,bhmd->bhnm', qi, ki) * intra
        o = (jnp.einsum('bhnm,bhme->bhne', s, vi)
             + jnp.einsum('bhnd,bhde->bhne', qi * q_dec[..., None], state))
        state = state * c_dec[:, None, None] + jnp.einsum(
            'bhmd,bhme->bhde', ki * k_dec[..., None], vi)
        return state, o

    state0 = jnp.zeros((B, H, dk, dv), jnp.float32)
    _, o = lax.scan(step, state0, (to_chunks(q), to_chunks(k), to_chunks(v)))
    return o.transpose(1, 0, 3, 2, 4).reshape(B, S, H, dv)


def _block_causal_attention(q, k, v, scale):
    B, S, H, dqk = q.shape
    nb = S // Q_BLOCK
    qb = q.reshape(B, nb, Q_BLOCK, H, dqk).transpose(1, 0, 2, 3, 4)
    key_chunk = jnp.arange(S) // CHUNK

    def one(args):
        qi, bi = args
        q_chunk = (bi * Q_BLOCK + jnp.arange(Q_BLOCK)) // CHUNK
        mask = key_chunk[None, :] <= q_chunk[:, None]
        s = jnp.einsum('bqhd,bkhd->bhqk', qi, k,
                       preferred_element_type=jnp.float32) * scale
        s = jnp.where(mask, s, -jnp.inf)
        p = jax.nn.softmax(s, axis=-1).astype(v.dtype)
        return jnp.einsum('bhqk,bkhe->bqhe', p, v)

    o = lax.map(one, (qb, jnp.arange(nb)))
    return o.transpose(1, 0, 2, 3, 4).reshape(B, S, H, v.shape[-1])


def _mixer(u, positions, w_in, ret_norm_g, w_ret_o, q_a_norm_g, w_q_b,
           kv_a_norm_g, w_kv_b, w_mla_o, w_out):
    B, S, _ = u.shape
    proj = u @ w_in
    offs = []
    acc = 0
    for w in IN_SPLITS[:-1]:
        acc += w
        offs.append(acc)
    (r_q, r_k, r_v, r_g, c_q, c_kv, k_pe, g_ret, g_mla) = jnp.split(proj, offs, axis=-1)

    rq = _rope(r_q.reshape(B, S, RET_HEADS, RET_QK_DIM), positions).astype(jnp.float32)
    rk = _rope(r_k.reshape(B, S, RET_HEADS, RET_QK_DIM), positions).astype(jnp.float32)
    rk = rk * (RET_QK_DIM ** -0.5)
    rv = r_v.reshape(B, S, RET_HEADS, RET_V_DIM).astype(jnp.float32)
    ry = _retention(rq, rk, rv)
    mu = jnp.mean(ry, axis=-1, keepdims=True)
    var = jnp.mean(jnp.square(ry - mu), axis=-1, keepdims=True)
    ry = ((ry - mu) * lax.rsqrt(var + EPS)).reshape(B, S, RET_V_W) * ret_norm_g.astype(jnp.float32)
    ry = ry.astype(u.dtype) * jax.nn.silu(r_g)
    y_ret = ry @ w_ret_o

    q = (_rmsnorm(c_q, q_a_norm_g) @ w_q_b).reshape(B, S, MLA_HEADS, MLA_QK_DIM)
    q_nope, q_pe = q[..., :QK_NOPE], _rope(q[..., QK_NOPE:], positions)
    kv = (_rmsnorm(c_kv, kv_a_norm_g) @ w_kv_b).reshape(B, S, MLA_HEADS, QK_NOPE + V_HEAD)
    k_nope, v = kv[..., :QK_NOPE], kv[..., QK_NOPE:]
    k_pe = _rope(k_pe.reshape(B, S, 1, QK_ROPE), positions)
    qf = jnp.concatenate([q_nope, q_pe], axis=-1)
    kf = jnp.concatenate([k_nope, jnp.broadcast_to(k_pe, (B, S, MLA_HEADS, QK_ROPE))], axis=-1)
    my = _block_causal_attention(qf, kf, v, MLA_QK_DIM ** -0.5)
    y_mla = my.reshape(B, S, MLA_V_W) @ w_mla_o

    merged = jax.nn.sigmoid(g_ret) * y_ret + jax.nn.sigmoid(g_mla) * y_mla
    return merged @ w_out


def _fwd_setup_inputs(seed: int = 0) -> dict:
    key = jax.random.key(seed)
    ks = jax.random.split(key, 20)
    f32 = jnp.float32

    def nrm(k, shape, fan_in):
        return jax.random.normal(k, shape, f32) * (fan_in ** -0.5)

    def gain(k, shape):
        return 1.0 + 0.02 * jax.random.normal(k, shape, f32)

    x = jax.random.normal(ks[0], (BATCH, SEQ, D_MODEL), f32)
    start = jax.random.randint(ks[1], (BATCH, 1), 0, 4096, dtype=jnp.int32)
    positions = (start + jnp.arange(SEQ, dtype=jnp.int32)[None, :]).astype(jnp.int32)
    return {
        "x": x,
        "positions": positions,
        "norm_mix_g": gain(ks[2], (DEPTH, D_MODEL)),
        "w_in": nrm(ks[3], (DEPTH, D_MODEL, N_IN), D_MODEL),
        "ret_norm_g": gain(ks[4], (DEPTH, RET_V_W)),
        "w_ret_o": nrm(ks[5], (DEPTH, RET_V_W, D_MODEL), RET_V_W),
        "q_a_norm_g": gain(ks[6], (DEPTH, Q_LORA)),
        "w_q_b": nrm(ks[7], (DEPTH, Q_LORA, MLA_HEADS * MLA_QK_DIM), Q_LORA),
        "kv_a_norm_g": gain(ks[8], (DEPTH, KV_LORA)),
        "w_kv_b": nrm(ks[9], (DEPTH, KV_LORA, MLA_HEADS * (QK_NOPE + V_HEAD)), KV_LORA),
        "w_mla_o": nrm(ks[10], (DEPTH, MLA_V_W, D_MODEL), MLA_V_W),
        "w_out": nrm(ks[11], (DEPTH, D_MODEL, D_MODEL), D_MODEL),
        "norm_mlp_g": gain(ks[12], (DEPTH, D_MODEL)),
        "w_up": nrm(ks[13], (DEPTH, D_MODEL, D_FF), D_MODEL),
        "w_down": nrm(ks[14], (DEPTH, D_FF, D_MODEL), D_FF),
        "norm_f_g": gain(ks[15], (D_MODEL,)),
    }


def _fwd_reference(x, positions, norm_mix_g, w_in, ret_norm_g, w_ret_o, q_a_norm_g, w_q_b,
              kv_a_norm_g, w_kv_b, w_mla_o, w_out, norm_mlp_g, w_up, w_down, norm_f_g):
    h = x
    for l in range(DEPTH):
        u = _rmsnorm(h, norm_mix_g[l])
        h = h + _mixer(u, positions, w_in[l], ret_norm_g[l], w_ret_o[l], q_a_norm_g[l],
                       w_q_b[l], kv_a_norm_g[l], w_kv_b[l], w_mla_o[l], w_out[l])
        z = _rmsnorm(h, norm_mlp_g[l]) @ w_up[l]
        h = h + jnp.square(jax.nn.relu(z)) @ w_down[l]
    return _rmsnorm(h, norm_f_g)


import jax as _jax
import jax.numpy as _jnp

TWIN_FORMAT = 'train_step'
FWD_PARAMS = ['x', 'positions', 'norm_mix_g', 'w_in', 'ret_norm_g', 'w_ret_o', 'q_a_norm_g', 'w_q_b', 'kv_a_norm_g', 'w_kv_b', 'w_mla_o', 'w_out', 'norm_mlp_g', 'w_up', 'w_down', 'norm_f_g']
TWIN_WEIGHTS = ['norm_mix_g', 'w_in', 'ret_norm_g', 'w_ret_o', 'q_a_norm_g', 'w_q_b', 'kv_a_norm_g', 'w_kv_b', 'w_mla_o', 'w_out', 'norm_mlp_g', 'w_up', 'w_down', 'norm_f_g']
TWIN_DIFF_INPUT = 'x'
TWIN_INPUTS = ['x', 'positions', 'norm_mix_g', 'w_in', 'ret_norm_g', 'w_ret_o', 'q_a_norm_g', 'w_q_b', 'kv_a_norm_g', 'w_kv_b', 'w_mla_o', 'w_out', 'norm_mlp_g', 'w_up', 'w_down', 'norm_f_g', 'loss_target', 'm_norm_mix_g', 'm_w_in', 'm_ret_norm_g', 'm_w_ret_o', 'm_q_a_norm_g', 'm_w_q_b', 'm_kv_a_norm_g', 'm_w_kv_b', 'm_w_mla_o', 'm_w_out', 'm_norm_mlp_g', 'm_w_up', 'm_w_down', 'm_norm_f_g', 'v_norm_mix_g', 'v_w_in', 'v_ret_norm_g', 'v_w_ret_o', 'v_q_a_norm_g', 'v_w_q_b', 'v_kv_a_norm_g', 'v_w_kv_b', 'v_w_mla_o', 'v_w_out', 'v_norm_mlp_g', 'v_w_up', 'v_w_down', 'v_norm_f_g']
TWIN_OUTPUTS = ['loss', 'grad_x', 'grad_norm_mix_g', 'grad_w_in', 'grad_ret_norm_g', 'grad_w_ret_o', 'grad_q_a_norm_g', 'grad_w_q_b', 'grad_kv_a_norm_g', 'grad_w_kv_b', 'grad_w_mla_o', 'grad_w_out', 'grad_norm_mlp_g', 'grad_w_up', 'grad_w_down', 'grad_norm_f_g', 'delta_norm_mix_g', 'delta_w_in', 'delta_ret_norm_g', 'delta_w_ret_o', 'delta_q_a_norm_g', 'delta_w_q_b', 'delta_kv_a_norm_g', 'delta_w_kv_b', 'delta_w_mla_o', 'delta_w_out', 'delta_norm_mlp_g', 'delta_w_up', 'delta_w_down', 'delta_norm_f_g', 'new_m_norm_mix_g', 'new_m_w_in', 'new_m_ret_norm_g', 'new_m_w_ret_o', 'new_m_q_a_norm_g', 'new_m_w_q_b', 'new_m_kv_a_norm_g', 'new_m_w_kv_b', 'new_m_w_mla_o', 'new_m_w_out', 'new_m_norm_mlp_g', 'new_m_w_up', 'new_m_w_down', 'new_m_norm_f_g', 'new_v_norm_mix_g', 'new_v_w_in', 'new_v_ret_norm_g', 'new_v_w_ret_o', 'new_v_q_a_norm_g', 'new_v_w_q_b', 'new_v_kv_a_norm_g', 'new_v_w_kv_b', 'new_v_w_mla_o', 'new_v_w_out', 'new_v_norm_mlp_g', 'new_v_w_up', 'new_v_w_down', 'new_v_norm_f_g']
TWIN_LEAF_KINDS = {'loss': 'loss', 'grad_x': 'grad_x', 'grad_norm_mix_g': 'grad_w', 'grad_w_in': 'grad_w', 'grad_ret_norm_g': 'grad_w', 'grad_w_ret_o': 'grad_w', 'grad_q_a_norm_g': 'grad_w', 'grad_w_q_b': 'grad_w', 'grad_kv_a_norm_g': 'grad_w', 'grad_w_kv_b': 'grad_w', 'grad_w_mla_o': 'grad_w', 'grad_w_out': 'grad_w', 'grad_norm_mlp_g': 'grad_w', 'grad_w_up': 'grad_w', 'grad_w_down': 'grad_w', 'grad_norm_f_g': 'grad_w', 'delta_norm_mix_g': 'delta_w', 'delta_w_in': 'delta_w', 'delta_ret_norm_g': 'delta_w', 'delta_w_ret_o': 'delta_w', 'delta_q_a_norm_g': 'delta_w', 'delta_w_q_b': 'delta_w', 'delta_kv_a_norm_g': 'delta_w', 'delta_w_kv_b': 'delta_w', 'delta_w_mla_o': 'delta_w', 'delta_w_out': 'delta_w', 'delta_norm_mlp_g': 'delta_w', 'delta_w_up': 'delta_w', 'delta_w_down': 'delta_w', 'delta_norm_f_g': 'delta_w', 'new_m_norm_mix_g': 'new_m', 'new_m_w_in': 'new_m', 'new_m_ret_norm_g': 'new_m', 'new_m_w_ret_o': 'new_m', 'new_m_q_a_norm_g': 'new_m', 'new_m_w_q_b': 'new_m', 'new_m_kv_a_norm_g': 'new_m', 'new_m_w_kv_b': 'new_m', 'new_m_w_mla_o': 'new_m', 'new_m_w_out': 'new_m', 'new_m_norm_mlp_g': 'new_m', 'new_m_w_up': 'new_m', 'new_m_w_down': 'new_m', 'new_m_norm_f_g': 'new_m', 'new_v_norm_mix_g': 'new_v', 'new_v_w_in': 'new_v', 'new_v_ret_norm_g': 'new_v', 'new_v_w_ret_o': 'new_v', 'new_v_q_a_norm_g': 'new_v', 'new_v_w_q_b': 'new_v', 'new_v_kv_a_norm_g': 'new_v', 'new_v_w_kv_b': 'new_v', 'new_v_w_mla_o': 'new_v', 'new_v_w_out': 'new_v', 'new_v_norm_mlp_g': 'new_v', 'new_v_w_up': 'new_v', 'new_v_w_down': 'new_v', 'new_v_norm_f_g': 'new_v'}


def _forward(args):
    return _fwd_reference(*[args[k] for k in FWD_PARAMS])


def _output_shape():
    def fwd():
        inp = _fwd_setup_inputs(0)
        return _fwd_reference(*[inp[k] for k in FWD_PARAMS])
    out = _jax.eval_shape(fwd)
    return out.shape, out.dtype

N_MICROBATCH = 1
ADAM_LR = 0.001
ADAM_B1 = 0.9
ADAM_B2 = 0.999
ADAM_EPS = 1e-08
ADAM_WD = 0.01
ADAM_STEP = 10
PER_EXAMPLE_BATCH_AXIS = {'x': 0, 'positions': 0, 'loss_target': 0}
SHARED_INPUTS = []
_WEIGHT_DTYPES = {'norm_mix_g': _jnp.float32, 'w_in': _jnp.float32, 'ret_norm_g': _jnp.float32, 'w_ret_o': _jnp.float32, 'q_a_norm_g': _jnp.float32, 'w_q_b': _jnp.float32, 'kv_a_norm_g': _jnp.float32, 'w_kv_b': _jnp.float32, 'w_mla_o': _jnp.float32, 'w_out': _jnp.float32, 'norm_mlp_g': _jnp.float32, 'w_up': _jnp.float32, 'w_down': _jnp.float32, 'norm_f_g': _jnp.float32}
MOMENT_SCALE = {'norm_mix_g': 6.327475e-02, 'w_in': 2.663001e-02, 'ret_norm_g': 3.036342e-02, 'w_ret_o': 2.976409e-02, 'q_a_norm_g': 1.440096e-02, 'w_q_b': 5.819209e-03, 'kv_a_norm_g': 1.956597e-02, 'w_kv_b': 6.743288e-03, 'w_mla_o': 7.469276e-03, 'w_out': 3.060693e-02, 'norm_mlp_g': 7.968762e-02, 'w_up': 3.888038e-02, 'w_down': 7.848618e-02, 'norm_f_g': 1.614702e+01}


def _to_microbatches(a, axis):
    t = _jnp.moveaxis(a, axis, 0)
    t = t.reshape((N_MICROBATCH, t.shape[0] // N_MICROBATCH) + t.shape[1:])
    return _jnp.moveaxis(t, 1, axis + 1)


def setup_inputs(seed: int = 0) -> dict:
    inp = _fwd_setup_inputs(seed)
    key = _jax.random.fold_in(_jax.random.key(seed), 7919)
    shape, _ = _output_shape()
    out = dict(inp)
    out["loss_target"] = _jax.random.normal(_jax.random.fold_in(key, 0), shape, _jnp.float32)
    for i, name in enumerate(TWIN_WEIGHTS):
        w = inp[name].astype(_jnp.float32)
        if MOMENT_SCALE is None:
            s = _jnp.sqrt(_jnp.mean(_jnp.square(w)) + 1e-30)
        else:
            s = MOMENT_SCALE[name]
        km, kv = _jax.random.split(_jax.random.fold_in(key, i + 1))
        out[name] = w
        out["m_" + name] = s * _jax.random.normal(km, w.shape, _jnp.float32)
        out["v_" + name] = (s * s) * _jax.random.uniform(kv, w.shape, _jnp.float32, 0.5, 1.5)
    if N_MICROBATCH > 1:
        for name, axis in PER_EXAMPLE_BATCH_AXIS.items():
            out[name] = _to_microbatches(out[name], axis)
    return {'x': out['x'], 'positions': out['positions'], 'norm_mix_g': out['norm_mix_g'], 'w_in': out['w_in'], 'ret_norm_g': out['ret_norm_g'], 'w_ret_o': out['w_ret_o'], 'q_a_norm_g': out['q_a_norm_g'], 'w_q_b': out['w_q_b'], 'kv_a_norm_g': out['kv_a_norm_g'], 'w_kv_b': out['w_kv_b'], 'w_mla_o': out['w_mla_o'], 'w_out': out['w_out'], 'norm_mlp_g': out['norm_mlp_g'], 'w_up': out['w_up'], 'w_down': out['w_down'], 'norm_f_g': out['norm_f_g'], 'loss_target': out['loss_target'], 'm_norm_mix_g': out['m_norm_mix_g'], 'm_w_in': out['m_w_in'], 'm_ret_norm_g': out['m_ret_norm_g'], 'm_w_ret_o': out['m_w_ret_o'], 'm_q_a_norm_g': out['m_q_a_norm_g'], 'm_w_q_b': out['m_w_q_b'], 'm_kv_a_norm_g': out['m_kv_a_norm_g'], 'm_w_kv_b': out['m_w_kv_b'], 'm_w_mla_o': out['m_w_mla_o'], 'm_w_out': out['m_w_out'], 'm_norm_mlp_g': out['m_norm_mlp_g'], 'm_w_up': out['m_w_up'], 'm_w_down': out['m_w_down'], 'm_norm_f_g': out['m_norm_f_g'], 'v_norm_mix_g': out['v_norm_mix_g'], 'v_w_in': out['v_w_in'], 'v_ret_norm_g': out['v_ret_norm_g'], 'v_w_ret_o': out['v_w_ret_o'], 'v_q_a_norm_g': out['v_q_a_norm_g'], 'v_w_q_b': out['v_w_q_b'], 'v_kv_a_norm_g': out['v_kv_a_norm_g'], 'v_w_kv_b': out['v_w_kv_b'], 'v_w_mla_o': out['v_w_mla_o'], 'v_w_out': out['v_w_out'], 'v_norm_mlp_g': out['v_norm_mlp_g'], 'v_w_up': out['v_w_up'], 'v_w_down': out['v_w_down'], 'v_norm_f_g': out['v_norm_f_g']}


def _loss(weights, diff, rest, loss_target):
    with _jax.named_scope("forward"):
        args = {**rest, TWIN_DIFF_INPUT: diff, **{k: w.astype(_WEIGHT_DTYPES[k]) for k, w in weights.items()}}
        y = _forward(args)
    with _jax.named_scope("loss_head"):
        err = _jnp.square(y.astype(_jnp.float32) - loss_target)
        return 0.5 * _jnp.sum(_jnp.mean(err, axis=-1)) if err.ndim else 0.5 * err


def _adamw(w, g, m, v):
    m = ADAM_B1 * m + (1.0 - ADAM_B1) * g
    v = ADAM_B2 * v + (1.0 - ADAM_B2) * _jnp.square(g)
    m_hat = m / (1.0 - ADAM_B1 ** ADAM_STEP)
    v_hat = v / (1.0 - ADAM_B2 ** ADAM_STEP)
    delta = -ADAM_LR * (m_hat / (_jnp.sqrt(v_hat) + ADAM_EPS) + ADAM_WD * w)
    return delta, m, v


def reference(x, positions, norm_mix_g, w_in, ret_norm_g, w_ret_o, q_a_norm_g, w_q_b, kv_a_norm_g, w_kv_b, w_mla_o, w_out, norm_mlp_g, w_up, w_down, norm_f_g, loss_target, m_norm_mix_g, m_w_in, m_ret_norm_g, m_w_ret_o, m_q_a_norm_g, m_w_q_b, m_kv_a_norm_g, m_w_kv_b, m_w_mla_o, m_w_out, m_norm_mlp_g, m_w_up, m_w_down, m_norm_f_g, v_norm_mix_g, v_w_in, v_ret_norm_g, v_w_ret_o, v_q_a_norm_g, v_w_q_b, v_kv_a_norm_g, v_w_kv_b, v_w_mla_o, v_w_out, v_norm_mlp_g, v_w_up, v_w_down, v_norm_f_g):
    given = dict(x=x, positions=positions, norm_mix_g=norm_mix_g, w_in=w_in, ret_norm_g=ret_norm_g, w_ret_o=w_ret_o, q_a_norm_g=q_a_norm_g, w_q_b=w_q_b, kv_a_norm_g=kv_a_norm_g, w_kv_b=w_kv_b, w_mla_o=w_mla_o, w_out=w_out, norm_mlp_g=norm_mlp_g, w_up=w_up, w_down=w_down, norm_f_g=norm_f_g, loss_target=loss_target, m_norm_mix_g=m_norm_mix_g, m_w_in=m_w_in, m_ret_norm_g=m_ret_norm_g, m_w_ret_o=m_w_ret_o, m_q_a_norm_g=m_q_a_norm_g, m_w_q_b=m_w_q_b, m_kv_a_norm_g=m_kv_a_norm_g, m_w_kv_b=m_w_kv_b, m_w_mla_o=m_w_mla_o, m_w_out=m_w_out, m_norm_mlp_g=m_norm_mlp_g, m_w_up=m_w_up, m_w_down=m_w_down, m_norm_f_g=m_norm_f_g, v_norm_mix_g=v_norm_mix_g, v_w_in=v_w_in, v_ret_norm_g=v_ret_norm_g, v_w_ret_o=v_w_ret_o, v_q_a_norm_g=v_q_a_norm_g, v_w_q_b=v_w_q_b, v_kv_a_norm_g=v_kv_a_norm_g, v_w_kv_b=v_w_kv_b, v_w_mla_o=v_w_mla_o, v_w_out=v_w_out, v_norm_mlp_g=v_norm_mlp_g, v_w_up=v_w_up, v_w_down=v_w_down, v_norm_f_g=v_norm_f_g)
    weights = {n: given[n] for n in TWIN_WEIGHTS}
    shared = {n: given[n] for n in SHARED_INPUTS}
    per_example = {n: given[n] for n in ['x', 'positions']}
    grad_fn = _jax.value_and_grad(_loss, argnums=(0, 1))

    def one_microbatch(ex, loss_target):
        ex = dict(ex)
        diff = ex.pop(TWIN_DIFF_INPUT)
        return grad_fn(weights, diff, {**shared, **ex}, loss_target)

    if N_MICROBATCH == 1:
        loss, (grad_w, grad_x) = one_microbatch(per_example, given["loss_target"])
    else:
        def body(carry, xs):
            loss_sum, grad_sum = carry
            l_k, (gw_k, gx_k) = one_microbatch(xs[0], xs[1])
            with _jax.named_scope("update"):
                return (loss_sum + l_k, _jax.tree.map(_jnp.add, grad_sum, gw_k)), gx_k

        init = (_jnp.zeros((), _jnp.float32), _jax.tree.map(_jnp.zeros_like, weights))
        (loss, grad_w), grad_x = _jax.lax.scan(body, init, (per_example, given["loss_target"]))
    with _jax.named_scope("update"):
        delta_w, new_m, new_v = {}, {}, {}
        for n in TWIN_WEIGHTS:
            delta_w[n], new_m[n], new_v[n] = _adamw(weights[n], grad_w[n], given["m_" + n], given["v_" + n])
    return (loss, grad_x, *[grad_w[n] for n in TWIN_WEIGHTS], *[delta_w[n] for n in TWIN_WEIGHTS],
            *[new_m[n] for n in TWIN_WEIGHTS], *[new_v[n] for n in TWIN_WEIGHTS])
```

```python
import functools
import math

import jax
import jax.numpy as jnp
from jax import lax
from jax.experimental import pallas as pl
from jax.experimental.pallas import tpu as pltpu

F32 = jnp.float32
BF16 = jnp.bfloat16
MESH = pl.DeviceIdType.MESH

EPS = 1e-6
ROPE_THETA = 10000.0
CHUNK_SHIFT = 6
RET_QK = 128
RET_V = 256
NOPE = 128
ROPE = 64
VHEAD = 128
QPAD = 256
LANES = 128
N_DEV = 8
VMEM_LIMIT = 56 * 1024 * 1024

ADAM_LR = 0.001
ADAM_B1 = 0.9
ADAM_B2 = 0.999
ADAM_EPS = 1e-08
ADAM_WD = 0.01
ADAM_STEP = 10

NN = (((1,), (0,)), ((), ()))
NT = (((1,), (1,)), ((), ()))
TN = (((0,), (0,)), ((), ()))


def _dot(a, b, dims):
    return lax.dot_general(a.astype(BF16), b.astype(BF16), dims, preferred_element_type=F32)


def _tile(dim, pref):
    if dim <= pref:
        return dim
    t = (pref // LANES) * LANES
    while t >= LANES:
        if dim % t == 0:
            return t
        t -= LANES
    raise ValueError(f"no tile for {dim}")


def _params(sem):
    return pltpu.CompilerParams(dimension_semantics=sem, vmem_limit_bytes=VMEM_LIMIT)


def _sig(v):
    return 1.0 / (1.0 + jnp.exp(-v))


def _mm(name, a, b, mode, out_dtypes, *, tm=1024, tn=1024, tk=512, extras=(), epilogue=None):
    if mode == "nn":
        (m, k), (_, n) = a.shape, b.shape
    elif mode == "nt":
        (m, k), (n, _) = a.shape, b.shape
    else:
        (k, m), (_, n) = a.shape, b.shape
    tm, tn, tk = _tile(m, tm), _tile(n, tn), _tile(k, tk)
    nk = k // tk
    dims = {"nn": NN, "nt": NT, "tn": TN}[mode]
    a_spec = (pl.BlockSpec((tk, tm), lambda i, j, kk: (kk, i)) if mode == "tn"
              else pl.BlockSpec((tm, tk), lambda i, j, kk: (i, kk)))
    b_spec = (pl.BlockSpec((tn, tk), lambda i, j, kk: (j, kk)) if mode == "nt"
              else pl.BlockSpec((tk, tn), lambda i, j, kk: (kk, j)))
    tile_spec = pl.BlockSpec((tm, tn), lambda i, j, kk: (i, j))
    n_ex = len(extras)
    single = not isinstance(out_dtypes, (tuple, list))
    dts = (out_dtypes,) if single else tuple(out_dtypes)

    def body(a_ref, b_ref, *rest):
        ex, outs, acc = rest[:n_ex], rest[n_ex:-1], rest[-1]
        kk = pl.program_id(2)

        @pl.when(kk == 0)
        def _():
            acc[...] = jnp.zeros_like(acc)

        acc[...] += _dot(a_ref[...], b_ref[...], dims)

        @pl.when(kk == nk - 1)
        def _():
            r = acc[...]
            vals = (r,) if epilogue is None else epilogue(r, *[e[...] for e in ex])
            for o, v in zip(outs, vals):
                o[...] = v.astype(o.dtype)

    res = pl.pallas_call(
        body, name=name, grid=(m // tm, n // tn, nk),
        in_specs=[a_spec, b_spec] + [tile_spec] * n_ex,
        out_specs=[tile_spec] * len(dts),
        out_shape=[jax.ShapeDtypeStruct((m, n), d) for d in dts],
        scratch_shapes=[pltpu.VMEM((tm, tn), F32)],
        compiler_params=_params(("parallel", "parallel", "arbitrary")),
    )(a, b, *extras)
    return res[0] if single else res


def _rows(name, body, n_rows, tm, ins, outs, accs=()):
    in_specs, args = [], []
    for t in ins:
        if len(t) == 1:
            in_specs.append(pl.BlockSpec(t[0].shape, lambda i, nd=t[0].ndim: (0,) * nd))
        else:
            in_specs.append(pl.BlockSpec((tm, t[1]), lambda i, cb=t[2]: (i, cb)))
        args.append(t[0])
    out_specs = [pl.BlockSpec((tm, w), lambda i: (i, 0)) for w, _ in outs]
    out_specs += [pl.BlockSpec((r, w), lambda i: (0, 0)) for r, w in accs]
    out_shape = [jax.ShapeDtypeStruct((n_rows, w), d) for w, d in outs]
    out_shape += [jax.ShapeDtypeStruct((r, w), F32) for r, w in accs]
    return pl.pallas_call(
        body, name=name, grid=(n_rows // tm,), in_specs=in_specs, out_specs=out_specs, out_shape=out_shape,
        compiler_params=_params(("arbitrary",) if accs else ("parallel",)),
    )(*args)


def _zero_first(*accs):
    @pl.when(pl.program_id(0) == 0)
    def _():
        for a in accs:
            a[...] = jnp.zeros_like(a)


def _rope64(t, cos, sin):
    return t * cos + pltpu.roll(t, RET_QK // 2, 1) * sin


def _rope32(t, cos, sin_a, sin_b):
    return t * cos + pltpu.roll(t, LANES - ROPE // 2, 1) * sin_a + pltpu.roll(t, ROPE // 2, 1) * sin_b


def _rms_fwd(name, x, g, tm):
    s, d = x.shape

    def body(x_ref, g_ref, u_ref):
        v = x_ref[...]
        r = lax.rsqrt(jnp.mean(v * v, axis=-1, keepdims=True) + EPS)
        u_ref[...] = (v * r * g_ref[...]).astype(BF16)

    return _rows(name, body, s, tm, [(x, d, 0), (g,)], [(d, BF16)])[0]


def _rms_res_fwd(name, x, mix, g, tm):
    s, d = x.shape

    def body(x_ref, m_ref, g_ref, h_ref, u_ref):
        v = x_ref[...] + m_ref[...]
        h_ref[...] = v
        r = lax.rsqrt(jnp.mean(v * v, axis=-1, keepdims=True) + EPS)
        u_ref[...] = (v * r * g_ref[...]).astype(BF16)

    return _rows(name, body, s, tm, [(x, d, 0), (mix, d, 0), (g,)], [(d, F32), (d, BF16)])


def _rms_bwd(name, dy, x, g, dres, tm):
    s, d = x.shape

    def body(dy_ref, x_ref, g_ref, dres_ref, dx_ref, dg_ref):
        _zero_first(dg_ref)
        v, dyv = x_ref[...], dy_ref[...]
        r = lax.rsqrt(jnp.mean(v * v, axis=-1, keepdims=True) + EPS)
        xh = v * r
        dxh = dyv * g_ref[...]
        dx_ref[...] = dres_ref[...] + r * (dxh - xh * jnp.mean(dxh * xh, axis=-1, keepdims=True))
        dg_ref[...] += jnp.sum(dyv * xh, axis=0, keepdims=True)

    return _rows(name, body, s, tm, [(dy, d, 0), (x, d, 0), (g,), (dres, d, 0)], [(d, F32)], [(1, d)])


def _final(name, h1, dn, g, tgt, tm):
    s, d = h1.shape

    def body(h_ref, dn_ref, g_ref, t_ref, dh_ref, dg_ref, loss_ref):
        _zero_first(dg_ref, loss_ref)
        v = h_ref[...] + dn_ref[...]
        r = lax.rsqrt(jnp.mean(v * v, axis=-1, keepdims=True) + EPS)
        xh = v * r
        gv = g_ref[...]
        e = xh * gv - t_ref[...]
        loss_ref[...] += 0.5 * jnp.sum(jnp.mean(e * e, axis=-1, keepdims=True))
        dy = e * (1.0 / d)
        dg_ref[...] += jnp.sum(dy * xh, axis=0, keepdims=True)
        dxh = dy * gv
        dh_ref[...] = r * (dxh - xh * jnp.mean(dxh * xh, axis=-1, keepdims=True))

    return _rows(name, body, s, tm, [(h1, d, 0), (dn, d, 0), (g,), (tgt, d, 0)], [(d, F32)], [(1, d), (1, LANES)])


def _decay_mask(lg, blk):
    n = lax.broadcasted_iota(jnp.int32, (blk, blk), 0)
    m = lax.broadcasted_iota(jnp.int32, (blk, blk), 1)
    w = jnp.exp(lg * jnp.abs(n - m).astype(F32))
    return jnp.where(jnp.right_shift(m, CHUNK_SHIFT) <= jnp.right_shift(n, CHUNK_SHIFT), w, 0.0)


def _decays(lg, blk):
    pos = lax.broadcasted_iota(jnp.int32, (blk, 1), 0).astype(F32)
    return jnp.exp(lg * (pos + 1.0)), jnp.exp(lg * (blk - 1.0 - pos)), jnp.exp(lg * float(blk))


def _ret_fwd(proj, lay, cos, sin, lgs, blk):
    s = proj.shape[0]
    heads = lay["ret_heads"]
    nb = s // blk
    scale = RET_QK ** -0.5

    def body(lg_ref, q_ref, k_ref, v_ref, cos_ref, sin_ref, o_ref, st_ref, state, mask):
        lg = lg_ref[0:1, 0:1]

        @pl.when(pl.program_id(1) == 0)
        def _():
            state[...] = jnp.zeros_like(state)
            mask[...] = _decay_mask(lg, blk)

        a, c, gb = _decays(lg, blk)
        q = _rope64(q_ref[...], cos_ref[...], sin_ref[...])
        k = _rope64(k_ref[...], cos_ref[...], sin_ref[...]) * scale
        v = v_ref[...]
        st = state[...]
        st_ref[...] = st
        sm = _dot(q, k, NT) * mask[...]
        o_ref[...] = _dot(sm, v, NN) + _dot(q * a, st, NN)
        state[...] = st * gb + _dot(k * c, v, TN)

    qb, kb, vb = lay["off"]["r_q"] // RET_QK, lay["off"]["r_k"] // RET_QK, lay["off"]["r_v"] // RET_V
    return pl.pallas_call(
        body, name="ret_fwd", grid=(heads, nb),
        in_specs=[pl.BlockSpec((None, 8, LANES), lambda h, b: (h, 0, 0)),
                  pl.BlockSpec((blk, RET_QK), lambda h, b: (b, qb + h)),
                  pl.BlockSpec((blk, RET_QK), lambda h, b: (b, kb + h)),
                  pl.BlockSpec((blk, RET_V), lambda h, b: (b, vb + h)),
                  pl.BlockSpec((blk, LANES), lambda h, b: (b, 0)),
                  pl.BlockSpec((blk, LANES), lambda h, b: (b, 0))],
        out_specs=[pl.BlockSpec((blk, RET_V), lambda h, b: (b, h)),
                   pl.BlockSpec((None, None, RET_QK, RET_V), lambda h, b: (h, b, 0, 0))],
        out_shape=[jax.ShapeDtypeStruct((s, heads * RET_V), F32),
                   jax.ShapeDtypeStruct((heads, nb, RET_QK, RET_V), F32)],
        scratch_shapes=[pltpu.VMEM((RET_QK, RET_V), F32), pltpu.VMEM((blk, blk), F32)],
        compiler_params=_params(("parallel", "arbitrary")),
    )(lgs, proj, proj, proj, cos, sin)


def _ret_bwd(proj, lay, cos, sin, lgs, states, d_o, blk):
    s = proj.shape[0]
    heads = lay["ret_heads"]
    nb = s // blk
    scale = RET_QK ** -0.5

    def body(lg_ref, q_ref, k_ref, v_ref, cos_ref, sin_ref, st_ref, do_ref, dq_ref, dk_ref, dv_ref, dstate, mask):
        lg = lg_ref[0:1, 0:1]

        @pl.when(pl.program_id(1) == 0)
        def _():
            dstate[...] = jnp.zeros_like(dstate)
            mask[...] = _decay_mask(lg, blk)

        a, c, gb = _decays(lg, blk)
        cs, sn = cos_ref[...], sin_ref[...]
        q = _rope64(q_ref[...], cs, sn)
        k = _rope64(k_ref[...], cs, sn) * scale
        v = v_ref[...]
        st = st_ref[...]
        do = do_ref[...]
        dst = dstate[...]
        mk = mask[...]
        sm = _dot(q, k, NT) * mk
        ds = _dot(do, v, NT) * mk
        dq = _dot(ds, k, NN) + _dot(do, st, NT) * a
        dk = _dot(ds, q, TN) + _dot(v, dst, NT) * c
        dv_ref[...] = (_dot(sm, do, TN) + _dot(k * c, dst, NN)).astype(dv_ref.dtype)
        dstate[...] = dst * gb + _dot(q * a, do, TN)
        dq_ref[...] = _rope64(dq, cs, -sn).astype(dq_ref.dtype)
        dk_ref[...] = (_rope64(dk, cs, -sn) * scale).astype(dk_ref.dtype)

    qb, kb, vb = lay["off"]["r_q"] // RET_QK, lay["off"]["r_k"] // RET_QK, lay["off"]["r_v"] // RET_V
    last = nb - 1
    return pl.pallas_call(
        body, name="ret_bwd", grid=(heads, nb),
        in_specs=[pl.BlockSpec((None, 8, LANES), lambda h, b: (h, 0, 0)),
                  pl.BlockSpec((blk, RET_QK), lambda h, b: (last - b, qb + h)),
                  pl.BlockSpec((blk, RET_QK), lambda h, b: (last - b, kb + h)),
                  pl.BlockSpec((blk, RET_V), lambda h, b: (last - b, vb + h)),
                  pl.BlockSpec((blk, LANES), lambda h, b: (last - b, 0)),
                  pl.BlockSpec((blk, LANES), lambda h, b: (last - b, 0)),
                  pl.BlockSpec((None, None, RET_QK, RET_V), lambda h, b: (h, last - b, 0, 0)),
                  pl.BlockSpec((blk, RET_V), lambda h, b: (last - b, h))],
        out_specs=[pl.BlockSpec((blk, RET_QK), lambda h, b: (last - b, h)),
                   pl.BlockSpec((blk, RET_QK), lambda h, b: (last - b, h)),
                   pl.BlockSpec((blk, RET_V), lambda h, b: (last - b, h))],
        out_shape=[jax.ShapeDtypeStruct((s, heads * RET_QK), BF16),
                   jax.ShapeDtypeStruct((s, heads * RET_QK), BF16),
                   jax.ShapeDtypeStruct((s, heads * RET_V), BF16)],
        scratch_shapes=[pltpu.VMEM((RET_QK, RET_V), F32), pltpu.VMEM((blk, blk), F32)],
        compiler_params=_params(("parallel", "arbitrary")),
    )(lgs, proj, proj, proj, cos, sin, states, d_o)


def _ret_post(proj, lay, o, g, tm):
    s, vw = o.shape
    heads = lay["ret_heads"]

    def body(o_ref, rg_ref, g_ref, ry_ref):
        for h in range(heads):
            sl = slice(h * RET_V, (h + 1) * RET_V)
            oh = o_ref[:, sl]
            dlt = oh - jnp.mean(oh, axis=-1, keepdims=True)
            rstd = lax.rsqrt(jnp.mean(dlt * dlt, axis=-1, keepdims=True) + EPS)
            rg = rg_ref[:, sl]
            ry_ref[:, sl] = (dlt * rstd * g_ref[:, sl] * (rg * _sig(rg))).astype(BF16)

    return _rows("ret_post", body, s, tm, [(o, vw, 0), (proj, vw, lay["off"]["r_g"] // vw), (g,)], [(vw, BF16)])[0]


def _ret_post_bwd(proj, lay, d_ry, o, g, tm):
    s, vw = o.shape
    heads = lay["ret_heads"]

    def body(dry_ref, o_ref, rg_ref, g_ref, do_ref, drg_ref, dg_ref):
        _zero_first(dg_ref)
        for h in range(heads):
            sl = slice(h * RET_V, (h + 1) * RET_V)
            oh = o_ref[:, sl]
            dlt = oh - jnp.mean(oh, axis=-1, keepdims=True)
            rstd = lax.rsqrt(jnp.mean(dlt * dlt, axis=-1, keepdims=True) + EPS)
            oh = dlt * rstd
            gv = g_ref[:, sl]
            rg = rg_ref[:, sl]
            sg = _sig(rg)
            dry = dry_ref[:, sl]
            dt = dry * (rg * sg)
            drg_ref[:, sl] = (dry * (oh * gv) * (sg * (1.0 + rg * (1.0 - sg)))).astype(BF16)
            dg_ref[:, sl] += jnp.sum(dt * oh, axis=0, keepdims=True)
            doh = dt * gv
            do_ref[:, sl] = rstd * (doh - jnp.mean(doh, axis=-1, keepdims=True)
                                    - oh * jnp.mean(doh * oh, axis=-1, keepdims=True))

    return _rows("ret_post_bwd", body, s, tm,
                 [(d_ry, vw, 0), (o, vw, 0), (proj, vw, lay["off"]["r_g"] // vw), (g,)],
                 [(vw, F32), (vw, BF16)], [(1, vw)])


def _mla_prep(proj, lay, gq, gkv, cos, sin_a, sin_b, tm):
    s = proj.shape[0]
    ql, kl = lay["q_lora"], lay["kv_lora"]

    def body(cq_ref, ckv_ref, kpe_ref, gq_ref, gkv_ref, cos_ref, sa_ref, sb_ref, cqn_ref, ckvn_ref, kpr_ref):
        for src, gref, dst in ((cq_ref, gq_ref, cqn_ref), (ckv_ref, gkv_ref, ckvn_ref)):
            v = src[...]
            r = lax.rsqrt(jnp.mean(v * v, axis=-1, keepdims=True) + EPS)
            dst[...] = (v * r * gref[...]).astype(BF16)
        kpr_ref[...] = _rope32(kpe_ref[...], cos_ref[...], sa_ref[...], sb_ref[...]).astype(BF16)

    off = lay["off"]
    return _rows("mla_prep", body, s, tm,
                 [(proj, ql, off["c_q"] // ql), (proj, kl, off["c_kv"] // kl), (proj, LANES, off["k_pe"] // LANES),
                  (gq,), (gkv,), (cos, LANES, 0), (sin_a, LANES, 0), (sin_b, LANES, 0)],
                 [(ql, BF16), (kl, BF16), (LANES, BF16)])


def _mla_prep_bwd(proj, lay, d_cqn, d_ckvn, gq, gkv, tm):
    s = proj.shape[0]
    ql, kl = lay["q_lora"], lay["kv_lora"]

    def body(dq_ref, dkv_ref, cq_ref, ckv_ref, gq_ref, gkv_ref, dcq_ref, dckv_ref, dgq_ref, dgkv_ref):
        _zero_first(dgq_ref, dgkv_ref)
        for dref, src, gref, dst, dg in ((dq_ref, cq_ref, gq_ref, dcq_ref, dgq_ref),
                                         (dkv_ref, ckv_ref, gkv_ref, dckv_ref, dgkv_ref)):
            v, dy = src[...], dref[...]
            r = lax.rsqrt(jnp.mean(v * v, axis=-1, keepdims=True) + EPS)
            xh = v * r
            dxh = dy * gref[...]
            dst[...] = (r * (dxh - xh * jnp.mean(dxh * xh, axis=-1, keepdims=True))).astype(BF16)
            dg[...] += jnp.sum(dy * xh, axis=0, keepdims=True)

    off = lay["off"]
    return _rows("mla_prep_bwd", body, s, tm,
                 [(d_cqn, ql, 0), (d_ckvn, kl, 0), (proj, ql, off["c_q"] // ql), (proj, kl, off["c_kv"] // kl),
                  (gq,), (gkv,)],
                 [(ql, BF16), (kl, BF16)], [(1, ql), (1, kl)])


def _attn_prep(qp, kv, kpr, lay, cos, sin_a, sin_b, tm):
    s = qp.shape[0]
    heads = lay["mla_heads"]
    w = heads * QPAD

    def body(qp_ref, kv_ref, kpr_ref, cos_ref, sa_ref, sb_ref, qf_ref, kf_ref):
        cs, sa, sb = cos_ref[...], sa_ref[...], sb_ref[...]
        kp = kpr_ref[...]
        for h in range(heads):
            lo, hi = h * QPAD, h * QPAD + NOPE
            qf_ref[:, lo:hi] = qp_ref[:, lo:hi].astype(BF16)
            qf_ref[:, hi:hi + LANES] = _rope32(qp_ref[:, hi:hi + LANES], cs, sa, sb).astype(BF16)
            kf_ref[:, lo:hi] = kv_ref[:, lo:hi]
            kf_ref[:, hi:hi + LANES] = kp

    return _rows("attn_prep", body, s, tm,
                 [(qp, w, 0), (kv, w, 0), (kpr, LANES, 0), (cos, LANES, 0), (sin_a, LANES, 0), (sin_b, LANES, 0)],
                 [(w, BF16), (w, BF16)])


def _attn_post_bwd(dqf, dkf, dv, lay, cos, sin_a, sin_b, tm):
    s = dqf.shape[0]
    heads = lay["mla_heads"]
    w = heads * QPAD

    def body(dqf_ref, dkf_ref, dv_ref, cos_ref, sa_ref, sb_ref, dqp_ref, dkv_ref, dkpe_ref):
        cs, sa, sb = cos_ref[...], -sa_ref[...], -sb_ref[...]
        kpe = jnp.zeros((tm, LANES), F32)
        for h in range(heads):
            lo, hi = h * QPAD, h * QPAD + NOPE
            dqp_ref[:, lo:hi] = dqf_ref[:, lo:hi].astype(BF16)
            dqp_ref[:, hi:hi + LANES] = _rope32(dqf_ref[:, hi:hi + LANES], cs, sa, sb).astype(BF16)
            dkv_ref[:, lo:hi] = dkf_ref[:, lo:hi].astype(BF16)
            dkv_ref[:, hi:hi + LANES] = dv_ref[:, h * VHEAD:(h + 1) * VHEAD].astype(BF16)
            kpe = kpe + dkf_ref[:, hi:hi + LANES]
        dkpe_ref[...] = _rope32(kpe, cs, sa, sb).astype(BF16)

    return _rows("attn_post_bwd", body, s, tm,
                 [(dqf, w, 0), (dkf, w, 0), (dv, heads * VHEAD, 0), (cos, LANES, 0), (sin_a, LANES, 0),
                  (sin_b, LANES, 0)],
                 [(w, BF16), (w, BF16), (LANES, BF16)])


def _chunk_mask(i, j, t):
    qpos = i * t + lax.broadcasted_iota(jnp.int32, (t, t), 0)
    kpos = j * t + lax.broadcasted_iota(jnp.int32, (t, t), 1)
    return jnp.right_shift(kpos, CHUNK_SHIFT) <= jnp.right_shift(qpos, CHUNK_SHIFT)


def _attn_fwd(qf, kf, kv, lay, t):
    s = qf.shape[0]
    heads = lay["mla_heads"]
    nt = s // t
    scale = (NOPE + ROPE) ** -0.5

    def body(q_ref, k_ref, v_ref, o_ref, lse_ref, m_s, l_s, acc):
        i, j = pl.program_id(1), pl.program_id(2)

        @pl.when(j == 0)
        def _():
            m_s[...] = jnp.full_like(m_s, -jnp.inf)
            l_s[...] = jnp.zeros_like(l_s)
            acc[...] = jnp.zeros_like(acc)

        def step(masked):
            sc = _dot(q_ref[...], k_ref[...], NT) * scale
            if masked:
                sc = jnp.where(_chunk_mask(i, j, t), sc, -jnp.inf)
            m_old = m_s[...]
            m_new = jnp.maximum(m_old, jnp.max(sc, axis=-1, keepdims=True))
            alpha = jnp.exp(m_old - m_new)
            p = jnp.exp(sc - m_new)
            l_s[...] = alpha * l_s[...] + jnp.sum(p, axis=-1, keepdims=True)
            acc[...] = alpha * acc[...] + _dot(p, v_ref[...], NN)
            m_s[...] = m_new

        pl.when(j < i)(functools.partial(step, False))
        pl.when(j == i)(functools.partial(step, True))

        @pl.when(j == nt - 1)
        def _():
            o_ref[...] = acc[...] / l_s[...]
            lse_ref[...] = m_s[...] + jnp.log(l_s[...])

    return pl.pallas_call(
        body, name="attn_fwd", grid=(heads, nt, nt),
        in_specs=[pl.BlockSpec((t, QPAD), lambda h, i, j: (i, h)),
                  pl.BlockSpec((t, QPAD), lambda h, i, j: (jnp.minimum(i, j), h)),
                  pl.BlockSpec((t, VHEAD), lambda h, i, j: (jnp.minimum(i, j), 2 * h + 1))],
        out_specs=[pl.BlockSpec((t, VHEAD), lambda h, i, j: (i, h)),
                   pl.BlockSpec((None, t, 1), lambda h, i, j: (h, i, 0))],
        out_shape=[jax.ShapeDtypeStruct((s, heads * VHEAD), F32),
                   jax.ShapeDtypeStruct((heads, s, 1), F32)],
        scratch_shapes=[pltpu.VMEM((t, 1), F32), pltpu.VMEM((t, 1), F32), pltpu.VMEM((t, VHEAD), F32)],
        compiler_params=_params(("parallel", "parallel", "arbitrary")),
    )(qf, kf, kv)


def _attn_bwd_dq(qf, kf, kv, o, lse, d_o, lay, t):
    s = qf.shape[0]
    heads = lay["mla_heads"]
    nt = s // t
    scale = (NOPE + ROPE) ** -0.5

    def body(q_ref, k_ref, v_ref, o_ref, lse_ref, do_ref, dq_ref, dl_ref, acc):
        i, j = pl.program_id(1), pl.program_id(2)

        @pl.when(j == 0)
        def _():
            acc[...] = jnp.zeros_like(acc)
            dl_ref[...] = jnp.sum(do_ref[...] * o_ref[...], axis=-1, keepdims=True)

        def step(masked):
            sc = _dot(q_ref[...], k_ref[...], NT) * scale
            p = jnp.exp(sc - lse_ref[...])
            if masked:
                p = jnp.where(_chunk_mask(i, j, t), p, 0.0)
            dp = _dot(do_ref[...], v_ref[...], NT)
            ds = p * (dp - dl_ref[...]) * scale
            acc[...] += _dot(ds, k_ref[...], NN)

        pl.when(j < i)(functools.partial(step, False))
        pl.when(j == i)(functools.partial(step, True))

        @pl.when(j == nt - 1)
        def _():
            dq_ref[...] = acc[...]

    return pl.pallas_call(
        body, name="attn_bwd_dq", grid=(heads, nt, nt),
        in_specs=[pl.BlockSpec((t, QPAD), lambda h, i, j: (i, h)),
                  pl.BlockSpec((t, QPAD), lambda h, i, j: (jnp.minimum(i, j), h)),
                  pl.BlockSpec((t, VHEAD), lambda h, i, j: (jnp.minimum(i, j), 2 * h + 1)),
                  pl.BlockSpec((t, VHEAD), lambda h, i, j: (i, h)),
                  pl.BlockSpec((None, t, 1), lambda h, i, j: (h, i, 0)),
                  pl.BlockSpec((t, VHEAD), lambda h, i, j: (i, h))],
        out_specs=[pl.BlockSpec((t, QPAD), lambda h, i, j: (i, h)),
                   pl.BlockSpec((None, t, 1), lambda h, i, j: (h, i, 0))],
        out_shape=[jax.ShapeDtypeStruct((s, heads * QPAD), F32),
                   jax.ShapeDtypeStruct((heads, s, 1), F32)],
        scratch_shapes=[pltpu.VMEM((t, QPAD), F32)],
        compiler_params=_params(("parallel", "parallel", "arbitrary")),
    )(qf, kf, kv, o, lse, d_o)


def _attn_bwd_dkv(qf, kf, kv, lse, delta, d_o, lay, t):
    s = qf.shape[0]
    heads = lay["mla_heads"]
    nt = s // t
    scale = (NOPE + ROPE) ** -0.5

    def body(q_ref, k_ref, v_ref, lse_ref, dl_ref, do_ref, dk_ref, dv_ref, dk_acc, dv_acc):
        j, i = pl.program_id(1), pl.program_id(2)

        @pl.when(i == 0)
        def _():
            dk_acc[...] = jnp.zeros_like(dk_acc)
            dv_acc[...] = jnp.zeros_like(dv_acc)

        def step(masked):
            sc = _dot(q_ref[...], k_ref[...], NT) * scale
            p = jnp.exp(sc - lse_ref[...])
            if masked:
                p = jnp.where(_chunk_mask(i, j, t), p, 0.0)
            dv_acc[...] += _dot(p, do_ref[...], TN)
            dp = _dot(do_ref[...], v_ref[...], NT)
            ds = p * (dp - dl_ref[...]) * scale
            dk_acc[...] += _dot(ds, q_ref[...], TN)

        pl.when(i > j)(functools.partial(step, False))
        pl.when(i == j)(functools.partial(step, True))

        @pl.when(i == nt - 1)
        def _():
            dk_ref[...] = dk_acc[...]
            dv_ref[...] = dv_acc[...]

    return pl.pallas_call(
        body, name="attn_bwd_dkv", grid=(heads, nt, nt),
        in_specs=[pl.BlockSpec((t, QPAD), lambda h, j, i: (jnp.maximum(i, j), h)),
                  pl.BlockSpec((t, QPAD), lambda h, j, i: (j, h)),
                  pl.BlockSpec((t, VHEAD), lambda h, j, i: (j, 2 * h + 1)),
                  pl.BlockSpec((None, t, 1), lambda h, j, i: (h, jnp.maximum(i, j), 0)),
                  pl.BlockSpec((None, t, 1), lambda h, j, i: (h, jnp.maximum(i, j), 0)),
                  pl.BlockSpec((t, VHEAD), lambda h, j, i: (jnp.maximum(i, j), h))],
        out_specs=[pl.BlockSpec((t, QPAD), lambda h, j, i: (j, h)),
                   pl.BlockSpec((t, VHEAD), lambda h, j, i: (j, h))],
        out_shape=[jax.ShapeDtypeStruct((s, heads * QPAD), F32),
                   jax.ShapeDtypeStruct((s, heads * VHEAD), F32)],
        scratch_shapes=[pltpu.VMEM((t, QPAD), F32), pltpu.VMEM((t, VHEAD), F32)],
        compiler_params=_params(("parallel", "parallel", "arbitrary")),
    )(qf, kf, kv, lse, delta, d_o)


def _merge(proj, lay, y_ret, y_mla, tm):
    s, d = y_ret.shape

    def body(gr_ref, gm_ref, yr_ref, ym_ref, out_ref):
        out_ref[...] = (_sig(gr_ref[...]) * yr_ref[...] + _sig(gm_ref[...]) * ym_ref[...]).astype(BF16)

    off = lay["off"]
    return _rows("merge", body, s, tm,
                 [(proj, d, off["g_ret"] // d), (proj, d, off["g_mla"] // d), (y_ret, d, 0), (y_mla, d, 0)],
                 [(d, BF16)])[0]


def _merge_bwd(proj, lay, d_merged, y_ret, y_mla, tm):
    s, d = y_ret.shape

    def body(dm_ref, gr_ref, gm_ref, yr_ref, ym_ref, dyr_ref, dym_ref, dgr_ref, dgm_ref):
        dm = dm_ref[...]
        for g_ref, y_ref, dy_ref, dg_ref in ((gr_ref, yr_ref, dyr_ref, dgr_ref), (gm_ref, ym_ref, dym_ref, dgm_ref)):
            sg = _sig(g_ref[...])
            dy_ref[...] = (dm * sg).astype(BF16)
            dg_ref[...] = (dm * y_ref[...] * (sg * (1.0 - sg))).astype(BF16)

    off = lay["off"]
    return _rows("merge_bwd", body, s, tm,
                 [(d_merged, d, 0), (proj, d, off["g_ret"] // d), (proj, d, off["g_mla"] // d), (y_ret, d, 0),
                  (y_mla, d, 0)],
                 [(d, BF16)] * 4)


ANY = pl.BlockSpec(memory_space=pl.ANY)


def _place():
    return lax.axis_index("x"), lax.axis_index("y"), lax.axis_index("c")


def _other_chips(x, y):
    return [(1 - x, y), (x, 1 - y), (1 - x, 1 - y)]


def _all_gather(shards):
    nw = len(shards)

    def body(*refs):
        ins, outs = refs[:nw], refs[nw:2 * nw]
        send_sems, recv_sems, local_sems = refs[2 * nw:]
        x, y, c = _place()
        me, sibling = (x, y, c), (x, y, 1 - c)
        chips = _other_chips(x, y)

        def slot(p):
            return 4 * p[0] + 2 * p[1] + p[2]

        def copy(w, k, block, to, src=None):
            dst = outs[w].at[slot(block)]
            return pltpu.make_async_remote_copy(
                src_ref=dst if src is None else src, dst_ref=dst,
                send_sem=send_sems.at[w, k], recv_sem=recv_sems.at[w, k], device_id=to, device_id_type=MESH)

        mine = [pltpu.make_async_copy(ins[w], outs[w].at[slot(me)], local_sems.at[w]) for w in range(nw)]
        for cp in mine:
            cp.start()
        first = []
        for w in range(nw):
            first.append(copy(w, 0, me, sibling, src=ins[w]))
            first += [copy(w, 1 + j, me, (*chip, c), src=ins[w]) for j, chip in enumerate(chips)]
        for cp in first:
            cp.start()
        passed = []
        for j, chip in enumerate(chips):
            for w in range(nw):
                copy(w, 1 + j, (*chip, c), me).wait_recv()
                cp = copy(w, 4 + j, (*chip, c), sibling)
                cp.start()
                passed.append(cp)
        for w in range(nw):
            copy(w, 0, sibling, me).wait_recv()
            for j, chip in enumerate(chips):
                copy(w, 4 + j, (*chip, 1 - c), me).wait_recv()
        for cp in first + passed:
            cp.wait_send()
        for cp in mine:
            cp.wait()

    return pl.pallas_call(
        body, name="weights_all_gather",
        in_specs=[ANY] * nw, out_specs=[ANY] * nw,
        out_shape=[jax.ShapeDtypeStruct((N_DEV,) + s.shape, s.dtype) for s in shards],
        scratch_shapes=[pltpu.SemaphoreType.DMA((nw, 7)), pltpu.SemaphoreType.DMA((nw, 7)),
                        pltpu.SemaphoreType.DMA((nw,))],
        compiler_params=pltpu.CompilerParams(has_side_effects=True),
    )(*shards)


def _rs_sibling(grads):
    nw = len(grads)

    def body(*refs):
        ins, outs = refs[:nw], refs[nw:2 * nw]
        send_sems, recv_sems = refs[2 * nw:]
        x, y, c = _place()
        sibling = (x, y, 1 - c)
        copies = []
        for w in range(nw):
            for p in range(4):
                cp = pltpu.make_async_remote_copy(
                    src_ref=ins[w].at[2 * p + (1 - c)], dst_ref=outs[w].at[p],
                    send_sem=send_sems.at[w, p], recv_sem=recv_sems.at[w, p], device_id=sibling, device_id_type=MESH)
                cp.start()
                copies.append(cp)
        for cp in copies:
            cp.wait()

    return pl.pallas_call(
        body, name="grads_to_sibling",
        in_specs=[ANY] * nw, out_specs=[ANY] * nw,
        out_shape=[jax.ShapeDtypeStruct((4,) + g.shape[1:], g.dtype) for g in grads],
        scratch_shapes=[pltpu.SemaphoreType.DMA((nw, 4)), pltpu.SemaphoreType.DMA((nw, 4))],
        compiler_params=pltpu.CompilerParams(has_side_effects=True),
    )(*grads)


def _pair_sum(name, g, got, c_arr, tr):
    _, rows, cols = g.shape
    tr = _tile_rows(rows, tr)

    def body(c_ref, a_ref, b_ref, o_ref):
        o_ref[...] = (a_ref[...].astype(F32) + b_ref[...].astype(F32)).astype(BF16)

    return pl.pallas_call(
        body, name=name,
        grid_spec=pltpu.PrefetchScalarGridSpec(
            num_scalar_prefetch=1, grid=(4, rows // tr),
            in_specs=[pl.BlockSpec((None, tr, cols), lambda p, r, cr: (2 * p + cr[0], r, 0)),
                      pl.BlockSpec((None, tr, cols), lambda p, r, cr: (p, r, 0))],
            out_specs=pl.BlockSpec((None, tr, cols), lambda p, r, cr: (p, r, 0))),
        out_shape=jax.ShapeDtypeStruct((4, rows, cols), BF16),
        compiler_params=_params(("parallel", "parallel")),
    )(c_arr, g, got)


def _rs_chips(sums):
    nw = len(sums)

    def body(*refs):
        ins, outs = refs[:nw], refs[nw:2 * nw]
        send_sems, recv_sems = refs[2 * nw:]
        x, y, c = _place()
        copies = []
        for w in range(nw):
            for k, (px, py) in enumerate(_other_chips(x, y)):
                cp = pltpu.make_async_remote_copy(
                    src_ref=ins[w].at[2 * px + py], dst_ref=outs[w].at[k],
                    send_sem=send_sems.at[w, k], recv_sem=recv_sems.at[w, k], device_id=(px, py, c),
                    device_id_type=MESH)
                cp.start()
                copies.append(cp)
        for cp in copies:
            cp.wait()

    return pl.pallas_call(
        body, name="grads_to_chips",
        in_specs=[ANY] * nw, out_specs=[ANY] * nw,
        out_shape=[jax.ShapeDtypeStruct((3,) + g.shape[1:], g.dtype) for g in sums],
        scratch_shapes=[pltpu.SemaphoreType.DMA((nw, 3)), pltpu.SemaphoreType.DMA((nw, 3))],
        compiler_params=pltpu.CompilerParams(has_side_effects=True),
    )(*sums)


def _tile_rows(rows, pref):
    t = min(rows, pref)
    while rows % t or t % 8:
        t -= 1
    return t


def _adam(w, g, m, v):
    m = ADAM_B1 * m + (1.0 - ADAM_B1) * g
    v = ADAM_B2 * v + (1.0 - ADAM_B2) * (g * g)
    m_hat = m / (1.0 - ADAM_B1 ** ADAM_STEP)
    v_hat = v / (1.0 - ADAM_B2 ** ADAM_STEP)
    return -ADAM_LR * (m_hat / (jnp.sqrt(v_hat) + ADAM_EPS) + ADAM_WD * w), m, v


def _adamw_shard(name, w, m, v, sums, got, chip_arr, tr):
    rows, cols = w.shape
    tr = _tile_rows(rows, tr)

    def body(p_ref, w_ref, m_ref, v_ref, s_ref, r_ref, g_ref, d_ref, nm_ref, nv_ref):
        g = s_ref[...].astype(F32)
        for k in range(3):
            g = g + r_ref[k].astype(F32)
        g_ref[...] = g
        d_ref[...], nm_ref[...], nv_ref[...] = _adam(w_ref[...], g, m_ref[...], v_ref[...])

    tile = pl.BlockSpec((tr, cols), lambda r, pr: (r, 0))
    return pl.pallas_call(
        body, name=name,
        grid_spec=pltpu.PrefetchScalarGridSpec(
            num_scalar_prefetch=1, grid=(rows // tr,),
            in_specs=[tile, tile, tile,
                      pl.BlockSpec((None, tr, cols), lambda r, pr: (pr[0], r, 0)),
                      pl.BlockSpec((3, tr, cols), lambda r, pr: (0, r, 0))],
            out_specs=[tile] * 4),
        out_shape=[jax.ShapeDtypeStruct((rows, cols), F32)] * 4,
        compiler_params=_params(("parallel",)),
    )(chip_arr, w, m, v, sums, got)


def _small_all_reduce_adam(part, w, m, v):
    rows = part.shape[0]

    def body(p_ref, w_ref, m_ref, v_ref, g_ref, d_ref, nm_ref, nv_ref, buf, send_sems, recv_sems):
        x, y, c = _place()
        me = 4 * x + 2 * y + c
        buf[me] = p_ref[...]
        peers = [(x, y, 1 - c)] + [(px, py, pc) for px, py in _other_chips(x, y) for pc in (c, 1 - c)]
        copies = []
        for k, peer in enumerate(peers):
            cp = pltpu.make_async_remote_copy(
                src_ref=buf.at[me], dst_ref=buf.at[me], send_sem=send_sems.at[k], recv_sem=recv_sems.at[k],
                device_id=peer, device_id_type=MESH)
            cp.start()
            copies.append(cp)
        for cp in copies:
            cp.wait()
        g = buf[0]
        for k in range(1, N_DEV):
            g = g + buf[k]
        g_ref[...] = g
        d_ref[...], nm_ref[...], nv_ref[...] = _adam(w_ref[...], g, m_ref[...], v_ref[...])

    vm = pl.BlockSpec(memory_space=pltpu.VMEM)
    return pl.pallas_call(
        body, name="gains_all_reduce_adamw",
        in_specs=[vm] * 4, out_specs=[vm] * 4,
        out_shape=[jax.ShapeDtypeStruct((rows, LANES), F32)] * 4,
        scratch_shapes=[pltpu.VMEM((N_DEV, rows, LANES), F32), pltpu.SemaphoreType.DMA((7,)),
                        pltpu.SemaphoreType.DMA((7,))],
        compiler_params=pltpu.CompilerParams(has_side_effects=True),
    )(part, w, m, v)


IN_ORDER = ("r_q", "r_k", "r_v", "r_g", "c_q", "c_kv", "k_pe", "g_ret", "g_mla")
MY_ORDER = ("g_ret", "g_mla", "r_v", "r_g", "r_q", "r_k", "c_q", "c_kv", "k_pe")


def _make_layout(d, vw, qw, ql, kl, mla_w):
    width = {"r_q": qw, "r_k": qw, "r_v": vw, "r_g": vw, "c_q": ql, "c_kv": kl, "k_pe": ROPE, "g_ret": d, "g_mla": d}
    src, o = {}, 0
    for n in IN_ORDER:
        src[n] = o
        o += width[n]
    off, o = {}, 0
    for n in MY_ORDER:
        off[n] = o
        o += LANES if n == "k_pe" else width[n]
    total = -(-o // 256) * 256
    for n, blk in (("g_ret", d), ("g_mla", d), ("r_v", vw), ("r_g", vw), ("r_q", RET_QK), ("r_k", RET_QK),
                   ("r_v", RET_V), ("c_q", ql), ("c_kv", kl), ("k_pe", LANES)):
        assert off[n] % blk == 0
    return {"width": width, "src": src, "off": off, "total": total, "n_in": sum(width.values()),
            "ret_heads": vw // RET_V, "mla_heads": mla_w // VHEAD, "q_lora": ql, "kv_lora": kl}


def _cols_to_full(g):
    n, r, c = g.shape
    return jnp.transpose(g, (1, 0, 2)).reshape(r, n * c)


def _full_to_cols(w):
    r, c = w.shape
    return jnp.transpose(w.reshape(r, N_DEV, c // N_DEV), (1, 0, 2))


def _w_in_to_mine(w, lay):
    parts = []
    for n in MY_ORDER:
        seg = w[:, lay["src"][n]:lay["src"][n] + lay["width"][n]]
        if n == "k_pe":
            seg = jnp.pad(seg, ((0, 0), (0, LANES - ROPE)))
        parts.append(seg)
    cur = sum(p.shape[1] for p in parts)
    parts.append(jnp.zeros((w.shape[0], lay["total"] - cur), w.dtype))
    return jnp.concatenate(parts, axis=1)


def _mine_to_w_in(w, lay):
    return jnp.concatenate([w[:, lay["off"][n]:lay["off"][n] + lay["width"][n]] for n in IN_ORDER], axis=1)


def _rope_tables(positions, half):
    inv = ROPE_THETA ** (-jnp.arange(half, dtype=F32) / half)
    ang = positions.astype(F32)[:, None] * inv
    return jnp.cos(ang), jnp.sin(ang)


def _pack_rows(vs):
    return jnp.concatenate([v.reshape(-1, LANES) for v in vs], axis=0)


def kernel(x, positions, norm_mix_g, w_in, ret_norm_g, w_ret_o, q_a_norm_g, w_q_b, kv_a_norm_g, w_kv_b, w_mla_o, w_out, norm_mlp_g, w_up, w_down, norm_f_g, loss_target, m_norm_mix_g, m_w_in, m_ret_norm_g, m_w_ret_o, m_q_a_norm_g, m_w_q_b, m_kv_a_norm_g, m_w_kv_b, m_w_mla_o, m_w_out, m_norm_mlp_g, m_w_up, m_w_down, m_norm_f_g, v_norm_mix_g, v_w_in, v_ret_norm_g, v_w_ret_o, v_q_a_norm_g, v_w_q_b, v_kv_a_norm_g, v_w_kv_b, v_w_mla_o, v_w_out, v_norm_mlp_g, v_w_up, v_w_down, v_norm_f_g):
    xs, tgt, pos = x[0], loss_target[0], positions[0]
    s, d = xs.shape
    mats = {"w_in": w_in[0], "w_ret_o": w_ret_o[0], "w_q_b": w_q_b[0], "w_kv_b": w_kv_b[0], "w_mla_o": w_mla_o[0],
            "w_out": w_out[0], "w_up": w_up[0], "w_down": w_down[0]}
    mat_m = {"w_in": m_w_in[0], "w_ret_o": m_w_ret_o[0], "w_q_b": m_w_q_b[0], "w_kv_b": m_w_kv_b[0],
             "w_mla_o": m_w_mla_o[0], "w_out": m_w_out[0], "w_up": m_w_up[0], "w_down": m_w_down[0]}
    mat_v = {"w_in": v_w_in[0], "w_ret_o": v_w_ret_o[0], "w_q_b": v_w_q_b[0], "w_kv_b": v_w_kv_b[0],
             "w_mla_o": v_w_mla_o[0], "w_out": v_w_out[0], "w_up": v_w_up[0], "w_down": v_w_down[0]}
    names = list(mats)
    col_sharded = ("w_in", "w_q_b", "w_kv_b", "w_up")
    vw = ret_norm_g.shape[1]
    mla_w = mats["w_mla_o"].shape[0] * N_DEV
    ql, kl = q_a_norm_g.shape[1], kv_a_norm_g.shape[1]
    n_in = mats["w_in"].shape[1] * N_DEV
    qw = (n_in - 2 * vw - ql - kl - ROPE - 2 * d) // 2
    lay = _make_layout(d, vw, qw, ql, kl, mla_w)
    assert lay["n_in"] == n_in
    heads_r, heads_m = lay["ret_heads"], lay["mla_heads"]

    gathered = dict(zip(names, _all_gather([mats[n].astype(BF16) for n in names])))
    full = {n: (_cols_to_full(g) if n in col_sharded else g.reshape(-1, g.shape[2])) for n, g in gathered.items()}
    w_mine = _w_in_to_mine(full["w_in"], lay)
    wq_pad = jnp.pad(full["w_q_b"].reshape(ql, heads_m, NOPE + ROPE),
                     ((0, 0), (0, 0), (0, QPAD - NOPE - ROPE))).reshape(ql, heads_m * QPAD)

    c64, s64 = _rope_tables(pos, RET_QK // 2)
    cos_r = jnp.concatenate([c64, c64], axis=1)
    sin_r = jnp.concatenate([-s64, s64], axis=1)
    c32, s32 = _rope_tables(pos, ROPE // 2)
    z32, z64 = jnp.zeros_like(c32), jnp.zeros((s, LANES - ROPE), F32)
    cos_p = jnp.concatenate([c32, c32, z64], axis=1)
    sin_a = jnp.concatenate([-s32, z32, z64], axis=1)
    sin_b = jnp.concatenate([z32, s32, z64], axis=1)
    lg = jnp.log(1.0 - 2.0 ** (-5.0 - jnp.arange(heads_r, dtype=F32)))
    lgs = jnp.broadcast_to(lg[:, None, None], (heads_r, 8, LANES))

    tm = min(256, s)
    blk = min(256, s)
    t_att = min(512, s)

    u = _rms_fwd("norm_mix", xs, norm_mix_g, tm)
    proj = _mm("in_proj", u, w_mine, "nn", F32)
    o_ret, states = _ret_fwd(proj, lay, cos_r, sin_r, lgs, blk)
    ry = _ret_post(proj, lay, o_ret, ret_norm_g, tm)
    y_ret = _mm("ret_out", ry, full["w_ret_o"], "nn", F32)
    cqn, ckvn, kpr = _mla_prep(proj, lay, q_a_norm_g, kv_a_norm_g, cos_p, sin_a, sin_b, tm)
    qp = _mm("q_up", cqn, wq_pad, "nn", F32)
    kv = _mm("kv_up", ckvn, full["w_kv_b"], "nn", BF16)
    qf, kf = _attn_prep(qp, kv, kpr, lay, cos_p, sin_a, sin_b, tm)
    o_mla, lse = _attn_fwd(qf, kf, kv, lay, t_att)
    y_mla = _mm("mla_out", o_mla, full["w_mla_o"], "nn", F32)
    merged = _merge(proj, lay, y_ret, y_mla, tm)
    mix = _mm("out_proj", merged, full["w_out"], "nn", F32)
    h1, n2 = _rms_res_fwd("norm_mlp", xs, mix, norm_mlp_g, tm)
    z, act = _mm("mlp_up", n2, full["w_up"], "nn", (F32, BF16),
                 epilogue=lambda r: (r, jnp.square(jnp.maximum(r, 0.0))))
    dn = _mm("mlp_down", act, full["w_down"], "nn", F32)
    dh2, g_norm_f, loss_part = _final("loss_head", h1, dn, norm_f_g.reshape(1, d), tgt, tm)

    dz = _mm("mlp_down_dx", dh2, full["w_down"], "nt", BF16, extras=(z,),
             epilogue=lambda r, zz: (r * (2.0 * jnp.maximum(zz, 0.0)),))
    g_w_down = _mm("mlp_down_dw", act, dh2, "tn", BF16)
    g_w_up = _mm("mlp_up_dw", n2, dz, "tn", BF16)
    dn2 = _mm("mlp_up_dx", dz, full["w_up"], "nt", F32)
    dh1, g_norm_mlp = _rms_bwd("norm_mlp_bwd", dn2, h1, norm_mlp_g, dh2, tm)
    d_merged = _mm("out_proj_dx", dh1, full["w_out"], "nt", F32)
    g_w_out = _mm("out_proj_dw", merged, dh1, "tn", BF16)
    dy_ret, dy_mla, dg_ret, dg_mla = _merge_bwd(proj, lay, d_merged, y_ret, y_mla, tm)
    d_ry = _mm("ret_out_dx", dy_ret, full["w_ret_o"], "nt", F32)
    g_w_ret_o = _mm("ret_out_dw", ry, dy_ret, "tn", BF16)
    d_omla = _mm("mla_out_dx", dy_mla, full["w_mla_o"], "nt", F32)
    g_w_mla_o = _mm("mla_out_dw", o_mla, dy_mla, "tn", BF16)
    d_oret, d_rg, g_ret_norm = _ret_post_bwd(proj, lay, d_ry, o_ret, ret_norm_g, tm)
    d_rq, d_rk, d_rv = _ret_bwd(proj, lay, cos_r, sin_r, lgs, states, d_oret, blk)
    dqf, delta = _attn_bwd_dq(qf, kf, kv, o_mla, lse, d_omla, lay, t_att)
    dkf, dv = _attn_bwd_dkv(qf, kf, kv, lse, delta, d_omla, lay, t_att)
    dqp, dkv, d_kpe = _attn_post_bwd(dqf, dkf, dv, lay, cos_p, sin_a, sin_b, tm)
    d_cqn = _mm("q_up_dx", dqp, wq_pad, "nt", F32)
    g_wq_pad = _mm("q_up_dw", cqn, dqp, "tn", BF16)
    d_ckvn = _mm("kv_up_dx", dkv, full["w_kv_b"], "nt", F32)
    g_w_kv_b = _mm("kv_up_dw", ckvn, dkv, "tn", BF16)
    d_cq, d_ckv, g_q_a, g_kv_a = _mla_prep_bwd(proj, lay, d_cqn, d_ckvn, q_a_norm_g, kv_a_norm_g, tm)
    d_by = {"g_ret": dg_ret, "g_mla": dg_mla, "r_v": d_rv, "r_g": d_rg, "r_q": d_rq, "r_k": d_rk, "c_q": d_cq,
            "c_kv": d_ckv, "k_pe": d_kpe}
    parts = [d_by[n] for n in MY_ORDER]
    parts.append(jnp.zeros((s, lay["total"] - sum(p.shape[1] for p in parts)), BF16))
    d_proj = jnp.concatenate(parts, axis=1)
    du = _mm("in_proj_dx", d_proj, w_mine, "nt", F32)
    g_w_mine = _mm("in_proj_dw", u, d_proj, "tn", BF16)
    grad_x, g_norm_mix = _rms_bwd("norm_mix_bwd", du, xs, norm_mix_g, dh1, tm)

    g_w_q_b = g_wq_pad.reshape(ql, heads_m, QPAD)[:, :, :NOPE + ROPE].reshape(ql, heads_m * (NOPE + ROPE))
    g_full = {"w_in": _mine_to_w_in(g_w_mine, lay), "w_ret_o": g_w_ret_o, "w_q_b": g_w_q_b, "w_kv_b": g_w_kv_b,
              "w_mla_o": g_w_mla_o, "w_out": g_w_out, "w_up": g_w_up, "w_down": g_w_down}
    g_blocks = [(_full_to_cols(g_full[n]) if n in col_sharded else g_full[n].reshape((N_DEV,) + mats[n].shape))
                for n in names]
    mx, my, mc = _place()
    c_arr = jnp.reshape(mc, (1,)).astype(jnp.int32)
    chip_arr = jnp.reshape(2 * mx + my, (1,)).astype(jnp.int32)
    from_sibling = _rs_sibling(g_blocks)
    sums = [_pair_sum("pair_sum_" + n, g, r, c_arr, 256) for n, g, r in zip(names, g_blocks, from_sibling)]
    from_chips = _rs_chips(sums)
    upd = {n: _adamw_shard("adamw_" + n, mats[n], mat_m[n], mat_v[n], sm, r, chip_arr, 256)
           for n, sm, r in zip(names, sums, from_chips)}

    gains = [("norm_mix_g", norm_mix_g, m_norm_mix_g, v_norm_mix_g, g_norm_mix),
             ("ret_norm_g", ret_norm_g, m_ret_norm_g, v_ret_norm_g, g_ret_norm),
             ("q_a_norm_g", q_a_norm_g, m_q_a_norm_g, v_q_a_norm_g, g_q_a),
             ("kv_a_norm_g", kv_a_norm_g, m_kv_a_norm_g, v_kv_a_norm_g, g_kv_a),
             ("norm_mlp_g", norm_mlp_g, m_norm_mlp_g, v_norm_mlp_g, g_norm_mlp),
             ("norm_f_g", norm_f_g, m_norm_f_g, v_norm_f_g, g_norm_f)]
    n_rows = sum(g[1].size for g in gains) // LANES
    pad_rows = -(-(n_rows + 1) // 8) * 8 - n_rows
    tail = jnp.zeros((pad_rows, LANES), F32)
    part = jnp.concatenate([_pack_rows([g[4] for g in gains]),
                            jnp.broadcast_to(loss_part[:, :1], (1, LANES)), tail[1:]], axis=0)
    packed = [jnp.concatenate([_pack_rows([g[k] for g in gains]), tail], axis=0) for k in (1, 2, 3)]
    g_sm, d_sm, m_sm, v_sm = _small_all_reduce_adam(part, *packed)
    loss = g_sm[n_rows, 0]
    small = {}
    o = 0
    for name, w, _, _, _ in gains:
        r = w.size // LANES
        small[name] = [a[o:o + r].reshape(w.shape) for a in (g_sm, d_sm, m_sm, v_sm)]
        o += r

    order = ["norm_mix_g", "w_in", "ret_norm_g", "w_ret_o", "q_a_norm_g", "w_q_b", "kv_a_norm_g", "w_kv_b", "w_mla_o",
             "w_out", "norm_mlp_g", "w_up", "w_down", "norm_f_g"]
    outs = [loss, grad_x[None]]
    for k in range(4):
        for n in order:
            outs.append(small[n][k] if n in small else upd[n][k][None])
    return tuple(outs)
```

```python
import functools
import math

import jax
import jax.numpy as jnp
from jax import lax
from jax.experimental import pallas as pl
from jax.experimental.pallas import tpu as pltpu

F32 = jnp.float32
BF16 = jnp.bfloat16
MESH = pl.DeviceIdType.MESH

EPS = 1e-6
ROPE_THETA = 10000.0
CHUNK_SHIFT = 6
RET_QK = 128
RET_V = 256
NOPE = 128
ROPE = 64
VHEAD = 128
QPAD = 256
LANES = 128
N_DEV = 8
VMEM_LIMIT = 56 * 1024 * 1024

ADAM_LR = 0.001
ADAM_B1 = 0.9
ADAM_B2 = 0.999
ADAM_EPS = 1e-08
ADAM_WD = 0.01
ADAM_STEP = 10

NN = (((1,), (0,)), ((), ()))
NT = (((1,), (1,)), ((), ()))
TN = (((0,), (0,)), ((), ()))


def _dot(a, b, dims):
    return lax.dot_general(a.astype(BF16), b.astype(BF16), dims, preferred_element_type=F32)


def _tile(dim, pref):
    if dim <= pref:
        return dim
    t = (pref // LANES) * LANES
    while t >= LANES:
        if dim % t == 0:
            return t
        t -= LANES
    raise ValueError(f"no tile for {dim}")


def _params(sem):
    return pltpu.CompilerParams(dimension_semantics=sem, vmem_limit_bytes=VMEM_LIMIT)


def _sig(v):
    return 1.0 / (1.0 + jnp.exp(-v))


def _mm(name, a, b, mode, out_dtypes, *, tm=1024, tn=1024, tk=2048, extras=(), epilogue=None):
    if mode == "nn":
        (m, k), (_, n) = a.shape, b.shape
    elif mode == "nt":
        (m, k), (n, _) = a.shape, b.shape
    else:
        (k, m), (_, n) = a.shape, b.shape
    tm, tn, tk = _tile(m, tm), _tile(n, tn), _tile(k, tk)
    nk = k // tk
    dims = {"nn": NN, "nt": NT, "tn": TN}[mode]
    a_spec = (pl.BlockSpec((tk, tm), lambda i, j, kk: (kk, i)) if mode == "tn"
              else pl.BlockSpec((tm, tk), lambda i, j, kk: (i, kk)))
    b_spec = (pl.BlockSpec((tn, tk), lambda i, j, kk: (j, kk)) if mode == "nt"
              else pl.BlockSpec((tk, tn), lambda i, j, kk: (kk, j)))
    tile_spec = pl.BlockSpec((tm, tn), lambda i, j, kk: (i, j))
    n_ex = len(extras)
    single = not isinstance(out_dtypes, (tuple, list))
    dts = (out_dtypes,) if single else tuple(out_dtypes)

    def body(a_ref, b_ref, *rest):
        ex, outs = rest[:n_ex], rest[n_ex:n_ex + len(dts)]

        def finish(r):
            vals = (r,) if epilogue is None else epilogue(r, *[e[...] for e in ex])
            for o, v in zip(outs, vals):
                o[...] = v.astype(o.dtype)

        part = _dot(a_ref[...], b_ref[...], dims)
        if nk == 1:
            finish(part)
            return
        acc = rest[-1]
        kk = pl.program_id(2)

        @pl.when(kk == 0)
        def _():
            acc[...] = part

        @pl.when(jnp.logical_and(kk > 0, kk < nk - 1))
        def _():
            acc[...] += part

        @pl.when(kk == nk - 1)
        def _():
            finish(acc[...] + part)

    res = pl.pallas_call(
        body, name=name, grid=(m // tm, n // tn, nk),
        in_specs=[a_spec, b_spec] + [tile_spec] * n_ex,
        out_specs=[tile_spec] * len(dts),
        out_shape=[jax.ShapeDtypeStruct((m, n), d) for d in dts],
        scratch_shapes=[pltpu.VMEM((tm, tn), F32)] if nk > 1 else [],
        compiler_params=_params(("parallel", "parallel", "arbitrary")),
    )(a, b, *extras)
    return res[0] if single else res


def _rows(name, body, n_rows, tm, ins, outs, accs=()):
    in_specs, args = [], []
    for t in ins:
        if len(t) == 1:
            in_specs.append(pl.BlockSpec(t[0].shape, lambda i, nd=t[0].ndim: (0,) * nd))
        else:
            in_specs.append(pl.BlockSpec((tm, t[1]), lambda i, cb=t[2]: (i, cb)))
        args.append(t[0])
    out_specs = [pl.BlockSpec((tm, w), lambda i: (i, 0)) for w, _ in outs]
    out_specs += [pl.BlockSpec((r, w), lambda i: (0, 0)) for r, w in accs]
    out_shape = [jax.ShapeDtypeStruct((n_rows, w), d) for w, d in outs]
    out_shape += [jax.ShapeDtypeStruct((r, w), F32) for r, w in accs]
    return pl.pallas_call(
        body, name=name, grid=(n_rows // tm,), in_specs=in_specs, out_specs=out_specs, out_shape=out_shape,
        compiler_params=_params(("arbitrary",) if accs else ("parallel",)),
    )(*args)


def _zero_first(*accs):
    @pl.when(pl.program_id(0) == 0)
    def _():
        for a in accs:
            a[...] = jnp.zeros_like(a)


def _rope64(t, cos, sin):
    return t * cos + pltpu.roll(t, RET_QK // 2, 1) * sin


def _rope32(t, cos, sin_a, sin_b):
    return t * cos + pltpu.roll(t, LANES - ROPE // 2, 1) * sin_a + pltpu.roll(t, ROPE // 2, 1) * sin_b


def _rms_fwd(name, x, g, tm):
    s, d = x.shape

    def body(x_ref, g_ref, u_ref):
        v = x_ref[...]
        r = lax.rsqrt(jnp.mean(v * v, axis=-1, keepdims=True) + EPS)
        u_ref[...] = (v * r * g_ref[...]).astype(BF16)

    return _rows(name, body, s, tm, [(x, d, 0), (g,)], [(d, BF16)])[0]


def _rms_res_fwd(name, x, mix, g, tm):
    s, d = x.shape

    def body(x_ref, m_ref, g_ref, h_ref, u_ref):
        v = x_ref[...] + m_ref[...]
        h_ref[...] = v
        r = lax.rsqrt(jnp.mean(v * v, axis=-1, keepdims=True) + EPS)
        u_ref[...] = (v * r * g_ref[...]).astype(BF16)

    return _rows(name, body, s, tm, [(x, d, 0), (mix, d, 0), (g,)], [(d, F32), (d, BF16)])


def _rms_bwd(name, dy, x, g, dres, tm):
    s, d = x.shape

    def body(dy_ref, x_ref, g_ref, dres_ref, dx_ref, dg_ref):
        _zero_first(dg_ref)
        v, dyv = x_ref[...], dy_ref[...]
        r = lax.rsqrt(jnp.mean(v * v, axis=-1, keepdims=True) + EPS)
        xh = v * r
        dxh = dyv * g_ref[...]
        dx_ref[...] = dres_ref[...] + r * (dxh - xh * jnp.mean(dxh * xh, axis=-1, keepdims=True))
        dg_ref[...] += jnp.sum(dyv * xh, axis=0, keepdims=True)

    return _rows(name, body, s, tm, [(dy, d, 0), (x, d, 0), (g,), (dres, d, 0)], [(d, F32)], [(1, d)])


def _final(name, h1, dn, g, tgt, tm):
    s, d = h1.shape

    def body(h_ref, dn_ref, g_ref, t_ref, dh_ref, dg_ref, loss_ref):
        _zero_first(dg_ref, loss_ref)
        v = h_ref[...] + dn_ref[...]
        r = lax.rsqrt(jnp.mean(v * v, axis=-1, keepdims=True) + EPS)
        xh = v * r
        gv = g_ref[...]
        e = xh * gv - t_ref[...]
        loss_ref[...] += 0.5 * jnp.sum(jnp.mean(e * e, axis=-1, keepdims=True))
        dy = e * (1.0 / d)
        dg_ref[...] += jnp.sum(dy * xh, axis=0, keepdims=True)
        dxh = dy * gv
        dh_ref[...] = r * (dxh - xh * jnp.mean(dxh * xh, axis=-1, keepdims=True))

    return _rows(name, body, s, tm, [(h1, d, 0), (dn, d, 0), (g,), (tgt, d, 0)], [(d, F32)], [(1, d), (1, LANES)])


def _decay_mask(lg, blk):
    n = lax.broadcasted_iota(jnp.int32, (blk, blk), 0)
    m = lax.broadcasted_iota(jnp.int32, (blk, blk), 1)
    w = jnp.exp(lg * jnp.abs(n - m).astype(F32))
    return jnp.where(jnp.right_shift(m, CHUNK_SHIFT) <= jnp.right_shift(n, CHUNK_SHIFT), w, 0.0)


def _decays(lg, blk):
    pos = lax.broadcasted_iota(jnp.int32, (blk, 1), 0).astype(F32)
    return jnp.exp(lg * (pos + 1.0)), jnp.exp(lg * (blk - 1.0 - pos)), jnp.exp(lg * float(blk))


def _ret_fwd(proj, lay, cos, sin, lgs, blk):
    s = proj.shape[0]
    heads = lay["ret_heads"]
    nb = s // blk
    scale = RET_QK ** -0.5

    def body(lg_ref, q_ref, k_ref, v_ref, cos_ref, sin_ref, o_ref, st_ref, state, mask):
        lg = lg_ref[0:1, 0:1]

        @pl.when(pl.program_id(1) == 0)
        def _():
            state[...] = jnp.zeros_like(state)
            mask[...] = _decay_mask(lg, blk)

        a, c, gb = _decays(lg, blk)
        q = _rope64(q_ref[...], cos_ref[...], sin_ref[...])
        k = _rope64(k_ref[...], cos_ref[...], sin_ref[...]) * scale
        v = v_ref[...]
        st = state[...]
        st_ref[...] = st
        sm = _dot(q, k, NT) * mask[...]
        o_ref[...] = _dot(sm, v, NN) + _dot(q * a, st, NN)
        state[...] = st * gb + _dot(k * c, v, TN)

    qb, kb, vb = lay["off"]["r_q"] // RET_QK, lay["off"]["r_k"] // RET_QK, lay["off"]["r_v"] // RET_V
    return pl.pallas_call(
        body, name="ret_fwd", grid=(heads, nb),
        in_specs=[pl.BlockSpec((None, 8, LANES), lambda h, b: (h, 0, 0)),
                  pl.BlockSpec((blk, RET_QK), lambda h, b: (b, qb + h)),
                  pl.BlockSpec((blk, RET_QK), lambda h, b: (b, kb + h)),
                  pl.BlockSpec((blk, RET_V), lambda h, b: (b, vb + h)),
                  pl.BlockSpec((blk, LANES), lambda h, b: (b, 0)),
                  pl.BlockSpec((blk, LANES), lambda h, b: (b, 0))],
        out_specs=[pl.BlockSpec((blk, RET_V), lambda h, b: (b, h)),
                   pl.BlockSpec((None, None, RET_QK, RET_V), lambda h, b: (h, b, 0, 0))],
        out_shape=[jax.ShapeDtypeStruct((s, heads * RET_V), F32),
                   jax.ShapeDtypeStruct((heads, nb, RET_QK, RET_V), F32)],
        scratch_shapes=[pltpu.VMEM((RET_QK, RET_V), F32), pltpu.VMEM((blk, blk), F32)],
        compiler_params=_params(("parallel", "arbitrary")),
    )(lgs, proj, proj, proj, cos, sin)


def _ret_bwd(proj, lay, cos, sin, lgs, states, d_o, blk):
    s = proj.shape[0]
    heads = lay["ret_heads"]
    nb = s // blk
    scale = RET_QK ** -0.5

    def body(lg_ref, q_ref, k_ref, v_ref, cos_ref, sin_ref, st_ref, do_ref, dq_ref, dk_ref, dv_ref, dstate, mask):
        lg = lg_ref[0:1, 0:1]

        @pl.when(pl.program_id(1) == 0)
        def _():
            dstate[...] = jnp.zeros_like(dstate)
            mask[...] = _decay_mask(lg, blk)

        a, c, gb = _decays(lg, blk)
        cs, sn = cos_ref[...], sin_ref[...]
        q = _rope64(q_ref[...], cs, sn)
        k = _rope64(k_ref[...], cs, sn) * scale
        v = v_ref[...]
        st = st_ref[...]
        do = do_ref[...]
        dst = dstate[...]
        mk = mask[...]
        sm = _dot(q, k, NT) * mk
        ds = _dot(do, v, NT) * mk
        dq = _dot(ds, k, NN) + _dot(do, st, NT) * a
        dk = _dot(ds, q, TN) + _dot(v, dst, NT) * c
        dv_ref[...] = (_dot(sm, do, TN) + _dot(k * c, dst, NN)).astype(dv_ref.dtype)
        dstate[...] = dst * gb + _dot(q * a, do, TN)
        dq_ref[...] = _rope64(dq, cs, -sn).astype(dq_ref.dtype)
        dk_ref[...] = (_rope64(dk, cs, -sn) * scale).astype(dk_ref.dtype)

    qb, kb, vb = lay["off"]["r_q"] // RET_QK, lay["off"]["r_k"] // RET_QK, lay["off"]["r_v"] // RET_V
    last = nb - 1
    return pl.pallas_call(
        body, name="ret_bwd", grid=(heads, nb),
        in_specs=[pl.BlockSpec((None, 8, LANES), lambda h, b: (h, 0, 0)),
                  pl.BlockSpec((blk, RET_QK), lambda h, b: (last - b, qb + h)),
                  pl.BlockSpec((blk, RET_QK), lambda h, b: (last - b, kb + h)),
                  pl.BlockSpec((blk, RET_V), lambda h, b: (last - b, vb + h)),
                  pl.BlockSpec((blk, LANES), lambda h, b: (last - b, 0)),
                  pl.BlockSpec((blk, LANES), lambda h, b: (last - b, 0)),
                  pl.BlockSpec((None, None, RET_QK, RET_V), lambda h, b: (h, last - b, 0, 0)),
                  pl.BlockSpec((blk, RET_V), lambda h, b: (last - b, h))],
        out_specs=[pl.BlockSpec((blk, RET_QK), lambda h, b: (last - b, h)),
                   pl.BlockSpec((blk, RET_QK), lambda h, b: (last - b, h)),
                   pl.BlockSpec((blk, RET_V), lambda h, b: (last - b, h))],
        out_shape=[jax.ShapeDtypeStruct((s, heads * RET_QK), BF16),
                   jax.ShapeDtypeStruct((s, heads * RET_QK), BF16),
                   jax.ShapeDtypeStruct((s, heads * RET_V), BF16)],
        scratch_shapes=[pltpu.VMEM((RET_QK, RET_V), F32), pltpu.VMEM((blk, blk), F32)],
        compiler_params=_params(("parallel", "arbitrary")),
    )(lgs, proj, proj, proj, cos, sin, states, d_o)


def _ret_post(proj, lay, o, g, tm):
    s, vw = o.shape
    heads = lay["ret_heads"]

    def body(o_ref, rg_ref, g_ref, ry_ref):
        for h in range(heads):
            sl = slice(h * RET_V, (h + 1) * RET_V)
            oh = o_ref[:, sl]
            dlt = oh - jnp.mean(oh, axis=-1, keepdims=True)
            rstd = lax.rsqrt(jnp.mean(dlt * dlt, axis=-1, keepdims=True) + EPS)
            rg = rg_ref[:, sl]
            ry_ref[:, sl] = (dlt * rstd * g_ref[:, sl] * (rg * _sig(rg))).astype(BF16)

    return _rows("ret_post", body, s, tm, [(o, vw, 0), (proj, vw, lay["off"]["r_g"] // vw), (g,)], [(vw, BF16)])[0]


def _ret_post_bwd(proj, lay, d_ry, o, g, tm):
    s, vw = o.shape
    heads = lay["ret_heads"]

    def body(dry_ref, o_ref, rg_ref, g_ref, do_ref, drg_ref, dg_ref):
        _zero_first(dg_ref)
        for h in range(heads):
            sl = slice(h * RET_V, (h + 1) * RET_V)
            oh = o_ref[:, sl]
            dlt = oh - jnp.mean(oh, axis=-1, keepdims=True)
            rstd = lax.rsqrt(jnp.mean(dlt * dlt, axis=-1, keepdims=True) + EPS)
            oh = dlt * rstd
            gv = g_ref[:, sl]
            rg = rg_ref[:, sl]
            sg = _sig(rg)
            dry = dry_ref[:, sl]
            dt = dry * (rg * sg)
            drg_ref[:, sl] = (dry * (oh * gv) * (sg * (1.0 + rg * (1.0 - sg)))).astype(BF16)
            dg_ref[:, sl] += jnp.sum(dt * oh, axis=0, keepdims=True)
            doh = dt * gv
            do_ref[:, sl] = rstd * (doh - jnp.mean(doh, axis=-1, keepdims=True)
                                    - oh * jnp.mean(doh * oh, axis=-1, keepdims=True))

    return _rows("ret_post_bwd", body, s, tm,
                 [(d_ry, vw, 0), (o, vw, 0), (proj, vw, lay["off"]["r_g"] // vw), (g,)],
                 [(vw, F32), (vw, BF16)], [(1, vw)])


def _mla_prep(proj, lay, gq, gkv, cos, sin_a, sin_b, tm):
    s = proj.shape[0]
    ql, kl = lay["q_lora"], lay["kv_lora"]

    def body(cq_ref, ckv_ref, kpe_ref, gq_ref, gkv_ref, cos_ref, sa_ref, sb_ref, cqn_ref, ckvn_ref, kpr_ref):
        for src, gref, dst in ((cq_ref, gq_ref, cqn_ref), (ckv_ref, gkv_ref, ckvn_ref)):
            v = src[...]
            r = lax.rsqrt(jnp.mean(v * v, axis=-1, keepdims=True) + EPS)
            dst[...] = (v * r * gref[...]).astype(BF16)
        kpr_ref[...] = _rope32(kpe_ref[...], cos_ref[...], sa_ref[...], sb_ref[...]).astype(BF16)

    off = lay["off"]
    return _rows("mla_prep", body, s, tm,
                 [(proj, ql, off["c_q"] // ql), (proj, kl, off["c_kv"] // kl), (proj, LANES, off["k_pe"] // LANES),
                  (gq,), (gkv,), (cos, LANES, 0), (sin_a, LANES, 0), (sin_b, LANES, 0)],
                 [(ql, BF16), (kl, BF16), (LANES, BF16)])


def _mla_prep_bwd(proj, lay, d_cqn, d_ckvn, gq, gkv, tm):
    s = proj.shape[0]
    ql, kl = lay["q_lora"], lay["kv_lora"]

    def body(dq_ref, dkv_ref, cq_ref, ckv_ref, gq_ref, gkv_ref, dcq_ref, dckv_ref, dgq_ref, dgkv_ref):
        _zero_first(dgq_ref, dgkv_ref)
        for dref, src, gref, dst, dg in ((dq_ref, cq_ref, gq_ref, dcq_ref, dgq_ref),
                                         (dkv_ref, ckv_ref, gkv_ref, dckv_ref, dgkv_ref)):
            v, dy = src[...], dref[...]
            r = lax.rsqrt(jnp.mean(v * v, axis=-1, keepdims=True) + EPS)
            xh = v * r
            dxh = dy * gref[...]
            dst[...] = (r * (dxh - xh * jnp.mean(dxh * xh, axis=-1, keepdims=True))).astype(BF16)
            dg[...] += jnp.sum(dy * xh, axis=0, keepdims=True)

    off = lay["off"]
    return _rows("mla_prep_bwd", body, s, tm,
                 [(d_cqn, ql, 0), (d_ckvn, kl, 0), (proj, ql, off["c_q"] // ql), (proj, kl, off["c_kv"] // kl),
                  (gq,), (gkv,)],
                 [(ql, BF16), (kl, BF16)], [(1, ql), (1, kl)])


def _attn_prep(qp, kv, kpr, lay, cos, sin_a, sin_b, tm):
    s = qp.shape[0]
    heads = lay["mla_heads"]
    w = heads * QPAD

    def body(qp_ref, kv_ref, kpr_ref, cos_ref, sa_ref, sb_ref, qf_ref, kf_ref):
        qs = (NOPE + ROPE) ** -0.5 * math.log2(math.e)
        cs, sa, sb = cos_ref[...] * qs, sa_ref[...] * qs, sb_ref[...] * qs
        kp = kpr_ref[...]
        for h in range(heads):
            lo, hi = h * QPAD, h * QPAD + NOPE
            qf_ref[:, lo:hi] = (qp_ref[:, lo:hi] * qs).astype(BF16)
            qf_ref[:, hi:hi + LANES] = _rope32(qp_ref[:, hi:hi + LANES], cs, sa, sb).astype(BF16)
            kf_ref[:, lo:hi] = kv_ref[:, lo:hi]
            kf_ref[:, hi:hi + LANES] = kp

    return _rows("attn_prep", body, s, tm,
                 [(qp, w, 0), (kv, w, 0), (kpr, LANES, 0), (cos, LANES, 0), (sin_a, LANES, 0), (sin_b, LANES, 0)],
                 [(w, BF16), (w, BF16)])


def _attn_post_bwd(dqf, dkf, dv, lay, cos, sin_a, sin_b, tm):
    s = dqf.shape[0]
    heads = lay["mla_heads"]
    w = heads * QPAD

    def body(dqf_ref, dkf_ref, dv_ref, cos_ref, sa_ref, sb_ref, dqp_ref, dkv_ref, dkpe_ref):
        cs, sa, sb = cos_ref[...], -sa_ref[...], -sb_ref[...]
        kpe = jnp.zeros((tm, LANES), F32)
        for h in range(heads):
            lo, hi = h * QPAD, h * QPAD + NOPE
            dqp_ref[:, lo:hi] = dqf_ref[:, lo:hi].astype(BF16)
            dqp_ref[:, hi:hi + LANES] = _rope32(dqf_ref[:, hi:hi + LANES], cs, sa, sb).astype(BF16)
            dkv_ref[:, lo:hi] = dkf_ref[:, lo:hi].astype(BF16)
            dkv_ref[:, hi:hi + LANES] = dv_ref[:, h * VHEAD:(h + 1) * VHEAD].astype(BF16)
            kpe = kpe + dkf_ref[:, hi:hi + LANES]
        dkpe_ref[...] = _rope32(kpe, cs, sa, sb).astype(BF16)

    return _rows("attn_post_bwd", body, s, tm,
                 [(dqf, w, 0), (dkf, w, 0), (dv, heads * VHEAD, 0), (cos, LANES, 0), (sin_a, LANES, 0),
                  (sin_b, LANES, 0)],
                 [(w, BF16), (w, BF16), (LANES, BF16)])


def _diag_mask(t, keys_on_rows=False):
    row = lax.broadcasted_iota(jnp.int32, (t, t), 0)
    col = lax.broadcasted_iota(jnp.int32, (t, t), 1)
    key, query = (row, col) if keys_on_rows else (col, row)
    return jnp.right_shift(key, CHUNK_SHIFT) <= jnp.right_shift(query, CHUNK_SHIFT)


def _tile_pairs(nt, by_key):
    if by_key:
        pairs = [(i, j) for j in range(nt) for i in range(j, nt)]
    else:
        pairs = [(i, j) for i in range(nt) for j in range(i + 1)]
    return (jnp.asarray([p[0] for p in pairs], jnp.int32), jnp.asarray([p[1] for p in pairs], jnp.int32))


def _head_block(heads):
    return 2 if heads % 2 == 0 else 1


def _attn_fwd(qf, kf, kv, lay, t):
    s = qf.shape[0]
    heads = lay["mla_heads"]
    hb = _head_block(heads)
    nt = s // t
    qi, kj = _tile_pairs(nt, False)

    def body(qi_ref, kj_ref, q_ref, k_ref, kv_ref, o_ref, lse_ref, m_s, l_s, acc):
        p = pl.program_id(1)
        i, j = qi_ref[p], kj_ref[p]

        @pl.when(j == 0)
        def _():
            m_s[...] = jnp.full_like(m_s, -jnp.inf)
            l_s[...] = jnp.zeros_like(l_s)
            acc[...] = jnp.zeros_like(acc)

        def step(diagonal):
            ones = jnp.ones((t, LANES), BF16)
            scores = [_dot(q_ref[:, hh * QPAD:(hh + 1) * QPAD], k_ref[:, hh * QPAD:(hh + 1) * QPAD], NT)
                      for hh in range(hb)]
            for hh in range(hb):
                sc = scores[hh]
                if diagonal:
                    sc = jnp.where(_diag_mask(t), sc, -jnp.inf)
                cols = [sc[:, c * LANES:(c + 1) * LANES] for c in range(t // LANES)]
                m_old = m_s[hh]
                m_new = jnp.maximum(m_old, jnp.max(functools.reduce(jnp.maximum, cols), axis=-1, keepdims=True))
                alpha = jnp.exp2(m_old - m_new)
                pr = jnp.concatenate([jnp.exp2(c - m_new).astype(BF16) for c in cols], axis=1)
                pv = _dot(pr, jnp.concatenate([kv_ref[:, hh * QPAD + NOPE:(hh + 1) * QPAD], ones], axis=1), NN)
                l_new = alpha * l_s[hh] + pv[:, VHEAD:]
                a_new = alpha * acc[hh] + pv[:, :VHEAD]
                if diagonal:
                    o_ref[:, hh * VHEAD:(hh + 1) * VHEAD] = a_new / l_new
                    lse_ref[hh] = (m_new + jnp.log2(l_new))[:, :1]
                else:
                    m_s[hh], l_s[hh], acc[hh] = m_new, l_new, a_new

        pl.when(j < i)(functools.partial(step, False))
        pl.when(j == i)(functools.partial(step, True))

    return pl.pallas_call(
        body, name="attn_fwd",
        grid_spec=pltpu.PrefetchScalarGridSpec(
            num_scalar_prefetch=2, grid=(heads // hb, int(qi.shape[0])),
            in_specs=[pl.BlockSpec((t, hb * QPAD), lambda h, p, qi, kj: (qi[p], h)),
                      pl.BlockSpec((t, hb * QPAD), lambda h, p, qi, kj: (kj[p], h)),
                      pl.BlockSpec((t, hb * QPAD), lambda h, p, qi, kj: (kj[p], h))],
            out_specs=[pl.BlockSpec((t, hb * VHEAD), lambda h, p, qi, kj: (qi[p], h)),
                       pl.BlockSpec((hb, t, 1), lambda h, p, qi, kj: (h, qi[p], 0))],
            scratch_shapes=[pltpu.VMEM((hb, t, LANES), F32), pltpu.VMEM((hb, t, LANES), F32),
                            pltpu.VMEM((hb, t, VHEAD), F32)]),
        out_shape=[jax.ShapeDtypeStruct((s, heads * VHEAD), F32),
                   jax.ShapeDtypeStruct((heads, s, 1), F32)],
        compiler_params=_params(("parallel", "arbitrary")),
    )(qi, kj, qf, kf, kv)


def _attn_bwd_dq(qf, kf, kv, o, lse, d_o, lay, t):
    s = qf.shape[0]
    heads = lay["mla_heads"]
    hb = _head_block(heads)
    nt = s // t
    scale = (NOPE + ROPE) ** -0.5
    qi, kj = _tile_pairs(nt, False)

    def body(qi_ref, kj_ref, q_ref, k_ref, kv_ref, o_ref, lse_ref, do_ref, dq_ref, dl_ref, acc):
        p = pl.program_id(1)
        i, j = qi_ref[p], kj_ref[p]

        @pl.when(j == 0)
        def _():
            acc[...] = jnp.zeros_like(acc)
            for hh in range(hb):
                sl = slice(hh * VHEAD, (hh + 1) * VHEAD)
                dl_ref[hh] = jnp.sum(do_ref[:, sl] * o_ref[:, sl], axis=-1, keepdims=True)

        def step(diagonal):
            for hh in range(hb):
                lo = hh * QPAD
                k = k_ref[:, lo:lo + QPAD]
                sc = _dot(q_ref[:, lo:lo + QPAD], k, NT)
                pr = jnp.exp2(sc - lse_ref[hh])
                if diagonal:
                    pr = jnp.where(_diag_mask(t), pr, 0.0)
                dp = _dot(do_ref[:, hh * VHEAD:(hh + 1) * VHEAD], kv_ref[:, lo + NOPE:lo + QPAD], NT)
                a_new = acc[hh] + _dot(pr * (dp - dl_ref[hh]), k, NN)
                if diagonal:
                    dq_ref[:, lo:lo + QPAD] = a_new * scale
                else:
                    acc[hh] = a_new

        pl.when(j < i)(functools.partial(step, False))
        pl.when(j == i)(functools.partial(step, True))

    return pl.pallas_call(
        body, name="attn_bwd_dq",
        grid_spec=pltpu.PrefetchScalarGridSpec(
            num_scalar_prefetch=2, grid=(heads // hb, int(qi.shape[0])),
            in_specs=[pl.BlockSpec((t, hb * QPAD), lambda h, p, qi, kj: (qi[p], h)),
                      pl.BlockSpec((t, hb * QPAD), lambda h, p, qi, kj: (kj[p], h)),
                      pl.BlockSpec((t, hb * QPAD), lambda h, p, qi, kj: (kj[p], h)),
                      pl.BlockSpec((t, hb * VHEAD), lambda h, p, qi, kj: (qi[p], h)),
                      pl.BlockSpec((hb, t, 1), lambda h, p, qi, kj: (h, qi[p], 0)),
                      pl.BlockSpec((t, hb * VHEAD), lambda h, p, qi, kj: (qi[p], h))],
            out_specs=[pl.BlockSpec((t, hb * QPAD), lambda h, p, qi, kj: (qi[p], h)),
                       pl.BlockSpec((hb, t, 1), lambda h, p, qi, kj: (h, qi[p], 0))],
            scratch_shapes=[pltpu.VMEM((hb, t, QPAD), F32)]),
        out_shape=[jax.ShapeDtypeStruct((s, heads * QPAD), F32),
                   jax.ShapeDtypeStruct((heads, s, 1), F32)],
        compiler_params=_params(("parallel", "arbitrary")),
    )(qi, kj, qf, kf, kv, o, lse, d_o)


def _attn_bwd_dkv(qf, kf, kv, lse, delta, d_o, lay, t):
    s = qf.shape[0]
    heads = lay["mla_heads"]
    hb = _head_block(heads)
    nt = s // t
    qi, kj = _tile_pairs(nt, True)

    def body(qi_ref, kj_ref, q_ref, k_ref, kv_ref, lse_ref, dl_ref, do_ref, dk_ref, dv_ref, dk_acc, dv_acc):
        p = pl.program_id(1)
        i, j = qi_ref[p], kj_ref[p]

        def step(diagonal):
            for hh in range(hb):
                lo = hh * QPAD
                q = q_ref[:, lo:lo + QPAD]
                do = do_ref[:, hh * VHEAD:(hh + 1) * VHEAD]
                sc = _dot(k_ref[:, lo:lo + QPAD], q, NT)
                pr = jnp.exp2(sc - lse_ref[hh])
                if diagonal:
                    pr = jnp.where(_diag_mask(t, keys_on_rows=True), pr, 0.0)
                dv_part = _dot(pr, do, NN)
                dp = _dot(kv_ref[:, lo + NOPE:lo + QPAD], do, NT)
                dk_part = _dot(pr * (dp - dl_ref[hh]), q, NN)
                if diagonal:
                    dk_acc[hh], dv_acc[hh] = dk_part, dv_part
                else:
                    dk_acc[hh] += dk_part
                    dv_acc[hh] += dv_part

        pl.when(i > j)(functools.partial(step, False))
        pl.when(i == j)(functools.partial(step, True))

        @pl.when(i == nt - 1)
        def _():
            for hh in range(hb):
                dk_ref[:, hh * QPAD:(hh + 1) * QPAD] = dk_acc[hh] * math.log(2.0)
                dv_ref[:, hh * VHEAD:(hh + 1) * VHEAD] = dv_acc[hh]

    return pl.pallas_call(
        body, name="attn_bwd_dkv",
        grid_spec=pltpu.PrefetchScalarGridSpec(
            num_scalar_prefetch=2, grid=(heads // hb, int(qi.shape[0])),
            in_specs=[pl.BlockSpec((t, hb * QPAD), lambda h, p, qi, kj: (qi[p], h)),
                      pl.BlockSpec((t, hb * QPAD), lambda h, p, qi, kj: (kj[p], h)),
                      pl.BlockSpec((t, hb * QPAD), lambda h, p, qi, kj: (kj[p], h)),
                      pl.BlockSpec((hb, 1, t), lambda h, p, qi, kj: (h, 0, qi[p])),
                      pl.BlockSpec((hb, 1, t), lambda h, p, qi, kj: (h, 0, qi[p])),
                      pl.BlockSpec((t, hb * VHEAD), lambda h, p, qi, kj: (qi[p], h))],
            out_specs=[pl.BlockSpec((t, hb * QPAD), lambda h, p, qi, kj: (kj[p], h)),
                       pl.BlockSpec((t, hb * VHEAD), lambda h, p, qi, kj: (kj[p], h))],
            scratch_shapes=[pltpu.VMEM((hb, t, QPAD), F32), pltpu.VMEM((hb, t, VHEAD), F32)]),
        out_shape=[jax.ShapeDtypeStruct((s, heads * QPAD), F32),
                   jax.ShapeDtypeStruct((s, heads * VHEAD), F32)],
        compiler_params=_params(("parallel", "arbitrary")),
    )(qi, kj, qf, kf, kv, lse, delta, d_o)


def _merge(proj, lay, y_ret, y_mla, tm):
    s, d = y_ret.shape

    def body(gr_ref, gm_ref, yr_ref, ym_ref, out_ref):
        out_ref[...] = (_sig(gr_ref[...]) * yr_ref[...] + _sig(gm_ref[...]) * ym_ref[...]).astype(BF16)

    off = lay["off"]
    return _rows("merge", body, s, tm,
                 [(proj, d, off["g_ret"] // d), (proj, d, off["g_mla"] // d), (y_ret, d, 0), (y_mla, d, 0)],
                 [(d, BF16)])[0]


def _merge_bwd(proj, lay, d_merged, y_ret, y_mla, tm):
    s, d = y_ret.shape

    def body(dm_ref, gr_ref, gm_ref, yr_ref, ym_ref, dyr_ref, dym_ref, dgr_ref, dgm_ref):
        dm = dm_ref[...]
        for g_ref, y_ref, dy_ref, dg_ref in ((gr_ref, yr_ref, dyr_ref, dgr_ref), (gm_ref, ym_ref, dym_ref, dgm_ref)):
            sg = _sig(g_ref[...])
            dy_ref[...] = (dm * sg).astype(BF16)
            dg_ref[...] = (dm * y_ref[...] * (sg * (1.0 - sg))).astype(BF16)

    off = lay["off"]
    return _rows("merge_bwd", body, s, tm,
                 [(d_merged, d, 0), (proj, d, off["g_ret"] // d), (proj, d, off["g_mla"] // d), (y_ret, d, 0),
                  (y_mla, d, 0)],
                 [(d, BF16)] * 4)


ANY = pl.BlockSpec(memory_space=pl.ANY)


def _place():
    return lax.axis_index("x"), lax.axis_index("y"), lax.axis_index("c")


def _other_chips(x, y):
    return [(1 - x, y), (x, 1 - y), (1 - x, 1 - y)]


def _all_gather(shards):
    nw = len(shards)

    def body(*refs):
        ins, outs = refs[:nw], refs[nw:2 * nw]
        send_sems, recv_sems, local_sems = refs[2 * nw:]
        x, y, c = _place()
        me, sibling = (x, y, c), (x, y, 1 - c)
        chips = _other_chips(x, y)

        def slot(p):
            return 4 * p[0] + 2 * p[1] + p[2]

        def copy(w, k, block, to, src=None):
            dst = outs[w].at[slot(block)]
            return pltpu.make_async_remote_copy(
                src_ref=dst if src is None else src, dst_ref=dst,
                send_sem=send_sems.at[w, k], recv_sem=recv_sems.at[w, k], device_id=to, device_id_type=MESH)

        mine = [pltpu.make_async_copy(ins[w], outs[w].at[slot(me)], local_sems.at[w]) for w in range(nw)]
        for cp in mine:
            cp.start()
        first = []
        for w in range(nw):
            first.append(copy(w, 0, me, sibling, src=ins[w]))
            first += [copy(w, 1 + j, me, (*chip, c), src=ins[w]) for j, chip in enumerate(chips)]
        for cp in first:
            cp.start()
        passed = []
        for j, chip in enumerate(chips):
            for w in range(nw):
                copy(w, 1 + j, (*chip, c), me).wait_recv()
                cp = copy(w, 4 + j, (*chip, c), sibling)
                cp.start()
                passed.append(cp)
        for w in range(nw):
            copy(w, 0, sibling, me).wait_recv()
            for j, chip in enumerate(chips):
                copy(w, 4 + j, (*chip, 1 - c), me).wait_recv()
        for cp in first + passed:
            cp.wait_send()
        for cp in mine:
            cp.wait()

    return pl.pallas_call(
        body, name="weights_all_gather",
        in_specs=[ANY] * nw, out_specs=[ANY] * nw,
        out_shape=[jax.ShapeDtypeStruct((N_DEV,) + s.shape, s.dtype) for s in shards],
        scratch_shapes=[pltpu.SemaphoreType.DMA((nw, 7)), pltpu.SemaphoreType.DMA((nw, 7)),
                        pltpu.SemaphoreType.DMA((nw,))],
        compiler_params=pltpu.CompilerParams(has_side_effects=True),
    )(*shards)


def _rs_sibling(grads):
    nw = len(grads)

    def body(*refs):
        ins, outs = refs[:nw], refs[nw:2 * nw]
        send_sems, recv_sems = refs[2 * nw:]
        x, y, c = _place()
        sibling = (x, y, 1 - c)
        copies = []
        for w in range(nw):
            for p in range(4):
                cp = pltpu.make_async_remote_copy(
                    src_ref=ins[w].at[2 * p + (1 - c)], dst_ref=outs[w].at[p],
                    send_sem=send_sems.at[w, p], recv_sem=recv_sems.at[w, p], device_id=sibling, device_id_type=MESH)
                cp.start()
                copies.append(cp)
        for cp in copies:
            cp.wait()

    return pl.pallas_call(
        body, name="grads_to_sibling",
        in_specs=[ANY] * nw, out_specs=[ANY] * nw,
        out_shape=[jax.ShapeDtypeStruct((4,) + g.shape[1:], g.dtype) for g in grads],
        scratch_shapes=[pltpu.SemaphoreType.DMA((nw, 4)), pltpu.SemaphoreType.DMA((nw, 4))],
        compiler_params=pltpu.CompilerParams(has_side_effects=True),
    )(*grads)


def _pair_sum(name, g, got, c_arr, tr):
    _, rows, cols = g.shape
    tr = _tile_rows(rows, tr)

    def body(c_ref, a_ref, b_ref, o_ref):
        o_ref[...] = (a_ref[...].astype(F32) + b_ref[...].astype(F32)).astype(BF16)

    return pl.pallas_call(
        body, name=name,
        grid_spec=pltpu.PrefetchScalarGridSpec(
            num_scalar_prefetch=1, grid=(4, rows // tr),
            in_specs=[pl.BlockSpec((None, tr, cols), lambda p, r, cr: (2 * p + cr[0], r, 0)),
                      pl.BlockSpec((None, tr, cols), lambda p, r, cr: (p, r, 0))],
            out_specs=pl.BlockSpec((None, tr, cols), lambda p, r, cr: (p, r, 0))),
        out_shape=jax.ShapeDtypeStruct((4, rows, cols), BF16),
        compiler_params=_params(("parallel", "parallel")),
    )(c_arr, g, got)


def _rs_chips(sums):
    nw = len(sums)

    def body(*refs):
        ins, outs = refs[:nw], refs[nw:2 * nw]
        send_sems, recv_sems = refs[2 * nw:]
        x, y, c = _place()
        copies = []
        for w in range(nw):
            for k, (px, py) in enumerate(_other_chips(x, y)):
                cp = pltpu.make_async_remote_copy(
                    src_ref=ins[w].at[2 * px + py], dst_ref=outs[w].at[k],
                    send_sem=send_sems.at[w, k], recv_sem=recv_sems.at[w, k], device_id=(px, py, c),
                    device_id_type=MESH)
                cp.start()
                copies.append(cp)
        for cp in copies:
            cp.wait()

    return pl.pallas_call(
        body, name="grads_to_chips",
        in_specs=[ANY] * nw, out_specs=[ANY] * nw,
        out_shape=[jax.ShapeDtypeStruct((3,) + g.shape[1:], g.dtype) for g in sums],
        scratch_shapes=[pltpu.SemaphoreType.DMA((nw, 3)), pltpu.SemaphoreType.DMA((nw, 3))],
        compiler_params=pltpu.CompilerParams(has_side_effects=True),
    )(*sums)


def _tile_rows(rows, pref):
    t = min(rows, pref)
    while rows % t or t % 8:
        t -= 1
    return t


def _adam(w, g, m, v):
    m = ADAM_B1 * m + (1.0 - ADAM_B1) * g
    v = ADAM_B2 * v + (1.0 - ADAM_B2) * (g * g)
    m_hat = m / (1.0 - ADAM_B1 ** ADAM_STEP)
    v_hat = v / (1.0 - ADAM_B2 ** ADAM_STEP)
    return -ADAM_LR * (m_hat / (jnp.sqrt(v_hat) + ADAM_EPS) + ADAM_WD * w), m, v


def _adamw_shard(name, w, m, v, sums, got, chip_arr, tr):
    rows, cols = w.shape
    tr = _tile_rows(rows, tr)

    def body(p_ref, w_ref, m_ref, v_ref, s_ref, r_ref, g_ref, d_ref, nm_ref, nv_ref):
        g = s_ref[...].astype(F32)
        for k in range(3):
            g = g + r_ref[k].astype(F32)
        g_ref[...] = g
        d_ref[...], nm_ref[...], nv_ref[...] = _adam(w_ref[...], g, m_ref[...], v_ref[...])

    tile = pl.BlockSpec((tr, cols), lambda r, pr: (r, 0))
    return pl.pallas_call(
        body, name=name,
        grid_spec=pltpu.PrefetchScalarGridSpec(
            num_scalar_prefetch=1, grid=(rows // tr,),
            in_specs=[tile, tile, tile,
                      pl.BlockSpec((None, tr, cols), lambda r, pr: (pr[0], r, 0)),
                      pl.BlockSpec((3, tr, cols), lambda r, pr: (0, r, 0))],
            out_specs=[tile] * 4),
        out_shape=[jax.ShapeDtypeStruct((rows, cols), F32)] * 4,
        compiler_params=_params(("parallel",)),
    )(chip_arr, w, m, v, sums, got)


def _small_all_reduce_adam(part, w, m, v):
    rows = part.shape[0]

    def body(p_ref, w_ref, m_ref, v_ref, g_ref, d_ref, nm_ref, nv_ref, buf, send_sems, recv_sems):
        x, y, c = _place()
        me = 4 * x + 2 * y + c
        buf[me] = p_ref[...]
        peers = [(x, y, 1 - c)] + [(px, py, pc) for px, py in _other_chips(x, y) for pc in (c, 1 - c)]
        copies = []
        for k, peer in enumerate(peers):
            cp = pltpu.make_async_remote_copy(
                src_ref=buf.at[me], dst_ref=buf.at[me], send_sem=send_sems.at[k], recv_sem=recv_sems.at[k],
                device_id=peer, device_id_type=MESH)
            cp.start()
            copies.append(cp)
        for cp in copies:
            cp.wait()
        g = buf[0]
        for k in range(1, N_DEV):
            g = g + buf[k]
        g_ref[...] = g
        d_ref[...], nm_ref[...], nv_ref[...] = _adam(w_ref[...], g, m_ref[...], v_ref[...])

    vm = pl.BlockSpec(memory_space=pltpu.VMEM)
    return pl.pallas_call(
        body, name="gains_all_reduce_adamw",
        in_specs=[vm] * 4, out_specs=[vm] * 4,
        out_shape=[jax.ShapeDtypeStruct((rows, LANES), F32)] * 4,
        scratch_shapes=[pltpu.VMEM((N_DEV, rows, LANES), F32), pltpu.SemaphoreType.DMA((7,)),
                        pltpu.SemaphoreType.DMA((7,))],
        compiler_params=pltpu.CompilerParams(has_side_effects=True),
    )(part, w, m, v)


IN_ORDER = ("r_q", "r_k", "r_v", "r_g", "c_q", "c_kv", "k_pe", "g_ret", "g_mla")
MY_ORDER = ("g_ret", "g_mla", "r_v", "r_g", "r_q", "r_k", "c_q", "c_kv", "k_pe")


def _make_layout(d, vw, qw, ql, kl, mla_w):
    width = {"r_q": qw, "r_k": qw, "r_v": vw, "r_g": vw, "c_q": ql, "c_kv": kl, "k_pe": ROPE, "g_ret": d, "g_mla": d}
    src, o = {}, 0
    for n in IN_ORDER:
        src[n] = o
        o += width[n]
    off, o = {}, 0
    for n in MY_ORDER:
        off[n] = o
        o += LANES if n == "k_pe" else width[n]
    total = -(-o // 256) * 256
    for n, blk in (("g_ret", d), ("g_mla", d), ("r_v", vw), ("r_g", vw), ("r_q", RET_QK), ("r_k", RET_QK),
                   ("r_v", RET_V), ("c_q", ql), ("c_kv", kl), ("k_pe", LANES)):
        assert off[n] % blk == 0
    return {"width": width, "src": src, "off": off, "total": total, "n_in": sum(width.values()),
            "ret_heads": vw // RET_V, "mla_heads": mla_w // VHEAD, "q_lora": ql, "kv_lora": kl}


def _cols_to_full(g):
    n, r, c = g.shape
    return jnp.transpose(g, (1, 0, 2)).reshape(r, n * c)


def _full_to_cols(w):
    r, c = w.shape
    return jnp.transpose(w.reshape(r, N_DEV, c // N_DEV), (1, 0, 2))


def _w_in_to_mine(w, lay):
    parts = []
    for n in MY_ORDER:
        seg = w[:, lay["src"][n]:lay["src"][n] + lay["width"][n]]
        if n == "k_pe":
            seg = jnp.pad(seg, ((0, 0), (0, LANES - ROPE)))
        parts.append(seg)
    cur = sum(p.shape[1] for p in parts)
    parts.append(jnp.zeros((w.shape[0], lay["total"] - cur), w.dtype))
    return jnp.concatenate(parts, axis=1)


def _mine_to_w_in(w, lay):
    return jnp.concatenate([w[:, lay["off"][n]:lay["off"][n] + lay["width"][n]] for n in IN_ORDER], axis=1)


def _rope_tables(positions, half):
    inv = ROPE_THETA ** (-jnp.arange(half, dtype=F32) / half)
    ang = positions.astype(F32)[:, None] * inv
    return jnp.cos(ang), jnp.sin(ang)


def _pack_rows(vs):
    return jnp.concatenate([v.reshape(-1, LANES) for v in vs], axis=0)


def kernel(x, positions, norm_mix_g, w_in, ret_norm_g, w_ret_o, q_a_norm_g, w_q_b, kv_a_norm_g, w_kv_b, w_mla_o, w_out, norm_mlp_g, w_up, w_down, norm_f_g, loss_target, m_norm_mix_g, m_w_in, m_ret_norm_g, m_w_ret_o, m_q_a_norm_g, m_w_q_b, m_kv_a_norm_g, m_w_kv_b, m_w_mla_o, m_w_out, m_norm_mlp_g, m_w_up, m_w_down, m_norm_f_g, v_norm_mix_g, v_w_in, v_ret_norm_g, v_w_ret_o, v_q_a_norm_g, v_w_q_b, v_kv_a_norm_g, v_w_kv_b, v_w_mla_o, v_w_out, v_norm_mlp_g, v_w_up, v_w_down, v_norm_f_g):
    xs, tgt, pos = x[0], loss_target[0], positions[0]
    s, d = xs.shape
    mats = {"w_in": w_in[0], "w_ret_o": w_ret_o[0], "w_q_b": w_q_b[0], "w_kv_b": w_kv_b[0], "w_mla_o": w_mla_o[0],
            "w_out": w_out[0], "w_up": w_up[0], "w_down": w_down[0]}
    mat_m = {"w_in": m_w_in[0], "w_ret_o": m_w_ret_o[0], "w_q_b": m_w_q_b[0], "w_kv_b": m_w_kv_b[0],
             "w_mla_o": m_w_mla_o[0], "w_out": m_w_out[0], "w_up": m_w_up[0], "w_down": m_w_down[0]}
    mat_v = {"w_in": v_w_in[0], "w_ret_o": v_w_ret_o[0], "w_q_b": v_w_q_b[0], "w_kv_b": v_w_kv_b[0],
             "w_mla_o": v_w_mla_o[0], "w_out": v_w_out[0], "w_up": v_w_up[0], "w_down": v_w_down[0]}
    names = list(mats)
    col_sharded = ("w_in", "w_q_b", "w_kv_b", "w_up")
    vw = ret_norm_g.shape[1]
    mla_w = mats["w_mla_o"].shape[0] * N_DEV
    ql, kl = q_a_norm_g.shape[1], kv_a_norm_g.shape[1]
    n_in = mats["w_in"].shape[1] * N_DEV
    qw = (n_in - 2 * vw - ql - kl - ROPE - 2 * d) // 2
    lay = _make_layout(d, vw, qw, ql, kl, mla_w)
    assert lay["n_in"] == n_in
    heads_r, heads_m = lay["ret_heads"], lay["mla_heads"]

    gathered = dict(zip(names, _all_gather([mats[n].astype(BF16) for n in names])))
    full = {n: (_cols_to_full(g) if n in col_sharded else g.reshape(-1, g.shape[2])) for n, g in gathered.items()}
    w_mine = _w_in_to_mine(full["w_in"], lay)
    wq_pad = jnp.pad(full["w_q_b"].reshape(ql, heads_m, NOPE + ROPE),
                     ((0, 0), (0, 0), (0, QPAD - NOPE - ROPE))).reshape(ql, heads_m * QPAD)

    c64, s64 = _rope_tables(pos, RET_QK // 2)
    cos_r = jnp.concatenate([c64, c64], axis=1)
    sin_r = jnp.concatenate([-s64, s64], axis=1)
    c32, s32 = _rope_tables(pos, ROPE // 2)
    z32, z64 = jnp.zeros_like(c32), jnp.zeros((s, LANES - ROPE), F32)
    cos_p = jnp.concatenate([c32, c32, z64], axis=1)
    sin_a = jnp.concatenate([-s32, z32, z64], axis=1)
    sin_b = jnp.concatenate([z32, s32, z64], axis=1)
    lg = jnp.log(1.0 - 2.0 ** (-5.0 - jnp.arange(heads_r, dtype=F32)))
    lgs = jnp.broadcast_to(lg[:, None, None], (heads_r, 8, LANES))

    tm = min(256, s)
    blk = min(256, s)
    t_att = min(512, s)

    u = _rms_fwd("norm_mix", xs, norm_mix_g, tm)
    proj = _mm("in_proj", u, w_mine, "nn", F32)
    o_ret, states = _ret_fwd(proj, lay, cos_r, sin_r, lgs, blk)
    ry = _ret_post(proj, lay, o_ret, ret_norm_g, tm)
    y_ret = _mm("ret_out", ry, full["w_ret_o"], "nn", F32)
    cqn, ckvn, kpr = _mla_prep(proj, lay, q_a_norm_g, kv_a_norm_g, cos_p, sin_a, sin_b, tm)
    qp = _mm("q_up", cqn, wq_pad, "nn", F32)
    kv = _mm("kv_up", ckvn, full["w_kv_b"], "nn", BF16)
    qf, kf = _attn_prep(qp, kv, kpr, lay, cos_p, sin_a, sin_b, tm)
    o_mla, lse = _attn_fwd(qf, kf, kv, lay, t_att)
    y_mla = _mm("mla_out", o_mla, full["w_mla_o"], "nn", F32)
    merged = _merge(proj, lay, y_ret, y_mla, tm)
    mix = _mm("out_proj", merged, full["w_out"], "nn", F32)
    h1, n2 = _rms_res_fwd("norm_mlp", xs, mix, norm_mlp_g, tm)
    z, act = _mm("mlp_up", n2, full["w_up"], "nn", (F32, BF16),
                 epilogue=lambda r: (r, jnp.square(jnp.maximum(r, 0.0))))
    dn = _mm("mlp_down", act, full["w_down"], "nn", F32)
    dh2, g_norm_f, loss_part = _final("loss_head", h1, dn, norm_f_g.reshape(1, d), tgt, tm)

    dz = _mm("mlp_down_dx", dh2, full["w_down"], "nt", BF16, extras=(z,),
             epilogue=lambda r, zz: (r * (2.0 * jnp.maximum(zz, 0.0)),))
    g_w_down = _mm("mlp_down_dw", act, dh2, "tn", BF16)
    g_w_up = _mm("mlp_up_dw", n2, dz, "tn", BF16)
    dn2 = _mm("mlp_up_dx", dz, full["w_up"], "nt", F32)
    dh1, g_norm_mlp = _rms_bwd("norm_mlp_bwd", dn2, h1, norm_mlp_g, dh2, tm)
    d_merged = _mm("out_proj_dx", dh1, full["w_out"], "nt", F32)
    g_w_out = _mm("out_proj_dw", merged, dh1, "tn", BF16)
    dy_ret, dy_mla, dg_ret, dg_mla = _merge_bwd(proj, lay, d_merged, y_ret, y_mla, tm)
    d_ry = _mm("ret_out_dx", dy_ret, full["w_ret_o"], "nt", F32)
    g_w_ret_o = _mm("ret_out_dw", ry, dy_ret, "tn", BF16)
    d_omla = _mm("mla_out_dx", dy_mla, full["w_mla_o"], "nt", F32)
    g_w_mla_o = _mm("mla_out_dw", o_mla, dy_mla, "tn", BF16)
    d_oret, d_rg, g_ret_norm = _ret_post_bwd(proj, lay, d_ry, o_ret, ret_norm_g, tm)
    d_rq, d_rk, d_rv = _ret_bwd(proj, lay, cos_r, sin_r, lgs, states, d_oret, blk)
    dqf, delta = _attn_bwd_dq(qf, kf, kv, o_mla, lse, d_omla, lay, t_att)
    dkf, dv = _attn_bwd_dkv(qf, kf, kv, lse.reshape(heads_m, 1, s), delta.reshape(heads_m, 1, s), d_omla, lay, t_att)
    dqp, dkv, d_kpe = _attn_post_bwd(dqf, dkf, dv, lay, cos_p, sin_a, sin_b, tm)
    d_cqn = _mm("q_up_dx", dqp, wq_pad, "nt", F32)
    g_wq_pad = _mm("q_up_dw", cqn, dqp, "tn", BF16)
    d_ckvn = _mm("kv_up_dx", dkv, full["w_kv_b"], "nt", F32)
    g_w_kv_b = _mm("kv_up_dw", ckvn, dkv, "tn", BF16)
    d_cq, d_ckv, g_q_a, g_kv_a = _mla_prep_bwd(proj, lay, d_cqn, d_ckvn, q_a_norm_g, kv_a_norm_g, tm)
    d_by = {"g_ret": dg_ret, "g_mla": dg_mla, "r_v": d_rv, "r_g": d_rg, "r_q": d_rq, "r_k": d_rk, "c_q": d_cq,
            "c_kv": d_ckv, "k_pe": d_kpe}
    parts = [d_by[n] for n in MY_ORDER]
    parts.append(jnp.zeros((s, lay["total"] - sum(p.shape[1] for p in parts)), BF16))
    d_proj = jnp.concatenate(parts, axis=1)
    du = _mm("in_proj_dx", d_proj, w_mine, "nt", F32)
    g_w_mine = _mm("in_proj_dw", u, d_proj, "tn", BF16)
    grad_x, g_norm_mix = _rms_bwd("norm_mix_bwd", du, xs, norm_mix_g, dh1, tm)

    g_w_q_b = g_wq_pad.reshape(ql, heads_m, QPAD)[:, :, :NOPE + ROPE].reshape(ql, heads_m * (NOPE + ROPE))
    g_full = {"w_in": _mine_to_w_in(g_w_mine, lay), "w_ret_o": g_w_ret_o, "w_q_b": g_w_q_b, "w_kv_b": g_w_kv_b,
              "w_mla_o": g_w_mla_o, "w_out": g_w_out, "w_up": g_w_up, "w_down": g_w_down}
    g_blocks = [(_full_to_cols(g_full[n]) if n in col_sharded else g_full[n].reshape((N_DEV,) + mats[n].shape))
                for n in names]
    mx, my, mc = _place()
    c_arr = jnp.reshape(mc, (1,)).astype(jnp.int32)
    chip_arr = jnp.reshape(2 * mx + my, (1,)).astype(jnp.int32)
    from_sibling = _rs_sibling(g_blocks)
    sums = [_pair_sum("pair_sum_" + n, g, r, c_arr, 256) for n, g, r in zip(names, g_blocks, from_sibling)]
    from_chips = _rs_chips(sums)
    upd = {n: _adamw_shard("adamw_" + n, mats[n], mat_m[n], mat_v[n], sm, r, chip_arr, 256)
           for n, sm, r in zip(names, sums, from_chips)}

    gains = [("norm_mix_g", norm_mix_g, m_norm_mix_g, v_norm_mix_g, g_norm_mix),
             ("ret_norm_g", ret_norm_g, m_ret_norm_g, v_ret_norm_g, g_ret_norm),
             ("q_a_norm_g", q_a_norm_g, m_q_a_norm_g, v_q_a_norm_g, g_q_a),
             ("kv_a_norm_g", kv_a_norm_g, m_kv_a_norm_g, v_kv_a_norm_g, g_kv_a),
             ("norm_mlp_g", norm_mlp_g, m_norm_mlp_g, v_norm_mlp_g, g_norm_mlp),
             ("norm_f_g", norm_f_g, m_norm_f_g, v_norm_f_g, g_norm_f)]
    n_rows = sum(g[1].size for g in gains) // LANES
    pad_rows = -(-(n_rows + 1) // 8) * 8 - n_rows
    tail = jnp.zeros((pad_rows, LANES), F32)
    part = jnp.concatenate([_pack_rows([g[4] for g in gains]),
                            jnp.broadcast_to(loss_part[:, :1], (1, LANES)), tail[1:]], axis=0)
    packed = [jnp.concatenate([_pack_rows([g[k] for g in gains]), tail], axis=0) for k in (1, 2, 3)]
    g_sm, d_sm, m_sm, v_sm = _small_all_reduce_adam(part, *packed)
    loss = g_sm[n_rows, 0]
    small = {}
    o = 0
    for name, w, _, _, _ in gains:
        r = w.size // LANES
        small[name] = [a[o:o + r].reshape(w.shape) for a in (g_sm, d_sm, m_sm, v_sm)]
        o += r

    order = ["norm_mix_g", "w_in", "ret_norm_g", "w_ret_o", "q_a_norm_g", "w_q_b", "kv_a_norm_g", "w_kv_b", "w_mla_o",
             "w_out", "norm_mlp_g", "w_up", "w_down", "norm_f_g"]
    outs = [loss, grad_x[None]]
    for k in range(4):
        for n in order:
            outs.append(small[n][k] if n in small else upd[n][k][None])
    return tuple(outs)
```

```python
import functools
import math

import jax
import jax.numpy as jnp
from jax import lax
from jax.experimental import pallas as pl
from jax.experimental.pallas import tpu as pltpu

F32 = jnp.float32
BF16 = jnp.bfloat16
MESH = pl.DeviceIdType.MESH

EPS = 1e-6
ROPE_THETA = 10000.0
CHUNK_SHIFT = 6
RET_QK = 128
RET_V = 256
NOPE = 128
ROPE = 64
VHEAD = 128
QPAD = 256
LANES = 128
N_DEV = 8
VMEM_LIMIT = 56 * 1024 * 1024

ADAM_LR = 0.001
ADAM_B1 = 0.9
ADAM_B2 = 0.999
ADAM_EPS = 1e-08
ADAM_WD = 0.01
ADAM_STEP = 10

NN = (((1,), (0,)), ((), ()))
NT = (((1,), (1,)), ((), ()))
TN = (((0,), (0,)), ((), ()))


def _dot(a, b, dims):
    return lax.dot_general(a.astype(BF16), b.astype(BF16), dims, preferred_element_type=F32)


def _tile(dim, pref):
    if dim <= pref:
        return dim
    t = (pref // LANES) * LANES
    while t >= LANES:
        if dim % t == 0:
            return t
        t -= LANES
    raise ValueError(f"no tile for {dim}")


def _params(sem):
    return pltpu.CompilerParams(dimension_semantics=sem, vmem_limit_bytes=VMEM_LIMIT)


def _sig(v):
    return 1.0 / (1.0 + jnp.exp(-v))


def _mm(name, a, b, mode, out_dtypes, *, tm=1024, tn=1024, tk=2048, extras=(), epilogue=None, ride=None):
    if mode == "nn":
        (m, k), (_, n) = a.shape, b.shape
    elif mode == "nt":
        (m, k), (n, _) = a.shape, b.shape
    else:
        (k, m), (_, n) = a.shape, b.shape
    tm, tn, tk = _tile(m, tm), _tile(n, tn), _tile(k, tk)
    nk = k // tk
    dims = {"nn": NN, "nt": NT, "tn": TN}[mode]
    a_spec = (pl.BlockSpec((tk, tm), lambda i, j, kk: (kk, i)) if mode == "tn"
              else pl.BlockSpec((tm, tk), lambda i, j, kk: (i, kk)))
    b_spec = (pl.BlockSpec((tn, tk), lambda i, j, kk: (j, kk)) if mode == "nt"
              else pl.BlockSpec((tk, tn), lambda i, j, kk: (kk, j)))
    tile_spec = pl.BlockSpec((tm, tn), lambda i, j, kk: (i, j))
    n_ex = len(extras)
    single = not isinstance(out_dtypes, (tuple, list))
    dts = (out_dtypes,) if single else tuple(out_dtypes)

    grid = (m // tm, n // tn, nk)
    r_in, r_out, r_sc = ride.counts() if ride else (0, 0, 0)
    n_acc = 1 if nk > 1 else 0

    def body(a_ref, b_ref, *rest):
        ex, rest = rest[:n_ex], rest[n_ex:]
        ride_in, rest = rest[:r_in], rest[r_in:]
        outs, rest = rest[:len(dts)], rest[len(dts):]
        ride_out, rest = rest[:r_out], rest[r_out:]
        ride_scratch = rest[n_acc:]
        if ride:
            first, mid, last = _steps([pl.program_id(d) for d in range(3)], grid)
            ride.run(ride_in, ride_out, ride_scratch, (first, mid, None))

        def finish(r):
            vals = (r,) if epilogue is None else epilogue(r, *[e[...] for e in ex])
            for o, v in zip(outs, vals):
                o[...] = v.astype(o.dtype)

        part = _dot(a_ref[...], b_ref[...], dims)
        if nk == 1:
            finish(part)
        else:
            acc = rest[0]
            kk = pl.program_id(2)

            @pl.when(kk == 0)
            def _():
                acc[...] = part

            @pl.when(jnp.logical_and(kk > 0, kk < nk - 1))
            def _():
                acc[...] += part

            @pl.when(kk == nk - 1)
            def _():
                finish(acc[...] + part)

        if ride:
            ride.run(ride_in, ride_out, ride_scratch, (None, None, last))

    res = pl.pallas_call(
        body, name=name, grid=grid,
        in_specs=[a_spec, b_spec] + [tile_spec] * n_ex + [ANY] * r_in,
        out_specs=[tile_spec] * len(dts) + [ANY] * r_out,
        out_shape=[jax.ShapeDtypeStruct((m, n), d) for d in dts] + (ride.out_shape if ride else []),
        scratch_shapes=([pltpu.VMEM((tm, tn), F32)] if nk > 1 else []) + (ride.scratch if ride else []),
        compiler_params=_params(("arbitrary",) * 3 if ride else ("parallel", "parallel", "arbitrary")),
    )(a, b, *extras, *(ride.ins if ride else []))
    own = res[0] if single else res[:len(dts)]
    return (own, res[len(dts):]) if ride else own


def _rows(name, body, n_rows, tm, ins, outs, accs=()):
    in_specs, args = [], []
    for t in ins:
        if len(t) == 1:
            in_specs.append(pl.BlockSpec(t[0].shape, lambda i, nd=t[0].ndim: (0,) * nd))
        else:
            in_specs.append(pl.BlockSpec((tm, t[1]), lambda i, cb=t[2]: (i, cb)))
        args.append(t[0])
    out_specs = [pl.BlockSpec((tm, w), lambda i: (i, 0)) for w, _ in outs]
    out_specs += [pl.BlockSpec((r, w), lambda i: (0, 0)) for r, w in accs]
    out_shape = [jax.ShapeDtypeStruct((n_rows, w), d) for w, d in outs]
    out_shape += [jax.ShapeDtypeStruct((r, w), F32) for r, w in accs]
    return pl.pallas_call(
        body, name=name, grid=(n_rows // tm,), in_specs=in_specs, out_specs=out_specs, out_shape=out_shape,
        compiler_params=_params(("arbitrary",) if accs else ("parallel",)),
    )(*args)


def _zero_first(*accs):
    @pl.when(pl.program_id(0) == 0)
    def _():
        for a in accs:
            a[...] = jnp.zeros_like(a)


def _rope64(t, cos, sin):
    return t * cos + pltpu.roll(t, RET_QK // 2, 1) * sin


def _rope32(t, cos, sin_a, sin_b):
    return t * cos + pltpu.roll(t, LANES - ROPE // 2, 1) * sin_a + pltpu.roll(t, ROPE // 2, 1) * sin_b


def _rms_fwd(name, x, g, tm):
    s, d = x.shape

    def body(x_ref, g_ref, u_ref):
        v = x_ref[...]
        r = lax.rsqrt(jnp.mean(v * v, axis=-1, keepdims=True) + EPS)
        u_ref[...] = (v * r * g_ref[...]).astype(BF16)

    return _rows(name, body, s, tm, [(x, d, 0), (g,)], [(d, BF16)])[0]


def _rms_res_fwd(name, x, mix, g, tm):
    s, d = x.shape

    def body(x_ref, m_ref, g_ref, h_ref, u_ref):
        v = x_ref[...] + m_ref[...]
        h_ref[...] = v
        r = lax.rsqrt(jnp.mean(v * v, axis=-1, keepdims=True) + EPS)
        u_ref[...] = (v * r * g_ref[...]).astype(BF16)

    return _rows(name, body, s, tm, [(x, d, 0), (mix, d, 0), (g,)], [(d, F32), (d, BF16)])


def _rms_bwd(name, dy, x, g, dres, tm):
    s, d = x.shape

    def body(dy_ref, x_ref, g_ref, dres_ref, dx_ref, dg_ref):
        _zero_first(dg_ref)
        v, dyv = x_ref[...], dy_ref[...]
        r = lax.rsqrt(jnp.mean(v * v, axis=-1, keepdims=True) + EPS)
        xh = v * r
        dxh = dyv * g_ref[...]
        dx_ref[...] = dres_ref[...] + r * (dxh - xh * jnp.mean(dxh * xh, axis=-1, keepdims=True))
        dg_ref[...] += jnp.sum(dyv * xh, axis=0, keepdims=True)

    return _rows(name, body, s, tm, [(dy, d, 0), (x, d, 0), (g,), (dres, d, 0)], [(d, F32)], [(1, d)])


def _final(name, h1, dn, g, tgt, tm):
    s, d = h1.shape

    def body(h_ref, dn_ref, g_ref, t_ref, dh_ref, dg_ref, loss_ref):
        _zero_first(dg_ref, loss_ref)
        v = h_ref[...] + dn_ref[...]
        r = lax.rsqrt(jnp.mean(v * v, axis=-1, keepdims=True) + EPS)
        xh = v * r
        gv = g_ref[...]
        e = xh * gv - t_ref[...]
        loss_ref[...] += 0.5 * jnp.sum(jnp.mean(e * e, axis=-1, keepdims=True))
        dy = e * (1.0 / d)
        dg_ref[...] += jnp.sum(dy * xh, axis=0, keepdims=True)
        dxh = dy * gv
        dh_ref[...] = r * (dxh - xh * jnp.mean(dxh * xh, axis=-1, keepdims=True))

    return _rows(name, body, s, tm, [(h1, d, 0), (dn, d, 0), (g,), (tgt, d, 0)], [(d, F32)], [(1, d), (1, LANES)])


def _decay_mask(lg, blk):
    n = lax.broadcasted_iota(jnp.int32, (blk, blk), 0)
    m = lax.broadcasted_iota(jnp.int32, (blk, blk), 1)
    w = jnp.exp(lg * jnp.abs(n - m).astype(F32))
    return jnp.where(jnp.right_shift(m, CHUNK_SHIFT) <= jnp.right_shift(n, CHUNK_SHIFT), w, 0.0)


def _decays(lg, blk):
    pos = lax.broadcasted_iota(jnp.int32, (blk, 1), 0).astype(F32)
    return jnp.exp(lg * (pos + 1.0)), jnp.exp(lg * (blk - 1.0 - pos)), jnp.exp(lg * float(blk))


def _ret_fwd(proj, lay, cos, sin, lgs, blk):
    s = proj.shape[0]
    heads = lay["ret_heads"]
    nb = s // blk
    scale = RET_QK ** -0.5

    def body(lg_ref, q_ref, k_ref, v_ref, cos_ref, sin_ref, o_ref, st_ref, state, mask):
        lg = lg_ref[0:1, 0:1]

        @pl.when(pl.program_id(1) == 0)
        def _():
            state[...] = jnp.zeros_like(state)
            mask[...] = _decay_mask(lg, blk)

        a, c, gb = _decays(lg, blk)
        q = _rope64(q_ref[...], cos_ref[...], sin_ref[...])
        k = _rope64(k_ref[...], cos_ref[...], sin_ref[...]) * scale
        v = v_ref[...]
        st = state[...]
        st_ref[...] = st
        sm = _dot(q, k, NT) * mask[...]
        o_ref[...] = _dot(sm, v, NN) + _dot(q * a, st, NN)
        state[...] = st * gb + _dot(k * c, v, TN)

    qb, kb, vb = lay["off"]["r_q"] // RET_QK, lay["off"]["r_k"] // RET_QK, lay["off"]["r_v"] // RET_V
    return pl.pallas_call(
        body, name="ret_fwd", grid=(heads, nb),
        in_specs=[pl.BlockSpec((None, 8, LANES), lambda h, b: (h, 0, 0)),
                  pl.BlockSpec((blk, RET_QK), lambda h, b: (b, qb + h)),
                  pl.BlockSpec((blk, RET_QK), lambda h, b: (b, kb + h)),
                  pl.BlockSpec((blk, RET_V), lambda h, b: (b, vb + h)),
                  pl.BlockSpec((blk, LANES), lambda h, b: (b, 0)),
                  pl.BlockSpec((blk, LANES), lambda h, b: (b, 0))],
        out_specs=[pl.BlockSpec((blk, RET_V), lambda h, b: (b, h)),
                   pl.BlockSpec((None, None, RET_QK, RET_V), lambda h, b: (h, b, 0, 0))],
        out_shape=[jax.ShapeDtypeStruct((s, heads * RET_V), F32),
                   jax.ShapeDtypeStruct((heads, nb, RET_QK, RET_V), F32)],
        scratch_shapes=[pltpu.VMEM((RET_QK, RET_V), F32), pltpu.VMEM((blk, blk), F32)],
        compiler_params=_params(("parallel", "arbitrary")),
    )(lgs, proj, proj, proj, cos, sin)


def _ret_bwd(proj, lay, cos, sin, lgs, states, d_o, blk):
    s = proj.shape[0]
    heads = lay["ret_heads"]
    nb = s // blk
    scale = RET_QK ** -0.5

    def body(lg_ref, q_ref, k_ref, v_ref, cos_ref, sin_ref, st_ref, do_ref, dq_ref, dk_ref, dv_ref, dstate, mask):
        lg = lg_ref[0:1, 0:1]

        @pl.when(pl.program_id(1) == 0)
        def _():
            dstate[...] = jnp.zeros_like(dstate)
            mask[...] = _decay_mask(lg, blk)

        a, c, gb = _decays(lg, blk)
        cs, sn = cos_ref[...], sin_ref[...]
        q = _rope64(q_ref[...], cs, sn)
        k = _rope64(k_ref[...], cs, sn) * scale
        v = v_ref[...]
        st = st_ref[...]
        do = do_ref[...]
        dst = dstate[...]
        mk = mask[...]
        sm = _dot(q, k, NT) * mk
        ds = _dot(do, v, NT) * mk
        dq = _dot(ds, k, NN) + _dot(do, st, NT) * a
        dk = _dot(ds, q, TN) + _dot(v, dst, NT) * c
        dv_ref[...] = (_dot(sm, do, TN) + _dot(k * c, dst, NN)).astype(dv_ref.dtype)
        dstate[...] = dst * gb + _dot(q * a, do, TN)
        dq_ref[...] = _rope64(dq, cs, -sn).astype(dq_ref.dtype)
        dk_ref[...] = (_rope64(dk, cs, -sn) * scale).astype(dk_ref.dtype)

    qb, kb, vb = lay["off"]["r_q"] // RET_QK, lay["off"]["r_k"] // RET_QK, lay["off"]["r_v"] // RET_V
    last = nb - 1
    return pl.pallas_call(
        body, name="ret_bwd", grid=(heads, nb),
        in_specs=[pl.BlockSpec((None, 8, LANES), lambda h, b: (h, 0, 0)),
                  pl.BlockSpec((blk, RET_QK), lambda h, b: (last - b, qb + h)),
                  pl.BlockSpec((blk, RET_QK), lambda h, b: (last - b, kb + h)),
                  pl.BlockSpec((blk, RET_V), lambda h, b: (last - b, vb + h)),
                  pl.BlockSpec((blk, LANES), lambda h, b: (last - b, 0)),
                  pl.BlockSpec((blk, LANES), lambda h, b: (last - b, 0)),
                  pl.BlockSpec((None, None, RET_QK, RET_V), lambda h, b: (h, last - b, 0, 0)),
                  pl.BlockSpec((blk, RET_V), lambda h, b: (last - b, h))],
        out_specs=[pl.BlockSpec((blk, RET_QK), lambda h, b: (last - b, h)),
                   pl.BlockSpec((blk, RET_QK), lambda h, b: (last - b, h)),
                   pl.BlockSpec((blk, RET_V), lambda h, b: (last - b, h))],
        out_shape=[jax.ShapeDtypeStruct((s, heads * RET_QK), BF16),
                   jax.ShapeDtypeStruct((s, heads * RET_QK), BF16),
                   jax.ShapeDtypeStruct((s, heads * RET_V), BF16)],
        scratch_shapes=[pltpu.VMEM((RET_QK, RET_V), F32), pltpu.VMEM((blk, blk), F32)],
        compiler_params=_params(("parallel", "arbitrary")),
    )(lgs, proj, proj, proj, cos, sin, states, d_o)


def _ret_post(proj, lay, o, g, tm):
    s, vw = o.shape
    heads = lay["ret_heads"]

    def body(o_ref, rg_ref, g_ref, ry_ref):
        for h in range(heads):
            sl = slice(h * RET_V, (h + 1) * RET_V)
            oh = o_ref[:, sl]
            dlt = oh - jnp.mean(oh, axis=-1, keepdims=True)
            rstd = lax.rsqrt(jnp.mean(dlt * dlt, axis=-1, keepdims=True) + EPS)
            rg = rg_ref[:, sl]
            ry_ref[:, sl] = (dlt * rstd * g_ref[:, sl] * (rg * _sig(rg))).astype(BF16)

    return _rows("ret_post", body, s, tm, [(o, vw, 0), (proj, vw, lay["off"]["r_g"] // vw), (g,)], [(vw, BF16)])[0]


def _ret_post_bwd(proj, lay, d_ry, o, g, tm):
    s, vw = o.shape
    heads = lay["ret_heads"]

    def body(dry_ref, o_ref, rg_ref, g_ref, do_ref, drg_ref, dg_ref):
        _zero_first(dg_ref)
        for h in range(heads):
            sl = slice(h * RET_V, (h + 1) * RET_V)
            oh = o_ref[:, sl]
            dlt = oh - jnp.mean(oh, axis=-1, keepdims=True)
            rstd = lax.rsqrt(jnp.mean(dlt * dlt, axis=-1, keepdims=True) + EPS)
            oh = dlt * rstd
            gv = g_ref[:, sl]
            rg = rg_ref[:, sl]
            sg = _sig(rg)
            dry = dry_ref[:, sl]
            dt = dry * (rg * sg)
            drg_ref[:, sl] = (dry * (oh * gv) * (sg * (1.0 + rg * (1.0 - sg)))).astype(BF16)
            dg_ref[:, sl] += jnp.sum(dt * oh, axis=0, keepdims=True)
            doh = dt * gv
            do_ref[:, sl] = rstd * (doh - jnp.mean(doh, axis=-1, keepdims=True)
                                    - oh * jnp.mean(doh * oh, axis=-1, keepdims=True))

    return _rows("ret_post_bwd", body, s, tm,
                 [(d_ry, vw, 0), (o, vw, 0), (proj, vw, lay["off"]["r_g"] // vw), (g,)],
                 [(vw, F32), (vw, BF16)], [(1, vw)])


def _mla_prep(proj, lay, gq, gkv, cos, sin_a, sin_b, tm):
    s = proj.shape[0]
    ql, kl = lay["q_lora"], lay["kv_lora"]

    def body(cq_ref, ckv_ref, kpe_ref, gq_ref, gkv_ref, cos_ref, sa_ref, sb_ref, cqn_ref, ckvn_ref, kpr_ref):
        for src, gref, dst in ((cq_ref, gq_ref, cqn_ref), (ckv_ref, gkv_ref, ckvn_ref)):
            v = src[...]
            r = lax.rsqrt(jnp.mean(v * v, axis=-1, keepdims=True) + EPS)
            dst[...] = (v * r * gref[...]).astype(BF16)
        kpr_ref[...] = _rope32(kpe_ref[...], cos_ref[...], sa_ref[...], sb_ref[...]).astype(BF16)

    off = lay["off"]
    return _rows("mla_prep", body, s, tm,
                 [(proj, ql, off["c_q"] // ql), (proj, kl, off["c_kv"] // kl), (proj, LANES, off["k_pe"] // LANES),
                  (gq,), (gkv,), (cos, LANES, 0), (sin_a, LANES, 0), (sin_b, LANES, 0)],
                 [(ql, BF16), (kl, BF16), (LANES, BF16)])


def _mla_prep_bwd(proj, lay, d_cqn, d_ckvn, gq, gkv, tm):
    s = proj.shape[0]
    ql, kl = lay["q_lora"], lay["kv_lora"]

    def body(dq_ref, dkv_ref, cq_ref, ckv_ref, gq_ref, gkv_ref, dcq_ref, dckv_ref, dgq_ref, dgkv_ref):
        _zero_first(dgq_ref, dgkv_ref)
        for dref, src, gref, dst, dg in ((dq_ref, cq_ref, gq_ref, dcq_ref, dgq_ref),
                                         (dkv_ref, ckv_ref, gkv_ref, dckv_ref, dgkv_ref)):
            v, dy = src[...], dref[...]
            r = lax.rsqrt(jnp.mean(v * v, axis=-1, keepdims=True) + EPS)
            xh = v * r
            dxh = dy * gref[...]
            dst[...] = (r * (dxh - xh * jnp.mean(dxh * xh, axis=-1, keepdims=True))).astype(BF16)
            dg[...] += jnp.sum(dy * xh, axis=0, keepdims=True)

    off = lay["off"]
    return _rows("mla_prep_bwd", body, s, tm,
                 [(d_cqn, ql, 0), (d_ckvn, kl, 0), (proj, ql, off["c_q"] // ql), (proj, kl, off["c_kv"] // kl),
                  (gq,), (gkv,)],
                 [(ql, BF16), (kl, BF16)], [(1, ql), (1, kl)])


def _attn_prep(qp, kv, kpr, lay, cos, sin_a, sin_b, tm):
    s = qp.shape[0]
    heads = lay["mla_heads"]
    w = heads * QPAD

    def body(qp_ref, kv_ref, kpr_ref, cos_ref, sa_ref, sb_ref, qf_ref, kf_ref):
        qs = (NOPE + ROPE) ** -0.5 * math.log2(math.e)
        cs, sa, sb = cos_ref[...] * qs, sa_ref[...] * qs, sb_ref[...] * qs
        kp = kpr_ref[...]
        for h in range(heads):
            lo, hi = h * QPAD, h * QPAD + NOPE
            qf_ref[:, lo:hi] = (qp_ref[:, lo:hi] * qs).astype(BF16)
            qf_ref[:, hi:hi + LANES] = _rope32(qp_ref[:, hi:hi + LANES], cs, sa, sb).astype(BF16)
            kf_ref[:, lo:hi] = kv_ref[:, lo:hi]
            kf_ref[:, hi:hi + LANES] = kp

    return _rows("attn_prep", body, s, tm,
                 [(qp, w, 0), (kv, w, 0), (kpr, LANES, 0), (cos, LANES, 0), (sin_a, LANES, 0), (sin_b, LANES, 0)],
                 [(w, BF16), (w, BF16)])


def _attn_post_bwd(dqf, dkf, dv, lay, cos, sin_a, sin_b, tm):
    s = dqf.shape[0]
    heads = lay["mla_heads"]
    w = heads * QPAD

    def body(dqf_ref, dkf_ref, dv_ref, cos_ref, sa_ref, sb_ref, dqp_ref, dkv_ref, dkpe_ref):
        cs, sa, sb = cos_ref[...], -sa_ref[...], -sb_ref[...]
        kpe = jnp.zeros((tm, LANES), F32)
        for h in range(heads):
            lo, hi = h * QPAD, h * QPAD + NOPE
            dqp_ref[:, lo:hi] = dqf_ref[:, lo:hi].astype(BF16)
            dqp_ref[:, hi:hi + LANES] = _rope32(dqf_ref[:, hi:hi + LANES], cs, sa, sb).astype(BF16)
            dkv_ref[:, lo:hi] = dkf_ref[:, lo:hi].astype(BF16)
            dkv_ref[:, hi:hi + LANES] = dv_ref[:, h * VHEAD:(h + 1) * VHEAD].astype(BF16)
            kpe = kpe + dkf_ref[:, hi:hi + LANES]
        dkpe_ref[...] = _rope32(kpe, cs, sa, sb).astype(BF16)

    return _rows("attn_post_bwd", body, s, tm,
                 [(dqf, w, 0), (dkf, w, 0), (dv, heads * VHEAD, 0), (cos, LANES, 0), (sin_a, LANES, 0),
                  (sin_b, LANES, 0)],
                 [(w, BF16), (w, BF16), (LANES, BF16)])


def _diag_mask(t, keys_on_rows=False):
    row = lax.broadcasted_iota(jnp.int32, (t, t), 0)
    col = lax.broadcasted_iota(jnp.int32, (t, t), 1)
    key, query = (row, col) if keys_on_rows else (col, row)
    return jnp.right_shift(key, CHUNK_SHIFT) <= jnp.right_shift(query, CHUNK_SHIFT)


def _tile_pairs(nt, by_key):
    if by_key:
        pairs = [(i, j) for j in range(nt) for i in range(j, nt)]
    else:
        pairs = [(i, j) for i in range(nt) for j in range(i + 1)]
    return (jnp.asarray([p[0] for p in pairs], jnp.int32), jnp.asarray([p[1] for p in pairs], jnp.int32))


def _head_block(heads):
    return 2 if heads % 2 == 0 else 1


def _attn_fwd(qf, kf, kv, lay, t, ride=None):
    s = qf.shape[0]
    heads = lay["mla_heads"]
    hb = _head_block(heads)
    nt = s // t
    qi, kj = _tile_pairs(nt, False)
    grid = (heads // hb, int(qi.shape[0]))
    ride_in_specs, ride_ins, ride_out_specs, ride_out_shape, ride_scratch = _ride_args(ride)

    def body(qi_ref, kj_ref, q_ref, k_ref, kv_ref, o_ref, lse_ref, m_s, l_s, acc):
        p = pl.program_id(1)
        i, j = qi_ref[p], kj_ref[p]

        @pl.when(j == 0)
        def _():
            m_s[...] = jnp.full_like(m_s, -jnp.inf)
            l_s[...] = jnp.zeros_like(l_s)
            acc[...] = jnp.zeros_like(acc)

        def step(diagonal):
            ones = jnp.ones((t, LANES), BF16)
            scores = [_dot(q_ref[:, hh * QPAD:(hh + 1) * QPAD], k_ref[:, hh * QPAD:(hh + 1) * QPAD], NT)
                      for hh in range(hb)]
            for hh in range(hb):
                sc = scores[hh]
                if diagonal:
                    sc = jnp.where(_diag_mask(t), sc, -jnp.inf)
                cols = [sc[:, c * LANES:(c + 1) * LANES] for c in range(t // LANES)]
                m_old = m_s[hh]
                m_new = jnp.maximum(m_old, jnp.max(functools.reduce(jnp.maximum, cols), axis=-1, keepdims=True))
                alpha = jnp.exp2(m_old - m_new)
                pr = jnp.concatenate([jnp.exp2(c - m_new).astype(BF16) for c in cols], axis=1)
                pv = _dot(pr, jnp.concatenate([kv_ref[:, hh * QPAD + NOPE:(hh + 1) * QPAD], ones], axis=1), NN)
                l_new = alpha * l_s[hh] + pv[:, VHEAD:]
                a_new = alpha * acc[hh] + pv[:, :VHEAD]
                if diagonal:
                    o_ref[:, hh * VHEAD:(hh + 1) * VHEAD] = a_new / l_new
                    lse_ref[hh] = (m_new + jnp.log2(l_new))[:, :1]
                else:
                    m_s[hh], l_s[hh], acc[hh] = m_new, l_new, a_new

        pl.when(j < i)(functools.partial(step, False))
        pl.when(j == i)(functools.partial(step, True))

    res = pl.pallas_call(
        _with_ride(body, ride, grid, 2, 3, 2), name="attn_fwd",
        grid_spec=pltpu.PrefetchScalarGridSpec(
            num_scalar_prefetch=2, grid=grid,
            in_specs=[pl.BlockSpec((t, hb * QPAD), lambda h, p, qi, kj: (qi[p], h)),
                      pl.BlockSpec((t, hb * QPAD), lambda h, p, qi, kj: (kj[p], h)),
                      pl.BlockSpec((t, hb * QPAD), lambda h, p, qi, kj: (kj[p], h))] + ride_in_specs,
            out_specs=[pl.BlockSpec((t, hb * VHEAD), lambda h, p, qi, kj: (qi[p], h)),
                       pl.BlockSpec((hb, t, 1), lambda h, p, qi, kj: (h, qi[p], 0))] + ride_out_specs,
            scratch_shapes=[pltpu.VMEM((hb, t, LANES), F32), pltpu.VMEM((hb, t, LANES), F32),
                            pltpu.VMEM((hb, t, VHEAD), F32)] + ride_scratch),
        out_shape=[jax.ShapeDtypeStruct((s, heads * VHEAD), F32),
                   jax.ShapeDtypeStruct((heads, s, 1), F32)] + ride_out_shape,
        compiler_params=_params(("arbitrary", "arbitrary") if ride else ("parallel", "arbitrary")),
    )(qi, kj, qf, kf, kv, *ride_ins)
    return (res[0], res[1], res[2:]) if ride else res


def _attn_bwd_dq(qf, kf, kv, o, lse, d_o, lay, t, ride=None):
    s = qf.shape[0]
    heads = lay["mla_heads"]
    hb = _head_block(heads)
    nt = s // t
    scale = (NOPE + ROPE) ** -0.5
    qi, kj = _tile_pairs(nt, False)
    grid = (heads // hb, int(qi.shape[0]))
    ride_in_specs, ride_ins, ride_out_specs, ride_out_shape, ride_scratch = _ride_args(ride)

    def body(qi_ref, kj_ref, q_ref, k_ref, kv_ref, o_ref, lse_ref, do_ref, dq_ref, dl_ref, acc):
        p = pl.program_id(1)
        i, j = qi_ref[p], kj_ref[p]

        @pl.when(j == 0)
        def _():
            acc[...] = jnp.zeros_like(acc)
            for hh in range(hb):
                sl = slice(hh * VHEAD, (hh + 1) * VHEAD)
                dl_ref[hh] = jnp.sum(do_ref[:, sl] * o_ref[:, sl], axis=-1, keepdims=True)

        def step(diagonal):
            for hh in range(hb):
                lo = hh * QPAD
                k = k_ref[:, lo:lo + QPAD]
                sc = _dot(q_ref[:, lo:lo + QPAD], k, NT)
                pr = jnp.exp2(sc - lse_ref[hh])
                if diagonal:
                    pr = jnp.where(_diag_mask(t), pr, 0.0)
                dp = _dot(do_ref[:, hh * VHEAD:(hh + 1) * VHEAD], kv_ref[:, lo + NOPE:lo + QPAD], NT)
                a_new = acc[hh] + _dot(pr * (dp - dl_ref[hh]), k, NN)
                if diagonal:
                    dq_ref[:, lo:lo + QPAD] = a_new * scale
                else:
                    acc[hh] = a_new

        pl.when(j < i)(functools.partial(step, False))
        pl.when(j == i)(functools.partial(step, True))

    res = pl.pallas_call(
        _with_ride(body, ride, grid, 2, 6, 2), name="attn_bwd_dq",
        grid_spec=pltpu.PrefetchScalarGridSpec(
            num_scalar_prefetch=2, grid=grid,
            in_specs=[pl.BlockSpec((t, hb * QPAD), lambda h, p, qi, kj: (qi[p], h)),
                      pl.BlockSpec((t, hb * QPAD), lambda h, p, qi, kj: (kj[p], h)),
                      pl.BlockSpec((t, hb * QPAD), lambda h, p, qi, kj: (kj[p], h)),
                      pl.BlockSpec((t, hb * VHEAD), lambda h, p, qi, kj: (qi[p], h)),
                      pl.BlockSpec((hb, t, 1), lambda h, p, qi, kj: (h, qi[p], 0)),
                      pl.BlockSpec((t, hb * VHEAD), lambda h, p, qi, kj: (qi[p], h))] + ride_in_specs,
            out_specs=[pl.BlockSpec((t, hb * QPAD), lambda h, p, qi, kj: (qi[p], h)),
                       pl.BlockSpec((hb, t, 1), lambda h, p, qi, kj: (h, qi[p], 0))] + ride_out_specs,
            scratch_shapes=[pltpu.VMEM((hb, t, QPAD), F32)] + ride_scratch),
        out_shape=[jax.ShapeDtypeStruct((s, heads * QPAD), F32),
                   jax.ShapeDtypeStruct((heads, s, 1), F32)] + ride_out_shape,
        compiler_params=_params(("arbitrary", "arbitrary") if ride else ("parallel", "arbitrary")),
    )(qi, kj, qf, kf, kv, o, lse, d_o, *ride_ins)
    return (res[0], res[1], res[2:]) if ride else res


def _attn_bwd_dkv(qf, kf, kv, lse, delta, d_o, lay, t, ride=None):
    s = qf.shape[0]
    heads = lay["mla_heads"]
    hb = _head_block(heads)
    nt = s // t
    qi, kj = _tile_pairs(nt, True)
    grid = (heads // hb, int(qi.shape[0]))
    ride_in_specs, ride_ins, ride_out_specs, ride_out_shape, ride_scratch = _ride_args(ride)

    def body(qi_ref, kj_ref, q_ref, k_ref, kv_ref, lse_ref, dl_ref, do_ref, dk_ref, dv_ref, dk_acc, dv_acc):
        p = pl.program_id(1)
        i, j = qi_ref[p], kj_ref[p]

        def step(diagonal):
            for hh in range(hb):
                lo = hh * QPAD
                q = q_ref[:, lo:lo + QPAD]
                do = do_ref[:, hh * VHEAD:(hh + 1) * VHEAD]
                sc = _dot(k_ref[:, lo:lo + QPAD], q, NT)
                pr = jnp.exp2(sc - lse_ref[hh])
                if diagonal:
                    pr = jnp.where(_diag_mask(t, keys_on_rows=True), pr, 0.0)
                dv_part = _dot(pr, do, NN)
                dp = _dot(kv_ref[:, lo + NOPE:lo + QPAD], do, NT)
                dk_part = _dot(pr * (dp - dl_ref[hh]), q, NN)
                if diagonal:
                    dk_acc[hh], dv_acc[hh] = dk_part, dv_part
                else:
                    dk_acc[hh] += dk_part
                    dv_acc[hh] += dv_part

        pl.when(i > j)(functools.partial(step, False))
        pl.when(i == j)(functools.partial(step, True))

        @pl.when(i == nt - 1)
        def _():
            for hh in range(hb):
                dk_ref[:, hh * QPAD:(hh + 1) * QPAD] = dk_acc[hh] * math.log(2.0)
                dv_ref[:, hh * VHEAD:(hh + 1) * VHEAD] = dv_acc[hh]

    res = pl.pallas_call(
        _with_ride(body, ride, grid, 2, 6, 2), name="attn_bwd_dkv",
        grid_spec=pltpu.PrefetchScalarGridSpec(
            num_scalar_prefetch=2, grid=grid,
            in_specs=[pl.BlockSpec((t, hb * QPAD), lambda h, p, qi, kj: (qi[p], h)),
                      pl.BlockSpec((t, hb * QPAD), lambda h, p, qi, kj: (kj[p], h)),
                      pl.BlockSpec((t, hb * QPAD), lambda h, p, qi, kj: (kj[p], h)),
                      pl.BlockSpec((hb, 1, t), lambda h, p, qi, kj: (h, 0, qi[p])),
                      pl.BlockSpec((hb, 1, t), lambda h, p, qi, kj: (h, 0, qi[p])),
                      pl.BlockSpec((t, hb * VHEAD), lambda h, p, qi, kj: (qi[p], h))] + ride_in_specs,
            out_specs=[pl.BlockSpec((t, hb * QPAD), lambda h, p, qi, kj: (kj[p], h)),
                       pl.BlockSpec((t, hb * VHEAD), lambda h, p, qi, kj: (kj[p], h))] + ride_out_specs,
            scratch_shapes=[pltpu.VMEM((hb, t, QPAD), F32), pltpu.VMEM((hb, t, VHEAD), F32)] + ride_scratch),
        out_shape=[jax.ShapeDtypeStruct((s, heads * QPAD), F32),
                   jax.ShapeDtypeStruct((s, heads * VHEAD), F32)] + ride_out_shape,
        compiler_params=_params(("arbitrary", "arbitrary") if ride else ("parallel", "arbitrary")),
    )(qi, kj, qf, kf, kv, lse, delta, d_o, *ride_ins)
    return (res[0], res[1], res[2:]) if ride else res


def _merge(proj, lay, y_ret, y_mla, tm):
    s, d = y_ret.shape

    def body(gr_ref, gm_ref, yr_ref, ym_ref, out_ref):
        out_ref[...] = (_sig(gr_ref[...]) * yr_ref[...] + _sig(gm_ref[...]) * ym_ref[...]).astype(BF16)

    off = lay["off"]
    return _rows("merge", body, s, tm,
                 [(proj, d, off["g_ret"] // d), (proj, d, off["g_mla"] // d), (y_ret, d, 0), (y_mla, d, 0)],
                 [(d, BF16)])[0]


def _merge_bwd(proj, lay, d_merged, y_ret, y_mla, tm):
    s, d = y_ret.shape

    def body(dm_ref, gr_ref, gm_ref, yr_ref, ym_ref, dyr_ref, dym_ref, dgr_ref, dgm_ref):
        dm = dm_ref[...]
        for g_ref, y_ref, dy_ref, dg_ref in ((gr_ref, yr_ref, dyr_ref, dgr_ref), (gm_ref, ym_ref, dym_ref, dgm_ref)):
            sg = _sig(g_ref[...])
            dy_ref[...] = (dm * sg).astype(BF16)
            dg_ref[...] = (dm * y_ref[...] * (sg * (1.0 - sg))).astype(BF16)

    off = lay["off"]
    return _rows("merge_bwd", body, s, tm,
                 [(d_merged, d, 0), (proj, d, off["g_ret"] // d), (proj, d, off["g_mla"] // d), (y_ret, d, 0),
                  (y_mla, d, 0)],
                 [(d, BF16)] * 4)


ANY = pl.BlockSpec(memory_space=pl.ANY)


def _place():
    return lax.axis_index("x"), lax.axis_index("y"), lax.axis_index("c")


def _other_chips(x, y):
    return [(1 - x, y), (x, 1 - y), (1 - x, 1 - y)]


class _Exchange:
    def __init__(self, ins, out_shape, scratch, phases):
        self.ins, self.out_shape, self.scratch, self.phases = list(ins), list(out_shape), list(scratch), phases

    def counts(self):
        return len(self.ins), len(self.out_shape), len(self.scratch)

    def run(self, r_in, r_out, r_scratch, conds):
        for cond, phase in zip(conds, self.phases):
            if phase is not None and cond is not None:
                pl.when(cond)(functools.partial(phase, r_in, r_out, r_scratch))


def _steps(ids, sizes):
    lin, total = 0, 1
    for i, n in zip(ids, sizes):
        lin, total = lin * n + i, total * n
    return lin == 0, lin == total // 2, lin == total - 1


def _ride_args(ride):
    if ride is None:
        return [], [], [], [], []
    n_in, n_out, _ = ride.counts()
    return [ANY] * n_in, ride.ins, [ANY] * n_out, ride.out_shape, ride.scratch


def _with_ride(body, ride, grid, n_prefetch, n_in, n_out):
    if ride is None:
        return body
    r_in, r_out, r_sc = ride.counts()

    def hosted(*refs):
        cuts = (n_prefetch, n_in, r_in, n_out, r_out)
        parts, pos = [], 0
        for n in cuts:
            parts.append(refs[pos:pos + n])
            pos += n
        pre, ins, ride_in, outs, ride_out = parts
        scratch, ride_scratch = refs[pos:len(refs) - r_sc], refs[len(refs) - r_sc:]
        first, mid, last = _steps([pl.program_id(d) for d in range(len(grid))], grid)
        ride.run(ride_in, ride_out, ride_scratch, (first, mid, None))
        body(*pre, *ins, *outs, *scratch)
        ride.run(ride_in, ride_out, ride_scratch, (None, None, last))

    return hosted


def _exchange_alone(name, ex):
    n_in, n_out, _ = ex.counts()

    def body(*refs):
        for phase in ex.phases:
            if phase is not None:
                phase(refs[:n_in], refs[n_in:n_in + n_out], refs[n_in + n_out:])

    return pl.pallas_call(
        body, name=name, in_specs=[ANY] * n_in, out_specs=[ANY] * n_out, out_shape=ex.out_shape,
        scratch_shapes=ex.scratch)(*ex.ins)


def _gather_exchange(shards):
    nw = len(shards)

    def parts(ins, outs, sems):
        send_sems, recv_sems, local_sems = sems
        x, y, c = _place()
        me, sibling = (x, y, c), (x, y, 1 - c)
        chips = _other_chips(x, y)

        def slot(p):
            return 4 * p[0] + 2 * p[1] + p[2]

        def copy(w, k, block, to, src=None):
            dst = outs[w].at[slot(block)]
            return pltpu.make_async_remote_copy(
                src_ref=dst if src is None else src, dst_ref=dst,
                send_sem=send_sems.at[w, k], recv_sem=recv_sems.at[w, k], device_id=to, device_id_type=MESH)

        mine = [pltpu.make_async_copy(ins[w], outs[w].at[slot(me)], local_sems.at[w]) for w in range(nw)]
        first = []
        for w in range(nw):
            first.append(copy(w, 0, me, sibling, src=ins[w]))
            first += [copy(w, 1 + j, me, (*chip, c), src=ins[w]) for j, chip in enumerate(chips)]
        passed = [copy(w, 4 + j, (*chip, c), sibling) for j, chip in enumerate(chips) for w in range(nw)]
        return me, sibling, chips, c, copy, mine, first, passed

    def start(ins, outs, sems):
        _, _, _, _, _, mine, first, _ = parts(ins, outs, sems)
        for cp in mine + first:
            cp.start()

    def middle(ins, outs, sems):
        me, _, chips, c, copy, _, _, passed = parts(ins, outs, sems)
        for j, chip in enumerate(chips):
            for w in range(nw):
                copy(w, 1 + j, (*chip, c), me).wait_recv()
                passed[j * nw + w].start()

    def finish(ins, outs, sems):
        me, sibling, chips, c, copy, mine, first, passed = parts(ins, outs, sems)
        for w in range(nw):
            copy(w, 0, sibling, me).wait_recv()
            for j, chip in enumerate(chips):
                copy(w, 4 + j, (*chip, 1 - c), me).wait_recv()
        for cp in first + passed:
            cp.wait_send()
        for cp in mine:
            cp.wait()

    return _Exchange(
        shards, [jax.ShapeDtypeStruct((N_DEV,) + s.shape, s.dtype) for s in shards],
        [pltpu.SemaphoreType.DMA((nw, 7)), pltpu.SemaphoreType.DMA((nw, 7)), pltpu.SemaphoreType.DMA((nw,))],
        (start, middle, finish))


def _sibling_exchange(grads):
    nw = len(grads)

    def copies(ins, outs, sems):
        x, y, c = _place()
        return [pltpu.make_async_remote_copy(
            src_ref=ins[w].at[2 * p + (1 - c)], dst_ref=outs[w].at[p], send_sem=sems[0].at[w, p],
            recv_sem=sems[1].at[w, p], device_id=(x, y, 1 - c), device_id_type=MESH)
            for w in range(nw) for p in range(4)]

    def start(ins, outs, sems):
        for cp in copies(ins, outs, sems):
            cp.start()

    def finish(ins, outs, sems):
        for cp in copies(ins, outs, sems):
            cp.wait()

    return _Exchange(grads, [jax.ShapeDtypeStruct((4,) + g.shape[1:], g.dtype) for g in grads],
                     [pltpu.SemaphoreType.DMA((nw, 4)), pltpu.SemaphoreType.DMA((nw, 4))], (start, None, finish))


def _chips_exchange(sums):
    nw = len(sums)

    def copies(ins, outs, sems):
        x, y, c = _place()
        return [pltpu.make_async_remote_copy(
            src_ref=ins[w].at[2 * px + py], dst_ref=outs[w].at[k], send_sem=sems[0].at[w, k],
            recv_sem=sems[1].at[w, k], device_id=(px, py, c), device_id_type=MESH)
            for w in range(nw) for k, (px, py) in enumerate(_other_chips(x, y))]

    def start(ins, outs, sems):
        for cp in copies(ins, outs, sems):
            cp.start()

    def finish(ins, outs, sems):
        for cp in copies(ins, outs, sems):
            cp.wait()

    return _Exchange(sums, [jax.ShapeDtypeStruct((3,) + g.shape[1:], g.dtype) for g in sums],
                     [pltpu.SemaphoreType.DMA((nw, 3)), pltpu.SemaphoreType.DMA((nw, 3))], (start, None, finish))


def _pair_sum(name, g, got, c_arr, tr):
    _, rows, cols = g.shape
    tr = _tile_rows(rows, tr)

    def body(c_ref, a_ref, b_ref, o_ref):
        o_ref[...] = (a_ref[...].astype(F32) + b_ref[...].astype(F32)).astype(BF16)

    return pl.pallas_call(
        body, name=name,
        grid_spec=pltpu.PrefetchScalarGridSpec(
            num_scalar_prefetch=1, grid=(4, rows // tr),
            in_specs=[pl.BlockSpec((None, tr, cols), lambda p, r, cr: (2 * p + cr[0], r, 0)),
                      pl.BlockSpec((None, tr, cols), lambda p, r, cr: (p, r, 0))],
            out_specs=pl.BlockSpec((None, tr, cols), lambda p, r, cr: (p, r, 0))),
        out_shape=jax.ShapeDtypeStruct((4, rows, cols), BF16),
        compiler_params=_params(("parallel", "parallel")),
    )(c_arr, g, got)


def _tile_rows(rows, pref):
    t = min(rows, pref)
    while rows % t or t % 8:
        t -= 1
    return t


def _adam(w, g, m, v):
    m = ADAM_B1 * m + (1.0 - ADAM_B1) * g
    v = ADAM_B2 * v + (1.0 - ADAM_B2) * (g * g)
    m_hat = m / (1.0 - ADAM_B1 ** ADAM_STEP)
    v_hat = v / (1.0 - ADAM_B2 ** ADAM_STEP)
    return -ADAM_LR * (m_hat / (jnp.sqrt(v_hat) + ADAM_EPS) + ADAM_WD * w), m, v


def _adamw_shard(name, w, m, v, sums, got, chip_arr, tr):
    rows, cols = w.shape
    tr = _tile_rows(rows, tr)

    def body(p_ref, w_ref, m_ref, v_ref, s_ref, r_ref, g_ref, d_ref, nm_ref, nv_ref):
        g = s_ref[...].astype(F32)
        for k in range(3):
            g = g + r_ref[k].astype(F32)
        g_ref[...] = g
        d_ref[...], nm_ref[...], nv_ref[...] = _adam(w_ref[...], g, m_ref[...], v_ref[...])

    tile = pl.BlockSpec((tr, cols), lambda r, pr: (r, 0))
    return pl.pallas_call(
        body, name=name,
        grid_spec=pltpu.PrefetchScalarGridSpec(
            num_scalar_prefetch=1, grid=(rows // tr,),
            in_specs=[tile, tile, tile,
                      pl.BlockSpec((None, tr, cols), lambda r, pr: (pr[0], r, 0)),
                      pl.BlockSpec((3, tr, cols), lambda r, pr: (0, r, 0))],
            out_specs=[tile] * 4),
        out_shape=[jax.ShapeDtypeStruct((rows, cols), F32)] * 4,
        compiler_params=_params(("parallel",)),
    )(chip_arr, w, m, v, sums, got)


def _small_all_reduce_adam(part, w, m, v):
    rows = part.shape[0]

    def body(p_ref, w_ref, m_ref, v_ref, g_ref, d_ref, nm_ref, nv_ref, buf, send_sems, recv_sems):
        x, y, c = _place()
        me = 4 * x + 2 * y + c
        buf[me] = p_ref[...]
        peers = [(x, y, 1 - c)] + [(px, py, pc) for px, py in _other_chips(x, y) for pc in (c, 1 - c)]
        copies = []
        for k, peer in enumerate(peers):
            cp = pltpu.make_async_remote_copy(
                src_ref=buf.at[me], dst_ref=buf.at[me], send_sem=send_sems.at[k], recv_sem=recv_sems.at[k],
                device_id=peer, device_id_type=MESH)
            cp.start()
            copies.append(cp)
        for cp in copies:
            cp.wait()
        g = buf[0]
        for k in range(1, N_DEV):
            g = g + buf[k]
        g_ref[...] = g
        d_ref[...], nm_ref[...], nv_ref[...] = _adam(w_ref[...], g, m_ref[...], v_ref[...])

    vm = pl.BlockSpec(memory_space=pltpu.VMEM)
    return pl.pallas_call(
        body, name="gains_all_reduce_adamw",
        in_specs=[vm] * 4, out_specs=[vm] * 4,
        out_shape=[jax.ShapeDtypeStruct((rows, LANES), F32)] * 4,
        scratch_shapes=[pltpu.VMEM((N_DEV, rows, LANES), F32), pltpu.SemaphoreType.DMA((7,)),
                        pltpu.SemaphoreType.DMA((7,))],
        compiler_params=pltpu.CompilerParams(has_side_effects=True),
    )(part, w, m, v)


IN_ORDER = ("r_q", "r_k", "r_v", "r_g", "c_q", "c_kv", "k_pe", "g_ret", "g_mla")
MY_ORDER = ("g_ret", "g_mla", "r_v", "r_g", "r_q", "r_k", "c_q", "c_kv", "k_pe")


def _make_layout(d, vw, qw, ql, kl, mla_w):
    width = {"r_q": qw, "r_k": qw, "r_v": vw, "r_g": vw, "c_q": ql, "c_kv": kl, "k_pe": ROPE, "g_ret": d, "g_mla": d}
    src, o = {}, 0
    for n in IN_ORDER:
        src[n] = o
        o += width[n]
    off, o = {}, 0
    for n in MY_ORDER:
        off[n] = o
        o += LANES if n == "k_pe" else width[n]
    total = -(-o // 256) * 256
    for n, blk in (("g_ret", d), ("g_mla", d), ("r_v", vw), ("r_g", vw), ("r_q", RET_QK), ("r_k", RET_QK),
                   ("r_v", RET_V), ("c_q", ql), ("c_kv", kl), ("k_pe", LANES)):
        assert off[n] % blk == 0
    return {"width": width, "src": src, "off": off, "total": total, "n_in": sum(width.values()),
            "ret_heads": vw // RET_V, "mla_heads": mla_w // VHEAD, "q_lora": ql, "kv_lora": kl}


def _cols_to_full(g):
    n, r, c = g.shape
    return jnp.transpose(g, (1, 0, 2)).reshape(r, n * c)


def _full_to_cols(w):
    r, c = w.shape
    return jnp.transpose(w.reshape(r, N_DEV, c // N_DEV), (1, 0, 2))


def _w_in_to_mine(w, lay):
    parts = []
    for n in MY_ORDER:
        seg = w[:, lay["src"][n]:lay["src"][n] + lay["width"][n]]
        if n == "k_pe":
            seg = jnp.pad(seg, ((0, 0), (0, LANES - ROPE)))
        parts.append(seg)
    cur = sum(p.shape[1] for p in parts)
    parts.append(jnp.zeros((w.shape[0], lay["total"] - cur), w.dtype))
    return jnp.concatenate(parts, axis=1)


def _mine_to_w_in(w, lay):
    return jnp.concatenate([w[:, lay["off"][n]:lay["off"][n] + lay["width"][n]] for n in IN_ORDER], axis=1)


def _rope_tables(positions, half):
    inv = ROPE_THETA ** (-jnp.arange(half, dtype=F32) / half)
    ang = positions.astype(F32)[:, None] * inv
    return jnp.cos(ang), jnp.sin(ang)


def _pack_rows(vs):
    return jnp.concatenate([v.reshape(-1, LANES) for v in vs], axis=0)


def kernel(x, positions, norm_mix_g, w_in, ret_norm_g, w_ret_o, q_a_norm_g, w_q_b, kv_a_norm_g, w_kv_b, w_mla_o, w_out, norm_mlp_g, w_up, w_down, norm_f_g, loss_target, m_norm_mix_g, m_w_in, m_ret_norm_g, m_w_ret_o, m_q_a_norm_g, m_w_q_b, m_kv_a_norm_g, m_w_kv_b, m_w_mla_o, m_w_out, m_norm_mlp_g, m_w_up, m_w_down, m_norm_f_g, v_norm_mix_g, v_w_in, v_ret_norm_g, v_w_ret_o, v_q_a_norm_g, v_w_q_b, v_kv_a_norm_g, v_w_kv_b, v_w_mla_o, v_w_out, v_norm_mlp_g, v_w_up, v_w_down, v_norm_f_g):
    xs, tgt, pos = x[0], loss_target[0], positions[0]
    s, d = xs.shape
    mats = {"w_in": w_in[0], "w_ret_o": w_ret_o[0], "w_q_b": w_q_b[0], "w_kv_b": w_kv_b[0], "w_mla_o": w_mla_o[0],
            "w_out": w_out[0], "w_up": w_up[0], "w_down": w_down[0]}
    mat_m = {"w_in": m_w_in[0], "w_ret_o": m_w_ret_o[0], "w_q_b": m_w_q_b[0], "w_kv_b": m_w_kv_b[0],
             "w_mla_o": m_w_mla_o[0], "w_out": m_w_out[0], "w_up": m_w_up[0], "w_down": m_w_down[0]}
    mat_v = {"w_in": v_w_in[0], "w_ret_o": v_w_ret_o[0], "w_q_b": v_w_q_b[0], "w_kv_b": v_w_kv_b[0],
             "w_mla_o": v_w_mla_o[0], "w_out": v_w_out[0], "w_up": v_w_up[0], "w_down": v_w_down[0]}
    names = list(mats)
    col_sharded = ("w_in", "w_q_b", "w_kv_b", "w_up")
    vw = ret_norm_g.shape[1]
    mla_w = mats["w_mla_o"].shape[0] * N_DEV
    ql, kl = q_a_norm_g.shape[1], kv_a_norm_g.shape[1]
    n_in = mats["w_in"].shape[1] * N_DEV
    qw = (n_in - 2 * vw - ql - kl - ROPE - 2 * d) // 2
    lay = _make_layout(d, vw, qw, ql, kl, mla_w)
    assert lay["n_in"] == n_in
    heads_r, heads_m = lay["ret_heads"], lay["mla_heads"]

    shard16 = {n: mats[n].astype(BF16) for n in names}
    with_in_proj = ("w_ret_o", "w_q_b", "w_kv_b", "w_mla_o", "w_out")
    with_attn = ("w_up", "w_down")
    full = {}

    def keep(group, gathered):
        for n, g in zip(group, gathered):
            full[n] = _cols_to_full(g) if n in col_sharded else g.reshape(-1, g.shape[2])

    keep(("w_in",), _exchange_alone("gather_w_in", _gather_exchange([shard16["w_in"]])))
    w_mine = _w_in_to_mine(full["w_in"], lay)

    c64, s64 = _rope_tables(pos, RET_QK // 2)
    cos_r = jnp.concatenate([c64, c64], axis=1)
    sin_r = jnp.concatenate([-s64, s64], axis=1)
    c32, s32 = _rope_tables(pos, ROPE // 2)
    z32, z64 = jnp.zeros_like(c32), jnp.zeros((s, LANES - ROPE), F32)
    cos_p = jnp.concatenate([c32, c32, z64], axis=1)
    sin_a = jnp.concatenate([-s32, z32, z64], axis=1)
    sin_b = jnp.concatenate([z32, s32, z64], axis=1)
    lg = jnp.log(1.0 - 2.0 ** (-5.0 - jnp.arange(heads_r, dtype=F32)))
    lgs = jnp.broadcast_to(lg[:, None, None], (heads_r, 8, LANES))

    tm = min(256, s)
    blk = min(256, s)
    t_att = min(512, s)

    u = _rms_fwd("norm_mix", xs, norm_mix_g, tm)
    proj, gathered = _mm("in_proj", u, w_mine, "nn", F32,
                         ride=_gather_exchange([shard16[n] for n in with_in_proj]))
    keep(with_in_proj, gathered)
    wq_pad = jnp.pad(full["w_q_b"].reshape(ql, heads_m, NOPE + ROPE),
                     ((0, 0), (0, 0), (0, QPAD - NOPE - ROPE))).reshape(ql, heads_m * QPAD)
    o_ret, states = _ret_fwd(proj, lay, cos_r, sin_r, lgs, blk)
    ry = _ret_post(proj, lay, o_ret, ret_norm_g, tm)
    y_ret = _mm("ret_out", ry, full["w_ret_o"], "nn", F32)
    cqn, ckvn, kpr = _mla_prep(proj, lay, q_a_norm_g, kv_a_norm_g, cos_p, sin_a, sin_b, tm)
    qp = _mm("q_up", cqn, wq_pad, "nn", F32)
    kv = _mm("kv_up", ckvn, full["w_kv_b"], "nn", BF16)
    qf, kf = _attn_prep(qp, kv, kpr, lay, cos_p, sin_a, sin_b, tm)
    o_mla, lse, gathered = _attn_fwd(qf, kf, kv, lay, t_att, ride=_gather_exchange([shard16[n] for n in with_attn]))
    keep(with_attn, gathered)
    y_mla = _mm("mla_out", o_mla, full["w_mla_o"], "nn", F32)
    merged = _merge(proj, lay, y_ret, y_mla, tm)
    mix = _mm("out_proj", merged, full["w_out"], "nn", F32)
    h1, n2 = _rms_res_fwd("norm_mlp", xs, mix, norm_mlp_g, tm)
    z, act = _mm("mlp_up", n2, full["w_up"], "nn", (F32, BF16),
                 epilogue=lambda r: (r, jnp.square(jnp.maximum(r, 0.0))))
    dn = _mm("mlp_down", act, full["w_down"], "nn", F32)
    dh2, g_norm_f, loss_part = _final("loss_head", h1, dn, norm_f_g.reshape(1, d), tgt, tm)

    mx, my, mc = _place()
    c_arr = jnp.reshape(mc, (1,)).astype(jnp.int32)
    chip_arr = jnp.reshape(2 * mx + my, (1,)).astype(jnp.int32)
    sums, from_chips = {}, {}

    def blocks(group, grads):
        return [(_full_to_cols(g) if n in col_sharded else g.reshape((N_DEV,) + mats[n].shape))
                for n, g in zip(group, grads)]

    def pair_sums(group, mine, from_sibling):
        for n, g, r in zip(group, mine, from_sibling):
            sums[n] = _pair_sum("pair_sum_" + n, g, r, c_arr, 256)
        return [sums[n] for n in group]

    dz = _mm("mlp_down_dx", dh2, full["w_down"], "nt", BF16, extras=(z,),
             epilogue=lambda r, zz: (r * (2.0 * jnp.maximum(zz, 0.0)),))
    g_w_down = _mm("mlp_down_dw", act, dh2, "tn", BF16)
    g_w_up = _mm("mlp_up_dw", n2, dz, "tn", BF16)
    mlp_blocks = blocks(with_attn, (g_w_up, g_w_down))
    dn2, got = _mm("mlp_up_dx", dz, full["w_up"], "nt", F32, ride=_sibling_exchange(mlp_blocks))
    mlp_sums = pair_sums(with_attn, mlp_blocks, got)
    dh1, g_norm_mlp = _rms_bwd("norm_mlp_bwd", dn2, h1, norm_mlp_g, dh2, tm)
    d_merged = _mm("out_proj_dx", dh1, full["w_out"], "nt", F32)
    g_w_out = _mm("out_proj_dw", merged, dh1, "tn", BF16)
    dy_ret, dy_mla, dg_ret, dg_mla = _merge_bwd(proj, lay, d_merged, y_ret, y_mla, tm)
    g_w_ret_o = _mm("ret_out_dw", ry, dy_ret, "tn", BF16)
    g_w_mla_o = _mm("mla_out_dw", o_mla, dy_mla, "tn", BF16)
    mixer = ("w_out", "w_ret_o", "w_mla_o")
    mixer_blocks = blocks(mixer, (g_w_out, g_w_ret_o, g_w_mla_o))
    d_ry, got = _mm("ret_out_dx", dy_ret, full["w_ret_o"], "nt", F32, ride=_sibling_exchange(mixer_blocks))
    mixer_sums = pair_sums(mixer, mixer_blocks, got)
    d_omla = _mm("mla_out_dx", dy_mla, full["w_mla_o"], "nt", F32)
    d_oret, d_rg, g_ret_norm = _ret_post_bwd(proj, lay, d_ry, o_ret, ret_norm_g, tm)
    d_rq, d_rk, d_rv = _ret_bwd(proj, lay, cos_r, sin_r, lgs, states, d_oret, blk)
    dqf, delta, got = _attn_bwd_dq(qf, kf, kv, o_mla, lse, d_omla, lay, t_att, ride=_chips_exchange(mlp_sums))
    from_chips.update(zip(with_attn, got))
    dkf, dv, got = _attn_bwd_dkv(qf, kf, kv, lse.reshape(heads_m, 1, s), delta.reshape(heads_m, 1, s), d_omla, lay,
                                 t_att, ride=_chips_exchange(mixer_sums))
    from_chips.update(zip(mixer, got))
    dqp, dkv, d_kpe = _attn_post_bwd(dqf, dkf, dv, lay, cos_p, sin_a, sin_b, tm)
    d_cqn = _mm("q_up_dx", dqp, wq_pad, "nt", F32)
    g_wq_pad = _mm("q_up_dw", cqn, dqp, "tn", BF16)
    d_ckvn = _mm("kv_up_dx", dkv, full["w_kv_b"], "nt", F32)
    g_w_kv_b = _mm("kv_up_dw", ckvn, dkv, "tn", BF16)
    d_cq, d_ckv, g_q_a, g_kv_a = _mla_prep_bwd(proj, lay, d_cqn, d_ckvn, q_a_norm_g, kv_a_norm_g, tm)
    d_by = {"g_ret": dg_ret, "g_mla": dg_mla, "r_v": d_rv, "r_g": d_rg, "r_q": d_rq, "r_k": d_rk, "c_q": d_cq,
            "c_kv": d_ckv, "k_pe": d_kpe}
    parts = [d_by[n] for n in MY_ORDER]
    parts.append(jnp.zeros((s, lay["total"] - sum(p.shape[1] for p in parts)), BF16))
    d_proj = jnp.concatenate(parts, axis=1)
    g_w_mine = _mm("in_proj_dw", u, d_proj, "tn", BF16)
    g_w_q_b = g_wq_pad.reshape(ql, heads_m, QPAD)[:, :, :NOPE + ROPE].reshape(ql, heads_m * (NOPE + ROPE))
    last = ("w_in", "w_q_b", "w_kv_b")
    last_blocks = blocks(last, (_mine_to_w_in(g_w_mine, lay), g_w_q_b, g_w_kv_b))
    last_sums = pair_sums(last, last_blocks, _exchange_alone("grads_to_sibling", _sibling_exchange(last_blocks)))
    du, got = _mm("in_proj_dx", d_proj, w_mine, "nt", F32, ride=_chips_exchange(last_sums))
    from_chips.update(zip(last, got))
    grad_x, g_norm_mix = _rms_bwd("norm_mix_bwd", du, xs, norm_mix_g, dh1, tm)

    upd = {n: _adamw_shard("adamw_" + n, mats[n], mat_m[n], mat_v[n], sums[n], from_chips[n], chip_arr, 256)
           for n in names}

    gains = [("norm_mix_g", norm_mix_g, m_norm_mix_g, v_norm_mix_g, g_norm_mix),
             ("ret_norm_g", ret_norm_g, m_ret_norm_g, v_ret_norm_g, g_ret_norm),
             ("q_a_norm_g", q_a_norm_g, m_q_a_norm_g, v_q_a_norm_g, g_q_a),
             ("kv_a_norm_g", kv_a_norm_g, m_kv_a_norm_g, v_kv_a_norm_g, g_kv_a),
             ("norm_mlp_g", norm_mlp_g, m_norm_mlp_g, v_norm_mlp_g, g_norm_mlp),
             ("norm_f_g", norm_f_g, m_norm_f_g, v_norm_f_g, g_norm_f)]
    n_rows = sum(g[1].size for g in gains) // LANES
    pad_rows = -(-(n_rows + 1) // 8) * 8 - n_rows
    tail = jnp.zeros((pad_rows, LANES), F32)
    part = jnp.concatenate([_pack_rows([g[4] for g in gains]),
                            jnp.broadcast_to(loss_part[:, :1], (1, LANES)), tail[1:]], axis=0)
    packed = [jnp.concatenate([_pack_rows([g[k] for g in gains]), tail], axis=0) for k in (1, 2, 3)]
    g_sm, d_sm, m_sm, v_sm = _small_all_reduce_adam(part, *packed)
    loss = g_sm[n_rows, 0]
    small = {}
    o = 0
    for name, w, _, _, _ in gains:
        r = w.size // LANES
        small[name] = [a[o:o + r].reshape(w.shape) for a in (g_sm, d_sm, m_sm, v_sm)]
        o += r

    order = ["norm_mix_g", "w_in", "ret_norm_g", "w_ret_o", "q_a_norm_g", "w_q_b", "kv_a_norm_g", "w_kv_b", "w_mla_o",
             "w_out", "norm_mlp_g", "w_up", "w_down", "norm_f_g"]
    outs = [loss, grad_x[None]]
    for k in range(4):
        for n in order:
            outs.append(small[n][k] if n in small else upd[n][k][None])
    return tuple(outs)
```

```python
import functools
import math

import jax
import jax.numpy as jnp
from jax import lax
from jax.experimental import pallas as pl
from jax.experimental.pallas import tpu as pltpu

F32 = jnp.float32
BF16 = jnp.bfloat16
MESH = pl.DeviceIdType.MESH

EPS = 1e-6
ROPE_THETA = 10000.0
CHUNK_SHIFT = 6
RET_QK = 128
RET_V = 256
NOPE = 128
ROPE = 64
VHEAD = 128
QPAD = 256
LANES = 128
N_DEV = 8
VMEM_LIMIT = 56 * 1024 * 1024

ADAM_LR = 0.001
ADAM_B1 = 0.9
ADAM_B2 = 0.999
ADAM_EPS = 1e-08
ADAM_WD = 0.01
ADAM_STEP = 10

NN = (((1,), (0,)), ((), ()))
NT = (((1,), (1,)), ((), ()))
TN = (((0,), (0,)), ((), ()))


def _dot(a, b, dims):
    return lax.dot_general(a.astype(BF16), b.astype(BF16), dims, preferred_element_type=F32)


def _tile(dim, pref):
    if dim <= pref:
        return dim
    t = (pref // LANES) * LANES
    while t >= LANES:
        if dim % t == 0:
            return t
        t -= LANES
    raise ValueError(f"no tile for {dim}")


def _params(sem):
    return pltpu.CompilerParams(dimension_semantics=sem, vmem_limit_bytes=VMEM_LIMIT)


def _sig(v):
    return 1.0 / (1.0 + jnp.exp(-v))


def _mm(name, a, b, mode, out_dtypes, *, tm=1024, tn=1024, tk=2048, extras=(), epilogue=None, ride=None,
        b_by_device=False, out_by_device=False):
    if b_by_device:
        b_cols = b.shape[2]
        b_shape = (b.shape[1], N_DEV * b_cols)
    else:
        b_shape = b.shape
    if mode == "nn":
        (m, k), (_, n) = a.shape, b_shape
    elif mode == "nt":
        (m, k), (n, _) = a.shape, b_shape
    else:
        (k, m), (_, n) = a.shape, b_shape
    tm, tn, tk = _tile(m, tm), _tile(n, tn), _tile(k, tk)
    if b_by_device and mode == "nt":
        tk = _tile(b_cols, tk)
    elif b_by_device:
        tn = _tile(b_cols, tn)
    if out_by_device:
        tn = _tile(n // N_DEV, tn)
    nk = k // tk
    dims = {"nn": NN, "nt": NT, "tn": TN}[mode]
    a_spec = (pl.BlockSpec((tk, tm), lambda i, j, kk: (kk, i)) if mode == "tn"
              else pl.BlockSpec((tm, tk), lambda i, j, kk: (i, kk)))
    if b_by_device and mode == "nt":
        per = b_cols // tk
        b_spec = pl.BlockSpec((None, tn, tk), lambda i, j, kk: (kk // per, j, kk % per))
    elif b_by_device:
        per = b_cols // tn
        b_spec = pl.BlockSpec((None, tk, tn), lambda i, j, kk: (j // per, kk, j % per))
    else:
        b_spec = (pl.BlockSpec((tn, tk), lambda i, j, kk: (j, kk)) if mode == "nt"
                  else pl.BlockSpec((tk, tn), lambda i, j, kk: (kk, j)))
    tile_spec = pl.BlockSpec((tm, tn), lambda i, j, kk: (i, j))
    if out_by_device:
        per_out = n // N_DEV // tn
        out_spec = pl.BlockSpec((None, tm, tn), lambda i, j, kk: (j // per_out, i, j % per_out))
        out_dims = (N_DEV, m, n // N_DEV)
    else:
        out_spec, out_dims = tile_spec, (m, n)
    n_ex = len(extras)
    single = not isinstance(out_dtypes, (tuple, list))
    dts = (out_dtypes,) if single else tuple(out_dtypes)

    grid = (m // tm, n // tn, nk)
    r_in, r_out, r_sc = ride.counts() if ride else (0, 0, 0)
    n_acc = 1 if nk > 1 else 0

    def body(a_ref, b_ref, *rest):
        ex, rest = rest[:n_ex], rest[n_ex:]
        ride_in, rest = rest[:r_in], rest[r_in:]
        outs, rest = rest[:len(dts)], rest[len(dts):]
        ride_out, rest = rest[:r_out], rest[r_out:]
        ride_scratch = rest[n_acc:]
        if ride:
            first, mid, last = _steps([pl.program_id(d) for d in range(3)], grid)
            ride.run(ride_in, ride_out, ride_scratch, (first, mid, None))

        def finish(r):
            vals = (r,) if epilogue is None else epilogue(r, *[e[...] for e in ex])
            for o, v in zip(outs, vals):
                o[...] = v.astype(o.dtype)

        part = _dot(a_ref[...], b_ref[...], dims)
        if nk == 1:
            finish(part)
        else:
            acc = rest[0]
            kk = pl.program_id(2)

            @pl.when(kk == 0)
            def _():
                acc[...] = part

            @pl.when(jnp.logical_and(kk > 0, kk < nk - 1))
            def _():
                acc[...] += part

            @pl.when(kk == nk - 1)
            def _():
                finish(acc[...] + part)

        if ride:
            ride.run(ride_in, ride_out, ride_scratch, (None, None, last))

    res = pl.pallas_call(
        body, name=name, grid=grid,
        in_specs=[a_spec, b_spec] + [tile_spec] * n_ex + [ANY] * r_in,
        out_specs=[out_spec] * len(dts) + [ANY] * r_out,
        out_shape=[jax.ShapeDtypeStruct(out_dims, d) for d in dts] + (ride.out_shape if ride else []),
        scratch_shapes=([pltpu.VMEM((tm, tn), F32)] if nk > 1 else []) + (ride.scratch if ride else []),
        compiler_params=_params(("arbitrary",) * 3 if ride else ("parallel", "parallel", "arbitrary")),
    )(a, b, *extras, *(ride.ins if ride else []))
    own = res[0] if single else res[:len(dts)]
    return (own, res[len(dts):]) if ride else own


def _rows(name, body, n_rows, tm, ins, outs, accs=()):
    in_specs, args = [], []
    for t in ins:
        if len(t) == 1:
            in_specs.append(pl.BlockSpec(t[0].shape, lambda i, nd=t[0].ndim: (0,) * nd))
        else:
            in_specs.append(pl.BlockSpec((tm, t[1]), lambda i, cb=t[2]: (i, cb)))
        args.append(t[0])
    out_specs = [pl.BlockSpec((tm, w), lambda i: (i, 0)) for w, _ in outs]
    out_specs += [pl.BlockSpec((r, w), lambda i: (0, 0)) for r, w in accs]
    out_shape = [jax.ShapeDtypeStruct((n_rows, w), d) for w, d in outs]
    out_shape += [jax.ShapeDtypeStruct((r, w), F32) for r, w in accs]
    return pl.pallas_call(
        body, name=name, grid=(n_rows // tm,), in_specs=in_specs, out_specs=out_specs, out_shape=out_shape,
        compiler_params=_params(("arbitrary",) if accs else ("parallel",)),
    )(*args)


def _zero_first(*accs):
    @pl.when(pl.program_id(0) == 0)
    def _():
        for a in accs:
            a[...] = jnp.zeros_like(a)


def _rope64(t, cos, sin):
    return t * cos + pltpu.roll(t, RET_QK // 2, 1) * sin


def _rope32(t, cos, sin_a, sin_b):
    return t * cos + pltpu.roll(t, LANES - ROPE // 2, 1) * sin_a + pltpu.roll(t, ROPE // 2, 1) * sin_b


def _rms_fwd(name, x, g, tm):
    s, d = x.shape

    def body(x_ref, g_ref, u_ref):
        v = x_ref[...]
        r = lax.rsqrt(jnp.mean(v * v, axis=-1, keepdims=True) + EPS)
        u_ref[...] = (v * r * g_ref[...]).astype(BF16)

    return _rows(name, body, s, tm, [(x, d, 0), (g,)], [(d, BF16)])[0]


def _rms_res_fwd(name, x, mix, g, tm):
    s, d = x.shape

    def body(x_ref, m_ref, g_ref, h_ref, u_ref):
        v = x_ref[...] + m_ref[...]
        h_ref[...] = v
        r = lax.rsqrt(jnp.mean(v * v, axis=-1, keepdims=True) + EPS)
        u_ref[...] = (v * r * g_ref[...]).astype(BF16)

    return _rows(name, body, s, tm, [(x, d, 0), (mix, d, 0), (g,)], [(d, F32), (d, BF16)])


def _rms_bwd(name, dy, x, g, dres, tm):
    s, d = x.shape

    def body(dy_ref, x_ref, g_ref, dres_ref, dx_ref, dg_ref):
        _zero_first(dg_ref)
        v, dyv = x_ref[...], dy_ref[...]
        r = lax.rsqrt(jnp.mean(v * v, axis=-1, keepdims=True) + EPS)
        xh = v * r
        dxh = dyv * g_ref[...]
        dx_ref[...] = dres_ref[...] + r * (dxh - xh * jnp.mean(dxh * xh, axis=-1, keepdims=True))
        dg_ref[...] += jnp.sum(dyv * xh, axis=0, keepdims=True)

    return _rows(name, body, s, tm, [(dy, d, 0), (x, d, 0), (g,), (dres, d, 0)], [(d, F32)], [(1, d)])


def _final(name, h1, dn, g, tgt, tm):
    s, d = h1.shape

    def body(h_ref, dn_ref, g_ref, t_ref, dh_ref, dg_ref, loss_ref):
        _zero_first(dg_ref, loss_ref)
        v = h_ref[...] + dn_ref[...]
        r = lax.rsqrt(jnp.mean(v * v, axis=-1, keepdims=True) + EPS)
        xh = v * r
        gv = g_ref[...]
        e = xh * gv - t_ref[...]
        loss_ref[...] += 0.5 * jnp.sum(jnp.mean(e * e, axis=-1, keepdims=True))
        dy = e * (1.0 / d)
        dg_ref[...] += jnp.sum(dy * xh, axis=0, keepdims=True)
        dxh = dy * gv
        dh_ref[...] = r * (dxh - xh * jnp.mean(dxh * xh, axis=-1, keepdims=True))

    return _rows(name, body, s, tm, [(h1, d, 0), (dn, d, 0), (g,), (tgt, d, 0)], [(d, F32)], [(1, d), (1, LANES)])


def _decay_mask(lg, blk):
    n = lax.broadcasted_iota(jnp.int32, (blk, blk), 0)
    m = lax.broadcasted_iota(jnp.int32, (blk, blk), 1)
    w = jnp.exp(lg * jnp.abs(n - m).astype(F32))
    return jnp.where(jnp.right_shift(m, CHUNK_SHIFT) <= jnp.right_shift(n, CHUNK_SHIFT), w, 0.0)


def _decays(lg, blk):
    pos = lax.broadcasted_iota(jnp.int32, (blk, 1), 0).astype(F32)
    return jnp.exp(lg * (pos + 1.0)), jnp.exp(lg * (blk - 1.0 - pos)), jnp.exp(lg * float(blk))


def _ret_fwd(proj, lay, cos, sin, lgs, blk):
    s = proj.shape[0]
    heads = lay["ret_heads"]
    nb = s // blk
    scale = RET_QK ** -0.5

    def body(lg_ref, q_ref, k_ref, v_ref, cos_ref, sin_ref, o_ref, st_ref, state, mask):
        lg = lg_ref[0:1, 0:1]

        @pl.when(pl.program_id(1) == 0)
        def _():
            state[...] = jnp.zeros_like(state)
            mask[...] = _decay_mask(lg, blk)

        a, c, gb = _decays(lg, blk)
        q = _rope64(q_ref[...], cos_ref[...], sin_ref[...])
        k = _rope64(k_ref[...], cos_ref[...], sin_ref[...]) * scale
        v = v_ref[...]
        st = state[...]
        st_ref[...] = st
        sm = _dot(q, k, NT) * mask[...]
        o_ref[...] = _dot(sm, v, NN) + _dot(q * a, st, NN)
        state[...] = st * gb + _dot(k * c, v, TN)

    qb, kb, vb = lay["off"]["r_q"] // RET_QK, lay["off"]["r_k"] // RET_QK, lay["off"]["r_v"] // RET_V
    return pl.pallas_call(
        body, name="ret_fwd", grid=(heads, nb),
        in_specs=[pl.BlockSpec((None, 8, LANES), lambda h, b: (h, 0, 0)),
                  pl.BlockSpec((blk, RET_QK), lambda h, b: (b, qb + h)),
                  pl.BlockSpec((blk, RET_QK), lambda h, b: (b, kb + h)),
                  pl.BlockSpec((blk, RET_V), lambda h, b: (b, vb + h)),
                  pl.BlockSpec((blk, LANES), lambda h, b: (b, 0)),
                  pl.BlockSpec((blk, LANES), lambda h, b: (b, 0))],
        out_specs=[pl.BlockSpec((blk, RET_V), lambda h, b: (b, h)),
                   pl.BlockSpec((None, None, RET_QK, RET_V), lambda h, b: (h, b, 0, 0))],
        out_shape=[jax.ShapeDtypeStruct((s, heads * RET_V), F32),
                   jax.ShapeDtypeStruct((heads, nb, RET_QK, RET_V), F32)],
        scratch_shapes=[pltpu.VMEM((RET_QK, RET_V), F32), pltpu.VMEM((blk, blk), F32)],
        compiler_params=_params(("parallel", "arbitrary")),
    )(lgs, proj, proj, proj, cos, sin)


def _ret_bwd(proj, lay, cos, sin, lgs, states, d_o, blk, ride=None):
    ride_in_specs, ride_ins, ride_out_specs, ride_out_shape, ride_scratch = _ride_args(ride)
    s = proj.shape[0]
    heads = lay["ret_heads"]
    nb = s // blk
    scale = RET_QK ** -0.5

    def body(lg_ref, q_ref, k_ref, v_ref, cos_ref, sin_ref, st_ref, do_ref, dq_ref, dk_ref, dv_ref, dstate, mask):
        lg = lg_ref[0:1, 0:1]

        @pl.when(pl.program_id(1) == 0)
        def _():
            dstate[...] = jnp.zeros_like(dstate)
            mask[...] = _decay_mask(lg, blk)

        a, c, gb = _decays(lg, blk)
        cs, sn = cos_ref[...], sin_ref[...]
        q = _rope64(q_ref[...], cs, sn)
        k = _rope64(k_ref[...], cs, sn) * scale
        v = v_ref[...]
        st = st_ref[...]
        do = do_ref[...]
        dst = dstate[...]
        mk = mask[...]
        sm = _dot(q, k, NT) * mk
        ds = _dot(do, v, NT) * mk
        dq = _dot(ds, k, NN) + _dot(do, st, NT) * a
        dk = _dot(ds, q, TN) + _dot(v, dst, NT) * c
        dv_ref[...] = (_dot(sm, do, TN) + _dot(k * c, dst, NN)).astype(dv_ref.dtype)
        dstate[...] = dst * gb + _dot(q * a, do, TN)
        dq_ref[...] = _rope64(dq, cs, -sn).astype(dq_ref.dtype)
        dk_ref[...] = (_rope64(dk, cs, -sn) * scale).astype(dk_ref.dtype)

    qb, kb, vb = lay["off"]["r_q"] // RET_QK, lay["off"]["r_k"] // RET_QK, lay["off"]["r_v"] // RET_V
    last = nb - 1
    res = pl.pallas_call(
        _with_ride(body, ride, (heads, nb), 0, 8, 3), name="ret_bwd", grid=(heads, nb),
        in_specs=[pl.BlockSpec((None, 8, LANES), lambda h, b: (h, 0, 0)),
                  pl.BlockSpec((blk, RET_QK), lambda h, b: (last - b, qb + h)),
                  pl.BlockSpec((blk, RET_QK), lambda h, b: (last - b, kb + h)),
                  pl.BlockSpec((blk, RET_V), lambda h, b: (last - b, vb + h)),
                  pl.BlockSpec((blk, LANES), lambda h, b: (last - b, 0)),
                  pl.BlockSpec((blk, LANES), lambda h, b: (last - b, 0)),
                  pl.BlockSpec((None, None, RET_QK, RET_V), lambda h, b: (h, last - b, 0, 0)),
                  pl.BlockSpec((blk, RET_V), lambda h, b: (last - b, h))] + ride_in_specs,
        out_specs=[pl.BlockSpec((blk, RET_QK), lambda h, b: (last - b, h)),
                   pl.BlockSpec((blk, RET_QK), lambda h, b: (last - b, h)),
                   pl.BlockSpec((blk, RET_V), lambda h, b: (last - b, h))] + ride_out_specs,
        out_shape=[jax.ShapeDtypeStruct((s, heads * RET_QK), BF16),
                   jax.ShapeDtypeStruct((s, heads * RET_QK), BF16),
                   jax.ShapeDtypeStruct((s, heads * RET_V), BF16)] + ride_out_shape,
        scratch_shapes=[pltpu.VMEM((RET_QK, RET_V), F32), pltpu.VMEM((blk, blk), F32)] + ride_scratch,
        compiler_params=_params(("arbitrary", "arbitrary") if ride else ("parallel", "arbitrary")),
    )(lgs, proj, proj, proj, cos, sin, states, d_o, *ride_ins)
    return (res[0], res[1], res[2], res[3:]) if ride else res


def _ret_post(proj, lay, o, g, tm):
    s, vw = o.shape
    heads = lay["ret_heads"]

    def body(o_ref, rg_ref, g_ref, ry_ref):
        for h in range(heads):
            sl = slice(h * RET_V, (h + 1) * RET_V)
            oh = o_ref[:, sl]
            dlt = oh - jnp.mean(oh, axis=-1, keepdims=True)
            rstd = lax.rsqrt(jnp.mean(dlt * dlt, axis=-1, keepdims=True) + EPS)
            rg = rg_ref[:, sl]
            ry_ref[:, sl] = (dlt * rstd * g_ref[:, sl] * (rg * _sig(rg))).astype(BF16)

    return _rows("ret_post", body, s, tm, [(o, vw, 0), (proj, vw, lay["off"]["r_g"] // vw), (g,)], [(vw, BF16)])[0]


def _ret_post_bwd(proj, lay, d_ry, o, g, tm):
    s, vw = o.shape
    heads = lay["ret_heads"]

    def body(dry_ref, o_ref, rg_ref, g_ref, do_ref, drg_ref, dg_ref):
        _zero_first(dg_ref)
        for h in range(heads):
            sl = slice(h * RET_V, (h + 1) * RET_V)
            oh = o_ref[:, sl]
            dlt = oh - jnp.mean(oh, axis=-1, keepdims=True)
            rstd = lax.rsqrt(jnp.mean(dlt * dlt, axis=-1, keepdims=True) + EPS)
            oh = dlt * rstd
            gv = g_ref[:, sl]
            rg = rg_ref[:, sl]
            sg = _sig(rg)
            dry = dry_ref[:, sl]
            dt = dry * (rg * sg)
            drg_ref[:, sl] = (dry * (oh * gv) * (sg * (1.0 + rg * (1.0 - sg)))).astype(BF16)
            dg_ref[:, sl] += jnp.sum(dt * oh, axis=0, keepdims=True)
            doh = dt * gv
            do_ref[:, sl] = rstd * (doh - jnp.mean(doh, axis=-1, keepdims=True)
                                    - oh * jnp.mean(doh * oh, axis=-1, keepdims=True))

    return _rows("ret_post_bwd", body, s, tm,
                 [(d_ry, vw, 0), (o, vw, 0), (proj, vw, lay["off"]["r_g"] // vw), (g,)],
                 [(vw, F32), (vw, BF16)], [(1, vw)])


def _mla_prep(proj, lay, gq, gkv, cos, sin_a, sin_b, tm):
    s = proj.shape[0]
    ql, kl = lay["q_lora"], lay["kv_lora"]

    def body(cq_ref, ckv_ref, kpe_ref, gq_ref, gkv_ref, cos_ref, sa_ref, sb_ref, cqn_ref, ckvn_ref, kpr_ref):
        for src, gref, dst in ((cq_ref, gq_ref, cqn_ref), (ckv_ref, gkv_ref, ckvn_ref)):
            v = src[...]
            r = lax.rsqrt(jnp.mean(v * v, axis=-1, keepdims=True) + EPS)
            dst[...] = (v * r * gref[...]).astype(BF16)
        kpr_ref[...] = _rope32(kpe_ref[...], cos_ref[...], sa_ref[...], sb_ref[...]).astype(BF16)

    off = lay["off"]
    return _rows("mla_prep", body, s, tm,
                 [(proj, ql, off["c_q"] // ql), (proj, kl, off["c_kv"] // kl), (proj, LANES, off["k_pe"] // LANES),
                  (gq,), (gkv,), (cos, LANES, 0), (sin_a, LANES, 0), (sin_b, LANES, 0)],
                 [(ql, BF16), (kl, BF16), (LANES, BF16)])


def _mla_prep_bwd(proj, lay, d_cqn, d_ckvn, gq, gkv, tm):
    s = proj.shape[0]
    ql, kl = lay["q_lora"], lay["kv_lora"]

    def body(dq_ref, dkv_ref, cq_ref, ckv_ref, gq_ref, gkv_ref, dcq_ref, dckv_ref, dgq_ref, dgkv_ref):
        _zero_first(dgq_ref, dgkv_ref)
        for dref, src, gref, dst, dg in ((dq_ref, cq_ref, gq_ref, dcq_ref, dgq_ref),
                                         (dkv_ref, ckv_ref, gkv_ref, dckv_ref, dgkv_ref)):
            v, dy = src[...], dref[...]
            r = lax.rsqrt(jnp.mean(v * v, axis=-1, keepdims=True) + EPS)
            xh = v * r
            dxh = dy * gref[...]
            dst[...] = (r * (dxh - xh * jnp.mean(dxh * xh, axis=-1, keepdims=True))).astype(BF16)
            dg[...] += jnp.sum(dy * xh, axis=0, keepdims=True)

    off = lay["off"]
    return _rows("mla_prep_bwd", body, s, tm,
                 [(d_cqn, ql, 0), (d_ckvn, kl, 0), (proj, ql, off["c_q"] // ql), (proj, kl, off["c_kv"] // kl),
                  (gq,), (gkv,)],
                 [(ql, BF16), (kl, BF16)], [(1, ql), (1, kl)])


def _attn_prep(qp, kv, kpr, lay, cos, sin_a, sin_b, tm):
    s = qp.shape[0]
    heads = lay["mla_heads"]
    w = heads * QPAD

    def body(qp_ref, kv_ref, kpr_ref, cos_ref, sa_ref, sb_ref, qf_ref, kf_ref):
        qs = (NOPE + ROPE) ** -0.5 * math.log2(math.e)
        cs, sa, sb = cos_ref[...] * qs, sa_ref[...] * qs, sb_ref[...] * qs
        kp = kpr_ref[...]
        for h in range(heads):
            lo, hi = h * QPAD, h * QPAD + NOPE
            qf_ref[:, lo:hi] = (qp_ref[:, lo:hi] * qs).astype(BF16)
            qf_ref[:, hi:hi + LANES] = _rope32(qp_ref[:, hi:hi + LANES], cs, sa, sb).astype(BF16)
            kf_ref[:, lo:hi] = kv_ref[:, lo:hi]
            kf_ref[:, hi:hi + LANES] = kp

    return _rows("attn_prep", body, s, tm,
                 [(qp, w, 0), (kv, w, 0), (kpr, LANES, 0), (cos, LANES, 0), (sin_a, LANES, 0), (sin_b, LANES, 0)],
                 [(w, BF16), (w, BF16)])


def _attn_post_bwd(dqf, dkf, dv, lay, cos, sin_a, sin_b, tm):
    s = dqf.shape[0]
    heads = lay["mla_heads"]
    w = heads * QPAD

    def body(dqf_ref, dkf_ref, dv_ref, cos_ref, sa_ref, sb_ref, dqp_ref, dkv_ref, dkpe_ref):
        cs, sa, sb = cos_ref[...], -sa_ref[...], -sb_ref[...]
        kpe = jnp.zeros((tm, LANES), F32)
        for h in range(heads):
            lo, hi = h * QPAD, h * QPAD + NOPE
            dqp_ref[:, lo:hi] = dqf_ref[:, lo:hi].astype(BF16)
            dqp_ref[:, hi:hi + LANES] = _rope32(dqf_ref[:, hi:hi + LANES], cs, sa, sb).astype(BF16)
            dkv_ref[:, lo:hi] = dkf_ref[:, lo:hi].astype(BF16)
            dkv_ref[:, hi:hi + LANES] = dv_ref[:, h * VHEAD:(h + 1) * VHEAD].astype(BF16)
            kpe = kpe + dkf_ref[:, hi:hi + LANES]
        dkpe_ref[...] = _rope32(kpe, cs, sa, sb).astype(BF16)

    return _rows("attn_post_bwd", body, s, tm,
                 [(dqf, w, 0), (dkf, w, 0), (dv, heads * VHEAD, 0), (cos, LANES, 0), (sin_a, LANES, 0),
                  (sin_b, LANES, 0)],
                 [(w, BF16), (w, BF16), (LANES, BF16)])


def _diag_mask(t, keys_on_rows=False):
    row = lax.broadcasted_iota(jnp.int32, (t, t), 0)
    col = lax.broadcasted_iota(jnp.int32, (t, t), 1)
    key, query = (row, col) if keys_on_rows else (col, row)
    return jnp.right_shift(key, CHUNK_SHIFT) <= jnp.right_shift(query, CHUNK_SHIFT)


def _tile_pairs(nt, by_key):
    if by_key:
        pairs = [(i, j) for j in range(nt) for i in range(j, nt)]
    else:
        pairs = [(i, j) for i in range(nt) for j in range(i + 1)]
    return (jnp.asarray([p[0] for p in pairs], jnp.int32), jnp.asarray([p[1] for p in pairs], jnp.int32))


def _head_block(heads):
    return 2 if heads % 2 == 0 else 1


def _attn_fwd(qf, kf, kv, lay, t, ride=None):
    s = qf.shape[0]
    heads = lay["mla_heads"]
    hb = _head_block(heads)
    nt = s // t
    qi, kj = _tile_pairs(nt, False)
    grid = (heads // hb, int(qi.shape[0]))
    ride_in_specs, ride_ins, ride_out_specs, ride_out_shape, ride_scratch = _ride_args(ride)

    def body(qi_ref, kj_ref, q_ref, k_ref, kv_ref, o_ref, lse_ref, m_s, l_s, acc):
        p = pl.program_id(1)
        i, j = qi_ref[p], kj_ref[p]

        @pl.when(j == 0)
        def _():
            m_s[...] = jnp.full_like(m_s, -jnp.inf)
            l_s[...] = jnp.zeros_like(l_s)
            acc[...] = jnp.zeros_like(acc)

        def step(diagonal):
            ones = jnp.ones((t, LANES), BF16)
            scores = [_dot(q_ref[:, hh * QPAD:(hh + 1) * QPAD], k_ref[:, hh * QPAD:(hh + 1) * QPAD], NT)
                      for hh in range(hb)]
            for hh in range(hb):
                sc = scores[hh]
                if diagonal:
                    sc = jnp.where(_diag_mask(t), sc, -jnp.inf)
                cols = [sc[:, c * LANES:(c + 1) * LANES] for c in range(t // LANES)]
                m_old = m_s[hh]
                m_new = jnp.maximum(m_old, jnp.max(functools.reduce(jnp.maximum, cols), axis=-1, keepdims=True))
                alpha = jnp.exp2(m_old - m_new)
                pr = jnp.concatenate([jnp.exp2(c - m_new).astype(BF16) for c in cols], axis=1)
                pv = _dot(pr, jnp.concatenate([kv_ref[:, hh * QPAD + NOPE:(hh + 1) * QPAD], ones], axis=1), NN)
                l_new = alpha * l_s[hh] + pv[:, VHEAD:]
                a_new = alpha * acc[hh] + pv[:, :VHEAD]
                if diagonal:
                    o_ref[:, hh * VHEAD:(hh + 1) * VHEAD] = a_new / l_new
                    lse_ref[hh] = jnp.transpose(m_new + jnp.log2(l_new))[:1]
                else:
                    m_s[hh], l_s[hh], acc[hh] = m_new, l_new, a_new

        pl.when(j < i)(functools.partial(step, False))
        pl.when(j == i)(functools.partial(step, True))

    res = pl.pallas_call(
        _with_ride(body, ride, grid, 2, 3, 2), name="attn_fwd",
        grid_spec=pltpu.PrefetchScalarGridSpec(
            num_scalar_prefetch=2, grid=grid,
            in_specs=[pl.BlockSpec((t, hb * QPAD), lambda h, p, qi, kj: (qi[p], h)),
                      pl.BlockSpec((t, hb * QPAD), lambda h, p, qi, kj: (kj[p], h)),
                      pl.BlockSpec((t, hb * QPAD), lambda h, p, qi, kj: (kj[p], h))] + ride_in_specs,
            out_specs=[pl.BlockSpec((t, hb * VHEAD), lambda h, p, qi, kj: (qi[p], h)),
                       pl.BlockSpec((hb, 1, t), lambda h, p, qi, kj: (h, 0, qi[p]))] + ride_out_specs,
            scratch_shapes=[pltpu.VMEM((hb, t, LANES), F32), pltpu.VMEM((hb, t, LANES), F32),
                            pltpu.VMEM((hb, t, VHEAD), F32)] + ride_scratch),
        out_shape=[jax.ShapeDtypeStruct((s, heads * VHEAD), F32),
                   jax.ShapeDtypeStruct((heads, 1, s), F32)] + ride_out_shape,
        compiler_params=_params(("arbitrary", "arbitrary") if ride else ("parallel", "arbitrary")),
    )(qi, kj, qf, kf, kv, *ride_ins)
    return (res[0], res[1], res[2:]) if ride else res


def _attn_delta(d_o, o, lay, tm):
    s = o.shape[0]
    heads = lay["mla_heads"]

    def body(do_ref, o_ref, dl_ref):
        for h in range(heads):
            sl = slice(h * VHEAD, (h + 1) * VHEAD)
            dl_ref[h] = jnp.sum(jnp.transpose(do_ref[:, sl] * o_ref[:, sl]), axis=0, keepdims=True)

    tile = pl.BlockSpec((tm, heads * VHEAD), lambda i: (i, 0))
    return pl.pallas_call(
        body, name="attn_delta", grid=(s // tm,), in_specs=[tile, tile],
        out_specs=pl.BlockSpec((heads, 1, tm), lambda i: (0, 0, i)),
        out_shape=jax.ShapeDtypeStruct((heads, 1, s), F32),
        compiler_params=_params(("parallel",)),
    )(d_o, o)


def _attn_bwd(qf, kf, kv, lse, delta, d_o, lay, t, ride=None):
    s = qf.shape[0]
    heads = lay["mla_heads"]
    hb = _head_block(heads)
    nt = s // t
    scale = (NOPE + ROPE) ** -0.5
    qi, kj = _tile_pairs(nt, True)
    grid = (heads // hb, int(qi.shape[0]))
    ride_in_specs, ride_ins, ride_out_specs, ride_out_shape, ride_scratch = _ride_args(ride)

    def body(qi_ref, kj_ref, q_ref, k_ref, kv_ref, lse_ref, dl_ref, do_ref, dq_ref, dk_ref, dv_ref, dk_acc, dv_acc):
        p = pl.program_id(1)
        i, j = qi_ref[p], kj_ref[p]
        rows = pl.ds(pl.multiple_of(i * t, t), t)

        @pl.when(p == 0)
        def _():
            dq_ref[...] = jnp.zeros_like(dq_ref)

        def step(diagonal):
            for hh in range(hb):
                lo = hh * QPAD
                q, k = q_ref[:, lo:lo + QPAD], k_ref[:, lo:lo + QPAD]
                do = do_ref[:, hh * VHEAD:(hh + 1) * VHEAD]
                pr = jnp.exp2(_dot(k, q, NT) - lse_ref[hh])
                if diagonal:
                    pr = jnp.where(_diag_mask(t, keys_on_rows=True), pr, 0.0)
                dv_part = _dot(pr, do, NN)
                ds = (pr * (_dot(kv_ref[:, lo + NOPE:lo + QPAD], do, NT) - dl_ref[hh])).astype(BF16)
                dk_part = _dot(ds, q, NN)
                dq_ref[rows, lo:lo + QPAD] += _dot(ds, k, TN) * scale
                if diagonal:
                    dk_acc[hh], dv_acc[hh] = dk_part, dv_part
                else:
                    dk_acc[hh] += dk_part
                    dv_acc[hh] += dv_part

        pl.when(i > j)(functools.partial(step, False))
        pl.when(i == j)(functools.partial(step, True))

        @pl.when(i == nt - 1)
        def _():
            for hh in range(hb):
                dk_ref[:, hh * QPAD:(hh + 1) * QPAD] = dk_acc[hh] * math.log(2.0)
                dv_ref[:, hh * VHEAD:(hh + 1) * VHEAD] = dv_acc[hh]

    res = pl.pallas_call(
        _with_ride(body, ride, grid, 2, 6, 3), name="attn_bwd",
        grid_spec=pltpu.PrefetchScalarGridSpec(
            num_scalar_prefetch=2, grid=grid,
            in_specs=[pl.BlockSpec((t, hb * QPAD), lambda h, p, qi, kj: (qi[p], h)),
                      pl.BlockSpec((t, hb * QPAD), lambda h, p, qi, kj: (kj[p], h)),
                      pl.BlockSpec((t, hb * QPAD), lambda h, p, qi, kj: (kj[p], h)),
                      pl.BlockSpec((hb, 1, t), lambda h, p, qi, kj: (h, 0, qi[p])),
                      pl.BlockSpec((hb, 1, t), lambda h, p, qi, kj: (h, 0, qi[p])),
                      pl.BlockSpec((t, hb * VHEAD), lambda h, p, qi, kj: (qi[p], h))] + ride_in_specs,
            out_specs=[pl.BlockSpec((s, hb * QPAD), lambda h, p, qi, kj: (0, h)),
                       pl.BlockSpec((t, hb * QPAD), lambda h, p, qi, kj: (kj[p], h)),
                       pl.BlockSpec((t, hb * VHEAD), lambda h, p, qi, kj: (kj[p], h))] + ride_out_specs,
            scratch_shapes=[pltpu.VMEM((hb, t, QPAD), F32), pltpu.VMEM((hb, t, VHEAD), F32)] + ride_scratch),
        out_shape=[jax.ShapeDtypeStruct((s, heads * QPAD), F32),
                   jax.ShapeDtypeStruct((s, heads * QPAD), F32),
                   jax.ShapeDtypeStruct((s, heads * VHEAD), F32)] + ride_out_shape,
        compiler_params=_params(("arbitrary", "arbitrary") if ride else ("parallel", "arbitrary")),
    )(qi, kj, qf, kf, kv, lse, delta, d_o, *ride_ins)
    return (res[0], res[1], res[2], res[3:]) if ride else res


def _merge(proj, lay, y_ret, y_mla, tm):
    s, d = y_ret.shape

    def body(gr_ref, gm_ref, yr_ref, ym_ref, out_ref):
        out_ref[...] = (_sig(gr_ref[...]) * yr_ref[...] + _sig(gm_ref[...]) * ym_ref[...]).astype(BF16)

    off = lay["off"]
    return _rows("merge", body, s, tm,
                 [(proj, d, off["g_ret"] // d), (proj, d, off["g_mla"] // d), (y_ret, d, 0), (y_mla, d, 0)],
                 [(d, BF16)])[0]


def _merge_bwd(proj, lay, d_merged, y_ret, y_mla, tm):
    s, d = y_ret.shape

    def body(dm_ref, gr_ref, gm_ref, yr_ref, ym_ref, dyr_ref, dym_ref, dgr_ref, dgm_ref):
        dm = dm_ref[...]
        for g_ref, y_ref, dy_ref, dg_ref in ((gr_ref, yr_ref, dyr_ref, dgr_ref), (gm_ref, ym_ref, dym_ref, dgm_ref)):
            sg = _sig(g_ref[...])
            dy_ref[...] = (dm * sg).astype(BF16)
            dg_ref[...] = (dm * y_ref[...] * (sg * (1.0 - sg))).astype(BF16)

    off = lay["off"]
    return _rows("merge_bwd", body, s, tm,
                 [(d_merged, d, 0), (proj, d, off["g_ret"] // d), (proj, d, off["g_mla"] // d), (y_ret, d, 0),
                  (y_mla, d, 0)],
                 [(d, BF16)] * 4)


ANY = pl.BlockSpec(memory_space=pl.ANY)


def _place():
    return lax.axis_index("x"), lax.axis_index("y"), lax.axis_index("c")


def _other_chips(x, y):
    return [(1 - x, y), (x, 1 - y), (1 - x, 1 - y)]


class _Exchange:
    def __init__(self, ins, out_shape, scratch, phases):
        self.ins, self.out_shape, self.scratch, self.phases = list(ins), list(out_shape), list(scratch), phases

    def counts(self):
        return len(self.ins), len(self.out_shape), len(self.scratch)

    def run(self, r_in, r_out, r_scratch, conds):
        for cond, phase in zip(conds, self.phases):
            if phase is not None and cond is not None:
                pl.when(cond)(functools.partial(phase, r_in, r_out, r_scratch))


def _steps(ids, sizes):
    lin, total = 0, 1
    for i, n in zip(ids, sizes):
        lin, total = lin * n + i, total * n
    return lin == 0, lin == total // 2, lin == total - 1


def _ride_args(ride):
    if ride is None:
        return [], [], [], [], []
    n_in, n_out, _ = ride.counts()
    return [ANY] * n_in, ride.ins, [ANY] * n_out, ride.out_shape, ride.scratch


def _with_ride(body, ride, grid, n_prefetch, n_in, n_out):
    if ride is None:
        return body
    r_in, r_out, r_sc = ride.counts()

    def hosted(*refs):
        cuts = (n_prefetch, n_in, r_in, n_out, r_out)
        parts, pos = [], 0
        for n in cuts:
            parts.append(refs[pos:pos + n])
            pos += n
        pre, ins, ride_in, outs, ride_out = parts
        scratch, ride_scratch = refs[pos:len(refs) - r_sc], refs[len(refs) - r_sc:]
        first, mid, last = _steps([pl.program_id(d) for d in range(len(grid))], grid)
        ride.run(ride_in, ride_out, ride_scratch, (first, mid, None))
        body(*pre, *ins, *outs, *scratch)
        ride.run(ride_in, ride_out, ride_scratch, (None, None, last))

    return hosted


def _exchange_alone(name, ex):
    n_in, n_out, _ = ex.counts()

    def body(*refs):
        for phase in ex.phases:
            if phase is not None:
                phase(refs[:n_in], refs[n_in:n_in + n_out], refs[n_in + n_out:])

    return pl.pallas_call(
        body, name=name, in_specs=[ANY] * n_in, out_specs=[ANY] * n_out, out_shape=ex.out_shape,
        scratch_shapes=ex.scratch)(*ex.ins)


def _gather_exchange(shards):
    nw = len(shards)

    def parts(ins, outs, sems):
        send_sems, recv_sems, local_sems = sems
        x, y, c = _place()

        def slot(px, py, pc):
            return 4 * px + 2 * py + pc

        def copy(w, k, rows, to, src=None):
            return pltpu.make_async_remote_copy(
                src_ref=rows if src is None else src, dst_ref=rows, send_sem=send_sems.at[w, k],
                recv_sem=recv_sems.at[w, k], device_id=to, device_id_type=MESH)

        def plan(w, mine):
            side = c if mine else 1 - c
            half = shards[w].shape[0] // 2
            whole = lambda px, py: outs[w].at[slot(px, py, side)]
            top = lambda px, py: outs[w].at[slot(px, py, side), pl.ds(0, half)]
            bottom = lambda px, py: outs[w].at[slot(px, py, side), pl.ds(half, half)]
            xn, yn, sib = (1 - x, y, side), (x, 1 - y, side), (x, y, 1 - side)
            own = ins[w] if mine else None
            return [copy(w, 0, whole(x, y), sib, own), copy(w, 1, whole(x, y), xn, own),
                    copy(w, 2, whole(x, y), yn, own), copy(w, 3, top(1 - x, y), yn), copy(w, 4, bottom(x, 1 - y), xn),
                    copy(w, 5, whole(1 - x, y), sib), copy(w, 6, whole(x, 1 - y), sib),
                    copy(w, 7, top(1 - x, 1 - y), sib), copy(w, 8, bottom(1 - x, 1 - y), sib)]

        def arrivals(w):
            half = shards[w].shape[0] // 2
            at = lambda px, py, *rows: outs[w].at[(slot(px, py, c),) + rows]
            return {1: copy(w, 1, at(1 - x, y), (x, y, c)), 2: copy(w, 2, at(x, 1 - y), (x, y, c)),
                    3: copy(w, 3, at(1 - x, 1 - y, pl.ds(0, half)), (x, y, c)),
                    4: copy(w, 4, at(1 - x, 1 - y, pl.ds(half, half)), (x, y, c))}

        local = [pltpu.make_async_copy(ins[w], outs[w].at[slot(x, y, c)], local_sems.at[w]) for w in range(nw)]
        return plan, arrivals, local

    def start(ins, outs, sems):
        plan, _, local = parts(ins, outs, sems)
        for cp in local:
            cp.start()
        for w in range(nw):
            for k in (0, 1, 2):
                plan(w, True)[k].start()

    def middle(ins, outs, sems):
        plan, arrivals, _ = parts(ins, outs, sems)
        for landed, onward in ((1, (3, 5)), (2, (4, 6))):
            for w in range(nw):
                arrivals(w)[landed].wait_recv()
                for k in onward:
                    plan(w, True)[k].start()

    def finish(ins, outs, sems):
        plan, arrivals, local = parts(ins, outs, sems)
        for landed, onward in ((3, 7), (4, 8)):
            for w in range(nw):
                arrivals(w)[landed].wait_recv()
                plan(w, True)[onward].start()
        for w in range(nw):
            from_sibling = plan(w, False)
            for k in (0, 5, 6, 7, 8):
                from_sibling[k].wait_recv()
            for cp in plan(w, True):
                cp.wait_send()
        for cp in local:
            cp.wait()

    return _Exchange(
        shards, [jax.ShapeDtypeStruct((N_DEV,) + s.shape, s.dtype) for s in shards],
        [pltpu.SemaphoreType.DMA((nw, 9)), pltpu.SemaphoreType.DMA((nw, 9)), pltpu.SemaphoreType.DMA((nw,))],
        (start, middle, finish))


def _sibling_exchange(grads):
    nw = len(grads)

    def copies(ins, outs, sems):
        x, y, c = _place()
        return [pltpu.make_async_remote_copy(
            src_ref=ins[w].at[2 * p + (1 - c)], dst_ref=outs[w].at[p], send_sem=sems[0].at[w, p],
            recv_sem=sems[1].at[w, p], device_id=(x, y, 1 - c), device_id_type=MESH)
            for w in range(nw) for p in range(4)]

    def start(ins, outs, sems):
        for cp in copies(ins, outs, sems):
            cp.start()

    def finish(ins, outs, sems):
        for cp in copies(ins, outs, sems):
            cp.wait()

    return _Exchange(grads, [jax.ShapeDtypeStruct((4,) + g.shape[1:], g.dtype) for g in grads],
                     [pltpu.SemaphoreType.DMA((nw, 4)), pltpu.SemaphoreType.DMA((nw, 4))], (start, None, finish))


def _chips_exchange(sums):
    nw = len(sums)

    def copies(ins, outs, sems):
        x, y, c = _place()
        return [pltpu.make_async_remote_copy(
            src_ref=ins[w].at[2 * px + py], dst_ref=outs[w].at[k], send_sem=sems[0].at[w, k],
            recv_sem=sems[1].at[w, k], device_id=(px, py, c), device_id_type=MESH)
            for w in range(nw) for k, (px, py) in enumerate(_other_chips(x, y))]

    def start(ins, outs, sems):
        for cp in copies(ins, outs, sems):
            cp.start()

    def finish(ins, outs, sems):
        for cp in copies(ins, outs, sems):
            cp.wait()

    return _Exchange(sums, [jax.ShapeDtypeStruct((3,) + g.shape[1:], g.dtype) for g in sums],
                     [pltpu.SemaphoreType.DMA((nw, 3)), pltpu.SemaphoreType.DMA((nw, 3))], (start, None, finish))


def _pair_sum(name, g, got, c_arr, tr):
    _, rows, cols = g.shape
    tr = _tile_rows(rows, tr)

    def body(c_ref, a_ref, b_ref, o_ref):
        o_ref[...] = (a_ref[...].astype(F32) + b_ref[...].astype(F32)).astype(BF16)

    return pl.pallas_call(
        body, name=name,
        grid_spec=pltpu.PrefetchScalarGridSpec(
            num_scalar_prefetch=1, grid=(4, rows // tr),
            in_specs=[pl.BlockSpec((None, tr, cols), lambda p, r, cr: (2 * p + cr[0], r, 0)),
                      pl.BlockSpec((None, tr, cols), lambda p, r, cr: (p, r, 0))],
            out_specs=pl.BlockSpec((None, tr, cols), lambda p, r, cr: (p, r, 0))),
        out_shape=jax.ShapeDtypeStruct((4, rows, cols), BF16),
        compiler_params=_params(("parallel", "parallel")),
    )(c_arr, g, got)


def _tile_rows(rows, pref):
    t = min(rows, pref)
    while rows % t or t % 8:
        t -= 1
    return t


def _adam(w, g, m, v):
    m = ADAM_B1 * m + (1.0 - ADAM_B1) * g
    v = ADAM_B2 * v + (1.0 - ADAM_B2) * (g * g)
    m_hat = m / (1.0 - ADAM_B1 ** ADAM_STEP)
    v_hat = v / (1.0 - ADAM_B2 ** ADAM_STEP)
    return -ADAM_LR * (m_hat / (jnp.sqrt(v_hat) + ADAM_EPS) + ADAM_WD * w), m, v


def _adamw_shard(name, w, m, v, sums, got, chip_arr, tr):
    rows, cols = w.shape
    tr = _tile_rows(rows, tr)

    def body(p_ref, w_ref, m_ref, v_ref, s_ref, r_ref, g_ref, d_ref, nm_ref, nv_ref):
        g = s_ref[...].astype(F32)
        for k in range(3):
            g = g + r_ref[k].astype(F32)
        g_ref[...] = g
        d_ref[...], nm_ref[...], nv_ref[...] = _adam(w_ref[...], g, m_ref[...], v_ref[...])

    tile = pl.BlockSpec((tr, cols), lambda r, pr: (r, 0))
    return pl.pallas_call(
        body, name=name,
        grid_spec=pltpu.PrefetchScalarGridSpec(
            num_scalar_prefetch=1, grid=(rows // tr,),
            in_specs=[tile, tile, tile,
                      pl.BlockSpec((None, tr, cols), lambda r, pr: (pr[0], r, 0)),
                      pl.BlockSpec((3, tr, cols), lambda r, pr: (0, r, 0))],
            out_specs=[tile] * 4),
        out_shape=[jax.ShapeDtypeStruct((rows, cols), F32)] * 4,
        compiler_params=_params(("parallel",)),
    )(chip_arr, w, m, v, sums, got)


def _small_all_reduce_adam(part, w, m, v):
    rows = part.shape[0]

    def body(p_ref, w_ref, m_ref, v_ref, g_ref, d_ref, nm_ref, nv_ref, buf, send_sems, recv_sems):
        x, y, c = _place()
        me = 4 * x + 2 * y + c
        buf[me] = p_ref[...]
        peers = [(x, y, 1 - c)] + [(px, py, pc) for px, py in _other_chips(x, y) for pc in (c, 1 - c)]
        copies = []
        for k, peer in enumerate(peers):
            cp = pltpu.make_async_remote_copy(
                src_ref=buf.at[me], dst_ref=buf.at[me], send_sem=send_sems.at[k], recv_sem=recv_sems.at[k],
                device_id=peer, device_id_type=MESH)
            cp.start()
            copies.append(cp)
        for cp in copies:
            cp.wait()
        g = buf[0]
        for k in range(1, N_DEV):
            g = g + buf[k]
        g_ref[...] = g
        d_ref[...], nm_ref[...], nv_ref[...] = _adam(w_ref[...], g, m_ref[...], v_ref[...])

    vm = pl.BlockSpec(memory_space=pltpu.VMEM)
    return pl.pallas_call(
        body, name="gains_all_reduce_adamw",
        in_specs=[vm] * 4, out_specs=[vm] * 4,
        out_shape=[jax.ShapeDtypeStruct((rows, LANES), F32)] * 4,
        scratch_shapes=[pltpu.VMEM((N_DEV, rows, LANES), F32), pltpu.SemaphoreType.DMA((7,)),
                        pltpu.SemaphoreType.DMA((7,))],
        compiler_params=pltpu.CompilerParams(has_side_effects=True),
    )(part, w, m, v)


IN_ORDER = ("r_q", "r_k", "r_v", "r_g", "c_q", "c_kv", "k_pe", "g_ret", "g_mla")
MY_ORDER = ("g_ret", "g_mla", "r_v", "r_g", "r_q", "r_k", "c_q", "c_kv", "k_pe")


def _make_layout(d, vw, qw, ql, kl, mla_w):
    width = {"r_q": qw, "r_k": qw, "r_v": vw, "r_g": vw, "c_q": ql, "c_kv": kl, "k_pe": ROPE, "g_ret": d, "g_mla": d}
    src, o = {}, 0
    for n in IN_ORDER:
        src[n] = o
        o += width[n]
    off, o = {}, 0
    for n in MY_ORDER:
        off[n] = o
        o += LANES if n == "k_pe" else width[n]
    total = -(-o // 256) * 256
    for n, blk in (("g_ret", d), ("g_mla", d), ("r_v", vw), ("r_g", vw), ("r_q", RET_QK), ("r_k", RET_QK),
                   ("r_v", RET_V), ("c_q", ql), ("c_kv", kl), ("k_pe", LANES)):
        assert off[n] % blk == 0
    return {"width": width, "src": src, "off": off, "total": total, "n_in": sum(width.values()),
            "ret_heads": vw // RET_V, "mla_heads": mla_w // VHEAD, "q_lora": ql, "kv_lora": kl}


def _cols_to_full(g):
    n, r, c = g.shape
    return jnp.transpose(g, (1, 0, 2)).reshape(r, n * c)


def _full_to_cols(w):
    r, c = w.shape
    return jnp.transpose(w.reshape(r, N_DEV, c // N_DEV), (1, 0, 2))


def _w_in_to_mine(w, lay):
    parts = []
    for n in MY_ORDER:
        seg = w[:, lay["src"][n]:lay["src"][n] + lay["width"][n]]
        if n == "k_pe":
            seg = jnp.pad(seg, ((0, 0), (0, LANES - ROPE)))
        parts.append(seg)
    cur = sum(p.shape[1] for p in parts)
    parts.append(jnp.zeros((w.shape[0], lay["total"] - cur), w.dtype))
    return jnp.concatenate(parts, axis=1)


def _mine_to_w_in(w, lay):
    return jnp.concatenate([w[:, lay["off"][n]:lay["off"][n] + lay["width"][n]] for n in IN_ORDER], axis=1)


def _rope_tables(positions, half):
    inv = ROPE_THETA ** (-jnp.arange(half, dtype=F32) / half)
    ang = positions.astype(F32)[:, None] * inv
    return jnp.cos(ang), jnp.sin(ang)


def _pack_rows(vs):
    return jnp.concatenate([v.reshape(-1, LANES) for v in vs], axis=0)


def kernel(x, positions, norm_mix_g, w_in, ret_norm_g, w_ret_o, q_a_norm_g, w_q_b, kv_a_norm_g, w_kv_b, w_mla_o, w_out, norm_mlp_g, w_up, w_down, norm_f_g, loss_target, m_norm_mix_g, m_w_in, m_ret_norm_g, m_w_ret_o, m_q_a_norm_g, m_w_q_b, m_kv_a_norm_g, m_w_kv_b, m_w_mla_o, m_w_out, m_norm_mlp_g, m_w_up, m_w_down, m_norm_f_g, v_norm_mix_g, v_w_in, v_ret_norm_g, v_w_ret_o, v_q_a_norm_g, v_w_q_b, v_kv_a_norm_g, v_w_kv_b, v_w_mla_o, v_w_out, v_norm_mlp_g, v_w_up, v_w_down, v_norm_f_g):
    xs, tgt, pos = x[0], loss_target[0], positions[0]
    s, d = xs.shape
    mats = {"w_in": w_in[0], "w_ret_o": w_ret_o[0], "w_q_b": w_q_b[0], "w_kv_b": w_kv_b[0], "w_mla_o": w_mla_o[0],
            "w_out": w_out[0], "w_up": w_up[0], "w_down": w_down[0]}
    mat_m = {"w_in": m_w_in[0], "w_ret_o": m_w_ret_o[0], "w_q_b": m_w_q_b[0], "w_kv_b": m_w_kv_b[0],
             "w_mla_o": m_w_mla_o[0], "w_out": m_w_out[0], "w_up": m_w_up[0], "w_down": m_w_down[0]}
    mat_v = {"w_in": v_w_in[0], "w_ret_o": v_w_ret_o[0], "w_q_b": v_w_q_b[0], "w_kv_b": v_w_kv_b[0],
             "w_mla_o": v_w_mla_o[0], "w_out": v_w_out[0], "w_up": v_w_up[0], "w_down": v_w_down[0]}
    names = list(mats)
    col_sharded = ("w_in", "w_q_b", "w_kv_b", "w_up")
    vw = ret_norm_g.shape[1]
    mla_w = mats["w_mla_o"].shape[0] * N_DEV
    ql, kl = q_a_norm_g.shape[1], kv_a_norm_g.shape[1]
    n_in = mats["w_in"].shape[1] * N_DEV
    qw = (n_in - 2 * vw - ql - kl - ROPE - 2 * d) // 2
    lay = _make_layout(d, vw, qw, ql, kl, mla_w)
    assert lay["n_in"] == n_in
    heads_r, heads_m = lay["ret_heads"], lay["mla_heads"]

    shard16 = {n: mats[n].astype(BF16) for n in names}
    with_in_proj = ("w_ret_o", "w_q_b", "w_kv_b", "w_mla_o", "w_out")
    with_attn = ("w_up", "w_down")
    by_device = ("w_up", "w_kv_b")
    full = {}

    def keep(group, gathered):
        for n, g in zip(group, gathered):
            if n not in by_device:
                g = _cols_to_full(g) if n in col_sharded else g.reshape(-1, g.shape[2])
            full[n] = g

    keep(("w_in",), _exchange_alone("gather_w_in", _gather_exchange([shard16["w_in"]])))
    w_mine = _w_in_to_mine(full["w_in"], lay)

    c64, s64 = _rope_tables(pos, RET_QK // 2)
    cos_r = jnp.concatenate([c64, c64], axis=1)
    sin_r = jnp.concatenate([-s64, s64], axis=1)
    c32, s32 = _rope_tables(pos, ROPE // 2)
    z32, z64 = jnp.zeros_like(c32), jnp.zeros((s, LANES - ROPE), F32)
    cos_p = jnp.concatenate([c32, c32, z64], axis=1)
    sin_a = jnp.concatenate([-s32, z32, z64], axis=1)
    sin_b = jnp.concatenate([z32, s32, z64], axis=1)
    lg = jnp.log(1.0 - 2.0 ** (-5.0 - jnp.arange(heads_r, dtype=F32)))
    lgs = jnp.broadcast_to(lg[:, None, None], (heads_r, 8, LANES))

    tm = min(256, s)
    blk = min(256, s)
    t_att = min(512, s)

    u = _rms_fwd("norm_mix", xs, norm_mix_g, tm)
    proj, gathered = _mm("in_proj", u, w_mine, "nn", F32,
                         ride=_gather_exchange([shard16[n] for n in with_in_proj]))
    keep(with_in_proj, gathered)
    wq_pad = jnp.pad(full["w_q_b"].reshape(ql, heads_m, NOPE + ROPE),
                     ((0, 0), (0, 0), (0, QPAD - NOPE - ROPE))).reshape(ql, heads_m * QPAD)
    o_ret, states = _ret_fwd(proj, lay, cos_r, sin_r, lgs, blk)
    ry = _ret_post(proj, lay, o_ret, ret_norm_g, tm)
    y_ret = _mm("ret_out", ry, full["w_ret_o"], "nn", F32)
    cqn, ckvn, kpr = _mla_prep(proj, lay, q_a_norm_g, kv_a_norm_g, cos_p, sin_a, sin_b, tm)
    qp = _mm("q_up", cqn, wq_pad, "nn", F32)
    kv = _mm("kv_up", ckvn, full["w_kv_b"], "nn", BF16, b_by_device=True)
    qf, kf = _attn_prep(qp, kv, kpr, lay, cos_p, sin_a, sin_b, tm)
    o_mla, lse, gathered = _attn_fwd(qf, kf, kv, lay, t_att, ride=_gather_exchange([shard16[n] for n in with_attn]))
    keep(with_attn, gathered)
    y_mla = _mm("mla_out", o_mla, full["w_mla_o"], "nn", F32)
    merged = _merge(proj, lay, y_ret, y_mla, tm)
    mix = _mm("out_proj", merged, full["w_out"], "nn", F32)
    h1, n2 = _rms_res_fwd("norm_mlp", xs, mix, norm_mlp_g, tm)
    z, act = _mm("mlp_up", n2, full["w_up"], "nn", (F32, BF16), b_by_device=True,
                 epilogue=lambda r: (r, jnp.square(jnp.maximum(r, 0.0))))
    dn = _mm("mlp_down", act, full["w_down"], "nn", F32)
    dh2, g_norm_f, loss_part = _final("loss_head", h1, dn, norm_f_g.reshape(1, d), tgt, tm)

    mx, my, mc = _place()
    c_arr = jnp.reshape(mc, (1,)).astype(jnp.int32)
    chip_arr = jnp.reshape(2 * mx + my, (1,)).astype(jnp.int32)
    sums, from_chips = {}, {}

    def blocks(group, grads):
        return [g if n in by_device else (_full_to_cols(g) if n in col_sharded else g.reshape((N_DEV,) + mats[n].shape))
                for n, g in zip(group, grads)]

    def pair_sums(group, mine, from_sibling):
        for n, g, r in zip(group, mine, from_sibling):
            sums[n] = _pair_sum("pair_sum_" + n, g, r, c_arr, 256)
        return [sums[n] for n in group]

    dz = _mm("mlp_down_dx", dh2, full["w_down"], "nt", BF16, extras=(z,),
             epilogue=lambda r, zz: (r * (2.0 * jnp.maximum(zz, 0.0)),))
    g_w_down = _mm("mlp_down_dw", act, dh2, "tn", BF16)
    g_w_up = _mm("mlp_up_dw", n2, dz, "tn", BF16, out_by_device=True)
    mlp_blocks = blocks(with_attn, (g_w_up, g_w_down))
    dn2, got = _mm("mlp_up_dx", dz, full["w_up"], "nt", F32, b_by_device=True, ride=_sibling_exchange(mlp_blocks))
    mlp_sums = pair_sums(with_attn, mlp_blocks, got)
    dh1, g_norm_mlp = _rms_bwd("norm_mlp_bwd", dn2, h1, norm_mlp_g, dh2, tm)
    d_merged = _mm("out_proj_dx", dh1, full["w_out"], "nt", F32)
    g_w_out = _mm("out_proj_dw", merged, dh1, "tn", BF16)
    dy_ret, dy_mla, dg_ret, dg_mla = _merge_bwd(proj, lay, d_merged, y_ret, y_mla, tm)
    g_w_ret_o = _mm("ret_out_dw", ry, dy_ret, "tn", BF16)
    g_w_mla_o = _mm("mla_out_dw", o_mla, dy_mla, "tn", BF16)
    mixer = ("w_out", "w_ret_o", "w_mla_o")
    mixer_blocks = blocks(mixer, (g_w_out, g_w_ret_o, g_w_mla_o))
    d_ry, got = _mm("ret_out_dx", dy_ret, full["w_ret_o"], "nt", F32, ride=_sibling_exchange(mixer_blocks))
    mixer_sums = pair_sums(mixer, mixer_blocks, got)
    d_omla = _mm("mla_out_dx", dy_mla, full["w_mla_o"], "nt", F32)
    d_oret, d_rg, g_ret_norm = _ret_post_bwd(proj, lay, d_ry, o_ret, ret_norm_g, tm)
    d_rq, d_rk, d_rv, got = _ret_bwd(proj, lay, cos_r, sin_r, lgs, states, d_oret, blk,
                                     ride=_chips_exchange(mixer_sums))
    from_chips.update(zip(mixer, got))
    delta = _attn_delta(d_omla, o_mla, lay, t_att)
    dqf, dkf, dv, got = _attn_bwd(qf, kf, kv, lse, delta, d_omla, lay, t_att, ride=_chips_exchange(mlp_sums))
    from_chips.update(zip(with_attn, got))
    dqp, dkv, d_kpe = _attn_post_bwd(dqf, dkf, dv, lay, cos_p, sin_a, sin_b, tm)
    d_cqn = _mm("q_up_dx", dqp, wq_pad, "nt", F32)
    g_wq_pad = _mm("q_up_dw", cqn, dqp, "tn", BF16)
    d_ckvn = _mm("kv_up_dx", dkv, full["w_kv_b"], "nt", F32, b_by_device=True)
    g_w_kv_b = _mm("kv_up_dw", ckvn, dkv, "tn", BF16, out_by_device=True)
    d_cq, d_ckv, g_q_a, g_kv_a = _mla_prep_bwd(proj, lay, d_cqn, d_ckvn, q_a_norm_g, kv_a_norm_g, tm)
    d_by = {"g_ret": dg_ret, "g_mla": dg_mla, "r_v": d_rv, "r_g": d_rg, "r_q": d_rq, "r_k": d_rk, "c_q": d_cq,
            "c_kv": d_ckv, "k_pe": d_kpe}
    parts = [d_by[n] for n in MY_ORDER]
    parts.append(jnp.zeros((s, lay["total"] - sum(p.shape[1] for p in parts)), BF16))
    d_proj = jnp.concatenate(parts, axis=1)
    g_w_mine = _mm("in_proj_dw", u, d_proj, "tn", BF16)
    g_w_q_b = g_wq_pad.reshape(ql, heads_m, QPAD)[:, :, :NOPE + ROPE].reshape(ql, heads_m * (NOPE + ROPE))
    last = ("w_in", "w_q_b", "w_kv_b")
    last_blocks = blocks(last, (_mine_to_w_in(g_w_mine, lay), g_w_q_b, g_w_kv_b))
    last_sums = pair_sums(last, last_blocks, _exchange_alone("grads_to_sibling", _sibling_exchange(last_blocks)))
    du, got = _mm("in_proj_dx", d_proj, w_mine, "nt", F32, ride=_chips_exchange(last_sums))
    from_chips.update(zip(last, got))
    grad_x, g_norm_mix = _rms_bwd("norm_mix_bwd", du, xs, norm_mix_g, dh1, tm)

    upd = {n: _adamw_shard("adamw_" + n, mats[n], mat_m[n], mat_v[n], sums[n], from_chips[n], chip_arr, 256)
           for n in names}

    gains = [("norm_mix_g", norm_mix_g, m_norm_mix_g, v_norm_mix_g, g_norm_mix),
             ("ret_norm_g", ret_norm_g, m_ret_norm_g, v_ret_norm_g, g_ret_norm),
             ("q_a_norm_g", q_a_norm_g, m_q_a_norm_g, v_q_a_norm_g, g_q_a),
             ("kv_a_norm_g", kv_a_norm_g, m_kv_a_norm_g, v_kv_a_norm_g, g_kv_a),
             ("norm_mlp_g", norm_mlp_g, m_norm_mlp_g, v_norm_mlp_g, g_norm_mlp),
             ("norm_f_g", norm_f_g, m_norm_f_g, v_norm_f_g, g_norm_f)]
    n_rows = sum(g[1].size for g in gains) // LANES
    pad_rows = -(-(n_rows + 1) // 8) * 8 - n_rows
    tail = jnp.zeros((pad_rows, LANES), F32)
    part = jnp.concatenate([_pack_rows([g[4] for g in gains]),
                            jnp.broadcast_to(loss_part[:, :1], (1, LANES)), tail[1:]], axis=0)
    packed = [jnp.concatenate([_pack_rows([g[k] for g in gains]), tail], axis=0) for k in (1, 2, 3)]
    g_sm, d_sm, m_sm, v_sm = _small_all_reduce_adam(part, *packed)
    loss = g_sm[n_rows, 0]
    small = {}
    o = 0
    for name, w, _, _, _ in gains:
        r = w.size // LANES
        small[name] = [a[o:o + r].reshape(w.shape) for a in (g_sm, d_sm, m_sm, v_sm)]
        o += r

    order = ["norm_mix_g", "w_in", "ret_norm_g", "w_ret_o", "q_a_norm_g", "w_q_b", "kv_a_norm_g", "w_kv_b", "w_mla_o",
             "w_out", "norm_mlp_g", "w_up", "w_down", "norm_f_g"]
    outs = [loss, grad_x[None]]
    for k in range(4):
        for n in order:
            outs.append(small[n][k] if n in small else upd[n][k][None])
    return tuple(outs)
```

```python
import functools
import math

import jax
import jax.numpy as jnp
from jax import lax
from jax.experimental import pallas as pl
from jax.experimental.pallas import tpu as pltpu

F32 = jnp.float32
BF16 = jnp.bfloat16
MESH = pl.DeviceIdType.MESH

EPS = 1e-6
ROPE_THETA = 10000.0
CHUNK_SHIFT = 6
RET_QK = 128
RET_V = 256
NOPE = 128
ROPE = 64
VHEAD = 128
QPAD = 256
LANES = 128
N_DEV = 8
VMEM_LIMIT = 56 * 1024 * 1024

ADAM_LR = 0.001
ADAM_B1 = 0.9
ADAM_B2 = 0.999
ADAM_EPS = 1e-08
ADAM_WD = 0.01
ADAM_STEP = 10

NN = (((1,), (0,)), ((), ()))
NT = (((1,), (1,)), ((), ()))
TN = (((0,), (0,)), ((), ()))


def _dot(a, b, dims):
    return lax.dot_general(a.astype(BF16), b.astype(BF16), dims, preferred_element_type=F32)


def _tile(dim, pref):
    if dim <= pref:
        return dim
    t = (pref // LANES) * LANES
    while t >= LANES:
        if dim % t == 0:
            return t
        t -= LANES
    raise ValueError(f"no tile for {dim}")


def _params(sem):
    return pltpu.CompilerParams(dimension_semantics=sem, vmem_limit_bytes=VMEM_LIMIT)


def _sig(v):
    return 1.0 / (1.0 + jnp.exp(-v))


def _mm(name, a, b, mode, out_dtypes, *, tm=1024, tn=1024, tk=2048, extras=(), epilogue=None, ride=None,
        b_by_device=False, out_by_device=False):
    if b_by_device:
        b_cols = b.shape[2]
        b_shape = (b.shape[1], N_DEV * b_cols)
    else:
        b_shape = b.shape
    if mode == "nn":
        (m, k), (_, n) = a.shape, b_shape
    elif mode == "nt":
        (m, k), (n, _) = a.shape, b_shape
    else:
        (k, m), (_, n) = a.shape, b_shape
    tm, tn, tk = _tile(m, tm), _tile(n, tn), _tile(k, tk)
    if b_by_device and mode == "nt":
        tk = _tile(b_cols, tk)
    elif b_by_device:
        tn = _tile(b_cols, tn)
    if out_by_device:
        tn = _tile(n // N_DEV, tn)
    nk = k // tk
    dims = {"nn": NN, "nt": NT, "tn": TN}[mode]
    a_spec = (pl.BlockSpec((tk, tm), lambda i, j, kk: (kk, i)) if mode == "tn"
              else pl.BlockSpec((tm, tk), lambda i, j, kk: (i, kk)))
    if b_by_device and mode == "nt":
        per = b_cols // tk
        b_spec = pl.BlockSpec((None, tn, tk), lambda i, j, kk: (kk // per, j, kk % per))
    elif b_by_device:
        per = b_cols // tn
        b_spec = pl.BlockSpec((None, tk, tn), lambda i, j, kk: (j // per, kk, j % per))
    else:
        b_spec = (pl.BlockSpec((tn, tk), lambda i, j, kk: (j, kk)) if mode == "nt"
                  else pl.BlockSpec((tk, tn), lambda i, j, kk: (kk, j)))
    tile_spec = pl.BlockSpec((tm, tn), lambda i, j, kk: (i, j))
    if out_by_device:
        per_out = n // N_DEV // tn
        out_spec = pl.BlockSpec((None, tm, tn), lambda i, j, kk: (j // per_out, i, j % per_out))
        out_dims = (N_DEV, m, n // N_DEV)
    else:
        out_spec, out_dims = tile_spec, (m, n)
    n_ex = len(extras)
    single = not isinstance(out_dtypes, (tuple, list))
    dts = (out_dtypes,) if single else tuple(out_dtypes)

    grid = (m // tm, n // tn, nk)
    r_in, r_out, r_sc = ride.counts() if ride else (0, 0, 0)
    n_acc = 1 if nk > 1 else 0

    def body(a_ref, b_ref, *rest):
        ex, rest = rest[:n_ex], rest[n_ex:]
        ride_in, rest = rest[:r_in], rest[r_in:]
        outs, rest = rest[:len(dts)], rest[len(dts):]
        ride_out, rest = rest[:r_out], rest[r_out:]
        ride_scratch = rest[n_acc:]
        if ride:
            first, mid, last = _steps([pl.program_id(d) for d in range(3)], grid)
            ride.run(ride_in, ride_out, ride_scratch, (first, mid, None))

        def finish(r):
            vals = (r,) if epilogue is None else epilogue(r, *[e[...] for e in ex])
            for o, v in zip(outs, vals):
                o[...] = v.astype(o.dtype)

        part = _dot(a_ref[...], b_ref[...], dims)
        if nk == 1:
            finish(part)
        else:
            acc = rest[0]
            kk = pl.program_id(2)

            @pl.when(kk == 0)
            def _():
                acc[...] = part

            @pl.when(jnp.logical_and(kk > 0, kk < nk - 1))
            def _():
                acc[...] += part

            @pl.when(kk == nk - 1)
            def _():
                finish(acc[...] + part)

        if ride:
            ride.run(ride_in, ride_out, ride_scratch, (None, None, last))

    res = pl.pallas_call(
        body, name=name, grid=grid,
        in_specs=[a_spec, b_spec] + [tile_spec] * n_ex + [ANY] * r_in,
        out_specs=[out_spec] * len(dts) + [ANY] * r_out,
        out_shape=[jax.ShapeDtypeStruct(out_dims, d) for d in dts] + (ride.out_shape if ride else []),
        scratch_shapes=([pltpu.VMEM((tm, tn), F32)] if nk > 1 else []) + (ride.scratch if ride else []),
        compiler_params=_params(("arbitrary",) * 3 if ride else ("parallel", "parallel", "arbitrary")),
    )(a, b, *extras, *(ride.ins if ride else []))
    own = res[0] if single else res[:len(dts)]
    return (own, res[len(dts):]) if ride else own


def _rows(name, body, n_rows, tm, ins, outs, accs=(), into=None):
    in_specs, args = [], []
    for t in ins:
        if len(t) == 1:
            in_specs.append(pl.BlockSpec(t[0].shape, lambda i, nd=t[0].ndim: (0,) * nd))
        else:
            in_specs.append(pl.BlockSpec((tm, t[1]), lambda i, cb=t[2]: (i, cb)))
        args.append(t[0])
    outs = [(o + (o[0], 0))[:4] for o in outs]
    out_specs = [pl.BlockSpec((tm, w), lambda i, cb=cb: (i, cb)) for w, _, _, cb in outs]
    out_shape = [jax.ShapeDtypeStruct((n_rows, total), d) for _, d, total, _ in outs]
    aliases, kernel = {}, body
    if into is not None:
        arr, w, cb = into
        in_specs.append(ANY)
        args.append(arr)
        out_specs.append(pl.BlockSpec((tm, w), lambda i: (i, cb)))
        out_shape.append(jax.ShapeDtypeStruct(arr.shape, arr.dtype))
        aliases = {len(ins): len(outs)}
        n_in = len(ins)

        def kernel(*refs):
            body(*refs[:n_in], *refs[n_in + 1:])

    out_specs += [pl.BlockSpec((r, w), lambda i: (0, 0)) for r, w in accs]
    out_shape += [jax.ShapeDtypeStruct((r, w), F32) for r, w in accs]
    return pl.pallas_call(
        kernel, name=name, grid=(n_rows // tm,), in_specs=in_specs, out_specs=out_specs, out_shape=out_shape,
        input_output_aliases=aliases, compiler_params=_params(("arbitrary",) if accs else ("parallel",)),
    )(*args)


def _zero_first(*accs):
    @pl.when(pl.program_id(0) == 0)
    def _():
        for a in accs:
            a[...] = jnp.zeros_like(a)


def _rope64(t, cos, sin):
    return t * cos + pltpu.roll(t, RET_QK // 2, 1) * sin


def _rope32(t, cos, sin_a, sin_b):
    return t * cos + pltpu.roll(t, LANES - ROPE // 2, 1) * sin_a + pltpu.roll(t, ROPE // 2, 1) * sin_b


def _rms_fwd(name, x, g, tm):
    s, d = x.shape

    def body(x_ref, g_ref, u_ref):
        v = x_ref[...]
        r = lax.rsqrt(jnp.mean(v * v, axis=-1, keepdims=True) + EPS)
        u_ref[...] = (v * r * g_ref[...]).astype(BF16)

    return _rows(name, body, s, tm, [(x, d, 0), (g,)], [(d, BF16)])[0]


def _rms_res_fwd(name, x, mix, g, tm):
    s, d = x.shape

    def body(x_ref, m_ref, g_ref, h_ref, u_ref):
        v = x_ref[...] + m_ref[...]
        h_ref[...] = v
        r = lax.rsqrt(jnp.mean(v * v, axis=-1, keepdims=True) + EPS)
        u_ref[...] = (v * r * g_ref[...]).astype(BF16)

    return _rows(name, body, s, tm, [(x, d, 0), (mix, d, 0), (g,)], [(d, F32), (d, BF16)])


def _rms_bwd(name, dy, x, g, dres, tm):
    s, d = x.shape

    def body(dy_ref, x_ref, g_ref, dres_ref, dx_ref, dg_ref):
        _zero_first(dg_ref)
        v, dyv = x_ref[...], dy_ref[...]
        r = lax.rsqrt(jnp.mean(v * v, axis=-1, keepdims=True) + EPS)
        xh = v * r
        dxh = dyv * g_ref[...]
        dx_ref[...] = dres_ref[...] + r * (dxh - xh * jnp.mean(dxh * xh, axis=-1, keepdims=True))
        dg_ref[...] += jnp.sum(dyv * xh, axis=0, keepdims=True)

    return _rows(name, body, s, tm, [(dy, d, 0), (x, d, 0), (g,), (dres, d, 0)], [(d, F32)], [(1, d)])


def _final(name, h1, dn, g, tgt, tm):
    s, d = h1.shape

    def body(h_ref, dn_ref, g_ref, t_ref, dh_ref, dg_ref, loss_ref):
        _zero_first(dg_ref, loss_ref)
        v = h_ref[...] + dn_ref[...]
        r = lax.rsqrt(jnp.mean(v * v, axis=-1, keepdims=True) + EPS)
        xh = v * r
        gv = g_ref[...]
        e = xh * gv - t_ref[...]
        loss_ref[...] += 0.5 * jnp.sum(jnp.mean(e * e, axis=-1, keepdims=True))
        dy = e * (1.0 / d)
        dg_ref[...] += jnp.sum(dy * xh, axis=0, keepdims=True)
        dxh = dy * gv
        dh_ref[...] = r * (dxh - xh * jnp.mean(dxh * xh, axis=-1, keepdims=True))

    return _rows(name, body, s, tm, [(h1, d, 0), (dn, d, 0), (g,), (tgt, d, 0)], [(d, F32)], [(1, d), (1, LANES)])


def _decay_mask(lg, blk):
    n = lax.broadcasted_iota(jnp.int32, (blk, blk), 0)
    m = lax.broadcasted_iota(jnp.int32, (blk, blk), 1)
    w = jnp.exp(lg * jnp.abs(n - m).astype(F32))
    return jnp.where(jnp.right_shift(m, CHUNK_SHIFT) <= jnp.right_shift(n, CHUNK_SHIFT), w, 0.0)


def _decays(lg, blk):
    pos = lax.broadcasted_iota(jnp.int32, (blk, 1), 0).astype(F32)
    return jnp.exp(lg * (pos + 1.0)), jnp.exp(lg * (blk - 1.0 - pos)), jnp.exp(lg * float(blk))


def _ret_fwd(proj, lay, cos, sin, lgs, blk, ride=None):
    s = proj.shape[0]
    heads = lay["ret_heads"]
    nb = s // blk
    scale = RET_QK ** -0.5
    ride_in_specs, ride_ins, ride_out_specs, ride_out_shape, ride_scratch = _ride_args(ride)

    def body(lg_ref, qkv_ref, cos_ref, sin_ref, o_ref, st_ref, state, mask):
        lg = lg_ref[0:1, 0:1]

        @pl.when(pl.program_id(1) == 0)
        def _():
            state[...] = jnp.zeros_like(state)
            mask[...] = _decay_mask(lg, blk)

        a, c, gb = _decays(lg, blk)
        q = _rope64(qkv_ref[:, :RET_QK], cos_ref[...], sin_ref[...])
        k = _rope64(qkv_ref[:, RET_QK:2 * RET_QK], cos_ref[...], sin_ref[...]) * scale
        v = qkv_ref[:, 2 * RET_QK:]
        st = state[...]
        st_ref[...] = st
        sm = _dot(q, k, NT) * mask[...]
        o_ref[...] = _dot(sm, v, NN) + _dot(q * a, st, NN)
        state[...] = st * gb + _dot(k * c, v, TN)

    first = lay["off"]["heads"] // RET_HEAD
    res = pl.pallas_call(
        _with_ride(body, ride, (heads, nb), 0, 4, 2), name="ret_fwd", grid=(heads, nb),
        in_specs=[pl.BlockSpec((None, 8, LANES), lambda h, b: (h, 0, 0)),
                  pl.BlockSpec((blk, RET_HEAD), lambda h, b: (b, first + h)),
                  pl.BlockSpec((blk, LANES), lambda h, b: (b, 0)),
                  pl.BlockSpec((blk, LANES), lambda h, b: (b, 0))] + ride_in_specs,
        out_specs=[pl.BlockSpec((blk, RET_V), lambda h, b: (b, h)),
                   pl.BlockSpec((None, None, RET_QK, RET_V), lambda h, b: (h, b, 0, 0))] + ride_out_specs,
        out_shape=[jax.ShapeDtypeStruct((s, heads * RET_V), F32),
                   jax.ShapeDtypeStruct((heads, nb, RET_QK, RET_V), F32)] + ride_out_shape,
        scratch_shapes=[pltpu.VMEM((RET_QK, RET_V), F32), pltpu.VMEM((blk, blk), F32)] + ride_scratch,
        compiler_params=_params(("arbitrary", "arbitrary") if ride else ("parallel", "arbitrary")),
    )(lgs, proj, cos, sin, *ride_ins)
    return (res[0], res[1], res[2:]) if ride else res


def _ret_bwd(proj, lay, cos, sin, lgs, states, d_o, d_proj, blk, ride=None):
    ride_in_specs, ride_ins, ride_out_specs, ride_out_shape, ride_scratch = _ride_args(ride)
    s = proj.shape[0]
    heads = lay["ret_heads"]
    nb = s // blk
    scale = RET_QK ** -0.5

    def body(lg_ref, qkv_ref, cos_ref, sin_ref, st_ref, do_ref, _, dqkv_ref, dstate, mask):
        lg = lg_ref[0:1, 0:1]

        @pl.when(pl.program_id(1) == 0)
        def _():
            dstate[...] = jnp.zeros_like(dstate)
            mask[...] = _decay_mask(lg, blk)

        a, c, gb = _decays(lg, blk)
        cs, sn = cos_ref[...], sin_ref[...]
        q = _rope64(qkv_ref[:, :RET_QK], cs, sn)
        k = _rope64(qkv_ref[:, RET_QK:2 * RET_QK], cs, sn) * scale
        v = qkv_ref[:, 2 * RET_QK:]
        st = st_ref[...]
        do = do_ref[...]
        dst = dstate[...]
        mk = mask[...]
        sm = _dot(q, k, NT) * mk
        ds = _dot(do, v, NT) * mk
        dq = _dot(ds, k, NN) + _dot(do, st, NT) * a
        dk = _dot(ds, q, TN) + _dot(v, dst, NT) * c
        dqkv_ref[:, 2 * RET_QK:] = (_dot(sm, do, TN) + _dot(k * c, dst, NN)).astype(BF16)
        dstate[...] = dst * gb + _dot(q * a, do, TN)
        dqkv_ref[:, :RET_QK] = _rope64(dq, cs, -sn).astype(BF16)
        dqkv_ref[:, RET_QK:2 * RET_QK] = (_rope64(dk, cs, -sn) * scale).astype(BF16)

    first = lay["off"]["heads"] // RET_HEAD
    last = nb - 1
    res = pl.pallas_call(
        _with_ride(body, ride, (heads, nb), 0, 7, 1), name="ret_bwd", grid=(heads, nb),
        in_specs=[pl.BlockSpec((None, 8, LANES), lambda h, b: (h, 0, 0)),
                  pl.BlockSpec((blk, RET_HEAD), lambda h, b: (last - b, first + h)),
                  pl.BlockSpec((blk, LANES), lambda h, b: (last - b, 0)),
                  pl.BlockSpec((blk, LANES), lambda h, b: (last - b, 0)),
                  pl.BlockSpec((None, None, RET_QK, RET_V), lambda h, b: (h, last - b, 0, 0)),
                  pl.BlockSpec((blk, RET_V), lambda h, b: (last - b, h)), ANY] + ride_in_specs,
        out_specs=[pl.BlockSpec((blk, RET_HEAD), lambda h, b: (last - b, first + h))] + ride_out_specs,
        out_shape=[jax.ShapeDtypeStruct(d_proj.shape, d_proj.dtype)] + ride_out_shape,
        scratch_shapes=[pltpu.VMEM((RET_QK, RET_V), F32), pltpu.VMEM((blk, blk), F32)] + ride_scratch,
        input_output_aliases={6: 0},
        compiler_params=_params(("arbitrary", "arbitrary") if ride else ("parallel", "arbitrary")),
    )(lgs, proj, cos, sin, states, d_o, d_proj, *ride_ins)
    return (res[0], res[1:]) if ride else res[0]


def _ret_post(proj, lay, o, g, tm):
    s, vw = o.shape
    heads = lay["ret_heads"]

    def body(o_ref, rg_ref, g_ref, ry_ref):
        for h in range(heads):
            sl = slice(h * RET_V, (h + 1) * RET_V)
            oh = o_ref[:, sl]
            dlt = oh - jnp.mean(oh, axis=-1, keepdims=True)
            rstd = lax.rsqrt(jnp.mean(dlt * dlt, axis=-1, keepdims=True) + EPS)
            rg = rg_ref[:, sl]
            ry_ref[:, sl] = (dlt * rstd * g_ref[:, sl] * (rg * _sig(rg))).astype(BF16)

    return _rows("ret_post", body, s, tm, [(o, vw, 0), (proj, vw, lay["off"]["r_g"] // vw), (g,)], [(vw, BF16)])[0]


def _ret_post_bwd(proj, lay, d_ry, o, g, d_proj, tm):
    s, vw = o.shape
    heads = lay["ret_heads"]

    def body(dry_ref, o_ref, rg_ref, g_ref, do_ref, drg_ref, dg_ref):
        _zero_first(dg_ref)
        for h in range(heads):
            sl = slice(h * RET_V, (h + 1) * RET_V)
            oh = o_ref[:, sl]
            dlt = oh - jnp.mean(oh, axis=-1, keepdims=True)
            rstd = lax.rsqrt(jnp.mean(dlt * dlt, axis=-1, keepdims=True) + EPS)
            oh = dlt * rstd
            gv = g_ref[:, sl]
            rg = rg_ref[:, sl]
            sg = _sig(rg)
            dry = dry_ref[:, sl]
            dt = dry * (rg * sg)
            drg_ref[:, sl] = (dry * (oh * gv) * (sg * (1.0 + rg * (1.0 - sg)))).astype(BF16)
            dg_ref[:, sl] += jnp.sum(dt * oh, axis=0, keepdims=True)
            doh = dt * gv
            do_ref[:, sl] = rstd * (doh - jnp.mean(doh, axis=-1, keepdims=True)
                                    - oh * jnp.mean(doh * oh, axis=-1, keepdims=True))

    return _rows("ret_post_bwd", body, s, tm,
                 [(d_ry, vw, 0), (o, vw, 0), (proj, vw, lay["off"]["r_g"] // vw), (g,)],
                 [(vw, F32)], [(1, vw)], into=(d_proj, vw, lay["off"]["r_g"] // vw))


def _mla_prep(proj, lay, gq, gkv, cos, sin_a, sin_b, tm):
    s = proj.shape[0]
    ql, kl = lay["q_lora"], lay["kv_lora"]

    def body(cq_ref, ckv_ref, kpe_ref, gq_ref, gkv_ref, cos_ref, sa_ref, sb_ref, cqn_ref, ckvn_ref, kpr_ref):
        for src, gref, dst in ((cq_ref, gq_ref, cqn_ref), (ckv_ref, gkv_ref, ckvn_ref)):
            v = src[...]
            r = lax.rsqrt(jnp.mean(v * v, axis=-1, keepdims=True) + EPS)
            dst[...] = (v * r * gref[...]).astype(BF16)
        kpr_ref[...] = _rope32(kpe_ref[...], cos_ref[...], sa_ref[...], sb_ref[...]).astype(BF16)

    off = lay["off"]
    return _rows("mla_prep", body, s, tm,
                 [(proj, ql, off["c_q"] // ql), (proj, kl, off["c_kv"] // kl), (proj, LANES, off["k_pe"] // LANES),
                  (gq,), (gkv,), (cos, LANES, 0), (sin_a, LANES, 0), (sin_b, LANES, 0)],
                 [(ql, BF16), (kl, BF16), (LANES, BF16)])


def _mla_prep_bwd(proj, lay, d_cqn, d_ckvn, gq, gkv, d_proj, tm):
    s = proj.shape[0]
    ql, kl = lay["q_lora"], lay["kv_lora"]

    def body(dq_ref, dkv_ref, cq_ref, ckv_ref, gq_ref, gkv_ref, dc_ref, dgq_ref, dgkv_ref):
        _zero_first(dgq_ref, dgkv_ref)
        for dref, src, gref, cols, dg in ((dq_ref, cq_ref, gq_ref, slice(0, ql), dgq_ref),
                                          (dkv_ref, ckv_ref, gkv_ref, slice(ql, ql + kl), dgkv_ref)):
            v, dy = src[...], dref[...]
            r = lax.rsqrt(jnp.mean(v * v, axis=-1, keepdims=True) + EPS)
            xh = v * r
            dxh = dy * gref[...]
            dc_ref[:, cols] = (r * (dxh - xh * jnp.mean(dxh * xh, axis=-1, keepdims=True))).astype(BF16)
            dg[...] += jnp.sum(dy * xh, axis=0, keepdims=True)

    off = lay["off"]
    return _rows("mla_prep_bwd", body, s, tm,
                 [(d_cqn, ql, 0), (d_ckvn, kl, 0), (proj, ql, off["c_q"] // ql), (proj, kl, off["c_kv"] // kl),
                  (gq,), (gkv,)],
                 [], [(1, ql), (1, kl)], into=(d_proj, ql + kl, off["c_q"] // (ql + kl)))


def _attn_prep(qp, kv, kpr, lay, cos, sin_a, sin_b, tm):
    s = qp.shape[0]
    heads = lay["mla_heads"]
    w = heads * QPAD

    def body(qp_ref, kv_ref, kpr_ref, cos_ref, sa_ref, sb_ref, qf_ref, kf_ref):
        qs = (NOPE + ROPE) ** -0.5 * math.log2(math.e)
        cs, sa, sb = cos_ref[...] * qs, sa_ref[...] * qs, sb_ref[...] * qs
        kp = kpr_ref[...]
        for h in range(heads):
            lo, hi = h * QPAD, h * QPAD + NOPE
            qf_ref[:, lo:hi] = (qp_ref[:, lo:hi] * qs).astype(BF16)
            qf_ref[:, hi:hi + LANES] = _rope32(qp_ref[:, hi:hi + LANES], cs, sa, sb).astype(BF16)
            kf_ref[:, lo:hi] = kv_ref[:, lo:hi]
            kf_ref[:, hi:hi + LANES] = kp

    return _rows("attn_prep", body, s, tm,
                 [(qp, w, 0), (kv, w, 0), (kpr, LANES, 0), (cos, LANES, 0), (sin_a, LANES, 0), (sin_b, LANES, 0)],
                 [(w, BF16), (w, BF16)])


def _attn_post_bwd(dqf, dkf, dv, lay, cos, sin_a, sin_b, d_proj, tm):
    s = dqf.shape[0]
    heads = lay["mla_heads"]
    w = heads * QPAD

    def body(dqf_ref, dkf_ref, dv_ref, cos_ref, sa_ref, sb_ref, dqp_ref, dkv_ref, dkpe_ref):
        cs, sa, sb = cos_ref[...], -sa_ref[...], -sb_ref[...]
        kpe = jnp.zeros((tm, LANES), F32)
        for h in range(heads):
            lo, hi = h * QPAD, h * QPAD + NOPE
            dqp_ref[:, lo:hi] = dqf_ref[:, lo:hi].astype(BF16)
            dqp_ref[:, hi:hi + LANES] = _rope32(dqf_ref[:, hi:hi + LANES], cs, sa, sb).astype(BF16)
            dkv_ref[:, lo:hi] = dkf_ref[:, lo:hi].astype(BF16)
            dkv_ref[:, hi:hi + LANES] = dv_ref[:, h * VHEAD:(h + 1) * VHEAD].astype(BF16)
            kpe = kpe + dkf_ref[:, hi:hi + LANES]
        dkpe_ref[:, :LANES] = _rope32(kpe, cs, sa, sb).astype(BF16)
        dkpe_ref[:, LANES:] = jnp.zeros((tm, LANES), BF16)

    return _rows("attn_post_bwd", body, s, tm,
                 [(dqf, w, 0), (dkf, w, 0), (dv, heads * VHEAD, 0), (cos, LANES, 0), (sin_a, LANES, 0),
                  (sin_b, LANES, 0)],
                 [(w, BF16), (w, BF16)], into=(d_proj, 2 * LANES, lay["off"]["k_pe"] // (2 * LANES)))


def _diag_mask(t, keys_on_rows=False):
    row = lax.broadcasted_iota(jnp.int32, (t, t), 0)
    col = lax.broadcasted_iota(jnp.int32, (t, t), 1)
    key, query = (row, col) if keys_on_rows else (col, row)
    return jnp.right_shift(key, CHUNK_SHIFT) <= jnp.right_shift(query, CHUNK_SHIFT)


def _tile_pairs(nt, by_key):
    if by_key:
        pairs = [(i, j) for j in range(nt) for i in range(j, nt)]
    else:
        pairs = [(i, j) for i in range(nt) for j in range(i + 1)]
    return (jnp.asarray([p[0] for p in pairs], jnp.int32), jnp.asarray([p[1] for p in pairs], jnp.int32))


def _head_block(heads):
    return 2 if heads % 2 == 0 else 1


def _attn_fwd(qf, kf, kv, lay, t, ride=None):
    s = qf.shape[0]
    heads = lay["mla_heads"]
    hb = _head_block(heads)
    nt = s // t
    qi, kj = _tile_pairs(nt, False)
    grid = (heads // hb, int(qi.shape[0]))
    ride_in_specs, ride_ins, ride_out_specs, ride_out_shape, ride_scratch = _ride_args(ride)

    def body(qi_ref, kj_ref, q_ref, k_ref, kv_ref, o_ref, lse_ref, m_s, l_s, acc):
        p = pl.program_id(1)
        i, j = qi_ref[p], kj_ref[p]

        @pl.when(j == 0)
        def _():
            m_s[...] = jnp.full_like(m_s, -jnp.inf)
            l_s[...] = jnp.zeros_like(l_s)
            acc[...] = jnp.zeros_like(acc)

        def step(diagonal):
            ones = jnp.ones((t, LANES), BF16)
            scores = [_dot(q_ref[:, hh * QPAD:(hh + 1) * QPAD], k_ref[:, hh * QPAD:(hh + 1) * QPAD], NT)
                      for hh in range(hb)]
            for hh in range(hb):
                sc = scores[hh]
                if diagonal:
                    sc = jnp.where(_diag_mask(t), sc, -jnp.inf)
                cols = [sc[:, c * LANES:(c + 1) * LANES] for c in range(t // LANES)]
                m_old = m_s[hh]
                m_new = jnp.maximum(m_old, jnp.max(functools.reduce(jnp.maximum, cols), axis=-1, keepdims=True))
                alpha = jnp.exp2(m_old - m_new)
                pr = jnp.concatenate([jnp.exp2(c - m_new).astype(BF16) for c in cols], axis=1)
                pv = _dot(pr, jnp.concatenate([kv_ref[:, hh * QPAD + NOPE:(hh + 1) * QPAD], ones], axis=1), NN)
                l_new = alpha * l_s[hh] + pv[:, VHEAD:]
                a_new = alpha * acc[hh] + pv[:, :VHEAD]
                if diagonal:
                    o_ref[:, hh * VHEAD:(hh + 1) * VHEAD] = a_new / l_new
                    lse_ref[hh] = jnp.transpose(m_new + jnp.log2(l_new))[:1]
                else:
                    m_s[hh], l_s[hh], acc[hh] = m_new, l_new, a_new

        pl.when(j < i)(functools.partial(step, False))
        pl.when(j == i)(functools.partial(step, True))

    res = pl.pallas_call(
        _with_ride(body, ride, grid, 2, 3, 2), name="attn_fwd",
        grid_spec=pltpu.PrefetchScalarGridSpec(
            num_scalar_prefetch=2, grid=grid,
            in_specs=[pl.BlockSpec((t, hb * QPAD), lambda h, p, qi, kj: (qi[p], h)),
                      pl.BlockSpec((t, hb * QPAD), lambda h, p, qi, kj: (kj[p], h)),
                      pl.BlockSpec((t, hb * QPAD), lambda h, p, qi, kj: (kj[p], h))] + ride_in_specs,
            out_specs=[pl.BlockSpec((t, hb * VHEAD), lambda h, p, qi, kj: (qi[p], h)),
                       pl.BlockSpec((hb, 1, t), lambda h, p, qi, kj: (h, 0, qi[p]))] + ride_out_specs,
            scratch_shapes=[pltpu.VMEM((hb, t, LANES), F32), pltpu.VMEM((hb, t, LANES), F32),
                            pltpu.VMEM((hb, t, VHEAD), F32)] + ride_scratch),
        out_shape=[jax.ShapeDtypeStruct((s, heads * VHEAD), F32),
                   jax.ShapeDtypeStruct((heads, 1, s), F32)] + ride_out_shape,
        compiler_params=_params(("arbitrary", "arbitrary") if ride else ("parallel", "arbitrary")),
    )(qi, kj, qf, kf, kv, *ride_ins)
    return (res[0], res[1], res[2:]) if ride else res


def _attn_delta(d_o, o, lay, tm):
    s = o.shape[0]
    heads = lay["mla_heads"]

    def body(do_ref, o_ref, dl_ref):
        for h in range(heads):
            sl = slice(h * VHEAD, (h + 1) * VHEAD)
            dl_ref[h] = jnp.sum(jnp.transpose(do_ref[:, sl] * o_ref[:, sl]), axis=0, keepdims=True)

    tile = pl.BlockSpec((tm, heads * VHEAD), lambda i: (i, 0))
    return pl.pallas_call(
        body, name="attn_delta", grid=(s // tm,), in_specs=[tile, tile],
        out_specs=pl.BlockSpec((heads, 1, tm), lambda i: (0, 0, i)),
        out_shape=jax.ShapeDtypeStruct((heads, 1, s), F32),
        compiler_params=_params(("parallel",)),
    )(d_o, o)


def _attn_bwd(qf, kf, kv, lse, delta, d_o, lay, t, ride=None):
    s = qf.shape[0]
    heads = lay["mla_heads"]
    hb = _head_block(heads)
    nt = s // t
    scale = (NOPE + ROPE) ** -0.5
    qi, kj = _tile_pairs(nt, True)
    grid = (heads // hb, int(qi.shape[0]))
    ride_in_specs, ride_ins, ride_out_specs, ride_out_shape, ride_scratch = _ride_args(ride)

    def body(qi_ref, kj_ref, q_ref, k_ref, kv_ref, lse_ref, dl_ref, do_ref, dq_ref, dk_ref, dv_ref, dk_acc, dv_acc):
        p = pl.program_id(1)
        i, j = qi_ref[p], kj_ref[p]
        rows = pl.ds(pl.multiple_of(i * t, t), t)

        @pl.when(p == 0)
        def _():
            dq_ref[...] = jnp.zeros_like(dq_ref)

        def step(diagonal):
            for hh in range(hb):
                lo = hh * QPAD
                q, k = q_ref[:, lo:lo + QPAD], k_ref[:, lo:lo + QPAD]
                do = do_ref[:, hh * VHEAD:(hh + 1) * VHEAD]
                pr = jnp.exp2(_dot(k, q, NT) - lse_ref[hh])
                if diagonal:
                    pr = jnp.where(_diag_mask(t, keys_on_rows=True), pr, 0.0)
                dv_part = _dot(pr, do, NN)
                ds = (pr * (_dot(kv_ref[:, lo + NOPE:lo + QPAD], do, NT) - dl_ref[hh])).astype(BF16)
                dk_part = _dot(ds, q, NN)
                dq_ref[rows, lo:lo + QPAD] += _dot(ds, k, TN) * scale
                if diagonal:
                    dk_acc[hh], dv_acc[hh] = dk_part, dv_part
                else:
                    dk_acc[hh] += dk_part
                    dv_acc[hh] += dv_part

        pl.when(i > j)(functools.partial(step, False))
        pl.when(i == j)(functools.partial(step, True))

        @pl.when(i == nt - 1)
        def _():
            for hh in range(hb):
                dk_ref[:, hh * QPAD:(hh + 1) * QPAD] = dk_acc[hh] * math.log(2.0)
                dv_ref[:, hh * VHEAD:(hh + 1) * VHEAD] = dv_acc[hh]

    res = pl.pallas_call(
        _with_ride(body, ride, grid, 2, 6, 3), name="attn_bwd",
        grid_spec=pltpu.PrefetchScalarGridSpec(
            num_scalar_prefetch=2, grid=grid,
            in_specs=[pl.BlockSpec((t, hb * QPAD), lambda h, p, qi, kj: (qi[p], h)),
                      pl.BlockSpec((t, hb * QPAD), lambda h, p, qi, kj: (kj[p], h)),
                      pl.BlockSpec((t, hb * QPAD), lambda h, p, qi, kj: (kj[p], h)),
                      pl.BlockSpec((hb, 1, t), lambda h, p, qi, kj: (h, 0, qi[p])),
                      pl.BlockSpec((hb, 1, t), lambda h, p, qi, kj: (h, 0, qi[p])),
                      pl.BlockSpec((t, hb * VHEAD), lambda h, p, qi, kj: (qi[p], h))] + ride_in_specs,
            out_specs=[pl.BlockSpec((s, hb * QPAD), lambda h, p, qi, kj: (0, h)),
                       pl.BlockSpec((t, hb * QPAD), lambda h, p, qi, kj: (kj[p], h)),
                       pl.BlockSpec((t, hb * VHEAD), lambda h, p, qi, kj: (kj[p], h))] + ride_out_specs,
            scratch_shapes=[pltpu.VMEM((hb, t, QPAD), F32), pltpu.VMEM((hb, t, VHEAD), F32)] + ride_scratch),
        out_shape=[jax.ShapeDtypeStruct((s, heads * QPAD), F32),
                   jax.ShapeDtypeStruct((s, heads * QPAD), F32),
                   jax.ShapeDtypeStruct((s, heads * VHEAD), F32)] + ride_out_shape,
        compiler_params=_params(("arbitrary", "arbitrary") if ride else ("parallel", "arbitrary")),
    )(qi, kj, qf, kf, kv, lse, delta, d_o, *ride_ins)
    return (res[0], res[1], res[2], res[3:]) if ride else res


def _merge(proj, lay, y_ret, y_mla, tm):
    s, d = y_ret.shape

    def body(gr_ref, gm_ref, yr_ref, ym_ref, out_ref):
        out_ref[...] = (_sig(gr_ref[...]) * yr_ref[...] + _sig(gm_ref[...]) * ym_ref[...]).astype(BF16)

    off = lay["off"]
    return _rows("merge", body, s, tm,
                 [(proj, d, off["g_ret"] // d), (proj, d, off["g_mla"] // d), (y_ret, d, 0), (y_mla, d, 0)],
                 [(d, BF16)])[0]


def _merge_bwd(proj, lay, d_merged, y_ret, y_mla, tm):
    s, d = y_ret.shape
    off = lay["off"]
    assert off["g_ret"] == 0 and off["g_mla"] == d

    def body(dm_ref, gr_ref, gm_ref, yr_ref, ym_ref, dyr_ref, dym_ref, dg_ref):
        dm = dm_ref[...]
        for g_ref, y_ref, dy_ref, cols in ((gr_ref, yr_ref, dyr_ref, slice(0, d)),
                                           (gm_ref, ym_ref, dym_ref, slice(d, 2 * d))):
            sg = _sig(g_ref[...])
            dy_ref[...] = (dm * sg).astype(BF16)
            dg_ref[:, cols] = (dm * y_ref[...] * (sg * (1.0 - sg))).astype(BF16)

    return _rows("merge_bwd", body, s, tm,
                 [(d_merged, d, 0), (proj, d, 0), (proj, d, 1), (y_ret, d, 0), (y_mla, d, 0)],
                 [(d, BF16), (d, BF16), (2 * d, BF16, lay["total"], 0)])


ANY = pl.BlockSpec(memory_space=pl.ANY)


def _place():
    return lax.axis_index("x"), lax.axis_index("y"), lax.axis_index("c")


def _other_chips(x, y):
    return [(1 - x, y), (x, 1 - y), (1 - x, 1 - y)]


class _Exchange:
    def __init__(self, ins, out_shape, scratch, phases):
        self.ins, self.out_shape, self.scratch, self.phases = list(ins), list(out_shape), list(scratch), phases

    def counts(self):
        return len(self.ins), len(self.out_shape), len(self.scratch)

    def run(self, r_in, r_out, r_scratch, conds):
        for cond, phase in zip(conds, self.phases):
            if phase is not None and cond is not None:
                pl.when(cond)(functools.partial(phase, r_in, r_out, r_scratch))


def _steps(ids, sizes):
    lin, total = 0, 1
    for i, n in zip(ids, sizes):
        lin, total = lin * n + i, total * n
    return lin == 0, lin == total // 2, lin == total - 1


def _ride_args(ride):
    if ride is None:
        return [], [], [], [], []
    n_in, n_out, _ = ride.counts()
    return [ANY] * n_in, ride.ins, [ANY] * n_out, ride.out_shape, ride.scratch


def _with_ride(body, ride, grid, n_prefetch, n_in, n_out):
    if ride is None:
        return body
    r_in, r_out, r_sc = ride.counts()

    def hosted(*refs):
        cuts = (n_prefetch, n_in, r_in, n_out, r_out)
        parts, pos = [], 0
        for n in cuts:
            parts.append(refs[pos:pos + n])
            pos += n
        pre, ins, ride_in, outs, ride_out = parts
        scratch, ride_scratch = refs[pos:len(refs) - r_sc], refs[len(refs) - r_sc:]
        first, mid, last = _steps([pl.program_id(d) for d in range(len(grid))], grid)
        ride.run(ride_in, ride_out, ride_scratch, (first, mid, None))
        body(*pre, *ins, *outs, *scratch)
        ride.run(ride_in, ride_out, ride_scratch, (None, None, last))

    return hosted


def _exchange_alone(name, ex):
    n_in, n_out, _ = ex.counts()

    def body(*refs):
        for phase in ex.phases:
            if phase is not None:
                phase(refs[:n_in], refs[n_in:n_in + n_out], refs[n_in + n_out:])

    return pl.pallas_call(
        body, name=name, in_specs=[ANY] * n_in, out_specs=[ANY] * n_out, out_shape=ex.out_shape,
        scratch_shapes=ex.scratch)(*ex.ins)


def _gather_exchange(shards):
    nw = len(shards)

    def parts(ins, outs, sems):
        send_sems, recv_sems, local_sems = sems
        x, y, c = _place()

        def slot(px, py, pc):
            return 4 * px + 2 * py + pc

        def copy(w, k, rows, to, src=None):
            return pltpu.make_async_remote_copy(
                src_ref=rows if src is None else src, dst_ref=rows, send_sem=send_sems.at[w, k],
                recv_sem=recv_sems.at[w, k], device_id=to, device_id_type=MESH)

        def plan(w, mine):
            side = c if mine else 1 - c
            half = shards[w].shape[0] // 2
            whole = lambda px, py: outs[w].at[slot(px, py, side)]
            top = lambda px, py: outs[w].at[slot(px, py, side), pl.ds(0, half)]
            bottom = lambda px, py: outs[w].at[slot(px, py, side), pl.ds(half, half)]
            xn, yn, sib = (1 - x, y, side), (x, 1 - y, side), (x, y, 1 - side)
            own = ins[w] if mine else None
            return [copy(w, 0, whole(x, y), sib, own), copy(w, 1, whole(x, y), xn, own),
                    copy(w, 2, whole(x, y), yn, own), copy(w, 3, top(1 - x, y), yn), copy(w, 4, bottom(x, 1 - y), xn),
                    copy(w, 5, whole(1 - x, y), sib), copy(w, 6, whole(x, 1 - y), sib),
                    copy(w, 7, top(1 - x, 1 - y), sib), copy(w, 8, bottom(1 - x, 1 - y), sib)]

        def arrivals(w):
            half = shards[w].shape[0] // 2
            at = lambda px, py, *rows: outs[w].at[(slot(px, py, c),) + rows]
            return {1: copy(w, 1, at(1 - x, y), (x, y, c)), 2: copy(w, 2, at(x, 1 - y), (x, y, c)),
                    3: copy(w, 3, at(1 - x, 1 - y, pl.ds(0, half)), (x, y, c)),
                    4: copy(w, 4, at(1 - x, 1 - y, pl.ds(half, half)), (x, y, c))}

        local = [pltpu.make_async_copy(ins[w], outs[w].at[slot(x, y, c)], local_sems.at[w]) for w in range(nw)]
        return plan, arrivals, local

    def start(ins, outs, sems):
        plan, _, local = parts(ins, outs, sems)
        for cp in local:
            cp.start()
        for w in range(nw):
            for k in (0, 1, 2):
                plan(w, True)[k].start()

    def middle(ins, outs, sems):
        plan, arrivals, _ = parts(ins, outs, sems)
        for landed, onward in ((1, (3, 5)), (2, (4, 6))):
            for w in range(nw):
                arrivals(w)[landed].wait_recv()
                for k in onward:
                    plan(w, True)[k].start()

    def finish(ins, outs, sems):
        plan, arrivals, local = parts(ins, outs, sems)
        for landed, onward in ((3, 7), (4, 8)):
            for w in range(nw):
                arrivals(w)[landed].wait_recv()
                plan(w, True)[onward].start()
        for w in range(nw):
            from_sibling = plan(w, False)
            for k in (0, 5, 6, 7, 8):
                from_sibling[k].wait_recv()
            for cp in plan(w, True):
                cp.wait_send()
        for cp in local:
            cp.wait()

    return _Exchange(
        shards, [jax.ShapeDtypeStruct((N_DEV,) + s.shape, s.dtype) for s in shards],
        [pltpu.SemaphoreType.DMA((nw, 9)), pltpu.SemaphoreType.DMA((nw, 9)), pltpu.SemaphoreType.DMA((nw,))],
        (start, middle, finish))


def _sibling_exchange(grads):
    nw = len(grads)

    def copies(ins, outs, sems):
        x, y, c = _place()
        return [pltpu.make_async_remote_copy(
            src_ref=ins[w].at[2 * p + (1 - c)], dst_ref=outs[w].at[p], send_sem=sems[0].at[w, p],
            recv_sem=sems[1].at[w, p], device_id=(x, y, 1 - c), device_id_type=MESH)
            for w in range(nw) for p in range(4)]

    def start(ins, outs, sems):
        for cp in copies(ins, outs, sems):
            cp.start()

    def finish(ins, outs, sems):
        for cp in copies(ins, outs, sems):
            cp.wait()

    return _Exchange(grads, [jax.ShapeDtypeStruct((4,) + g.shape[1:], g.dtype) for g in grads],
                     [pltpu.SemaphoreType.DMA((nw, 4)), pltpu.SemaphoreType.DMA((nw, 4))], (start, None, finish))


def _chips_exchange(sums):
    nw = len(sums)

    def copies(ins, outs, sems):
        x, y, c = _place()
        return [pltpu.make_async_remote_copy(
            src_ref=ins[w].at[2 * px + py], dst_ref=outs[w].at[k], send_sem=sems[0].at[w, k],
            recv_sem=sems[1].at[w, k], device_id=(px, py, c), device_id_type=MESH)
            for w in range(nw) for k, (px, py) in enumerate(_other_chips(x, y))]

    def start(ins, outs, sems):
        for cp in copies(ins, outs, sems):
            cp.start()

    def finish(ins, outs, sems):
        for cp in copies(ins, outs, sems):
            cp.wait()

    return _Exchange(sums, [jax.ShapeDtypeStruct((3,) + g.shape[1:], g.dtype) for g in sums],
                     [pltpu.SemaphoreType.DMA((nw, 3)), pltpu.SemaphoreType.DMA((nw, 3))], (start, None, finish))


def _pair_sum(name, g, got, c_arr, tr):
    _, rows, cols = g.shape
    tr = _tile_rows(rows, tr)

    def body(c_ref, a_ref, b_ref, o_ref):
        o_ref[...] = (a_ref[...].astype(F32) + b_ref[...].astype(F32)).astype(BF16)

    return pl.pallas_call(
        body, name=name,
        grid_spec=pltpu.PrefetchScalarGridSpec(
            num_scalar_prefetch=1, grid=(4, rows // tr),
            in_specs=[pl.BlockSpec((None, tr, cols), lambda p, r, cr: (2 * p + cr[0], r, 0)),
                      pl.BlockSpec((None, tr, cols), lambda p, r, cr: (p, r, 0))],
            out_specs=pl.BlockSpec((None, tr, cols), lambda p, r, cr: (p, r, 0))),
        out_shape=jax.ShapeDtypeStruct((4, rows, cols), BF16),
        compiler_params=_params(("parallel", "parallel")),
    )(c_arr, g, got)


def _tile_rows(rows, pref):
    t = min(rows, pref)
    while rows % t or t % 8:
        t -= 1
    return t


def _adam(w, g, m, v):
    m = ADAM_B1 * m + (1.0 - ADAM_B1) * g
    v = ADAM_B2 * v + (1.0 - ADAM_B2) * (g * g)
    m_hat = m / (1.0 - ADAM_B1 ** ADAM_STEP)
    v_hat = v / (1.0 - ADAM_B2 ** ADAM_STEP)
    return -ADAM_LR * (m_hat / (jnp.sqrt(v_hat) + ADAM_EPS) + ADAM_WD * w), m, v


def _adamw_shard(name, w, m, v, sums, got, chip_arr, tr):
    _, rows, cols = w.shape
    tr = _tile_rows(rows, tr)

    def body(p_ref, w_ref, m_ref, v_ref, s_ref, r_ref, g_ref, d_ref, nm_ref, nv_ref):
        g = s_ref[...].astype(F32)
        for k in range(3):
            g = g + r_ref[k].astype(F32)
        g_ref[...] = g
        d_ref[...], nm_ref[...], nv_ref[...] = _adam(w_ref[...], g, m_ref[...], v_ref[...])

    tile = pl.BlockSpec((None, tr, cols), lambda r, pr: (0, r, 0))
    return pl.pallas_call(
        body, name=name,
        grid_spec=pltpu.PrefetchScalarGridSpec(
            num_scalar_prefetch=1, grid=(rows // tr,),
            in_specs=[tile, tile, tile,
                      pl.BlockSpec((None, tr, cols), lambda r, pr: (pr[0], r, 0)),
                      pl.BlockSpec((3, tr, cols), lambda r, pr: (0, r, 0))],
            out_specs=[tile] * 4),
        out_shape=[jax.ShapeDtypeStruct((1, rows, cols), F32)] * 4,
        compiler_params=_params(("parallel",)),
    )(chip_arr, w, m, v, sums, got)


def _small_all_reduce_adam(part, w, m, v):
    rows = part.shape[0]

    def body(p_ref, w_ref, m_ref, v_ref, g_ref, d_ref, nm_ref, nv_ref, buf, send_sems, recv_sems):
        x, y, c = _place()
        me = 4 * x + 2 * y + c
        buf[me] = p_ref[...]
        peers = [(x, y, 1 - c)] + [(px, py, pc) for px, py in _other_chips(x, y) for pc in (c, 1 - c)]
        copies = []
        for k, peer in enumerate(peers):
            cp = pltpu.make_async_remote_copy(
                src_ref=buf.at[me], dst_ref=buf.at[me], send_sem=send_sems.at[k], recv_sem=recv_sems.at[k],
                device_id=peer, device_id_type=MESH)
            cp.start()
            copies.append(cp)
        for cp in copies:
            cp.wait()
        g = buf[0]
        for k in range(1, N_DEV):
            g = g + buf[k]
        g_ref[...] = g
        d_ref[...], nm_ref[...], nv_ref[...] = _adam(w_ref[...], g, m_ref[...], v_ref[...])

    vm = pl.BlockSpec(memory_space=pltpu.VMEM)
    return pl.pallas_call(
        body, name="gains_all_reduce_adamw",
        in_specs=[vm] * 4, out_specs=[vm] * 4,
        out_shape=[jax.ShapeDtypeStruct((rows, LANES), F32)] * 4,
        scratch_shapes=[pltpu.VMEM((N_DEV, rows, LANES), F32), pltpu.SemaphoreType.DMA((7,)),
                        pltpu.SemaphoreType.DMA((7,))],
        compiler_params=pltpu.CompilerParams(has_side_effects=True),
    )(part, w, m, v)


IN_ORDER = ("r_q", "r_k", "r_v", "r_g", "c_q", "c_kv", "k_pe", "g_ret", "g_mla")
RET_HEAD = 2 * RET_QK + RET_V


def _make_layout(d, vw, qw, ql, kl, mla_w):
    width = {"r_q": qw, "r_k": qw, "r_v": vw, "r_g": vw, "c_q": ql, "c_kv": kl, "k_pe": ROPE, "g_ret": d, "g_mla": d}
    src, o = {}, 0
    for n in IN_ORDER:
        src[n] = o
        o += width[n]
    heads = vw // RET_V
    off, pieces, o = {}, [], 0

    def put(name, w, s):
        nonlocal o
        off.setdefault(name, o)
        pieces.append((o, w, s))
        o += w

    for n in ("g_ret", "g_mla", "r_g"):
        put(n, width[n], src[n])
    for h in range(heads):
        put("heads", RET_QK, src["r_q"] + h * RET_QK)
        put("heads", RET_QK, src["r_k"] + h * RET_QK)
        put("heads", RET_V, src["r_v"] + h * RET_V)
    for n in ("c_q", "c_kv", "k_pe"):
        put(n, width[n], src[n])
    total = off["k_pe"] + 2 * LANES
    for n, blk in (("g_ret", d), ("g_mla", d), ("r_g", vw), ("heads", RET_HEAD), ("c_q", ql + kl), ("k_pe", 2 * LANES)):
        assert off[n] % blk == 0
    assert ql == kl and off["c_kv"] == off["c_q"] + ql
    return {"off": off, "pieces": pieces, "total": total, "n_in": sum(width.values()),
            "ret_heads": heads, "mla_heads": mla_w // VHEAD, "q_lora": ql, "kv_lora": kl}


def _cols_to_full(g):
    n, r, c = g.shape
    return jnp.transpose(g, (1, 0, 2)).reshape(r, n * c)


def _full_to_cols(w):
    r, c = w.shape
    return jnp.transpose(w.reshape(r, N_DEV, c // N_DEV), (1, 0, 2))


def _w_in_to_mine(g, lay):
    _, rows, cols = g.shape
    parts, at = [], 0
    for o, w, s in lay["pieces"]:
        if o > at:
            parts.append(jnp.zeros((rows, o - at), g.dtype))
        while w > 0:
            k, a = divmod(s, cols)
            take = min(w, cols - a)
            parts.append(g[k, :, a:a + take])
            s, w, o = s + take, w - take, o + take
        at = o
    parts.append(jnp.zeros((rows, lay["total"] - at), g.dtype))
    return jnp.concatenate(parts, axis=1)


def _mine_to_blocks(g, lay):
    cols = lay["n_in"] // N_DEV
    by_src = sorted(lay["pieces"], key=lambda p: p[2])
    blocks = []
    for k in range(N_DEV):
        lo, hi, parts = k * cols, (k + 1) * cols, []
        for o, w, s in by_src:
            a, b = max(lo, s), min(hi, s + w)
            if a < b:
                parts.append(g[:, o + a - s:o + b - s])
        blocks.append(jnp.concatenate(parts, axis=1))
    return jnp.stack(blocks)


def _rope_tables(positions, half):
    inv = ROPE_THETA ** (-jnp.arange(half, dtype=F32) / half)
    ang = positions.astype(F32)[:, None] * inv
    return jnp.cos(ang), jnp.sin(ang)


def _pack_rows(vs):
    return jnp.concatenate([v.reshape(-1, LANES) for v in vs], axis=0)


def kernel(x, positions, norm_mix_g, w_in, ret_norm_g, w_ret_o, q_a_norm_g, w_q_b, kv_a_norm_g, w_kv_b, w_mla_o, w_out, norm_mlp_g, w_up, w_down, norm_f_g, loss_target, m_norm_mix_g, m_w_in, m_ret_norm_g, m_w_ret_o, m_q_a_norm_g, m_w_q_b, m_kv_a_norm_g, m_w_kv_b, m_w_mla_o, m_w_out, m_norm_mlp_g, m_w_up, m_w_down, m_norm_f_g, v_norm_mix_g, v_w_in, v_ret_norm_g, v_w_ret_o, v_q_a_norm_g, v_w_q_b, v_kv_a_norm_g, v_w_kv_b, v_w_mla_o, v_w_out, v_norm_mlp_g, v_w_up, v_w_down, v_norm_f_g):
    xs, tgt, pos = x[0], loss_target[0], positions[0]
    s, d = xs.shape
    mats = {"w_in": w_in[0], "w_ret_o": w_ret_o[0], "w_q_b": w_q_b[0], "w_kv_b": w_kv_b[0], "w_mla_o": w_mla_o[0],
            "w_out": w_out[0], "w_up": w_up[0], "w_down": w_down[0]}
    mat_w = {"w_in": w_in, "w_ret_o": w_ret_o, "w_q_b": w_q_b, "w_kv_b": w_kv_b, "w_mla_o": w_mla_o, "w_out": w_out,
             "w_up": w_up, "w_down": w_down}
    mat_m = {"w_in": m_w_in, "w_ret_o": m_w_ret_o, "w_q_b": m_w_q_b, "w_kv_b": m_w_kv_b, "w_mla_o": m_w_mla_o,
             "w_out": m_w_out, "w_up": m_w_up, "w_down": m_w_down}
    mat_v = {"w_in": v_w_in, "w_ret_o": v_w_ret_o, "w_q_b": v_w_q_b, "w_kv_b": v_w_kv_b, "w_mla_o": v_w_mla_o,
             "w_out": v_w_out, "w_up": v_w_up, "w_down": v_w_down}
    names = list(mats)
    col_sharded = ("w_in", "w_q_b", "w_kv_b", "w_up")
    vw = ret_norm_g.shape[1]
    mla_w = mats["w_mla_o"].shape[0] * N_DEV
    ql, kl = q_a_norm_g.shape[1], kv_a_norm_g.shape[1]
    n_in = mats["w_in"].shape[1] * N_DEV
    qw = (n_in - 2 * vw - ql - kl - ROPE - 2 * d) // 2
    lay = _make_layout(d, vw, qw, ql, kl, mla_w)
    assert lay["n_in"] == n_in
    heads_r, heads_m = lay["ret_heads"], lay["mla_heads"]

    shard16 = {n: mats[n].astype(BF16) for n in names}
    with_in_proj = ("w_ret_o", "w_q_b", "w_kv_b", "w_mla_o", "w_out")
    mlp = ("w_up", "w_down")
    by_device = ("w_up", "w_kv_b")
    full = {}

    def keep(group, gathered):
        for n, g in zip(group, gathered):
            if n not in by_device:
                g = _cols_to_full(g) if n in col_sharded else g.reshape(-1, g.shape[2])
            full[n] = g

    w_mine = _w_in_to_mine(_exchange_alone("gather_w_in", _gather_exchange([shard16["w_in"]]))[0], lay)

    c64, s64 = _rope_tables(pos, RET_QK // 2)
    cos_r = jnp.concatenate([c64, c64], axis=1)
    sin_r = jnp.concatenate([-s64, s64], axis=1)
    c32, s32 = _rope_tables(pos, ROPE // 2)
    z32, z64 = jnp.zeros_like(c32), jnp.zeros((s, LANES - ROPE), F32)
    cos_p = jnp.concatenate([c32, c32, z64], axis=1)
    sin_a = jnp.concatenate([-s32, z32, z64], axis=1)
    sin_b = jnp.concatenate([z32, s32, z64], axis=1)
    lg = jnp.log(1.0 - 2.0 ** (-5.0 - jnp.arange(heads_r, dtype=F32)))
    lgs = jnp.broadcast_to(lg[:, None, None], (heads_r, 8, LANES))

    tm = min(256, s)
    blk = min(256, s)
    t_att = min(512, s)

    u = _rms_fwd("norm_mix", xs, norm_mix_g, tm)
    proj, gathered = _mm("in_proj", u, w_mine, "nn", F32,
                         ride=_gather_exchange([shard16[n] for n in with_in_proj]))
    keep(with_in_proj, gathered)
    wq_pad = jnp.pad(full["w_q_b"].reshape(ql, heads_m, NOPE + ROPE),
                     ((0, 0), (0, 0), (0, QPAD - NOPE - ROPE))).reshape(ql, heads_m * QPAD)
    o_ret, states, gathered = _ret_fwd(proj, lay, cos_r, sin_r, lgs, blk, ride=_gather_exchange([shard16["w_up"]]))
    keep(("w_up",), gathered)
    ry = _ret_post(proj, lay, o_ret, ret_norm_g, tm)
    y_ret = _mm("ret_out", ry, full["w_ret_o"], "nn", F32)
    cqn, ckvn, kpr = _mla_prep(proj, lay, q_a_norm_g, kv_a_norm_g, cos_p, sin_a, sin_b, tm)
    qp = _mm("q_up", cqn, wq_pad, "nn", F32)
    kv = _mm("kv_up", ckvn, full["w_kv_b"], "nn", BF16, b_by_device=True)
    qf, kf = _attn_prep(qp, kv, kpr, lay, cos_p, sin_a, sin_b, tm)
    o_mla, lse, gathered = _attn_fwd(qf, kf, kv, lay, t_att, ride=_gather_exchange([shard16["w_down"]]))
    keep(("w_down",), gathered)
    y_mla = _mm("mla_out", o_mla, full["w_mla_o"], "nn", F32)
    merged = _merge(proj, lay, y_ret, y_mla, tm)
    mix = _mm("out_proj", merged, full["w_out"], "nn", F32)
    h1, n2 = _rms_res_fwd("norm_mlp", xs, mix, norm_mlp_g, tm)
    z, act = _mm("mlp_up", n2, full["w_up"], "nn", (F32, BF16), b_by_device=True,
                 epilogue=lambda r: (r, jnp.square(jnp.maximum(r, 0.0))))
    dn = _mm("mlp_down", act, full["w_down"], "nn", F32)
    dh2, g_norm_f, loss_part = _final("loss_head", h1, dn, norm_f_g.reshape(1, d), tgt, tm)

    mx, my, mc = _place()
    c_arr = jnp.reshape(mc, (1,)).astype(jnp.int32)
    chip_arr = jnp.reshape(2 * mx + my, (1,)).astype(jnp.int32)
    sums, from_chips = {}, {}

    def blocks(group, grads):
        return [g if n in by_device else (_full_to_cols(g) if n in col_sharded else g.reshape((N_DEV,) + mats[n].shape))
                for n, g in zip(group, grads)]

    def pair_sums(group, mine, from_sibling):
        for n, g, r in zip(group, mine, from_sibling):
            sums[n] = _pair_sum("pair_sum_" + n, g, r, c_arr, 256)
        return [sums[n] for n in group]

    dz = _mm("mlp_down_dx", dh2, full["w_down"], "nt", BF16, extras=(z,),
             epilogue=lambda r, zz: (r * (2.0 * jnp.maximum(zz, 0.0)),))
    g_w_down = _mm("mlp_down_dw", act, dh2, "tn", BF16)
    g_w_up = _mm("mlp_up_dw", n2, dz, "tn", BF16, out_by_device=True)
    mlp_blocks = blocks(mlp, (g_w_up, g_w_down))
    dn2, got = _mm("mlp_up_dx", dz, full["w_up"], "nt", F32, b_by_device=True, ride=_sibling_exchange(mlp_blocks))
    mlp_sums = pair_sums(mlp, mlp_blocks, got)
    dh1, g_norm_mlp = _rms_bwd("norm_mlp_bwd", dn2, h1, norm_mlp_g, dh2, tm)
    d_merged = _mm("out_proj_dx", dh1, full["w_out"], "nt", F32)
    g_w_out = _mm("out_proj_dw", merged, dh1, "tn", BF16)
    dy_ret, dy_mla, d_proj = _merge_bwd(proj, lay, d_merged, y_ret, y_mla, tm)
    g_w_ret_o = _mm("ret_out_dw", ry, dy_ret, "tn", BF16)
    g_w_mla_o = _mm("mla_out_dw", o_mla, dy_mla, "tn", BF16)
    mixer = ("w_out", "w_ret_o", "w_mla_o")
    mixer_blocks = blocks(mixer, (g_w_out, g_w_ret_o, g_w_mla_o))
    d_ry, got = _mm("ret_out_dx", dy_ret, full["w_ret_o"], "nt", F32, ride=_sibling_exchange(mixer_blocks))
    mixer_sums = pair_sums(mixer, mixer_blocks, got)
    d_omla = _mm("mla_out_dx", dy_mla, full["w_mla_o"], "nt", F32)
    d_oret, d_proj, g_ret_norm = _ret_post_bwd(proj, lay, d_ry, o_ret, ret_norm_g, d_proj, tm)
    d_proj, got = _ret_bwd(proj, lay, cos_r, sin_r, lgs, states, d_oret, d_proj, blk,
                           ride=_chips_exchange(mixer_sums))
    from_chips.update(zip(mixer, got))
    delta = _attn_delta(d_omla, o_mla, lay, t_att)
    dqf, dkf, dv, got = _attn_bwd(qf, kf, kv, lse, delta, d_omla, lay, t_att, ride=_chips_exchange(mlp_sums))
    from_chips.update(zip(mlp, got))
    dqp, dkv, d_proj = _attn_post_bwd(dqf, dkf, dv, lay, cos_p, sin_a, sin_b, d_proj, tm)
    d_cqn = _mm("q_up_dx", dqp, wq_pad, "nt", F32)
    g_wq_pad = _mm("q_up_dw", cqn, dqp, "tn", BF16)
    d_ckvn = _mm("kv_up_dx", dkv, full["w_kv_b"], "nt", F32, b_by_device=True)
    g_w_kv_b = _mm("kv_up_dw", ckvn, dkv, "tn", BF16, out_by_device=True)
    d_proj, g_q_a, g_kv_a = _mla_prep_bwd(proj, lay, d_cqn, d_ckvn, q_a_norm_g, kv_a_norm_g, d_proj, tm)
    g_w_mine = _mm("in_proj_dw", u, d_proj, "tn", BF16)
    g_w_q_b = g_wq_pad.reshape(ql, heads_m, QPAD)[:, :, :NOPE + ROPE].reshape(ql, heads_m * (NOPE + ROPE))
    last = ("w_in", "w_q_b", "w_kv_b")
    last_blocks = [_mine_to_blocks(g_w_mine, lay)] + blocks(last[1:], (g_w_q_b, g_w_kv_b))
    last_sums = pair_sums(last, last_blocks, _exchange_alone("grads_to_sibling", _sibling_exchange(last_blocks)))
    du, got = _mm("in_proj_dx", d_proj, w_mine, "nt", F32, ride=_chips_exchange(last_sums))
    from_chips.update(zip(last, got))
    grad_x, g_norm_mix = _rms_bwd("norm_mix_bwd", du, xs, norm_mix_g, dh1, tm)

    upd = {n: _adamw_shard("adamw_" + n, mat_w[n], mat_m[n], mat_v[n], sums[n], from_chips[n], chip_arr, 256)
           for n in names}

    gains = [("norm_mix_g", norm_mix_g, m_norm_mix_g, v_norm_mix_g, g_norm_mix),
             ("ret_norm_g", ret_norm_g, m_ret_norm_g, v_ret_norm_g, g_ret_norm),
             ("q_a_norm_g", q_a_norm_g, m_q_a_norm_g, v_q_a_norm_g, g_q_a),
             ("kv_a_norm_g", kv_a_norm_g, m_kv_a_norm_g, v_kv_a_norm_g, g_kv_a),
             ("norm_mlp_g", norm_mlp_g, m_norm_mlp_g, v_norm_mlp_g, g_norm_mlp),
             ("norm_f_g", norm_f_g, m_norm_f_g, v_norm_f_g, g_norm_f)]
    n_rows = sum(g[1].size for g in gains) // LANES
    pad_rows = -(-(n_rows + 1) // 8) * 8 - n_rows
    tail = jnp.zeros((pad_rows, LANES), F32)
    part = jnp.concatenate([_pack_rows([g[4] for g in gains]),
                            jnp.broadcast_to(loss_part[:, :1], (1, LANES)), tail[1:]], axis=0)
    packed = [jnp.concatenate([_pack_rows([g[k] for g in gains]), tail], axis=0) for k in (1, 2, 3)]
    g_sm, d_sm, m_sm, v_sm = _small_all_reduce_adam(part, *packed)
    loss = g_sm[n_rows, 0]
    small = {}
    o = 0
    for name, w, _, _, _ in gains:
        r = w.size // LANES
        small[name] = [a[o:o + r].reshape(w.shape) for a in (g_sm, d_sm, m_sm, v_sm)]
        o += r

    order = ["norm_mix_g", "w_in", "ret_norm_g", "w_ret_o", "q_a_norm_g", "w_q_b", "kv_a_norm_g", "w_kv_b", "w_mla_o",
             "w_out", "norm_mlp_g", "w_up", "w_down", "norm_f_g"]
    outs = [loss, grad_x[None]]
    for k in range(4):
        for n in order:
            outs.append(small[n][k] if n in small else upd[n][k])
    return tuple(outs)
```

```python
import functools
import math

import jax
import jax.numpy as jnp
from jax import lax
from jax.experimental import pallas as pl
from jax.experimental.pallas import tpu as pltpu

F32 = jnp.float32
BF16 = jnp.bfloat16
MESH = pl.DeviceIdType.MESH

EPS = 1e-6
ROPE_THETA = 10000.0
CHUNK_SHIFT = 6
RET_QK = 128
RET_V = 256
NOPE = 128
ROPE = 64
VHEAD = 128
QPAD = 256
LANES = 128
N_DEV = 8
VMEM_LIMIT = 56 * 1024 * 1024

ADAM_LR = 0.001
ADAM_B1 = 0.9
ADAM_B2 = 0.999
ADAM_EPS = 1e-08
ADAM_WD = 0.01
ADAM_STEP = 10

NN = (((1,), (0,)), ((), ()))
NT = (((1,), (1,)), ((), ()))
TN = (((0,), (0,)), ((), ()))


def _dot(a, b, dims):
    return lax.dot_general(a.astype(BF16), b.astype(BF16), dims, preferred_element_type=F32)


def _tile(dim, pref):
    if dim <= pref:
        return dim
    t = (pref // LANES) * LANES
    while t >= LANES:
        if dim % t == 0:
            return t
        t -= LANES
    raise ValueError(f"no tile for {dim}")


def _params(sem):
    return pltpu.CompilerParams(dimension_semantics=sem, vmem_limit_bytes=VMEM_LIMIT)


def _sig(v):
    return 1.0 / (1.0 + jnp.exp(-v))


def _mm(name, a, b, mode, out_dtypes, *, tm=1024, tn=1024, tk=2048, extras=(), epilogue=None, ride=None,
        b_by_device=False, out_by_device=False):
    if b_by_device:
        b_cols = b.shape[2]
        b_shape = (b.shape[1], N_DEV * b_cols)
    else:
        b_shape = b.shape
    if mode == "nn":
        (m, k), (_, n) = a.shape, b_shape
    elif mode == "nt":
        (m, k), (n, _) = a.shape, b_shape
    else:
        (k, m), (_, n) = a.shape, b_shape
    tm, tn, tk = _tile(m, tm), _tile(n, tn), _tile(k, tk)
    if b_by_device and mode == "nt":
        tk = _tile(b_cols, tk)
    elif b_by_device:
        tn = _tile(b_cols, tn)
    if out_by_device:
        tn = _tile(n // N_DEV, tn)
    nk = k // tk
    dims = {"nn": NN, "nt": NT, "tn": TN}[mode]
    a_spec = (pl.BlockSpec((tk, tm), lambda i, j, kk: (kk, i)) if mode == "tn"
              else pl.BlockSpec((tm, tk), lambda i, j, kk: (i, kk)))
    if b_by_device and mode == "nt":
        per = b_cols // tk
        b_spec = pl.BlockSpec((None, tn, tk), lambda i, j, kk: (kk // per, j, kk % per))
    elif b_by_device:
        per = b_cols // tn
        b_spec = pl.BlockSpec((None, tk, tn), lambda i, j, kk: (j // per, kk, j % per))
    else:
        b_spec = (pl.BlockSpec((tn, tk), lambda i, j, kk: (j, kk)) if mode == "nt"
                  else pl.BlockSpec((tk, tn), lambda i, j, kk: (kk, j)))
    tile_spec = pl.BlockSpec((tm, tn), lambda i, j, kk: (i, j))
    if out_by_device:
        per_out = n // N_DEV // tn
        out_spec = pl.BlockSpec((None, tm, tn), lambda i, j, kk: (j // per_out, i, j % per_out))
        out_dims = (N_DEV, m, n // N_DEV)
    else:
        out_spec, out_dims = tile_spec, (m, n)
    n_ex = len(extras)
    single = not isinstance(out_dtypes, (tuple, list))
    dts = (out_dtypes,) if single else tuple(out_dtypes)

    grid = (m // tm, n // tn, nk)
    r_in, r_out, r_sc = ride.counts() if ride else (0, 0, 0)
    n_acc = 1 if nk > 1 else 0

    def body(a_ref, b_ref, *rest):
        ex, rest = rest[:n_ex], rest[n_ex:]
        ride_in, rest = rest[:r_in], rest[r_in:]
        outs, rest = rest[:len(dts)], rest[len(dts):]
        ride_out, rest = rest[:r_out], rest[r_out:]
        ride_scratch = rest[n_acc:]
        if ride:
            first, mid, last = _steps([pl.program_id(d) for d in range(3)], grid)
            ride.run(ride_in, ride_out, ride_scratch, (first, mid, None))

        def finish(r):
            vals = (r,) if epilogue is None else epilogue(r, *[e[...] for e in ex])
            for o, v in zip(outs, vals):
                o[...] = v.astype(o.dtype)

        part = _dot(a_ref[...], b_ref[...], dims)
        if nk == 1:
            finish(part)
        else:
            acc = rest[0]
            kk = pl.program_id(2)

            @pl.when(kk == 0)
            def _():
                acc[...] = part

            @pl.when(jnp.logical_and(kk > 0, kk < nk - 1))
            def _():
                acc[...] += part

            @pl.when(kk == nk - 1)
            def _():
                finish(acc[...] + part)

        if ride:
            ride.run(ride_in, ride_out, ride_scratch, (None, None, last))

    res = pl.pallas_call(
        body, name=name, grid=grid,
        in_specs=[a_spec, b_spec] + [tile_spec] * n_ex + [ANY] * r_in,
        out_specs=[out_spec] * len(dts) + [ANY] * r_out,
        out_shape=[jax.ShapeDtypeStruct(out_dims, d) for d in dts] + (ride.out_shape if ride else []),
        scratch_shapes=([pltpu.VMEM((tm, tn), F32)] if nk > 1 else []) + (ride.scratch if ride else []),
        compiler_params=_params(("arbitrary",) * 3 if ride else ("parallel", "parallel", "arbitrary")),
    )(a, b, *extras, *(ride.ins if ride else []))
    own = res[0] if single else res[:len(dts)]
    return (own, res[len(dts):]) if ride else own


def _rows(name, body, n_rows, tm, ins, outs, accs=(), into=None):
    in_specs, args = [], []
    for t in ins:
        if len(t) == 1:
            in_specs.append(pl.BlockSpec(t[0].shape, lambda i, nd=t[0].ndim: (0,) * nd))
        else:
            in_specs.append(pl.BlockSpec((tm, t[1]), lambda i, cb=t[2]: (i, cb)))
        args.append(t[0])
    outs = [(o + (o[0], 0))[:4] for o in outs]
    out_specs = [pl.BlockSpec((tm, w), lambda i, cb=cb: (i, cb)) for w, _, _, cb in outs]
    out_shape = [jax.ShapeDtypeStruct((n_rows, total), d) for _, d, total, _ in outs]
    aliases, kernel = {}, body
    if into is not None:
        arr, w, cb = into
        in_specs.append(ANY)
        args.append(arr)
        out_specs.append(pl.BlockSpec((tm, w), lambda i: (i, cb)))
        out_shape.append(jax.ShapeDtypeStruct(arr.shape, arr.dtype))
        aliases = {len(ins): len(outs)}
        n_in = len(ins)

        def kernel(*refs):
            body(*refs[:n_in], *refs[n_in + 1:])

    out_specs += [pl.BlockSpec((r, w), lambda i: (0, 0)) for r, w in accs]
    out_shape += [jax.ShapeDtypeStruct((r, w), F32) for r, w in accs]
    return pl.pallas_call(
        kernel, name=name, grid=(n_rows // tm,), in_specs=in_specs, out_specs=out_specs, out_shape=out_shape,
        input_output_aliases=aliases, compiler_params=_params(("arbitrary",) if accs else ("parallel",)),
    )(*args)


def _zero_first(*accs):
    @pl.when(pl.program_id(0) == 0)
    def _():
        for a in accs:
            a[...] = jnp.zeros_like(a)


def _rope64(t, cos, sin):
    return t * cos + pltpu.roll(t, RET_QK // 2, 1) * sin


def _rope32(t, cos, sin_a, sin_b):
    return t * cos + pltpu.roll(t, LANES - ROPE // 2, 1) * sin_a + pltpu.roll(t, ROPE // 2, 1) * sin_b


def _rms_fwd(name, x, g, tm):
    s, d = x.shape

    def body(x_ref, g_ref, u_ref):
        v = x_ref[...]
        r = lax.rsqrt(jnp.mean(v * v, axis=-1, keepdims=True) + EPS)
        u_ref[...] = (v * r * g_ref[...]).astype(BF16)

    return _rows(name, body, s, tm, [(x, d, 0), (g,)], [(d, BF16)])[0]


def _rms_res_fwd(name, x, mix, g, tm):
    s, d = x.shape

    def body(x_ref, m_ref, g_ref, h_ref, u_ref):
        v = x_ref[...] + m_ref[...]
        h_ref[...] = v
        r = lax.rsqrt(jnp.mean(v * v, axis=-1, keepdims=True) + EPS)
        u_ref[...] = (v * r * g_ref[...]).astype(BF16)

    return _rows(name, body, s, tm, [(x, d, 0), (mix, d, 0), (g,)], [(d, F32), (d, BF16)])


def _rms_bwd(name, dy, x, g, dres, tm):
    s, d = x.shape

    def body(dy_ref, x_ref, g_ref, dres_ref, dx_ref, dg_ref):
        _zero_first(dg_ref)
        v, dyv = x_ref[...], dy_ref[...]
        r = lax.rsqrt(jnp.mean(v * v, axis=-1, keepdims=True) + EPS)
        xh = v * r
        dxh = dyv * g_ref[...]
        dx_ref[...] = dres_ref[...] + r * (dxh - xh * jnp.mean(dxh * xh, axis=-1, keepdims=True))
        dg_ref[...] += jnp.sum(dyv * xh, axis=0, keepdims=True)

    return _rows(name, body, s, tm, [(dy, d, 0), (x, d, 0), (g,), (dres, d, 0)], [(d, F32)], [(1, d)])


def _final(name, h1, dn, g, tgt, tm):
    s, d = h1.shape

    def body(h_ref, dn_ref, g_ref, t_ref, dh_ref, dg_ref, loss_ref):
        _zero_first(dg_ref, loss_ref)
        v = h_ref[...] + dn_ref[...]
        r = lax.rsqrt(jnp.mean(v * v, axis=-1, keepdims=True) + EPS)
        xh = v * r
        gv = g_ref[...]
        e = xh * gv - t_ref[...]
        loss_ref[...] += 0.5 * jnp.sum(jnp.mean(e * e, axis=-1, keepdims=True))
        dy = e * (1.0 / d)
        dg_ref[...] += jnp.sum(dy * xh, axis=0, keepdims=True)
        dxh = dy * gv
        dh_ref[...] = r * (dxh - xh * jnp.mean(dxh * xh, axis=-1, keepdims=True))

    return _rows(name, body, s, tm, [(h1, d, 0), (dn, d, 0), (g,), (tgt, d, 0)], [(d, F32)], [(1, d), (1, LANES)])


def _decay_mask(lg, blk):
    n = lax.broadcasted_iota(jnp.int32, (blk, blk), 0)
    m = lax.broadcasted_iota(jnp.int32, (blk, blk), 1)
    w = jnp.exp(lg * jnp.abs(n - m).astype(F32))
    return jnp.where(jnp.right_shift(m, CHUNK_SHIFT) <= jnp.right_shift(n, CHUNK_SHIFT), w, 0.0)


def _decays(lg, blk):
    pos = lax.broadcasted_iota(jnp.int32, (blk, 1), 0).astype(F32)
    return jnp.exp(lg * (pos + 1.0)), jnp.exp(lg * (blk - 1.0 - pos)), jnp.exp(lg * float(blk))


def _ret_fwd(proj, lay, cos, sin, lgs, blk, ride=None):
    s = proj.shape[0]
    heads = lay["ret_heads"]
    nb = s // blk
    scale = RET_QK ** -0.5
    ride_in_specs, ride_ins, ride_out_specs, ride_out_shape, ride_scratch = _ride_args(ride)

    def body(lg_ref, qkv_ref, cos_ref, sin_ref, o_ref, st_ref, state, mask):
        lg = lg_ref[0:1, 0:1]

        @pl.when(pl.program_id(1) == 0)
        def _():
            state[...] = jnp.zeros_like(state)
            mask[...] = _decay_mask(lg, blk)

        a, c, gb = _decays(lg, blk)
        q = _rope64(qkv_ref[:, :RET_QK], cos_ref[...], sin_ref[...])
        k = _rope64(qkv_ref[:, RET_QK:2 * RET_QK], cos_ref[...], sin_ref[...]) * scale
        v = qkv_ref[:, 2 * RET_QK:]
        st = state[...]
        st_ref[...] = st
        sm = _dot(q, k, NT) * mask[...]
        o_ref[...] = _dot(sm, v, NN) + _dot(q * a, st, NN)
        state[...] = st * gb + _dot(k * c, v, TN)

    first = lay["off"]["heads"] // RET_HEAD
    res = pl.pallas_call(
        _with_ride(body, ride, (heads, nb), 0, 4, 2), name="ret_fwd", grid=(heads, nb),
        in_specs=[pl.BlockSpec((None, 8, LANES), lambda h, b: (h, 0, 0)),
                  pl.BlockSpec((blk, RET_HEAD), lambda h, b: (b, first + h)),
                  pl.BlockSpec((blk, LANES), lambda h, b: (b, 0)),
                  pl.BlockSpec((blk, LANES), lambda h, b: (b, 0))] + ride_in_specs,
        out_specs=[pl.BlockSpec((blk, RET_V), lambda h, b: (b, h)),
                   pl.BlockSpec((None, None, RET_QK, RET_V), lambda h, b: (h, b, 0, 0))] + ride_out_specs,
        out_shape=[jax.ShapeDtypeStruct((s, heads * RET_V), F32),
                   jax.ShapeDtypeStruct((heads, nb, RET_QK, RET_V), F32)] + ride_out_shape,
        scratch_shapes=[pltpu.VMEM((RET_QK, RET_V), F32), pltpu.VMEM((blk, blk), F32)] + ride_scratch,
        compiler_params=_params(("arbitrary", "arbitrary") if ride else ("parallel", "arbitrary")),
    )(lgs, proj, cos, sin, *ride_ins)
    return (res[0], res[1], res[2:]) if ride else res


def _ret_bwd(proj, lay, cos, sin, lgs, states, d_o, d_proj, blk, ride=None):
    ride_in_specs, ride_ins, ride_out_specs, ride_out_shape, ride_scratch = _ride_args(ride)
    s = proj.shape[0]
    heads = lay["ret_heads"]
    nb = s // blk
    scale = RET_QK ** -0.5

    def body(lg_ref, qkv_ref, cos_ref, sin_ref, st_ref, do_ref, _, dqkv_ref, dstate, mask):
        lg = lg_ref[0:1, 0:1]

        @pl.when(pl.program_id(1) == 0)
        def _():
            dstate[...] = jnp.zeros_like(dstate)
            mask[...] = _decay_mask(lg, blk)

        a, c, gb = _decays(lg, blk)
        cs, sn = cos_ref[...], sin_ref[...]
        q = _rope64(qkv_ref[:, :RET_QK], cs, sn)
        k = _rope64(qkv_ref[:, RET_QK:2 * RET_QK], cs, sn) * scale
        v = qkv_ref[:, 2 * RET_QK:]
        st = st_ref[...]
        do = do_ref[...]
        dst = dstate[...]
        mk = mask[...]
        sm = _dot(q, k, NT) * mk
        ds = _dot(do, v, NT) * mk
        dq = _dot(ds, k, NN) + _dot(do, st, NT) * a
        dk = _dot(ds, q, TN) + _dot(v, dst, NT) * c
        dqkv_ref[:, 2 * RET_QK:] = (_dot(sm, do, TN) + _dot(k * c, dst, NN)).astype(BF16)
        dstate[...] = dst * gb + _dot(q * a, do, TN)
        dqkv_ref[:, :RET_QK] = _rope64(dq, cs, -sn).astype(BF16)
        dqkv_ref[:, RET_QK:2 * RET_QK] = (_rope64(dk, cs, -sn) * scale).astype(BF16)

    first = lay["off"]["heads"] // RET_HEAD
    last = nb - 1
    res = pl.pallas_call(
        _with_ride(body, ride, (heads, nb), 0, 7, 1), name="ret_bwd", grid=(heads, nb),
        in_specs=[pl.BlockSpec((None, 8, LANES), lambda h, b: (h, 0, 0)),
                  pl.BlockSpec((blk, RET_HEAD), lambda h, b: (last - b, first + h)),
                  pl.BlockSpec((blk, LANES), lambda h, b: (last - b, 0)),
                  pl.BlockSpec((blk, LANES), lambda h, b: (last - b, 0)),
                  pl.BlockSpec((None, None, RET_QK, RET_V), lambda h, b: (h, last - b, 0, 0)),
                  pl.BlockSpec((blk, RET_V), lambda h, b: (last - b, h)), ANY] + ride_in_specs,
        out_specs=[pl.BlockSpec((blk, RET_HEAD), lambda h, b: (last - b, first + h))] + ride_out_specs,
        out_shape=[jax.ShapeDtypeStruct(d_proj.shape, d_proj.dtype)] + ride_out_shape,
        scratch_shapes=[pltpu.VMEM((RET_QK, RET_V), F32), pltpu.VMEM((blk, blk), F32)] + ride_scratch,
        input_output_aliases={6: 0},
        compiler_params=_params(("arbitrary", "arbitrary") if ride else ("parallel", "arbitrary")),
    )(lgs, proj, cos, sin, states, d_o, d_proj, *ride_ins)
    return (res[0], res[1:]) if ride else res[0]


def _ret_post(proj, lay, o, g, tm):
    s, vw = o.shape
    heads = lay["ret_heads"]

    def body(o_ref, rg_ref, g_ref, ry_ref):
        for h in range(heads):
            sl = slice(h * RET_V, (h + 1) * RET_V)
            oh = o_ref[:, sl]
            dlt = oh - jnp.mean(oh, axis=-1, keepdims=True)
            rstd = lax.rsqrt(jnp.mean(dlt * dlt, axis=-1, keepdims=True) + EPS)
            rg = rg_ref[:, sl]
            ry_ref[:, sl] = (dlt * rstd * g_ref[:, sl] * (rg * _sig(rg))).astype(BF16)

    return _rows("ret_post", body, s, tm, [(o, vw, 0), (proj, vw, lay["off"]["r_g"] // vw), (g,)], [(vw, BF16)])[0]


def _ret_post_bwd(proj, lay, d_ry, o, g, d_proj, tm):
    s, vw = o.shape
    heads = lay["ret_heads"]

    def body(dry_ref, o_ref, rg_ref, g_ref, do_ref, drg_ref, dg_ref):
        _zero_first(dg_ref)
        for h in range(heads):
            sl = slice(h * RET_V, (h + 1) * RET_V)
            oh = o_ref[:, sl]
            dlt = oh - jnp.mean(oh, axis=-1, keepdims=True)
            rstd = lax.rsqrt(jnp.mean(dlt * dlt, axis=-1, keepdims=True) + EPS)
            oh = dlt * rstd
            gv = g_ref[:, sl]
            rg = rg_ref[:, sl]
            sg = _sig(rg)
            dry = dry_ref[:, sl]
            dt = dry * (rg * sg)
            drg_ref[:, sl] = (dry * (oh * gv) * (sg * (1.0 + rg * (1.0 - sg)))).astype(BF16)
            dg_ref[:, sl] += jnp.sum(dt * oh, axis=0, keepdims=True)
            doh = dt * gv
            do_ref[:, sl] = rstd * (doh - jnp.mean(doh, axis=-1, keepdims=True)
                                    - oh * jnp.mean(doh * oh, axis=-1, keepdims=True))

    return _rows("ret_post_bwd", body, s, tm,
                 [(d_ry, vw, 0), (o, vw, 0), (proj, vw, lay["off"]["r_g"] // vw), (g,)],
                 [(vw, F32)], [(1, vw)], into=(d_proj, vw, lay["off"]["r_g"] // vw))


def _mla_prep(proj, lay, gq, gkv, cos, sin_a, sin_b, tm):
    s = proj.shape[0]
    ql, kl = lay["q_lora"], lay["kv_lora"]

    def body(cq_ref, ckv_ref, kpe_ref, gq_ref, gkv_ref, cos_ref, sa_ref, sb_ref, cqn_ref, ckvn_ref, kpr_ref):
        for src, gref, dst in ((cq_ref, gq_ref, cqn_ref), (ckv_ref, gkv_ref, ckvn_ref)):
            v = src[...]
            r = lax.rsqrt(jnp.mean(v * v, axis=-1, keepdims=True) + EPS)
            dst[...] = (v * r * gref[...]).astype(BF16)
        kpr_ref[...] = _rope32(kpe_ref[...], cos_ref[...], sa_ref[...], sb_ref[...]).astype(BF16)

    off = lay["off"]
    return _rows("mla_prep", body, s, tm,
                 [(proj, ql, off["c_q"] // ql), (proj, kl, off["c_kv"] // kl), (proj, LANES, off["k_pe"] // LANES),
                  (gq,), (gkv,), (cos, LANES, 0), (sin_a, LANES, 0), (sin_b, LANES, 0)],
                 [(ql, BF16), (kl, BF16), (LANES, BF16)])


def _mla_prep_bwd(proj, lay, d_cqn, d_ckvn, gq, gkv, d_proj, tm):
    s = proj.shape[0]
    ql, kl = lay["q_lora"], lay["kv_lora"]

    def body(dq_ref, dkv_ref, cq_ref, ckv_ref, gq_ref, gkv_ref, dc_ref, dgq_ref, dgkv_ref):
        _zero_first(dgq_ref, dgkv_ref)
        for dref, src, gref, cols, dg in ((dq_ref, cq_ref, gq_ref, slice(0, ql), dgq_ref),
                                          (dkv_ref, ckv_ref, gkv_ref, slice(ql, ql + kl), dgkv_ref)):
            v, dy = src[...], dref[...]
            r = lax.rsqrt(jnp.mean(v * v, axis=-1, keepdims=True) + EPS)
            xh = v * r
            dxh = dy * gref[...]
            dc_ref[:, cols] = (r * (dxh - xh * jnp.mean(dxh * xh, axis=-1, keepdims=True))).astype(BF16)
            dg[...] += jnp.sum(dy * xh, axis=0, keepdims=True)

    off = lay["off"]
    return _rows("mla_prep_bwd", body, s, tm,
                 [(d_cqn, ql, 0), (d_ckvn, kl, 0), (proj, ql, off["c_q"] // ql), (proj, kl, off["c_kv"] // kl),
                  (gq,), (gkv,)],
                 [], [(1, ql), (1, kl)], into=(d_proj, ql + kl, off["c_q"] // (ql + kl)))


def _attn_prep(qp, kv, kpr, lay, cos, sin_a, sin_b, tm):
    s = qp.shape[0]
    heads = lay["mla_heads"]
    w = heads * QPAD

    def body(qp_ref, kv_ref, kpr_ref, cos_ref, sa_ref, sb_ref, qf_ref, kf_ref):
        qs = (NOPE + ROPE) ** -0.5 * math.log2(math.e)
        cs, sa, sb = cos_ref[...] * qs, sa_ref[...] * qs, sb_ref[...] * qs
        kp = kpr_ref[...]
        for h in range(heads):
            lo, hi = h * QPAD, h * QPAD + NOPE
            qf_ref[:, lo:hi] = (qp_ref[:, lo:hi] * qs).astype(BF16)
            qf_ref[:, hi:hi + LANES] = _rope32(qp_ref[:, hi:hi + LANES], cs, sa, sb).astype(BF16)
            kf_ref[:, lo:hi] = kv_ref[:, lo:hi]
            kf_ref[:, hi:hi + LANES] = kp

    return _rows("attn_prep", body, s, tm,
                 [(qp, w, 0), (kv, w, 0), (kpr, LANES, 0), (cos, LANES, 0), (sin_a, LANES, 0), (sin_b, LANES, 0)],
                 [(w, BF16), (w, BF16)])


def _attn_post_bwd(dqf, dkf, dv, lay, cos, sin_a, sin_b, d_proj, tm):
    s = dqf.shape[0]
    heads = lay["mla_heads"]
    w = heads * QPAD

    def body(dqf_ref, dkf_ref, dv_ref, cos_ref, sa_ref, sb_ref, dqp_ref, dkv_ref, dkpe_ref):
        cs, sa, sb = cos_ref[...], -sa_ref[...], -sb_ref[...]
        kpe = jnp.zeros((tm, LANES), F32)
        for h in range(heads):
            lo, hi = h * QPAD, h * QPAD + NOPE
            dqp_ref[:, lo:hi] = dqf_ref[:, lo:hi].astype(BF16)
            dqp_ref[:, hi:hi + LANES] = _rope32(dqf_ref[:, hi:hi + LANES], cs, sa, sb).astype(BF16)
            dkv_ref[:, lo:hi] = dkf_ref[:, lo:hi].astype(BF16)
            dkv_ref[:, hi:hi + LANES] = dv_ref[:, h * VHEAD:(h + 1) * VHEAD].astype(BF16)
            kpe = kpe + dkf_ref[:, hi:hi + LANES]
        dkpe_ref[:, :LANES] = _rope32(kpe, cs, sa, sb).astype(BF16)
        dkpe_ref[:, LANES:] = jnp.zeros((tm, LANES), BF16)

    return _rows("attn_post_bwd", body, s, tm,
                 [(dqf, w, 0), (dkf, w, 0), (dv, heads * VHEAD, 0), (cos, LANES, 0), (sin_a, LANES, 0),
                  (sin_b, LANES, 0)],
                 [(w, BF16), (w, BF16)], into=(d_proj, 2 * LANES, lay["off"]["k_pe"] // (2 * LANES)))


def _diag_mask(t, keys_on_rows=False):
    row = lax.broadcasted_iota(jnp.int32, (t, t), 0)
    col = lax.broadcasted_iota(jnp.int32, (t, t), 1)
    key, query = (row, col) if keys_on_rows else (col, row)
    return jnp.right_shift(key, CHUNK_SHIFT) <= jnp.right_shift(query, CHUNK_SHIFT)


def _tile_pairs(nt, by_key):
    if by_key:
        pairs = [(i, j) for j in range(nt) for i in range(j, nt)]
    else:
        pairs = [(i, j) for i in range(nt) for j in range(i + 1)]
    return (jnp.asarray([p[0] for p in pairs], jnp.int32), jnp.asarray([p[1] for p in pairs], jnp.int32))


def _head_block(heads):
    return 4 if heads % 4 == 0 else 2 if heads % 2 == 0 else 1


def _attn_fwd(qf, kf, kv, lay, t, ride=None):
    s = qf.shape[0]
    heads = lay["mla_heads"]
    hb = _head_block(heads)
    nt = s // t
    qi, kj = _tile_pairs(nt, False)
    grid = (heads // hb, int(qi.shape[0]))
    ride_in_specs, ride_ins, ride_out_specs, ride_out_shape, ride_scratch = _ride_args(ride)

    def body(qi_ref, kj_ref, q_ref, k_ref, kv_ref, o_ref, lse_ref, m_s, l_s, acc):
        p = pl.program_id(1)
        i, j = qi_ref[p], kj_ref[p]

        @pl.when(j == 0)
        def _():
            m_s[...] = jnp.full_like(m_s, -jnp.inf)
            l_s[...] = jnp.zeros_like(l_s)
            acc[...] = jnp.zeros_like(acc)

        def step(diagonal):
            ones = jnp.ones((t, LANES), BF16)
            scores = [_dot(q_ref[:, hh * QPAD:(hh + 1) * QPAD], k_ref[:, hh * QPAD:(hh + 1) * QPAD], NT)
                      for hh in range(hb)]
            for hh in range(hb):
                sc = scores[hh]
                if diagonal:
                    sc = jnp.where(_diag_mask(t), sc, -jnp.inf)
                cols = [sc[:, c * LANES:(c + 1) * LANES] for c in range(t // LANES)]
                m_old = m_s[hh]
                m_new = jnp.maximum(m_old, jnp.max(functools.reduce(jnp.maximum, cols), axis=-1, keepdims=True))
                alpha = jnp.exp2(m_old - m_new)
                pr = jnp.concatenate([jnp.exp2(c - m_new).astype(BF16) for c in cols], axis=1)
                pv = _dot(pr, jnp.concatenate([kv_ref[:, hh * QPAD + NOPE:(hh + 1) * QPAD], ones], axis=1), NN)
                l_new = alpha * l_s[hh] + pv[:, VHEAD:]
                a_new = alpha * acc[hh] + pv[:, :VHEAD]
                if diagonal:
                    o_ref[:, hh * VHEAD:(hh + 1) * VHEAD] = a_new / l_new
                    lse_ref[hh] = jnp.transpose(m_new + jnp.log2(l_new))[:1]
                else:
                    m_s[hh], l_s[hh], acc[hh] = m_new, l_new, a_new

        pl.when(j < i)(functools.partial(step, False))
        pl.when(j == i)(functools.partial(step, True))

    res = pl.pallas_call(
        _with_ride(body, ride, grid, 2, 3, 2), name="attn_fwd",
        grid_spec=pltpu.PrefetchScalarGridSpec(
            num_scalar_prefetch=2, grid=grid,
            in_specs=[pl.BlockSpec((t, hb * QPAD), lambda h, p, qi, kj: (qi[p], h)),
                      pl.BlockSpec((t, hb * QPAD), lambda h, p, qi, kj: (kj[p], h)),
                      pl.BlockSpec((t, hb * QPAD), lambda h, p, qi, kj: (kj[p], h))] + ride_in_specs,
            out_specs=[pl.BlockSpec((t, hb * VHEAD), lambda h, p, qi, kj: (qi[p], h)),
                       pl.BlockSpec((hb, 1, t), lambda h, p, qi, kj: (h, 0, qi[p]))] + ride_out_specs,
            scratch_shapes=[pltpu.VMEM((hb, t, LANES), F32), pltpu.VMEM((hb, t, LANES), F32),
                            pltpu.VMEM((hb, t, VHEAD), F32)] + ride_scratch),
        out_shape=[jax.ShapeDtypeStruct((s, heads * VHEAD), F32),
                   jax.ShapeDtypeStruct((heads, 1, s), F32)] + ride_out_shape,
        compiler_params=_params(("arbitrary", "arbitrary") if ride else ("parallel", "arbitrary")),
    )(qi, kj, qf, kf, kv, *ride_ins)
    return (res[0], res[1], res[2:]) if ride else res


def _attn_delta(d_o, o, lay, tm):
    s = o.shape[0]
    heads = lay["mla_heads"]

    def body(do_ref, o_ref, dl_ref):
        for h in range(heads):
            sl = slice(h * VHEAD, (h + 1) * VHEAD)
            dl_ref[h] = jnp.sum(jnp.transpose(do_ref[:, sl] * o_ref[:, sl]), axis=0, keepdims=True)

    tile = pl.BlockSpec((tm, heads * VHEAD), lambda i: (i, 0))
    return pl.pallas_call(
        body, name="attn_delta", grid=(s // tm,), in_specs=[tile, tile],
        out_specs=pl.BlockSpec((heads, 1, tm), lambda i: (0, 0, i)),
        out_shape=jax.ShapeDtypeStruct((heads, 1, s), F32),
        compiler_params=_params(("parallel",)),
    )(d_o, o)


def _attn_bwd(qf, kf, kv, lse, delta, d_o, lay, t, ride=None):
    s = qf.shape[0]
    heads = lay["mla_heads"]
    hb = _head_block(heads)
    nt = s // t
    scale = (NOPE + ROPE) ** -0.5
    qi, kj = _tile_pairs(nt, True)
    grid = (heads // hb, int(qi.shape[0]))
    ride_in_specs, ride_ins, ride_out_specs, ride_out_shape, ride_scratch = _ride_args(ride)

    def body(qi_ref, kj_ref, q_ref, k_ref, kv_ref, lse_ref, dl_ref, do_ref, dq_ref, dk_ref, dv_ref, dk_acc, dv_acc):
        p = pl.program_id(1)
        i, j = qi_ref[p], kj_ref[p]
        rows = pl.ds(pl.multiple_of(i * t, t), t)

        @pl.when(p == 0)
        def _():
            dq_ref[...] = jnp.zeros_like(dq_ref)

        def step(diagonal):
            for hh in range(hb):
                lo = hh * QPAD
                q, k = q_ref[:, lo:lo + QPAD], k_ref[:, lo:lo + QPAD]
                do = do_ref[:, hh * VHEAD:(hh + 1) * VHEAD]
                pr = jnp.exp2(_dot(k, q, NT) - lse_ref[hh])
                if diagonal:
                    pr = jnp.where(_diag_mask(t, keys_on_rows=True), pr, 0.0)
                dv_part = _dot(pr, do, NN)
                ds = (pr * (_dot(kv_ref[:, lo + NOPE:lo + QPAD], do, NT) - dl_ref[hh])).astype(BF16)
                dk_part = _dot(ds, q, NN)
                dq_ref[rows, lo:lo + QPAD] += _dot(ds, k, TN) * scale
                if diagonal:
                    dk_acc[hh], dv_acc[hh] = dk_part, dv_part
                else:
                    dk_acc[hh] += dk_part
                    dv_acc[hh] += dv_part

        pl.when(i > j)(functools.partial(step, False))
        pl.when(i == j)(functools.partial(step, True))

        @pl.when(i == nt - 1)
        def _():
            for hh in range(hb):
                dk_ref[:, hh * QPAD:(hh + 1) * QPAD] = dk_acc[hh] * math.log(2.0)
                dv_ref[:, hh * VHEAD:(hh + 1) * VHEAD] = dv_acc[hh]

    res = pl.pallas_call(
        _with_ride(body, ride, grid, 2, 6, 3), name="attn_bwd",
        grid_spec=pltpu.PrefetchScalarGridSpec(
            num_scalar_prefetch=2, grid=grid,
            in_specs=[pl.BlockSpec((t, hb * QPAD), lambda h, p, qi, kj: (qi[p], h)),
                      pl.BlockSpec((t, hb * QPAD), lambda h, p, qi, kj: (kj[p], h)),
                      pl.BlockSpec((t, hb * QPAD), lambda h, p, qi, kj: (kj[p], h)),
                      pl.BlockSpec((hb, 1, t), lambda h, p, qi, kj: (h, 0, qi[p])),
                      pl.BlockSpec((hb, 1, t), lambda h, p, qi, kj: (h, 0, qi[p])),
                      pl.BlockSpec((t, hb * VHEAD), lambda h, p, qi, kj: (qi[p], h))] + ride_in_specs,
            out_specs=[pl.BlockSpec((s, hb * QPAD), lambda h, p, qi, kj: (0, h)),
                       pl.BlockSpec((t, hb * QPAD), lambda h, p, qi, kj: (kj[p], h)),
                       pl.BlockSpec((t, hb * VHEAD), lambda h, p, qi, kj: (kj[p], h))] + ride_out_specs,
            scratch_shapes=[pltpu.VMEM((hb, t, QPAD), F32), pltpu.VMEM((hb, t, VHEAD), F32)] + ride_scratch),
        out_shape=[jax.ShapeDtypeStruct((s, heads * QPAD), F32),
                   jax.ShapeDtypeStruct((s, heads * QPAD), F32),
                   jax.ShapeDtypeStruct((s, heads * VHEAD), F32)] + ride_out_shape,
        compiler_params=_params(("arbitrary", "arbitrary") if ride else ("parallel", "arbitrary")),
    )(qi, kj, qf, kf, kv, lse, delta, d_o, *ride_ins)
    return (res[0], res[1], res[2], res[3:]) if ride else res


def _merge(proj, lay, y_ret, y_mla, tm):
    s, d = y_ret.shape

    def body(gr_ref, gm_ref, yr_ref, ym_ref, out_ref):
        out_ref[...] = (_sig(gr_ref[...]) * yr_ref[...] + _sig(gm_ref[...]) * ym_ref[...]).astype(BF16)

    off = lay["off"]
    return _rows("merge", body, s, tm,
                 [(proj, d, off["g_ret"] // d), (proj, d, off["g_mla"] // d), (y_ret, d, 0), (y_mla, d, 0)],
                 [(d, BF16)])[0]


def _merge_bwd(proj, lay, d_merged, y_ret, y_mla, tm):
    s, d = y_ret.shape
    off = lay["off"]
    assert off["g_ret"] == 0 and off["g_mla"] == d

    def body(dm_ref, gr_ref, gm_ref, yr_ref, ym_ref, dyr_ref, dym_ref, dg_ref):
        dm = dm_ref[...]
        for g_ref, y_ref, dy_ref, cols in ((gr_ref, yr_ref, dyr_ref, slice(0, d)),
                                           (gm_ref, ym_ref, dym_ref, slice(d, 2 * d))):
            sg = _sig(g_ref[...])
            dy_ref[...] = (dm * sg).astype(BF16)
            dg_ref[:, cols] = (dm * y_ref[...] * (sg * (1.0 - sg))).astype(BF16)

    return _rows("merge_bwd", body, s, tm,
                 [(d_merged, d, 0), (proj, d, 0), (proj, d, 1), (y_ret, d, 0), (y_mla, d, 0)],
                 [(d, BF16), (d, BF16), (2 * d, BF16, lay["total"], 0)])


ANY = pl.BlockSpec(memory_space=pl.ANY)


def _place():
    return lax.axis_index("x"), lax.axis_index("y"), lax.axis_index("c")


def _other_chips(x, y):
    return [(1 - x, y), (x, 1 - y), (1 - x, 1 - y)]


class _Exchange:
    def __init__(self, ins, out_shape, scratch, phases):
        self.ins, self.out_shape, self.scratch, self.phases = list(ins), list(out_shape), list(scratch), phases

    def counts(self):
        return len(self.ins), len(self.out_shape), len(self.scratch)

    def run(self, r_in, r_out, r_scratch, conds):
        for cond, phase in zip(conds, self.phases):
            if phase is not None and cond is not None:
                pl.when(cond)(functools.partial(phase, r_in, r_out, r_scratch))


def _steps(ids, sizes):
    lin, total = 0, 1
    for i, n in zip(ids, sizes):
        lin, total = lin * n + i, total * n
    return lin == 0, lin == total // 2, lin == total - 1


def _ride_args(ride):
    if ride is None:
        return [], [], [], [], []
    n_in, n_out, _ = ride.counts()
    return [ANY] * n_in, ride.ins, [ANY] * n_out, ride.out_shape, ride.scratch


def _with_ride(body, ride, grid, n_prefetch, n_in, n_out):
    if ride is None:
        return body
    r_in, r_out, r_sc = ride.counts()

    def hosted(*refs):
        cuts = (n_prefetch, n_in, r_in, n_out, r_out)
        parts, pos = [], 0
        for n in cuts:
            parts.append(refs[pos:pos + n])
            pos += n
        pre, ins, ride_in, outs, ride_out = parts
        scratch, ride_scratch = refs[pos:len(refs) - r_sc], refs[len(refs) - r_sc:]
        first, mid, last = _steps([pl.program_id(d) for d in range(len(grid))], grid)
        ride.run(ride_in, ride_out, ride_scratch, (first, mid, None))
        body(*pre, *ins, *outs, *scratch)
        ride.run(ride_in, ride_out, ride_scratch, (None, None, last))

    return hosted


def _exchange_alone(name, ex):
    n_in, n_out, _ = ex.counts()

    def body(*refs):
        for phase in ex.phases:
            if phase is not None:
                phase(refs[:n_in], refs[n_in:n_in + n_out], refs[n_in + n_out:])

    return pl.pallas_call(
        body, name=name, in_specs=[ANY] * n_in, out_specs=[ANY] * n_out, out_shape=ex.out_shape,
        scratch_shapes=ex.scratch)(*ex.ins)


def _gather_exchange(shards):
    nw = len(shards)

    def parts(ins, outs, sems):
        send_sems, recv_sems, local_sems = sems
        x, y, c = _place()

        def slot(px, py, pc):
            return 4 * px + 2 * py + pc

        def copy(w, k, rows, to, src=None):
            return pltpu.make_async_remote_copy(
                src_ref=rows if src is None else src, dst_ref=rows, send_sem=send_sems.at[w, k],
                recv_sem=recv_sems.at[w, k], device_id=to, device_id_type=MESH)

        def plan(w, mine):
            side = c if mine else 1 - c
            half = shards[w].shape[0] // 2
            whole = lambda px, py: outs[w].at[slot(px, py, side)]
            top = lambda px, py: outs[w].at[slot(px, py, side), pl.ds(0, half)]
            bottom = lambda px, py: outs[w].at[slot(px, py, side), pl.ds(half, half)]
            xn, yn, sib = (1 - x, y, side), (x, 1 - y, side), (x, y, 1 - side)
            own = ins[w] if mine else None
            return [copy(w, 0, whole(x, y), sib, own), copy(w, 1, whole(x, y), xn, own),
                    copy(w, 2, whole(x, y), yn, own), copy(w, 3, top(1 - x, y), yn), copy(w, 4, bottom(x, 1 - y), xn),
                    copy(w, 5, whole(1 - x, y), sib), copy(w, 6, whole(x, 1 - y), sib),
                    copy(w, 7, top(1 - x, 1 - y), sib), copy(w, 8, bottom(1 - x, 1 - y), sib)]

        def arrivals(w):
            half = shards[w].shape[0] // 2
            at = lambda px, py, *rows: outs[w].at[(slot(px, py, c),) + rows]
            return {1: copy(w, 1, at(1 - x, y), (x, y, c)), 2: copy(w, 2, at(x, 1 - y), (x, y, c)),
                    3: copy(w, 3, at(1 - x, 1 - y, pl.ds(0, half)), (x, y, c)),
                    4: copy(w, 4, at(1 - x, 1 - y, pl.ds(half, half)), (x, y, c))}

        local = [pltpu.make_async_copy(ins[w], outs[w].at[slot(x, y, c)], local_sems.at[w]) for w in range(nw)]
        return plan, arrivals, local

    def start(ins, outs, sems):
        plan, _, local = parts(ins, outs, sems)
        for cp in local:
            cp.start()
        for w in range(nw):
            for k in (0, 1, 2):
                plan(w, True)[k].start()

    def middle(ins, outs, sems):
        plan, arrivals, _ = parts(ins, outs, sems)
        for landed, onward in ((1, (3, 5)), (2, (4, 6))):
            for w in range(nw):
                arrivals(w)[landed].wait_recv()
                for k in onward:
                    plan(w, True)[k].start()

    def finish(ins, outs, sems):
        plan, arrivals, local = parts(ins, outs, sems)
        for landed, onward in ((3, 7), (4, 8)):
            for w in range(nw):
                arrivals(w)[landed].wait_recv()
                plan(w, True)[onward].start()
        for w in range(nw):
            from_sibling = plan(w, False)
            for k in (0, 5, 6, 7, 8):
                from_sibling[k].wait_recv()
            for cp in plan(w, True):
                cp.wait_send()
        for cp in local:
            cp.wait()

    return _Exchange(
        shards, [jax.ShapeDtypeStruct((N_DEV,) + s.shape, s.dtype) for s in shards],
        [pltpu.SemaphoreType.DMA((nw, 9)), pltpu.SemaphoreType.DMA((nw, 9)), pltpu.SemaphoreType.DMA((nw,))],
        (start, middle, finish))


def _sibling_exchange(grads):
    nw = len(grads)

    def copies(ins, outs, sems):
        x, y, c = _place()
        return [pltpu.make_async_remote_copy(
            src_ref=ins[w].at[2 * p + (1 - c)], dst_ref=outs[w].at[p], send_sem=sems[0].at[w, p],
            recv_sem=sems[1].at[w, p], device_id=(x, y, 1 - c), device_id_type=MESH)
            for w in range(nw) for p in range(4)]

    def start(ins, outs, sems):
        for cp in copies(ins, outs, sems):
            cp.start()

    def finish(ins, outs, sems):
        for cp in copies(ins, outs, sems):
            cp.wait()

    return _Exchange(grads, [jax.ShapeDtypeStruct((4,) + g.shape[1:], g.dtype) for g in grads],
                     [pltpu.SemaphoreType.DMA((nw, 4)), pltpu.SemaphoreType.DMA((nw, 4))], (start, None, finish))


def _chips_exchange(sums):
    nw = len(sums)

    def copies(ins, outs, sems):
        x, y, c = _place()
        return [pltpu.make_async_remote_copy(
            src_ref=ins[w].at[2 * px + py], dst_ref=outs[w].at[k], send_sem=sems[0].at[w, k],
            recv_sem=sems[1].at[w, k], device_id=(px, py, c), device_id_type=MESH)
            for w in range(nw) for k, (px, py) in enumerate(_other_chips(x, y))]

    def start(ins, outs, sems):
        for cp in copies(ins, outs, sems):
            cp.start()

    def finish(ins, outs, sems):
        for cp in copies(ins, outs, sems):
            cp.wait()

    return _Exchange(sums, [jax.ShapeDtypeStruct((3,) + g.shape[1:], g.dtype) for g in sums],
                     [pltpu.SemaphoreType.DMA((nw, 3)), pltpu.SemaphoreType.DMA((nw, 3))], (start, None, finish))


def _pair_sum(name, g, got, c_arr, tr):
    _, rows, cols = g.shape
    tr = _tile_rows(rows, tr)

    def body(c_ref, a_ref, b_ref, o_ref):
        o_ref[...] = (a_ref[...].astype(F32) + b_ref[...].astype(F32)).astype(BF16)

    return pl.pallas_call(
        body, name=name,
        grid_spec=pltpu.PrefetchScalarGridSpec(
            num_scalar_prefetch=1, grid=(4, rows // tr),
            in_specs=[pl.BlockSpec((None, tr, cols), lambda p, r, cr: (2 * p + cr[0], r, 0)),
                      pl.BlockSpec((None, tr, cols), lambda p, r, cr: (p, r, 0))],
            out_specs=pl.BlockSpec((None, tr, cols), lambda p, r, cr: (p, r, 0))),
        out_shape=jax.ShapeDtypeStruct((4, rows, cols), BF16),
        compiler_params=_params(("parallel", "parallel")),
    )(c_arr, g, got)


def _tile_rows(rows, pref):
    t = min(rows, pref)
    while rows % t or t % 8:
        t -= 1
    return t


def _adam(w, g, m, v):
    m = ADAM_B1 * m + (1.0 - ADAM_B1) * g
    v = ADAM_B2 * v + (1.0 - ADAM_B2) * (g * g)
    m_hat = m / (1.0 - ADAM_B1 ** ADAM_STEP)
    v_hat = v / (1.0 - ADAM_B2 ** ADAM_STEP)
    return -ADAM_LR * (m_hat / (jnp.sqrt(v_hat) + ADAM_EPS) + ADAM_WD * w), m, v


def _adamw_shard(name, w, m, v, sums, got, chip_arr, tr):
    _, rows, cols = w.shape
    tr = _tile_rows(rows, tr)

    def body(p_ref, w_ref, m_ref, v_ref, s_ref, r_ref, g_ref, d_ref, nm_ref, nv_ref):
        g = s_ref[...].astype(F32)
        for k in range(3):
            g = g + r_ref[k].astype(F32)
        g_ref[...] = g
        d_ref[...], nm_ref[...], nv_ref[...] = _adam(w_ref[...], g, m_ref[...], v_ref[...])

    tile = pl.BlockSpec((None, tr, cols), lambda r, pr: (0, r, 0))
    return pl.pallas_call(
        body, name=name,
        grid_spec=pltpu.PrefetchScalarGridSpec(
            num_scalar_prefetch=1, grid=(rows // tr,),
            in_specs=[tile, tile, tile,
                      pl.BlockSpec((None, tr, cols), lambda r, pr: (pr[0], r, 0)),
                      pl.BlockSpec((3, tr, cols), lambda r, pr: (0, r, 0))],
            out_specs=[tile] * 4),
        out_shape=[jax.ShapeDtypeStruct((1, rows, cols), F32)] * 4,
        compiler_params=_params(("parallel",)),
    )(chip_arr, w, m, v, sums, got)


def _small_all_reduce_adam(part, w, m, v):
    rows = part.shape[0]

    def body(p_ref, w_ref, m_ref, v_ref, g_ref, d_ref, nm_ref, nv_ref, buf, send_sems, recv_sems):
        x, y, c = _place()
        me = 4 * x + 2 * y + c
        buf[me] = p_ref[...]
        peers = [(x, y, 1 - c)] + [(px, py, pc) for px, py in _other_chips(x, y) for pc in (c, 1 - c)]
        copies = []
        for k, peer in enumerate(peers):
            cp = pltpu.make_async_remote_copy(
                src_ref=buf.at[me], dst_ref=buf.at[me], send_sem=send_sems.at[k], recv_sem=recv_sems.at[k],
                device_id=peer, device_id_type=MESH)
            cp.start()
            copies.append(cp)
        for cp in copies:
            cp.wait()
        g = buf[0]
        for k in range(1, N_DEV):
            g = g + buf[k]
        g_ref[...] = g
        d_ref[...], nm_ref[...], nv_ref[...] = _adam(w_ref[...], g, m_ref[...], v_ref[...])

    vm = pl.BlockSpec(memory_space=pltpu.VMEM)
    return pl.pallas_call(
        body, name="gains_all_reduce_adamw",
        in_specs=[vm] * 4, out_specs=[vm] * 4,
        out_shape=[jax.ShapeDtypeStruct((rows, LANES), F32)] * 4,
        scratch_shapes=[pltpu.VMEM((N_DEV, rows, LANES), F32), pltpu.SemaphoreType.DMA((7,)),
                        pltpu.SemaphoreType.DMA((7,))],
        compiler_params=pltpu.CompilerParams(has_side_effects=True),
    )(part, w, m, v)


IN_ORDER = ("r_q", "r_k", "r_v", "r_g", "c_q", "c_kv", "k_pe", "g_ret", "g_mla")
RET_HEAD = 2 * RET_QK + RET_V


def _make_layout(d, vw, qw, ql, kl, mla_w):
    width = {"r_q": qw, "r_k": qw, "r_v": vw, "r_g": vw, "c_q": ql, "c_kv": kl, "k_pe": ROPE, "g_ret": d, "g_mla": d}
    src, o = {}, 0
    for n in IN_ORDER:
        src[n] = o
        o += width[n]
    heads = vw // RET_V
    off, pieces, o = {}, [], 0

    def put(name, w, s):
        nonlocal o
        off.setdefault(name, o)
        pieces.append((o, w, s))
        o += w

    for n in ("g_ret", "g_mla", "r_g"):
        put(n, width[n], src[n])
    for h in range(heads):
        put("heads", RET_QK, src["r_q"] + h * RET_QK)
        put("heads", RET_QK, src["r_k"] + h * RET_QK)
        put("heads", RET_V, src["r_v"] + h * RET_V)
    for n in ("c_q", "c_kv", "k_pe"):
        put(n, width[n], src[n])
    total = off["k_pe"] + 2 * LANES
    for n, blk in (("g_ret", d), ("g_mla", d), ("r_g", vw), ("heads", RET_HEAD), ("c_q", ql + kl), ("k_pe", 2 * LANES)):
        assert off[n] % blk == 0
    assert ql == kl and off["c_kv"] == off["c_q"] + ql
    return {"off": off, "pieces": pieces, "total": total, "n_in": sum(width.values()),
            "ret_heads": heads, "mla_heads": mla_w // VHEAD, "q_lora": ql, "kv_lora": kl}


def _cols_to_full(g):
    n, r, c = g.shape
    return jnp.transpose(g, (1, 0, 2)).reshape(r, n * c)


def _full_to_cols(w):
    r, c = w.shape
    return jnp.transpose(w.reshape(r, N_DEV, c // N_DEV), (1, 0, 2))


def _w_in_to_mine(g, lay):
    _, rows, cols = g.shape
    parts, at = [], 0
    for o, w, s in lay["pieces"]:
        if o > at:
            parts.append(jnp.zeros((rows, o - at), g.dtype))
        while w > 0:
            k, a = divmod(s, cols)
            take = min(w, cols - a)
            parts.append(g[k, :, a:a + take])
            s, w, o = s + take, w - take, o + take
        at = o
    parts.append(jnp.zeros((rows, lay["total"] - at), g.dtype))
    return jnp.concatenate(parts, axis=1)


def _mine_to_blocks(g, lay):
    cols = lay["n_in"] // N_DEV
    by_src = sorted(lay["pieces"], key=lambda p: p[2])
    blocks = []
    for k in range(N_DEV):
        lo, hi, parts = k * cols, (k + 1) * cols, []
        for o, w, s in by_src:
            a, b = max(lo, s), min(hi, s + w)
            if a < b:
                parts.append(g[:, o + a - s:o + b - s])
        blocks.append(jnp.concatenate(parts, axis=1))
    return jnp.stack(blocks)


def _rope_tables(positions, half):
    inv = ROPE_THETA ** (-jnp.arange(half, dtype=F32) / half)
    ang = positions.astype(F32)[:, None] * inv
    return jnp.cos(ang), jnp.sin(ang)


def _pack_rows(vs):
    return jnp.concatenate([v.reshape(-1, LANES) for v in vs], axis=0)


def kernel(x, positions, norm_mix_g, w_in, ret_norm_g, w_ret_o, q_a_norm_g, w_q_b, kv_a_norm_g, w_kv_b, w_mla_o, w_out, norm_mlp_g, w_up, w_down, norm_f_g, loss_target, m_norm_mix_g, m_w_in, m_ret_norm_g, m_w_ret_o, m_q_a_norm_g, m_w_q_b, m_kv_a_norm_g, m_w_kv_b, m_w_mla_o, m_w_out, m_norm_mlp_g, m_w_up, m_w_down, m_norm_f_g, v_norm_mix_g, v_w_in, v_ret_norm_g, v_w_ret_o, v_q_a_norm_g, v_w_q_b, v_kv_a_norm_g, v_w_kv_b, v_w_mla_o, v_w_out, v_norm_mlp_g, v_w_up, v_w_down, v_norm_f_g):
    xs, tgt, pos = x[0], loss_target[0], positions[0]
    s, d = xs.shape
    mats = {"w_in": w_in[0], "w_ret_o": w_ret_o[0], "w_q_b": w_q_b[0], "w_kv_b": w_kv_b[0], "w_mla_o": w_mla_o[0],
            "w_out": w_out[0], "w_up": w_up[0], "w_down": w_down[0]}
    mat_w = {"w_in": w_in, "w_ret_o": w_ret_o, "w_q_b": w_q_b, "w_kv_b": w_kv_b, "w_mla_o": w_mla_o, "w_out": w_out,
             "w_up": w_up, "w_down": w_down}
    mat_m = {"w_in": m_w_in, "w_ret_o": m_w_ret_o, "w_q_b": m_w_q_b, "w_kv_b": m_w_kv_b, "w_mla_o": m_w_mla_o,
             "w_out": m_w_out, "w_up": m_w_up, "w_down": m_w_down}
    mat_v = {"w_in": v_w_in, "w_ret_o": v_w_ret_o, "w_q_b": v_w_q_b, "w_kv_b": v_w_kv_b, "w_mla_o": v_w_mla_o,
             "w_out": v_w_out, "w_up": v_w_up, "w_down": v_w_down}
    names = list(mats)
    col_sharded = ("w_in", "w_q_b", "w_kv_b", "w_up")
    vw = ret_norm_g.shape[1]
    mla_w = mats["w_mla_o"].shape[0] * N_DEV
    ql, kl = q_a_norm_g.shape[1], kv_a_norm_g.shape[1]
    n_in = mats["w_in"].shape[1] * N_DEV
    qw = (n_in - 2 * vw - ql - kl - ROPE - 2 * d) // 2
    lay = _make_layout(d, vw, qw, ql, kl, mla_w)
    assert lay["n_in"] == n_in
    heads_r, heads_m = lay["ret_heads"], lay["mla_heads"]

    shard16 = {n: mats[n].astype(BF16) for n in names}
    with_in_proj = ("w_ret_o", "w_q_b", "w_kv_b", "w_mla_o", "w_out")
    mlp = ("w_up", "w_down")
    by_device = ("w_up", "w_kv_b")
    full = {}

    def keep(group, gathered):
        for n, g in zip(group, gathered):
            if n not in by_device:
                g = _cols_to_full(g) if n in col_sharded else g.reshape(-1, g.shape[2])
            full[n] = g

    w_mine = _w_in_to_mine(_exchange_alone("gather_w_in", _gather_exchange([shard16["w_in"]]))[0], lay)

    c64, s64 = _rope_tables(pos, RET_QK // 2)
    cos_r = jnp.concatenate([c64, c64], axis=1)
    sin_r = jnp.concatenate([-s64, s64], axis=1)
    c32, s32 = _rope_tables(pos, ROPE // 2)
    z32, z64 = jnp.zeros_like(c32), jnp.zeros((s, LANES - ROPE), F32)
    cos_p = jnp.concatenate([c32, c32, z64], axis=1)
    sin_a = jnp.concatenate([-s32, z32, z64], axis=1)
    sin_b = jnp.concatenate([z32, s32, z64], axis=1)
    lg = jnp.log(1.0 - 2.0 ** (-5.0 - jnp.arange(heads_r, dtype=F32)))
    lgs = jnp.broadcast_to(lg[:, None, None], (heads_r, 8, LANES))

    tm = min(256, s)
    blk = min(512, s)
    t_att = min(512, s)

    u = _rms_fwd("norm_mix", xs, norm_mix_g, tm)
    proj, gathered = _mm("in_proj", u, w_mine, "nn", F32,
                         ride=_gather_exchange([shard16[n] for n in with_in_proj]))
    keep(with_in_proj, gathered)
    wq_pad = jnp.pad(full["w_q_b"].reshape(ql, heads_m, NOPE + ROPE),
                     ((0, 0), (0, 0), (0, QPAD - NOPE - ROPE))).reshape(ql, heads_m * QPAD)
    o_ret, states = _ret_fwd(proj, lay, cos_r, sin_r, lgs, blk)
    ry = _ret_post(proj, lay, o_ret, ret_norm_g, tm)
    y_ret = _mm("ret_out", ry, full["w_ret_o"], "nn", F32)
    cqn, ckvn, kpr = _mla_prep(proj, lay, q_a_norm_g, kv_a_norm_g, cos_p, sin_a, sin_b, tm)
    qp = _mm("q_up", cqn, wq_pad, "nn", F32)
    kv = _mm("kv_up", ckvn, full["w_kv_b"], "nn", BF16, b_by_device=True)
    qf, kf = _attn_prep(qp, kv, kpr, lay, cos_p, sin_a, sin_b, tm)
    o_mla, lse, gathered = _attn_fwd(qf, kf, kv, lay, t_att, ride=_gather_exchange([shard16["w_up"]]))
    keep(("w_up",), gathered)
    y_mla = _mm("mla_out", o_mla, full["w_mla_o"], "nn", F32)
    merged = _merge(proj, lay, y_ret, y_mla, tm)
    mix = _mm("out_proj", merged, full["w_out"], "nn", F32)
    h1, n2 = _rms_res_fwd("norm_mlp", xs, mix, norm_mlp_g, tm)
    (z, act), gathered = _mm("mlp_up", n2, full["w_up"], "nn", (F32, BF16), b_by_device=True,
                             epilogue=lambda r: (r, jnp.square(jnp.maximum(r, 0.0))),
                             ride=_gather_exchange([shard16["w_down"]]))
    keep(("w_down",), gathered)
    dn = _mm("mlp_down", act, full["w_down"], "nn", F32)
    dh2, g_norm_f, loss_part = _final("loss_head", h1, dn, norm_f_g.reshape(1, d), tgt, tm)

    mx, my, mc = _place()
    c_arr = jnp.reshape(mc, (1,)).astype(jnp.int32)
    chip_arr = jnp.reshape(2 * mx + my, (1,)).astype(jnp.int32)
    sums, from_chips = {}, {}

    def blocks(group, grads):
        return [g if n in by_device else (_full_to_cols(g) if n in col_sharded else g.reshape((N_DEV,) + mats[n].shape))
                for n, g in zip(group, grads)]

    def pair_sums(group, mine, from_sibling):
        for n, g, r in zip(group, mine, from_sibling):
            sums[n] = _pair_sum("pair_sum_" + n, g, r, c_arr, 256)
        return [sums[n] for n in group]

    dz = _mm("mlp_down_dx", dh2, full["w_down"], "nt", BF16, extras=(z,),
             epilogue=lambda r, zz: (r * (2.0 * jnp.maximum(zz, 0.0)),))
    g_w_down = _mm("mlp_down_dw", act, dh2, "tn", BF16)
    down_blocks = blocks(("w_down",), (g_w_down,))
    g_w_up, got_down = _mm("mlp_up_dw", n2, dz, "tn", BF16, out_by_device=True, ride=_sibling_exchange(down_blocks))
    dn2, got_up = _mm("mlp_up_dx", dz, full["w_up"], "nt", F32, b_by_device=True, ride=_sibling_exchange([g_w_up]))
    mlp_sums = pair_sums(mlp, [g_w_up] + down_blocks, list(got_up) + list(got_down))
    dh1, g_norm_mlp = _rms_bwd("norm_mlp_bwd", dn2, h1, norm_mlp_g, dh2, tm)
    d_merged = _mm("out_proj_dx", dh1, full["w_out"], "nt", F32)
    g_w_out = _mm("out_proj_dw", merged, dh1, "tn", BF16)
    dy_ret, dy_mla, d_proj = _merge_bwd(proj, lay, d_merged, y_ret, y_mla, tm)
    g_w_ret_o = _mm("ret_out_dw", ry, dy_ret, "tn", BF16)
    g_w_mla_o = _mm("mla_out_dw", o_mla, dy_mla, "tn", BF16)
    mixer = ("w_out", "w_ret_o", "w_mla_o")
    mixer_blocks = blocks(mixer, (g_w_out, g_w_ret_o, g_w_mla_o))
    d_ry, got = _mm("ret_out_dx", dy_ret, full["w_ret_o"], "nt", F32, ride=_sibling_exchange(mixer_blocks))
    mixer_sums = pair_sums(mixer, mixer_blocks, got)
    d_omla = _mm("mla_out_dx", dy_mla, full["w_mla_o"], "nt", F32)
    d_oret, d_proj, g_ret_norm = _ret_post_bwd(proj, lay, d_ry, o_ret, ret_norm_g, d_proj, tm)
    d_proj, got = _ret_bwd(proj, lay, cos_r, sin_r, lgs, states, d_oret, d_proj, blk,
                           ride=_chips_exchange(mixer_sums))
    from_chips.update(zip(mixer, got))
    delta = _attn_delta(d_omla, o_mla, lay, t_att)
    dqf, dkf, dv, got = _attn_bwd(qf, kf, kv, lse, delta, d_omla, lay, t_att, ride=_chips_exchange(mlp_sums))
    from_chips.update(zip(mlp, got))
    dqp, dkv, d_proj = _attn_post_bwd(dqf, dkf, dv, lay, cos_p, sin_a, sin_b, d_proj, tm)
    d_cqn = _mm("q_up_dx", dqp, wq_pad, "nt", F32)
    g_wq_pad = _mm("q_up_dw", cqn, dqp, "tn", BF16)
    d_ckvn = _mm("kv_up_dx", dkv, full["w_kv_b"], "nt", F32, b_by_device=True)
    g_w_kv_b = _mm("kv_up_dw", ckvn, dkv, "tn", BF16, out_by_device=True)
    d_proj, g_q_a, g_kv_a = _mla_prep_bwd(proj, lay, d_cqn, d_ckvn, q_a_norm_g, kv_a_norm_g, d_proj, tm)
    g_w_mine = _mm("in_proj_dw", u, d_proj, "tn", BF16)
    g_w_q_b = g_wq_pad.reshape(ql, heads_m, QPAD)[:, :, :NOPE + ROPE].reshape(ql, heads_m * (NOPE + ROPE))
    last = ("w_in", "w_q_b", "w_kv_b")
    last_blocks = [_mine_to_blocks(g_w_mine, lay)] + blocks(last[1:], (g_w_q_b, g_w_kv_b))
    last_sums = pair_sums(last, last_blocks, _exchange_alone("grads_to_sibling", _sibling_exchange(last_blocks)))
    du, got = _mm("in_proj_dx", d_proj, w_mine, "nt", F32, ride=_chips_exchange(last_sums))
    from_chips.update(zip(last, got))
    grad_x, g_norm_mix = _rms_bwd("norm_mix_bwd", du, xs, norm_mix_g, dh1, tm)

    upd = {n: _adamw_shard("adamw_" + n, mat_w[n], mat_m[n], mat_v[n], sums[n], from_chips[n], chip_arr, 256)
           for n in names}

    gains = [("norm_mix_g", norm_mix_g, m_norm_mix_g, v_norm_mix_g, g_norm_mix),
             ("ret_norm_g", ret_norm_g, m_ret_norm_g, v_ret_norm_g, g_ret_norm),
             ("q_a_norm_g", q_a_norm_g, m_q_a_norm_g, v_q_a_norm_g, g_q_a),
             ("kv_a_norm_g", kv_a_norm_g, m_kv_a_norm_g, v_kv_a_norm_g, g_kv_a),
             ("norm_mlp_g", norm_mlp_g, m_norm_mlp_g, v_norm_mlp_g, g_norm_mlp),
             ("norm_f_g", norm_f_g, m_norm_f_g, v_norm_f_g, g_norm_f)]
    n_rows = sum(g[1].size for g in gains) // LANES
    pad_rows = -(-(n_rows + 1) // 8) * 8 - n_rows
    tail = jnp.zeros((pad_rows, LANES), F32)
    part = jnp.concatenate([_pack_rows([g[4] for g in gains]),
                            jnp.broadcast_to(loss_part[:, :1], (1, LANES)), tail[1:]], axis=0)
    packed = [jnp.concatenate([_pack_rows([g[k] for g in gains]), tail], axis=0) for k in (1, 2, 3)]
    g_sm, d_sm, m_sm, v_sm = _small_all_reduce_adam(part, *packed)
    loss = g_sm[n_rows, 0]
    small = {}
    o = 0
    for name, w, _, _, _ in gains:
        r = w.size // LANES
        small[name] = [a[o:o + r].reshape(w.shape) for a in (g_sm, d_sm, m_sm, v_sm)]
        o += r

    order = ["norm_mix_g", "w_in", "ret_norm_g", "w_ret_o", "q_a_norm_g", "w_q_b", "kv_a_norm_g", "w_kv_b", "w_mla_o",
             "w_out", "norm_mlp_g", "w_up", "w_down", "norm_f_g"]
    outs = [loss, grad_x[None]]
    for k in range(4):
        for n in order:
            outs.append(small[n][k] if n in small else upd[n][k])
    return tuple(outs)
```

```python
import functools
import math

import jax
import jax.numpy as jnp
from jax import lax
from jax.experimental import pallas as pl
from jax.experimental.pallas import tpu as pltpu

F32 = jnp.float32
BF16 = jnp.bfloat16
MESH = pl.DeviceIdType.MESH

EPS = 1e-6
ROPE_THETA = 10000.0
CHUNK_SHIFT = 6
RET_QK = 128
RET_V = 256
NOPE = 128
ROPE = 64
VHEAD = 128
QPAD = 256
LANES = 128
N_DEV = 8
VMEM_LIMIT = 56 * 1024 * 1024

ADAM_LR = 0.001
ADAM_B1 = 0.9
ADAM_B2 = 0.999
ADAM_EPS = 1e-08
ADAM_WD = 0.01
ADAM_STEP = 10

NN = (((1,), (0,)), ((), ()))
NT = (((1,), (1,)), ((), ()))
TN = (((0,), (0,)), ((), ()))


def _dot(a, b, dims):
    return lax.dot_general(a.astype(BF16), b.astype(BF16), dims, preferred_element_type=F32)


def _tile(dim, pref):
    if dim <= pref:
        return dim
    t = (pref // LANES) * LANES
    while t >= LANES:
        if dim % t == 0:
            return t
        t -= LANES
    raise ValueError(f"no tile for {dim}")


def _params(sem):
    return pltpu.CompilerParams(dimension_semantics=sem, vmem_limit_bytes=VMEM_LIMIT)


def _sig(v):
    return 1.0 / (1.0 + jnp.exp(-v))


def _mm(name, a, b, mode, out_dtypes, *, tm=1024, tn=1024, tk=2048, extras=(), epilogue=None, ride=None,
        b_by_device=False, out_by_device=False):
    if b_by_device:
        b_cols = b.shape[2]
        b_shape = (b.shape[1], N_DEV * b_cols)
    else:
        b_shape = b.shape
    if mode == "nn":
        (m, k), (_, n) = a.shape, b_shape
    elif mode == "nt":
        (m, k), (n, _) = a.shape, b_shape
    else:
        (k, m), (_, n) = a.shape, b_shape
    tm, tn, tk = _tile(m, tm), _tile(n, tn), _tile(k, tk)
    if b_by_device and mode != "nt":
        tn = _tile(b_cols, tn)
    if out_by_device:
        tn = _tile(n // N_DEV, tn)
    nk = k // tk
    dims = {"nn": NN, "nt": NT, "tn": TN}[mode]
    a_spec = (pl.BlockSpec((tk, tm), lambda i, j, kk: (kk, i)) if mode == "tn"
              else pl.BlockSpec((tm, tk), lambda i, j, kk: (i, kk)))
    if b_by_device and mode == "nt":
        piece = min(tk, b_cols)
        n_b, per = tk // piece, b_cols // piece
        b_specs = [pl.BlockSpec((None, tn, piece),
                                lambda i, j, kk, p=p: ((kk * n_b + p) // per, j, (kk * n_b + p) % per))
                   for p in range(n_b)]
    elif b_by_device:
        per = b_cols // tn
        n_b, piece = 1, tk
        b_specs = [pl.BlockSpec((None, tk, tn), lambda i, j, kk: (j // per, kk, j % per))]
    else:
        n_b, piece = 1, tk
        b_specs = [pl.BlockSpec((tn, tk), lambda i, j, kk: (j, kk)) if mode == "nt"
                   else pl.BlockSpec((tk, tn), lambda i, j, kk: (kk, j))]
    tile_spec = pl.BlockSpec((tm, tn), lambda i, j, kk: (i, j))
    if out_by_device:
        per_out = n // N_DEV // tn
        out_spec = pl.BlockSpec((None, tm, tn), lambda i, j, kk: (j // per_out, i, j % per_out))
        out_dims = (N_DEV, m, n // N_DEV)
    else:
        out_spec, out_dims = tile_spec, (m, n)
    ex_arrays, ex_specs = [], []
    for e in extras:
        arr, off = e if isinstance(e, tuple) else (e, 0)
        ex_arrays.append(arr)
        ex_specs.append(pl.BlockSpec((tm, arr.shape[1]), lambda i, j, kk: (i, 0)) if off is None
                        else pl.BlockSpec((tm, tn), lambda i, j, kk, off=off: (i, j + off)))
    n_ex = len(extras)
    single = not isinstance(out_dtypes, (tuple, list))
    dts = (out_dtypes,) if single else tuple(out_dtypes)

    grid = (m // tm, n // tn, nk)
    r_in, r_out, r_sc = ride.counts() if ride else (0, 0, 0)
    n_acc = 1 if nk > 1 else 0

    def body(a_ref, *rest):
        b_refs, rest = rest[:n_b], rest[n_b:]
        ex, rest = rest[:n_ex], rest[n_ex:]
        ride_in, rest = rest[:r_in], rest[r_in:]
        outs, rest = rest[:len(dts)], rest[len(dts):]
        ride_out, rest = rest[:r_out], rest[r_out:]
        ride_scratch = rest[n_acc:]
        if ride:
            first, mid, last = _steps([pl.program_id(d) for d in range(3)], grid)
            ride.run(ride_in, ride_out, ride_scratch, (first, mid, None))

        def finish(r):
            vals = (r,) if epilogue is None else epilogue(r, *[e[...] for e in ex])
            for o, v in zip(outs, vals):
                o[...] = v.astype(o.dtype)

        if n_b == 1:
            part = _dot(a_ref[...], b_refs[0][...], dims)
        else:
            part = sum(_dot(a_ref[:, p * piece:(p + 1) * piece], b_refs[p][...], dims) for p in range(n_b))
        if nk == 1:
            finish(part)
        else:
            acc = rest[0]
            kk = pl.program_id(2)

            @pl.when(kk == 0)
            def _():
                acc[...] = part

            @pl.when(jnp.logical_and(kk > 0, kk < nk - 1))
            def _():
                acc[...] += part

            @pl.when(kk == nk - 1)
            def _():
                finish(acc[...] + part)

        if ride:
            ride.run(ride_in, ride_out, ride_scratch, (None, None, last))

    res = pl.pallas_call(
        body, name=name, grid=grid,
        in_specs=[a_spec] + b_specs + ex_specs + [ANY] * r_in,
        out_specs=[out_spec] * len(dts) + [ANY] * r_out,
        out_shape=[jax.ShapeDtypeStruct(out_dims, d) for d in dts] + (ride.out_shape if ride else []),
        scratch_shapes=([pltpu.VMEM((tm, tn), F32)] if nk > 1 else []) + (ride.scratch if ride else []),
        compiler_params=_params(("arbitrary",) * 3 if ride else ("parallel", "parallel", "arbitrary")),
    )(a, *[b] * n_b, *ex_arrays, *(ride.ins if ride else []))
    own = res[0] if single else res[:len(dts)]
    return (own, res[len(dts):]) if ride else own


def _rows(name, body, n_rows, tm, ins, outs, accs=(), into=None):
    in_specs, args = [], []
    for t in ins:
        if len(t) == 1:
            in_specs.append(pl.BlockSpec(t[0].shape, lambda i, nd=t[0].ndim: (0,) * nd))
        else:
            in_specs.append(pl.BlockSpec((tm, t[1]), lambda i, cb=t[2]: (i, cb)))
        args.append(t[0])
    outs = [(o + (o[0], 0))[:4] for o in outs]
    out_specs = [pl.BlockSpec((tm, w), lambda i, cb=cb: (i, cb)) for w, _, _, cb in outs]
    out_shape = [jax.ShapeDtypeStruct((n_rows, total), d) for _, d, total, _ in outs]
    aliases, kernel = {}, body
    if into is not None:
        arr, w, cb = into
        in_specs.append(ANY)
        args.append(arr)
        out_specs.append(pl.BlockSpec((tm, w), lambda i: (i, cb)))
        out_shape.append(jax.ShapeDtypeStruct(arr.shape, arr.dtype))
        aliases = {len(ins): len(outs)}
        n_in = len(ins)

        def kernel(*refs):
            body(*refs[:n_in], *refs[n_in + 1:])

    out_specs += [pl.BlockSpec((r, w), lambda i: (0, 0)) for r, w in accs]
    out_shape += [jax.ShapeDtypeStruct((r, w), F32) for r, w in accs]
    return pl.pallas_call(
        kernel, name=name, grid=(n_rows // tm,), in_specs=in_specs, out_specs=out_specs, out_shape=out_shape,
        input_output_aliases=aliases, compiler_params=_params(("arbitrary",) if accs else ("parallel",)),
    )(*args)


def _zero_first(*accs):
    @pl.when(pl.program_id(0) == 0)
    def _():
        for a in accs:
            a[...] = jnp.zeros_like(a)


def _rope64(t, cos, sin):
    return t * cos + pltpu.roll(t, RET_QK // 2, 1) * sin


def _rope32(t, cos, sin_a, sin_b):
    return t * cos + pltpu.roll(t, LANES - ROPE // 2, 1) * sin_a + pltpu.roll(t, ROPE // 2, 1) * sin_b


def _rms_fwd(name, x, g, tm):
    s, d = x.shape

    def body(x_ref, g_ref, u_ref):
        v = x_ref[...]
        r = lax.rsqrt(jnp.mean(v * v, axis=-1, keepdims=True) + EPS)
        u_ref[...] = (v * r * g_ref[...]).astype(BF16)

    return _rows(name, body, s, tm, [(x, d, 0), (g,)], [(d, BF16)])[0]


def _rms_res_fwd(name, x, mix, g, tm):
    s, d = x.shape

    def body(x_ref, m_ref, g_ref, h_ref, u_ref):
        v = x_ref[...] + m_ref[...]
        h_ref[...] = v
        r = lax.rsqrt(jnp.mean(v * v, axis=-1, keepdims=True) + EPS)
        u_ref[...] = (v * r * g_ref[...]).astype(BF16)

    return _rows(name, body, s, tm, [(x, d, 0), (mix, d, 0), (g,)], [(d, F32), (d, BF16)])


def _rms_bwd(name, dy, x, g, dres, tm):
    s, d = x.shape

    def body(dy_ref, x_ref, g_ref, dres_ref, dx_ref, dg_ref):
        _zero_first(dg_ref)
        v, dyv = x_ref[...], dy_ref[...]
        r = lax.rsqrt(jnp.mean(v * v, axis=-1, keepdims=True) + EPS)
        xh = v * r
        dxh = dyv * g_ref[...]
        dx_ref[...] = dres_ref[...] + r * (dxh - xh * jnp.mean(dxh * xh, axis=-1, keepdims=True))
        dg_ref[...] += jnp.sum(dyv * xh, axis=0, keepdims=True)

    return _rows(name, body, s, tm, [(dy, d, 0), (x, d, 0), (g,), (dres, d, 0)], [(d, F32)], [(1, d)])


def _final(name, h1, dn, g, tgt, tm):
    s, d = h1.shape

    def body(h_ref, dn_ref, g_ref, t_ref, dh_ref, dg_ref, loss_ref):
        _zero_first(dg_ref, loss_ref)
        v = h_ref[...] + dn_ref[...]
        r = lax.rsqrt(jnp.mean(v * v, axis=-1, keepdims=True) + EPS)
        xh = v * r
        gv = g_ref[...]
        e = xh * gv - t_ref[...]
        loss_ref[...] += 0.5 * jnp.sum(jnp.mean(e * e, axis=-1, keepdims=True))
        dy = e * (1.0 / d)
        dg_ref[...] += jnp.sum(dy * xh, axis=0, keepdims=True)
        dxh = dy * gv
        dh_ref[...] = r * (dxh - xh * jnp.mean(dxh * xh, axis=-1, keepdims=True))

    return _rows(name, body, s, tm, [(h1, d, 0), (dn, d, 0), (g,), (tgt, d, 0)], [(d, F32)], [(1, d), (1, LANES)])


def _decay_mask(lg, blk):
    n = lax.broadcasted_iota(jnp.int32, (blk, blk), 0)
    m = lax.broadcasted_iota(jnp.int32, (blk, blk), 1)
    w = jnp.exp(lg * jnp.abs(n - m).astype(F32))
    return jnp.where(jnp.right_shift(m, CHUNK_SHIFT) <= jnp.right_shift(n, CHUNK_SHIFT), w, 0.0)


def _decays(lg, blk):
    pos = lax.broadcasted_iota(jnp.int32, (blk, 1), 0).astype(F32)
    return jnp.exp(lg * (pos + 1.0)), jnp.exp(lg * (blk - 1.0 - pos)), jnp.exp(lg * float(blk))


def _ret_fwd(proj, lay, cos, sin, lgs, blk, ride=None):
    s = proj.shape[0]
    heads = lay["ret_heads"]
    nb = s // blk
    scale = RET_QK ** -0.5
    ride_in_specs, ride_ins, ride_out_specs, ride_out_shape, ride_scratch = _ride_args(ride)

    def body(lg_ref, qkv_ref, cos_ref, sin_ref, o_ref, st_ref, state, mask):
        lg = lg_ref[0:1, 0:1]

        @pl.when(pl.program_id(1) == 0)
        def _():
            state[...] = jnp.zeros_like(state)
            mask[...] = _decay_mask(lg, blk)

        a, c, gb = _decays(lg, blk)
        q = _rope64(qkv_ref[:, :RET_QK], cos_ref[...], sin_ref[...])
        k = _rope64(qkv_ref[:, RET_QK:2 * RET_QK], cos_ref[...], sin_ref[...]) * scale
        v = qkv_ref[:, 2 * RET_QK:]
        st = state[...]
        st_ref[...] = st
        sm = _dot(q, k, NT) * mask[...]
        o_ref[...] = _dot(sm, v, NN) + _dot(q * a, st, NN)
        state[...] = st * gb + _dot(k * c, v, TN)

    first = lay["off"]["heads"] // RET_HEAD
    res = pl.pallas_call(
        _with_ride(body, ride, (heads, nb), 0, 4, 2), name="ret_fwd", grid=(heads, nb),
        in_specs=[pl.BlockSpec((None, 8, LANES), lambda h, b: (h, 0, 0)),
                  pl.BlockSpec((blk, RET_HEAD), lambda h, b: (b, first + h)),
                  pl.BlockSpec((blk, LANES), lambda h, b: (b, 0)),
                  pl.BlockSpec((blk, LANES), lambda h, b: (b, 0))] + ride_in_specs,
        out_specs=[pl.BlockSpec((blk, RET_V), lambda h, b: (b, h)),
                   pl.BlockSpec((None, None, RET_QK, RET_V), lambda h, b: (h, b, 0, 0))] + ride_out_specs,
        out_shape=[jax.ShapeDtypeStruct((s, heads * RET_V), F32),
                   jax.ShapeDtypeStruct((heads, nb, RET_QK, RET_V), F32)] + ride_out_shape,
        scratch_shapes=[pltpu.VMEM((RET_QK, RET_V), F32), pltpu.VMEM((blk, blk), F32)] + ride_scratch,
        compiler_params=_params(("arbitrary", "arbitrary") if ride else ("parallel", "arbitrary")),
    )(lgs, proj, cos, sin, *ride_ins)
    return (res[0], res[1], res[2:]) if ride else res


def _ret_bwd(proj, lay, cos, sin, lgs, states, d_o, d_proj, blk, ride=None):
    ride_in_specs, ride_ins, ride_out_specs, ride_out_shape, ride_scratch = _ride_args(ride)
    s = proj.shape[0]
    heads = lay["ret_heads"]
    nb = s // blk
    scale = RET_QK ** -0.5

    def body(lg_ref, qkv_ref, cos_ref, sin_ref, st_ref, do_ref, _, dqkv_ref, dstate, mask):
        lg = lg_ref[0:1, 0:1]

        @pl.when(pl.program_id(1) == 0)
        def _():
            dstate[...] = jnp.zeros_like(dstate)
            mask[...] = _decay_mask(lg, blk)

        a, c, gb = _decays(lg, blk)
        cs, sn = cos_ref[...], sin_ref[...]
        q = _rope64(qkv_ref[:, :RET_QK], cs, sn)
        k = _rope64(qkv_ref[:, RET_QK:2 * RET_QK], cs, sn) * scale
        v = qkv_ref[:, 2 * RET_QK:]
        st = st_ref[...]
        do = do_ref[...]
        dst = dstate[...]
        mk = mask[...]
        sm = _dot(q, k, NT) * mk
        ds = _dot(do, v, NT) * mk
        dq = _dot(ds, k, NN) + _dot(do, st, NT) * a
        dk = _dot(ds, q, TN) + _dot(v, dst, NT) * c
        dqkv_ref[:, 2 * RET_QK:] = (_dot(sm, do, TN) + _dot(k * c, dst, NN)).astype(BF16)
        dstate[...] = dst * gb + _dot(q * a, do, TN)
        dqkv_ref[:, :RET_QK] = _rope64(dq, cs, -sn).astype(BF16)
        dqkv_ref[:, RET_QK:2 * RET_QK] = (_rope64(dk, cs, -sn) * scale).astype(BF16)

    first = lay["off"]["heads"] // RET_HEAD
    last = nb - 1
    res = pl.pallas_call(
        _with_ride(body, ride, (heads, nb), 0, 7, 1), name="ret_bwd", grid=(heads, nb),
        in_specs=[pl.BlockSpec((None, 8, LANES), lambda h, b: (h, 0, 0)),
                  pl.BlockSpec((blk, RET_HEAD), lambda h, b: (last - b, first + h)),
                  pl.BlockSpec((blk, LANES), lambda h, b: (last - b, 0)),
                  pl.BlockSpec((blk, LANES), lambda h, b: (last - b, 0)),
                  pl.BlockSpec((None, None, RET_QK, RET_V), lambda h, b: (h, last - b, 0, 0)),
                  pl.BlockSpec((blk, RET_V), lambda h, b: (last - b, h)), ANY] + ride_in_specs,
        out_specs=[pl.BlockSpec((blk, RET_HEAD), lambda h, b: (last - b, first + h))] + ride_out_specs,
        out_shape=[jax.ShapeDtypeStruct(d_proj.shape, d_proj.dtype)] + ride_out_shape,
        scratch_shapes=[pltpu.VMEM((RET_QK, RET_V), F32), pltpu.VMEM((blk, blk), F32)] + ride_scratch,
        input_output_aliases={6: 0},
        compiler_params=_params(("arbitrary", "arbitrary") if ride else ("parallel", "arbitrary")),
    )(lgs, proj, cos, sin, states, d_o, d_proj, *ride_ins)
    return (res[0], res[1:]) if ride else res[0]


def _ret_post(proj, lay, o, g, tm):
    s, vw = o.shape
    heads = lay["ret_heads"]

    def body(o_ref, rg_ref, g_ref, ry_ref):
        for h in range(heads):
            sl = slice(h * RET_V, (h + 1) * RET_V)
            oh = o_ref[:, sl]
            dlt = oh - jnp.mean(oh, axis=-1, keepdims=True)
            rstd = lax.rsqrt(jnp.mean(dlt * dlt, axis=-1, keepdims=True) + EPS)
            rg = rg_ref[:, sl]
            ry_ref[:, sl] = (dlt * rstd * g_ref[:, sl] * (rg * _sig(rg))).astype(BF16)

    return _rows("ret_post", body, s, tm, [(o, vw, 0), (proj, vw, lay["off"]["r_g"] // vw), (g,)], [(vw, BF16)])[0]


def _ret_post_bwd(proj, lay, d_ry, o, g, d_proj, tm):
    s, vw = o.shape
    heads = lay["ret_heads"]

    def body(dry_ref, o_ref, rg_ref, g_ref, do_ref, drg_ref, dg_ref):
        _zero_first(dg_ref)
        for h in range(heads):
            sl = slice(h * RET_V, (h + 1) * RET_V)
            oh = o_ref[:, sl]
            dlt = oh - jnp.mean(oh, axis=-1, keepdims=True)
            rstd = lax.rsqrt(jnp.mean(dlt * dlt, axis=-1, keepdims=True) + EPS)
            oh = dlt * rstd
            gv = g_ref[:, sl]
            rg = rg_ref[:, sl]
            sg = _sig(rg)
            dry = dry_ref[:, sl]
            dt = dry * (rg * sg)
            drg_ref[:, sl] = (dry * (oh * gv) * (sg * (1.0 + rg * (1.0 - sg)))).astype(BF16)
            dg_ref[:, sl] += jnp.sum(dt * oh, axis=0, keepdims=True)
            doh = dt * gv
            do_ref[:, sl] = rstd * (doh - jnp.mean(doh, axis=-1, keepdims=True)
                                    - oh * jnp.mean(doh * oh, axis=-1, keepdims=True))

    return _rows("ret_post_bwd", body, s, tm,
                 [(d_ry, vw, 0), (o, vw, 0), (proj, vw, lay["off"]["r_g"] // vw), (g,)],
                 [(vw, F32)], [(1, vw)], into=(d_proj, vw, lay["off"]["r_g"] // vw))


def _mla_prep(proj, lay, gq, gkv, cos, sin_a, sin_b, tm):
    s = proj.shape[0]
    ql, kl = lay["q_lora"], lay["kv_lora"]

    def body(cq_ref, ckv_ref, kpe_ref, gq_ref, gkv_ref, cos_ref, sa_ref, sb_ref, cqn_ref, ckvn_ref, kpr_ref):
        for src, gref, dst in ((cq_ref, gq_ref, cqn_ref), (ckv_ref, gkv_ref, ckvn_ref)):
            v = src[...]
            r = lax.rsqrt(jnp.mean(v * v, axis=-1, keepdims=True) + EPS)
            dst[...] = (v * r * gref[...]).astype(BF16)
        kpr_ref[...] = _rope32(kpe_ref[...], cos_ref[...], sa_ref[...], sb_ref[...]).astype(BF16)

    off = lay["off"]
    return _rows("mla_prep", body, s, tm,
                 [(proj, ql, off["c_q"] // ql), (proj, kl, off["c_kv"] // kl), (proj, LANES, off["k_pe"] // LANES),
                  (gq,), (gkv,), (cos, LANES, 0), (sin_a, LANES, 0), (sin_b, LANES, 0)],
                 [(ql, BF16), (kl, BF16), (LANES, BF16)])


def _mla_prep_bwd(proj, lay, d_cqn, d_ckvn, gq, gkv, d_proj, tm):
    s = proj.shape[0]
    ql, kl = lay["q_lora"], lay["kv_lora"]

    def body(dq_ref, dkv_ref, cq_ref, ckv_ref, gq_ref, gkv_ref, dc_ref, dgq_ref, dgkv_ref):
        _zero_first(dgq_ref, dgkv_ref)
        for dref, src, gref, cols, dg in ((dq_ref, cq_ref, gq_ref, slice(0, ql), dgq_ref),
                                          (dkv_ref, ckv_ref, gkv_ref, slice(ql, ql + kl), dgkv_ref)):
            v, dy = src[...], dref[...]
            r = lax.rsqrt(jnp.mean(v * v, axis=-1, keepdims=True) + EPS)
            xh = v * r
            dxh = dy * gref[...]
            dc_ref[:, cols] = (r * (dxh - xh * jnp.mean(dxh * xh, axis=-1, keepdims=True))).astype(BF16)
            dg[...] += jnp.sum(dy * xh, axis=0, keepdims=True)

    off = lay["off"]
    return _rows("mla_prep_bwd", body, s, tm,
                 [(d_cqn, ql, 0), (d_ckvn, kl, 0), (proj, ql, off["c_q"] // ql), (proj, kl, off["c_kv"] // kl),
                  (gq,), (gkv,)],
                 [], [(1, ql), (1, kl)], into=(d_proj, ql + kl, off["c_q"] // (ql + kl)))


def _q_operand(r, cos, sin_a, sin_b):
    qs = (NOPE + ROPE) ** -0.5 * math.log2(math.e)
    cs, sa, sb = cos * qs, sin_a * qs, sin_b * qs
    parts = []
    for lo in range(0, r.shape[1], QPAD):
        parts += [r[:, lo:lo + NOPE] * qs, _rope32(r[:, lo + NOPE:lo + QPAD], cs, sa, sb)]
    return (jnp.concatenate(parts, axis=1),)


def _k_operand(r, kpr):
    parts = []
    for lo in range(0, r.shape[1], QPAD):
        parts += [r[:, lo:lo + NOPE], kpr.astype(F32)]
    return r, jnp.concatenate(parts, axis=1)


def _attn_post_bwd(dqf, dkf, dv, lay, cos, sin_a, sin_b, d_proj, tm):
    s = dqf.shape[0]
    heads = lay["mla_heads"]
    w = heads * QPAD

    def body(dqf_ref, dkf_ref, dv_ref, cos_ref, sa_ref, sb_ref, dqp_ref, dkv_ref, dkpe_ref):
        cs, sa, sb = cos_ref[...], -sa_ref[...], -sb_ref[...]
        kpe = jnp.zeros((tm, LANES), F32)
        for h in range(heads):
            lo, hi = h * QPAD, h * QPAD + NOPE
            dqp_ref[:, lo:hi] = dqf_ref[:, lo:hi].astype(BF16)
            dqp_ref[:, hi:hi + LANES] = _rope32(dqf_ref[:, hi:hi + LANES], cs, sa, sb).astype(BF16)
            dkv_ref[:, lo:hi] = dkf_ref[:, lo:hi].astype(BF16)
            dkv_ref[:, hi:hi + LANES] = dv_ref[:, h * VHEAD:(h + 1) * VHEAD].astype(BF16)
            kpe = kpe + dkf_ref[:, hi:hi + LANES]
        dkpe_ref[:, :LANES] = _rope32(kpe, cs, sa, sb).astype(BF16)
        dkpe_ref[:, LANES:] = jnp.zeros((tm, LANES), BF16)

    return _rows("attn_post_bwd", body, s, tm,
                 [(dqf, w, 0), (dkf, w, 0), (dv, heads * VHEAD, 0), (cos, LANES, 0), (sin_a, LANES, 0),
                  (sin_b, LANES, 0)],
                 [(w, BF16), (w, BF16)], into=(d_proj, 2 * LANES, lay["off"]["k_pe"] // (2 * LANES)))


def _diag_mask(t, keys_on_rows=False):
    row = lax.broadcasted_iota(jnp.int32, (t, t), 0)
    col = lax.broadcasted_iota(jnp.int32, (t, t), 1)
    key, query = (row, col) if keys_on_rows else (col, row)
    return jnp.right_shift(key, CHUNK_SHIFT) <= jnp.right_shift(query, CHUNK_SHIFT)


def _tile_pairs(nt, by_key):
    if by_key:
        pairs = [(i, j) for j in range(nt) for i in range(j, nt)]
    else:
        pairs = [(i, j) for i in range(nt) for j in range(i + 1)]
    return (jnp.asarray([p[0] for p in pairs], jnp.int32), jnp.asarray([p[1] for p in pairs], jnp.int32))


def _head_block(heads):
    return 4 if heads % 4 == 0 else 2 if heads % 2 == 0 else 1


def _attn_fwd(qf, kf, kv, lay, t, ride=None):
    s = qf.shape[0]
    heads = lay["mla_heads"]
    hb = _head_block(heads)
    nt = s // t
    qi, kj = _tile_pairs(nt, False)
    grid = (heads // hb, int(qi.shape[0]))
    ride_in_specs, ride_ins, ride_out_specs, ride_out_shape, ride_scratch = _ride_args(ride)

    def body(qi_ref, kj_ref, q_ref, k_ref, kv_ref, o_ref, lse_ref, m_s, l_s, acc):
        p = pl.program_id(1)
        i, j = qi_ref[p], kj_ref[p]

        @pl.when(j == 0)
        def _():
            m_s[...] = jnp.full_like(m_s, -jnp.inf)
            l_s[...] = jnp.zeros_like(l_s)
            acc[...] = jnp.zeros_like(acc)

        def step(diagonal):
            ones = jnp.ones((t, LANES), BF16)
            scores = [_dot(q_ref[:, hh * QPAD:(hh + 1) * QPAD], k_ref[:, hh * QPAD:(hh + 1) * QPAD], NT)
                      for hh in range(hb)]
            for hh in range(hb):
                sc = scores[hh]
                if diagonal:
                    sc = jnp.where(_diag_mask(t), sc, -jnp.inf)
                cols = [sc[:, c * LANES:(c + 1) * LANES] for c in range(t // LANES)]
                m_old = m_s[hh]
                m_new = jnp.maximum(m_old, jnp.max(functools.reduce(jnp.maximum, cols), axis=-1, keepdims=True))
                alpha = jnp.exp2(m_old - m_new)
                pr = jnp.concatenate([jnp.exp2(c - m_new).astype(BF16) for c in cols], axis=1)
                pv = _dot(pr, jnp.concatenate([kv_ref[:, hh * QPAD + NOPE:(hh + 1) * QPAD], ones], axis=1), NN)
                l_new = alpha * l_s[hh] + pv[:, VHEAD:]
                a_new = alpha * acc[hh] + pv[:, :VHEAD]
                if diagonal:
                    o_ref[:, hh * VHEAD:(hh + 1) * VHEAD] = a_new / l_new
                    lse_ref[hh] = jnp.transpose(m_new + jnp.log2(l_new))[:1]
                else:
                    m_s[hh], l_s[hh], acc[hh] = m_new, l_new, a_new

        pl.when(j < i)(functools.partial(step, False))
        pl.when(j == i)(functools.partial(step, True))

    res = pl.pallas_call(
        _with_ride(body, ride, grid, 2, 3, 2), name="attn_fwd",
        grid_spec=pltpu.PrefetchScalarGridSpec(
            num_scalar_prefetch=2, grid=grid,
            in_specs=[pl.BlockSpec((t, hb * QPAD), lambda h, p, qi, kj: (qi[p], h)),
                      pl.BlockSpec((t, hb * QPAD), lambda h, p, qi, kj: (kj[p], h)),
                      pl.BlockSpec((t, hb * QPAD), lambda h, p, qi, kj: (kj[p], h))] + ride_in_specs,
            out_specs=[pl.BlockSpec((t, hb * VHEAD), lambda h, p, qi, kj: (qi[p], h)),
                       pl.BlockSpec((hb, 1, t), lambda h, p, qi, kj: (h, 0, qi[p]))] + ride_out_specs,
            scratch_shapes=[pltpu.VMEM((hb, t, LANES), F32), pltpu.VMEM((hb, t, LANES), F32),
                            pltpu.VMEM((hb, t, VHEAD), F32)] + ride_scratch),
        out_shape=[jax.ShapeDtypeStruct((s, heads * VHEAD), F32),
                   jax.ShapeDtypeStruct((heads, 1, s), F32)] + ride_out_shape,
        compiler_params=_params(("arbitrary", "arbitrary") if ride else ("parallel", "arbitrary")),
    )(qi, kj, qf, kf, kv, *ride_ins)
    return (res[0], res[1], res[2:]) if ride else res


def _attn_delta(d_o, o, lay, tm):
    s = o.shape[0]
    heads = lay["mla_heads"]

    def body(do_ref, o_ref, dl_ref):
        for h in range(heads):
            sl = slice(h * VHEAD, (h + 1) * VHEAD)
            dl_ref[h] = jnp.sum(jnp.transpose(do_ref[:, sl] * o_ref[:, sl]), axis=0, keepdims=True)

    tile = pl.BlockSpec((tm, heads * VHEAD), lambda i: (i, 0))
    return pl.pallas_call(
        body, name="attn_delta", grid=(s // tm,), in_specs=[tile, tile],
        out_specs=pl.BlockSpec((heads, 1, tm), lambda i: (0, 0, i)),
        out_shape=jax.ShapeDtypeStruct((heads, 1, s), F32),
        compiler_params=_params(("parallel",)),
    )(d_o, o)


def _attn_bwd(qf, kf, kv, lse, delta, d_o, lay, t, ride=None):
    s = qf.shape[0]
    heads = lay["mla_heads"]
    hb = _head_block(heads)
    nt = s // t
    scale = (NOPE + ROPE) ** -0.5
    qi, kj = _tile_pairs(nt, True)
    grid = (heads // hb, int(qi.shape[0]))
    ride_in_specs, ride_ins, ride_out_specs, ride_out_shape, ride_scratch = _ride_args(ride)

    def body(qi_ref, kj_ref, q_ref, k_ref, kv_ref, lse_ref, dl_ref, do_ref, dq_ref, dk_ref, dv_ref, dk_acc, dv_acc):
        p = pl.program_id(1)
        i, j = qi_ref[p], kj_ref[p]
        rows = pl.ds(pl.multiple_of(i * t, t), t)

        @pl.when(p == 0)
        def _():
            dq_ref[...] = jnp.zeros_like(dq_ref)

        def step(diagonal):
            for hh in range(hb):
                lo = hh * QPAD
                q, k = q_ref[:, lo:lo + QPAD], k_ref[:, lo:lo + QPAD]
                do = do_ref[:, hh * VHEAD:(hh + 1) * VHEAD]
                pr = jnp.exp2(_dot(k, q, NT) - lse_ref[hh])
                if diagonal:
                    pr = jnp.where(_diag_mask(t, keys_on_rows=True), pr, 0.0)
                dv_part = _dot(pr, do, NN)
                ds = (pr * (_dot(kv_ref[:, lo + NOPE:lo + QPAD], do, NT) - dl_ref[hh])).astype(BF16)
                dk_part = _dot(ds, q, NN)
                dq_ref[rows, lo:lo + QPAD] += _dot(ds, k, TN) * scale
                if diagonal:
                    dk_acc[hh], dv_acc[hh] = dk_part, dv_part
                else:
                    dk_acc[hh] += dk_part
                    dv_acc[hh] += dv_part

        pl.when(i > j)(functools.partial(step, False))
        pl.when(i == j)(functools.partial(step, True))

        @pl.when(i == nt - 1)
        def _():
            for hh in range(hb):
                dk_ref[:, hh * QPAD:(hh + 1) * QPAD] = dk_acc[hh] * math.log(2.0)
                dv_ref[:, hh * VHEAD:(hh + 1) * VHEAD] = dv_acc[hh]

    res = pl.pallas_call(
        _with_ride(body, ride, grid, 2, 6, 3), name="attn_bwd",
        grid_spec=pltpu.PrefetchScalarGridSpec(
            num_scalar_prefetch=2, grid=grid,
            in_specs=[pl.BlockSpec((t, hb * QPAD), lambda h, p, qi, kj: (qi[p], h)),
                      pl.BlockSpec((t, hb * QPAD), lambda h, p, qi, kj: (kj[p], h)),
                      pl.BlockSpec((t, hb * QPAD), lambda h, p, qi, kj: (kj[p], h)),
                      pl.BlockSpec((hb, 1, t), lambda h, p, qi, kj: (h, 0, qi[p])),
                      pl.BlockSpec((hb, 1, t), lambda h, p, qi, kj: (h, 0, qi[p])),
                      pl.BlockSpec((t, hb * VHEAD), lambda h, p, qi, kj: (qi[p], h))] + ride_in_specs,
            out_specs=[pl.BlockSpec((s, hb * QPAD), lambda h, p, qi, kj: (0, h)),
                       pl.BlockSpec((t, hb * QPAD), lambda h, p, qi, kj: (kj[p], h)),
                       pl.BlockSpec((t, hb * VHEAD), lambda h, p, qi, kj: (kj[p], h))] + ride_out_specs,
            scratch_shapes=[pltpu.VMEM((hb, t, QPAD), F32), pltpu.VMEM((hb, t, VHEAD), F32)] + ride_scratch),
        out_shape=[jax.ShapeDtypeStruct((s, heads * QPAD), F32),
                   jax.ShapeDtypeStruct((s, heads * QPAD), F32),
                   jax.ShapeDtypeStruct((s, heads * VHEAD), F32)] + ride_out_shape,
        compiler_params=_params(("arbitrary", "arbitrary") if ride else ("parallel", "arbitrary")),
    )(qi, kj, qf, kf, kv, lse, delta, d_o, *ride_ins)
    return (res[0], res[1], res[2], res[3:]) if ride else res


def _merge_bwd(proj, lay, d_merged, y_ret, y_mla, tm):
    s, d = y_ret.shape
    off = lay["off"]
    assert off["g_ret"] == 0 and off["g_mla"] == d

    def body(dm_ref, gr_ref, gm_ref, yr_ref, ym_ref, dyr_ref, dym_ref, dg_ref):
        dm = dm_ref[...]
        for g_ref, y_ref, dy_ref, cols in ((gr_ref, yr_ref, dyr_ref, slice(0, d)),
                                           (gm_ref, ym_ref, dym_ref, slice(d, 2 * d))):
            sg = _sig(g_ref[...])
            dy_ref[...] = (dm * sg).astype(BF16)
            dg_ref[:, cols] = (dm * y_ref[...] * (sg * (1.0 - sg))).astype(BF16)

    return _rows("merge_bwd", body, s, tm,
                 [(d_merged, d, 0), (proj, d, 0), (proj, d, 1), (y_ret, d, 0), (y_mla, d, 0)],
                 [(d, BF16), (d, BF16), (2 * d, BF16, lay["total"], 0)])


ANY = pl.BlockSpec(memory_space=pl.ANY)


def _place():
    return lax.axis_index("x"), lax.axis_index("y"), lax.axis_index("c")


def _other_chips(x, y):
    return [(1 - x, y), (x, 1 - y), (1 - x, 1 - y)]


class _Exchange:
    def __init__(self, ins, out_shape, scratch, phases):
        self.ins, self.out_shape, self.scratch, self.phases = list(ins), list(out_shape), list(scratch), phases

    def counts(self):
        return len(self.ins), len(self.out_shape), len(self.scratch)

    def run(self, r_in, r_out, r_scratch, conds):
        for cond, phase in zip(conds, self.phases):
            if phase is not None and cond is not None:
                pl.when(cond)(functools.partial(phase, r_in, r_out, r_scratch))


def _steps(ids, sizes):
    lin, total = 0, 1
    for i, n in zip(ids, sizes):
        lin, total = lin * n + i, total * n
    return lin == 0, lin == total // 2, lin == total - 1


def _ride_args(ride):
    if ride is None:
        return [], [], [], [], []
    n_in, n_out, _ = ride.counts()
    return [ANY] * n_in, ride.ins, [ANY] * n_out, ride.out_shape, ride.scratch


def _with_ride(body, ride, grid, n_prefetch, n_in, n_out):
    if ride is None:
        return body
    r_in, r_out, r_sc = ride.counts()

    def hosted(*refs):
        cuts = (n_prefetch, n_in, r_in, n_out, r_out)
        parts, pos = [], 0
        for n in cuts:
            parts.append(refs[pos:pos + n])
            pos += n
        pre, ins, ride_in, outs, ride_out = parts
        scratch, ride_scratch = refs[pos:len(refs) - r_sc], refs[len(refs) - r_sc:]
        first, mid, last = _steps([pl.program_id(d) for d in range(len(grid))], grid)
        ride.run(ride_in, ride_out, ride_scratch, (first, mid, None))
        body(*pre, *ins, *outs, *scratch)
        ride.run(ride_in, ride_out, ride_scratch, (None, None, last))

    return hosted


def _exchange_alone(name, ex):
    n_in, n_out, _ = ex.counts()

    def body(*refs):
        for phase in ex.phases:
            if phase is not None:
                phase(refs[:n_in], refs[n_in:n_in + n_out], refs[n_in + n_out:])

    return pl.pallas_call(
        body, name=name, in_specs=[ANY] * n_in, out_specs=[ANY] * n_out, out_shape=ex.out_shape,
        scratch_shapes=ex.scratch)(*ex.ins)


def _gather_exchange(shards):
    nw = len(shards)

    def parts(ins, outs, sems):
        send_sems, recv_sems, local_sems = sems
        x, y, c = _place()

        def slot(px, py, pc):
            return 4 * px + 2 * py + pc

        def copy(w, k, rows, to, src=None):
            return pltpu.make_async_remote_copy(
                src_ref=rows if src is None else src, dst_ref=rows, send_sem=send_sems.at[w, k],
                recv_sem=recv_sems.at[w, k], device_id=to, device_id_type=MESH)

        def plan(w, mine):
            side = c if mine else 1 - c
            half = shards[w].shape[0] // 2
            whole = lambda px, py: outs[w].at[slot(px, py, side)]
            top = lambda px, py: outs[w].at[slot(px, py, side), pl.ds(0, half)]
            bottom = lambda px, py: outs[w].at[slot(px, py, side), pl.ds(half, half)]
            xn, yn, sib = (1 - x, y, side), (x, 1 - y, side), (x, y, 1 - side)
            own = ins[w] if mine else None
            return [copy(w, 0, whole(x, y), sib, own), copy(w, 1, whole(x, y), xn, own),
                    copy(w, 2, whole(x, y), yn, own), copy(w, 3, top(1 - x, y), yn), copy(w, 4, bottom(x, 1 - y), xn),
                    copy(w, 5, whole(1 - x, y), sib), copy(w, 6, whole(x, 1 - y), sib),
                    copy(w, 7, top(1 - x, 1 - y), sib), copy(w, 8, bottom(1 - x, 1 - y), sib)]

        def arrivals(w):
            half = shards[w].shape[0] // 2
            at = lambda px, py, *rows: outs[w].at[(slot(px, py, c),) + rows]
            return {1: copy(w, 1, at(1 - x, y), (x, y, c)), 2: copy(w, 2, at(x, 1 - y), (x, y, c)),
                    3: copy(w, 3, at(1 - x, 1 - y, pl.ds(0, half)), (x, y, c)),
                    4: copy(w, 4, at(1 - x, 1 - y, pl.ds(half, half)), (x, y, c))}

        local = [pltpu.make_async_copy(ins[w], outs[w].at[slot(x, y, c)], local_sems.at[w]) for w in range(nw)]
        return plan, arrivals, local

    def start(ins, outs, sems):
        plan, _, local = parts(ins, outs, sems)
        for cp in local:
            cp.start()
        for w in range(nw):
            for k in (0, 1, 2):
                plan(w, True)[k].start()

    def middle(ins, outs, sems):
        plan, arrivals, _ = parts(ins, outs, sems)
        for landed, onward in ((1, (3, 5)), (2, (4, 6))):
            for w in range(nw):
                arrivals(w)[landed].wait_recv()
                for k in onward:
                    plan(w, True)[k].start()

    def finish(ins, outs, sems):
        plan, arrivals, local = parts(ins, outs, sems)
        for landed, onward in ((3, 7), (4, 8)):
            for w in range(nw):
                arrivals(w)[landed].wait_recv()
                plan(w, True)[onward].start()
        for w in range(nw):
            from_sibling = plan(w, False)
            for k in (0, 5, 6, 7, 8):
                from_sibling[k].wait_recv()
            for cp in plan(w, True):
                cp.wait_send()
        for cp in local:
            cp.wait()

    return _Exchange(
        shards, [jax.ShapeDtypeStruct((N_DEV,) + s.shape, s.dtype) for s in shards],
        [pltpu.SemaphoreType.DMA((nw, 9)), pltpu.SemaphoreType.DMA((nw, 9)), pltpu.SemaphoreType.DMA((nw,))],
        (start, middle, finish))


def _sibling_exchange(grads):
    nw = len(grads)

    def copies(ins, outs, sems):
        x, y, c = _place()
        return [pltpu.make_async_remote_copy(
            src_ref=ins[w].at[2 * p + (1 - c)], dst_ref=outs[w].at[p], send_sem=sems[0].at[w, p],
            recv_sem=sems[1].at[w, p], device_id=(x, y, 1 - c), device_id_type=MESH)
            for w in range(nw) for p in range(4)]

    def start(ins, outs, sems):
        for cp in copies(ins, outs, sems):
            cp.start()

    def finish(ins, outs, sems):
        for cp in copies(ins, outs, sems):
            cp.wait()

    return _Exchange(grads, [jax.ShapeDtypeStruct((4,) + g.shape[1:], g.dtype) for g in grads],
                     [pltpu.SemaphoreType.DMA((nw, 4)), pltpu.SemaphoreType.DMA((nw, 4))], (start, None, finish))


def _chips_exchange(sums):
    nw = len(sums)

    def copies(ins, outs, sems):
        x, y, c = _place()
        return [pltpu.make_async_remote_copy(
            src_ref=ins[w].at[2 * px + py], dst_ref=outs[w].at[k], send_sem=sems[0].at[w, k],
            recv_sem=sems[1].at[w, k], device_id=(px, py, c), device_id_type=MESH)
            for w in range(nw) for k, (px, py) in enumerate(_other_chips(x, y))]

    def start(ins, outs, sems):
        for cp in copies(ins, outs, sems):
            cp.start()

    def finish(ins, outs, sems):
        for cp in copies(ins, outs, sems):
            cp.wait()

    return _Exchange(sums, [jax.ShapeDtypeStruct((3,) + g.shape[1:], g.dtype) for g in sums],
                     [pltpu.SemaphoreType.DMA((nw, 3)), pltpu.SemaphoreType.DMA((nw, 3))], (start, None, finish))


def _pair_sum(name, g, got, c_arr, tr):
    _, rows, cols = g.shape
    tr = _tile_rows(rows, tr)

    def body(c_ref, a_ref, b_ref, o_ref):
        o_ref[...] = (a_ref[...].astype(F32) + b_ref[...].astype(F32)).astype(BF16)

    return pl.pallas_call(
        body, name=name,
        grid_spec=pltpu.PrefetchScalarGridSpec(
            num_scalar_prefetch=1, grid=(4, rows // tr),
            in_specs=[pl.BlockSpec((None, tr, cols), lambda p, r, cr: (2 * p + cr[0], r, 0)),
                      pl.BlockSpec((None, tr, cols), lambda p, r, cr: (p, r, 0))],
            out_specs=pl.BlockSpec((None, tr, cols), lambda p, r, cr: (p, r, 0))),
        out_shape=jax.ShapeDtypeStruct((4, rows, cols), BF16),
        compiler_params=_params(("parallel", "parallel")),
    )(c_arr, g, got)


def _tile_rows(rows, pref):
    t = min(rows, pref)
    while rows % t or t % 8:
        t -= 1
    return t


def _adam(w, g, m, v):
    m = ADAM_B1 * m + (1.0 - ADAM_B1) * g
    v = ADAM_B2 * v + (1.0 - ADAM_B2) * (g * g)
    m_hat = m / (1.0 - ADAM_B1 ** ADAM_STEP)
    v_hat = v / (1.0 - ADAM_B2 ** ADAM_STEP)
    return -ADAM_LR * (m_hat / (jnp.sqrt(v_hat) + ADAM_EPS) + ADAM_WD * w), m, v


def _adamw_shard(name, w, m, v, sums, got, chip_arr, tr):
    _, rows, cols = w.shape
    tr = _tile_rows(rows, tr)

    def body(p_ref, w_ref, m_ref, v_ref, s_ref, r_ref, g_ref, d_ref, nm_ref, nv_ref):
        g = s_ref[...].astype(F32)
        for k in range(3):
            g = g + r_ref[k].astype(F32)
        g_ref[...] = g
        d_ref[...], nm_ref[...], nv_ref[...] = _adam(w_ref[...], g, m_ref[...], v_ref[...])

    tile = pl.BlockSpec((None, tr, cols), lambda r, pr: (0, r, 0))
    return pl.pallas_call(
        body, name=name,
        grid_spec=pltpu.PrefetchScalarGridSpec(
            num_scalar_prefetch=1, grid=(rows // tr,),
            in_specs=[tile, tile, tile,
                      pl.BlockSpec((None, tr, cols), lambda r, pr: (pr[0], r, 0)),
                      pl.BlockSpec((3, tr, cols), lambda r, pr: (0, r, 0))],
            out_specs=[tile] * 4),
        out_shape=[jax.ShapeDtypeStruct((1, rows, cols), F32)] * 4,
        compiler_params=_params(("parallel",)),
    )(chip_arr, w, m, v, sums, got)


def _small_all_reduce_adam(part, w, m, v):
    rows = part.shape[0]

    def body(p_ref, w_ref, m_ref, v_ref, g_ref, d_ref, nm_ref, nv_ref, buf, send_sems, recv_sems):
        x, y, c = _place()
        me = 4 * x + 2 * y + c
        buf[me] = p_ref[...]
        peers = [(x, y, 1 - c)] + [(px, py, pc) for px, py in _other_chips(x, y) for pc in (c, 1 - c)]
        copies = []
        for k, peer in enumerate(peers):
            cp = pltpu.make_async_remote_copy(
                src_ref=buf.at[me], dst_ref=buf.at[me], send_sem=send_sems.at[k], recv_sem=recv_sems.at[k],
                device_id=peer, device_id_type=MESH)
            cp.start()
            copies.append(cp)
        for cp in copies:
            cp.wait()
        g = buf[0]
        for k in range(1, N_DEV):
            g = g + buf[k]
        g_ref[...] = g
        d_ref[...], nm_ref[...], nv_ref[...] = _adam(w_ref[...], g, m_ref[...], v_ref[...])

    vm = pl.BlockSpec(memory_space=pltpu.VMEM)
    return pl.pallas_call(
        body, name="gains_all_reduce_adamw",
        in_specs=[vm] * 4, out_specs=[vm] * 4,
        out_shape=[jax.ShapeDtypeStruct((rows, LANES), F32)] * 4,
        scratch_shapes=[pltpu.VMEM((N_DEV, rows, LANES), F32), pltpu.SemaphoreType.DMA((7,)),
                        pltpu.SemaphoreType.DMA((7,))],
        compiler_params=pltpu.CompilerParams(has_side_effects=True),
    )(part, w, m, v)


IN_ORDER = ("r_q", "r_k", "r_v", "r_g", "c_q", "c_kv", "k_pe", "g_ret", "g_mla")
RET_HEAD = 2 * RET_QK + RET_V


def _make_layout(d, vw, qw, ql, kl, mla_w):
    width = {"r_q": qw, "r_k": qw, "r_v": vw, "r_g": vw, "c_q": ql, "c_kv": kl, "k_pe": ROPE, "g_ret": d, "g_mla": d}
    src, o = {}, 0
    for n in IN_ORDER:
        src[n] = o
        o += width[n]
    heads = vw // RET_V
    off, pieces, o = {}, [], 0

    def put(name, w, s):
        nonlocal o
        off.setdefault(name, o)
        pieces.append((o, w, s))
        o += w

    for n in ("g_ret", "g_mla", "r_g"):
        put(n, width[n], src[n])
    for h in range(heads):
        put("heads", RET_QK, src["r_q"] + h * RET_QK)
        put("heads", RET_QK, src["r_k"] + h * RET_QK)
        put("heads", RET_V, src["r_v"] + h * RET_V)
    for n in ("c_q", "c_kv", "k_pe"):
        put(n, width[n], src[n])
    total = off["k_pe"] + 2 * LANES
    for n, blk in (("g_ret", d), ("g_mla", d), ("r_g", vw), ("heads", RET_HEAD), ("c_q", ql + kl), ("k_pe", 2 * LANES)):
        assert off[n] % blk == 0
    assert ql == kl and off["c_kv"] == off["c_q"] + ql
    return {"off": off, "pieces": pieces, "total": total, "n_in": sum(width.values()),
            "ret_heads": heads, "mla_heads": mla_w // VHEAD, "q_lora": ql, "kv_lora": kl}


def _cols_to_full(g):
    n, r, c = g.shape
    return jnp.transpose(g, (1, 0, 2)).reshape(r, n * c)


def _full_to_cols(w):
    r, c = w.shape
    return jnp.transpose(w.reshape(r, N_DEV, c // N_DEV), (1, 0, 2))


def _w_in_to_mine(g, lay):
    _, rows, cols = g.shape
    parts, at = [], 0
    for o, w, s in lay["pieces"]:
        if o > at:
            parts.append(jnp.zeros((rows, o - at), g.dtype))
        while w > 0:
            k, a = divmod(s, cols)
            take = min(w, cols - a)
            parts.append(g[k, :, a:a + take])
            s, w, o = s + take, w - take, o + take
        at = o
    parts.append(jnp.zeros((rows, lay["total"] - at), g.dtype))
    return jnp.concatenate(parts, axis=1)


def _mine_to_blocks(g, lay):
    cols = lay["n_in"] // N_DEV
    by_src = sorted(lay["pieces"], key=lambda p: p[2])
    blocks = []
    for k in range(N_DEV):
        lo, hi, parts = k * cols, (k + 1) * cols, []
        for o, w, s in by_src:
            a, b = max(lo, s), min(hi, s + w)
            if a < b:
                parts.append(g[:, o + a - s:o + b - s])
        blocks.append(jnp.concatenate(parts, axis=1))
    return jnp.stack(blocks)


def _rope_tables(positions, half):
    inv = ROPE_THETA ** (-jnp.arange(half, dtype=F32) / half)
    ang = positions.astype(F32)[:, None] * inv
    return jnp.cos(ang), jnp.sin(ang)


def _pack_rows(vs):
    return jnp.concatenate([v.reshape(-1, LANES) for v in vs], axis=0)


def kernel(x, positions, norm_mix_g, w_in, ret_norm_g, w_ret_o, q_a_norm_g, w_q_b, kv_a_norm_g, w_kv_b, w_mla_o, w_out, norm_mlp_g, w_up, w_down, norm_f_g, loss_target, m_norm_mix_g, m_w_in, m_ret_norm_g, m_w_ret_o, m_q_a_norm_g, m_w_q_b, m_kv_a_norm_g, m_w_kv_b, m_w_mla_o, m_w_out, m_norm_mlp_g, m_w_up, m_w_down, m_norm_f_g, v_norm_mix_g, v_w_in, v_ret_norm_g, v_w_ret_o, v_q_a_norm_g, v_w_q_b, v_kv_a_norm_g, v_w_kv_b, v_w_mla_o, v_w_out, v_norm_mlp_g, v_w_up, v_w_down, v_norm_f_g):
    xs, tgt, pos = x[0], loss_target[0], positions[0]
    s, d = xs.shape
    mats = {"w_in": w_in[0], "w_ret_o": w_ret_o[0], "w_q_b": w_q_b[0], "w_kv_b": w_kv_b[0], "w_mla_o": w_mla_o[0],
            "w_out": w_out[0], "w_up": w_up[0], "w_down": w_down[0]}
    mat_w = {"w_in": w_in, "w_ret_o": w_ret_o, "w_q_b": w_q_b, "w_kv_b": w_kv_b, "w_mla_o": w_mla_o, "w_out": w_out,
             "w_up": w_up, "w_down": w_down}
    mat_m = {"w_in": m_w_in, "w_ret_o": m_w_ret_o, "w_q_b": m_w_q_b, "w_kv_b": m_w_kv_b, "w_mla_o": m_w_mla_o,
             "w_out": m_w_out, "w_up": m_w_up, "w_down": m_w_down}
    mat_v = {"w_in": v_w_in, "w_ret_o": v_w_ret_o, "w_q_b": v_w_q_b, "w_kv_b": v_w_kv_b, "w_mla_o": v_w_mla_o,
             "w_out": v_w_out, "w_up": v_w_up, "w_down": v_w_down}
    names = list(mats)
    col_sharded = ("w_in", "w_q_b", "w_kv_b", "w_up")
    vw = ret_norm_g.shape[1]
    mla_w = mats["w_mla_o"].shape[0] * N_DEV
    ql, kl = q_a_norm_g.shape[1], kv_a_norm_g.shape[1]
    n_in = mats["w_in"].shape[1] * N_DEV
    qw = (n_in - 2 * vw - ql - kl - ROPE - 2 * d) // 2
    lay = _make_layout(d, vw, qw, ql, kl, mla_w)
    assert lay["n_in"] == n_in
    heads_r, heads_m = lay["ret_heads"], lay["mla_heads"]

    shard16 = {n: mats[n].astype(BF16) for n in names}
    with_in_proj = ("w_ret_o", "w_q_b", "w_kv_b", "w_mla_o", "w_out")
    mlp = ("w_up", "w_down")
    by_device = ("w_up", "w_kv_b")
    full = {}

    def keep(group, gathered):
        for n, g in zip(group, gathered):
            if n not in by_device:
                g = _cols_to_full(g) if n in col_sharded else g.reshape(-1, g.shape[2])
            full[n] = g

    w_mine = _w_in_to_mine(_exchange_alone("gather_w_in", _gather_exchange([shard16["w_in"]]))[0], lay)

    c64, s64 = _rope_tables(pos, RET_QK // 2)
    cos_r = jnp.concatenate([c64, c64], axis=1)
    sin_r = jnp.concatenate([-s64, s64], axis=1)
    c32, s32 = _rope_tables(pos, ROPE // 2)
    z32, z64 = jnp.zeros_like(c32), jnp.zeros((s, LANES - ROPE), F32)
    cos_p = jnp.concatenate([c32, c32, z64], axis=1)
    sin_a = jnp.concatenate([-s32, z32, z64], axis=1)
    sin_b = jnp.concatenate([z32, s32, z64], axis=1)
    lg = jnp.log(1.0 - 2.0 ** (-5.0 - jnp.arange(heads_r, dtype=F32)))
    lgs = jnp.broadcast_to(lg[:, None, None], (heads_r, 8, LANES))

    tm = min(256, s)
    blk = min(512, s)
    t_att = min(512, s)

    u = _rms_fwd("norm_mix", xs, norm_mix_g, tm)
    proj, gathered = _mm("in_proj", u, w_mine, "nn", F32,
                         ride=_gather_exchange([shard16[n] for n in with_in_proj]))
    keep(with_in_proj, gathered)
    wq_pad = jnp.pad(full["w_q_b"].reshape(ql, heads_m, NOPE + ROPE),
                     ((0, 0), (0, 0), (0, QPAD - NOPE - ROPE))).reshape(ql, heads_m * QPAD)
    o_ret, states = _ret_fwd(proj, lay, cos_r, sin_r, lgs, blk)
    ry = _ret_post(proj, lay, o_ret, ret_norm_g, tm)
    y_ret = _mm("ret_out", ry, full["w_ret_o"], "nn", F32)
    cqn, ckvn, kpr = _mla_prep(proj, lay, q_a_norm_g, kv_a_norm_g, cos_p, sin_a, sin_b, tm)
    qf = _mm("q_up", cqn, wq_pad, "nn", BF16, extras=((cos_p, None), (sin_a, None), (sin_b, None)), epilogue=_q_operand)
    kv, kf = _mm("kv_up", ckvn, full["w_kv_b"], "nn", (BF16, BF16), b_by_device=True, extras=((kpr, None),),
                 epilogue=_k_operand)
    o_mla, lse, gathered = _attn_fwd(qf, kf, kv, lay, t_att, ride=_gather_exchange([shard16["w_up"]]))
    keep(("w_up",), gathered)
    gate_tile = _tile(d, 1024)
    y_mla, merged = _mm(
        "mla_out", o_mla, full["w_mla_o"], "nn", (F32, BF16), tm=512, tn=gate_tile,
        extras=((proj, lay["off"]["g_ret"] // gate_tile), (proj, lay["off"]["g_mla"] // gate_tile), y_ret),
        epilogue=lambda r, gr, gm, yr: (r, _sig(gr) * yr + _sig(gm) * r))
    mix = _mm("out_proj", merged, full["w_out"], "nn", F32)
    h1, n2 = _rms_res_fwd("norm_mlp", xs, mix, norm_mlp_g, tm)
    (z, act), gathered = _mm("mlp_up", n2, full["w_up"], "nn", (F32, BF16), b_by_device=True,
                             epilogue=lambda r: (r, jnp.square(jnp.maximum(r, 0.0))),
                             ride=_gather_exchange([shard16["w_down"]]))
    keep(("w_down",), gathered)
    dn = _mm("mlp_down", act, full["w_down"], "nn", F32)
    dh2, g_norm_f, loss_part = _final("loss_head", h1, dn, norm_f_g.reshape(1, d), tgt, tm)

    mx, my, mc = _place()
    c_arr = jnp.reshape(mc, (1,)).astype(jnp.int32)
    chip_arr = jnp.reshape(2 * mx + my, (1,)).astype(jnp.int32)
    sums, from_chips = {}, {}

    def blocks(group, grads):
        return [g if n in by_device else (_full_to_cols(g) if n in col_sharded else g.reshape((N_DEV,) + mats[n].shape))
                for n, g in zip(group, grads)]

    def pair_sums(group, mine, from_sibling):
        for n, g, r in zip(group, mine, from_sibling):
            sums[n] = _pair_sum("pair_sum_" + n, g, r, c_arr, 256)
        return [sums[n] for n in group]

    dz = _mm("mlp_down_dx", dh2, full["w_down"], "nt", BF16, extras=(z,),
             epilogue=lambda r, zz: (r * (2.0 * jnp.maximum(zz, 0.0)),))
    g_w_down = _mm("mlp_down_dw", act, dh2, "tn", BF16)
    down_blocks = blocks(("w_down",), (g_w_down,))
    g_w_up, got_down = _mm("mlp_up_dw", n2, dz, "tn", BF16, out_by_device=True, ride=_sibling_exchange(down_blocks))
    dn2, got_up = _mm("mlp_up_dx", dz, full["w_up"], "nt", F32, b_by_device=True, ride=_sibling_exchange([g_w_up]))
    mlp_sums = pair_sums(mlp, [g_w_up] + down_blocks, list(got_up) + list(got_down))
    dh1, g_norm_mlp = _rms_bwd("norm_mlp_bwd", dn2, h1, norm_mlp_g, dh2, tm)
    d_merged = _mm("out_proj_dx", dh1, full["w_out"], "nt", F32)
    g_w_out = _mm("out_proj_dw", merged, dh1, "tn", BF16)
    dy_ret, dy_mla, d_proj = _merge_bwd(proj, lay, d_merged, y_ret, y_mla, tm)
    g_w_ret_o = _mm("ret_out_dw", ry, dy_ret, "tn", BF16)
    g_w_mla_o = _mm("mla_out_dw", o_mla, dy_mla, "tn", BF16)
    mixer = ("w_out", "w_ret_o", "w_mla_o")
    mixer_blocks = blocks(mixer, (g_w_out, g_w_ret_o, g_w_mla_o))
    d_ry, got = _mm("ret_out_dx", dy_ret, full["w_ret_o"], "nt", F32, ride=_sibling_exchange(mixer_blocks))
    mixer_sums = pair_sums(mixer, mixer_blocks, got)
    d_omla = _mm("mla_out_dx", dy_mla, full["w_mla_o"], "nt", F32)
    d_oret, d_proj, g_ret_norm = _ret_post_bwd(proj, lay, d_ry, o_ret, ret_norm_g, d_proj, tm)
    d_proj, got = _ret_bwd(proj, lay, cos_r, sin_r, lgs, states, d_oret, d_proj, blk,
                           ride=_chips_exchange(mixer_sums))
    from_chips.update(zip(mixer, got))
    delta = _attn_delta(d_omla, o_mla, lay, t_att)
    dqf, dkf, dv, got = _attn_bwd(qf, kf, kv, lse, delta, d_omla, lay, t_att, ride=_chips_exchange(mlp_sums))
    from_chips.update(zip(mlp, got))
    dqp, dkv, d_proj = _attn_post_bwd(dqf, dkf, dv, lay, cos_p, sin_a, sin_b, d_proj, tm)
    d_cqn = _mm("q_up_dx", dqp, wq_pad, "nt", F32)
    g_wq_pad = _mm("q_up_dw", cqn, dqp, "tn", BF16)
    d_ckvn = _mm("kv_up_dx", dkv, full["w_kv_b"], "nt", F32, b_by_device=True)
    g_w_kv_b = _mm("kv_up_dw", ckvn, dkv, "tn", BF16, out_by_device=True)
    d_proj, g_q_a, g_kv_a = _mla_prep_bwd(proj, lay, d_cqn, d_ckvn, q_a_norm_g, kv_a_norm_g, d_proj, tm)
    g_w_mine = _mm("in_proj_dw", u, d_proj, "tn", BF16)
    g_w_q_b = g_wq_pad.reshape(ql, heads_m, QPAD)[:, :, :NOPE + ROPE].reshape(ql, heads_m * (NOPE + ROPE))
    last = ("w_in", "w_q_b", "w_kv_b")
    last_blocks = [_mine_to_blocks(g_w_mine, lay)] + blocks(last[1:], (g_w_q_b, g_w_kv_b))
    last_sums = pair_sums(last, last_blocks, _exchange_alone("grads_to_sibling", _sibling_exchange(last_blocks)))
    du, got = _mm("in_proj_dx", d_proj, w_mine, "nt", F32, ride=_chips_exchange(last_sums))
    from_chips.update(zip(last, got))
    grad_x, g_norm_mix = _rms_bwd("norm_mix_bwd", du, xs, norm_mix_g, dh1, tm)

    upd = {n: _adamw_shard("adamw_" + n, mat_w[n], mat_m[n], mat_v[n], sums[n], from_chips[n], chip_arr, 256)
           for n in names}

    gains = [("norm_mix_g", norm_mix_g, m_norm_mix_g, v_norm_mix_g, g_norm_mix),
             ("ret_norm_g", ret_norm_g, m_ret_norm_g, v_ret_norm_g, g_ret_norm),
             ("q_a_norm_g", q_a_norm_g, m_q_a_norm_g, v_q_a_norm_g, g_q_a),
             ("kv_a_norm_g", kv_a_norm_g, m_kv_a_norm_g, v_kv_a_norm_g, g_kv_a),
             ("norm_mlp_g", norm_mlp_g, m_norm_mlp_g, v_norm_mlp_g, g_norm_mlp),
             ("norm_f_g", norm_f_g, m_norm_f_g, v_norm_f_g, g_norm_f)]
    n_rows = sum(g[1].size for g in gains) // LANES
    pad_rows = -(-(n_rows + 1) // 8) * 8 - n_rows
    tail = jnp.zeros((pad_rows, LANES), F32)
    part = jnp.concatenate([_pack_rows([g[4] for g in gains]),
                            jnp.broadcast_to(loss_part[:, :1], (1, LANES)), tail[1:]], axis=0)
    packed = [jnp.concatenate([_pack_rows([g[k] for g in gains]), tail], axis=0) for k in (1, 2, 3)]
    g_sm, d_sm, m_sm, v_sm = _small_all_reduce_adam(part, *packed)
    loss = g_sm[n_rows, 0]
    small = {}
    o = 0
    for name, w, _, _, _ in gains:
        r = w.size // LANES
        small[name] = [a[o:o + r].reshape(w.shape) for a in (g_sm, d_sm, m_sm, v_sm)]
        o += r

    order = ["norm_mix_g", "w_in", "ret_norm_g", "w_ret_o", "q_a_norm_g", "w_q_b", "kv_a_norm_g", "w_kv_b", "w_mla_o",
             "w_out", "norm_mlp_g", "w_up", "w_down", "norm_f_g"]
    outs = [loss, grad_x[None]]
    for k in range(4):
        for n in order:
            outs.append(small[n][k] if n in small else upd[n][k])
    return tuple(outs)
```

```python
import functools
import math

import jax
import jax.numpy as jnp
from jax import lax
from jax.experimental import pallas as pl
from jax.experimental.pallas import tpu as pltpu

F32 = jnp.float32
BF16 = jnp.bfloat16
MESH = pl.DeviceIdType.MESH

EPS = 1e-6
ROPE_THETA = 10000.0
CHUNK_SHIFT = 6
RET_QK = 128
RET_V = 256
NOPE = 128
ROPE = 64
VHEAD = 128
QPAD = 256
LANES = 128
N_DEV = 8
VMEM_LIMIT = 56 * 1024 * 1024

ADAM_LR = 0.001
ADAM_B1 = 0.9
ADAM_B2 = 0.999
ADAM_EPS = 1e-08
ADAM_WD = 0.01
ADAM_STEP = 10

NN = (((1,), (0,)), ((), ()))
NT = (((1,), (1,)), ((), ()))
TN = (((0,), (0,)), ((), ()))


def _dot(a, b, dims):
    return lax.dot_general(a.astype(BF16), b.astype(BF16), dims, preferred_element_type=F32)


def _tile(dim, pref):
    if dim <= pref:
        return dim
    t = (pref // LANES) * LANES
    while t >= LANES:
        if dim % t == 0:
            return t
        t -= LANES
    raise ValueError(f"no tile for {dim}")


def _params(sem):
    return pltpu.CompilerParams(dimension_semantics=sem, vmem_limit_bytes=VMEM_LIMIT)


def _sig(v):
    return 1.0 / (1.0 + jnp.exp(-v))


def _mm(name, a, b, mode, out_dtypes, *, tm=1024, tn=1024, tk=2048, extras=(), epilogue=None, ride=None,
        b_by_device=False, out_by_device=False):
    if b_by_device:
        b_cols = b.shape[2]
        b_shape = (b.shape[1], N_DEV * b_cols)
    else:
        b_shape = b.shape
    if mode == "nn":
        (m, k), (_, n) = a.shape, b_shape
    elif mode == "nt":
        (m, k), (n, _) = a.shape, b_shape
    else:
        (k, m), (_, n) = a.shape, b_shape
    tm, tn, tk = _tile(m, tm), _tile(n, tn), _tile(k, tk)
    if b_by_device and mode != "nt":
        tn = _tile(b_cols, tn)
    if out_by_device:
        tn = _tile(n // N_DEV, tn)
    nk = k // tk
    dims = {"nn": NN, "nt": NT, "tn": TN}[mode]
    a_spec = (pl.BlockSpec((tk, tm), lambda i, j, kk: (kk, i)) if mode == "tn"
              else pl.BlockSpec((tm, tk), lambda i, j, kk: (i, kk)))
    if b_by_device and mode == "nt":
        piece = min(tk, b_cols)
        n_b, per = tk // piece, b_cols // piece
        b_specs = [pl.BlockSpec((None, tn, piece),
                                lambda i, j, kk, p=p: ((kk * n_b + p) // per, j, (kk * n_b + p) % per))
                   for p in range(n_b)]
    elif b_by_device:
        per = b_cols // tn
        n_b, piece = 1, tk
        b_specs = [pl.BlockSpec((None, tk, tn), lambda i, j, kk: (j // per, kk, j % per))]
    else:
        n_b, piece = 1, tk
        b_specs = [pl.BlockSpec((tn, tk), lambda i, j, kk: (j, kk)) if mode == "nt"
                   else pl.BlockSpec((tk, tn), lambda i, j, kk: (kk, j))]
    tile_spec = pl.BlockSpec((tm, tn), lambda i, j, kk: (i, j))
    if out_by_device:
        per_out = n // N_DEV // tn
        out_spec = pl.BlockSpec((None, tm, tn), lambda i, j, kk: (j // per_out, i, j % per_out))
        out_dims = (N_DEV, m, n // N_DEV)
    else:
        out_spec, out_dims = tile_spec, (m, n)
    ex_arrays, ex_specs = [], []
    for e in extras:
        arr, off = e if isinstance(e, tuple) else (e, 0)
        ex_arrays.append(arr)
        ex_specs.append(pl.BlockSpec((tm, arr.shape[1]), lambda i, j, kk: (i, 0)) if off is None
                        else pl.BlockSpec((tm, tn), lambda i, j, kk, off=off: (i, j + off)))
    n_ex = len(extras)
    single = not isinstance(out_dtypes, (tuple, list))
    dts = (out_dtypes,) if single else tuple(out_dtypes)

    grid = (m // tm, n // tn, nk)
    r_in, r_out, r_sc = ride.counts() if ride else (0, 0, 0)
    n_acc = 1 if nk > 1 else 0

    def body(a_ref, *rest):
        b_refs, rest = rest[:n_b], rest[n_b:]
        ex, rest = rest[:n_ex], rest[n_ex:]
        ride_in, rest = rest[:r_in], rest[r_in:]
        outs, rest = rest[:len(dts)], rest[len(dts):]
        ride_out, rest = rest[:r_out], rest[r_out:]
        ride_scratch = rest[n_acc:]
        if ride:
            first, mid, last = _steps([pl.program_id(d) for d in range(3)], grid)
            ride.run(ride_in, ride_out, ride_scratch, (first, mid, None))

        def finish(r):
            vals = (r,) if epilogue is None else epilogue(r, *[e[...] for e in ex])
            for o, v in zip(outs, vals):
                o[...] = v.astype(o.dtype)

        if n_b == 1:
            part = _dot(a_ref[...], b_refs[0][...], dims)
        else:
            part = sum(_dot(a_ref[:, p * piece:(p + 1) * piece], b_refs[p][...], dims) for p in range(n_b))
        if nk == 1:
            finish(part)
        else:
            acc = rest[0]
            kk = pl.program_id(2)

            @pl.when(kk == 0)
            def _():
                acc[...] = part

            @pl.when(jnp.logical_and(kk > 0, kk < nk - 1))
            def _():
                acc[...] += part

            @pl.when(kk == nk - 1)
            def _():
                finish(acc[...] + part)

        if ride:
            ride.run(ride_in, ride_out, ride_scratch, (None, None, last))

    res = pl.pallas_call(
        body, name=name, grid=grid,
        in_specs=[a_spec] + b_specs + ex_specs + [ANY] * r_in,
        out_specs=[out_spec] * len(dts) + [ANY] * r_out,
        out_shape=[jax.ShapeDtypeStruct(out_dims, d) for d in dts] + (ride.out_shape if ride else []),
        scratch_shapes=([pltpu.VMEM((tm, tn), F32)] if nk > 1 else []) + (ride.scratch if ride else []),
        compiler_params=_params(("arbitrary",) * 3 if ride else ("parallel", "parallel", "arbitrary")),
    )(a, *[b] * n_b, *ex_arrays, *(ride.ins if ride else []))
    own = res[0] if single else res[:len(dts)]
    return (own, res[len(dts):]) if ride else own


def _rows(name, body, n_rows, tm, ins, outs, accs=(), into=None):
    in_specs, args = [], []
    for t in ins:
        if len(t) == 1:
            in_specs.append(pl.BlockSpec(t[0].shape, lambda i, nd=t[0].ndim: (0,) * nd))
        else:
            in_specs.append(pl.BlockSpec((tm, t[1]), lambda i, cb=t[2]: (i, cb)))
        args.append(t[0])
    outs = [(o + (o[0], 0))[:4] for o in outs]
    out_specs = [pl.BlockSpec((tm, w), lambda i, cb=cb: (i, cb)) for w, _, _, cb in outs]
    out_shape = [jax.ShapeDtypeStruct((n_rows, total), d) for _, d, total, _ in outs]
    aliases, kernel = {}, body
    if into is not None:
        arr, w, cb = into
        in_specs.append(ANY)
        args.append(arr)
        out_specs.append(pl.BlockSpec((tm, w), lambda i: (i, cb)))
        out_shape.append(jax.ShapeDtypeStruct(arr.shape, arr.dtype))
        aliases = {len(ins): len(outs)}
        n_in = len(ins)

        def kernel(*refs):
            body(*refs[:n_in], *refs[n_in + 1:])

    out_specs += [pl.BlockSpec((r, w), lambda i: (0, 0)) for r, w in accs]
    out_shape += [jax.ShapeDtypeStruct((r, w), F32) for r, w in accs]
    return pl.pallas_call(
        kernel, name=name, grid=(n_rows // tm,), in_specs=in_specs, out_specs=out_specs, out_shape=out_shape,
        input_output_aliases=aliases, compiler_params=_params(("arbitrary",) if accs else ("parallel",)),
    )(*args)


def _zero_first(*accs):
    @pl.when(pl.program_id(0) == 0)
    def _():
        for a in accs:
            a[...] = jnp.zeros_like(a)


def _rope64(t, cos, sin):
    return t * cos + pltpu.roll(t, RET_QK // 2, 1) * sin


def _rope32(t, cos, sin_a, sin_b):
    return t * cos + pltpu.roll(t, LANES - ROPE // 2, 1) * sin_a + pltpu.roll(t, ROPE // 2, 1) * sin_b


def _rms_fwd(name, x, g, tm):
    s, d = x.shape

    def body(x_ref, g_ref, u_ref):
        v = x_ref[...]
        r = lax.rsqrt(jnp.mean(v * v, axis=-1, keepdims=True) + EPS)
        u_ref[...] = (v * r * g_ref[...]).astype(BF16)

    return _rows(name, body, s, tm, [(x, d, 0), (g,)], [(d, BF16)])[0]


def _rms_res_fwd(name, x, mix, g, tm):
    s, d = x.shape

    def body(x_ref, m_ref, g_ref, h_ref, u_ref):
        v = x_ref[...] + m_ref[...]
        h_ref[...] = v
        r = lax.rsqrt(jnp.mean(v * v, axis=-1, keepdims=True) + EPS)
        u_ref[...] = (v * r * g_ref[...]).astype(BF16)

    return _rows(name, body, s, tm, [(x, d, 0), (mix, d, 0), (g,)], [(d, F32), (d, BF16)])


def _rms_bwd(name, dy, x, g, dres, tm):
    s, d = x.shape

    def body(dy_ref, x_ref, g_ref, dres_ref, dx_ref, dg_ref):
        _zero_first(dg_ref)
        v, dyv = x_ref[...], dy_ref[...]
        r = lax.rsqrt(jnp.mean(v * v, axis=-1, keepdims=True) + EPS)
        xh = v * r
        dxh = dyv * g_ref[...]
        dx_ref[...] = dres_ref[...] + r * (dxh - xh * jnp.mean(dxh * xh, axis=-1, keepdims=True))
        dg_ref[...] += jnp.sum(dyv * xh, axis=0, keepdims=True)

    return _rows(name, body, s, tm, [(dy, d, 0), (x, d, 0), (g,), (dres, d, 0)], [(d, F32)], [(1, d)])


def _final(name, h1, dn, g, tgt, tm):
    s, d = h1.shape

    def body(h_ref, dn_ref, g_ref, t_ref, dh_ref, dg_ref, loss_ref):
        _zero_first(dg_ref, loss_ref)
        v = h_ref[...] + dn_ref[...]
        r = lax.rsqrt(jnp.mean(v * v, axis=-1, keepdims=True) + EPS)
        xh = v * r
        gv = g_ref[...]
        e = xh * gv - t_ref[...]
        loss_ref[...] += 0.5 * jnp.sum(jnp.mean(e * e, axis=-1, keepdims=True))
        dy = e * (1.0 / d)
        dg_ref[...] += jnp.sum(dy * xh, axis=0, keepdims=True)
        dxh = dy * gv
        dh_ref[...] = r * (dxh - xh * jnp.mean(dxh * xh, axis=-1, keepdims=True))

    return _rows(name, body, s, tm, [(h1, d, 0), (dn, d, 0), (g,), (tgt, d, 0)], [(d, F32)], [(1, d), (1, LANES)])


def _decay_mask(lg, blk):
    n = lax.broadcasted_iota(jnp.int32, (blk, blk), 0)
    m = lax.broadcasted_iota(jnp.int32, (blk, blk), 1)
    w = jnp.exp(lg * jnp.abs(n - m).astype(F32))
    return jnp.where(jnp.right_shift(m, CHUNK_SHIFT) <= jnp.right_shift(n, CHUNK_SHIFT), w, 0.0)


def _decays(lg, blk):
    pos = lax.broadcasted_iota(jnp.int32, (blk, 1), 0).astype(F32)
    return jnp.exp(lg * (pos + 1.0)), jnp.exp(lg * (blk - 1.0 - pos)), jnp.exp(lg * float(blk))


def _ret_fwd(proj, lay, cos, sin, lgs, blk, ride=None):
    s = proj.shape[0]
    heads = lay["ret_heads"]
    nb = s // blk
    scale = RET_QK ** -0.5
    ride_in_specs, ride_ins, ride_out_specs, ride_out_shape, ride_scratch = _ride_args(ride)

    def body(lg_ref, qkv_ref, cos_ref, sin_ref, o_ref, st_ref, state, mask):
        lg = lg_ref[0:1, 0:1]

        @pl.when(pl.program_id(1) == 0)
        def _():
            state[...] = jnp.zeros_like(state)
            mask[...] = _decay_mask(lg, blk)

        a, c, gb = _decays(lg, blk)
        q = _rope64(qkv_ref[:, :RET_QK], cos_ref[...], sin_ref[...])
        k = _rope64(qkv_ref[:, RET_QK:2 * RET_QK], cos_ref[...], sin_ref[...]) * scale
        v = qkv_ref[:, 2 * RET_QK:]
        st = state[...]
        st_ref[...] = st
        sm = _dot(q, k, NT) * mask[...]
        o_ref[...] = _dot(sm, v, NN) + _dot(q * a, st, NN)
        state[...] = st * gb + _dot(k * c, v, TN)

    first = lay["off"]["heads"] // RET_HEAD
    res = pl.pallas_call(
        _with_ride(body, ride, (heads, nb), 0, 4, 2), name="ret_fwd", grid=(heads, nb),
        in_specs=[pl.BlockSpec((None, 8, LANES), lambda h, b: (h, 0, 0)),
                  pl.BlockSpec((blk, RET_HEAD), lambda h, b: (b, first + h)),
                  pl.BlockSpec((blk, LANES), lambda h, b: (b, 0)),
                  pl.BlockSpec((blk, LANES), lambda h, b: (b, 0))] + ride_in_specs,
        out_specs=[pl.BlockSpec((blk, RET_V), lambda h, b: (b, h)),
                   pl.BlockSpec((None, None, RET_QK, RET_V), lambda h, b: (h, b, 0, 0))] + ride_out_specs,
        out_shape=[jax.ShapeDtypeStruct((s, heads * RET_V), F32),
                   jax.ShapeDtypeStruct((heads, nb, RET_QK, RET_V), F32)] + ride_out_shape,
        scratch_shapes=[pltpu.VMEM((RET_QK, RET_V), F32), pltpu.VMEM((blk, blk), F32)] + ride_scratch,
        compiler_params=_params(("arbitrary", "arbitrary") if ride else ("parallel", "arbitrary")),
    )(lgs, proj, cos, sin, *ride_ins)
    return (res[0], res[1], res[2:]) if ride else res


def _ret_bwd(proj, lay, cos, sin, lgs, states, d_o, d_proj, blk, ride=None):
    ride_in_specs, ride_ins, ride_out_specs, ride_out_shape, ride_scratch = _ride_args(ride)
    s = proj.shape[0]
    heads = lay["ret_heads"]
    nb = s // blk
    scale = RET_QK ** -0.5

    def body(lg_ref, qkv_ref, cos_ref, sin_ref, st_ref, do_ref, _, dqkv_ref, dstate, mask):
        lg = lg_ref[0:1, 0:1]

        @pl.when(pl.program_id(1) == 0)
        def _():
            dstate[...] = jnp.zeros_like(dstate)
            mask[...] = _decay_mask(lg, blk)

        a, c, gb = _decays(lg, blk)
        cs, sn = cos_ref[...], sin_ref[...]
        q = _rope64(qkv_ref[:, :RET_QK], cs, sn)
        k = _rope64(qkv_ref[:, RET_QK:2 * RET_QK], cs, sn) * scale
        v = qkv_ref[:, 2 * RET_QK:]
        st = st_ref[...]
        do = do_ref[...]
        dst = dstate[...]
        mk = mask[...]
        sm = _dot(q, k, NT) * mk
        ds = _dot(do, v, NT) * mk
        dq = _dot(ds, k, NN) + _dot(do, st, NT) * a
        dk = _dot(ds, q, TN) + _dot(v, dst, NT) * c
        dqkv_ref[:, 2 * RET_QK:] = (_dot(sm, do, TN) + _dot(k * c, dst, NN)).astype(BF16)
        dstate[...] = dst * gb + _dot(q * a, do, TN)
        dqkv_ref[:, :RET_QK] = _rope64(dq, cs, -sn).astype(BF16)
        dqkv_ref[:, RET_QK:2 * RET_QK] = (_rope64(dk, cs, -sn) * scale).astype(BF16)

    first = lay["off"]["heads"] // RET_HEAD
    last = nb - 1
    res = pl.pallas_call(
        _with_ride(body, ride, (heads, nb), 0, 7, 1), name="ret_bwd", grid=(heads, nb),
        in_specs=[pl.BlockSpec((None, 8, LANES), lambda h, b: (h, 0, 0)),
                  pl.BlockSpec((blk, RET_HEAD), lambda h, b: (last - b, first + h)),
                  pl.BlockSpec((blk, LANES), lambda h, b: (last - b, 0)),
                  pl.BlockSpec((blk, LANES), lambda h, b: (last - b, 0)),
                  pl.BlockSpec((None, None, RET_QK, RET_V), lambda h, b: (h, last - b, 0, 0)),
                  pl.BlockSpec((blk, RET_V), lambda h, b: (last - b, h)), ANY] + ride_in_specs,
        out_specs=[pl.BlockSpec((blk, RET_HEAD), lambda h, b: (last - b, first + h))] + ride_out_specs,
        out_shape=[jax.ShapeDtypeStruct(d_proj.shape, d_proj.dtype)] + ride_out_shape,
        scratch_shapes=[pltpu.VMEM((RET_QK, RET_V), F32), pltpu.VMEM((blk, blk), F32)] + ride_scratch,
        input_output_aliases={6: 0},
        compiler_params=_params(("arbitrary", "arbitrary") if ride else ("parallel", "arbitrary")),
    )(lgs, proj, cos, sin, states, d_o, d_proj, *ride_ins)
    return (res[0], res[1:]) if ride else res[0]


def _ret_post(proj, lay, o, g, tm):
    s, vw = o.shape
    heads = lay["ret_heads"]

    def body(o_ref, rg_ref, g_ref, ry_ref):
        for h in range(heads):
            sl = slice(h * RET_V, (h + 1) * RET_V)
            oh = o_ref[:, sl]
            dlt = oh - jnp.mean(oh, axis=-1, keepdims=True)
            rstd = lax.rsqrt(jnp.mean(dlt * dlt, axis=-1, keepdims=True) + EPS)
            rg = rg_ref[:, sl]
            ry_ref[:, sl] = (dlt * rstd * g_ref[:, sl] * (rg * _sig(rg))).astype(BF16)

    return _rows("ret_post", body, s, tm, [(o, vw, 0), (proj, vw, lay["off"]["r_g"] // vw), (g,)], [(vw, BF16)])[0]


def _ret_post_bwd(proj, lay, d_ry, o, g, d_proj, tm):
    s, vw = o.shape
    heads = lay["ret_heads"]

    def body(dry_ref, o_ref, rg_ref, g_ref, do_ref, drg_ref, dg_ref):
        _zero_first(dg_ref)
        for h in range(heads):
            sl = slice(h * RET_V, (h + 1) * RET_V)
            oh = o_ref[:, sl]
            dlt = oh - jnp.mean(oh, axis=-1, keepdims=True)
            rstd = lax.rsqrt(jnp.mean(dlt * dlt, axis=-1, keepdims=True) + EPS)
            oh = dlt * rstd
            gv = g_ref[:, sl]
            rg = rg_ref[:, sl]
            sg = _sig(rg)
            dry = dry_ref[:, sl]
            dt = dry * (rg * sg)
            drg_ref[:, sl] = (dry * (oh * gv) * (sg * (1.0 + rg * (1.0 - sg)))).astype(BF16)
            dg_ref[:, sl] += jnp.sum(dt * oh, axis=0, keepdims=True)
            doh = dt * gv
            do_ref[:, sl] = rstd * (doh - jnp.mean(doh, axis=-1, keepdims=True)
                                    - oh * jnp.mean(doh * oh, axis=-1, keepdims=True))

    return _rows("ret_post_bwd", body, s, tm,
                 [(d_ry, vw, 0), (o, vw, 0), (proj, vw, lay["off"]["r_g"] // vw), (g,)],
                 [(vw, F32)], [(1, vw)], into=(d_proj, vw, lay["off"]["r_g"] // vw))


def _mla_prep(proj, lay, gq, gkv, cos, sin_a, sin_b, tm):
    s = proj.shape[0]
    ql, kl = lay["q_lora"], lay["kv_lora"]

    def body(cq_ref, ckv_ref, kpe_ref, gq_ref, gkv_ref, cos_ref, sa_ref, sb_ref, cqn_ref, ckvn_ref, kpr_ref):
        for src, gref, dst in ((cq_ref, gq_ref, cqn_ref), (ckv_ref, gkv_ref, ckvn_ref)):
            v = src[...]
            r = lax.rsqrt(jnp.mean(v * v, axis=-1, keepdims=True) + EPS)
            dst[...] = (v * r * gref[...]).astype(BF16)
        kpr_ref[...] = _rope32(kpe_ref[...], cos_ref[...], sa_ref[...], sb_ref[...]).astype(BF16)

    off = lay["off"]
    return _rows("mla_prep", body, s, tm,
                 [(proj, ql, off["c_q"] // ql), (proj, kl, off["c_kv"] // kl), (proj, LANES, off["k_pe"] // LANES),
                  (gq,), (gkv,), (cos, LANES, 0), (sin_a, LANES, 0), (sin_b, LANES, 0)],
                 [(ql, BF16), (kl, BF16), (LANES, BF16)])


def _mla_prep_bwd(proj, lay, d_cqn, d_ckvn, gq, gkv, d_proj, tm):
    s = proj.shape[0]
    ql, kl = lay["q_lora"], lay["kv_lora"]

    def body(dq_ref, dkv_ref, cq_ref, ckv_ref, gq_ref, gkv_ref, dc_ref, dgq_ref, dgkv_ref):
        _zero_first(dgq_ref, dgkv_ref)
        for dref, src, gref, cols, dg in ((dq_ref, cq_ref, gq_ref, slice(0, ql), dgq_ref),
                                          (dkv_ref, ckv_ref, gkv_ref, slice(ql, ql + kl), dgkv_ref)):
            v, dy = src[...], dref[...]
            r = lax.rsqrt(jnp.mean(v * v, axis=-1, keepdims=True) + EPS)
            xh = v * r
            dxh = dy * gref[...]
            dc_ref[:, cols] = (r * (dxh - xh * jnp.mean(dxh * xh, axis=-1, keepdims=True))).astype(BF16)
            dg[...] += jnp.sum(dy * xh, axis=0, keepdims=True)

    off = lay["off"]
    return _rows("mla_prep_bwd", body, s, tm,
                 [(d_cqn, ql, 0), (d_ckvn, kl, 0), (proj, ql, off["c_q"] // ql), (proj, kl, off["c_kv"] // kl),
                  (gq,), (gkv,)],
                 [], [(1, ql), (1, kl)], into=(d_proj, ql + kl, off["c_q"] // (ql + kl)))


def _q_operand(r, cos, sin_a, sin_b):
    qs = (NOPE + ROPE) ** -0.5 * math.log2(math.e)
    cs, sa, sb = cos * qs, sin_a * qs, sin_b * qs
    parts = []
    for lo in range(0, r.shape[1], QPAD):
        parts += [r[:, lo:lo + NOPE] * qs, _rope32(r[:, lo + NOPE:lo + QPAD], cs, sa, sb)]
    return (jnp.concatenate(parts, axis=1),)


def _k_operand(r, kpr):
    parts = []
    for lo in range(0, r.shape[1], QPAD):
        parts += [r[:, lo:lo + NOPE], kpr.astype(F32)]
    return r, jnp.concatenate(parts, axis=1)


def _rope_key_grad(parts, lay, d_proj, tm):
    s, w = parts.shape

    def body(p_ref, dkpe_ref):
        dkpe_ref[:, :LANES] = sum(p_ref[:, lo:lo + LANES] for lo in range(0, w, LANES)).astype(BF16)
        dkpe_ref[:, LANES:] = jnp.zeros((tm, LANES), BF16)

    return _rows("rope_key_grad", body, s, tm, [(parts, w, 0)], [],
                 into=(d_proj, 2 * LANES, lay["off"]["k_pe"] // (2 * LANES)))[0]


def _diag_mask(t, keys_on_rows=False):
    row = lax.broadcasted_iota(jnp.int32, (t, t), 0)
    col = lax.broadcasted_iota(jnp.int32, (t, t), 1)
    key, query = (row, col) if keys_on_rows else (col, row)
    return jnp.right_shift(key, CHUNK_SHIFT) <= jnp.right_shift(query, CHUNK_SHIFT)


def _tile_pairs(nt, by_key):
    if by_key:
        pairs = [(i, j) for j in range(nt) for i in range(j, nt)]
    else:
        pairs = [(i, j) for i in range(nt) for j in range(i + 1)]
    return (jnp.asarray([p[0] for p in pairs], jnp.int32), jnp.asarray([p[1] for p in pairs], jnp.int32))


def _head_block(heads):
    return 4 if heads % 4 == 0 else 2 if heads % 2 == 0 else 1


def _attn_fwd(qf, kf, kv, lay, t, ride=None):
    s = qf.shape[0]
    heads = lay["mla_heads"]
    hb = _head_block(heads)
    nt = s // t
    qi, kj = _tile_pairs(nt, False)
    grid = (heads // hb, int(qi.shape[0]))
    ride_in_specs, ride_ins, ride_out_specs, ride_out_shape, ride_scratch = _ride_args(ride)

    def body(qi_ref, kj_ref, q_ref, k_ref, kv_ref, o_ref, lse_ref, m_s, l_s, acc):
        p = pl.program_id(1)
        i, j = qi_ref[p], kj_ref[p]

        @pl.when(j == 0)
        def _():
            m_s[...] = jnp.full_like(m_s, -jnp.inf)
            l_s[...] = jnp.zeros_like(l_s)
            acc[...] = jnp.zeros_like(acc)

        def step(diagonal):
            ones = jnp.ones((t, LANES), BF16)
            scores = [_dot(q_ref[:, hh * QPAD:(hh + 1) * QPAD], k_ref[:, hh * QPAD:(hh + 1) * QPAD], NT)
                      for hh in range(hb)]
            for hh in range(hb):
                sc = scores[hh]
                if diagonal:
                    sc = jnp.where(_diag_mask(t), sc, -jnp.inf)
                cols = [sc[:, c * LANES:(c + 1) * LANES] for c in range(t // LANES)]
                m_old = m_s[hh]
                m_new = jnp.maximum(m_old, jnp.max(functools.reduce(jnp.maximum, cols), axis=-1, keepdims=True))
                alpha = jnp.exp2(m_old - m_new)
                pr = jnp.concatenate([jnp.exp2(c - m_new).astype(BF16) for c in cols], axis=1)
                pv = _dot(pr, jnp.concatenate([kv_ref[:, hh * QPAD + NOPE:(hh + 1) * QPAD], ones], axis=1), NN)
                l_new = alpha * l_s[hh] + pv[:, VHEAD:]
                a_new = alpha * acc[hh] + pv[:, :VHEAD]
                if diagonal:
                    o_ref[:, hh * VHEAD:(hh + 1) * VHEAD] = a_new / l_new
                    lse_ref[hh] = jnp.transpose(m_new + jnp.log2(l_new))[:1]
                else:
                    m_s[hh], l_s[hh], acc[hh] = m_new, l_new, a_new

        pl.when(j < i)(functools.partial(step, False))
        pl.when(j == i)(functools.partial(step, True))

    res = pl.pallas_call(
        _with_ride(body, ride, grid, 2, 3, 2), name="attn_fwd",
        grid_spec=pltpu.PrefetchScalarGridSpec(
            num_scalar_prefetch=2, grid=grid,
            in_specs=[pl.BlockSpec((t, hb * QPAD), lambda h, p, qi, kj: (qi[p], h)),
                      pl.BlockSpec((t, hb * QPAD), lambda h, p, qi, kj: (kj[p], h)),
                      pl.BlockSpec((t, hb * QPAD), lambda h, p, qi, kj: (kj[p], h))] + ride_in_specs,
            out_specs=[pl.BlockSpec((t, hb * VHEAD), lambda h, p, qi, kj: (qi[p], h)),
                       pl.BlockSpec((hb, 1, t), lambda h, p, qi, kj: (h, 0, qi[p]))] + ride_out_specs,
            scratch_shapes=[pltpu.VMEM((hb, t, LANES), F32), pltpu.VMEM((hb, t, LANES), F32),
                            pltpu.VMEM((hb, t, VHEAD), F32)] + ride_scratch),
        out_shape=[jax.ShapeDtypeStruct((s, heads * VHEAD), F32),
                   jax.ShapeDtypeStruct((heads, 1, s), F32)] + ride_out_shape,
        compiler_params=_params(("arbitrary", "arbitrary") if ride else ("parallel", "arbitrary")),
    )(qi, kj, qf, kf, kv, *ride_ins)
    return (res[0], res[1], res[2:]) if ride else res


def _attn_delta(d_o, o, lay, tm):
    s = o.shape[0]
    heads = lay["mla_heads"]

    def body(do_ref, o_ref, dl_ref):
        for h in range(heads):
            sl = slice(h * VHEAD, (h + 1) * VHEAD)
            dl_ref[h] = jnp.sum(jnp.transpose(do_ref[:, sl] * o_ref[:, sl]), axis=0, keepdims=True)

    tile = pl.BlockSpec((tm, heads * VHEAD), lambda i: (i, 0))
    return pl.pallas_call(
        body, name="attn_delta", grid=(s // tm,), in_specs=[tile, tile],
        out_specs=pl.BlockSpec((heads, 1, tm), lambda i: (0, 0, i)),
        out_shape=jax.ShapeDtypeStruct((heads, 1, s), F32),
        compiler_params=_params(("parallel",)),
    )(d_o, o)


def _attn_bwd(qf, kf, kv, lse, delta, d_o, cos, sin_a, sin_b, lay, t, ride=None):
    s = qf.shape[0]
    heads = lay["mla_heads"]
    hb = _head_block(heads)
    nt = s // t
    scale = (NOPE + ROPE) ** -0.5
    qi, kj = _tile_pairs(nt, True)
    grid = (heads // hb, int(qi.shape[0]))
    ride_in_specs, ride_ins, ride_out_specs, ride_out_shape, ride_scratch = _ride_args(ride)

    def body(qi_ref, kj_ref, q_ref, k_ref, kv_ref, lse_ref, dl_ref, do_ref, cos_ref, sa_ref, sb_ref,
             dqp_ref, dkv_ref, dkpe_ref, dq_acc, dk_acc, dv_acc):
        p = pl.program_id(1)
        i, j = qi_ref[p], kj_ref[p]
        rows = pl.ds(pl.multiple_of(i * t, t), t)

        def unrope(v):
            return _rope32(v, cos_ref[...], -sa_ref[...], -sb_ref[...])

        @pl.when(p == 0)
        def _():
            dq_acc[...] = jnp.zeros_like(dq_acc)

        def step(diagonal):
            for hh in range(hb):
                lo = hh * QPAD
                q, k = q_ref[:, lo:lo + QPAD], k_ref[:, lo:lo + QPAD]
                do = do_ref[:, hh * VHEAD:(hh + 1) * VHEAD]
                pr = jnp.exp2(_dot(k, q, NT) - lse_ref[hh])
                if diagonal:
                    pr = jnp.where(_diag_mask(t, keys_on_rows=True), pr, 0.0)
                dv_part = _dot(pr, do, NN)
                ds = (pr * (_dot(kv_ref[:, lo + NOPE:lo + QPAD], do, NT) - dl_ref[hh])).astype(BF16)
                dk_part = _dot(ds, q, NN)
                dq = dq_acc[rows, lo:lo + QPAD] + _dot(ds, k, TN) * scale
                if diagonal:
                    dqp_ref[:, lo:lo + NOPE] = dq[:, :NOPE].astype(BF16)
                    dqp_ref[:, lo + NOPE:lo + QPAD] = unrope(dq[:, NOPE:]).astype(BF16)
                    dk_acc[hh], dv_acc[hh] = dk_part, dv_part
                else:
                    dq_acc[rows, lo:lo + QPAD] = dq
                    dk_acc[hh] += dk_part
                    dv_acc[hh] += dv_part

        pl.when(i > j)(functools.partial(step, False))
        pl.when(i == j)(functools.partial(step, True))

        @pl.when(i == nt - 1)
        def _():
            kpe = jnp.zeros((t, LANES), F32)
            for hh in range(hb):
                lo = hh * QPAD
                dk = dk_acc[hh] * math.log(2.0)
                dkv_ref[:, lo:lo + NOPE] = dk[:, :NOPE].astype(BF16)
                dkv_ref[:, lo + NOPE:lo + QPAD] = dv_acc[hh].astype(BF16)
                kpe = kpe + dk[:, NOPE:]
            dkpe_ref[...] = unrope(kpe)

    table = pl.BlockSpec((t, LANES), lambda h, p, qi, kj: (kj[p], 0))
    res = pl.pallas_call(
        _with_ride(body, ride, grid, 2, 9, 3), name="attn_bwd",
        grid_spec=pltpu.PrefetchScalarGridSpec(
            num_scalar_prefetch=2, grid=grid,
            in_specs=[pl.BlockSpec((t, hb * QPAD), lambda h, p, qi, kj: (qi[p], h)),
                      pl.BlockSpec((t, hb * QPAD), lambda h, p, qi, kj: (kj[p], h)),
                      pl.BlockSpec((t, hb * QPAD), lambda h, p, qi, kj: (kj[p], h)),
                      pl.BlockSpec((hb, 1, t), lambda h, p, qi, kj: (h, 0, qi[p])),
                      pl.BlockSpec((hb, 1, t), lambda h, p, qi, kj: (h, 0, qi[p])),
                      pl.BlockSpec((t, hb * VHEAD), lambda h, p, qi, kj: (qi[p], h)),
                      table, table, table] + ride_in_specs,
            out_specs=[pl.BlockSpec((t, hb * QPAD), lambda h, p, qi, kj: (kj[p], h)),
                       pl.BlockSpec((t, hb * QPAD), lambda h, p, qi, kj: (kj[p], h)),
                       pl.BlockSpec((t, LANES), lambda h, p, qi, kj: (kj[p], h))] + ride_out_specs,
            scratch_shapes=[pltpu.VMEM((s, hb * QPAD), F32), pltpu.VMEM((hb, t, QPAD), F32),
                            pltpu.VMEM((hb, t, VHEAD), F32)] + ride_scratch),
        out_shape=[jax.ShapeDtypeStruct((s, heads * QPAD), BF16),
                   jax.ShapeDtypeStruct((s, heads * QPAD), BF16),
                   jax.ShapeDtypeStruct((s, heads // hb * LANES), F32)] + ride_out_shape,
        compiler_params=_params(("arbitrary", "arbitrary") if ride else ("parallel", "arbitrary")),
    )(qi, kj, qf, kf, kv, lse, delta, d_o, cos, sin_a, sin_b, *ride_ins)
    return (res[0], res[1], res[2], res[3:]) if ride else res


def _merge_bwd(proj, lay, d_merged, y_ret, y_mla, tm):
    s, d = y_ret.shape
    off = lay["off"]
    assert off["g_ret"] == 0 and off["g_mla"] == d

    def body(dm_ref, gr_ref, gm_ref, yr_ref, ym_ref, dyr_ref, dym_ref, dg_ref):
        dm = dm_ref[...]
        for g_ref, y_ref, dy_ref, cols in ((gr_ref, yr_ref, dyr_ref, slice(0, d)),
                                           (gm_ref, ym_ref, dym_ref, slice(d, 2 * d))):
            sg = _sig(g_ref[...])
            dy_ref[...] = (dm * sg).astype(BF16)
            dg_ref[:, cols] = (dm * y_ref[...] * (sg * (1.0 - sg))).astype(BF16)

    return _rows("merge_bwd", body, s, tm,
                 [(d_merged, d, 0), (proj, d, 0), (proj, d, 1), (y_ret, d, 0), (y_mla, d, 0)],
                 [(d, BF16), (d, BF16), (2 * d, BF16, lay["total"], 0)])


ANY = pl.BlockSpec(memory_space=pl.ANY)


def _place():
    return lax.axis_index("x"), lax.axis_index("y"), lax.axis_index("c")


def _other_chips(x, y):
    return [(1 - x, y), (x, 1 - y), (1 - x, 1 - y)]


class _Exchange:
    def __init__(self, ins, out_shape, scratch, phases):
        self.ins, self.out_shape, self.scratch, self.phases = list(ins), list(out_shape), list(scratch), phases

    def counts(self):
        return len(self.ins), len(self.out_shape), len(self.scratch)

    def run(self, r_in, r_out, r_scratch, conds):
        for cond, phase in zip(conds, self.phases):
            if phase is not None and cond is not None:
                pl.when(cond)(functools.partial(phase, r_in, r_out, r_scratch))


def _steps(ids, sizes):
    lin, total = 0, 1
    for i, n in zip(ids, sizes):
        lin, total = lin * n + i, total * n
    return lin == 0, lin == total // 2, lin == total - 1


def _ride_args(ride):
    if ride is None:
        return [], [], [], [], []
    n_in, n_out, _ = ride.counts()
    return [ANY] * n_in, ride.ins, [ANY] * n_out, ride.out_shape, ride.scratch


def _with_ride(body, ride, grid, n_prefetch, n_in, n_out):
    if ride is None:
        return body
    r_in, r_out, r_sc = ride.counts()

    def hosted(*refs):
        cuts = (n_prefetch, n_in, r_in, n_out, r_out)
        parts, pos = [], 0
        for n in cuts:
            parts.append(refs[pos:pos + n])
            pos += n
        pre, ins, ride_in, outs, ride_out = parts
        scratch, ride_scratch = refs[pos:len(refs) - r_sc], refs[len(refs) - r_sc:]
        first, mid, last = _steps([pl.program_id(d) for d in range(len(grid))], grid)
        ride.run(ride_in, ride_out, ride_scratch, (first, mid, None))
        body(*pre, *ins, *outs, *scratch)
        ride.run(ride_in, ride_out, ride_scratch, (None, None, last))

    return hosted


def _exchange_alone(name, ex):
    n_in, n_out, _ = ex.counts()

    def body(*refs):
        for phase in ex.phases:
            if phase is not None:
                phase(refs[:n_in], refs[n_in:n_in + n_out], refs[n_in + n_out:])

    return pl.pallas_call(
        body, name=name, in_specs=[ANY] * n_in, out_specs=[ANY] * n_out, out_shape=ex.out_shape,
        scratch_shapes=ex.scratch)(*ex.ins)


def _gather_exchange(shards):
    nw = len(shards)

    def parts(ins, outs, sems):
        send_sems, recv_sems, local_sems = sems
        x, y, c = _place()

        def slot(px, py, pc):
            return 4 * px + 2 * py + pc

        def copy(w, k, rows, to, src=None):
            return pltpu.make_async_remote_copy(
                src_ref=rows if src is None else src, dst_ref=rows, send_sem=send_sems.at[w, k],
                recv_sem=recv_sems.at[w, k], device_id=to, device_id_type=MESH)

        def plan(w, mine):
            side = c if mine else 1 - c
            half = shards[w].shape[0] // 2
            whole = lambda px, py: outs[w].at[slot(px, py, side)]
            top = lambda px, py: outs[w].at[slot(px, py, side), pl.ds(0, half)]
            bottom = lambda px, py: outs[w].at[slot(px, py, side), pl.ds(half, half)]
            xn, yn, sib = (1 - x, y, side), (x, 1 - y, side), (x, y, 1 - side)
            own = ins[w] if mine else None
            return [copy(w, 0, whole(x, y), sib, own), copy(w, 1, whole(x, y), xn, own),
                    copy(w, 2, whole(x, y), yn, own), copy(w, 3, top(1 - x, y), yn), copy(w, 4, bottom(x, 1 - y), xn),
                    copy(w, 5, whole(1 - x, y), sib), copy(w, 6, whole(x, 1 - y), sib),
                    copy(w, 7, top(1 - x, 1 - y), sib), copy(w, 8, bottom(1 - x, 1 - y), sib)]

        def arrivals(w):
            half = shards[w].shape[0] // 2
            at = lambda px, py, *rows: outs[w].at[(slot(px, py, c),) + rows]
            return {1: copy(w, 1, at(1 - x, y), (x, y, c)), 2: copy(w, 2, at(x, 1 - y), (x, y, c)),
                    3: copy(w, 3, at(1 - x, 1 - y, pl.ds(0, half)), (x, y, c)),
                    4: copy(w, 4, at(1 - x, 1 - y, pl.ds(half, half)), (x, y, c))}

        local = [pltpu.make_async_copy(ins[w], outs[w].at[slot(x, y, c)], local_sems.at[w]) for w in range(nw)]
        return plan, arrivals, local

    def start(ins, outs, sems):
        plan, _, local = parts(ins, outs, sems)
        for cp in local:
            cp.start()
        for w in range(nw):
            for k in (0, 1, 2):
                plan(w, True)[k].start()

    def middle(ins, outs, sems):
        plan, arrivals, _ = parts(ins, outs, sems)
        for landed, onward in ((1, (3, 5)), (2, (4, 6))):
            for w in range(nw):
                arrivals(w)[landed].wait_recv()
                for k in onward:
                    plan(w, True)[k].start()

    def finish(ins, outs, sems):
        plan, arrivals, local = parts(ins, outs, sems)
        for landed, onward in ((3, 7), (4, 8)):
            for w in range(nw):
                arrivals(w)[landed].wait_recv()
                plan(w, True)[onward].start()
        for w in range(nw):
            from_sibling = plan(w, False)
            for k in (0, 5, 6, 7, 8):
                from_sibling[k].wait_recv()
            for cp in plan(w, True):
                cp.wait_send()
        for cp in local:
            cp.wait()

    return _Exchange(
        shards, [jax.ShapeDtypeStruct((N_DEV,) + s.shape, s.dtype) for s in shards],
        [pltpu.SemaphoreType.DMA((nw, 9)), pltpu.SemaphoreType.DMA((nw, 9)), pltpu.SemaphoreType.DMA((nw,))],
        (start, middle, finish))


def _sibling_exchange(grads):
    nw = len(grads)

    def copies(ins, outs, sems):
        x, y, c = _place()
        return [pltpu.make_async_remote_copy(
            src_ref=ins[w].at[2 * p + (1 - c)], dst_ref=outs[w].at[p], send_sem=sems[0].at[w, p],
            recv_sem=sems[1].at[w, p], device_id=(x, y, 1 - c), device_id_type=MESH)
            for w in range(nw) for p in range(4)]

    def start(ins, outs, sems):
        for cp in copies(ins, outs, sems):
            cp.start()

    def finish(ins, outs, sems):
        for cp in copies(ins, outs, sems):
            cp.wait()

    return _Exchange(grads, [jax.ShapeDtypeStruct((4,) + g.shape[1:], g.dtype) for g in grads],
                     [pltpu.SemaphoreType.DMA((nw, 4)), pltpu.SemaphoreType.DMA((nw, 4))], (start, None, finish))


def _chips_exchange(sums):
    nw = len(sums)

    def copies(ins, outs, sems):
        x, y, c = _place()
        return [pltpu.make_async_remote_copy(
            src_ref=ins[w].at[2 * px + py], dst_ref=outs[w].at[k], send_sem=sems[0].at[w, k],
            recv_sem=sems[1].at[w, k], device_id=(px, py, c), device_id_type=MESH)
            for w in range(nw) for k, (px, py) in enumerate(_other_chips(x, y))]

    def start(ins, outs, sems):
        for cp in copies(ins, outs, sems):
            cp.start()

    def finish(ins, outs, sems):
        for cp in copies(ins, outs, sems):
            cp.wait()

    return _Exchange(sums, [jax.ShapeDtypeStruct((3,) + g.shape[1:], g.dtype) for g in sums],
                     [pltpu.SemaphoreType.DMA((nw, 3)), pltpu.SemaphoreType.DMA((nw, 3))], (start, None, finish))


def _pair_sum(name, g, got, c_arr, tr):
    _, rows, cols = g.shape
    tr = _tile_rows(rows, tr)

    def body(c_ref, a_ref, b_ref, o_ref):
        o_ref[...] = (a_ref[...].astype(F32) + b_ref[...].astype(F32)).astype(BF16)

    return pl.pallas_call(
        body, name=name,
        grid_spec=pltpu.PrefetchScalarGridSpec(
            num_scalar_prefetch=1, grid=(4, rows // tr),
            in_specs=[pl.BlockSpec((None, tr, cols), lambda p, r, cr: (2 * p + cr[0], r, 0)),
                      pl.BlockSpec((None, tr, cols), lambda p, r, cr: (p, r, 0))],
            out_specs=pl.BlockSpec((None, tr, cols), lambda p, r, cr: (p, r, 0))),
        out_shape=jax.ShapeDtypeStruct((4, rows, cols), BF16),
        compiler_params=_params(("parallel", "parallel")),
    )(c_arr, g, got)


def _tile_rows(rows, pref):
    t = min(rows, pref)
    while rows % t or t % 8:
        t -= 1
    return t


def _adam(w, g, m, v):
    m = ADAM_B1 * m + (1.0 - ADAM_B1) * g
    v = ADAM_B2 * v + (1.0 - ADAM_B2) * (g * g)
    m_hat = m / (1.0 - ADAM_B1 ** ADAM_STEP)
    v_hat = v / (1.0 - ADAM_B2 ** ADAM_STEP)
    return -ADAM_LR * (m_hat / (jnp.sqrt(v_hat) + ADAM_EPS) + ADAM_WD * w), m, v


def _adamw_shard(name, w, m, v, sums, got, chip_arr, tr):
    _, rows, cols = w.shape
    tr = _tile_rows(rows, tr)

    def body(p_ref, w_ref, m_ref, v_ref, s_ref, r_ref, g_ref, d_ref, nm_ref, nv_ref):
        g = s_ref[...].astype(F32)
        for k in range(3):
            g = g + r_ref[k].astype(F32)
        g_ref[...] = g
        d_ref[...], nm_ref[...], nv_ref[...] = _adam(w_ref[...], g, m_ref[...], v_ref[...])

    tile = pl.BlockSpec((None, tr, cols), lambda r, pr: (0, r, 0))
    return pl.pallas_call(
        body, name=name,
        grid_spec=pltpu.PrefetchScalarGridSpec(
            num_scalar_prefetch=1, grid=(rows // tr,),
            in_specs=[tile, tile, tile,
                      pl.BlockSpec((None, tr, cols), lambda r, pr: (pr[0], r, 0)),
                      pl.BlockSpec((3, tr, cols), lambda r, pr: (0, r, 0))],
            out_specs=[tile] * 4),
        out_shape=[jax.ShapeDtypeStruct((1, rows, cols), F32)] * 4,
        compiler_params=_params(("parallel",)),
    )(chip_arr, w, m, v, sums, got)


def _small_all_reduce_adam(part, w, m, v):
    rows = part.shape[0]

    def body(p_ref, w_ref, m_ref, v_ref, g_ref, d_ref, nm_ref, nv_ref, buf, send_sems, recv_sems):
        x, y, c = _place()
        me = 4 * x + 2 * y + c
        buf[me] = p_ref[...]
        peers = [(x, y, 1 - c)] + [(px, py, pc) for px, py in _other_chips(x, y) for pc in (c, 1 - c)]
        copies = []
        for k, peer in enumerate(peers):
            cp = pltpu.make_async_remote_copy(
                src_ref=buf.at[me], dst_ref=buf.at[me], send_sem=send_sems.at[k], recv_sem=recv_sems.at[k],
                device_id=peer, device_id_type=MESH)
            cp.start()
            copies.append(cp)
        for cp in copies:
            cp.wait()
        g = buf[0]
        for k in range(1, N_DEV):
            g = g + buf[k]
        g_ref[...] = g
        d_ref[...], nm_ref[...], nv_ref[...] = _adam(w_ref[...], g, m_ref[...], v_ref[...])

    vm = pl.BlockSpec(memory_space=pltpu.VMEM)
    return pl.pallas_call(
        body, name="gains_all_reduce_adamw",
        in_specs=[vm] * 4, out_specs=[vm] * 4,
        out_shape=[jax.ShapeDtypeStruct((rows, LANES), F32)] * 4,
        scratch_shapes=[pltpu.VMEM((N_DEV, rows, LANES), F32), pltpu.SemaphoreType.DMA((7,)),
                        pltpu.SemaphoreType.DMA((7,))],
        compiler_params=pltpu.CompilerParams(has_side_effects=True),
    )(part, w, m, v)


IN_ORDER = ("r_q", "r_k", "r_v", "r_g", "c_q", "c_kv", "k_pe", "g_ret", "g_mla")
RET_HEAD = 2 * RET_QK + RET_V


def _make_layout(d, vw, qw, ql, kl, mla_w):
    width = {"r_q": qw, "r_k": qw, "r_v": vw, "r_g": vw, "c_q": ql, "c_kv": kl, "k_pe": ROPE, "g_ret": d, "g_mla": d}
    src, o = {}, 0
    for n in IN_ORDER:
        src[n] = o
        o += width[n]
    heads = vw // RET_V
    off, pieces, o = {}, [], 0

    def put(name, w, s):
        nonlocal o
        off.setdefault(name, o)
        pieces.append((o, w, s))
        o += w

    for n in ("g_ret", "g_mla", "r_g"):
        put(n, width[n], src[n])
    for h in range(heads):
        put("heads", RET_QK, src["r_q"] + h * RET_QK)
        put("heads", RET_QK, src["r_k"] + h * RET_QK)
        put("heads", RET_V, src["r_v"] + h * RET_V)
    for n in ("c_q", "c_kv", "k_pe"):
        put(n, width[n], src[n])
    total = off["k_pe"] + 2 * LANES
    for n, blk in (("g_ret", d), ("g_mla", d), ("r_g", vw), ("heads", RET_HEAD), ("c_q", ql + kl), ("k_pe", 2 * LANES)):
        assert off[n] % blk == 0
    assert ql == kl and off["c_kv"] == off["c_q"] + ql
    return {"off": off, "pieces": pieces, "total": total, "n_in": sum(width.values()),
            "ret_heads": heads, "mla_heads": mla_w // VHEAD, "q_lora": ql, "kv_lora": kl}


def _cols_to_full(g):
    n, r, c = g.shape
    return jnp.transpose(g, (1, 0, 2)).reshape(r, n * c)


def _full_to_cols(w):
    r, c = w.shape
    return jnp.transpose(w.reshape(r, N_DEV, c // N_DEV), (1, 0, 2))


def _w_in_to_mine(g, lay):
    _, rows, cols = g.shape
    parts, at = [], 0
    for o, w, s in lay["pieces"]:
        if o > at:
            parts.append(jnp.zeros((rows, o - at), g.dtype))
        while w > 0:
            k, a = divmod(s, cols)
            take = min(w, cols - a)
            parts.append(g[k, :, a:a + take])
            s, w, o = s + take, w - take, o + take
        at = o
    parts.append(jnp.zeros((rows, lay["total"] - at), g.dtype))
    return jnp.concatenate(parts, axis=1)


def _mine_to_blocks(g, lay):
    cols = lay["n_in"] // N_DEV
    by_src = sorted(lay["pieces"], key=lambda p: p[2])
    blocks = []
    for k in range(N_DEV):
        lo, hi, parts = k * cols, (k + 1) * cols, []
        for o, w, s in by_src:
            a, b = max(lo, s), min(hi, s + w)
            if a < b:
                parts.append(g[:, o + a - s:o + b - s])
        blocks.append(jnp.concatenate(parts, axis=1))
    return jnp.stack(blocks)


def _rope_tables(positions, half):
    inv = ROPE_THETA ** (-jnp.arange(half, dtype=F32) / half)
    ang = positions.astype(F32)[:, None] * inv
    return jnp.cos(ang), jnp.sin(ang)


def _pack_rows(vs):
    return jnp.concatenate([v.reshape(-1, LANES) for v in vs], axis=0)


def kernel(x, positions, norm_mix_g, w_in, ret_norm_g, w_ret_o, q_a_norm_g, w_q_b, kv_a_norm_g, w_kv_b, w_mla_o, w_out, norm_mlp_g, w_up, w_down, norm_f_g, loss_target, m_norm_mix_g, m_w_in, m_ret_norm_g, m_w_ret_o, m_q_a_norm_g, m_w_q_b, m_kv_a_norm_g, m_w_kv_b, m_w_mla_o, m_w_out, m_norm_mlp_g, m_w_up, m_w_down, m_norm_f_g, v_norm_mix_g, v_w_in, v_ret_norm_g, v_w_ret_o, v_q_a_norm_g, v_w_q_b, v_kv_a_norm_g, v_w_kv_b, v_w_mla_o, v_w_out, v_norm_mlp_g, v_w_up, v_w_down, v_norm_f_g):
    xs, tgt, pos = x[0], loss_target[0], positions[0]
    s, d = xs.shape
    mats = {"w_in": w_in[0], "w_ret_o": w_ret_o[0], "w_q_b": w_q_b[0], "w_kv_b": w_kv_b[0], "w_mla_o": w_mla_o[0],
            "w_out": w_out[0], "w_up": w_up[0], "w_down": w_down[0]}
    mat_w = {"w_in": w_in, "w_ret_o": w_ret_o, "w_q_b": w_q_b, "w_kv_b": w_kv_b, "w_mla_o": w_mla_o, "w_out": w_out,
             "w_up": w_up, "w_down": w_down}
    mat_m = {"w_in": m_w_in, "w_ret_o": m_w_ret_o, "w_q_b": m_w_q_b, "w_kv_b": m_w_kv_b, "w_mla_o": m_w_mla_o,
             "w_out": m_w_out, "w_up": m_w_up, "w_down": m_w_down}
    mat_v = {"w_in": v_w_in, "w_ret_o": v_w_ret_o, "w_q_b": v_w_q_b, "w_kv_b": v_w_kv_b, "w_mla_o": v_w_mla_o,
             "w_out": v_w_out, "w_up": v_w_up, "w_down": v_w_down}
    names = list(mats)
    col_sharded = ("w_in", "w_q_b", "w_kv_b", "w_up")
    vw = ret_norm_g.shape[1]
    mla_w = mats["w_mla_o"].shape[0] * N_DEV
    ql, kl = q_a_norm_g.shape[1], kv_a_norm_g.shape[1]
    n_in = mats["w_in"].shape[1] * N_DEV
    qw = (n_in - 2 * vw - ql - kl - ROPE - 2 * d) // 2
    lay = _make_layout(d, vw, qw, ql, kl, mla_w)
    assert lay["n_in"] == n_in
    heads_r, heads_m = lay["ret_heads"], lay["mla_heads"]

    shard16 = {n: mats[n].astype(BF16) for n in names}
    with_in_proj = ("w_ret_o", "w_q_b", "w_kv_b", "w_mla_o", "w_out")
    mlp = ("w_up", "w_down")
    by_device = ("w_up", "w_kv_b")
    full = {}

    def keep(group, gathered):
        for n, g in zip(group, gathered):
            if n not in by_device:
                g = _cols_to_full(g) if n in col_sharded else g.reshape(-1, g.shape[2])
            full[n] = g

    w_mine = _w_in_to_mine(_exchange_alone("gather_w_in", _gather_exchange([shard16["w_in"]]))[0], lay)

    c64, s64 = _rope_tables(pos, RET_QK // 2)
    cos_r = jnp.concatenate([c64, c64], axis=1)
    sin_r = jnp.concatenate([-s64, s64], axis=1)
    c32, s32 = _rope_tables(pos, ROPE // 2)
    z32, z64 = jnp.zeros_like(c32), jnp.zeros((s, LANES - ROPE), F32)
    cos_p = jnp.concatenate([c32, c32, z64], axis=1)
    sin_a = jnp.concatenate([-s32, z32, z64], axis=1)
    sin_b = jnp.concatenate([z32, s32, z64], axis=1)
    lg = jnp.log(1.0 - 2.0 ** (-5.0 - jnp.arange(heads_r, dtype=F32)))
    lgs = jnp.broadcast_to(lg[:, None, None], (heads_r, 8, LANES))

    tm = min(256, s)
    blk = min(512, s)
    t_att = min(512, s)

    u = _rms_fwd("norm_mix", xs, norm_mix_g, tm)
    proj, gathered = _mm("in_proj", u, w_mine, "nn", F32,
                         ride=_gather_exchange([shard16[n] for n in with_in_proj]))
    keep(with_in_proj, gathered)
    wq_pad = jnp.pad(full["w_q_b"].reshape(ql, heads_m, NOPE + ROPE),
                     ((0, 0), (0, 0), (0, QPAD - NOPE - ROPE))).reshape(ql, heads_m * QPAD)
    o_ret, states = _ret_fwd(proj, lay, cos_r, sin_r, lgs, blk)
    ry = _ret_post(proj, lay, o_ret, ret_norm_g, tm)
    y_ret = _mm("ret_out", ry, full["w_ret_o"], "nn", F32)
    cqn, ckvn, kpr = _mla_prep(proj, lay, q_a_norm_g, kv_a_norm_g, cos_p, sin_a, sin_b, tm)
    qf = _mm("q_up", cqn, wq_pad, "nn", BF16, extras=((cos_p, None), (sin_a, None), (sin_b, None)), epilogue=_q_operand)
    kv, kf = _mm("kv_up", ckvn, full["w_kv_b"], "nn", (BF16, BF16), b_by_device=True, extras=((kpr, None),),
                 epilogue=_k_operand)
    o_mla, lse, gathered = _attn_fwd(qf, kf, kv, lay, t_att, ride=_gather_exchange([shard16["w_up"]]))
    keep(("w_up",), gathered)
    gate_tile = _tile(d, 1024)
    y_mla, merged = _mm(
        "mla_out", o_mla, full["w_mla_o"], "nn", (F32, BF16), tm=512, tn=gate_tile,
        extras=((proj, lay["off"]["g_ret"] // gate_tile), (proj, lay["off"]["g_mla"] // gate_tile), y_ret),
        epilogue=lambda r, gr, gm, yr: (r, _sig(gr) * yr + _sig(gm) * r))
    mix = _mm("out_proj", merged, full["w_out"], "nn", F32)
    h1, n2 = _rms_res_fwd("norm_mlp", xs, mix, norm_mlp_g, tm)
    (z, act), gathered = _mm("mlp_up", n2, full["w_up"], "nn", (F32, BF16), b_by_device=True,
                             epilogue=lambda r: (r, jnp.square(jnp.maximum(r, 0.0))),
                             ride=_gather_exchange([shard16["w_down"]]))
    keep(("w_down",), gathered)
    dn = _mm("mlp_down", act, full["w_down"], "nn", F32)
    dh2, g_norm_f, loss_part = _final("loss_head", h1, dn, norm_f_g.reshape(1, d), tgt, tm)

    mx, my, mc = _place()
    c_arr = jnp.reshape(mc, (1,)).astype(jnp.int32)
    chip_arr = jnp.reshape(2 * mx + my, (1,)).astype(jnp.int32)
    sums, from_chips = {}, {}

    def blocks(group, grads):
        return [g if n in by_device else (_full_to_cols(g) if n in col_sharded else g.reshape((N_DEV,) + mats[n].shape))
                for n, g in zip(group, grads)]

    def pair_sums(group, mine, from_sibling):
        for n, g, r in zip(group, mine, from_sibling):
            sums[n] = _pair_sum("pair_sum_" + n, g, r, c_arr, 256)
        return [sums[n] for n in group]

    dz = _mm("mlp_down_dx", dh2, full["w_down"], "nt", BF16, extras=(z,),
             epilogue=lambda r, zz: (r * (2.0 * jnp.maximum(zz, 0.0)),))
    g_w_down = _mm("mlp_down_dw", act, dh2, "tn", BF16)
    down_blocks = blocks(("w_down",), (g_w_down,))
    g_w_up, got_down = _mm("mlp_up_dw", n2, dz, "tn", BF16, out_by_device=True, ride=_sibling_exchange(down_blocks))
    dn2, got_up = _mm("mlp_up_dx", dz, full["w_up"], "nt", F32, b_by_device=True, ride=_sibling_exchange([g_w_up]))
    mlp_sums = pair_sums(mlp, [g_w_up] + down_blocks, list(got_up) + list(got_down))
    dh1, g_norm_mlp = _rms_bwd("norm_mlp_bwd", dn2, h1, norm_mlp_g, dh2, tm)
    d_merged = _mm("out_proj_dx", dh1, full["w_out"], "nt", F32)
    g_w_out = _mm("out_proj_dw", merged, dh1, "tn", BF16)
    dy_ret, dy_mla, d_proj = _merge_bwd(proj, lay, d_merged, y_ret, y_mla, tm)
    g_w_ret_o = _mm("ret_out_dw", ry, dy_ret, "tn", BF16)
    g_w_mla_o = _mm("mla_out_dw", o_mla, dy_mla, "tn", BF16)
    mixer = ("w_out", "w_ret_o", "w_mla_o")
    mixer_blocks = blocks(mixer, (g_w_out, g_w_ret_o, g_w_mla_o))
    d_ry, got = _mm("ret_out_dx", dy_ret, full["w_ret_o"], "nt", F32, ride=_sibling_exchange(mixer_blocks))
    mixer_sums = pair_sums(mixer, mixer_blocks, got)
    d_omla = _mm("mla_out_dx", dy_mla, full["w_mla_o"], "nt", F32)
    d_oret, d_proj, g_ret_norm = _ret_post_bwd(proj, lay, d_ry, o_ret, ret_norm_g, d_proj, tm)
    d_proj, got = _ret_bwd(proj, lay, cos_r, sin_r, lgs, states, d_oret, d_proj, blk,
                           ride=_chips_exchange(mixer_sums))
    from_chips.update(zip(mixer, got))
    delta = _attn_delta(d_omla, o_mla, lay, t_att)
    dqp, dkv, dkpe_parts, got = _attn_bwd(qf, kf, kv, lse, delta, d_omla, cos_p, sin_a, sin_b, lay, t_att,
                                          ride=_chips_exchange(mlp_sums))
    from_chips.update(zip(mlp, got))
    d_proj = _rope_key_grad(dkpe_parts, lay, d_proj, tm)
    d_cqn = _mm("q_up_dx", dqp, wq_pad, "nt", F32)
    g_wq_pad = _mm("q_up_dw", cqn, dqp, "tn", BF16)
    d_ckvn = _mm("kv_up_dx", dkv, full["w_kv_b"], "nt", F32, b_by_device=True)
    g_w_kv_b = _mm("kv_up_dw", ckvn, dkv, "tn", BF16, out_by_device=True)
    d_proj, g_q_a, g_kv_a = _mla_prep_bwd(proj, lay, d_cqn, d_ckvn, q_a_norm_g, kv_a_norm_g, d_proj, tm)
    g_w_mine = _mm("in_proj_dw", u, d_proj, "tn", BF16)
    g_w_q_b = g_wq_pad.reshape(ql, heads_m, QPAD)[:, :, :NOPE + ROPE].reshape(ql, heads_m * (NOPE + ROPE))
    last = ("w_in", "w_q_b", "w_kv_b")
    last_blocks = [_mine_to_blocks(g_w_mine, lay)] + blocks(last[1:], (g_w_q_b, g_w_kv_b))
    last_sums = pair_sums(last, last_blocks, _exchange_alone("grads_to_sibling", _sibling_exchange(last_blocks)))
    du, got = _mm("in_proj_dx", d_proj, w_mine, "nt", F32, ride=_chips_exchange(last_sums))
    from_chips.update(zip(last, got))
    grad_x, g_norm_mix = _rms_bwd("norm_mix_bwd", du, xs, norm_mix_g, dh1, tm)

    upd = {n: _adamw_shard("adamw_" + n, mat_w[n], mat_m[n], mat_v[n], sums[n], from_chips[n], chip_arr, 256)
           for n in names}

    gains = [("norm_mix_g", norm_mix_g, m_norm_mix_g, v_norm_mix_g, g_norm_mix),
             ("ret_norm_g", ret_norm_g, m_ret_norm_g, v_ret_norm_g, g_ret_norm),
             ("q_a_norm_g", q_a_norm_g, m_q_a_norm_g, v_q_a_norm_g, g_q_a),
             ("kv_a_norm_g", kv_a_norm_g, m_kv_a_norm_g, v_kv_a_norm_g, g_kv_a),
             ("norm_mlp_g", norm_mlp_g, m_norm_mlp_g, v_norm_mlp_g, g_norm_mlp),
             ("norm_f_g", norm_f_g, m_norm_f_g, v_norm_f_g, g_norm_f)]
    n_rows = sum(g[1].size for g in gains) // LANES
    pad_rows = -(-(n_rows + 1) // 8) * 8 - n_rows
    tail = jnp.zeros((pad_rows, LANES), F32)
    part = jnp.concatenate([_pack_rows([g[4] for g in gains]),
                            jnp.broadcast_to(loss_part[:, :1], (1, LANES)), tail[1:]], axis=0)
    packed = [jnp.concatenate([_pack_rows([g[k] for g in gains]), tail], axis=0) for k in (1, 2, 3)]
    g_sm, d_sm, m_sm, v_sm = _small_all_reduce_adam(part, *packed)
    loss = g_sm[n_rows, 0]
    small = {}
    o = 0
    for name, w, _, _, _ in gains:
        r = w.size // LANES
        small[name] = [a[o:o + r].reshape(w.shape) for a in (g_sm, d_sm, m_sm, v_sm)]
        o += r

    order = ["norm_mix_g", "w_in", "ret_norm_g", "w_ret_o", "q_a_norm_g", "w_q_b", "kv_a_norm_g", "w_kv_b", "w_mla_o",
             "w_out", "norm_mlp_g", "w_up", "w_down", "norm_f_g"]
    outs = [loss, grad_x[None]]
    for k in range(4):
        for n in order:
            outs.append(small[n][k] if n in small else upd[n][k])
    return tuple(outs)
```

```python
import functools
import math

import jax
import jax.numpy as jnp
from jax import lax
from jax.experimental import pallas as pl
from jax.experimental.pallas import tpu as pltpu

F32 = jnp.float32
BF16 = jnp.bfloat16
MESH = pl.DeviceIdType.MESH

EPS = 1e-6
ROPE_THETA = 10000.0
CHUNK_SHIFT = 6
RET_QK = 128
RET_V = 256
NOPE = 128
ROPE = 64
VHEAD = 128
QPAD = 256
LANES = 128
N_DEV = 8
VMEM_LIMIT = 56 * 1024 * 1024

ADAM_LR = 0.001
ADAM_B1 = 0.9
ADAM_B2 = 0.999
ADAM_EPS = 1e-08
ADAM_WD = 0.01
ADAM_STEP = 10

NN = (((1,), (0,)), ((), ()))
NT = (((1,), (1,)), ((), ()))
TN = (((0,), (0,)), ((), ()))


def _dot(a, b, dims):
    return lax.dot_general(a.astype(BF16), b.astype(BF16), dims, preferred_element_type=F32)


def _tile(dim, pref):
    if dim <= pref:
        return dim
    t = (pref // LANES) * LANES
    while t >= LANES:
        if dim % t == 0:
            return t
        t -= LANES
    raise ValueError(f"no tile for {dim}")


def _params(sem):
    return pltpu.CompilerParams(dimension_semantics=sem, vmem_limit_bytes=VMEM_LIMIT)


def _sig(v):
    return 1.0 / (1.0 + jnp.exp(-v))


def _mm(name, a, b, mode, out_dtypes, *, tm=1024, tn=1024, tk=2048, extras=(), epilogue=None, ride=None,
        b_by_device=False, out_by_device=False):
    if b_by_device:
        b_cols = b.shape[2]
        b_shape = (b.shape[1], N_DEV * b_cols)
    else:
        b_shape = b.shape
    if mode == "nn":
        (m, k), (_, n) = a.shape, b_shape
    elif mode == "nt":
        (m, k), (n, _) = a.shape, b_shape
    else:
        (k, m), (_, n) = a.shape, b_shape
    tm, tn, tk = _tile(m, tm), _tile(n, tn), _tile(k, tk)
    if b_by_device and mode != "nt":
        tn = _tile(b_cols, tn)
    if out_by_device:
        tn = _tile(n // N_DEV, tn)
    nk = k // tk
    dims = {"nn": NN, "nt": NT, "tn": TN}[mode]
    a_spec = (pl.BlockSpec((tk, tm), lambda i, j, kk: (kk, i)) if mode == "tn"
              else pl.BlockSpec((tm, tk), lambda i, j, kk: (i, kk)))
    if b_by_device and mode == "nt":
        piece = min(tk, b_cols)
        n_b, per = tk // piece, b_cols // piece
        b_specs = [pl.BlockSpec((None, tn, piece),
                                lambda i, j, kk, p=p: ((kk * n_b + p) // per, j, (kk * n_b + p) % per))
                   for p in range(n_b)]
    elif b_by_device:
        per = b_cols // tn
        n_b, piece = 1, tk
        b_specs = [pl.BlockSpec((None, tk, tn), lambda i, j, kk: (j // per, kk, j % per))]
    else:
        n_b, piece = 1, tk
        b_specs = [pl.BlockSpec((tn, tk), lambda i, j, kk: (j, kk)) if mode == "nt"
                   else pl.BlockSpec((tk, tn), lambda i, j, kk: (kk, j))]
    tile_spec = pl.BlockSpec((tm, tn), lambda i, j, kk: (i, j))
    if out_by_device:
        per_out = n // N_DEV // tn
        out_spec = pl.BlockSpec((None, tm, tn), lambda i, j, kk: (j // per_out, i, j % per_out))
        out_dims = (N_DEV, m, n // N_DEV)
    else:
        out_spec, out_dims = tile_spec, (m, n)
    ex_arrays, ex_specs = [], []
    for e in extras:
        arr, off = e if isinstance(e, tuple) else (e, 0)
        ex_arrays.append(arr)
        if off == "whole":
            ex_specs.append(pl.BlockSpec(arr.shape, lambda i, j, kk, nd=arr.ndim: (0,) * nd))
        elif off is None:
            ex_specs.append(pl.BlockSpec((tm, arr.shape[1]), lambda i, j, kk: (i, 0)))
        else:
            ex_specs.append(pl.BlockSpec((tm, tn), lambda i, j, kk, off=off: (i, j + off)))
    n_ex = len(extras)
    single = not isinstance(out_dtypes, (tuple, list))
    dts = (out_dtypes,) if single else tuple(out_dtypes)
    out_specs, out_shapes = [], []
    for dt in dts:
        if isinstance(dt, tuple):
            dt, mult, width = dt
            out_specs.append(pl.BlockSpec((tm, mult * tn), lambda i, j, kk: (i, j)))
            out_shapes.append(jax.ShapeDtypeStruct((m, width), dt))
        else:
            out_specs.append(out_spec)
            out_shapes.append(jax.ShapeDtypeStruct(out_dims, dt))

    grid = (m // tm, n // tn, nk)
    r_in, r_out, r_sc = ride.counts() if ride else (0, 0, 0)
    n_acc = 1 if nk > 1 else 0

    def body(a_ref, *rest):
        b_refs, rest = rest[:n_b], rest[n_b:]
        ex, rest = rest[:n_ex], rest[n_ex:]
        ride_in, rest = rest[:r_in], rest[r_in:]
        outs, rest = rest[:len(dts)], rest[len(dts):]
        ride_out, rest = rest[:r_out], rest[r_out:]
        ride_scratch = rest[n_acc:]
        if ride:
            first, mid, last = _steps([pl.program_id(d) for d in range(3)], grid)
            ride.run(ride_in, ride_out, ride_scratch, (first, mid, None))

        def finish(r):
            vals = (r,) if epilogue is None else epilogue(r, *[e[...] for e in ex])
            for o, v in zip(outs, vals):
                o[...] = v.astype(o.dtype)

        if n_b == 1:
            part = _dot(a_ref[...], b_refs[0][...], dims)
        else:
            part = sum(_dot(a_ref[:, p * piece:(p + 1) * piece], b_refs[p][...], dims) for p in range(n_b))
        if nk == 1:
            finish(part)
        else:
            acc = rest[0]
            kk = pl.program_id(2)

            @pl.when(kk == 0)
            def _():
                acc[...] = part

            @pl.when(jnp.logical_and(kk > 0, kk < nk - 1))
            def _():
                acc[...] += part

            @pl.when(kk == nk - 1)
            def _():
                finish(acc[...] + part)

        if ride:
            ride.run(ride_in, ride_out, ride_scratch, (None, None, last))

    res = pl.pallas_call(
        body, name=name, grid=grid,
        in_specs=[a_spec] + b_specs + ex_specs + [ANY] * r_in,
        out_specs=out_specs + [ANY] * r_out,
        out_shape=out_shapes + (ride.out_shape if ride else []),
        scratch_shapes=([pltpu.VMEM((tm, tn), F32)] if nk > 1 else []) + (ride.scratch if ride else []),
        compiler_params=_params(("arbitrary",) * 3 if ride else ("parallel", "parallel", "arbitrary")),
    )(a, *[b] * n_b, *ex_arrays, *(ride.ins if ride else []))
    own = res[0] if single else res[:len(dts)]
    return (own, res[len(dts):]) if ride else own


def _rows(name, body, n_rows, tm, ins, outs, accs=(), into=None):
    in_specs, args = [], []
    for t in ins:
        if len(t) == 1:
            in_specs.append(pl.BlockSpec(t[0].shape, lambda i, nd=t[0].ndim: (0,) * nd))
        else:
            in_specs.append(pl.BlockSpec((tm, t[1]), lambda i, cb=t[2]: (i, cb)))
        args.append(t[0])
    outs = [(o + (o[0], 0))[:4] for o in outs]
    out_specs = [pl.BlockSpec((tm, w), lambda i, cb=cb: (i, cb)) for w, _, _, cb in outs]
    out_shape = [jax.ShapeDtypeStruct((n_rows, total), d) for _, d, total, _ in outs]
    aliases, kernel = {}, body
    if into is not None:
        arr, w, cb = into
        in_specs.append(ANY)
        args.append(arr)
        out_specs.append(pl.BlockSpec((tm, w), lambda i: (i, cb)))
        out_shape.append(jax.ShapeDtypeStruct(arr.shape, arr.dtype))
        aliases = {len(ins): len(outs)}
        n_in = len(ins)

        def kernel(*refs):
            body(*refs[:n_in], *refs[n_in + 1:])

    out_specs += [pl.BlockSpec((r, w), lambda i: (0, 0)) for r, w in accs]
    out_shape += [jax.ShapeDtypeStruct((r, w), F32) for r, w in accs]
    return pl.pallas_call(
        kernel, name=name, grid=(n_rows // tm,), in_specs=in_specs, out_specs=out_specs, out_shape=out_shape,
        input_output_aliases=aliases, compiler_params=_params(("arbitrary",) if accs else ("parallel",)),
    )(*args)


def _zero_first(*accs):
    @pl.when(pl.program_id(0) == 0)
    def _():
        for a in accs:
            a[...] = jnp.zeros_like(a)


def _rope64(t, cos, sin):
    return t * cos + pltpu.roll(t, RET_QK // 2, 1) * sin


def _rope32(t, cos, sin_a, sin_b):
    return t * cos + pltpu.roll(t, LANES - ROPE // 2, 1) * sin_a + pltpu.roll(t, ROPE // 2, 1) * sin_b


def _rms_fwd(name, x, g, tm):
    s, d = x.shape

    def body(x_ref, g_ref, u_ref):
        v = x_ref[...]
        r = lax.rsqrt(jnp.mean(v * v, axis=-1, keepdims=True) + EPS)
        u_ref[...] = (v * r * g_ref[...]).astype(BF16)

    return _rows(name, body, s, tm, [(x, d, 0), (g,)], [(d, BF16)])[0]


def _residual_norm(r, x, g):
    h = x + r
    return h, h * lax.rsqrt(jnp.mean(h * h, axis=-1, keepdims=True) + EPS) * g


def _gate_grads(dm, gr, gm, yr, ym):
    sr, sm = _sig(gr), _sig(gm)
    return dm * sr, dm * sm, jnp.concatenate([dm * yr * (sr * (1.0 - sr)), dm * ym * (sm * (1.0 - sm))], axis=1)


def _rms_bwd(name, dy, x, g, dres, tm):
    s, d = x.shape

    def body(dy_ref, x_ref, g_ref, dres_ref, dx_ref, dg_ref):
        _zero_first(dg_ref)
        v, dyv = x_ref[...], dy_ref[...]
        r = lax.rsqrt(jnp.mean(v * v, axis=-1, keepdims=True) + EPS)
        xh = v * r
        dxh = dyv * g_ref[...]
        dx_ref[...] = dres_ref[...] + r * (dxh - xh * jnp.mean(dxh * xh, axis=-1, keepdims=True))
        dg_ref[...] += jnp.sum(dyv * xh, axis=0, keepdims=True)

    return _rows(name, body, s, tm, [(dy, d, 0), (x, d, 0), (g,), (dres, d, 0)], [(d, F32)], [(1, d)])


def _final(name, h1, dn, g, tgt, tm):
    s, d = h1.shape

    def body(h_ref, dn_ref, g_ref, t_ref, dh_ref, dg_ref, loss_ref):
        _zero_first(dg_ref, loss_ref)
        v = h_ref[...] + dn_ref[...]
        r = lax.rsqrt(jnp.mean(v * v, axis=-1, keepdims=True) + EPS)
        xh = v * r
        gv = g_ref[...]
        e = xh * gv - t_ref[...]
        loss_ref[...] += 0.5 * jnp.sum(jnp.mean(e * e, axis=-1, keepdims=True))
        dy = e * (1.0 / d)
        dg_ref[...] += jnp.sum(dy * xh, axis=0, keepdims=True)
        dxh = dy * gv
        dh_ref[...] = r * (dxh - xh * jnp.mean(dxh * xh, axis=-1, keepdims=True))

    return _rows(name, body, s, tm, [(h1, d, 0), (dn, d, 0), (g,), (tgt, d, 0)], [(d, F32)], [(1, d), (1, LANES)])


def _decay_mask(lg, blk):
    n = lax.broadcasted_iota(jnp.int32, (blk, blk), 0)
    m = lax.broadcasted_iota(jnp.int32, (blk, blk), 1)
    w = jnp.exp(lg * jnp.abs(n - m).astype(F32))
    return jnp.where(jnp.right_shift(m, CHUNK_SHIFT) <= jnp.right_shift(n, CHUNK_SHIFT), w, 0.0)


def _decays(lg, blk):
    pos = lax.broadcasted_iota(jnp.int32, (blk, 1), 0).astype(F32)
    return jnp.exp(lg * (pos + 1.0)), jnp.exp(lg * (blk - 1.0 - pos)), jnp.exp(lg * float(blk))


def _ret_fwd(proj, lay, cos, sin, lgs, blk, ride=None):
    s = proj.shape[0]
    heads = lay["ret_heads"]
    nb = s // blk
    scale = RET_QK ** -0.5
    ride_in_specs, ride_ins, ride_out_specs, ride_out_shape, ride_scratch = _ride_args(ride)

    def body(lg_ref, qkv_ref, cos_ref, sin_ref, o_ref, st_ref, state, mask):
        lg = lg_ref[0:1, 0:1]

        @pl.when(pl.program_id(1) == 0)
        def _():
            state[...] = jnp.zeros_like(state)
            mask[...] = _decay_mask(lg, blk)

        a, c, gb = _decays(lg, blk)
        q = _rope64(qkv_ref[:, :RET_QK], cos_ref[...], sin_ref[...])
        k = _rope64(qkv_ref[:, RET_QK:2 * RET_QK], cos_ref[...], sin_ref[...]) * scale
        v = qkv_ref[:, 2 * RET_QK:]
        st = state[...]
        st_ref[...] = st
        sm = _dot(q, k, NT) * mask[...]
        o_ref[...] = _dot(sm, v, NN) + _dot(q * a, st, NN)
        state[...] = st * gb + _dot(k * c, v, TN)

    first = lay["off"]["heads"] // RET_HEAD
    res = pl.pallas_call(
        _with_ride(body, ride, (heads, nb), 0, 4, 2), name="ret_fwd", grid=(heads, nb),
        in_specs=[pl.BlockSpec((None, 8, LANES), lambda h, b: (h, 0, 0)),
                  pl.BlockSpec((blk, RET_HEAD), lambda h, b: (b, first + h)),
                  pl.BlockSpec((blk, LANES), lambda h, b: (b, 0)),
                  pl.BlockSpec((blk, LANES), lambda h, b: (b, 0))] + ride_in_specs,
        out_specs=[pl.BlockSpec((blk, RET_V), lambda h, b: (b, h)),
                   pl.BlockSpec((None, None, RET_QK, RET_V), lambda h, b: (h, b, 0, 0))] + ride_out_specs,
        out_shape=[jax.ShapeDtypeStruct((s, heads * RET_V), F32),
                   jax.ShapeDtypeStruct((heads, nb, RET_QK, RET_V), F32)] + ride_out_shape,
        scratch_shapes=[pltpu.VMEM((RET_QK, RET_V), F32), pltpu.VMEM((blk, blk), F32)] + ride_scratch,
        compiler_params=_params(("arbitrary", "arbitrary") if ride else ("parallel", "arbitrary")),
    )(lgs, proj, cos, sin, *ride_ins)
    return (res[0], res[1], res[2:]) if ride else res


def _ret_bwd(proj, lay, cos, sin, lgs, states, d_o, d_proj, blk, ride=None):
    ride_in_specs, ride_ins, ride_out_specs, ride_out_shape, ride_scratch = _ride_args(ride)
    s = proj.shape[0]
    heads = lay["ret_heads"]
    nb = s // blk
    scale = RET_QK ** -0.5

    def body(lg_ref, qkv_ref, cos_ref, sin_ref, st_ref, do_ref, _, dqkv_ref, dstate, mask):
        lg = lg_ref[0:1, 0:1]

        @pl.when(pl.program_id(1) == 0)
        def _():
            dstate[...] = jnp.zeros_like(dstate)
            mask[...] = _decay_mask(lg, blk)

        a, c, gb = _decays(lg, blk)
        cs, sn = cos_ref[...], sin_ref[...]
        q = _rope64(qkv_ref[:, :RET_QK], cs, sn)
        k = _rope64(qkv_ref[:, RET_QK:2 * RET_QK], cs, sn) * scale
        v = qkv_ref[:, 2 * RET_QK:]
        st = st_ref[...]
        do = do_ref[...]
        dst = dstate[...]
        mk = mask[...]
        sm = _dot(q, k, NT) * mk
        ds = _dot(do, v, NT) * mk
        dq = _dot(ds, k, NN) + _dot(do, st, NT) * a
        dk = _dot(ds, q, TN) + _dot(v, dst, NT) * c
        dqkv_ref[:, 2 * RET_QK:] = (_dot(sm, do, TN) + _dot(k * c, dst, NN)).astype(BF16)
        dstate[...] = dst * gb + _dot(q * a, do, TN)
        dqkv_ref[:, :RET_QK] = _rope64(dq, cs, -sn).astype(BF16)
        dqkv_ref[:, RET_QK:2 * RET_QK] = (_rope64(dk, cs, -sn) * scale).astype(BF16)

    first = lay["off"]["heads"] // RET_HEAD
    last = nb - 1
    res = pl.pallas_call(
        _with_ride(body, ride, (heads, nb), 0, 7, 1), name="ret_bwd", grid=(heads, nb),
        in_specs=[pl.BlockSpec((None, 8, LANES), lambda h, b: (h, 0, 0)),
                  pl.BlockSpec((blk, RET_HEAD), lambda h, b: (last - b, first + h)),
                  pl.BlockSpec((blk, LANES), lambda h, b: (last - b, 0)),
                  pl.BlockSpec((blk, LANES), lambda h, b: (last - b, 0)),
                  pl.BlockSpec((None, None, RET_QK, RET_V), lambda h, b: (h, last - b, 0, 0)),
                  pl.BlockSpec((blk, RET_V), lambda h, b: (last - b, h)), ANY] + ride_in_specs,
        out_specs=[pl.BlockSpec((blk, RET_HEAD), lambda h, b: (last - b, first + h))] + ride_out_specs,
        out_shape=[jax.ShapeDtypeStruct(d_proj.shape, d_proj.dtype)] + ride_out_shape,
        scratch_shapes=[pltpu.VMEM((RET_QK, RET_V), F32), pltpu.VMEM((blk, blk), F32)] + ride_scratch,
        input_output_aliases={6: 0},
        compiler_params=_params(("arbitrary", "arbitrary") if ride else ("parallel", "arbitrary")),
    )(lgs, proj, cos, sin, states, d_o, d_proj, *ride_ins)
    return (res[0], res[1:]) if ride else res[0]


def _ret_post(proj, lay, o, g, tm):
    s, vw = o.shape
    heads = lay["ret_heads"]

    def body(o_ref, rg_ref, g_ref, ry_ref):
        for h in range(heads):
            sl = slice(h * RET_V, (h + 1) * RET_V)
            oh = o_ref[:, sl]
            dlt = oh - jnp.mean(oh, axis=-1, keepdims=True)
            rstd = lax.rsqrt(jnp.mean(dlt * dlt, axis=-1, keepdims=True) + EPS)
            rg = rg_ref[:, sl]
            ry_ref[:, sl] = (dlt * rstd * g_ref[:, sl] * (rg * _sig(rg))).astype(BF16)

    return _rows("ret_post", body, s, tm, [(o, vw, 0), (proj, vw, lay["off"]["r_g"] // vw), (g,)], [(vw, BF16)])[0]


def _ret_post_bwd(proj, lay, d_ry, o, g, d_proj, tm):
    s, vw = o.shape
    heads = lay["ret_heads"]

    def body(dry_ref, o_ref, rg_ref, g_ref, do_ref, drg_ref, dg_ref):
        _zero_first(dg_ref)
        for h in range(heads):
            sl = slice(h * RET_V, (h + 1) * RET_V)
            oh = o_ref[:, sl]
            dlt = oh - jnp.mean(oh, axis=-1, keepdims=True)
            rstd = lax.rsqrt(jnp.mean(dlt * dlt, axis=-1, keepdims=True) + EPS)
            oh = dlt * rstd
            gv = g_ref[:, sl]
            rg = rg_ref[:, sl]
            sg = _sig(rg)
            dry = dry_ref[:, sl]
            dt = dry * (rg * sg)
            drg_ref[:, sl] = (dry * (oh * gv) * (sg * (1.0 + rg * (1.0 - sg)))).astype(BF16)
            dg_ref[:, sl] += jnp.sum(dt * oh, axis=0, keepdims=True)
            doh = dt * gv
            do_ref[:, sl] = rstd * (doh - jnp.mean(doh, axis=-1, keepdims=True)
                                    - oh * jnp.mean(doh * oh, axis=-1, keepdims=True))

    return _rows("ret_post_bwd", body, s, tm,
                 [(d_ry, vw, 0), (o, vw, 0), (proj, vw, lay["off"]["r_g"] // vw), (g,)],
                 [(vw, F32)], [(1, vw)], into=(d_proj, vw, lay["off"]["r_g"] // vw))


def _mla_prep(proj, lay, gq, gkv, cos, sin_a, sin_b, tm):
    s = proj.shape[0]
    ql, kl = lay["q_lora"], lay["kv_lora"]

    def body(cq_ref, ckv_ref, kpe_ref, gq_ref, gkv_ref, cos_ref, sa_ref, sb_ref, cqn_ref, ckvn_ref, kpr_ref):
        for src, gref, dst in ((cq_ref, gq_ref, cqn_ref), (ckv_ref, gkv_ref, ckvn_ref)):
            v = src[...]
            r = lax.rsqrt(jnp.mean(v * v, axis=-1, keepdims=True) + EPS)
            dst[...] = (v * r * gref[...]).astype(BF16)
        kpr_ref[...] = _rope32(kpe_ref[...], cos_ref[...], sa_ref[...], sb_ref[...]).astype(BF16)

    off = lay["off"]
    return _rows("mla_prep", body, s, tm,
                 [(proj, ql, off["c_q"] // ql), (proj, kl, off["c_kv"] // kl), (proj, LANES, off["k_pe"] // LANES),
                  (gq,), (gkv,), (cos, LANES, 0), (sin_a, LANES, 0), (sin_b, LANES, 0)],
                 [(ql, BF16), (kl, BF16), (LANES, BF16)])


def _mla_prep_bwd(proj, lay, d_cqn, d_ckvn, gq, gkv, d_proj, tm):
    s = proj.shape[0]
    ql, kl = lay["q_lora"], lay["kv_lora"]

    def body(dq_ref, dkv_ref, cq_ref, ckv_ref, gq_ref, gkv_ref, dc_ref, dgq_ref, dgkv_ref):
        _zero_first(dgq_ref, dgkv_ref)
        for dref, src, gref, cols, dg in ((dq_ref, cq_ref, gq_ref, slice(0, ql), dgq_ref),
                                          (dkv_ref, ckv_ref, gkv_ref, slice(ql, ql + kl), dgkv_ref)):
            v, dy = src[...], dref[...]
            r = lax.rsqrt(jnp.mean(v * v, axis=-1, keepdims=True) + EPS)
            xh = v * r
            dxh = dy * gref[...]
            dc_ref[:, cols] = (r * (dxh - xh * jnp.mean(dxh * xh, axis=-1, keepdims=True))).astype(BF16)
            dg[...] += jnp.sum(dy * xh, axis=0, keepdims=True)

    off = lay["off"]
    return _rows("mla_prep_bwd", body, s, tm,
                 [(d_cqn, ql, 0), (d_ckvn, kl, 0), (proj, ql, off["c_q"] // ql), (proj, kl, off["c_kv"] // kl),
                  (gq,), (gkv,)],
                 [], [(1, ql), (1, kl)], into=(d_proj, ql + kl, off["c_q"] // (ql + kl)))


def _q_operand(r, cos, sin_a, sin_b):
    qs = (NOPE + ROPE) ** -0.5 * math.log2(math.e)
    cs, sa, sb = cos * qs, sin_a * qs, sin_b * qs
    parts = []
    for lo in range(0, r.shape[1], QPAD):
        parts += [r[:, lo:lo + NOPE] * qs, _rope32(r[:, lo + NOPE:lo + QPAD], cs, sa, sb)]
    return (jnp.concatenate(parts, axis=1),)


def _k_operand(r, kpr):
    parts = []
    for lo in range(0, r.shape[1], QPAD):
        parts += [r[:, lo:lo + NOPE], kpr.astype(F32)]
    return r, jnp.concatenate(parts, axis=1)


def _rope_key_grad(parts, lay, d_proj, tm):
    s, w = parts.shape

    def body(p_ref, dkpe_ref):
        dkpe_ref[:, :LANES] = sum(p_ref[:, lo:lo + LANES] for lo in range(0, w, LANES)).astype(BF16)
        dkpe_ref[:, LANES:] = jnp.zeros((tm, LANES), BF16)

    return _rows("rope_key_grad", body, s, tm, [(parts, w, 0)], [],
                 into=(d_proj, 2 * LANES, lay["off"]["k_pe"] // (2 * LANES)))[0]


def _diag_mask(t, keys_on_rows=False):
    row = lax.broadcasted_iota(jnp.int32, (t, t), 0)
    col = lax.broadcasted_iota(jnp.int32, (t, t), 1)
    key, query = (row, col) if keys_on_rows else (col, row)
    return jnp.right_shift(key, CHUNK_SHIFT) <= jnp.right_shift(query, CHUNK_SHIFT)


def _tile_pairs(nt, by_key):
    if by_key:
        pairs = [(i, j) for j in range(nt) for i in range(j, nt)]
    else:
        pairs = [(i, j) for i in range(nt) for j in range(i + 1)]
    return (jnp.asarray([p[0] for p in pairs], jnp.int32), jnp.asarray([p[1] for p in pairs], jnp.int32))


def _head_block(heads):
    return 4 if heads % 4 == 0 else 2 if heads % 2 == 0 else 1


def _attn_fwd(qf, kf, kv, lay, t, ride=None):
    s = qf.shape[0]
    heads = lay["mla_heads"]
    hb = _head_block(heads)
    nt = s // t
    qi, kj = _tile_pairs(nt, False)
    grid = (heads // hb, int(qi.shape[0]))
    ride_in_specs, ride_ins, ride_out_specs, ride_out_shape, ride_scratch = _ride_args(ride)

    def body(qi_ref, kj_ref, q_ref, k_ref, kv_ref, o_ref, lse_ref, m_s, l_s, acc):
        p = pl.program_id(1)
        i, j = qi_ref[p], kj_ref[p]

        @pl.when(j == 0)
        def _():
            m_s[...] = jnp.full_like(m_s, -jnp.inf)
            l_s[...] = jnp.zeros_like(l_s)
            acc[...] = jnp.zeros_like(acc)

        def step(diagonal):
            ones = jnp.ones((t, LANES), BF16)
            scores = [_dot(q_ref[:, hh * QPAD:(hh + 1) * QPAD], k_ref[:, hh * QPAD:(hh + 1) * QPAD], NT)
                      for hh in range(hb)]
            for hh in range(hb):
                sc = scores[hh]
                if diagonal:
                    sc = jnp.where(_diag_mask(t), sc, -jnp.inf)
                cols = [sc[:, c * LANES:(c + 1) * LANES] for c in range(t // LANES)]
                m_old = m_s[hh]
                m_new = jnp.maximum(m_old, jnp.max(functools.reduce(jnp.maximum, cols), axis=-1, keepdims=True))
                alpha = jnp.exp2(m_old - m_new)
                pr = jnp.concatenate([jnp.exp2(c - m_new).astype(BF16) for c in cols], axis=1)
                pv = _dot(pr, jnp.concatenate([kv_ref[:, hh * QPAD + NOPE:(hh + 1) * QPAD], ones], axis=1), NN)
                l_new = alpha * l_s[hh] + pv[:, VHEAD:]
                a_new = alpha * acc[hh] + pv[:, :VHEAD]
                if diagonal:
                    o_ref[:, hh * VHEAD:(hh + 1) * VHEAD] = a_new / l_new
                    lse_ref[hh] = jnp.transpose(m_new + jnp.log2(l_new))[:1]
                else:
                    m_s[hh], l_s[hh], acc[hh] = m_new, l_new, a_new

        pl.when(j < i)(functools.partial(step, False))
        pl.when(j == i)(functools.partial(step, True))

    res = pl.pallas_call(
        _with_ride(body, ride, grid, 2, 3, 2), name="attn_fwd",
        grid_spec=pltpu.PrefetchScalarGridSpec(
            num_scalar_prefetch=2, grid=grid,
            in_specs=[pl.BlockSpec((t, hb * QPAD), lambda h, p, qi, kj: (qi[p], h)),
                      pl.BlockSpec((t, hb * QPAD), lambda h, p, qi, kj: (kj[p], h)),
                      pl.BlockSpec((t, hb * QPAD), lambda h, p, qi, kj: (kj[p], h))] + ride_in_specs,
            out_specs=[pl.BlockSpec((t, hb * VHEAD), lambda h, p, qi, kj: (qi[p], h)),
                       pl.BlockSpec((hb, 1, t), lambda h, p, qi, kj: (h, 0, qi[p]))] + ride_out_specs,
            scratch_shapes=[pltpu.VMEM((hb, t, LANES), F32), pltpu.VMEM((hb, t, LANES), F32),
                            pltpu.VMEM((hb, t, VHEAD), F32)] + ride_scratch),
        out_shape=[jax.ShapeDtypeStruct((s, heads * VHEAD), F32),
                   jax.ShapeDtypeStruct((heads, 1, s), F32)] + ride_out_shape,
        compiler_params=_params(("arbitrary", "arbitrary") if ride else ("parallel", "arbitrary")),
    )(qi, kj, qf, kf, kv, *ride_ins)
    return (res[0], res[1], res[2:]) if ride else res


def _attn_delta(d_o, o, lay, tm):
    s = o.shape[0]
    heads = lay["mla_heads"]

    def body(do_ref, o_ref, dl_ref):
        for h in range(heads):
            sl = slice(h * VHEAD, (h + 1) * VHEAD)
            dl_ref[h] = jnp.sum(jnp.transpose(do_ref[:, sl] * o_ref[:, sl]), axis=0, keepdims=True)

    tile = pl.BlockSpec((tm, heads * VHEAD), lambda i: (i, 0))
    return pl.pallas_call(
        body, name="attn_delta", grid=(s // tm,), in_specs=[tile, tile],
        out_specs=pl.BlockSpec((heads, 1, tm), lambda i: (0, 0, i)),
        out_shape=jax.ShapeDtypeStruct((heads, 1, s), F32),
        compiler_params=_params(("parallel",)),
    )(d_o, o)


def _attn_bwd(qf, kf, kv, lse, delta, d_o, cos, sin_a, sin_b, lay, t, ride=None):
    s = qf.shape[0]
    heads = lay["mla_heads"]
    hb = _head_block(heads)
    nt = s // t
    scale = (NOPE + ROPE) ** -0.5
    qi, kj = _tile_pairs(nt, True)
    grid = (heads // hb, int(qi.shape[0]))
    ride_in_specs, ride_ins, ride_out_specs, ride_out_shape, ride_scratch = _ride_args(ride)

    def body(qi_ref, kj_ref, q_ref, k_ref, kv_ref, lse_ref, dl_ref, do_ref, cos_ref, sa_ref, sb_ref,
             dqp_ref, dkv_ref, dkpe_ref, dq_acc, dk_acc, dv_acc):
        p = pl.program_id(1)
        i, j = qi_ref[p], kj_ref[p]
        rows = pl.ds(pl.multiple_of(i * t, t), t)

        def unrope(v):
            return _rope32(v, cos_ref[...], -sa_ref[...], -sb_ref[...])

        @pl.when(p == 0)
        def _():
            dq_acc[...] = jnp.zeros_like(dq_acc)

        def step(diagonal):
            for hh in range(hb):
                lo = hh * QPAD
                q, k = q_ref[:, lo:lo + QPAD], k_ref[:, lo:lo + QPAD]
                do = do_ref[:, hh * VHEAD:(hh + 1) * VHEAD]
                pr = jnp.exp2(_dot(k, q, NT) - lse_ref[hh])
                if diagonal:
                    pr = jnp.where(_diag_mask(t, keys_on_rows=True), pr, 0.0)
                dv_part = _dot(pr, do, NN)
                ds = (pr * (_dot(kv_ref[:, lo + NOPE:lo + QPAD], do, NT) - dl_ref[hh])).astype(BF16)
                dk_part = _dot(ds, q, NN)
                dq = dq_acc[rows, lo:lo + QPAD] + _dot(ds, k, TN) * scale
                if diagonal:
                    dqp_ref[:, lo:lo + NOPE] = dq[:, :NOPE].astype(BF16)
                    dqp_ref[:, lo + NOPE:lo + QPAD] = unrope(dq[:, NOPE:]).astype(BF16)
                    dk_acc[hh], dv_acc[hh] = dk_part, dv_part
                else:
                    dq_acc[rows, lo:lo + QPAD] = dq
                    dk_acc[hh] += dk_part
                    dv_acc[hh] += dv_part

        pl.when(i > j)(functools.partial(step, False))
        pl.when(i == j)(functools.partial(step, True))

        @pl.when(i == nt - 1)
        def _():
            kpe = jnp.zeros((t, LANES), F32)
            for hh in range(hb):
                lo = hh * QPAD
                dk = dk_acc[hh] * math.log(2.0)
                dkv_ref[:, lo:lo + NOPE] = dk[:, :NOPE].astype(BF16)
                dkv_ref[:, lo + NOPE:lo + QPAD] = dv_acc[hh].astype(BF16)
                kpe = kpe + dk[:, NOPE:]
            dkpe_ref[...] = unrope(kpe)

    table = pl.BlockSpec((t, LANES), lambda h, p, qi, kj: (kj[p], 0))
    res = pl.pallas_call(
        _with_ride(body, ride, grid, 2, 9, 3), name="attn_bwd",
        grid_spec=pltpu.PrefetchScalarGridSpec(
            num_scalar_prefetch=2, grid=grid,
            in_specs=[pl.BlockSpec((t, hb * QPAD), lambda h, p, qi, kj: (qi[p], h)),
                      pl.BlockSpec((t, hb * QPAD), lambda h, p, qi, kj: (kj[p], h)),
                      pl.BlockSpec((t, hb * QPAD), lambda h, p, qi, kj: (kj[p], h)),
                      pl.BlockSpec((hb, 1, t), lambda h, p, qi, kj: (h, 0, qi[p])),
                      pl.BlockSpec((hb, 1, t), lambda h, p, qi, kj: (h, 0, qi[p])),
                      pl.BlockSpec((t, hb * VHEAD), lambda h, p, qi, kj: (qi[p], h)),
                      table, table, table] + ride_in_specs,
            out_specs=[pl.BlockSpec((t, hb * QPAD), lambda h, p, qi, kj: (kj[p], h)),
                       pl.BlockSpec((t, hb * QPAD), lambda h, p, qi, kj: (kj[p], h)),
                       pl.BlockSpec((t, LANES), lambda h, p, qi, kj: (kj[p], h))] + ride_out_specs,
            scratch_shapes=[pltpu.VMEM((s, hb * QPAD), F32), pltpu.VMEM((hb, t, QPAD), F32),
                            pltpu.VMEM((hb, t, VHEAD), F32)] + ride_scratch),
        out_shape=[jax.ShapeDtypeStruct((s, heads * QPAD), BF16),
                   jax.ShapeDtypeStruct((s, heads * QPAD), BF16),
                   jax.ShapeDtypeStruct((s, heads // hb * LANES), F32)] + ride_out_shape,
        compiler_params=_params(("arbitrary", "arbitrary") if ride else ("parallel", "arbitrary")),
    )(qi, kj, qf, kf, kv, lse, delta, d_o, cos, sin_a, sin_b, *ride_ins)
    return (res[0], res[1], res[2], res[3:]) if ride else res


ANY = pl.BlockSpec(memory_space=pl.ANY)


def _place():
    return lax.axis_index("x"), lax.axis_index("y"), lax.axis_index("c")


def _other_chips(x, y):
    return [(1 - x, y), (x, 1 - y), (1 - x, 1 - y)]


class _Exchange:
    def __init__(self, ins, out_shape, scratch, phases):
        self.ins, self.out_shape, self.scratch, self.phases = list(ins), list(out_shape), list(scratch), phases

    def counts(self):
        return len(self.ins), len(self.out_shape), len(self.scratch)

    def run(self, r_in, r_out, r_scratch, conds):
        for cond, phase in zip(conds, self.phases):
            if phase is not None and cond is not None:
                pl.when(cond)(functools.partial(phase, r_in, r_out, r_scratch))


def _steps(ids, sizes):
    lin, total = 0, 1
    for i, n in zip(ids, sizes):
        lin, total = lin * n + i, total * n
    return lin == 0, lin == total // 2, lin == total - 1


def _ride_args(ride):
    if ride is None:
        return [], [], [], [], []
    n_in, n_out, _ = ride.counts()
    return [ANY] * n_in, ride.ins, [ANY] * n_out, ride.out_shape, ride.scratch


def _with_ride(body, ride, grid, n_prefetch, n_in, n_out):
    if ride is None:
        return body
    r_in, r_out, r_sc = ride.counts()

    def hosted(*refs):
        cuts = (n_prefetch, n_in, r_in, n_out, r_out)
        parts, pos = [], 0
        for n in cuts:
            parts.append(refs[pos:pos + n])
            pos += n
        pre, ins, ride_in, outs, ride_out = parts
        scratch, ride_scratch = refs[pos:len(refs) - r_sc], refs[len(refs) - r_sc:]
        first, mid, last = _steps([pl.program_id(d) for d in range(len(grid))], grid)
        ride.run(ride_in, ride_out, ride_scratch, (first, mid, None))
        body(*pre, *ins, *outs, *scratch)
        ride.run(ride_in, ride_out, ride_scratch, (None, None, last))

    return hosted


def _exchange_alone(name, ex):
    n_in, n_out, _ = ex.counts()

    def body(*refs):
        for phase in ex.phases:
            if phase is not None:
                phase(refs[:n_in], refs[n_in:n_in + n_out], refs[n_in + n_out:])

    return pl.pallas_call(
        body, name=name, in_specs=[ANY] * n_in, out_specs=[ANY] * n_out, out_shape=ex.out_shape,
        scratch_shapes=ex.scratch)(*ex.ins)


def _gather_exchange(shards):
    nw = len(shards)

    def parts(ins, outs, sems):
        send_sems, recv_sems, local_sems = sems
        x, y, c = _place()

        def slot(px, py, pc):
            return 4 * px + 2 * py + pc

        def copy(w, k, rows, to, src=None):
            return pltpu.make_async_remote_copy(
                src_ref=rows if src is None else src, dst_ref=rows, send_sem=send_sems.at[w, k],
                recv_sem=recv_sems.at[w, k], device_id=to, device_id_type=MESH)

        def plan(w, mine):
            side = c if mine else 1 - c
            half = shards[w].shape[0] // 2
            whole = lambda px, py: outs[w].at[slot(px, py, side)]
            top = lambda px, py: outs[w].at[slot(px, py, side), pl.ds(0, half)]
            bottom = lambda px, py: outs[w].at[slot(px, py, side), pl.ds(half, half)]
            xn, yn, sib = (1 - x, y, side), (x, 1 - y, side), (x, y, 1 - side)
            own = ins[w] if mine else None
            return [copy(w, 0, whole(x, y), sib, own), copy(w, 1, whole(x, y), xn, own),
                    copy(w, 2, whole(x, y), yn, own), copy(w, 3, top(1 - x, y), yn), copy(w, 4, bottom(x, 1 - y), xn),
                    copy(w, 5, whole(1 - x, y), sib), copy(w, 6, whole(x, 1 - y), sib),
                    copy(w, 7, top(1 - x, 1 - y), sib), copy(w, 8, bottom(1 - x, 1 - y), sib)]

        def arrivals(w):
            half = shards[w].shape[0] // 2
            at = lambda px, py, *rows: outs[w].at[(slot(px, py, c),) + rows]
            return {1: copy(w, 1, at(1 - x, y), (x, y, c)), 2: copy(w, 2, at(x, 1 - y), (x, y, c)),
                    3: copy(w, 3, at(1 - x, 1 - y, pl.ds(0, half)), (x, y, c)),
                    4: copy(w, 4, at(1 - x, 1 - y, pl.ds(half, half)), (x, y, c))}

        local = [pltpu.make_async_copy(ins[w], outs[w].at[slot(x, y, c)], local_sems.at[w]) for w in range(nw)]
        return plan, arrivals, local

    def start(ins, outs, sems):
        plan, _, local = parts(ins, outs, sems)
        for cp in local:
            cp.start()
        for w in range(nw):
            for k in (0, 1, 2):
                plan(w, True)[k].start()

    def middle(ins, outs, sems):
        plan, arrivals, _ = parts(ins, outs, sems)
        for landed, onward in ((1, (3, 5)), (2, (4, 6))):
            for w in range(nw):
                arrivals(w)[landed].wait_recv()
                for k in onward:
                    plan(w, True)[k].start()

    def finish(ins, outs, sems):
        plan, arrivals, local = parts(ins, outs, sems)
        for landed, onward in ((3, 7), (4, 8)):
            for w in range(nw):
                arrivals(w)[landed].wait_recv()
                plan(w, True)[onward].start()
        for w in range(nw):
            from_sibling = plan(w, False)
            for k in (0, 5, 6, 7, 8):
                from_sibling[k].wait_recv()
            for cp in plan(w, True):
                cp.wait_send()
        for cp in local:
            cp.wait()

    return _Exchange(
        shards, [jax.ShapeDtypeStruct((N_DEV,) + s.shape, s.dtype) for s in shards],
        [pltpu.SemaphoreType.DMA((nw, 9)), pltpu.SemaphoreType.DMA((nw, 9)), pltpu.SemaphoreType.DMA((nw,))],
        (start, middle, finish))


def _sibling_exchange(grads):
    nw = len(grads)

    def copies(ins, outs, sems):
        x, y, c = _place()
        return [pltpu.make_async_remote_copy(
            src_ref=ins[w].at[2 * p + (1 - c)], dst_ref=outs[w].at[p], send_sem=sems[0].at[w, p],
            recv_sem=sems[1].at[w, p], device_id=(x, y, 1 - c), device_id_type=MESH)
            for w in range(nw) for p in range(4)]

    def start(ins, outs, sems):
        for cp in copies(ins, outs, sems):
            cp.start()

    def finish(ins, outs, sems):
        for cp in copies(ins, outs, sems):
            cp.wait()

    return _Exchange(grads, [jax.ShapeDtypeStruct((4,) + g.shape[1:], g.dtype) for g in grads],
                     [pltpu.SemaphoreType.DMA((nw, 4)), pltpu.SemaphoreType.DMA((nw, 4))], (start, None, finish))


def _chips_exchange(sums):
    nw = len(sums)

    def copies(ins, outs, sems):
        x, y, c = _place()
        return [pltpu.make_async_remote_copy(
            src_ref=ins[w].at[2 * px + py], dst_ref=outs[w].at[k], send_sem=sems[0].at[w, k],
            recv_sem=sems[1].at[w, k], device_id=(px, py, c), device_id_type=MESH)
            for w in range(nw) for k, (px, py) in enumerate(_other_chips(x, y))]

    def start(ins, outs, sems):
        for cp in copies(ins, outs, sems):
            cp.start()

    def finish(ins, outs, sems):
        for cp in copies(ins, outs, sems):
            cp.wait()

    return _Exchange(sums, [jax.ShapeDtypeStruct((3,) + g.shape[1:], g.dtype) for g in sums],
                     [pltpu.SemaphoreType.DMA((nw, 3)), pltpu.SemaphoreType.DMA((nw, 3))], (start, None, finish))


def _pair_sum(name, g, got, c_arr, tr):
    _, rows, cols = g.shape
    tr = _tile_rows(rows, tr)

    def body(c_ref, a_ref, b_ref, o_ref):
        o_ref[...] = (a_ref[...].astype(F32) + b_ref[...].astype(F32)).astype(BF16)

    return pl.pallas_call(
        body, name=name,
        grid_spec=pltpu.PrefetchScalarGridSpec(
            num_scalar_prefetch=1, grid=(4, rows // tr),
            in_specs=[pl.BlockSpec((None, tr, cols), lambda p, r, cr: (2 * p + cr[0], r, 0)),
                      pl.BlockSpec((None, tr, cols), lambda p, r, cr: (p, r, 0))],
            out_specs=pl.BlockSpec((None, tr, cols), lambda p, r, cr: (p, r, 0))),
        out_shape=jax.ShapeDtypeStruct((4, rows, cols), BF16),
        compiler_params=_params(("parallel", "parallel")),
    )(c_arr, g, got)


def _tile_rows(rows, pref):
    t = min(rows, pref)
    while rows % t or t % 8:
        t -= 1
    return t


def _adam(w, g, m, v):
    m = ADAM_B1 * m + (1.0 - ADAM_B1) * g
    v = ADAM_B2 * v + (1.0 - ADAM_B2) * (g * g)
    m_hat = m / (1.0 - ADAM_B1 ** ADAM_STEP)
    v_hat = v / (1.0 - ADAM_B2 ** ADAM_STEP)
    return -ADAM_LR * (m_hat / (jnp.sqrt(v_hat) + ADAM_EPS) + ADAM_WD * w), m, v


def _adamw_shard(name, w, m, v, sums, got, chip_arr, tr):
    _, rows, cols = w.shape
    tr = _tile_rows(rows, tr)

    def body(p_ref, w_ref, m_ref, v_ref, s_ref, r_ref, g_ref, d_ref, nm_ref, nv_ref):
        g = s_ref[...].astype(F32)
        for k in range(3):
            g = g + r_ref[k].astype(F32)
        g_ref[...] = g
        d_ref[...], nm_ref[...], nv_ref[...] = _adam(w_ref[...], g, m_ref[...], v_ref[...])

    tile = pl.BlockSpec((None, tr, cols), lambda r, pr: (0, r, 0))
    return pl.pallas_call(
        body, name=name,
        grid_spec=pltpu.PrefetchScalarGridSpec(
            num_scalar_prefetch=1, grid=(rows // tr,),
            in_specs=[tile, tile, tile,
                      pl.BlockSpec((None, tr, cols), lambda r, pr: (pr[0], r, 0)),
                      pl.BlockSpec((3, tr, cols), lambda r, pr: (0, r, 0))],
            out_specs=[tile] * 4),
        out_shape=[jax.ShapeDtypeStruct((1, rows, cols), F32)] * 4,
        compiler_params=_params(("parallel",)),
    )(chip_arr, w, m, v, sums, got)


def _small_all_reduce_adam(part, w, m, v):
    rows = part.shape[0]

    def body(p_ref, w_ref, m_ref, v_ref, g_ref, d_ref, nm_ref, nv_ref, buf, send_sems, recv_sems):
        x, y, c = _place()
        me = 4 * x + 2 * y + c
        buf[me] = p_ref[...]
        peers = [(x, y, 1 - c)] + [(px, py, pc) for px, py in _other_chips(x, y) for pc in (c, 1 - c)]
        copies = []
        for k, peer in enumerate(peers):
            cp = pltpu.make_async_remote_copy(
                src_ref=buf.at[me], dst_ref=buf.at[me], send_sem=send_sems.at[k], recv_sem=recv_sems.at[k],
                device_id=peer, device_id_type=MESH)
            cp.start()
            copies.append(cp)
        for cp in copies:
            cp.wait()
        g = buf[0]
        for k in range(1, N_DEV):
            g = g + buf[k]
        g_ref[...] = g
        d_ref[...], nm_ref[...], nv_ref[...] = _adam(w_ref[...], g, m_ref[...], v_ref[...])

    vm = pl.BlockSpec(memory_space=pltpu.VMEM)
    return pl.pallas_call(
        body, name="gains_all_reduce_adamw",
        in_specs=[vm] * 4, out_specs=[vm] * 4,
        out_shape=[jax.ShapeDtypeStruct((rows, LANES), F32)] * 4,
        scratch_shapes=[pltpu.VMEM((N_DEV, rows, LANES), F32), pltpu.SemaphoreType.DMA((7,)),
                        pltpu.SemaphoreType.DMA((7,))],
        compiler_params=pltpu.CompilerParams(has_side_effects=True),
    )(part, w, m, v)


IN_ORDER = ("r_q", "r_k", "r_v", "r_g", "c_q", "c_kv", "k_pe", "g_ret", "g_mla")
RET_HEAD = 2 * RET_QK + RET_V


def _make_layout(d, vw, qw, ql, kl, mla_w):
    width = {"r_q": qw, "r_k": qw, "r_v": vw, "r_g": vw, "c_q": ql, "c_kv": kl, "k_pe": ROPE, "g_ret": d, "g_mla": d}
    src, o = {}, 0
    for n in IN_ORDER:
        src[n] = o
        o += width[n]
    heads = vw // RET_V
    off, pieces, o = {}, [], 0

    def put(name, w, s):
        nonlocal o
        off.setdefault(name, o)
        pieces.append((o, w, s))
        o += w

    for n in ("g_ret", "g_mla", "r_g"):
        put(n, width[n], src[n])
    for h in range(heads):
        put("heads", RET_QK, src["r_q"] + h * RET_QK)
        put("heads", RET_QK, src["r_k"] + h * RET_QK)
        put("heads", RET_V, src["r_v"] + h * RET_V)
    for n in ("c_q", "c_kv", "k_pe"):
        put(n, width[n], src[n])
    total = off["k_pe"] + 2 * LANES
    for n, blk in (("g_ret", d), ("g_mla", d), ("r_g", vw), ("heads", RET_HEAD), ("c_q", ql + kl), ("k_pe", 2 * LANES)):
        assert off[n] % blk == 0
    assert ql == kl and off["c_kv"] == off["c_q"] + ql
    return {"off": off, "pieces": pieces, "total": total, "n_in": sum(width.values()),
            "ret_heads": heads, "mla_heads": mla_w // VHEAD, "q_lora": ql, "kv_lora": kl}


def _cols_to_full(g):
    n, r, c = g.shape
    return jnp.transpose(g, (1, 0, 2)).reshape(r, n * c)


def _full_to_cols(w):
    r, c = w.shape
    return jnp.transpose(w.reshape(r, N_DEV, c // N_DEV), (1, 0, 2))


def _w_in_to_mine(g, lay):
    _, rows, cols = g.shape
    parts, at = [], 0
    for o, w, s in lay["pieces"]:
        if o > at:
            parts.append(jnp.zeros((rows, o - at), g.dtype))
        while w > 0:
            k, a = divmod(s, cols)
            take = min(w, cols - a)
            parts.append(g[k, :, a:a + take])
            s, w, o = s + take, w - take, o + take
        at = o
    parts.append(jnp.zeros((rows, lay["total"] - at), g.dtype))
    return jnp.concatenate(parts, axis=1)


def _mine_to_blocks(g, lay):
    cols = lay["n_in"] // N_DEV
    by_src = sorted(lay["pieces"], key=lambda p: p[2])
    blocks = []
    for k in range(N_DEV):
        lo, hi, parts = k * cols, (k + 1) * cols, []
        for o, w, s in by_src:
            a, b = max(lo, s), min(hi, s + w)
            if a < b:
                parts.append(g[:, o + a - s:o + b - s])
        blocks.append(jnp.concatenate(parts, axis=1))
    return jnp.stack(blocks)


def _rope_tables(positions, half):
    inv = ROPE_THETA ** (-jnp.arange(half, dtype=F32) / half)
    ang = positions.astype(F32)[:, None] * inv
    return jnp.cos(ang), jnp.sin(ang)


def _pack_rows(vs):
    return jnp.concatenate([v.reshape(-1, LANES) for v in vs], axis=0)


def kernel(x, positions, norm_mix_g, w_in, ret_norm_g, w_ret_o, q_a_norm_g, w_q_b, kv_a_norm_g, w_kv_b, w_mla_o, w_out, norm_mlp_g, w_up, w_down, norm_f_g, loss_target, m_norm_mix_g, m_w_in, m_ret_norm_g, m_w_ret_o, m_q_a_norm_g, m_w_q_b, m_kv_a_norm_g, m_w_kv_b, m_w_mla_o, m_w_out, m_norm_mlp_g, m_w_up, m_w_down, m_norm_f_g, v_norm_mix_g, v_w_in, v_ret_norm_g, v_w_ret_o, v_q_a_norm_g, v_w_q_b, v_kv_a_norm_g, v_w_kv_b, v_w_mla_o, v_w_out, v_norm_mlp_g, v_w_up, v_w_down, v_norm_f_g):
    xs, tgt, pos = x[0], loss_target[0], positions[0]
    s, d = xs.shape
    mats = {"w_in": w_in[0], "w_ret_o": w_ret_o[0], "w_q_b": w_q_b[0], "w_kv_b": w_kv_b[0], "w_mla_o": w_mla_o[0],
            "w_out": w_out[0], "w_up": w_up[0], "w_down": w_down[0]}
    mat_w = {"w_in": w_in, "w_ret_o": w_ret_o, "w_q_b": w_q_b, "w_kv_b": w_kv_b, "w_mla_o": w_mla_o, "w_out": w_out,
             "w_up": w_up, "w_down": w_down}
    mat_m = {"w_in": m_w_in, "w_ret_o": m_w_ret_o, "w_q_b": m_w_q_b, "w_kv_b": m_w_kv_b, "w_mla_o": m_w_mla_o,
             "w_out": m_w_out, "w_up": m_w_up, "w_down": m_w_down}
    mat_v = {"w_in": v_w_in, "w_ret_o": v_w_ret_o, "w_q_b": v_w_q_b, "w_kv_b": v_w_kv_b, "w_mla_o": v_w_mla_o,
             "w_out": v_w_out, "w_up": v_w_up, "w_down": v_w_down}
    names = list(mats)
    col_sharded = ("w_in", "w_q_b", "w_kv_b", "w_up")
    vw = ret_norm_g.shape[1]
    mla_w = mats["w_mla_o"].shape[0] * N_DEV
    ql, kl = q_a_norm_g.shape[1], kv_a_norm_g.shape[1]
    n_in = mats["w_in"].shape[1] * N_DEV
    qw = (n_in - 2 * vw - ql - kl - ROPE - 2 * d) // 2
    lay = _make_layout(d, vw, qw, ql, kl, mla_w)
    assert lay["n_in"] == n_in
    heads_r, heads_m = lay["ret_heads"], lay["mla_heads"]

    shard16 = {n: mats[n].astype(BF16) for n in names}
    with_in_proj = ("w_ret_o", "w_q_b", "w_kv_b", "w_mla_o", "w_out")
    mlp = ("w_up", "w_down")
    by_device = ("w_up", "w_kv_b")
    full = {}

    def keep(group, gathered):
        for n, g in zip(group, gathered):
            if n not in by_device:
                g = _cols_to_full(g) if n in col_sharded else g.reshape(-1, g.shape[2])
            full[n] = g

    w_mine = _w_in_to_mine(_exchange_alone("gather_w_in", _gather_exchange([shard16["w_in"]]))[0], lay)

    c64, s64 = _rope_tables(pos, RET_QK // 2)
    cos_r = jnp.concatenate([c64, c64], axis=1)
    sin_r = jnp.concatenate([-s64, s64], axis=1)
    c32, s32 = _rope_tables(pos, ROPE // 2)
    z32, z64 = jnp.zeros_like(c32), jnp.zeros((s, LANES - ROPE), F32)
    cos_p = jnp.concatenate([c32, c32, z64], axis=1)
    sin_a = jnp.concatenate([-s32, z32, z64], axis=1)
    sin_b = jnp.concatenate([z32, s32, z64], axis=1)
    lg = jnp.log(1.0 - 2.0 ** (-5.0 - jnp.arange(heads_r, dtype=F32)))
    lgs = jnp.broadcast_to(lg[:, None, None], (heads_r, 8, LANES))

    tm = min(256, s)
    blk = min(512, s)
    t_att = min(512, s)

    u = _rms_fwd("norm_mix", xs, norm_mix_g, tm)
    proj, gathered = _mm("in_proj", u, w_mine, "nn", F32,
                         ride=_gather_exchange([shard16[n] for n in with_in_proj]))
    keep(with_in_proj, gathered)
    wq_pad = jnp.pad(full["w_q_b"].reshape(ql, heads_m, NOPE + ROPE),
                     ((0, 0), (0, 0), (0, QPAD - NOPE - ROPE))).reshape(ql, heads_m * QPAD)
    o_ret, states = _ret_fwd(proj, lay, cos_r, sin_r, lgs, blk)
    ry = _ret_post(proj, lay, o_ret, ret_norm_g, tm)
    y_ret = _mm("ret_out", ry, full["w_ret_o"], "nn", F32)
    cqn, ckvn, kpr = _mla_prep(proj, lay, q_a_norm_g, kv_a_norm_g, cos_p, sin_a, sin_b, tm)
    qf = _mm("q_up", cqn, wq_pad, "nn", BF16, extras=((cos_p, None), (sin_a, None), (sin_b, None)), epilogue=_q_operand)
    kv, kf = _mm("kv_up", ckvn, full["w_kv_b"], "nn", (BF16, BF16), b_by_device=True, extras=((kpr, None),),
                 epilogue=_k_operand)
    o_mla, lse, gathered = _attn_fwd(qf, kf, kv, lay, t_att, ride=_gather_exchange([shard16["w_up"]]))
    keep(("w_up",), gathered)
    gate_tile = _tile(d, 1024)
    y_mla, merged = _mm(
        "mla_out", o_mla, full["w_mla_o"], "nn", (F32, BF16), tm=512, tn=gate_tile,
        extras=((proj, lay["off"]["g_ret"] // gate_tile), (proj, lay["off"]["g_mla"] // gate_tile), y_ret),
        epilogue=lambda r, gr, gm, yr: (r, _sig(gr) * yr + _sig(gm) * r))
    h1, n2 = _mm("out_proj", merged, full["w_out"], "nn", (F32, BF16), tm=512, tn=d,
                 extras=(xs, (norm_mlp_g, "whole")), epilogue=_residual_norm)
    (z, act), gathered = _mm("mlp_up", n2, full["w_up"], "nn", (F32, BF16), b_by_device=True,
                             epilogue=lambda r: (r, jnp.square(jnp.maximum(r, 0.0))),
                             ride=_gather_exchange([shard16["w_down"]]))
    keep(("w_down",), gathered)
    dn = _mm("mlp_down", act, full["w_down"], "nn", F32)
    dh2, g_norm_f, loss_part = _final("loss_head", h1, dn, norm_f_g.reshape(1, d), tgt, tm)

    mx, my, mc = _place()
    c_arr = jnp.reshape(mc, (1,)).astype(jnp.int32)
    chip_arr = jnp.reshape(2 * mx + my, (1,)).astype(jnp.int32)
    sums, from_chips = {}, {}

    def blocks(group, grads):
        return [g if n in by_device else (_full_to_cols(g) if n in col_sharded else g.reshape((N_DEV,) + mats[n].shape))
                for n, g in zip(group, grads)]

    def pair_sums(group, mine, from_sibling):
        for n, g, r in zip(group, mine, from_sibling):
            sums[n] = _pair_sum("pair_sum_" + n, g, r, c_arr, 256)
        return [sums[n] for n in group]

    dz = _mm("mlp_down_dx", dh2, full["w_down"], "nt", BF16, extras=(z,),
             epilogue=lambda r, zz: (r * (2.0 * jnp.maximum(zz, 0.0)),))
    g_w_down = _mm("mlp_down_dw", act, dh2, "tn", BF16)
    down_blocks = blocks(("w_down",), (g_w_down,))
    g_w_up, got_down = _mm("mlp_up_dw", n2, dz, "tn", BF16, out_by_device=True, ride=_sibling_exchange(down_blocks))
    dn2, got_up = _mm("mlp_up_dx", dz, full["w_up"], "nt", F32, b_by_device=True, ride=_sibling_exchange([g_w_up]))
    mlp_sums = pair_sums(mlp, [g_w_up] + down_blocks, list(got_up) + list(got_down))
    dh1, g_norm_mlp = _rms_bwd("norm_mlp_bwd", dn2, h1, norm_mlp_g, dh2, tm)
    assert lay["off"]["g_ret"] == 0 and lay["off"]["g_mla"] == d
    dy_ret, dy_mla, d_proj = _mm(
        "out_proj_dx", dh1, full["w_out"], "nt", (BF16, BF16, (BF16, 2, lay["total"])), tm=256, tn=d,
        extras=((proj, 0), (proj, 1), y_ret, y_mla), epilogue=_gate_grads)
    g_w_out = _mm("out_proj_dw", merged, dh1, "tn", BF16)
    g_w_ret_o = _mm("ret_out_dw", ry, dy_ret, "tn", BF16)
    g_w_mla_o = _mm("mla_out_dw", o_mla, dy_mla, "tn", BF16)
    mixer = ("w_out", "w_ret_o", "w_mla_o")
    mixer_blocks = blocks(mixer, (g_w_out, g_w_ret_o, g_w_mla_o))
    d_ry, got = _mm("ret_out_dx", dy_ret, full["w_ret_o"], "nt", F32, ride=_sibling_exchange(mixer_blocks))
    mixer_sums = pair_sums(mixer, mixer_blocks, got)
    d_omla = _mm("mla_out_dx", dy_mla, full["w_mla_o"], "nt", F32)
    d_oret, d_proj, g_ret_norm = _ret_post_bwd(proj, lay, d_ry, o_ret, ret_norm_g, d_proj, tm)
    d_proj, got = _ret_bwd(proj, lay, cos_r, sin_r, lgs, states, d_oret, d_proj, blk,
                           ride=_chips_exchange(mixer_sums))
    from_chips.update(zip(mixer, got))
    delta = _attn_delta(d_omla, o_mla, lay, t_att)
    dqp, dkv, dkpe_parts, got = _attn_bwd(qf, kf, kv, lse, delta, d_omla, cos_p, sin_a, sin_b, lay, t_att,
                                          ride=_chips_exchange(mlp_sums))
    from_chips.update(zip(mlp, got))
    d_proj = _rope_key_grad(dkpe_parts, lay, d_proj, tm)
    d_cqn = _mm("q_up_dx", dqp, wq_pad, "nt", F32)
    g_wq_pad = _mm("q_up_dw", cqn, dqp, "tn", BF16)
    d_ckvn = _mm("kv_up_dx", dkv, full["w_kv_b"], "nt", F32, b_by_device=True)
    g_w_kv_b = _mm("kv_up_dw", ckvn, dkv, "tn", BF16, out_by_device=True)
    d_proj, g_q_a, g_kv_a = _mla_prep_bwd(proj, lay, d_cqn, d_ckvn, q_a_norm_g, kv_a_norm_g, d_proj, tm)
    g_w_mine = _mm("in_proj_dw", u, d_proj, "tn", BF16)
    g_w_q_b = g_wq_pad.reshape(ql, heads_m, QPAD)[:, :, :NOPE + ROPE].reshape(ql, heads_m * (NOPE + ROPE))
    last = ("w_in", "w_q_b", "w_kv_b")
    last_blocks = [_mine_to_blocks(g_w_mine, lay)] + blocks(last[1:], (g_w_q_b, g_w_kv_b))
    last_sums = pair_sums(last, last_blocks, _exchange_alone("grads_to_sibling", _sibling_exchange(last_blocks)))
    du, got = _mm("in_proj_dx", d_proj, w_mine, "nt", F32, ride=_chips_exchange(last_sums))
    from_chips.update(zip(last, got))
    grad_x, g_norm_mix = _rms_bwd("norm_mix_bwd", du, xs, norm_mix_g, dh1, tm)

    upd = {n: _adamw_shard("adamw_" + n, mat_w[n], mat_m[n], mat_v[n], sums[n], from_chips[n], chip_arr, 256)
           for n in names}

    gains = [("norm_mix_g", norm_mix_g, m_norm_mix_g, v_norm_mix_g, g_norm_mix),
             ("ret_norm_g", ret_norm_g, m_ret_norm_g, v_ret_norm_g, g_ret_norm),
             ("q_a_norm_g", q_a_norm_g, m_q_a_norm_g, v_q_a_norm_g, g_q_a),
             ("kv_a_norm_g", kv_a_norm_g, m_kv_a_norm_g, v_kv_a_norm_g, g_kv_a),
             ("norm_mlp_g", norm_mlp_g, m_norm_mlp_g, v_norm_mlp_g, g_norm_mlp),
             ("norm_f_g", norm_f_g, m_norm_f_g, v_norm_f_g, g_norm_f)]
    n_rows = sum(g[1].size for g in gains) // LANES
    pad_rows = -(-(n_rows + 1) // 8) * 8 - n_rows
    tail = jnp.zeros((pad_rows, LANES), F32)
    part = jnp.concatenate([_pack_rows([g[4] for g in gains]),
                            jnp.broadcast_to(loss_part[:, :1], (1, LANES)), tail[1:]], axis=0)
    packed = [jnp.concatenate([_pack_rows([g[k] for g in gains]), tail], axis=0) for k in (1, 2, 3)]
    g_sm, d_sm, m_sm, v_sm = _small_all_reduce_adam(part, *packed)
    loss = g_sm[n_rows, 0]
    small = {}
    o = 0
    for name, w, _, _, _ in gains:
        r = w.size // LANES
        small[name] = [a[o:o + r].reshape(w.shape) for a in (g_sm, d_sm, m_sm, v_sm)]
        o += r

    order = ["norm_mix_g", "w_in", "ret_norm_g", "w_ret_o", "q_a_norm_g", "w_q_b", "kv_a_norm_g", "w_kv_b", "w_mla_o",
             "w_out", "norm_mlp_g", "w_up", "w_down", "norm_f_g"]
    outs = [loss, grad_x[None]]
    for k in range(4):
        for n in order:
            outs.append(small[n][k] if n in small else upd[n][k])
    return tuple(outs)
```

```python
import functools
import math

import jax
import jax.numpy as jnp
from jax import lax
from jax.experimental import pallas as pl
from jax.experimental.pallas import tpu as pltpu

F32 = jnp.float32
BF16 = jnp.bfloat16
MESH = pl.DeviceIdType.MESH

EPS = 1e-6
ROPE_THETA = 10000.0
CHUNK_SHIFT = 6
RET_QK = 128
RET_V = 256
NOPE = 128
ROPE = 64
VHEAD = 128
QPAD = 256
LANES = 128
N_DEV = 8
VMEM_LIMIT = 56 * 1024 * 1024

ADAM_LR = 0.001
ADAM_B1 = 0.9
ADAM_B2 = 0.999
ADAM_EPS = 1e-08
ADAM_WD = 0.01
ADAM_STEP = 10

NN = (((1,), (0,)), ((), ()))
NT = (((1,), (1,)), ((), ()))
TN = (((0,), (0,)), ((), ()))


def _dot(a, b, dims):
    return lax.dot_general(a.astype(BF16), b.astype(BF16), dims, preferred_element_type=F32)


def _tile(dim, pref):
    if dim <= pref:
        return dim
    t = (pref // LANES) * LANES
    while t >= LANES:
        if dim % t == 0:
            return t
        t -= LANES
    raise ValueError(f"no tile for {dim}")


def _params(sem):
    return pltpu.CompilerParams(dimension_semantics=sem, vmem_limit_bytes=VMEM_LIMIT)


def _sig(v):
    return 1.0 / (1.0 + jnp.exp(-v))


def _mm(name, a, b, mode, out_dtypes, *, tm=1024, tn=1024, tk=2048, extras=(), epilogue=None, ride=None,
        b_by_device=False, out_by_device=False):
    if b_by_device:
        b_cols = b.shape[2]
        b_shape = (b.shape[1], N_DEV * b_cols)
    else:
        b_shape = b.shape
    if mode == "nn":
        (m, k), (_, n) = a.shape, b_shape
    elif mode == "nt":
        (m, k), (n, _) = a.shape, b_shape
    else:
        (k, m), (_, n) = a.shape, b_shape
    tm, tn, tk = _tile(m, tm), _tile(n, tn), _tile(k, tk)
    if b_by_device and mode != "nt":
        tn = _tile(b_cols, tn)
    if out_by_device:
        tn = _tile(n // N_DEV, tn)
    nk = k // tk
    dims = {"nn": NN, "nt": NT, "tn": TN}[mode]
    a_spec = (pl.BlockSpec((tk, tm), lambda i, j, kk: (kk, i)) if mode == "tn"
              else pl.BlockSpec((tm, tk), lambda i, j, kk: (i, kk)))
    if b_by_device and mode == "nt":
        piece = min(tk, b_cols)
        n_b, per = tk // piece, b_cols // piece
        b_specs = [pl.BlockSpec((None, tn, piece),
                                lambda i, j, kk, p=p: ((kk * n_b + p) // per, j, (kk * n_b + p) % per))
                   for p in range(n_b)]
    elif b_by_device:
        per = b_cols // tn
        n_b, piece = 1, tk
        b_specs = [pl.BlockSpec((None, tk, tn), lambda i, j, kk: (j // per, kk, j % per))]
    else:
        n_b, piece = 1, tk
        b_specs = [pl.BlockSpec((tn, tk), lambda i, j, kk: (j, kk)) if mode == "nt"
                   else pl.BlockSpec((tk, tn), lambda i, j, kk: (kk, j))]
    tile_spec = pl.BlockSpec((tm, tn), lambda i, j, kk: (i, j))
    if out_by_device:
        per_out = n // N_DEV // tn
        out_spec = pl.BlockSpec((None, tm, tn), lambda i, j, kk: (j // per_out, i, j % per_out))
        out_dims = (N_DEV, m, n // N_DEV)
    else:
        out_spec, out_dims = tile_spec, (m, n)
    ex_arrays, ex_specs = [], []
    for e in extras:
        arr, off = e if isinstance(e, tuple) else (e, 0)
        ex_arrays.append(arr)
        if off == "whole":
            ex_specs.append(pl.BlockSpec(arr.shape, lambda i, j, kk, nd=arr.ndim: (0,) * nd))
        elif off is None:
            ex_specs.append(pl.BlockSpec((tm, arr.shape[1]), lambda i, j, kk: (i, 0)))
        else:
            ex_specs.append(pl.BlockSpec((tm, tn), lambda i, j, kk, off=off: (i, j + off)))
    n_ex = len(extras)
    single = not isinstance(out_dtypes, (tuple, list))
    dts = (out_dtypes,) if single else tuple(out_dtypes)
    out_specs, out_shapes = [], []
    for dt in dts:
        if isinstance(dt, tuple):
            dt, mult, width = dt
            out_specs.append(pl.BlockSpec((tm, mult * tn), lambda i, j, kk: (i, j)))
            out_shapes.append(jax.ShapeDtypeStruct((m, width), dt))
        else:
            out_specs.append(out_spec)
            out_shapes.append(jax.ShapeDtypeStruct(out_dims, dt))

    grid = (m // tm, n // tn, nk)
    r_in, r_out, r_sc = ride.counts() if ride else (0, 0, 0)
    n_acc = 1 if nk > 1 else 0

    def body(a_ref, *rest):
        b_refs, rest = rest[:n_b], rest[n_b:]
        ex, rest = rest[:n_ex], rest[n_ex:]
        ride_in, rest = rest[:r_in], rest[r_in:]
        outs, rest = rest[:len(dts)], rest[len(dts):]
        ride_out, rest = rest[:r_out], rest[r_out:]
        ride_scratch = rest[n_acc:]
        if ride:
            first, mid, last = _steps([pl.program_id(d) for d in range(3)], grid)
            ride.run(ride_in, ride_out, ride_scratch, (first, mid, None))

        def finish(r):
            vals = (r,) if epilogue is None else epilogue(r, *[e[...] for e in ex])
            for o, v in zip(outs, vals):
                o[...] = v.astype(o.dtype)

        if n_b == 1:
            part = _dot(a_ref[...], b_refs[0][...], dims)
        else:
            part = sum(_dot(a_ref[:, p * piece:(p + 1) * piece], b_refs[p][...], dims) for p in range(n_b))
        if nk == 1:
            finish(part)
        else:
            acc = rest[0]
            kk = pl.program_id(2)

            @pl.when(kk == 0)
            def _():
                acc[...] = part

            @pl.when(jnp.logical_and(kk > 0, kk < nk - 1))
            def _():
                acc[...] += part

            @pl.when(kk == nk - 1)
            def _():
                finish(acc[...] + part)

        if ride:
            ride.run(ride_in, ride_out, ride_scratch, (None, None, last))

    res = pl.pallas_call(
        body, name=name, grid=grid,
        in_specs=[a_spec] + b_specs + ex_specs + [ANY] * r_in,
        out_specs=out_specs + [ANY] * r_out,
        out_shape=out_shapes + (ride.out_shape if ride else []),
        scratch_shapes=([pltpu.VMEM((tm, tn), F32)] if nk > 1 else []) + (ride.scratch if ride else []),
        compiler_params=_params(("arbitrary",) * 3 if ride else ("parallel", "parallel", "arbitrary")),
    )(a, *[b] * n_b, *ex_arrays, *(ride.ins if ride else []))
    own = res[0] if single else res[:len(dts)]
    return (own, res[len(dts):]) if ride else own


def _rows(name, body, n_rows, tm, ins, outs, accs=(), into=None):
    in_specs, args = [], []
    for t in ins:
        if len(t) == 1:
            in_specs.append(pl.BlockSpec(t[0].shape, lambda i, nd=t[0].ndim: (0,) * nd))
        else:
            in_specs.append(pl.BlockSpec((tm, t[1]), lambda i, cb=t[2]: (i, cb)))
        args.append(t[0])
    outs = [(o + (o[0], 0))[:4] for o in outs]
    out_specs = [pl.BlockSpec((tm, w), lambda i, cb=cb: (i, cb)) for w, _, _, cb in outs]
    out_shape = [jax.ShapeDtypeStruct((n_rows, total), d) for _, d, total, _ in outs]
    aliases, kernel = {}, body
    if into is not None:
        arr, w, cb = into
        in_specs.append(ANY)
        args.append(arr)
        out_specs.append(pl.BlockSpec((tm, w), lambda i: (i, cb)))
        out_shape.append(jax.ShapeDtypeStruct(arr.shape, arr.dtype))
        aliases = {len(ins): len(outs)}
        n_in = len(ins)

        def kernel(*refs):
            body(*refs[:n_in], *refs[n_in + 1:])

    out_specs += [pl.BlockSpec((r, w), lambda i: (0, 0)) for r, w in accs]
    out_shape += [jax.ShapeDtypeStruct((r, w), F32) for r, w in accs]
    return pl.pallas_call(
        kernel, name=name, grid=(n_rows // tm,), in_specs=in_specs, out_specs=out_specs, out_shape=out_shape,
        input_output_aliases=aliases, compiler_params=_params(("arbitrary",) if accs else ("parallel",)),
    )(*args)


def _zero_first(*accs):
    @pl.when(pl.program_id(0) == 0)
    def _():
        for a in accs:
            a[...] = jnp.zeros_like(a)


def _rope64(t, cos, sin):
    return t * cos + pltpu.roll(t, RET_QK // 2, 1) * sin


def _rope32(t, cos, sin_a, sin_b):
    return t * cos + pltpu.roll(t, LANES - ROPE // 2, 1) * sin_a + pltpu.roll(t, ROPE // 2, 1) * sin_b


def _rms_fwd(name, x, g, tm):
    s, d = x.shape

    def body(x_ref, g_ref, u_ref):
        v = x_ref[...]
        r = lax.rsqrt(jnp.mean(v * v, axis=-1, keepdims=True) + EPS)
        u_ref[...] = (v * r * g_ref[...]).astype(BF16)

    return _rows(name, body, s, tm, [(x, d, 0), (g,)], [(d, BF16)])[0]


def _residual_norm(r, x, g):
    h = x + r
    return h, h * lax.rsqrt(jnp.mean(h * h, axis=-1, keepdims=True) + EPS) * g


def _gate_grads(dm, gr, gm, yr, ym):
    sr, sm = _sig(gr), _sig(gm)
    return dm * sr, dm * sm, jnp.concatenate([dm * yr * (sr * (1.0 - sr)), dm * ym * (sm * (1.0 - sm))], axis=1)


def _rms_bwd(name, dy, x, g, dres, tm, matmul_copy=False):
    s, d = x.shape

    def body(dy_ref, x_ref, g_ref, dres_ref, dx_ref, *rest):
        dg_ref = rest[-1]
        _zero_first(dg_ref)
        v, dyv = x_ref[...], dy_ref[...]
        r = lax.rsqrt(jnp.mean(v * v, axis=-1, keepdims=True) + EPS)
        xh = v * r
        dxh = dyv * g_ref[...]
        dx = dres_ref[...] + r * (dxh - xh * jnp.mean(dxh * xh, axis=-1, keepdims=True))
        dx_ref[...] = dx
        if matmul_copy:
            rest[0][...] = dx.astype(BF16)
        dg_ref[...] += jnp.sum(dyv * xh, axis=0, keepdims=True)

    return _rows(name, body, s, tm, [(dy, d, 0), (x, d, 0), (g,), (dres, d, 0)],
                 [(d, F32)] + [(d, BF16)] * matmul_copy, [(1, d)])


def _final(name, h1, dn, g, tgt, tm):
    s, d = h1.shape

    def body(h_ref, dn_ref, g_ref, t_ref, dh_ref, dh16_ref, dg_ref, loss_ref):
        _zero_first(dg_ref, loss_ref)
        v = h_ref[...] + dn_ref[...]
        r = lax.rsqrt(jnp.mean(v * v, axis=-1, keepdims=True) + EPS)
        xh = v * r
        gv = g_ref[...]
        e = xh * gv - t_ref[...]
        loss_ref[...] += 0.5 * jnp.sum(jnp.mean(e * e, axis=-1, keepdims=True))
        dy = e * (1.0 / d)
        dg_ref[...] += jnp.sum(dy * xh, axis=0, keepdims=True)
        dxh = dy * gv
        dh = r * (dxh - xh * jnp.mean(dxh * xh, axis=-1, keepdims=True))
        dh_ref[...] = dh
        dh16_ref[...] = dh.astype(BF16)

    return _rows(name, body, s, tm, [(h1, d, 0), (dn, d, 0), (g,), (tgt, d, 0)], [(d, F32), (d, BF16)],
                 [(1, d), (1, LANES)])


def _decay_mask(lg, blk):
    n = lax.broadcasted_iota(jnp.int32, (blk, blk), 0)
    m = lax.broadcasted_iota(jnp.int32, (blk, blk), 1)
    w = jnp.exp(lg * jnp.abs(n - m).astype(F32))
    return jnp.where(jnp.right_shift(m, CHUNK_SHIFT) <= jnp.right_shift(n, CHUNK_SHIFT), w, 0.0)


def _decays(lg, blk):
    pos = lax.broadcasted_iota(jnp.int32, (blk, 1), 0).astype(F32)
    return jnp.exp(lg * (pos + 1.0)), jnp.exp(lg * (blk - 1.0 - pos)), jnp.exp(lg * float(blk))


def _ret_fwd(proj, lay, cos, sin, lgs, blk, ride=None):
    s = proj.shape[0]
    heads = lay["ret_heads"]
    nb = s // blk
    scale = RET_QK ** -0.5
    ride_in_specs, ride_ins, ride_out_specs, ride_out_shape, ride_scratch = _ride_args(ride)

    def body(lg_ref, qkv_ref, cos_ref, sin_ref, o_ref, st_ref, state, mask):
        lg = lg_ref[0:1, 0:1]

        @pl.when(pl.program_id(1) == 0)
        def _():
            state[...] = jnp.zeros_like(state)
            mask[...] = _decay_mask(lg, blk)

        a, c, gb = _decays(lg, blk)
        q = _rope64(qkv_ref[:, :RET_QK], cos_ref[...], sin_ref[...])
        k = _rope64(qkv_ref[:, RET_QK:2 * RET_QK], cos_ref[...], sin_ref[...]) * scale
        v = qkv_ref[:, 2 * RET_QK:]
        st = state[...]
        st_ref[...] = st
        sm = _dot(q, k, NT) * mask[...]
        o_ref[...] = _dot(sm, v, NN) + _dot(q * a, st, NN)
        state[...] = st * gb + _dot(k * c, v, TN)

    first = lay["off"]["heads"] // RET_HEAD
    res = pl.pallas_call(
        _with_ride(body, ride, (heads, nb), 0, 4, 2), name="ret_fwd", grid=(heads, nb),
        in_specs=[pl.BlockSpec((None, 8, LANES), lambda h, b: (h, 0, 0)),
                  pl.BlockSpec((blk, RET_HEAD), lambda h, b: (b, first + h)),
                  pl.BlockSpec((blk, LANES), lambda h, b: (b, 0)),
                  pl.BlockSpec((blk, LANES), lambda h, b: (b, 0))] + ride_in_specs,
        out_specs=[pl.BlockSpec((blk, RET_V), lambda h, b: (b, h)),
                   pl.BlockSpec((None, None, RET_QK, RET_V), lambda h, b: (h, b, 0, 0))] + ride_out_specs,
        out_shape=[jax.ShapeDtypeStruct((s, heads * RET_V), F32),
                   jax.ShapeDtypeStruct((heads, nb, RET_QK, RET_V), F32)] + ride_out_shape,
        scratch_shapes=[pltpu.VMEM((RET_QK, RET_V), F32), pltpu.VMEM((blk, blk), F32)] + ride_scratch,
        compiler_params=_params(("arbitrary", "arbitrary") if ride else ("parallel", "arbitrary")),
    )(lgs, proj, cos, sin, *ride_ins)
    return (res[0], res[1], res[2:]) if ride else res


def _ret_bwd(proj, lay, cos, sin, lgs, states, d_o, d_proj, blk, ride=None):
    ride_in_specs, ride_ins, ride_out_specs, ride_out_shape, ride_scratch = _ride_args(ride)
    s = proj.shape[0]
    heads = lay["ret_heads"]
    nb = s // blk
    scale = RET_QK ** -0.5

    def body(lg_ref, qkv_ref, cos_ref, sin_ref, st_ref, do_ref, _, dqkv_ref, dstate, mask):
        lg = lg_ref[0:1, 0:1]

        @pl.when(pl.program_id(1) == 0)
        def _():
            dstate[...] = jnp.zeros_like(dstate)
            mask[...] = _decay_mask(lg, blk)

        a, c, gb = _decays(lg, blk)
        cs, sn = cos_ref[...], sin_ref[...]
        q = _rope64(qkv_ref[:, :RET_QK], cs, sn)
        k = _rope64(qkv_ref[:, RET_QK:2 * RET_QK], cs, sn) * scale
        v = qkv_ref[:, 2 * RET_QK:]
        st = st_ref[...]
        do = do_ref[...]
        dst = dstate[...]
        mk = mask[...]
        sm = _dot(q, k, NT) * mk
        ds = _dot(do, v, NT) * mk
        dq = _dot(ds, k, NN) + _dot(do, st, NT) * a
        dk = _dot(ds, q, TN) + _dot(v, dst, NT) * c
        dqkv_ref[:, 2 * RET_QK:] = (_dot(sm, do, TN) + _dot(k * c, dst, NN)).astype(BF16)
        dstate[...] = dst * gb + _dot(q * a, do, TN)
        dqkv_ref[:, :RET_QK] = _rope64(dq, cs, -sn).astype(BF16)
        dqkv_ref[:, RET_QK:2 * RET_QK] = (_rope64(dk, cs, -sn) * scale).astype(BF16)

    first = lay["off"]["heads"] // RET_HEAD
    last = nb - 1
    res = pl.pallas_call(
        _with_ride(body, ride, (heads, nb), 0, 7, 1), name="ret_bwd", grid=(heads, nb),
        in_specs=[pl.BlockSpec((None, 8, LANES), lambda h, b: (h, 0, 0)),
                  pl.BlockSpec((blk, RET_HEAD), lambda h, b: (last - b, first + h)),
                  pl.BlockSpec((blk, LANES), lambda h, b: (last - b, 0)),
                  pl.BlockSpec((blk, LANES), lambda h, b: (last - b, 0)),
                  pl.BlockSpec((None, None, RET_QK, RET_V), lambda h, b: (h, last - b, 0, 0)),
                  pl.BlockSpec((blk, RET_V), lambda h, b: (last - b, h)), ANY] + ride_in_specs,
        out_specs=[pl.BlockSpec((blk, RET_HEAD), lambda h, b: (last - b, first + h))] + ride_out_specs,
        out_shape=[jax.ShapeDtypeStruct(d_proj.shape, d_proj.dtype)] + ride_out_shape,
        scratch_shapes=[pltpu.VMEM((RET_QK, RET_V), F32), pltpu.VMEM((blk, blk), F32)] + ride_scratch,
        input_output_aliases={6: 0},
        compiler_params=_params(("arbitrary", "arbitrary") if ride else ("parallel", "arbitrary")),
    )(lgs, proj, cos, sin, states, d_o, d_proj, *ride_ins)
    return (res[0], res[1:]) if ride else res[0]


def _ret_post(proj, lay, o, g, tm):
    s, vw = o.shape
    heads = lay["ret_heads"]

    def body(o_ref, rg_ref, g_ref, ry_ref):
        for h in range(heads):
            sl = slice(h * RET_V, (h + 1) * RET_V)
            oh = o_ref[:, sl]
            dlt = oh - jnp.mean(oh, axis=-1, keepdims=True)
            rstd = lax.rsqrt(jnp.mean(dlt * dlt, axis=-1, keepdims=True) + EPS)
            rg = rg_ref[:, sl]
            ry_ref[:, sl] = (dlt * rstd * g_ref[:, sl] * (rg * _sig(rg))).astype(BF16)

    return _rows("ret_post", body, s, tm, [(o, vw, 0), (proj, vw, lay["off"]["r_g"] // vw), (g,)], [(vw, BF16)])[0]


def _ret_post_bwd(proj, lay, d_ry, o, g, d_proj, tm):
    s, vw = o.shape
    heads = lay["ret_heads"]

    def body(dry_ref, o_ref, rg_ref, g_ref, do_ref, drg_ref, dg_ref):
        _zero_first(dg_ref)
        for h in range(heads):
            sl = slice(h * RET_V, (h + 1) * RET_V)
            oh = o_ref[:, sl]
            dlt = oh - jnp.mean(oh, axis=-1, keepdims=True)
            rstd = lax.rsqrt(jnp.mean(dlt * dlt, axis=-1, keepdims=True) + EPS)
            oh = dlt * rstd
            gv = g_ref[:, sl]
            rg = rg_ref[:, sl]
            sg = _sig(rg)
            dry = dry_ref[:, sl]
            dt = dry * (rg * sg)
            drg_ref[:, sl] = (dry * (oh * gv) * (sg * (1.0 + rg * (1.0 - sg)))).astype(BF16)
            dg_ref[:, sl] += jnp.sum(dt * oh, axis=0, keepdims=True)
            doh = dt * gv
            do_ref[:, sl] = rstd * (doh - jnp.mean(doh, axis=-1, keepdims=True)
                                    - oh * jnp.mean(doh * oh, axis=-1, keepdims=True))

    return _rows("ret_post_bwd", body, s, tm,
                 [(d_ry, vw, 0), (o, vw, 0), (proj, vw, lay["off"]["r_g"] // vw), (g,)],
                 [(vw, F32)], [(1, vw)], into=(d_proj, vw, lay["off"]["r_g"] // vw))


def _mla_prep(proj, lay, gq, gkv, cos, sin_a, sin_b, tm):
    s = proj.shape[0]
    ql, kl = lay["q_lora"], lay["kv_lora"]

    def body(cq_ref, ckv_ref, kpe_ref, gq_ref, gkv_ref, cos_ref, sa_ref, sb_ref, cqn_ref, ckvn_ref, kpr_ref):
        for src, gref, dst in ((cq_ref, gq_ref, cqn_ref), (ckv_ref, gkv_ref, ckvn_ref)):
            v = src[...]
            r = lax.rsqrt(jnp.mean(v * v, axis=-1, keepdims=True) + EPS)
            dst[...] = (v * r * gref[...]).astype(BF16)
        kpr_ref[...] = _rope32(kpe_ref[...], cos_ref[...], sa_ref[...], sb_ref[...]).astype(BF16)

    off = lay["off"]
    return _rows("mla_prep", body, s, tm,
                 [(proj, ql, off["c_q"] // ql), (proj, kl, off["c_kv"] // kl), (proj, LANES, off["k_pe"] // LANES),
                  (gq,), (gkv,), (cos, LANES, 0), (sin_a, LANES, 0), (sin_b, LANES, 0)],
                 [(ql, BF16), (kl, BF16), (LANES, BF16)])


def _mla_prep_bwd(proj, lay, d_cqn, d_ckvn, gq, gkv, d_proj, tm):
    s = proj.shape[0]
    ql, kl = lay["q_lora"], lay["kv_lora"]

    def body(dq_ref, dkv_ref, cq_ref, ckv_ref, gq_ref, gkv_ref, dc_ref, dgq_ref, dgkv_ref):
        _zero_first(dgq_ref, dgkv_ref)
        for dref, src, gref, cols, dg in ((dq_ref, cq_ref, gq_ref, slice(0, ql), dgq_ref),
                                          (dkv_ref, ckv_ref, gkv_ref, slice(ql, ql + kl), dgkv_ref)):
            v, dy = src[...], dref[...]
            r = lax.rsqrt(jnp.mean(v * v, axis=-1, keepdims=True) + EPS)
            xh = v * r
            dxh = dy * gref[...]
            dc_ref[:, cols] = (r * (dxh - xh * jnp.mean(dxh * xh, axis=-1, keepdims=True))).astype(BF16)
            dg[...] += jnp.sum(dy * xh, axis=0, keepdims=True)

    off = lay["off"]
    return _rows("mla_prep_bwd", body, s, tm,
                 [(d_cqn, ql, 0), (d_ckvn, kl, 0), (proj, ql, off["c_q"] // ql), (proj, kl, off["c_kv"] // kl),
                  (gq,), (gkv,)],
                 [], [(1, ql), (1, kl)], into=(d_proj, ql + kl, off["c_q"] // (ql + kl)))


def _q_operand(r, cos, sin_a, sin_b):
    qs = (NOPE + ROPE) ** -0.5 * math.log2(math.e)
    cs, sa, sb = cos * qs, sin_a * qs, sin_b * qs
    parts = []
    for lo in range(0, r.shape[1], QPAD):
        parts += [r[:, lo:lo + NOPE] * qs, _rope32(r[:, lo + NOPE:lo + QPAD], cs, sa, sb)]
    return (jnp.concatenate(parts, axis=1),)


def _k_operand(r, kpr):
    parts = []
    for lo in range(0, r.shape[1], QPAD):
        parts += [r[:, lo:lo + NOPE], kpr.astype(F32)]
    return r, jnp.concatenate(parts, axis=1)


def _rope_key_grad(parts, lay, d_proj, tm):
    s, w = parts.shape

    def body(p_ref, dkpe_ref):
        dkpe_ref[:, :LANES] = sum(p_ref[:, lo:lo + LANES] for lo in range(0, w, LANES)).astype(BF16)
        dkpe_ref[:, LANES:] = jnp.zeros((tm, LANES), BF16)

    return _rows("rope_key_grad", body, s, tm, [(parts, w, 0)], [],
                 into=(d_proj, 2 * LANES, lay["off"]["k_pe"] // (2 * LANES)))[0]


def _diag_mask(t, keys_on_rows=False):
    row = lax.broadcasted_iota(jnp.int32, (t, t), 0)
    col = lax.broadcasted_iota(jnp.int32, (t, t), 1)
    key, query = (row, col) if keys_on_rows else (col, row)
    return jnp.right_shift(key, CHUNK_SHIFT) <= jnp.right_shift(query, CHUNK_SHIFT)


def _tile_pairs(nt, by_key):
    if by_key:
        pairs = [(i, j) for j in range(nt) for i in range(j, nt)]
    else:
        pairs = [(i, j) for i in range(nt) for j in range(i + 1)]
    return (jnp.asarray([p[0] for p in pairs], jnp.int32), jnp.asarray([p[1] for p in pairs], jnp.int32))


def _head_block(heads):
    return 4 if heads % 4 == 0 else 2 if heads % 2 == 0 else 1


def _attn_fwd(qf, kf, kv, lay, t, ride=None):
    s = qf.shape[0]
    heads = lay["mla_heads"]
    hb = _head_block(heads)
    nt = s // t
    qi, kj = _tile_pairs(nt, False)
    grid = (heads // hb, int(qi.shape[0]))
    ride_in_specs, ride_ins, ride_out_specs, ride_out_shape, ride_scratch = _ride_args(ride)

    def body(qi_ref, kj_ref, q_ref, k_ref, kv_ref, o_ref, lse_ref, m_s, l_s, acc):
        p = pl.program_id(1)
        i, j = qi_ref[p], kj_ref[p]

        @pl.when(j == 0)
        def _():
            m_s[...] = jnp.full_like(m_s, -jnp.inf)
            l_s[...] = jnp.zeros_like(l_s)
            acc[...] = jnp.zeros_like(acc)

        def step(diagonal):
            ones = jnp.ones((t, LANES), BF16)
            scores = [_dot(q_ref[:, hh * QPAD:(hh + 1) * QPAD], k_ref[:, hh * QPAD:(hh + 1) * QPAD], NT)
                      for hh in range(hb)]
            for hh in range(hb):
                sc = scores[hh]
                if diagonal:
                    sc = jnp.where(_diag_mask(t), sc, -jnp.inf)
                cols = [sc[:, c * LANES:(c + 1) * LANES] for c in range(t // LANES)]
                m_old = m_s[hh]
                m_new = jnp.maximum(m_old, jnp.max(functools.reduce(jnp.maximum, cols), axis=-1, keepdims=True))
                alpha = jnp.exp2(m_old - m_new)
                pr = jnp.concatenate([jnp.exp2(c - m_new).astype(BF16) for c in cols], axis=1)
                pv = _dot(pr, jnp.concatenate([kv_ref[:, hh * QPAD + NOPE:(hh + 1) * QPAD], ones], axis=1), NN)
                l_new = alpha * l_s[hh] + pv[:, VHEAD:]
                a_new = alpha * acc[hh] + pv[:, :VHEAD]
                if diagonal:
                    o_ref[:, hh * VHEAD:(hh + 1) * VHEAD] = a_new / l_new
                    lse_ref[hh] = jnp.transpose(m_new + jnp.log2(l_new))[:1]
                else:
                    m_s[hh], l_s[hh], acc[hh] = m_new, l_new, a_new

        pl.when(j < i)(functools.partial(step, False))
        pl.when(j == i)(functools.partial(step, True))

    res = pl.pallas_call(
        _with_ride(body, ride, grid, 2, 3, 2), name="attn_fwd",
        grid_spec=pltpu.PrefetchScalarGridSpec(
            num_scalar_prefetch=2, grid=grid,
            in_specs=[pl.BlockSpec((t, hb * QPAD), lambda h, p, qi, kj: (qi[p], h)),
                      pl.BlockSpec((t, hb * QPAD), lambda h, p, qi, kj: (kj[p], h)),
                      pl.BlockSpec((t, hb * QPAD), lambda h, p, qi, kj: (kj[p], h))] + ride_in_specs,
            out_specs=[pl.BlockSpec((t, hb * VHEAD), lambda h, p, qi, kj: (qi[p], h)),
                       pl.BlockSpec((hb, 1, t), lambda h, p, qi, kj: (h, 0, qi[p]))] + ride_out_specs,
            scratch_shapes=[pltpu.VMEM((hb, t, LANES), F32), pltpu.VMEM((hb, t, LANES), F32),
                            pltpu.VMEM((hb, t, VHEAD), F32)] + ride_scratch),
        out_shape=[jax.ShapeDtypeStruct((s, heads * VHEAD), F32),
                   jax.ShapeDtypeStruct((heads, 1, s), F32)] + ride_out_shape,
        compiler_params=_params(("arbitrary", "arbitrary") if ride else ("parallel", "arbitrary")),
    )(qi, kj, qf, kf, kv, *ride_ins)
    return (res[0], res[1], res[2:]) if ride else res


def _attn_delta(d_o, o, lay, tm):
    s = o.shape[0]
    heads = lay["mla_heads"]

    def body(do_ref, o_ref, dl_ref):
        for h in range(heads):
            sl = slice(h * VHEAD, (h + 1) * VHEAD)
            dl_ref[h] = jnp.sum(jnp.transpose(do_ref[:, sl] * o_ref[:, sl]), axis=0, keepdims=True)

    tile = pl.BlockSpec((tm, heads * VHEAD), lambda i: (i, 0))
    return pl.pallas_call(
        body, name="attn_delta", grid=(s // tm,), in_specs=[tile, tile],
        out_specs=pl.BlockSpec((heads, 1, tm), lambda i: (0, 0, i)),
        out_shape=jax.ShapeDtypeStruct((heads, 1, s), F32),
        compiler_params=_params(("parallel",)),
    )(d_o, o)


def _attn_bwd(qf, kf, kv, lse, delta, d_o, cos, sin_a, sin_b, lay, t, ride=None):
    s = qf.shape[0]
    heads = lay["mla_heads"]
    hb = _head_block(heads)
    nt = s // t
    scale = (NOPE + ROPE) ** -0.5
    qi, kj = _tile_pairs(nt, True)
    grid = (heads // hb, int(qi.shape[0]))
    ride_in_specs, ride_ins, ride_out_specs, ride_out_shape, ride_scratch = _ride_args(ride)

    def body(qi_ref, kj_ref, q_ref, k_ref, kv_ref, lse_ref, dl_ref, do_ref, cos_ref, sa_ref, sb_ref,
             dqp_ref, dkv_ref, dkpe_ref, dq_acc, dk_acc, dv_acc):
        p = pl.program_id(1)
        i, j = qi_ref[p], kj_ref[p]
        rows = pl.ds(pl.multiple_of(i * t, t), t)

        def unrope(v):
            return _rope32(v, cos_ref[...], -sa_ref[...], -sb_ref[...])

        @pl.when(p == 0)
        def _():
            dq_acc[...] = jnp.zeros_like(dq_acc)

        def step(diagonal):
            for hh in range(hb):
                lo = hh * QPAD
                q, k = q_ref[:, lo:lo + QPAD], k_ref[:, lo:lo + QPAD]
                do = do_ref[:, hh * VHEAD:(hh + 1) * VHEAD]
                pr = jnp.exp2(_dot(k, q, NT) - lse_ref[hh])
                if diagonal:
                    pr = jnp.where(_diag_mask(t, keys_on_rows=True), pr, 0.0)
                dv_part = _dot(pr, do, NN)
                ds = (pr * (_dot(kv_ref[:, lo + NOPE:lo + QPAD], do, NT) - dl_ref[hh])).astype(BF16)
                dk_part = _dot(ds, q, NN)
                dq = dq_acc[rows, lo:lo + QPAD] + _dot(ds, k, TN) * scale
                if diagonal:
                    dqp_ref[:, lo:lo + NOPE] = dq[:, :NOPE].astype(BF16)
                    dqp_ref[:, lo + NOPE:lo + QPAD] = unrope(dq[:, NOPE:]).astype(BF16)
                    dk_acc[hh], dv_acc[hh] = dk_part, dv_part
                else:
                    dq_acc[rows, lo:lo + QPAD] = dq
                    dk_acc[hh] += dk_part
                    dv_acc[hh] += dv_part

        pl.when(i > j)(functools.partial(step, False))
        pl.when(i == j)(functools.partial(step, True))

        @pl.when(i == nt - 1)
        def _():
            kpe = jnp.zeros((t, LANES), F32)
            for hh in range(hb):
                lo = hh * QPAD
                dk = dk_acc[hh] * math.log(2.0)
                dkv_ref[:, lo:lo + NOPE] = dk[:, :NOPE].astype(BF16)
                dkv_ref[:, lo + NOPE:lo + QPAD] = dv_acc[hh].astype(BF16)
                kpe = kpe + dk[:, NOPE:]
            dkpe_ref[...] = unrope(kpe)

    table = pl.BlockSpec((t, LANES), lambda h, p, qi, kj: (kj[p], 0))
    res = pl.pallas_call(
        _with_ride(body, ride, grid, 2, 9, 3), name="attn_bwd",
        grid_spec=pltpu.PrefetchScalarGridSpec(
            num_scalar_prefetch=2, grid=grid,
            in_specs=[pl.BlockSpec((t, hb * QPAD), lambda h, p, qi, kj: (qi[p], h)),
                      pl.BlockSpec((t, hb * QPAD), lambda h, p, qi, kj: (kj[p], h)),
                      pl.BlockSpec((t, hb * QPAD), lambda h, p, qi, kj: (kj[p], h)),
                      pl.BlockSpec((hb, 1, t), lambda h, p, qi, kj: (h, 0, qi[p])),
                      pl.BlockSpec((hb, 1, t), lambda h, p, qi, kj: (h, 0, qi[p])),
                      pl.BlockSpec((t, hb * VHEAD), lambda h, p, qi, kj: (qi[p], h)),
                      table, table, table] + ride_in_specs,
            out_specs=[pl.BlockSpec((t, hb * QPAD), lambda h, p, qi, kj: (kj[p], h)),
                       pl.BlockSpec((t, hb * QPAD), lambda h, p, qi, kj: (kj[p], h)),
                       pl.BlockSpec((t, LANES), lambda h, p, qi, kj: (kj[p], h))] + ride_out_specs,
            scratch_shapes=[pltpu.VMEM((s, hb * QPAD), F32), pltpu.VMEM((hb, t, QPAD), F32),
                            pltpu.VMEM((hb, t, VHEAD), F32)] + ride_scratch),
        out_shape=[jax.ShapeDtypeStruct((s, heads * QPAD), BF16),
                   jax.ShapeDtypeStruct((s, heads * QPAD), BF16),
                   jax.ShapeDtypeStruct((s, heads // hb * LANES), F32)] + ride_out_shape,
        compiler_params=_params(("arbitrary", "arbitrary") if ride else ("parallel", "arbitrary")),
    )(qi, kj, qf, kf, kv, lse, delta, d_o, cos, sin_a, sin_b, *ride_ins)
    return (res[0], res[1], res[2], res[3:]) if ride else res


ANY = pl.BlockSpec(memory_space=pl.ANY)


def _place():
    return lax.axis_index("x"), lax.axis_index("y"), lax.axis_index("c")


def _other_chips(x, y):
    return [(1 - x, y), (x, 1 - y), (1 - x, 1 - y)]


class _Exchange:
    def __init__(self, ins, out_shape, scratch, phases):
        self.ins, self.out_shape, self.scratch, self.phases = list(ins), list(out_shape), list(scratch), phases

    def counts(self):
        return len(self.ins), len(self.out_shape), len(self.scratch)

    def run(self, r_in, r_out, r_scratch, conds):
        for cond, phase in zip(conds, self.phases):
            if phase is not None and cond is not None:
                pl.when(cond)(functools.partial(phase, r_in, r_out, r_scratch))


def _steps(ids, sizes):
    lin, total = 0, 1
    for i, n in zip(ids, sizes):
        lin, total = lin * n + i, total * n
    return lin == 0, lin == total // 2, lin == total - 1


def _ride_args(ride):
    if ride is None:
        return [], [], [], [], []
    n_in, n_out, _ = ride.counts()
    return [ANY] * n_in, ride.ins, [ANY] * n_out, ride.out_shape, ride.scratch


def _with_ride(body, ride, grid, n_prefetch, n_in, n_out):
    if ride is None:
        return body
    r_in, r_out, r_sc = ride.counts()

    def hosted(*refs):
        cuts = (n_prefetch, n_in, r_in, n_out, r_out)
        parts, pos = [], 0
        for n in cuts:
            parts.append(refs[pos:pos + n])
            pos += n
        pre, ins, ride_in, outs, ride_out = parts
        scratch, ride_scratch = refs[pos:len(refs) - r_sc], refs[len(refs) - r_sc:]
        first, mid, last = _steps([pl.program_id(d) for d in range(len(grid))], grid)
        ride.run(ride_in, ride_out, ride_scratch, (first, mid, None))
        body(*pre, *ins, *outs, *scratch)
        ride.run(ride_in, ride_out, ride_scratch, (None, None, last))

    return hosted


def _exchange_alone(name, ex):
    n_in, n_out, _ = ex.counts()

    def body(*refs):
        for phase in ex.phases:
            if phase is not None:
                phase(refs[:n_in], refs[n_in:n_in + n_out], refs[n_in + n_out:])

    return pl.pallas_call(
        body, name=name, in_specs=[ANY] * n_in, out_specs=[ANY] * n_out, out_shape=ex.out_shape,
        scratch_shapes=ex.scratch)(*ex.ins)


def _gather_exchange(shards):
    nw = len(shards)

    def parts(ins, outs, sems):
        send_sems, recv_sems, local_sems = sems
        x, y, c = _place()

        def slot(px, py, pc):
            return 4 * px + 2 * py + pc

        def copy(w, k, rows, to, src=None):
            return pltpu.make_async_remote_copy(
                src_ref=rows if src is None else src, dst_ref=rows, send_sem=send_sems.at[w, k],
                recv_sem=recv_sems.at[w, k], device_id=to, device_id_type=MESH)

        def plan(w, mine):
            side = c if mine else 1 - c
            half = shards[w].shape[0] // 2
            whole = lambda px, py: outs[w].at[slot(px, py, side)]
            top = lambda px, py: outs[w].at[slot(px, py, side), pl.ds(0, half)]
            bottom = lambda px, py: outs[w].at[slot(px, py, side), pl.ds(half, half)]
            xn, yn, sib = (1 - x, y, side), (x, 1 - y, side), (x, y, 1 - side)
            own = ins[w] if mine else None
            return [copy(w, 0, whole(x, y), sib, own), copy(w, 1, whole(x, y), xn, own),
                    copy(w, 2, whole(x, y), yn, own), copy(w, 3, top(1 - x, y), yn), copy(w, 4, bottom(x, 1 - y), xn),
                    copy(w, 5, whole(1 - x, y), sib), copy(w, 6, whole(x, 1 - y), sib),
                    copy(w, 7, top(1 - x, 1 - y), sib), copy(w, 8, bottom(1 - x, 1 - y), sib)]

        def arrivals(w):
            half = shards[w].shape[0] // 2
            at = lambda px, py, *rows: outs[w].at[(slot(px, py, c),) + rows]
            return {1: copy(w, 1, at(1 - x, y), (x, y, c)), 2: copy(w, 2, at(x, 1 - y), (x, y, c)),
                    3: copy(w, 3, at(1 - x, 1 - y, pl.ds(0, half)), (x, y, c)),
                    4: copy(w, 4, at(1 - x, 1 - y, pl.ds(half, half)), (x, y, c))}

        local = [pltpu.make_async_copy(ins[w], outs[w].at[slot(x, y, c)], local_sems.at[w]) for w in range(nw)]
        return plan, arrivals, local

    def start(ins, outs, sems):
        plan, _, local = parts(ins, outs, sems)
        for cp in local:
            cp.start()
        for w in range(nw):
            for k in (0, 1, 2):
                plan(w, True)[k].start()

    def middle(ins, outs, sems):
        plan, arrivals, _ = parts(ins, outs, sems)
        for landed, onward in ((1, (3, 5)), (2, (4, 6))):
            for w in range(nw):
                arrivals(w)[landed].wait_recv()
                for k in onward:
                    plan(w, True)[k].start()

    def finish(ins, outs, sems):
        plan, arrivals, local = parts(ins, outs, sems)
        for landed, onward in ((3, 7), (4, 8)):
            for w in range(nw):
                arrivals(w)[landed].wait_recv()
                plan(w, True)[onward].start()
        for w in range(nw):
            from_sibling = plan(w, False)
            for k in (0, 5, 6, 7, 8):
                from_sibling[k].wait_recv()
            for cp in plan(w, True):
                cp.wait_send()
        for cp in local:
            cp.wait()

    return _Exchange(
        shards, [jax.ShapeDtypeStruct((N_DEV,) + s.shape, s.dtype) for s in shards],
        [pltpu.SemaphoreType.DMA((nw, 9)), pltpu.SemaphoreType.DMA((nw, 9)), pltpu.SemaphoreType.DMA((nw,))],
        (start, middle, finish))


def _sibling_exchange(grads):
    nw = len(grads)

    def copies(ins, outs, sems):
        x, y, c = _place()
        return [pltpu.make_async_remote_copy(
            src_ref=ins[w].at[2 * p + (1 - c)], dst_ref=outs[w].at[p], send_sem=sems[0].at[w, p],
            recv_sem=sems[1].at[w, p], device_id=(x, y, 1 - c), device_id_type=MESH)
            for w in range(nw) for p in range(4)]

    def start(ins, outs, sems):
        for cp in copies(ins, outs, sems):
            cp.start()

    def finish(ins, outs, sems):
        for cp in copies(ins, outs, sems):
            cp.wait()

    return _Exchange(grads, [jax.ShapeDtypeStruct((4,) + g.shape[1:], g.dtype) for g in grads],
                     [pltpu.SemaphoreType.DMA((nw, 4)), pltpu.SemaphoreType.DMA((nw, 4))], (start, None, finish))


def _chips_exchange(sums):
    nw = len(sums)

    def copies(ins, outs, sems):
        x, y, c = _place()
        return [pltpu.make_async_remote_copy(
            src_ref=ins[w].at[2 * px + py], dst_ref=outs[w].at[k], send_sem=sems[0].at[w, k],
            recv_sem=sems[1].at[w, k], device_id=(px, py, c), device_id_type=MESH)
            for w in range(nw) for k, (px, py) in enumerate(_other_chips(x, y))]

    def start(ins, outs, sems):
        for cp in copies(ins, outs, sems):
            cp.start()

    def finish(ins, outs, sems):
        for cp in copies(ins, outs, sems):
            cp.wait()

    return _Exchange(sums, [jax.ShapeDtypeStruct((3,) + g.shape[1:], g.dtype) for g in sums],
                     [pltpu.SemaphoreType.DMA((nw, 3)), pltpu.SemaphoreType.DMA((nw, 3))], (start, None, finish))


def _pair_sum(name, g, got, c_arr, tr):
    _, rows, cols = g.shape
    tr = _tile_rows(rows, tr)

    def body(c_ref, a_ref, b_ref, o_ref):
        o_ref[...] = (a_ref[...].astype(F32) + b_ref[...].astype(F32)).astype(BF16)

    return pl.pallas_call(
        body, name=name,
        grid_spec=pltpu.PrefetchScalarGridSpec(
            num_scalar_prefetch=1, grid=(4, rows // tr),
            in_specs=[pl.BlockSpec((None, tr, cols), lambda p, r, cr: (2 * p + cr[0], r, 0)),
                      pl.BlockSpec((None, tr, cols), lambda p, r, cr: (p, r, 0))],
            out_specs=pl.BlockSpec((None, tr, cols), lambda p, r, cr: (p, r, 0))),
        out_shape=jax.ShapeDtypeStruct((4, rows, cols), BF16),
        compiler_params=_params(("parallel", "parallel")),
    )(c_arr, g, got)


def _tile_rows(rows, pref):
    t = min(rows, pref)
    while rows % t or t % 8:
        t -= 1
    return t


def _adam(w, g, m, v):
    m = ADAM_B1 * m + (1.0 - ADAM_B1) * g
    v = ADAM_B2 * v + (1.0 - ADAM_B2) * (g * g)
    m_hat = m / (1.0 - ADAM_B1 ** ADAM_STEP)
    v_hat = v / (1.0 - ADAM_B2 ** ADAM_STEP)
    return -ADAM_LR * (m_hat / (jnp.sqrt(v_hat) + ADAM_EPS) + ADAM_WD * w), m, v


def _adamw_shard(name, w, m, v, sums, got, chip_arr, tr):
    _, rows, cols = w.shape
    tr = _tile_rows(rows, tr)

    def body(p_ref, w_ref, m_ref, v_ref, s_ref, r_ref, g_ref, d_ref, nm_ref, nv_ref):
        g = s_ref[...].astype(F32)
        for k in range(3):
            g = g + r_ref[k].astype(F32)
        g_ref[...] = g
        d_ref[...], nm_ref[...], nv_ref[...] = _adam(w_ref[...], g, m_ref[...], v_ref[...])

    tile = pl.BlockSpec((None, tr, cols), lambda r, pr: (0, r, 0))
    return pl.pallas_call(
        body, name=name,
        grid_spec=pltpu.PrefetchScalarGridSpec(
            num_scalar_prefetch=1, grid=(rows // tr,),
            in_specs=[tile, tile, tile,
                      pl.BlockSpec((None, tr, cols), lambda r, pr: (pr[0], r, 0)),
                      pl.BlockSpec((3, tr, cols), lambda r, pr: (0, r, 0))],
            out_specs=[tile] * 4),
        out_shape=[jax.ShapeDtypeStruct((1, rows, cols), F32)] * 4,
        compiler_params=_params(("parallel",)),
    )(chip_arr, w, m, v, sums, got)


def _small_all_reduce_adam(part, w, m, v):
    rows = part.shape[0]

    def body(p_ref, w_ref, m_ref, v_ref, g_ref, d_ref, nm_ref, nv_ref, buf, send_sems, recv_sems):
        x, y, c = _place()
        me = 4 * x + 2 * y + c
        buf[me] = p_ref[...]
        peers = [(x, y, 1 - c)] + [(px, py, pc) for px, py in _other_chips(x, y) for pc in (c, 1 - c)]
        copies = []
        for k, peer in enumerate(peers):
            cp = pltpu.make_async_remote_copy(
                src_ref=buf.at[me], dst_ref=buf.at[me], send_sem=send_sems.at[k], recv_sem=recv_sems.at[k],
                device_id=peer, device_id_type=MESH)
            cp.start()
            copies.append(cp)
        for cp in copies:
            cp.wait()
        g = buf[0]
        for k in range(1, N_DEV):
            g = g + buf[k]
        g_ref[...] = g
        d_ref[...], nm_ref[...], nv_ref[...] = _adam(w_ref[...], g, m_ref[...], v_ref[...])

    vm = pl.BlockSpec(memory_space=pltpu.VMEM)
    return pl.pallas_call(
        body, name="gains_all_reduce_adamw",
        in_specs=[vm] * 4, out_specs=[vm] * 4,
        out_shape=[jax.ShapeDtypeStruct((rows, LANES), F32)] * 4,
        scratch_shapes=[pltpu.VMEM((N_DEV, rows, LANES), F32), pltpu.SemaphoreType.DMA((7,)),
                        pltpu.SemaphoreType.DMA((7,))],
        compiler_params=pltpu.CompilerParams(has_side_effects=True),
    )(part, w, m, v)


IN_ORDER = ("r_q", "r_k", "r_v", "r_g", "c_q", "c_kv", "k_pe", "g_ret", "g_mla")
RET_HEAD = 2 * RET_QK + RET_V


def _make_layout(d, vw, qw, ql, kl, mla_w):
    width = {"r_q": qw, "r_k": qw, "r_v": vw, "r_g": vw, "c_q": ql, "c_kv": kl, "k_pe": ROPE, "g_ret": d, "g_mla": d}
    src, o = {}, 0
    for n in IN_ORDER:
        src[n] = o
        o += width[n]
    heads = vw // RET_V
    off, pieces, o = {}, [], 0

    def put(name, w, s):
        nonlocal o
        off.setdefault(name, o)
        pieces.append((o, w, s))
        o += w

    for n in ("g_ret", "g_mla", "r_g"):
        put(n, width[n], src[n])
    for h in range(heads):
        put("heads", RET_QK, src["r_q"] + h * RET_QK)
        put("heads", RET_QK, src["r_k"] + h * RET_QK)
        put("heads", RET_V, src["r_v"] + h * RET_V)
    for n in ("c_q", "c_kv", "k_pe"):
        put(n, width[n], src[n])
    total = off["k_pe"] + 2 * LANES
    for n, blk in (("g_ret", d), ("g_mla", d), ("r_g", vw), ("heads", RET_HEAD), ("c_q", ql + kl), ("k_pe", 2 * LANES)):
        assert off[n] % blk == 0
    assert ql == kl and off["c_kv"] == off["c_q"] + ql
    return {"off": off, "pieces": pieces, "total": total, "n_in": sum(width.values()),
            "ret_heads": heads, "mla_heads": mla_w // VHEAD, "q_lora": ql, "kv_lora": kl}


def _cols_to_full(g):
    n, r, c = g.shape
    return jnp.transpose(g, (1, 0, 2)).reshape(r, n * c)


def _full_to_cols(w):
    r, c = w.shape
    return jnp.transpose(w.reshape(r, N_DEV, c // N_DEV), (1, 0, 2))


def _w_in_to_mine(g, lay):
    _, rows, cols = g.shape
    parts, at = [], 0
    for o, w, s in lay["pieces"]:
        if o > at:
            parts.append(jnp.zeros((rows, o - at), g.dtype))
        while w > 0:
            k, a = divmod(s, cols)
            take = min(w, cols - a)
            parts.append(g[k, :, a:a + take])
            s, w, o = s + take, w - take, o + take
        at = o
    parts.append(jnp.zeros((rows, lay["total"] - at), g.dtype))
    return jnp.concatenate(parts, axis=1)


def _mine_to_blocks(g, lay):
    cols = lay["n_in"] // N_DEV
    by_src = sorted(lay["pieces"], key=lambda p: p[2])
    blocks = []
    for k in range(N_DEV):
        lo, hi, parts = k * cols, (k + 1) * cols, []
        for o, w, s in by_src:
            a, b = max(lo, s), min(hi, s + w)
            if a < b:
                parts.append(g[:, o + a - s:o + b - s])
        blocks.append(jnp.concatenate(parts, axis=1))
    return jnp.stack(blocks)


def _rope_tables(positions, half):
    inv = ROPE_THETA ** (-jnp.arange(half, dtype=F32) / half)
    ang = positions.astype(F32)[:, None] * inv
    return jnp.cos(ang), jnp.sin(ang)


def _pack_rows(vs):
    return jnp.concatenate([v.reshape(-1, LANES) for v in vs], axis=0)


def kernel(x, positions, norm_mix_g, w_in, ret_norm_g, w_ret_o, q_a_norm_g, w_q_b, kv_a_norm_g, w_kv_b, w_mla_o, w_out, norm_mlp_g, w_up, w_down, norm_f_g, loss_target, m_norm_mix_g, m_w_in, m_ret_norm_g, m_w_ret_o, m_q_a_norm_g, m_w_q_b, m_kv_a_norm_g, m_w_kv_b, m_w_mla_o, m_w_out, m_norm_mlp_g, m_w_up, m_w_down, m_norm_f_g, v_norm_mix_g, v_w_in, v_ret_norm_g, v_w_ret_o, v_q_a_norm_g, v_w_q_b, v_kv_a_norm_g, v_w_kv_b, v_w_mla_o, v_w_out, v_norm_mlp_g, v_w_up, v_w_down, v_norm_f_g):
    xs, tgt, pos = x[0], loss_target[0], positions[0]
    s, d = xs.shape
    mats = {"w_in": w_in[0], "w_ret_o": w_ret_o[0], "w_q_b": w_q_b[0], "w_kv_b": w_kv_b[0], "w_mla_o": w_mla_o[0],
            "w_out": w_out[0], "w_up": w_up[0], "w_down": w_down[0]}
    mat_w = {"w_in": w_in, "w_ret_o": w_ret_o, "w_q_b": w_q_b, "w_kv_b": w_kv_b, "w_mla_o": w_mla_o, "w_out": w_out,
             "w_up": w_up, "w_down": w_down}
    mat_m = {"w_in": m_w_in, "w_ret_o": m_w_ret_o, "w_q_b": m_w_q_b, "w_kv_b": m_w_kv_b, "w_mla_o": m_w_mla_o,
             "w_out": m_w_out, "w_up": m_w_up, "w_down": m_w_down}
    mat_v = {"w_in": v_w_in, "w_ret_o": v_w_ret_o, "w_q_b": v_w_q_b, "w_kv_b": v_w_kv_b, "w_mla_o": v_w_mla_o,
             "w_out": v_w_out, "w_up": v_w_up, "w_down": v_w_down}
    names = list(mats)
    col_sharded = ("w_in", "w_q_b", "w_kv_b", "w_up")
    vw = ret_norm_g.shape[1]
    mla_w = mats["w_mla_o"].shape[0] * N_DEV
    ql, kl = q_a_norm_g.shape[1], kv_a_norm_g.shape[1]
    n_in = mats["w_in"].shape[1] * N_DEV
    qw = (n_in - 2 * vw - ql - kl - ROPE - 2 * d) // 2
    lay = _make_layout(d, vw, qw, ql, kl, mla_w)
    assert lay["n_in"] == n_in
    heads_r, heads_m = lay["ret_heads"], lay["mla_heads"]

    shard16 = {n: mats[n].astype(BF16) for n in names}
    with_in_proj = ("w_ret_o", "w_q_b", "w_kv_b", "w_mla_o", "w_out")
    mlp = ("w_up", "w_down")
    by_device = ("w_up", "w_kv_b")
    full = {}

    def keep(group, gathered):
        for n, g in zip(group, gathered):
            if n not in by_device:
                g = _cols_to_full(g) if n in col_sharded else g.reshape(-1, g.shape[2])
            full[n] = g

    w_mine = _w_in_to_mine(_exchange_alone("gather_w_in", _gather_exchange([shard16["w_in"]]))[0], lay)

    c64, s64 = _rope_tables(pos, RET_QK // 2)
    cos_r = jnp.concatenate([c64, c64], axis=1)
    sin_r = jnp.concatenate([-s64, s64], axis=1)
    c32, s32 = _rope_tables(pos, ROPE // 2)
    z32, z64 = jnp.zeros_like(c32), jnp.zeros((s, LANES - ROPE), F32)
    cos_p = jnp.concatenate([c32, c32, z64], axis=1)
    sin_a = jnp.concatenate([-s32, z32, z64], axis=1)
    sin_b = jnp.concatenate([z32, s32, z64], axis=1)
    lg = jnp.log(1.0 - 2.0 ** (-5.0 - jnp.arange(heads_r, dtype=F32)))
    lgs = jnp.broadcast_to(lg[:, None, None], (heads_r, 8, LANES))

    tm = min(256, s)
    blk = min(512, s)
    t_att = min(512, s)

    u = _rms_fwd("norm_mix", xs, norm_mix_g, tm)
    proj, gathered = _mm("in_proj", u, w_mine, "nn", F32,
                         ride=_gather_exchange([shard16[n] for n in with_in_proj]))
    keep(with_in_proj, gathered)
    wq_pad = jnp.pad(full["w_q_b"].reshape(ql, heads_m, NOPE + ROPE),
                     ((0, 0), (0, 0), (0, QPAD - NOPE - ROPE))).reshape(ql, heads_m * QPAD)
    o_ret, states = _ret_fwd(proj, lay, cos_r, sin_r, lgs, blk)
    ry = _ret_post(proj, lay, o_ret, ret_norm_g, tm)
    y_ret = _mm("ret_out", ry, full["w_ret_o"], "nn", F32)
    cqn, ckvn, kpr = _mla_prep(proj, lay, q_a_norm_g, kv_a_norm_g, cos_p, sin_a, sin_b, tm)
    qf = _mm("q_up", cqn, wq_pad, "nn", BF16, extras=((cos_p, None), (sin_a, None), (sin_b, None)), epilogue=_q_operand)
    kv, kf = _mm("kv_up", ckvn, full["w_kv_b"], "nn", (BF16, BF16), b_by_device=True, extras=((kpr, None),),
                 epilogue=_k_operand)
    o_mla, lse, gathered = _attn_fwd(qf, kf, kv, lay, t_att, ride=_gather_exchange([shard16["w_up"]]))
    keep(("w_up",), gathered)
    gate_tile = _tile(d, 1024)
    y_mla, merged = _mm(
        "mla_out", o_mla, full["w_mla_o"], "nn", (F32, BF16), tm=512, tn=gate_tile,
        extras=((proj, lay["off"]["g_ret"] // gate_tile), (proj, lay["off"]["g_mla"] // gate_tile), y_ret),
        epilogue=lambda r, gr, gm, yr: (r, _sig(gr) * yr + _sig(gm) * r))
    h1, n2 = _mm("out_proj", merged, full["w_out"], "nn", (F32, BF16), tm=512, tn=d,
                 extras=(xs, (norm_mlp_g, "whole")), epilogue=_residual_norm)
    (z, act), gathered = _mm("mlp_up", n2, full["w_up"], "nn", (F32, BF16), b_by_device=True,
                             epilogue=lambda r: (r, jnp.square(jnp.maximum(r, 0.0))),
                             ride=_gather_exchange([shard16["w_down"]]))
    keep(("w_down",), gathered)
    dn = _mm("mlp_down", act, full["w_down"], "nn", F32)
    dh2, dh2_16, g_norm_f, loss_part = _final("loss_head", h1, dn, norm_f_g.reshape(1, d), tgt, tm)

    mx, my, mc = _place()
    c_arr = jnp.reshape(mc, (1,)).astype(jnp.int32)
    chip_arr = jnp.reshape(2 * mx + my, (1,)).astype(jnp.int32)
    sums, from_chips = {}, {}

    def blocks(group, grads):
        return [g if n in by_device else (_full_to_cols(g) if n in col_sharded else g.reshape((N_DEV,) + mats[n].shape))
                for n, g in zip(group, grads)]

    def pair_sums(group, mine, from_sibling):
        for n, g, r in zip(group, mine, from_sibling):
            sums[n] = _pair_sum("pair_sum_" + n, g, r, c_arr, 256)
        return [sums[n] for n in group]

    dz = _mm("mlp_down_dx", dh2_16, full["w_down"], "nt", BF16, extras=(z,),
             epilogue=lambda r, zz: (r * (2.0 * jnp.maximum(zz, 0.0)),))
    g_w_down = _mm("mlp_down_dw", act, dh2_16, "tn", BF16, tm=512, tn=d, tk=s)
    down_blocks = blocks(("w_down",), (g_w_down,))
    g_w_up, got_down = _mm("mlp_up_dw", n2, dz, "tn", BF16, tk=s, out_by_device=True,
                           ride=_sibling_exchange(down_blocks))
    dn2, got_up = _mm("mlp_up_dx", dz, full["w_up"], "nt", F32, b_by_device=True, ride=_sibling_exchange([g_w_up]))
    mlp_sums = pair_sums(mlp, [g_w_up] + down_blocks, list(got_up) + list(got_down))
    dh1, dh1_16, g_norm_mlp = _rms_bwd("norm_mlp_bwd", dn2, h1, norm_mlp_g, dh2, tm, matmul_copy=True)
    assert lay["off"]["g_ret"] == 0 and lay["off"]["g_mla"] == d
    dy_ret, dy_mla, d_proj = _mm(
        "out_proj_dx", dh1_16, full["w_out"], "nt", (BF16, BF16, (BF16, 2, lay["total"])), tm=256, tn=d,
        extras=((proj, 0), (proj, 1), y_ret, y_mla), epilogue=_gate_grads)
    g_w_out = _mm("out_proj_dw", merged, dh1_16, "tn", BF16, tm=512, tn=d, tk=s)
    g_w_ret_o = _mm("ret_out_dw", ry, dy_ret, "tn", BF16, tm=512, tn=d, tk=s)
    g_w_mla_o = _mm("mla_out_dw", o_mla, dy_mla, "tn", BF16, tm=256, tn=d, tk=s)
    mixer = ("w_out", "w_ret_o", "w_mla_o")
    mixer_blocks = blocks(mixer, (g_w_out, g_w_ret_o, g_w_mla_o))
    d_ry, got = _mm("ret_out_dx", dy_ret, full["w_ret_o"], "nt", F32, ride=_sibling_exchange(mixer_blocks))
    mixer_sums = pair_sums(mixer, mixer_blocks, got)
    d_omla = _mm("mla_out_dx", dy_mla, full["w_mla_o"], "nt", F32)
    d_oret, d_proj, g_ret_norm = _ret_post_bwd(proj, lay, d_ry, o_ret, ret_norm_g, d_proj, tm)
    d_proj, got = _ret_bwd(proj, lay, cos_r, sin_r, lgs, states, d_oret, d_proj, blk,
                           ride=_chips_exchange(mixer_sums))
    from_chips.update(zip(mixer, got))
    delta = _attn_delta(d_omla, o_mla, lay, t_att)
    dqp, dkv, dkpe_parts, got = _attn_bwd(qf, kf, kv, lse, delta, d_omla, cos_p, sin_a, sin_b, lay, t_att,
                                          ride=_chips_exchange(mlp_sums))
    from_chips.update(zip(mlp, got))
    d_proj = _rope_key_grad(dkpe_parts, lay, d_proj, tm)
    d_cqn = _mm("q_up_dx", dqp, wq_pad, "nt", F32)
    g_wq_pad = _mm("q_up_dw", cqn, dqp, "tn", BF16)
    d_ckvn = _mm("kv_up_dx", dkv, full["w_kv_b"], "nt", F32, b_by_device=True)
    g_w_kv_b = _mm("kv_up_dw", ckvn, dkv, "tn", BF16, out_by_device=True)
    d_proj, g_q_a, g_kv_a = _mla_prep_bwd(proj, lay, d_cqn, d_ckvn, q_a_norm_g, kv_a_norm_g, d_proj, tm)
    g_w_mine = _mm("in_proj_dw", u, d_proj, "tn", BF16, tk=s)
    g_w_q_b = g_wq_pad.reshape(ql, heads_m, QPAD)[:, :, :NOPE + ROPE].reshape(ql, heads_m * (NOPE + ROPE))
    last = ("w_in", "w_q_b", "w_kv_b")
    last_blocks = [_mine_to_blocks(g_w_mine, lay)] + blocks(last[1:], (g_w_q_b, g_w_kv_b))
    last_sums = pair_sums(last, last_blocks, _exchange_alone("grads_to_sibling", _sibling_exchange(last_blocks)))
    du, got = _mm("in_proj_dx", d_proj, w_mine, "nt", F32, ride=_chips_exchange(last_sums))
    from_chips.update(zip(last, got))
    grad_x, g_norm_mix = _rms_bwd("norm_mix_bwd", du, xs, norm_mix_g, dh1, tm)

    upd = {n: _adamw_shard("adamw_" + n, mat_w[n], mat_m[n], mat_v[n], sums[n], from_chips[n], chip_arr, 256)
           for n in names}

    gains = [("norm_mix_g", norm_mix_g, m_norm_mix_g, v_norm_mix_g, g_norm_mix),
             ("ret_norm_g", ret_norm_g, m_ret_norm_g, v_ret_norm_g, g_ret_norm),
             ("q_a_norm_g", q_a_norm_g, m_q_a_norm_g, v_q_a_norm_g, g_q_a),
             ("kv_a_norm_g", kv_a_norm_g, m_kv_a_norm_g, v_kv_a_norm_g, g_kv_a),
             ("norm_mlp_g", norm_mlp_g, m_norm_mlp_g, v_norm_mlp_g, g_norm_mlp),
             ("norm_f_g", norm_f_g, m_norm_f_g, v_norm_f_g, g_norm_f)]
    n_rows = sum(g[1].size for g in gains) // LANES
    pad_rows = -(-(n_rows + 1) // 8) * 8 - n_rows
    tail = jnp.zeros((pad_rows, LANES), F32)
    part = jnp.concatenate([_pack_rows([g[4] for g in gains]),
                            jnp.broadcast_to(loss_part[:, :1], (1, LANES)), tail[1:]], axis=0)
    packed = [jnp.concatenate([_pack_rows([g[k] for g in gains]), tail], axis=0) for k in (1, 2, 3)]
    g_sm, d_sm, m_sm, v_sm = _small_all_reduce_adam(part, *packed)
    loss = g_sm[n_rows, 0]
    small = {}
    o = 0
    for name, w, _, _, _ in gains:
        r = w.size // LANES
        small[name] = [a[o:o + r].reshape(w.shape) for a in (g_sm, d_sm, m_sm, v_sm)]
        o += r

    order = ["norm_mix_g", "w_in", "ret_norm_g", "w_ret_o", "q_a_norm_g", "w_q_b", "kv_a_norm_g", "w_kv_b", "w_mla_o",
             "w_out", "norm_mlp_g", "w_up", "w_down", "norm_f_g"]
    outs = [loss, grad_x[None]]
    for k in range(4):
        for n in order:
            outs.append(small[n][k] if n in small else upd[n][k])
    return tuple(outs)
```

```python
import functools
import math

import jax
import jax.numpy as jnp
from jax import lax
from jax.experimental import pallas as pl
from jax.experimental.pallas import tpu as pltpu

F32 = jnp.float32
BF16 = jnp.bfloat16
MESH = pl.DeviceIdType.MESH

EPS = 1e-6
ROPE_THETA = 10000.0
CHUNK_SHIFT = 6
RET_QK = 128
RET_V = 256
NOPE = 128
ROPE = 64
VHEAD = 128
QPAD = 256
LANES = 128
N_DEV = 8
VMEM_LIMIT = 56 * 1024 * 1024

ADAM_LR = 0.001
ADAM_B1 = 0.9
ADAM_B2 = 0.999
ADAM_EPS = 1e-08
ADAM_WD = 0.01
ADAM_STEP = 10

NN = (((1,), (0,)), ((), ()))
NT = (((1,), (1,)), ((), ()))
TN = (((0,), (0,)), ((), ()))


def _dot(a, b, dims):
    return lax.dot_general(a.astype(BF16), b.astype(BF16), dims, preferred_element_type=F32)


def _tile(dim, pref):
    if dim <= pref:
        return dim
    t = (pref // LANES) * LANES
    while t >= LANES:
        if dim % t == 0:
            return t
        t -= LANES
    raise ValueError(f"no tile for {dim}")


def _params(sem):
    return pltpu.CompilerParams(dimension_semantics=sem, vmem_limit_bytes=VMEM_LIMIT)


def _sig(v):
    return 1.0 / (1.0 + jnp.exp(-v))


def _mm(name, a, b, mode, out_dtypes, *, tm=1024, tn=1024, tk=2048, extras=(), epilogue=None, ride=None,
        b_by_device=False, out_by_device=False):
    if b_by_device:
        b_cols = b.shape[2]
        b_shape = (b.shape[1], N_DEV * b_cols)
    else:
        b_shape = b.shape
    if mode == "nn":
        (m, k), (_, n) = a.shape, b_shape
    elif mode == "nt":
        (m, k), (n, _) = a.shape, b_shape
    else:
        (k, m), (_, n) = a.shape, b_shape
    tm, tn, tk = _tile(m, tm), _tile(n, tn), _tile(k, tk)
    if b_by_device and mode != "nt":
        tn = _tile(b_cols, tn)
    if out_by_device:
        tn = _tile(n // N_DEV, tn)
    nk = k // tk
    dims = {"nn": NN, "nt": NT, "tn": TN}[mode]
    a_spec = (pl.BlockSpec((tk, tm), lambda i, j, kk: (kk, i)) if mode == "tn"
              else pl.BlockSpec((tm, tk), lambda i, j, kk: (i, kk)))
    if b_by_device and mode == "nt":
        piece = min(tk, b_cols)
        n_b, per = tk // piece, b_cols // piece
        b_specs = [pl.BlockSpec((None, tn, piece),
                                lambda i, j, kk, p=p: ((kk * n_b + p) // per, j, (kk * n_b + p) % per))
                   for p in range(n_b)]
    elif b_by_device:
        per = b_cols // tn
        n_b, piece = 1, tk
        b_specs = [pl.BlockSpec((None, tk, tn), lambda i, j, kk: (j // per, kk, j % per))]
    else:
        n_b, piece = 1, tk
        b_specs = [pl.BlockSpec((tn, tk), lambda i, j, kk: (j, kk)) if mode == "nt"
                   else pl.BlockSpec((tk, tn), lambda i, j, kk: (kk, j))]
    tile_spec = pl.BlockSpec((tm, tn), lambda i, j, kk: (i, j))
    if out_by_device:
        per_out = n // N_DEV // tn
        out_spec = pl.BlockSpec((None, tm, tn), lambda i, j, kk: (j // per_out, i, j % per_out))
        out_dims = (N_DEV, m, n // N_DEV)
    else:
        out_spec, out_dims = tile_spec, (m, n)
    ex_arrays, ex_specs = [], []
    for e in extras:
        arr, off = e if isinstance(e, tuple) else (e, 0)
        ex_arrays.append(arr)
        if off == "whole":
            ex_specs.append(pl.BlockSpec(arr.shape, lambda i, j, kk, nd=arr.ndim: (0,) * nd))
        elif off is None:
            ex_specs.append(pl.BlockSpec((tm, arr.shape[1]), lambda i, j, kk: (i, 0)))
        else:
            ex_specs.append(pl.BlockSpec((tm, tn), lambda i, j, kk, off=off: (i, j + off)))
    n_ex = len(extras)
    single = not isinstance(out_dtypes, (tuple, list))
    dts = (out_dtypes,) if single else tuple(out_dtypes)
    out_specs, out_shapes = [], []
    for dt in dts:
        if isinstance(dt, tuple):
            dt, mult, width = dt
            out_specs.append(pl.BlockSpec((tm, mult * tn), lambda i, j, kk: (i, j)))
            out_shapes.append(jax.ShapeDtypeStruct((m, width), dt))
        else:
            out_specs.append(out_spec)
            out_shapes.append(jax.ShapeDtypeStruct(out_dims, dt))

    grid = (m // tm, n // tn, nk)
    r_in, r_out, r_sc = ride.counts() if ride else (0, 0, 0)
    n_acc = 1 if nk > 1 else 0

    def body(a_ref, *rest):
        b_refs, rest = rest[:n_b], rest[n_b:]
        ex, rest = rest[:n_ex], rest[n_ex:]
        ride_in, rest = rest[:r_in], rest[r_in:]
        outs, rest = rest[:len(dts)], rest[len(dts):]
        ride_out, rest = rest[:r_out], rest[r_out:]
        ride_scratch = rest[n_acc:]
        if ride:
            first, mid, last = _steps([pl.program_id(d) for d in range(3)], grid)
            ride.run(ride_in, ride_out, ride_scratch, (first, mid, None))

        def finish(r):
            vals = (r,) if epilogue is None else epilogue(r, *[e[...] for e in ex])
            for o, v in zip(outs, vals):
                o[...] = v.astype(o.dtype)

        if n_b == 1:
            part = _dot(a_ref[...], b_refs[0][...], dims)
        else:
            part = sum(_dot(a_ref[:, p * piece:(p + 1) * piece], b_refs[p][...], dims) for p in range(n_b))
        if nk == 1:
            finish(part)
        else:
            acc = rest[0]
            kk = pl.program_id(2)

            @pl.when(kk == 0)
            def _():
                acc[...] = part

            @pl.when(jnp.logical_and(kk > 0, kk < nk - 1))
            def _():
                acc[...] += part

            @pl.when(kk == nk - 1)
            def _():
                finish(acc[...] + part)

        if ride:
            ride.run(ride_in, ride_out, ride_scratch, (None, None, last))

    res = pl.pallas_call(
        body, name=name, grid=grid,
        in_specs=[a_spec] + b_specs + ex_specs + [ANY] * r_in,
        out_specs=out_specs + [ANY] * r_out,
        out_shape=out_shapes + (ride.out_shape if ride else []),
        scratch_shapes=([pltpu.VMEM((tm, tn), F32)] if nk > 1 else []) + (ride.scratch if ride else []),
        compiler_params=_params(("arbitrary",) * 3 if ride else ("parallel", "parallel", "arbitrary")),
    )(a, *[b] * n_b, *ex_arrays, *(ride.ins if ride else []))
    own = res[0] if single else res[:len(dts)]
    return (own, res[len(dts):]) if ride else own


def _rows(name, body, n_rows, tm, ins, outs, accs=(), into=None):
    in_specs, args = [], []
    for t in ins:
        if len(t) == 1:
            in_specs.append(pl.BlockSpec(t[0].shape, lambda i, nd=t[0].ndim: (0,) * nd))
        else:
            in_specs.append(pl.BlockSpec((tm, t[1]), lambda i, cb=t[2]: (i, cb)))
        args.append(t[0])
    outs = [(o + (o[0], 0))[:4] for o in outs]
    out_specs = [pl.BlockSpec((tm, w), lambda i, cb=cb: (i, cb)) for w, _, _, cb in outs]
    out_shape = [jax.ShapeDtypeStruct((n_rows, total), d) for _, d, total, _ in outs]
    aliases, kernel = {}, body
    if into is not None:
        arr, w, cb = into
        in_specs.append(ANY)
        args.append(arr)
        out_specs.append(pl.BlockSpec((tm, w), lambda i: (i, cb)))
        out_shape.append(jax.ShapeDtypeStruct(arr.shape, arr.dtype))
        aliases = {len(ins): len(outs)}
        n_in = len(ins)

        def kernel(*refs):
            body(*refs[:n_in], *refs[n_in + 1:])

    out_specs += [pl.BlockSpec((r, w), lambda i: (0, 0)) for r, w in accs]
    out_shape += [jax.ShapeDtypeStruct((r, w), F32) for r, w in accs]
    return pl.pallas_call(
        kernel, name=name, grid=(n_rows // tm,), in_specs=in_specs, out_specs=out_specs, out_shape=out_shape,
        input_output_aliases=aliases, compiler_params=_params(("arbitrary",) if accs else ("parallel",)),
    )(*args)


def _zero_first(*accs):
    @pl.when(pl.program_id(0) == 0)
    def _():
        for a in accs:
            a[...] = jnp.zeros_like(a)


def _rope64(t, cos, sin):
    return t * cos + pltpu.roll(t, RET_QK // 2, 1) * sin


def _rope32(t, cos, sin_a, sin_b):
    return t * cos + pltpu.roll(t, LANES - ROPE // 2, 1) * sin_a + pltpu.roll(t, ROPE // 2, 1) * sin_b


def _rms_fwd(name, x, g, tm):
    s, d = x.shape

    def body(x_ref, g_ref, u_ref):
        v = x_ref[...]
        r = lax.rsqrt(jnp.mean(v * v, axis=-1, keepdims=True) + EPS)
        u_ref[...] = (v * r * g_ref[...]).astype(BF16)

    return _rows(name, body, s, tm, [(x, d, 0), (g,)], [(d, BF16)])[0]


def _residual_norm(r, x, g):
    h = x + r
    return h, h * lax.rsqrt(jnp.mean(h * h, axis=-1, keepdims=True) + EPS) * g


def _gate_grads(dm, gr, gm, yr, ym):
    sr, sm = _sig(gr), _sig(gm)
    return dm * sr, dm * sm, jnp.concatenate([dm * yr * (sr * (1.0 - sr)), dm * ym * (sm * (1.0 - sm))], axis=1)


def _rms_bwd(name, dy, x, g, dres, tm, matmul_copy=False):
    s, d = x.shape

    def body(dy_ref, x_ref, g_ref, dres_ref, dx_ref, *rest):
        dg_ref = rest[-1]
        _zero_first(dg_ref)
        v, dyv = x_ref[...], dy_ref[...]
        r = lax.rsqrt(jnp.mean(v * v, axis=-1, keepdims=True) + EPS)
        xh = v * r
        dxh = dyv * g_ref[...]
        dx = dres_ref[...] + r * (dxh - xh * jnp.mean(dxh * xh, axis=-1, keepdims=True))
        dx_ref[...] = dx
        if matmul_copy:
            rest[0][...] = dx.astype(BF16)
        dg_ref[...] += jnp.sum(dyv * xh, axis=0, keepdims=True)

    return _rows(name, body, s, tm, [(dy, d, 0), (x, d, 0), (g,), (dres, d, 0)],
                 [(d, F32)] + [(d, BF16)] * matmul_copy, [(1, d)])


def _final(name, h1, dn, g, tgt, tm):
    s, d = h1.shape

    def body(h_ref, dn_ref, g_ref, t_ref, dh_ref, dh16_ref, dg_ref, loss_ref):
        _zero_first(dg_ref, loss_ref)
        v = h_ref[...] + dn_ref[...]
        r = lax.rsqrt(jnp.mean(v * v, axis=-1, keepdims=True) + EPS)
        xh = v * r
        gv = g_ref[...]
        e = xh * gv - t_ref[...]
        loss_ref[...] += 0.5 * jnp.sum(jnp.mean(e * e, axis=-1, keepdims=True))
        dy = e * (1.0 / d)
        dg_ref[...] += jnp.sum(dy * xh, axis=0, keepdims=True)
        dxh = dy * gv
        dh = r * (dxh - xh * jnp.mean(dxh * xh, axis=-1, keepdims=True))
        dh_ref[...] = dh
        dh16_ref[...] = dh.astype(BF16)

    return _rows(name, body, s, tm, [(h1, d, 0), (dn, d, 0), (g,), (tgt, d, 0)], [(d, F32), (d, BF16)],
                 [(1, d), (1, LANES)])


def _decay_mask(lg, blk):
    n = lax.broadcasted_iota(jnp.int32, (blk, blk), 0)
    m = lax.broadcasted_iota(jnp.int32, (blk, blk), 1)
    w = jnp.exp(lg * jnp.abs(n - m).astype(F32))
    return jnp.where(jnp.right_shift(m, CHUNK_SHIFT) <= jnp.right_shift(n, CHUNK_SHIFT), w, 0.0)


def _decays(lg, blk):
    pos = lax.broadcasted_iota(jnp.int32, (blk, 1), 0).astype(F32)
    return jnp.exp(lg * (pos + 1.0)), jnp.exp(lg * (blk - 1.0 - pos)), jnp.exp(lg * float(blk))


def _ret_fwd(proj, lay, cos, sin, lgs, gain, blk, ride=None):
    s = proj.shape[0]
    heads = lay["ret_heads"]
    nb = s // blk
    scale = RET_QK ** -0.5
    ride_in_specs, ride_ins, ride_out_specs, ride_out_shape, ride_scratch = _ride_args(ride)

    def body(lg_ref, qkv_ref, cos_ref, sin_ref, rg_ref, g_ref, o_ref, st_ref, ry_ref, state, mask):
        lg = lg_ref[0:1, 0:1]

        @pl.when(pl.program_id(1) == 0)
        def _():
            state[...] = jnp.zeros_like(state)
            mask[...] = _decay_mask(lg, blk)

        a, c, gb = _decays(lg, blk)
        q = _rope64(qkv_ref[:, :RET_QK], cos_ref[...], sin_ref[...])
        k = _rope64(qkv_ref[:, RET_QK:2 * RET_QK], cos_ref[...], sin_ref[...]) * scale
        v = qkv_ref[:, 2 * RET_QK:]
        st = state[...]
        st_ref[...] = st
        sm = _dot(q, k, NT) * mask[...]
        o = _dot(sm, v, NN) + _dot(q * a, st, NN)
        o_ref[...] = o
        state[...] = st * gb + _dot(k * c, v, TN)
        dlt = o - jnp.mean(o, axis=-1, keepdims=True)
        rstd = lax.rsqrt(jnp.mean(dlt * dlt, axis=-1, keepdims=True) + EPS)
        rg = rg_ref[...]
        ry_ref[...] = (dlt * rstd * g_ref[...] * (rg * _sig(rg))).astype(BF16)

    first = lay["off"]["heads"] // RET_HEAD
    gate = lay["off"]["r_g"] // RET_V
    res = pl.pallas_call(
        _with_ride(body, ride, (heads, nb), 0, 6, 3), name="ret_fwd", grid=(heads, nb),
        in_specs=[pl.BlockSpec((None, 8, LANES), lambda h, b: (h, 0, 0)),
                  pl.BlockSpec((blk, RET_HEAD), lambda h, b: (b, first + h)),
                  pl.BlockSpec((blk, LANES), lambda h, b: (b, 0)),
                  pl.BlockSpec((blk, LANES), lambda h, b: (b, 0)),
                  pl.BlockSpec((blk, RET_V), lambda h, b: (b, gate + h)),
                  pl.BlockSpec((1, RET_V), lambda h, b: (0, h))] + ride_in_specs,
        out_specs=[pl.BlockSpec((blk, RET_V), lambda h, b: (b, h)),
                   pl.BlockSpec((None, None, RET_QK, RET_V), lambda h, b: (h, b, 0, 0)),
                   pl.BlockSpec((blk, RET_V), lambda h, b: (b, h))] + ride_out_specs,
        out_shape=[jax.ShapeDtypeStruct((s, heads * RET_V), F32),
                   jax.ShapeDtypeStruct((heads, nb, RET_QK, RET_V), F32),
                   jax.ShapeDtypeStruct((s, heads * RET_V), BF16)] + ride_out_shape,
        scratch_shapes=[pltpu.VMEM((RET_QK, RET_V), F32), pltpu.VMEM((blk, blk), F32)] + ride_scratch,
        compiler_params=_params(("arbitrary", "arbitrary") if ride else ("parallel", "arbitrary")),
    )(lgs, proj, cos, sin, proj, gain, *ride_ins)
    return (res[0], res[1], res[2], res[3:]) if ride else res


def _ret_bwd(proj, lay, cos, sin, lgs, states, d_o, d_proj, blk, ride=None):
    ride_in_specs, ride_ins, ride_out_specs, ride_out_shape, ride_scratch = _ride_args(ride)
    s = proj.shape[0]
    heads = lay["ret_heads"]
    nb = s // blk
    scale = RET_QK ** -0.5

    def body(lg_ref, qkv_ref, cos_ref, sin_ref, st_ref, do_ref, _, dqkv_ref, dstate, mask):
        lg = lg_ref[0:1, 0:1]

        @pl.when(pl.program_id(1) == 0)
        def _():
            dstate[...] = jnp.zeros_like(dstate)
            mask[...] = _decay_mask(lg, blk)

        a, c, gb = _decays(lg, blk)
        cs, sn = cos_ref[...], sin_ref[...]
        q = _rope64(qkv_ref[:, :RET_QK], cs, sn)
        k = _rope64(qkv_ref[:, RET_QK:2 * RET_QK], cs, sn) * scale
        v = qkv_ref[:, 2 * RET_QK:]
        st = st_ref[...]
        do = do_ref[...]
        dst = dstate[...]
        mk = mask[...]
        sm = _dot(q, k, NT) * mk
        ds = _dot(do, v, NT) * mk
        dq = _dot(ds, k, NN) + _dot(do, st, NT) * a
        dk = _dot(ds, q, TN) + _dot(v, dst, NT) * c
        dqkv_ref[:, 2 * RET_QK:] = (_dot(sm, do, TN) + _dot(k * c, dst, NN)).astype(BF16)
        dstate[...] = dst * gb + _dot(q * a, do, TN)
        dqkv_ref[:, :RET_QK] = _rope64(dq, cs, -sn).astype(BF16)
        dqkv_ref[:, RET_QK:2 * RET_QK] = (_rope64(dk, cs, -sn) * scale).astype(BF16)

    first = lay["off"]["heads"] // RET_HEAD
    last = nb - 1
    res = pl.pallas_call(
        _with_ride(body, ride, (heads, nb), 0, 7, 1), name="ret_bwd", grid=(heads, nb),
        in_specs=[pl.BlockSpec((None, 8, LANES), lambda h, b: (h, 0, 0)),
                  pl.BlockSpec((blk, RET_HEAD), lambda h, b: (last - b, first + h)),
                  pl.BlockSpec((blk, LANES), lambda h, b: (last - b, 0)),
                  pl.BlockSpec((blk, LANES), lambda h, b: (last - b, 0)),
                  pl.BlockSpec((None, None, RET_QK, RET_V), lambda h, b: (h, last - b, 0, 0)),
                  pl.BlockSpec((blk, RET_V), lambda h, b: (last - b, h)), ANY] + ride_in_specs,
        out_specs=[pl.BlockSpec((blk, RET_HEAD), lambda h, b: (last - b, first + h))] + ride_out_specs,
        out_shape=[jax.ShapeDtypeStruct(d_proj.shape, d_proj.dtype)] + ride_out_shape,
        scratch_shapes=[pltpu.VMEM((RET_QK, RET_V), F32), pltpu.VMEM((blk, blk), F32)] + ride_scratch,
        input_output_aliases={6: 0},
        compiler_params=_params(("arbitrary", "arbitrary") if ride else ("parallel", "arbitrary")),
    )(lgs, proj, cos, sin, states, d_o, d_proj, *ride_ins)
    return (res[0], res[1:]) if ride else res[0]


def _ret_post_bwd(proj, lay, d_ry, o, g, d_proj, tm):
    s, vw = o.shape
    heads = lay["ret_heads"]

    def body(dry_ref, o_ref, rg_ref, g_ref, do_ref, drg_ref, dg_ref):
        _zero_first(dg_ref)
        for h in range(heads):
            sl = slice(h * RET_V, (h + 1) * RET_V)
            oh = o_ref[:, sl]
            dlt = oh - jnp.mean(oh, axis=-1, keepdims=True)
            rstd = lax.rsqrt(jnp.mean(dlt * dlt, axis=-1, keepdims=True) + EPS)
            oh = dlt * rstd
            gv = g_ref[:, sl]
            rg = rg_ref[:, sl]
            sg = _sig(rg)
            dry = dry_ref[:, sl]
            dt = dry * (rg * sg)
            drg_ref[:, sl] = (dry * (oh * gv) * (sg * (1.0 + rg * (1.0 - sg)))).astype(BF16)
            dg_ref[:, sl] += jnp.sum(dt * oh, axis=0, keepdims=True)
            doh = dt * gv
            do_ref[:, sl] = rstd * (doh - jnp.mean(doh, axis=-1, keepdims=True)
                                    - oh * jnp.mean(doh * oh, axis=-1, keepdims=True))

    return _rows("ret_post_bwd", body, s, tm,
                 [(d_ry, vw, 0), (o, vw, 0), (proj, vw, lay["off"]["r_g"] // vw), (g,)],
                 [(vw, F32)], [(1, vw)], into=(d_proj, vw, lay["off"]["r_g"] // vw))


def _mla_prep(proj, lay, gq, gkv, cos, sin_a, sin_b, tm):
    s = proj.shape[0]
    ql, kl = lay["q_lora"], lay["kv_lora"]

    def body(cq_ref, ckv_ref, kpe_ref, gq_ref, gkv_ref, cos_ref, sa_ref, sb_ref, cqn_ref, ckvn_ref, kpr_ref):
        for src, gref, dst in ((cq_ref, gq_ref, cqn_ref), (ckv_ref, gkv_ref, ckvn_ref)):
            v = src[...]
            r = lax.rsqrt(jnp.mean(v * v, axis=-1, keepdims=True) + EPS)
            dst[...] = (v * r * gref[...]).astype(BF16)
        kpr_ref[...] = _rope32(kpe_ref[...], cos_ref[...], sa_ref[...], sb_ref[...]).astype(BF16)

    off = lay["off"]
    return _rows("mla_prep", body, s, tm,
                 [(proj, ql, off["c_q"] // ql), (proj, kl, off["c_kv"] // kl), (proj, LANES, off["k_pe"] // LANES),
                  (gq,), (gkv,), (cos, LANES, 0), (sin_a, LANES, 0), (sin_b, LANES, 0)],
                 [(ql, BF16), (kl, BF16), (LANES, BF16)])


def _mla_prep_bwd(proj, lay, d_cqn, d_ckvn, gq, gkv, d_proj, tm):
    s = proj.shape[0]
    ql, kl = lay["q_lora"], lay["kv_lora"]

    def body(dq_ref, dkv_ref, cq_ref, ckv_ref, gq_ref, gkv_ref, dc_ref, dgq_ref, dgkv_ref):
        _zero_first(dgq_ref, dgkv_ref)
        for dref, src, gref, cols, dg in ((dq_ref, cq_ref, gq_ref, slice(0, ql), dgq_ref),
                                          (dkv_ref, ckv_ref, gkv_ref, slice(ql, ql + kl), dgkv_ref)):
            v, dy = src[...], dref[...]
            r = lax.rsqrt(jnp.mean(v * v, axis=-1, keepdims=True) + EPS)
            xh = v * r
            dxh = dy * gref[...]
            dc_ref[:, cols] = (r * (dxh - xh * jnp.mean(dxh * xh, axis=-1, keepdims=True))).astype(BF16)
            dg[...] += jnp.sum(dy * xh, axis=0, keepdims=True)

    off = lay["off"]
    return _rows("mla_prep_bwd", body, s, tm,
                 [(d_cqn, ql, 0), (d_ckvn, kl, 0), (proj, ql, off["c_q"] // ql), (proj, kl, off["c_kv"] // kl),
                  (gq,), (gkv,)],
                 [], [(1, ql), (1, kl)], into=(d_proj, ql + kl, off["c_q"] // (ql + kl)))


def _q_operand(r, cos, sin_a, sin_b):
    qs = (NOPE + ROPE) ** -0.5 * math.log2(math.e)
    cs, sa, sb = cos * qs, sin_a * qs, sin_b * qs
    parts = []
    for lo in range(0, r.shape[1], QPAD):
        parts += [r[:, lo:lo + NOPE] * qs, _rope32(r[:, lo + NOPE:lo + QPAD], cs, sa, sb)]
    return (jnp.concatenate(parts, axis=1),)


def _k_operand(r, kpr):
    parts = []
    for lo in range(0, r.shape[1], QPAD):
        parts += [r[:, lo:lo + NOPE], kpr.astype(F32)]
    return r, jnp.concatenate(parts, axis=1)


def _rope_key_grad(parts, lay, d_proj, tm):
    s, w = parts.shape

    def body(p_ref, dkpe_ref):
        dkpe_ref[:, :LANES] = sum(p_ref[:, lo:lo + LANES] for lo in range(0, w, LANES)).astype(BF16)
        dkpe_ref[:, LANES:] = jnp.zeros((tm, LANES), BF16)

    return _rows("rope_key_grad", body, s, tm, [(parts, w, 0)], [],
                 into=(d_proj, 2 * LANES, lay["off"]["k_pe"] // (2 * LANES)))[0]


def _diag_mask(t, keys_on_rows=False):
    row = lax.broadcasted_iota(jnp.int32, (t, t), 0)
    col = lax.broadcasted_iota(jnp.int32, (t, t), 1)
    key, query = (row, col) if keys_on_rows else (col, row)
    return jnp.right_shift(key, CHUNK_SHIFT) <= jnp.right_shift(query, CHUNK_SHIFT)


def _tile_pairs(nt, by_key):
    if by_key:
        pairs = [(i, j) for j in range(nt) for i in range(j, nt)]
    else:
        pairs = [(i, j) for i in range(nt) for j in range(i + 1)]
    return (jnp.asarray([p[0] for p in pairs], jnp.int32), jnp.asarray([p[1] for p in pairs], jnp.int32))


def _head_block(heads):
    return 4 if heads % 4 == 0 else 2 if heads % 2 == 0 else 1


def _attn_fwd(qf, kf, kv, lay, t, ride=None):
    s = qf.shape[0]
    heads = lay["mla_heads"]
    hb = _head_block(heads)
    nt = s // t
    qi, kj = _tile_pairs(nt, False)
    grid = (heads // hb, int(qi.shape[0]))
    ride_in_specs, ride_ins, ride_out_specs, ride_out_shape, ride_scratch = _ride_args(ride)

    def body(qi_ref, kj_ref, q_ref, k_ref, kv_ref, o_ref, lse_ref, m_s, l_s, acc):
        p = pl.program_id(1)
        i, j = qi_ref[p], kj_ref[p]

        @pl.when(j == 0)
        def _():
            m_s[...] = jnp.full_like(m_s, -jnp.inf)
            l_s[...] = jnp.zeros_like(l_s)
            acc[...] = jnp.zeros_like(acc)

        def step(diagonal):
            ones = jnp.ones((t, LANES), BF16)
            scores = [_dot(q_ref[:, hh * QPAD:(hh + 1) * QPAD], k_ref[:, hh * QPAD:(hh + 1) * QPAD], NT)
                      for hh in range(hb)]
            for hh in range(hb):
                sc = scores[hh]
                if diagonal:
                    sc = jnp.where(_diag_mask(t), sc, -jnp.inf)
                cols = [sc[:, c * LANES:(c + 1) * LANES] for c in range(t // LANES)]
                m_old = m_s[hh]
                m_new = jnp.maximum(m_old, jnp.max(functools.reduce(jnp.maximum, cols), axis=-1, keepdims=True))
                alpha = jnp.exp2(m_old - m_new)
                pr = jnp.concatenate([jnp.exp2(c - m_new).astype(BF16) for c in cols], axis=1)
                pv = _dot(pr, jnp.concatenate([kv_ref[:, hh * QPAD + NOPE:(hh + 1) * QPAD], ones], axis=1), NN)
                l_new = alpha * l_s[hh] + pv[:, VHEAD:]
                a_new = alpha * acc[hh] + pv[:, :VHEAD]
                if diagonal:
                    o_ref[:, hh * VHEAD:(hh + 1) * VHEAD] = a_new / l_new
                    lse_ref[hh] = jnp.transpose(m_new + jnp.log2(l_new))[:1]
                else:
                    m_s[hh], l_s[hh], acc[hh] = m_new, l_new, a_new

        pl.when(j < i)(functools.partial(step, False))
        pl.when(j == i)(functools.partial(step, True))

    res = pl.pallas_call(
        _with_ride(body, ride, grid, 2, 3, 2), name="attn_fwd",
        grid_spec=pltpu.PrefetchScalarGridSpec(
            num_scalar_prefetch=2, grid=grid,
            in_specs=[pl.BlockSpec((t, hb * QPAD), lambda h, p, qi, kj: (qi[p], h)),
                      pl.BlockSpec((t, hb * QPAD), lambda h, p, qi, kj: (kj[p], h)),
                      pl.BlockSpec((t, hb * QPAD), lambda h, p, qi, kj: (kj[p], h))] + ride_in_specs,
            out_specs=[pl.BlockSpec((t, hb * VHEAD), lambda h, p, qi, kj: (qi[p], h)),
                       pl.BlockSpec((hb, 1, t), lambda h, p, qi, kj: (h, 0, qi[p]))] + ride_out_specs,
            scratch_shapes=[pltpu.VMEM((hb, t, LANES), F32), pltpu.VMEM((hb, t, LANES), F32),
                            pltpu.VMEM((hb, t, VHEAD), F32)] + ride_scratch),
        out_shape=[jax.ShapeDtypeStruct((s, heads * VHEAD), F32),
                   jax.ShapeDtypeStruct((heads, 1, s), F32)] + ride_out_shape,
        compiler_params=_params(("arbitrary", "arbitrary") if ride else ("parallel", "arbitrary")),
    )(qi, kj, qf, kf, kv, *ride_ins)
    return (res[0], res[1], res[2:]) if ride else res


def _attn_delta(d_o, o, lay, tm):
    s = o.shape[0]
    heads = lay["mla_heads"]

    def body(do_ref, o_ref, dl_ref):
        for h in range(heads):
            sl = slice(h * VHEAD, (h + 1) * VHEAD)
            dl_ref[h] = jnp.sum(jnp.transpose(do_ref[:, sl] * o_ref[:, sl]), axis=0, keepdims=True)

    tile = pl.BlockSpec((tm, heads * VHEAD), lambda i: (i, 0))
    return pl.pallas_call(
        body, name="attn_delta", grid=(s // tm,), in_specs=[tile, tile],
        out_specs=pl.BlockSpec((heads, 1, tm), lambda i: (0, 0, i)),
        out_shape=jax.ShapeDtypeStruct((heads, 1, s), F32),
        compiler_params=_params(("parallel",)),
    )(d_o, o)


def _attn_bwd(qf, kf, kv, lse, delta, d_o, cos, sin_a, sin_b, lay, t, ride=None):
    s = qf.shape[0]
    heads = lay["mla_heads"]
    hb = _head_block(heads)
    nt = s // t
    scale = (NOPE + ROPE) ** -0.5
    qi, kj = _tile_pairs(nt, True)
    grid = (heads // hb, int(qi.shape[0]))
    ride_in_specs, ride_ins, ride_out_specs, ride_out_shape, ride_scratch = _ride_args(ride)

    def body(qi_ref, kj_ref, q_ref, k_ref, kv_ref, lse_ref, dl_ref, do_ref, cos_ref, sa_ref, sb_ref,
             dqp_ref, dkv_ref, dkpe_ref, dq_acc, dk_acc, dv_acc):
        p = pl.program_id(1)
        i, j = qi_ref[p], kj_ref[p]
        rows = pl.ds(pl.multiple_of(i * t, t), t)

        def unrope(v):
            return _rope32(v, cos_ref[...], -sa_ref[...], -sb_ref[...])

        @pl.when(p == 0)
        def _():
            dq_acc[...] = jnp.zeros_like(dq_acc)

        def step(diagonal):
            for hh in range(hb):
                lo = hh * QPAD
                q, k = q_ref[:, lo:lo + QPAD], k_ref[:, lo:lo + QPAD]
                do = do_ref[:, hh * VHEAD:(hh + 1) * VHEAD]
                pr = jnp.exp2(_dot(k, q, NT) - lse_ref[hh])
                if diagonal:
                    pr = jnp.where(_diag_mask(t, keys_on_rows=True), pr, 0.0)
                dv_part = _dot(pr, do, NN)
                ds = (pr * (_dot(kv_ref[:, lo + NOPE:lo + QPAD], do, NT) - dl_ref[hh])).astype(BF16)
                dk_part = _dot(ds, q, NN)
                dq = dq_acc[rows, lo:lo + QPAD] + _dot(ds, k, TN) * scale
                if diagonal:
                    dqp_ref[:, lo:lo + NOPE] = dq[:, :NOPE].astype(BF16)
                    dqp_ref[:, lo + NOPE:lo + QPAD] = unrope(dq[:, NOPE:]).astype(BF16)
                    dk_acc[hh], dv_acc[hh] = dk_part, dv_part
                else:
                    dq_acc[rows, lo:lo + QPAD] = dq
                    dk_acc[hh] += dk_part
                    dv_acc[hh] += dv_part

        pl.when(i > j)(functools.partial(step, False))
        pl.when(i == j)(functools.partial(step, True))

        @pl.when(i == nt - 1)
        def _():
            kpe = jnp.zeros((t, LANES), F32)
            for hh in range(hb):
                lo = hh * QPAD
                dk = dk_acc[hh] * math.log(2.0)
                dkv_ref[:, lo:lo + NOPE] = dk[:, :NOPE].astype(BF16)
                dkv_ref[:, lo + NOPE:lo + QPAD] = dv_acc[hh].astype(BF16)
                kpe = kpe + dk[:, NOPE:]
            dkpe_ref[...] = unrope(kpe)

    table = pl.BlockSpec((t, LANES), lambda h, p, qi, kj: (kj[p], 0))
    res = pl.pallas_call(
        _with_ride(body, ride, grid, 2, 9, 3), name="attn_bwd",
        grid_spec=pltpu.PrefetchScalarGridSpec(
            num_scalar_prefetch=2, grid=grid,
            in_specs=[pl.BlockSpec((t, hb * QPAD), lambda h, p, qi, kj: (qi[p], h)),
                      pl.BlockSpec((t, hb * QPAD), lambda h, p, qi, kj: (kj[p], h)),
                      pl.BlockSpec((t, hb * QPAD), lambda h, p, qi, kj: (kj[p], h)),
                      pl.BlockSpec((hb, 1, t), lambda h, p, qi, kj: (h, 0, qi[p])),
                      pl.BlockSpec((hb, 1, t), lambda h, p, qi, kj: (h, 0, qi[p])),
                      pl.BlockSpec((t, hb * VHEAD), lambda h, p, qi, kj: (qi[p], h)),
                      table, table, table] + ride_in_specs,
            out_specs=[pl.BlockSpec((t, hb * QPAD), lambda h, p, qi, kj: (kj[p], h)),
                       pl.BlockSpec((t, hb * QPAD), lambda h, p, qi, kj: (kj[p], h)),
                       pl.BlockSpec((t, LANES), lambda h, p, qi, kj: (kj[p], h))] + ride_out_specs,
            scratch_shapes=[pltpu.VMEM((s, hb * QPAD), F32), pltpu.VMEM((hb, t, QPAD), F32),
                            pltpu.VMEM((hb, t, VHEAD), F32)] + ride_scratch),
        out_shape=[jax.ShapeDtypeStruct((s, heads * QPAD), BF16),
                   jax.ShapeDtypeStruct((s, heads * QPAD), BF16),
                   jax.ShapeDtypeStruct((s, heads // hb * LANES), F32)] + ride_out_shape,
        compiler_params=_params(("arbitrary", "arbitrary") if ride else ("parallel", "arbitrary")),
    )(qi, kj, qf, kf, kv, lse, delta, d_o, cos, sin_a, sin_b, *ride_ins)
    return (res[0], res[1], res[2], res[3:]) if ride else res


ANY = pl.BlockSpec(memory_space=pl.ANY)


def _place():
    return lax.axis_index("x"), lax.axis_index("y"), lax.axis_index("c")


def _other_chips(x, y):
    return [(1 - x, y), (x, 1 - y), (1 - x, 1 - y)]


class _Exchange:
    def __init__(self, ins, out_shape, scratch, phases):
        self.ins, self.out_shape, self.scratch, self.phases = list(ins), list(out_shape), list(scratch), phases

    def counts(self):
        return len(self.ins), len(self.out_shape), len(self.scratch)

    def run(self, r_in, r_out, r_scratch, conds):
        for cond, phase in zip(conds, self.phases):
            if phase is not None and cond is not None:
                pl.when(cond)(functools.partial(phase, r_in, r_out, r_scratch))


def _steps(ids, sizes):
    lin, total = 0, 1
    for i, n in zip(ids, sizes):
        lin, total = lin * n + i, total * n
    return lin == 0, lin == total // 2, lin == total - 1


def _ride_args(ride):
    if ride is None:
        return [], [], [], [], []
    n_in, n_out, _ = ride.counts()
    return [ANY] * n_in, ride.ins, [ANY] * n_out, ride.out_shape, ride.scratch


def _with_ride(body, ride, grid, n_prefetch, n_in, n_out):
    if ride is None:
        return body
    r_in, r_out, r_sc = ride.counts()

    def hosted(*refs):
        cuts = (n_prefetch, n_in, r_in, n_out, r_out)
        parts, pos = [], 0
        for n in cuts:
            parts.append(refs[pos:pos + n])
            pos += n
        pre, ins, ride_in, outs, ride_out = parts
        scratch, ride_scratch = refs[pos:len(refs) - r_sc], refs[len(refs) - r_sc:]
        first, mid, last = _steps([pl.program_id(d) for d in range(len(grid))], grid)
        ride.run(ride_in, ride_out, ride_scratch, (first, mid, None))
        body(*pre, *ins, *outs, *scratch)
        ride.run(ride_in, ride_out, ride_scratch, (None, None, last))

    return hosted


def _exchange_alone(name, ex):
    n_in, n_out, _ = ex.counts()

    def body(*refs):
        for phase in ex.phases:
            if phase is not None:
                phase(refs[:n_in], refs[n_in:n_in + n_out], refs[n_in + n_out:])

    return pl.pallas_call(
        body, name=name, in_specs=[ANY] * n_in, out_specs=[ANY] * n_out, out_shape=ex.out_shape,
        scratch_shapes=ex.scratch)(*ex.ins)


def _gather_exchange(shards):
    nw = len(shards)

    def parts(ins, outs, sems):
        send_sems, recv_sems, local_sems = sems
        x, y, c = _place()

        def slot(px, py, pc):
            return 4 * px + 2 * py + pc

        def copy(w, k, rows, to, src=None):
            return pltpu.make_async_remote_copy(
                src_ref=rows if src is None else src, dst_ref=rows, send_sem=send_sems.at[w, k],
                recv_sem=recv_sems.at[w, k], device_id=to, device_id_type=MESH)

        def plan(w, mine):
            side = c if mine else 1 - c
            half = shards[w].shape[0] // 2
            whole = lambda px, py: outs[w].at[slot(px, py, side)]
            top = lambda px, py: outs[w].at[slot(px, py, side), pl.ds(0, half)]
            bottom = lambda px, py: outs[w].at[slot(px, py, side), pl.ds(half, half)]
            xn, yn, sib = (1 - x, y, side), (x, 1 - y, side), (x, y, 1 - side)
            own = ins[w] if mine else None
            return [copy(w, 0, whole(x, y), sib, own), copy(w, 1, whole(x, y), xn, own),
                    copy(w, 2, whole(x, y), yn, own), copy(w, 3, top(1 - x, y), yn), copy(w, 4, bottom(x, 1 - y), xn),
                    copy(w, 5, whole(1 - x, y), sib), copy(w, 6, whole(x, 1 - y), sib),
                    copy(w, 7, top(1 - x, 1 - y), sib), copy(w, 8, bottom(1 - x, 1 - y), sib)]

        def arrivals(w):
            half = shards[w].shape[0] // 2
            at = lambda px, py, *rows: outs[w].at[(slot(px, py, c),) + rows]
            return {1: copy(w, 1, at(1 - x, y), (x, y, c)), 2: copy(w, 2, at(x, 1 - y), (x, y, c)),
                    3: copy(w, 3, at(1 - x, 1 - y, pl.ds(0, half)), (x, y, c)),
                    4: copy(w, 4, at(1 - x, 1 - y, pl.ds(half, half)), (x, y, c))}

        local = [pltpu.make_async_copy(ins[w], outs[w].at[slot(x, y, c)], local_sems.at[w]) for w in range(nw)]
        return plan, arrivals, local

    def start(ins, outs, sems):
        plan, _, local = parts(ins, outs, sems)
        for cp in local:
            cp.start()
        for w in range(nw):
            for k in (0, 1, 2):
                plan(w, True)[k].start()

    def middle(ins, outs, sems):
        plan, arrivals, _ = parts(ins, outs, sems)
        for landed, onward in ((1, (3, 5)), (2, (4, 6))):
            for w in range(nw):
                arrivals(w)[landed].wait_recv()
                for k in onward:
                    plan(w, True)[k].start()

    def finish(ins, outs, sems):
        plan, arrivals, local = parts(ins, outs, sems)
        for landed, onward in ((3, 7), (4, 8)):
            for w in range(nw):
                arrivals(w)[landed].wait_recv()
                plan(w, True)[onward].start()
        for w in range(nw):
            from_sibling = plan(w, False)
            for k in (0, 5, 6, 7, 8):
                from_sibling[k].wait_recv()
            for cp in plan(w, True):
                cp.wait_send()
        for cp in local:
            cp.wait()

    return _Exchange(
        shards, [jax.ShapeDtypeStruct((N_DEV,) + s.shape, s.dtype) for s in shards],
        [pltpu.SemaphoreType.DMA((nw, 9)), pltpu.SemaphoreType.DMA((nw, 9)), pltpu.SemaphoreType.DMA((nw,))],
        (start, middle, finish))


def _sibling_exchange(grads):
    nw = len(grads)

    def copies(ins, outs, sems):
        x, y, c = _place()
        return [pltpu.make_async_remote_copy(
            src_ref=ins[w].at[2 * p + (1 - c)], dst_ref=outs[w].at[p], send_sem=sems[0].at[w, p],
            recv_sem=sems[1].at[w, p], device_id=(x, y, 1 - c), device_id_type=MESH)
            for w in range(nw) for p in range(4)]

    def start(ins, outs, sems):
        for cp in copies(ins, outs, sems):
            cp.start()

    def finish(ins, outs, sems):
        for cp in copies(ins, outs, sems):
            cp.wait()

    return _Exchange(grads, [jax.ShapeDtypeStruct((4,) + g.shape[1:], g.dtype) for g in grads],
                     [pltpu.SemaphoreType.DMA((nw, 4)), pltpu.SemaphoreType.DMA((nw, 4))], (start, None, finish))


def _chips_exchange(sums):
    nw = len(sums)

    def copies(ins, outs, sems):
        x, y, c = _place()
        return [pltpu.make_async_remote_copy(
            src_ref=ins[w].at[2 * px + py], dst_ref=outs[w].at[k], send_sem=sems[0].at[w, k],
            recv_sem=sems[1].at[w, k], device_id=(px, py, c), device_id_type=MESH)
            for w in range(nw) for k, (px, py) in enumerate(_other_chips(x, y))]

    def start(ins, outs, sems):
        for cp in copies(ins, outs, sems):
            cp.start()

    def finish(ins, outs, sems):
        for cp in copies(ins, outs, sems):
            cp.wait()

    return _Exchange(sums, [jax.ShapeDtypeStruct((3,) + g.shape[1:], g.dtype) for g in sums],
                     [pltpu.SemaphoreType.DMA((nw, 3)), pltpu.SemaphoreType.DMA((nw, 3))], (start, None, finish))


def _pair_sum(name, g, got, c_arr, tr):
    _, rows, cols = g.shape
    tr = _tile_rows(rows, tr)

    def body(c_ref, a_ref, b_ref, o_ref):
        o_ref[...] = (a_ref[...].astype(F32) + b_ref[...].astype(F32)).astype(BF16)

    return pl.pallas_call(
        body, name=name,
        grid_spec=pltpu.PrefetchScalarGridSpec(
            num_scalar_prefetch=1, grid=(4, rows // tr),
            in_specs=[pl.BlockSpec((None, tr, cols), lambda p, r, cr: (2 * p + cr[0], r, 0)),
                      pl.BlockSpec((None, tr, cols), lambda p, r, cr: (p, r, 0))],
            out_specs=pl.BlockSpec((None, tr, cols), lambda p, r, cr: (p, r, 0))),
        out_shape=jax.ShapeDtypeStruct((4, rows, cols), BF16),
        compiler_params=_params(("parallel", "parallel")),
    )(c_arr, g, got)


def _tile_rows(rows, pref):
    t = min(rows, pref)
    while rows % t or t % 8:
        t -= 1
    return t


def _adam(w, g, m, v):
    m = ADAM_B1 * m + (1.0 - ADAM_B1) * g
    v = ADAM_B2 * v + (1.0 - ADAM_B2) * (g * g)
    m_hat = m / (1.0 - ADAM_B1 ** ADAM_STEP)
    v_hat = v / (1.0 - ADAM_B2 ** ADAM_STEP)
    return -ADAM_LR * (m_hat / (jnp.sqrt(v_hat) + ADAM_EPS) + ADAM_WD * w), m, v


def _adamw_shard(name, w, m, v, sums, got, chip_arr, tr):
    _, rows, cols = w.shape
    tr = _tile_rows(rows, tr)

    def body(p_ref, w_ref, m_ref, v_ref, s_ref, r_ref, g_ref, d_ref, nm_ref, nv_ref):
        g = s_ref[...].astype(F32)
        for k in range(3):
            g = g + r_ref[k].astype(F32)
        g_ref[...] = g
        d_ref[...], nm_ref[...], nv_ref[...] = _adam(w_ref[...], g, m_ref[...], v_ref[...])

    tile = pl.BlockSpec((None, tr, cols), lambda r, pr: (0, r, 0))
    return pl.pallas_call(
        body, name=name,
        grid_spec=pltpu.PrefetchScalarGridSpec(
            num_scalar_prefetch=1, grid=(rows // tr,),
            in_specs=[tile, tile, tile,
                      pl.BlockSpec((None, tr, cols), lambda r, pr: (pr[0], r, 0)),
                      pl.BlockSpec((3, tr, cols), lambda r, pr: (0, r, 0))],
            out_specs=[tile] * 4),
        out_shape=[jax.ShapeDtypeStruct((1, rows, cols), F32)] * 4,
        compiler_params=_params(("parallel",)),
    )(chip_arr, w, m, v, sums, got)


def _small_all_reduce_adam(part, w, m, v):
    rows = part.shape[0]

    def body(p_ref, w_ref, m_ref, v_ref, g_ref, d_ref, nm_ref, nv_ref, buf, send_sems, recv_sems):
        x, y, c = _place()
        me = 4 * x + 2 * y + c
        buf[me] = p_ref[...]
        peers = [(x, y, 1 - c)] + [(px, py, pc) for px, py in _other_chips(x, y) for pc in (c, 1 - c)]
        copies = []
        for k, peer in enumerate(peers):
            cp = pltpu.make_async_remote_copy(
                src_ref=buf.at[me], dst_ref=buf.at[me], send_sem=send_sems.at[k], recv_sem=recv_sems.at[k],
                device_id=peer, device_id_type=MESH)
            cp.start()
            copies.append(cp)
        for cp in copies:
            cp.wait()
        g = buf[0]
        for k in range(1, N_DEV):
            g = g + buf[k]
        g_ref[...] = g
        d_ref[...], nm_ref[...], nv_ref[...] = _adam(w_ref[...], g, m_ref[...], v_ref[...])

    vm = pl.BlockSpec(memory_space=pltpu.VMEM)
    return pl.pallas_call(
        body, name="gains_all_reduce_adamw",
        in_specs=[vm] * 4, out_specs=[vm] * 4,
        out_shape=[jax.ShapeDtypeStruct((rows, LANES), F32)] * 4,
        scratch_shapes=[pltpu.VMEM((N_DEV, rows, LANES), F32), pltpu.SemaphoreType.DMA((7,)),
                        pltpu.SemaphoreType.DMA((7,))],
        compiler_params=pltpu.CompilerParams(has_side_effects=True),
    )(part, w, m, v)


IN_ORDER = ("r_q", "r_k", "r_v", "r_g", "c_q", "c_kv", "k_pe", "g_ret", "g_mla")
RET_HEAD = 2 * RET_QK + RET_V


def _make_layout(d, vw, qw, ql, kl, mla_w):
    width = {"r_q": qw, "r_k": qw, "r_v": vw, "r_g": vw, "c_q": ql, "c_kv": kl, "k_pe": ROPE, "g_ret": d, "g_mla": d}
    src, o = {}, 0
    for n in IN_ORDER:
        src[n] = o
        o += width[n]
    heads = vw // RET_V
    off, pieces, o = {}, [], 0

    def put(name, w, s):
        nonlocal o
        off.setdefault(name, o)
        pieces.append((o, w, s))
        o += w

    for n in ("g_ret", "g_mla", "r_g"):
        put(n, width[n], src[n])
    for h in range(heads):
        put("heads", RET_QK, src["r_q"] + h * RET_QK)
        put("heads", RET_QK, src["r_k"] + h * RET_QK)
        put("heads", RET_V, src["r_v"] + h * RET_V)
    for n in ("c_q", "c_kv", "k_pe"):
        put(n, width[n], src[n])
    total = off["k_pe"] + 2 * LANES
    for n, blk in (("g_ret", d), ("g_mla", d), ("r_g", vw), ("heads", RET_HEAD), ("c_q", ql + kl), ("k_pe", 2 * LANES)):
        assert off[n] % blk == 0
    assert ql == kl and off["c_kv"] == off["c_q"] + ql
    return {"off": off, "pieces": pieces, "total": total, "n_in": sum(width.values()),
            "ret_heads": heads, "mla_heads": mla_w // VHEAD, "q_lora": ql, "kv_lora": kl}


def _cols_to_full(g):
    n, r, c = g.shape
    return jnp.transpose(g, (1, 0, 2)).reshape(r, n * c)


def _full_to_cols(w):
    r, c = w.shape
    return jnp.transpose(w.reshape(r, N_DEV, c // N_DEV), (1, 0, 2))


def _w_in_to_mine(g, lay):
    _, rows, cols = g.shape
    parts, at = [], 0
    for o, w, s in lay["pieces"]:
        if o > at:
            parts.append(jnp.zeros((rows, o - at), g.dtype))
        while w > 0:
            k, a = divmod(s, cols)
            take = min(w, cols - a)
            parts.append(g[k, :, a:a + take])
            s, w, o = s + take, w - take, o + take
        at = o
    parts.append(jnp.zeros((rows, lay["total"] - at), g.dtype))
    return jnp.concatenate(parts, axis=1)


def _mine_to_blocks(g, lay):
    cols = lay["n_in"] // N_DEV
    by_src = sorted(lay["pieces"], key=lambda p: p[2])
    blocks = []
    for k in range(N_DEV):
        lo, hi, parts = k * cols, (k + 1) * cols, []
        for o, w, s in by_src:
            a, b = max(lo, s), min(hi, s + w)
            if a < b:
                parts.append(g[:, o + a - s:o + b - s])
        blocks.append(jnp.concatenate(parts, axis=1))
    return jnp.stack(blocks)


def _rope_tables(positions, half):
    inv = ROPE_THETA ** (-jnp.arange(half, dtype=F32) / half)
    ang = positions.astype(F32)[:, None] * inv
    return jnp.cos(ang), jnp.sin(ang)


def _pack_rows(vs):
    return jnp.concatenate([v.reshape(-1, LANES) for v in vs], axis=0)


def kernel(x, positions, norm_mix_g, w_in, ret_norm_g, w_ret_o, q_a_norm_g, w_q_b, kv_a_norm_g, w_kv_b, w_mla_o, w_out, norm_mlp_g, w_up, w_down, norm_f_g, loss_target, m_norm_mix_g, m_w_in, m_ret_norm_g, m_w_ret_o, m_q_a_norm_g, m_w_q_b, m_kv_a_norm_g, m_w_kv_b, m_w_mla_o, m_w_out, m_norm_mlp_g, m_w_up, m_w_down, m_norm_f_g, v_norm_mix_g, v_w_in, v_ret_norm_g, v_w_ret_o, v_q_a_norm_g, v_w_q_b, v_kv_a_norm_g, v_w_kv_b, v_w_mla_o, v_w_out, v_norm_mlp_g, v_w_up, v_w_down, v_norm_f_g):
    xs, tgt, pos = x[0], loss_target[0], positions[0]
    s, d = xs.shape
    mats = {"w_in": w_in[0], "w_ret_o": w_ret_o[0], "w_q_b": w_q_b[0], "w_kv_b": w_kv_b[0], "w_mla_o": w_mla_o[0],
            "w_out": w_out[0], "w_up": w_up[0], "w_down": w_down[0]}
    mat_w = {"w_in": w_in, "w_ret_o": w_ret_o, "w_q_b": w_q_b, "w_kv_b": w_kv_b, "w_mla_o": w_mla_o, "w_out": w_out,
             "w_up": w_up, "w_down": w_down}
    mat_m = {"w_in": m_w_in, "w_ret_o": m_w_ret_o, "w_q_b": m_w_q_b, "w_kv_b": m_w_kv_b, "w_mla_o": m_w_mla_o,
             "w_out": m_w_out, "w_up": m_w_up, "w_down": m_w_down}
    mat_v = {"w_in": v_w_in, "w_ret_o": v_w_ret_o, "w_q_b": v_w_q_b, "w_kv_b": v_w_kv_b, "w_mla_o": v_w_mla_o,
             "w_out": v_w_out, "w_up": v_w_up, "w_down": v_w_down}
    names = list(mats)
    col_sharded = ("w_in", "w_q_b", "w_kv_b", "w_up")
    vw = ret_norm_g.shape[1]
    mla_w = mats["w_mla_o"].shape[0] * N_DEV
    ql, kl = q_a_norm_g.shape[1], kv_a_norm_g.shape[1]
    n_in = mats["w_in"].shape[1] * N_DEV
    qw = (n_in - 2 * vw - ql - kl - ROPE - 2 * d) // 2
    lay = _make_layout(d, vw, qw, ql, kl, mla_w)
    assert lay["n_in"] == n_in
    heads_r, heads_m = lay["ret_heads"], lay["mla_heads"]

    shard16 = {n: mats[n].astype(BF16) for n in names}
    with_in_proj = ("w_ret_o", "w_q_b", "w_kv_b", "w_mla_o", "w_out")
    mlp = ("w_up", "w_down")
    by_device = ("w_up", "w_kv_b")
    full = {}

    def keep(group, gathered):
        for n, g in zip(group, gathered):
            if n not in by_device:
                g = _cols_to_full(g) if n in col_sharded else g.reshape(-1, g.shape[2])
            full[n] = g

    w_mine = _w_in_to_mine(_exchange_alone("gather_w_in", _gather_exchange([shard16["w_in"]]))[0], lay)

    c64, s64 = _rope_tables(pos, RET_QK // 2)
    cos_r = jnp.concatenate([c64, c64], axis=1)
    sin_r = jnp.concatenate([-s64, s64], axis=1)
    c32, s32 = _rope_tables(pos, ROPE // 2)
    z32, z64 = jnp.zeros_like(c32), jnp.zeros((s, LANES - ROPE), F32)
    cos_p = jnp.concatenate([c32, c32, z64], axis=1)
    sin_a = jnp.concatenate([-s32, z32, z64], axis=1)
    sin_b = jnp.concatenate([z32, s32, z64], axis=1)
    lg = jnp.log(1.0 - 2.0 ** (-5.0 - jnp.arange(heads_r, dtype=F32)))
    lgs = jnp.broadcast_to(lg[:, None, None], (heads_r, 8, LANES))

    tm = min(256, s)
    blk = min(512, s)
    t_att = min(512, s)

    u = _rms_fwd("norm_mix", xs, norm_mix_g, tm)
    proj, gathered = _mm("in_proj", u, w_mine, "nn", F32,
                         ride=_gather_exchange([shard16[n] for n in with_in_proj]))
    keep(with_in_proj, gathered)
    wq_pad = jnp.pad(full["w_q_b"].reshape(ql, heads_m, NOPE + ROPE),
                     ((0, 0), (0, 0), (0, QPAD - NOPE - ROPE))).reshape(ql, heads_m * QPAD)
    o_ret, states, ry = _ret_fwd(proj, lay, cos_r, sin_r, lgs, ret_norm_g, blk)
    y_ret = _mm("ret_out", ry, full["w_ret_o"], "nn", F32)
    cqn, ckvn, kpr = _mla_prep(proj, lay, q_a_norm_g, kv_a_norm_g, cos_p, sin_a, sin_b, tm)
    qf = _mm("q_up", cqn, wq_pad, "nn", BF16, extras=((cos_p, None), (sin_a, None), (sin_b, None)), epilogue=_q_operand)
    kv, kf = _mm("kv_up", ckvn, full["w_kv_b"], "nn", (BF16, BF16), b_by_device=True, extras=((kpr, None),),
                 epilogue=_k_operand)
    o_mla, lse, gathered = _attn_fwd(qf, kf, kv, lay, t_att, ride=_gather_exchange([shard16["w_up"]]))
    keep(("w_up",), gathered)
    gate_tile = _tile(d, 1024)
    y_mla, merged = _mm(
        "mla_out", o_mla, full["w_mla_o"], "nn", (F32, BF16), tm=512, tn=gate_tile,
        extras=((proj, lay["off"]["g_ret"] // gate_tile), (proj, lay["off"]["g_mla"] // gate_tile), y_ret),
        epilogue=lambda r, gr, gm, yr: (r, _sig(gr) * yr + _sig(gm) * r))
    h1, n2 = _mm("out_proj", merged, full["w_out"], "nn", (F32, BF16), tm=512, tn=d,
                 extras=(xs, (norm_mlp_g, "whole")), epilogue=_residual_norm)
    (z, act), gathered = _mm("mlp_up", n2, full["w_up"], "nn", (F32, BF16), b_by_device=True,
                             epilogue=lambda r: (r, jnp.square(jnp.maximum(r, 0.0))),
                             ride=_gather_exchange([shard16["w_down"]]))
    keep(("w_down",), gathered)
    dn = _mm("mlp_down", act, full["w_down"], "nn", F32, tk=4096)
    dh2, dh2_16, g_norm_f, loss_part = _final("loss_head", h1, dn, norm_f_g.reshape(1, d), tgt, tm)

    mx, my, mc = _place()
    c_arr = jnp.reshape(mc, (1,)).astype(jnp.int32)
    chip_arr = jnp.reshape(2 * mx + my, (1,)).astype(jnp.int32)
    sums, from_chips = {}, {}

    def blocks(group, grads):
        return [g if n in by_device else (_full_to_cols(g) if n in col_sharded else g.reshape((N_DEV,) + mats[n].shape))
                for n, g in zip(group, grads)]

    def pair_sums(group, mine, from_sibling):
        for n, g, r in zip(group, mine, from_sibling):
            sums[n] = _pair_sum("pair_sum_" + n, g, r, c_arr, 256)
        return [sums[n] for n in group]

    dz = _mm("mlp_down_dx", dh2_16, full["w_down"], "nt", BF16, extras=(z,),
             epilogue=lambda r, zz: (r * (2.0 * jnp.maximum(zz, 0.0)),))
    g_w_down = _mm("mlp_down_dw", act, dh2_16, "tn", BF16, tm=512, tn=d, tk=s)
    down_blocks = blocks(("w_down",), (g_w_down,))
    g_w_up, got_down = _mm("mlp_up_dw", n2, dz, "tn", BF16, tk=s, out_by_device=True,
                           ride=_sibling_exchange(down_blocks))
    dn2, got_up = _mm("mlp_up_dx", dz, full["w_up"], "nt", F32, tk=4096, b_by_device=True,
                      ride=_sibling_exchange([g_w_up]))
    mlp_sums = pair_sums(mlp, [g_w_up] + down_blocks, list(got_up) + list(got_down))
    dh1, dh1_16, g_norm_mlp = _rms_bwd("norm_mlp_bwd", dn2, h1, norm_mlp_g, dh2, tm, matmul_copy=True)
    assert lay["off"]["g_ret"] == 0 and lay["off"]["g_mla"] == d
    dy_ret, dy_mla, d_proj = _mm(
        "out_proj_dx", dh1_16, full["w_out"], "nt", (BF16, BF16, (BF16, 2, lay["total"])), tm=256, tn=d,
        extras=((proj, 0), (proj, 1), y_ret, y_mla), epilogue=_gate_grads)
    g_w_out = _mm("out_proj_dw", merged, dh1_16, "tn", BF16, tm=512, tn=d, tk=s)
    g_w_ret_o = _mm("ret_out_dw", ry, dy_ret, "tn", BF16, tm=512, tn=d, tk=s)
    g_w_mla_o = _mm("mla_out_dw", o_mla, dy_mla, "tn", BF16, tm=256, tn=d, tk=s)
    mixer = ("w_out", "w_ret_o", "w_mla_o")
    mixer_blocks = blocks(mixer, (g_w_out, g_w_ret_o, g_w_mla_o))
    d_ry, got = _mm("ret_out_dx", dy_ret, full["w_ret_o"], "nt", F32, ride=_sibling_exchange(mixer_blocks))
    mixer_sums = pair_sums(mixer, mixer_blocks, got)
    d_omla = _mm("mla_out_dx", dy_mla, full["w_mla_o"], "nt", F32)
    d_oret, d_proj, g_ret_norm = _ret_post_bwd(proj, lay, d_ry, o_ret, ret_norm_g, d_proj, tm)
    d_proj, got = _ret_bwd(proj, lay, cos_r, sin_r, lgs, states, d_oret, d_proj, blk,
                           ride=_chips_exchange(mixer_sums))
    from_chips.update(zip(mixer, got))
    delta = _attn_delta(d_omla, o_mla, lay, t_att)
    dqp, dkv, dkpe_parts, got = _attn_bwd(qf, kf, kv, lse, delta, d_omla, cos_p, sin_a, sin_b, lay, t_att,
                                          ride=_chips_exchange(mlp_sums))
    from_chips.update(zip(mlp, got))
    d_proj = _rope_key_grad(dkpe_parts, lay, d_proj, tm)
    d_cqn = _mm("q_up_dx", dqp, wq_pad, "nt", F32)
    g_wq_pad = _mm("q_up_dw", cqn, dqp, "tn", BF16)
    d_ckvn = _mm("kv_up_dx", dkv, full["w_kv_b"], "nt", F32, b_by_device=True)
    g_w_kv_b = _mm("kv_up_dw", ckvn, dkv, "tn", BF16, out_by_device=True)
    d_proj, g_q_a, g_kv_a = _mla_prep_bwd(proj, lay, d_cqn, d_ckvn, q_a_norm_g, kv_a_norm_g, d_proj, tm)
    g_w_mine = _mm("in_proj_dw", u, d_proj, "tn", BF16, tk=s)
    g_w_q_b = g_wq_pad.reshape(ql, heads_m, QPAD)[:, :, :NOPE + ROPE].reshape(ql, heads_m * (NOPE + ROPE))
    last = ("w_in", "w_q_b", "w_kv_b")
    last_blocks = [_mine_to_blocks(g_w_mine, lay)] + blocks(last[1:], (g_w_q_b, g_w_kv_b))
    last_sums = pair_sums(last, last_blocks, _exchange_alone("grads_to_sibling", _sibling_exchange(last_blocks)))
    du, got = _mm("in_proj_dx", d_proj, w_mine, "nt", F32, ride=_chips_exchange(last_sums))
    from_chips.update(zip(last, got))
    grad_x, g_norm_mix = _rms_bwd("norm_mix_bwd", du, xs, norm_mix_g, dh1, tm)

    upd = {n: _adamw_shard("adamw_" + n, mat_w[n], mat_m[n], mat_v[n], sums[n], from_chips[n], chip_arr, 256)
           for n in names}

    gains = [("norm_mix_g", norm_mix_g, m_norm_mix_g, v_norm_mix_g, g_norm_mix),
             ("ret_norm_g", ret_norm_g, m_ret_norm_g, v_ret_norm_g, g_ret_norm),
             ("q_a_norm_g", q_a_norm_g, m_q_a_norm_g, v_q_a_norm_g, g_q_a),
             ("kv_a_norm_g", kv_a_norm_g, m_kv_a_norm_g, v_kv_a_norm_g, g_kv_a),
             ("norm_mlp_g", norm_mlp_g, m_norm_mlp_g, v_norm_mlp_g, g_norm_mlp),
             ("norm_f_g", norm_f_g, m_norm_f_g, v_norm_f_g, g_norm_f)]
    n_rows = sum(g[1].size for g in gains) // LANES
    pad_rows = -(-(n_rows + 1) // 8) * 8 - n_rows
    tail = jnp.zeros((pad_rows, LANES), F32)
    part = jnp.concatenate([_pack_rows([g[4] for g in gains]),
                            jnp.broadcast_to(loss_part[:, :1], (1, LANES)), tail[1:]], axis=0)
    packed = [jnp.concatenate([_pack_rows([g[k] for g in gains]), tail], axis=0) for k in (1, 2, 3)]
    g_sm, d_sm, m_sm, v_sm = _small_all_reduce_adam(part, *packed)
    loss = g_sm[n_rows, 0]
    small = {}
    o = 0
    for name, w, _, _, _ in gains:
        r = w.size // LANES
        small[name] = [a[o:o + r].reshape(w.shape) for a in (g_sm, d_sm, m_sm, v_sm)]
        o += r

    order = ["norm_mix_g", "w_in", "ret_norm_g", "w_ret_o", "q_a_norm_g", "w_q_b", "kv_a_norm_g", "w_kv_b", "w_mla_o",
             "w_out", "norm_mlp_g", "w_up", "w_down", "norm_f_g"]
    outs = [loss, grad_x[None]]
    for k in range(4):
        for n in order:
            outs.append(small[n][k] if n in small else upd[n][k])
    return tuple(outs)
```

```python
import functools
import math

import jax
import jax.numpy as jnp
from jax import lax
from jax.experimental import pallas as pl
from jax.experimental.pallas import tpu as pltpu

F32 = jnp.float32
BF16 = jnp.bfloat16
MESH = pl.DeviceIdType.MESH

EPS = 1e-6
ROPE_THETA = 10000.0
CHUNK_SHIFT = 6
RET_QK = 128
RET_V = 256
NOPE = 128
ROPE = 64
VHEAD = 128
QPAD = 256
LANES = 128
N_DEV = 8
VMEM_LIMIT = 56 * 1024 * 1024

ADAM_LR = 0.001
ADAM_B1 = 0.9
ADAM_B2 = 0.999
ADAM_EPS = 1e-08
ADAM_WD = 0.01
ADAM_STEP = 10

NN = (((1,), (0,)), ((), ()))
NT = (((1,), (1,)), ((), ()))
TN = (((0,), (0,)), ((), ()))


def _dot(a, b, dims):
    return lax.dot_general(a.astype(BF16), b.astype(BF16), dims, preferred_element_type=F32)


def _tile(dim, pref):
    if dim <= pref:
        return dim
    t = (pref // LANES) * LANES
    while t >= LANES:
        if dim % t == 0:
            return t
        t -= LANES
    raise ValueError(f"no tile for {dim}")


def _params(sem):
    return pltpu.CompilerParams(dimension_semantics=sem, vmem_limit_bytes=VMEM_LIMIT)


def _sig(v):
    return 1.0 / (1.0 + jnp.exp(-v))


def _mm(name, a, b, mode, out_dtypes, *, tm=1024, tn=1024, tk=2048, extras=(), epilogue=None, ride=None,
        b_by_device=False, out_by_device=False):
    if b_by_device:
        b_cols = b.shape[2]
        b_shape = (b.shape[1], N_DEV * b_cols)
    else:
        b_shape = b.shape
    if mode == "nn":
        (m, k), (_, n) = a.shape, b_shape
    elif mode == "nt":
        (m, k), (n, _) = a.shape, b_shape
    else:
        (k, m), (_, n) = a.shape, b_shape
    tm, tn, tk = _tile(m, tm), _tile(n, tn), _tile(k, tk)
    if b_by_device and mode != "nt":
        tn = _tile(b_cols, tn)
    if out_by_device:
        tn = _tile(n // N_DEV, tn)
    nk = k // tk
    dims = {"nn": NN, "nt": NT, "tn": TN}[mode]
    a_spec = (pl.BlockSpec((tk, tm), lambda i, j, kk: (kk, i)) if mode == "tn"
              else pl.BlockSpec((tm, tk), lambda i, j, kk: (i, kk)))
    if b_by_device and mode == "nt":
        piece = min(tk, b_cols)
        n_b, per = tk // piece, b_cols // piece
        b_specs = [pl.BlockSpec((None, tn, piece),
                                lambda i, j, kk, p=p: ((kk * n_b + p) // per, j, (kk * n_b + p) % per))
                   for p in range(n_b)]
    elif b_by_device:
        per = b_cols // tn
        n_b, piece = 1, tk
        b_specs = [pl.BlockSpec((None, tk, tn), lambda i, j, kk: (j // per, kk, j % per))]
    else:
        n_b, piece = 1, tk
        b_specs = [pl.BlockSpec((tn, tk), lambda i, j, kk: (j, kk)) if mode == "nt"
                   else pl.BlockSpec((tk, tn), lambda i, j, kk: (kk, j))]
    tile_spec = pl.BlockSpec((tm, tn), lambda i, j, kk: (i, j))
    if out_by_device:
        per_out = n // N_DEV // tn
        out_spec = pl.BlockSpec((None, tm, tn), lambda i, j, kk: (j // per_out, i, j % per_out))
        out_dims = (N_DEV, m, n // N_DEV)
    else:
        out_spec, out_dims = tile_spec, (m, n)
    ex_arrays, ex_specs = [], []
    for e in extras:
        arr, off = e if isinstance(e, tuple) else (e, 0)
        ex_arrays.append(arr)
        if off == "whole":
            ex_specs.append(pl.BlockSpec(arr.shape, lambda i, j, kk, nd=arr.ndim: (0,) * nd))
        elif off is None:
            ex_specs.append(pl.BlockSpec((tm, arr.shape[1]), lambda i, j, kk: (i, 0)))
        else:
            ex_specs.append(pl.BlockSpec((tm, tn), lambda i, j, kk, off=off: (i, j + off)))
    n_ex = len(extras)
    single = not isinstance(out_dtypes, (tuple, list))
    dts = (out_dtypes,) if single else tuple(out_dtypes)
    out_specs, out_shapes = [], []
    for dt in dts:
        if isinstance(dt, tuple):
            dt, mult, width = dt
            out_specs.append(pl.BlockSpec((tm, mult * tn), lambda i, j, kk: (i, j)))
            out_shapes.append(jax.ShapeDtypeStruct((m, width), dt))
        else:
            out_specs.append(out_spec)
            out_shapes.append(jax.ShapeDtypeStruct(out_dims, dt))

    grid = (m // tm, n // tn, nk)
    r_in, r_out, r_sc = ride.counts() if ride else (0, 0, 0)
    n_acc = 1 if nk > 1 else 0

    def body(a_ref, *rest):
        b_refs, rest = rest[:n_b], rest[n_b:]
        ex, rest = rest[:n_ex], rest[n_ex:]
        ride_in, rest = rest[:r_in], rest[r_in:]
        outs, rest = rest[:len(dts)], rest[len(dts):]
        ride_out, rest = rest[:r_out], rest[r_out:]
        ride_scratch = rest[n_acc:]
        if ride:
            first, mid, last = _steps([pl.program_id(d) for d in range(3)], grid)
            ride.run(ride_in, ride_out, ride_scratch, (first, mid, None))

        def finish(r):
            vals = (r,) if epilogue is None else epilogue(r, *[e[...] for e in ex])
            for o, v in zip(outs, vals):
                o[...] = v.astype(o.dtype)

        if n_b == 1:
            part = _dot(a_ref[...], b_refs[0][...], dims)
        else:
            part = sum(_dot(a_ref[:, p * piece:(p + 1) * piece], b_refs[p][...], dims) for p in range(n_b))
        if nk == 1:
            finish(part)
        else:
            acc = rest[0]
            kk = pl.program_id(2)

            @pl.when(kk == 0)
            def _():
                acc[...] = part

            @pl.when(jnp.logical_and(kk > 0, kk < nk - 1))
            def _():
                acc[...] += part

            @pl.when(kk == nk - 1)
            def _():
                finish(acc[...] + part)

        if ride:
            ride.run(ride_in, ride_out, ride_scratch, (None, None, last))

    res = pl.pallas_call(
        body, name=name, grid=grid,
        in_specs=[a_spec] + b_specs + ex_specs + [ANY] * r_in,
        out_specs=out_specs + [ANY] * r_out,
        out_shape=out_shapes + (ride.out_shape if ride else []),
        scratch_shapes=([pltpu.VMEM((tm, tn), F32)] if nk > 1 else []) + (ride.scratch if ride else []),
        compiler_params=_params(("arbitrary",) * 3 if ride else ("parallel", "parallel", "arbitrary")),
    )(a, *[b] * n_b, *ex_arrays, *(ride.ins if ride else []))
    own = res[0] if single else res[:len(dts)]
    return (own, res[len(dts):]) if ride else own


def _rows(name, body, n_rows, tm, ins, outs, accs=(), into=None):
    in_specs, args = [], []
    for t in ins:
        if len(t) == 1:
            in_specs.append(pl.BlockSpec(t[0].shape, lambda i, nd=t[0].ndim: (0,) * nd))
        else:
            in_specs.append(pl.BlockSpec((tm, t[1]), lambda i, cb=t[2]: (i, cb)))
        args.append(t[0])
    outs = [(o + (o[0], 0))[:4] for o in outs]
    out_specs = [pl.BlockSpec((tm, w), lambda i, cb=cb: (i, cb)) for w, _, _, cb in outs]
    out_shape = [jax.ShapeDtypeStruct((n_rows, total), d) for _, d, total, _ in outs]
    aliases, kernel = {}, body
    if into is not None:
        arr, w, cb = into
        in_specs.append(ANY)
        args.append(arr)
        out_specs.append(pl.BlockSpec((tm, w), lambda i: (i, cb)))
        out_shape.append(jax.ShapeDtypeStruct(arr.shape, arr.dtype))
        aliases = {len(ins): len(outs)}
        n_in = len(ins)

        def kernel(*refs):
            body(*refs[:n_in], *refs[n_in + 1:])

    out_specs += [pl.BlockSpec((r, w), lambda i: (0, 0)) for r, w in accs]
    out_shape += [jax.ShapeDtypeStruct((r, w), F32) for r, w in accs]
    return pl.pallas_call(
        kernel, name=name, grid=(n_rows // tm,), in_specs=in_specs, out_specs=out_specs, out_shape=out_shape,
        input_output_aliases=aliases, compiler_params=_params(("arbitrary",) if accs else ("parallel",)),
    )(*args)


def _zero_first(*accs):
    @pl.when(pl.program_id(0) == 0)
    def _():
        for a in accs:
            a[...] = jnp.zeros_like(a)


def _rope64(t, cos, sin):
    return t * cos + pltpu.roll(t, RET_QK // 2, 1) * sin


def _rope32(t, cos, sin_a, sin_b):
    return t * cos + pltpu.roll(t, LANES - ROPE // 2, 1) * sin_a + pltpu.roll(t, ROPE // 2, 1) * sin_b


def _rms_fwd(name, x, g, tm):
    s, d = x.shape

    def body(x_ref, g_ref, u_ref):
        v = x_ref[...]
        r = lax.rsqrt(jnp.mean(v * v, axis=-1, keepdims=True) + EPS)
        u_ref[...] = (v * r * g_ref[...]).astype(BF16)

    return _rows(name, body, s, tm, [(x, d, 0), (g,)], [(d, BF16)])[0]


def _residual_norm(r, x, g):
    h = x + r
    return h, h * lax.rsqrt(jnp.mean(h * h, axis=-1, keepdims=True) + EPS) * g


def _gate_grads(dm, gr, gm, yr, ym):
    sr, sm = _sig(gr), _sig(gm)
    return dm * sr, dm * sm, jnp.concatenate([dm * yr * (sr * (1.0 - sr)), dm * ym * (sm * (1.0 - sm))], axis=1)


def _rms_bwd(name, dy, x, g, dres, tm, matmul_copy=False):
    s, d = x.shape

    def body(dy_ref, x_ref, g_ref, dres_ref, dx_ref, *rest):
        dg_ref = rest[-1]
        _zero_first(dg_ref)
        v, dyv = x_ref[...], dy_ref[...]
        r = lax.rsqrt(jnp.mean(v * v, axis=-1, keepdims=True) + EPS)
        xh = v * r
        dxh = dyv * g_ref[...]
        dx = dres_ref[...] + r * (dxh - xh * jnp.mean(dxh * xh, axis=-1, keepdims=True))
        dx_ref[...] = dx
        if matmul_copy:
            rest[0][...] = dx.astype(BF16)
        dg_ref[...] += jnp.sum(dyv * xh, axis=0, keepdims=True)

    return _rows(name, body, s, tm, [(dy, d, 0), (x, d, 0), (g,), (dres, d, 0)],
                 [(d, F32)] + [(d, BF16)] * matmul_copy, [(1, d)])


def _final(name, h1, dn, g, tgt, tm):
    s, d = h1.shape

    def body(h_ref, dn_ref, g_ref, t_ref, dh_ref, dh16_ref, dg_ref, loss_ref):
        _zero_first(dg_ref, loss_ref)
        v = h_ref[...] + dn_ref[...]
        r = lax.rsqrt(jnp.mean(v * v, axis=-1, keepdims=True) + EPS)
        xh = v * r
        gv = g_ref[...]
        e = xh * gv - t_ref[...]
        loss_ref[...] += 0.5 * jnp.sum(jnp.mean(e * e, axis=-1, keepdims=True))
        dy = e * (1.0 / d)
        dg_ref[...] += jnp.sum(dy * xh, axis=0, keepdims=True)
        dxh = dy * gv
        dh = r * (dxh - xh * jnp.mean(dxh * xh, axis=-1, keepdims=True))
        dh_ref[...] = dh
        dh16_ref[...] = dh.astype(BF16)

    return _rows(name, body, s, tm, [(h1, d, 0), (dn, d, 0), (g,), (tgt, d, 0)], [(d, F32), (d, BF16)],
                 [(1, d), (1, LANES)])


def _decay_mask(lg, blk):
    n = lax.broadcasted_iota(jnp.int32, (blk, blk), 0)
    m = lax.broadcasted_iota(jnp.int32, (blk, blk), 1)
    w = jnp.exp(lg * jnp.abs(n - m).astype(F32))
    return jnp.where(jnp.right_shift(m, CHUNK_SHIFT) <= jnp.right_shift(n, CHUNK_SHIFT), w, 0.0)


def _decays(lg, blk):
    pos = lax.broadcasted_iota(jnp.int32, (blk, 1), 0).astype(F32)
    return jnp.exp(lg * (pos + 1.0)), jnp.exp(lg * (blk - 1.0 - pos)), jnp.exp(lg * float(blk))


def _ret_fwd(proj, lay, cos, sin, lgs, gain, blk, ride=None):
    s = proj.shape[0]
    heads = lay["ret_heads"]
    nb = s // blk
    scale = RET_QK ** -0.5
    ride_in_specs, ride_ins, ride_out_specs, ride_out_shape, ride_scratch = _ride_args(ride)

    def body(lg_ref, qkv_ref, cos_ref, sin_ref, g_ref, o_ref, st_ref, ry_ref, state, mask):
        lg = lg_ref[0:1, 0:1]

        @pl.when(pl.program_id(1) == 0)
        def _():
            state[...] = jnp.zeros_like(state)
            mask[...] = _decay_mask(lg, blk)

        a, c, gb = _decays(lg, blk)
        q = _rope64(qkv_ref[:, :RET_QK], cos_ref[...], sin_ref[...])
        k = _rope64(qkv_ref[:, RET_QK:2 * RET_QK], cos_ref[...], sin_ref[...]) * scale
        v = qkv_ref[:, 2 * RET_QK:2 * RET_QK + RET_V]
        st = state[...]
        st_ref[...] = st
        sm = _dot(q, k, NT) * mask[...]
        o = _dot(sm, v, NN) + _dot(q * a, st, NN)
        o_ref[...] = o
        state[...] = st * gb + _dot(k * c, v, TN)
        dlt = o - jnp.mean(o, axis=-1, keepdims=True)
        rstd = lax.rsqrt(jnp.mean(dlt * dlt, axis=-1, keepdims=True) + EPS)
        rg = qkv_ref[:, 2 * RET_QK + RET_V:]
        ry_ref[...] = (dlt * rstd * g_ref[...] * (rg * _sig(rg))).astype(BF16)

    first = lay["off"]["heads"] // RET_HEAD
    res = pl.pallas_call(
        _with_ride(body, ride, (heads, nb), 0, 5, 3), name="ret_fwd", grid=(heads, nb),
        in_specs=[pl.BlockSpec((None, 8, LANES), lambda h, b: (h, 0, 0)),
                  pl.BlockSpec((blk, RET_HEAD), lambda h, b: (b, first + h)),
                  pl.BlockSpec((blk, LANES), lambda h, b: (b, 0)),
                  pl.BlockSpec((blk, LANES), lambda h, b: (b, 0)),
                  pl.BlockSpec((1, RET_V), lambda h, b: (0, h))] + ride_in_specs,
        out_specs=[pl.BlockSpec((blk, RET_V), lambda h, b: (b, h)),
                   pl.BlockSpec((None, None, RET_QK, RET_V), lambda h, b: (h, b, 0, 0)),
                   pl.BlockSpec((blk, RET_V), lambda h, b: (b, h))] + ride_out_specs,
        out_shape=[jax.ShapeDtypeStruct((s, heads * RET_V), F32),
                   jax.ShapeDtypeStruct((heads, nb, RET_QK, RET_V), F32),
                   jax.ShapeDtypeStruct((s, heads * RET_V), BF16)] + ride_out_shape,
        scratch_shapes=[pltpu.VMEM((RET_QK, RET_V), F32), pltpu.VMEM((blk, blk), F32)] + ride_scratch,
        compiler_params=_params(("arbitrary", "arbitrary") if ride else ("parallel", "arbitrary")),
    )(lgs, proj, cos, sin, gain, *ride_ins)
    return (res[0], res[1], res[2], res[3:]) if ride else res


def _ret_bwd(proj, lay, cos, sin, lgs, states, d_ry, o, gain, d_proj, blk, ride=None):
    ride_in_specs, ride_ins, ride_out_specs, ride_out_shape, ride_scratch = _ride_args(ride)
    s = proj.shape[0]
    heads = lay["ret_heads"]
    nb = s // blk
    scale = RET_QK ** -0.5

    def body(lg_ref, qkv_ref, cos_ref, sin_ref, st_ref, dry_ref, o_ref, g_ref, _, dqkv_ref, dg_ref, dstate, mask):
        lg = lg_ref[0:1, 0:1]

        @pl.when(pl.program_id(1) == 0)
        def _():
            dstate[...] = jnp.zeros_like(dstate)
            mask[...] = _decay_mask(lg, blk)
            dg_ref[...] = jnp.zeros_like(dg_ref)

        oh = o_ref[...]
        dlt = oh - jnp.mean(oh, axis=-1, keepdims=True)
        rstd = lax.rsqrt(jnp.mean(dlt * dlt, axis=-1, keepdims=True) + EPS)
        oh = dlt * rstd
        gv = g_ref[...]
        rg = qkv_ref[:, 2 * RET_QK + RET_V:]
        sg = _sig(rg)
        dry = dry_ref[...]
        dt = dry * (rg * sg)
        dqkv_ref[:, 2 * RET_QK + RET_V:] = (dry * (oh * gv) * (sg * (1.0 + rg * (1.0 - sg)))).astype(BF16)
        dg_ref[...] += jnp.sum(dt * oh, axis=0, keepdims=True)
        doh = dt * gv
        do = rstd * (doh - jnp.mean(doh, axis=-1, keepdims=True) - oh * jnp.mean(doh * oh, axis=-1, keepdims=True))

        a, c, gb = _decays(lg, blk)
        cs, sn = cos_ref[...], sin_ref[...]
        q = _rope64(qkv_ref[:, :RET_QK], cs, sn)
        k = _rope64(qkv_ref[:, RET_QK:2 * RET_QK], cs, sn) * scale
        v = qkv_ref[:, 2 * RET_QK:2 * RET_QK + RET_V]
        st = st_ref[...]
        dst = dstate[...]
        mk = mask[...]
        sm = _dot(q, k, NT) * mk
        ds = _dot(do, v, NT) * mk
        dq = _dot(ds, k, NN) + _dot(do, st, NT) * a
        dk = _dot(ds, q, TN) + _dot(v, dst, NT) * c
        dqkv_ref[:, 2 * RET_QK:2 * RET_QK + RET_V] = (_dot(sm, do, TN) + _dot(k * c, dst, NN)).astype(BF16)
        dstate[...] = dst * gb + _dot(q * a, do, TN)
        dqkv_ref[:, :RET_QK] = _rope64(dq, cs, -sn).astype(BF16)
        dqkv_ref[:, RET_QK:2 * RET_QK] = (_rope64(dk, cs, -sn) * scale).astype(BF16)

    first = lay["off"]["heads"] // RET_HEAD
    last = nb - 1
    head_tile = pl.BlockSpec((blk, RET_V), lambda h, b: (last - b, h))
    res = pl.pallas_call(
        _with_ride(body, ride, (heads, nb), 0, 9, 2), name="ret_bwd", grid=(heads, nb),
        in_specs=[pl.BlockSpec((None, 8, LANES), lambda h, b: (h, 0, 0)),
                  pl.BlockSpec((blk, RET_HEAD), lambda h, b: (last - b, first + h)),
                  pl.BlockSpec((blk, LANES), lambda h, b: (last - b, 0)),
                  pl.BlockSpec((blk, LANES), lambda h, b: (last - b, 0)),
                  pl.BlockSpec((None, None, RET_QK, RET_V), lambda h, b: (h, last - b, 0, 0)),
                  head_tile, head_tile, pl.BlockSpec((1, RET_V), lambda h, b: (0, h)), ANY] + ride_in_specs,
        out_specs=[pl.BlockSpec((blk, RET_HEAD), lambda h, b: (last - b, first + h)),
                   pl.BlockSpec((1, RET_V), lambda h, b: (0, h))] + ride_out_specs,
        out_shape=[jax.ShapeDtypeStruct(d_proj.shape, d_proj.dtype),
                   jax.ShapeDtypeStruct((1, heads * RET_V), F32)] + ride_out_shape,
        scratch_shapes=[pltpu.VMEM((RET_QK, RET_V), F32), pltpu.VMEM((blk, blk), F32)] + ride_scratch,
        input_output_aliases={8: 0},
        compiler_params=_params(("arbitrary", "arbitrary") if ride else ("parallel", "arbitrary")),
    )(lgs, proj, cos, sin, states, d_ry, o, gain, d_proj, *ride_ins)
    return (res[0], res[1], res[2:]) if ride else res[:2]


def _mla_prep(proj, lay, gq, gkv, cos, sin_a, sin_b, tm):
    s = proj.shape[0]
    ql, kl = lay["q_lora"], lay["kv_lora"]

    def body(cq_ref, ckv_ref, kpe_ref, gq_ref, gkv_ref, cos_ref, sa_ref, sb_ref, cqn_ref, ckvn_ref, kpr_ref):
        for src, gref, dst in ((cq_ref, gq_ref, cqn_ref), (ckv_ref, gkv_ref, ckvn_ref)):
            v = src[...]
            r = lax.rsqrt(jnp.mean(v * v, axis=-1, keepdims=True) + EPS)
            dst[...] = (v * r * gref[...]).astype(BF16)
        kpr_ref[...] = _rope32(kpe_ref[...], cos_ref[...], sa_ref[...], sb_ref[...]).astype(BF16)

    off = lay["off"]
    return _rows("mla_prep", body, s, tm,
                 [(proj, ql, off["c_q"] // ql), (proj, kl, off["c_kv"] // kl), (proj, LANES, off["k_pe"] // LANES),
                  (gq,), (gkv,), (cos, LANES, 0), (sin_a, LANES, 0), (sin_b, LANES, 0)],
                 [(ql, BF16), (kl, BF16), (LANES, BF16)])


def _latent_norm_bwd(name, proj, offset, d_normed, g, d_proj, tm):
    s, w = d_normed.shape

    def body(dy_ref, x_ref, g_ref, dx_ref, dg_ref):
        _zero_first(dg_ref)
        v, dy = x_ref[...], dy_ref[...]
        r = lax.rsqrt(jnp.mean(v * v, axis=-1, keepdims=True) + EPS)
        xh = v * r
        dxh = dy * g_ref[...]
        dx_ref[...] = (r * (dxh - xh * jnp.mean(dxh * xh, axis=-1, keepdims=True))).astype(BF16)
        dg_ref[...] += jnp.sum(dy * xh, axis=0, keepdims=True)

    return _rows(name, body, s, tm, [(d_normed, w, 0), (proj, w, offset // w), (g,)], [], [(1, w)],
                 into=(d_proj, w, offset // w))


def _q_operand(r, cos, sin_a, sin_b):
    qs = (NOPE + ROPE) ** -0.5 * math.log2(math.e)
    cs, sa, sb = cos * qs, sin_a * qs, sin_b * qs
    parts = []
    for lo in range(0, r.shape[1], QPAD):
        parts += [r[:, lo:lo + NOPE] * qs, _rope32(r[:, lo + NOPE:lo + QPAD], cs, sa, sb)]
    return (jnp.concatenate(parts, axis=1),)


def _k_operand(r, kpr):
    parts = []
    for lo in range(0, r.shape[1], QPAD):
        parts += [r[:, lo:lo + NOPE], kpr.astype(F32)]
    return r, jnp.concatenate(parts, axis=1)


def _rope_key_grad(parts, lay, d_proj, tm):
    s, w = parts.shape

    def body(p_ref, dkpe_ref):
        dkpe_ref[:, :LANES] = sum(p_ref[:, lo:lo + LANES] for lo in range(0, w, LANES)).astype(BF16)
        dkpe_ref[:, LANES:] = jnp.zeros((tm, LANES), BF16)

    return _rows("rope_key_grad", body, s, tm, [(parts, w, 0)], [],
                 into=(d_proj, 2 * LANES, lay["off"]["k_pe"] // (2 * LANES)))[0]


def _diag_mask(t, keys_on_rows=False):
    row = lax.broadcasted_iota(jnp.int32, (t, t), 0)
    col = lax.broadcasted_iota(jnp.int32, (t, t), 1)
    key, query = (row, col) if keys_on_rows else (col, row)
    return jnp.right_shift(key, CHUNK_SHIFT) <= jnp.right_shift(query, CHUNK_SHIFT)


def _tile_pairs(nt, by_key):
    if by_key:
        pairs = [(i, j) for j in range(nt) for i in range(j, nt)]
    else:
        pairs = [(i, j) for i in range(nt) for j in range(i + 1)]
    return (jnp.asarray([p[0] for p in pairs], jnp.int32), jnp.asarray([p[1] for p in pairs], jnp.int32))


def _head_block(heads):
    return 4 if heads % 4 == 0 else 2 if heads % 2 == 0 else 1


def _attn_fwd(qf, kf, kv, lay, t, ride=None):
    s = qf.shape[0]
    heads = lay["mla_heads"]
    hb = _head_block(heads)
    nt = s // t
    qi, kj = _tile_pairs(nt, False)
    grid = (heads // hb, int(qi.shape[0]))
    ride_in_specs, ride_ins, ride_out_specs, ride_out_shape, ride_scratch = _ride_args(ride)

    def body(qi_ref, kj_ref, q_ref, k_ref, kv_ref, o_ref, lse_ref, m_s, l_s, acc):
        p = pl.program_id(1)
        i, j = qi_ref[p], kj_ref[p]

        @pl.when(j == 0)
        def _():
            m_s[...] = jnp.full_like(m_s, -jnp.inf)
            l_s[...] = jnp.zeros_like(l_s)
            acc[...] = jnp.zeros_like(acc)

        def step(diagonal):
            ones = jnp.ones((t, LANES), BF16)
            scores = [_dot(q_ref[:, hh * QPAD:(hh + 1) * QPAD], k_ref[:, hh * QPAD:(hh + 1) * QPAD], NT)
                      for hh in range(hb)]
            for hh in range(hb):
                sc = scores[hh]
                if diagonal:
                    sc = jnp.where(_diag_mask(t), sc, -jnp.inf)
                cols = [sc[:, c * LANES:(c + 1) * LANES] for c in range(t // LANES)]
                m_old = m_s[hh]
                m_new = jnp.maximum(m_old, jnp.max(functools.reduce(jnp.maximum, cols), axis=-1, keepdims=True))
                alpha = jnp.exp2(m_old - m_new)
                pr = jnp.concatenate([jnp.exp2(c - m_new).astype(BF16) for c in cols], axis=1)
                pv = _dot(pr, jnp.concatenate([kv_ref[:, hh * QPAD + NOPE:(hh + 1) * QPAD], ones], axis=1), NN)
                l_new = alpha * l_s[hh] + pv[:, VHEAD:]
                a_new = alpha * acc[hh] + pv[:, :VHEAD]
                if diagonal:
                    o_ref[:, hh * VHEAD:(hh + 1) * VHEAD] = a_new / l_new
                    lse_ref[hh] = jnp.transpose(m_new + jnp.log2(l_new))[:1]
                else:
                    m_s[hh], l_s[hh], acc[hh] = m_new, l_new, a_new

        pl.when(j < i)(functools.partial(step, False))
        pl.when(j == i)(functools.partial(step, True))

    res = pl.pallas_call(
        _with_ride(body, ride, grid, 2, 3, 2), name="attn_fwd",
        grid_spec=pltpu.PrefetchScalarGridSpec(
            num_scalar_prefetch=2, grid=grid,
            in_specs=[pl.BlockSpec((t, hb * QPAD), lambda h, p, qi, kj: (qi[p], h)),
                      pl.BlockSpec((t, hb * QPAD), lambda h, p, qi, kj: (kj[p], h)),
                      pl.BlockSpec((t, hb * QPAD), lambda h, p, qi, kj: (kj[p], h))] + ride_in_specs,
            out_specs=[pl.BlockSpec((t, hb * VHEAD), lambda h, p, qi, kj: (qi[p], h)),
                       pl.BlockSpec((hb, 1, t), lambda h, p, qi, kj: (h, 0, qi[p]))] + ride_out_specs,
            scratch_shapes=[pltpu.VMEM((hb, t, LANES), F32), pltpu.VMEM((hb, t, LANES), F32),
                            pltpu.VMEM((hb, t, VHEAD), F32)] + ride_scratch),
        out_shape=[jax.ShapeDtypeStruct((s, heads * VHEAD), F32),
                   jax.ShapeDtypeStruct((heads, 1, s), F32)] + ride_out_shape,
        compiler_params=_params(("arbitrary", "arbitrary") if ride else ("parallel", "arbitrary")),
    )(qi, kj, qf, kf, kv, *ride_ins)
    return (res[0], res[1], res[2:]) if ride else res


def _attn_delta(d_o, o, lay, tm):
    s = o.shape[0]
    heads = lay["mla_heads"]

    def body(do_ref, o_ref, dl_ref):
        for h in range(heads):
            sl = slice(h * VHEAD, (h + 1) * VHEAD)
            dl_ref[h] = jnp.sum(jnp.transpose(do_ref[:, sl] * o_ref[:, sl]), axis=0, keepdims=True)

    tile = pl.BlockSpec((tm, heads * VHEAD), lambda i: (i, 0))
    return pl.pallas_call(
        body, name="attn_delta", grid=(s // tm,), in_specs=[tile, tile],
        out_specs=pl.BlockSpec((heads, 1, tm), lambda i: (0, 0, i)),
        out_shape=jax.ShapeDtypeStruct((heads, 1, s), F32),
        compiler_params=_params(("parallel",)),
    )(d_o, o)


def _attn_bwd(qf, kf, kv, lse, delta, d_o, cos, sin_a, sin_b, lay, t, ride=None):
    s = qf.shape[0]
    heads = lay["mla_heads"]
    hb = _head_block(heads)
    nt = s // t
    scale = (NOPE + ROPE) ** -0.5
    qi, kj = _tile_pairs(nt, True)
    grid = (heads // hb, int(qi.shape[0]))
    ride_in_specs, ride_ins, ride_out_specs, ride_out_shape, ride_scratch = _ride_args(ride)

    def body(qi_ref, kj_ref, q_ref, k_ref, kv_ref, lse_ref, dl_ref, do_ref, cos_ref, sa_ref, sb_ref,
             dqp_ref, dkv_ref, dkpe_ref, dq_acc, dk_acc, dv_acc):
        p = pl.program_id(1)
        i, j = qi_ref[p], kj_ref[p]
        rows = pl.ds(pl.multiple_of(i * t, t), t)

        def unrope(v):
            return _rope32(v, cos_ref[...], -sa_ref[...], -sb_ref[...])

        @pl.when(p == 0)
        def _():
            dq_acc[...] = jnp.zeros_like(dq_acc)

        def step(diagonal):
            for hh in range(hb):
                lo = hh * QPAD
                q, k = q_ref[:, lo:lo + QPAD], k_ref[:, lo:lo + QPAD]
                do = do_ref[:, hh * VHEAD:(hh + 1) * VHEAD]
                pr = jnp.exp2(_dot(k, q, NT) - lse_ref[hh])
                if diagonal:
                    pr = jnp.where(_diag_mask(t, keys_on_rows=True), pr, 0.0)
                dv_part = _dot(pr, do, NN)
                ds = (pr * (_dot(kv_ref[:, lo + NOPE:lo + QPAD], do, NT) - dl_ref[hh])).astype(BF16)
                dk_part = _dot(ds, q, NN)
                dq = dq_acc[rows, lo:lo + QPAD] + _dot(ds, k, TN) * scale
                if diagonal:
                    dqp_ref[:, lo:lo + NOPE] = dq[:, :NOPE].astype(BF16)
                    dqp_ref[:, lo + NOPE:lo + QPAD] = unrope(dq[:, NOPE:]).astype(BF16)
                    dk_acc[hh], dv_acc[hh] = dk_part, dv_part
                else:
                    dq_acc[rows, lo:lo + QPAD] = dq
                    dk_acc[hh] += dk_part
                    dv_acc[hh] += dv_part

        pl.when(i > j)(functools.partial(step, False))
        pl.when(i == j)(functools.partial(step, True))

        @pl.when(i == nt - 1)
        def _():
            kpe = jnp.zeros((t, LANES), F32)
            for hh in range(hb):
                lo = hh * QPAD
                dk = dk_acc[hh] * math.log(2.0)
                dkv_ref[:, lo:lo + NOPE] = dk[:, :NOPE].astype(BF16)
                dkv_ref[:, lo + NOPE:lo + QPAD] = dv_acc[hh].astype(BF16)
                kpe = kpe + dk[:, NOPE:]
            dkpe_ref[...] = unrope(kpe)

    table = pl.BlockSpec((t, LANES), lambda h, p, qi, kj: (kj[p], 0))
    res = pl.pallas_call(
        _with_ride(body, ride, grid, 2, 9, 3), name="attn_bwd",
        grid_spec=pltpu.PrefetchScalarGridSpec(
            num_scalar_prefetch=2, grid=grid,
            in_specs=[pl.BlockSpec((t, hb * QPAD), lambda h, p, qi, kj: (qi[p], h)),
                      pl.BlockSpec((t, hb * QPAD), lambda h, p, qi, kj: (kj[p], h)),
                      pl.BlockSpec((t, hb * QPAD), lambda h, p, qi, kj: (kj[p], h)),
                      pl.BlockSpec((hb, 1, t), lambda h, p, qi, kj: (h, 0, qi[p])),
                      pl.BlockSpec((hb, 1, t), lambda h, p, qi, kj: (h, 0, qi[p])),
                      pl.BlockSpec((t, hb * VHEAD), lambda h, p, qi, kj: (qi[p], h)),
                      table, table, table] + ride_in_specs,
            out_specs=[pl.BlockSpec((t, hb * QPAD), lambda h, p, qi, kj: (kj[p], h)),
                       pl.BlockSpec((t, hb * QPAD), lambda h, p, qi, kj: (kj[p], h)),
                       pl.BlockSpec((t, LANES), lambda h, p, qi, kj: (kj[p], h))] + ride_out_specs,
            scratch_shapes=[pltpu.VMEM((s, hb * QPAD), F32), pltpu.VMEM((hb, t, QPAD), F32),
                            pltpu.VMEM((hb, t, VHEAD), F32)] + ride_scratch),
        out_shape=[jax.ShapeDtypeStruct((s, heads * QPAD), BF16),
                   jax.ShapeDtypeStruct((s, heads * QPAD), BF16),
                   jax.ShapeDtypeStruct((s, heads // hb * LANES), F32)] + ride_out_shape,
        compiler_params=_params(("arbitrary", "arbitrary") if ride else ("parallel", "arbitrary")),
    )(qi, kj, qf, kf, kv, lse, delta, d_o, cos, sin_a, sin_b, *ride_ins)
    return (res[0], res[1], res[2], res[3:]) if ride else res


ANY = pl.BlockSpec(memory_space=pl.ANY)


def _place():
    return lax.axis_index("x"), lax.axis_index("y"), lax.axis_index("c")


def _other_chips(x, y):
    return [(1 - x, y), (x, 1 - y), (1 - x, 1 - y)]


class _Exchange:
    def __init__(self, ins, out_shape, scratch, phases):
        self.ins, self.out_shape, self.scratch, self.phases = list(ins), list(out_shape), list(scratch), phases

    def counts(self):
        return len(self.ins), len(self.out_shape), len(self.scratch)

    def run(self, r_in, r_out, r_scratch, conds):
        for cond, phase in zip(conds, self.phases):
            if phase is not None and cond is not None:
                pl.when(cond)(functools.partial(phase, r_in, r_out, r_scratch))


def _steps(ids, sizes):
    lin, total = 0, 1
    for i, n in zip(ids, sizes):
        lin, total = lin * n + i, total * n
    return lin == 0, lin == total // 2, lin == total - 1


def _ride_args(ride):
    if ride is None:
        return [], [], [], [], []
    n_in, n_out, _ = ride.counts()
    return [ANY] * n_in, ride.ins, [ANY] * n_out, ride.out_shape, ride.scratch


def _with_ride(body, ride, grid, n_prefetch, n_in, n_out):
    if ride is None:
        return body
    r_in, r_out, r_sc = ride.counts()

    def hosted(*refs):
        cuts = (n_prefetch, n_in, r_in, n_out, r_out)
        parts, pos = [], 0
        for n in cuts:
            parts.append(refs[pos:pos + n])
            pos += n
        pre, ins, ride_in, outs, ride_out = parts
        scratch, ride_scratch = refs[pos:len(refs) - r_sc], refs[len(refs) - r_sc:]
        first, mid, last = _steps([pl.program_id(d) for d in range(len(grid))], grid)
        ride.run(ride_in, ride_out, ride_scratch, (first, mid, None))
        body(*pre, *ins, *outs, *scratch)
        ride.run(ride_in, ride_out, ride_scratch, (None, None, last))

    return hosted


def _exchange_alone(name, ex):
    n_in, n_out, _ = ex.counts()

    def body(*refs):
        for phase in ex.phases:
            if phase is not None:
                phase(refs[:n_in], refs[n_in:n_in + n_out], refs[n_in + n_out:])

    return pl.pallas_call(
        body, name=name, in_specs=[ANY] * n_in, out_specs=[ANY] * n_out, out_shape=ex.out_shape,
        scratch_shapes=ex.scratch)(*ex.ins)


def _gather_exchange(shards):
    nw = len(shards)

    def parts(ins, outs, sems):
        send_sems, recv_sems, local_sems = sems
        x, y, c = _place()

        def slot(px, py, pc):
            return 4 * px + 2 * py + pc

        def copy(w, k, rows, to, src=None):
            return pltpu.make_async_remote_copy(
                src_ref=rows if src is None else src, dst_ref=rows, send_sem=send_sems.at[w, k],
                recv_sem=recv_sems.at[w, k], device_id=to, device_id_type=MESH)

        def plan(w, mine):
            side = c if mine else 1 - c
            half = shards[w].shape[0] // 2
            whole = lambda px, py: outs[w].at[slot(px, py, side)]
            top = lambda px, py: outs[w].at[slot(px, py, side), pl.ds(0, half)]
            bottom = lambda px, py: outs[w].at[slot(px, py, side), pl.ds(half, half)]
            xn, yn, sib = (1 - x, y, side), (x, 1 - y, side), (x, y, 1 - side)
            own = ins[w] if mine else None
            return [copy(w, 0, whole(x, y), sib, own), copy(w, 1, whole(x, y), xn, own),
                    copy(w, 2, whole(x, y), yn, own), copy(w, 3, top(1 - x, y), yn), copy(w, 4, bottom(x, 1 - y), xn),
                    copy(w, 5, whole(1 - x, y), sib), copy(w, 6, whole(x, 1 - y), sib),
                    copy(w, 7, top(1 - x, 1 - y), sib), copy(w, 8, bottom(1 - x, 1 - y), sib)]

        def arrivals(w):
            half = shards[w].shape[0] // 2
            at = lambda px, py, *rows: outs[w].at[(slot(px, py, c),) + rows]
            return {1: copy(w, 1, at(1 - x, y), (x, y, c)), 2: copy(w, 2, at(x, 1 - y), (x, y, c)),
                    3: copy(w, 3, at(1 - x, 1 - y, pl.ds(0, half)), (x, y, c)),
                    4: copy(w, 4, at(1 - x, 1 - y, pl.ds(half, half)), (x, y, c))}

        local = [pltpu.make_async_copy(ins[w], outs[w].at[slot(x, y, c)], local_sems.at[w]) for w in range(nw)]
        return plan, arrivals, local

    def start(ins, outs, sems):
        plan, _, local = parts(ins, outs, sems)
        for cp in local:
            cp.start()
        for w in range(nw):
            for k in (0, 1, 2):
                plan(w, True)[k].start()

    def middle(ins, outs, sems):
        plan, arrivals, _ = parts(ins, outs, sems)
        for landed, onward in ((1, (3, 5)), (2, (4, 6))):
            for w in range(nw):
                arrivals(w)[landed].wait_recv()
                for k in onward:
                    plan(w, True)[k].start()

    def finish(ins, outs, sems):
        plan, arrivals, local = parts(ins, outs, sems)
        for landed, onward in ((3, 7), (4, 8)):
            for w in range(nw):
                arrivals(w)[landed].wait_recv()
                plan(w, True)[onward].start()
        for w in range(nw):
            from_sibling = plan(w, False)
            for k in (0, 5, 6, 7, 8):
                from_sibling[k].wait_recv()
            for cp in plan(w, True):
                cp.wait_send()
        for cp in local:
            cp.wait()

    return _Exchange(
        shards, [jax.ShapeDtypeStruct((N_DEV,) + s.shape, s.dtype) for s in shards],
        [pltpu.SemaphoreType.DMA((nw, 9)), pltpu.SemaphoreType.DMA((nw, 9)), pltpu.SemaphoreType.DMA((nw,))],
        (start, middle, finish))


def _sibling_exchange(grads):
    nw = len(grads)

    def copies(ins, outs, sems):
        x, y, c = _place()
        return [pltpu.make_async_remote_copy(
            src_ref=ins[w].at[2 * p + (1 - c)], dst_ref=outs[w].at[p], send_sem=sems[0].at[w, p],
            recv_sem=sems[1].at[w, p], device_id=(x, y, 1 - c), device_id_type=MESH)
            for w in range(nw) for p in range(4)]

    def start(ins, outs, sems):
        for cp in copies(ins, outs, sems):
            cp.start()

    def finish(ins, outs, sems):
        for cp in copies(ins, outs, sems):
            cp.wait()

    return _Exchange(grads, [jax.ShapeDtypeStruct((4,) + g.shape[1:], g.dtype) for g in grads],
                     [pltpu.SemaphoreType.DMA((nw, 4)), pltpu.SemaphoreType.DMA((nw, 4))], (start, None, finish))


def _chips_exchange(sums):
    nw = len(sums)

    def copies(ins, outs, sems):
        x, y, c = _place()
        return [pltpu.make_async_remote_copy(
            src_ref=ins[w].at[2 * px + py], dst_ref=outs[w].at[k], send_sem=sems[0].at[w, k],
            recv_sem=sems[1].at[w, k], device_id=(px, py, c), device_id_type=MESH)
            for w in range(nw) for k, (px, py) in enumerate(_other_chips(x, y))]

    def start(ins, outs, sems):
        for cp in copies(ins, outs, sems):
            cp.start()

    def finish(ins, outs, sems):
        for cp in copies(ins, outs, sems):
            cp.wait()

    return _Exchange(sums, [jax.ShapeDtypeStruct((3,) + g.shape[1:], g.dtype) for g in sums],
                     [pltpu.SemaphoreType.DMA((nw, 3)), pltpu.SemaphoreType.DMA((nw, 3))], (start, None, finish))


def _pair_sum(name, g, got, c_arr, tr):
    _, rows, cols = g.shape
    tr = _tile_rows(rows, tr)

    def body(c_ref, a_ref, b_ref, o_ref):
        o_ref[...] = (a_ref[...].astype(F32) + b_ref[...].astype(F32)).astype(BF16)

    return pl.pallas_call(
        body, name=name,
        grid_spec=pltpu.PrefetchScalarGridSpec(
            num_scalar_prefetch=1, grid=(4, rows // tr),
            in_specs=[pl.BlockSpec((None, tr, cols), lambda p, r, cr: (2 * p + cr[0], r, 0)),
                      pl.BlockSpec((None, tr, cols), lambda p, r, cr: (p, r, 0))],
            out_specs=pl.BlockSpec((None, tr, cols), lambda p, r, cr: (p, r, 0))),
        out_shape=jax.ShapeDtypeStruct((4, rows, cols), BF16),
        compiler_params=_params(("parallel", "parallel")),
    )(c_arr, g, got)


def _tile_rows(rows, pref):
    t = min(rows, pref)
    while rows % t or t % 8:
        t -= 1
    return t


def _adam(w, g, m, v):
    m = ADAM_B1 * m + (1.0 - ADAM_B1) * g
    v = ADAM_B2 * v + (1.0 - ADAM_B2) * (g * g)
    m_hat = m / (1.0 - ADAM_B1 ** ADAM_STEP)
    v_hat = v / (1.0 - ADAM_B2 ** ADAM_STEP)
    return -ADAM_LR * (m_hat / (jnp.sqrt(v_hat) + ADAM_EPS) + ADAM_WD * w), m, v


def _adamw_shard(name, w, m, v, sums, got, chip_arr, tr):
    _, rows, cols = w.shape
    tr = _tile_rows(rows, tr)

    def body(p_ref, w_ref, m_ref, v_ref, s_ref, r_ref, g_ref, d_ref, nm_ref, nv_ref):
        g = s_ref[...].astype(F32)
        for k in range(3):
            g = g + r_ref[k].astype(F32)
        g_ref[...] = g
        d_ref[...], nm_ref[...], nv_ref[...] = _adam(w_ref[...], g, m_ref[...], v_ref[...])

    tile = pl.BlockSpec((None, tr, cols), lambda r, pr: (0, r, 0))
    return pl.pallas_call(
        body, name=name,
        grid_spec=pltpu.PrefetchScalarGridSpec(
            num_scalar_prefetch=1, grid=(rows // tr,),
            in_specs=[tile, tile, tile,
                      pl.BlockSpec((None, tr, cols), lambda r, pr: (pr[0], r, 0)),
                      pl.BlockSpec((3, tr, cols), lambda r, pr: (0, r, 0))],
            out_specs=[tile] * 4),
        out_shape=[jax.ShapeDtypeStruct((1, rows, cols), F32)] * 4,
        compiler_params=_params(("parallel",)),
    )(chip_arr, w, m, v, sums, got)


def _small_all_reduce_adam(part, w, m, v):
    rows = part.shape[0]

    def body(p_ref, w_ref, m_ref, v_ref, g_ref, d_ref, nm_ref, nv_ref, buf, send_sems, recv_sems):
        x, y, c = _place()
        me = 4 * x + 2 * y + c
        buf[me] = p_ref[...]
        peers = [(x, y, 1 - c)] + [(px, py, pc) for px, py in _other_chips(x, y) for pc in (c, 1 - c)]
        copies = []
        for k, peer in enumerate(peers):
            cp = pltpu.make_async_remote_copy(
                src_ref=buf.at[me], dst_ref=buf.at[me], send_sem=send_sems.at[k], recv_sem=recv_sems.at[k],
                device_id=peer, device_id_type=MESH)
            cp.start()
            copies.append(cp)
        for cp in copies:
            cp.wait()
        g = buf[0]
        for k in range(1, N_DEV):
            g = g + buf[k]
        g_ref[...] = g
        d_ref[...], nm_ref[...], nv_ref[...] = _adam(w_ref[...], g, m_ref[...], v_ref[...])

    vm = pl.BlockSpec(memory_space=pltpu.VMEM)
    return pl.pallas_call(
        body, name="gains_all_reduce_adamw",
        in_specs=[vm] * 4, out_specs=[vm] * 4,
        out_shape=[jax.ShapeDtypeStruct((rows, LANES), F32)] * 4,
        scratch_shapes=[pltpu.VMEM((N_DEV, rows, LANES), F32), pltpu.SemaphoreType.DMA((7,)),
                        pltpu.SemaphoreType.DMA((7,))],
        compiler_params=pltpu.CompilerParams(has_side_effects=True),
    )(part, w, m, v)


IN_ORDER = ("r_q", "r_k", "r_v", "r_g", "c_q", "c_kv", "k_pe", "g_ret", "g_mla")
RET_HEAD = 2 * RET_QK + 2 * RET_V


def _make_layout(d, vw, qw, ql, kl, mla_w):
    width = {"r_q": qw, "r_k": qw, "r_v": vw, "r_g": vw, "c_q": ql, "c_kv": kl, "k_pe": ROPE, "g_ret": d, "g_mla": d}
    src, o = {}, 0
    for n in IN_ORDER:
        src[n] = o
        o += width[n]
    heads = vw // RET_V
    off, pieces, o = {}, [], 0

    def put(name, w, s):
        nonlocal o
        off.setdefault(name, o)
        pieces.append((o, w, s))
        o += w

    for n in ("g_ret", "g_mla", "c_q"):
        put(n, width[n], src[n])
    for h in range(heads):
        put("heads", RET_QK, src["r_q"] + h * RET_QK)
        put("heads", RET_QK, src["r_k"] + h * RET_QK)
        put("heads", RET_V, src["r_v"] + h * RET_V)
        put("heads", RET_V, src["r_g"] + h * RET_V)
    for n in ("c_kv", "k_pe"):
        put(n, width[n], src[n])
    total = off["k_pe"] + 2 * LANES
    for n, blk in (("g_ret", d), ("g_mla", d), ("c_q", ql), ("heads", RET_HEAD), ("c_kv", kl), ("k_pe", 2 * LANES)):
        assert off[n] % blk == 0
    return {"off": off, "pieces": pieces, "total": total, "n_in": sum(width.values()),
            "ret_heads": heads, "mla_heads": mla_w // VHEAD, "q_lora": ql, "kv_lora": kl}


def _cols_to_full(g):
    n, r, c = g.shape
    return jnp.transpose(g, (1, 0, 2)).reshape(r, n * c)


def _full_to_cols(w):
    r, c = w.shape
    return jnp.transpose(w.reshape(r, N_DEV, c // N_DEV), (1, 0, 2))


def _w_in_to_mine(g, lay):
    _, rows, cols = g.shape
    parts, at = [], 0
    for o, w, s in lay["pieces"]:
        if o > at:
            parts.append(jnp.zeros((rows, o - at), g.dtype))
        while w > 0:
            k, a = divmod(s, cols)
            take = min(w, cols - a)
            parts.append(g[k, :, a:a + take])
            s, w, o = s + take, w - take, o + take
        at = o
    parts.append(jnp.zeros((rows, lay["total"] - at), g.dtype))
    return jnp.concatenate(parts, axis=1)


def _mine_to_blocks(g, lay):
    cols = lay["n_in"] // N_DEV
    by_src = sorted(lay["pieces"], key=lambda p: p[2])
    blocks = []
    for k in range(N_DEV):
        lo, hi, parts = k * cols, (k + 1) * cols, []
        for o, w, s in by_src:
            a, b = max(lo, s), min(hi, s + w)
            if a < b:
                parts.append(g[:, o + a - s:o + b - s])
        blocks.append(jnp.concatenate(parts, axis=1))
    return jnp.stack(blocks)


def _rope_tables(positions, half):
    inv = ROPE_THETA ** (-jnp.arange(half, dtype=F32) / half)
    ang = positions.astype(F32)[:, None] * inv
    return jnp.cos(ang), jnp.sin(ang)


def _pack_rows(vs):
    return jnp.concatenate([v.reshape(-1, LANES) for v in vs], axis=0)


def kernel(x, positions, norm_mix_g, w_in, ret_norm_g, w_ret_o, q_a_norm_g, w_q_b, kv_a_norm_g, w_kv_b, w_mla_o, w_out, norm_mlp_g, w_up, w_down, norm_f_g, loss_target, m_norm_mix_g, m_w_in, m_ret_norm_g, m_w_ret_o, m_q_a_norm_g, m_w_q_b, m_kv_a_norm_g, m_w_kv_b, m_w_mla_o, m_w_out, m_norm_mlp_g, m_w_up, m_w_down, m_norm_f_g, v_norm_mix_g, v_w_in, v_ret_norm_g, v_w_ret_o, v_q_a_norm_g, v_w_q_b, v_kv_a_norm_g, v_w_kv_b, v_w_mla_o, v_w_out, v_norm_mlp_g, v_w_up, v_w_down, v_norm_f_g):
    xs, tgt, pos = x[0], loss_target[0], positions[0]
    s, d = xs.shape
    mats = {"w_in": w_in[0], "w_ret_o": w_ret_o[0], "w_q_b": w_q_b[0], "w_kv_b": w_kv_b[0], "w_mla_o": w_mla_o[0],
            "w_out": w_out[0], "w_up": w_up[0], "w_down": w_down[0]}
    mat_w = {"w_in": w_in, "w_ret_o": w_ret_o, "w_q_b": w_q_b, "w_kv_b": w_kv_b, "w_mla_o": w_mla_o, "w_out": w_out,
             "w_up": w_up, "w_down": w_down}
    mat_m = {"w_in": m_w_in, "w_ret_o": m_w_ret_o, "w_q_b": m_w_q_b, "w_kv_b": m_w_kv_b, "w_mla_o": m_w_mla_o,
             "w_out": m_w_out, "w_up": m_w_up, "w_down": m_w_down}
    mat_v = {"w_in": v_w_in, "w_ret_o": v_w_ret_o, "w_q_b": v_w_q_b, "w_kv_b": v_w_kv_b, "w_mla_o": v_w_mla_o,
             "w_out": v_w_out, "w_up": v_w_up, "w_down": v_w_down}
    names = list(mats)
    col_sharded = ("w_in", "w_q_b", "w_kv_b", "w_up")
    vw = ret_norm_g.shape[1]
    mla_w = mats["w_mla_o"].shape[0] * N_DEV
    ql, kl = q_a_norm_g.shape[1], kv_a_norm_g.shape[1]
    n_in = mats["w_in"].shape[1] * N_DEV
    qw = (n_in - 2 * vw - ql - kl - ROPE - 2 * d) // 2
    lay = _make_layout(d, vw, qw, ql, kl, mla_w)
    assert lay["n_in"] == n_in
    heads_r, heads_m = lay["ret_heads"], lay["mla_heads"]

    shard16 = {n: mats[n].astype(BF16) for n in names}
    with_in_proj = ("w_ret_o", "w_q_b", "w_kv_b", "w_mla_o", "w_out")
    mlp = ("w_up", "w_down")
    by_device = ("w_up", "w_kv_b")
    full = {}

    def keep(group, gathered):
        for n, g in zip(group, gathered):
            if n not in by_device:
                g = _cols_to_full(g) if n in col_sharded else g.reshape(-1, g.shape[2])
            full[n] = g

    w_mine = _w_in_to_mine(_exchange_alone("gather_w_in", _gather_exchange([shard16["w_in"]]))[0], lay)

    c64, s64 = _rope_tables(pos, RET_QK // 2)
    cos_r = jnp.concatenate([c64, c64], axis=1)
    sin_r = jnp.concatenate([-s64, s64], axis=1)
    c32, s32 = _rope_tables(pos, ROPE // 2)
    z32, z64 = jnp.zeros_like(c32), jnp.zeros((s, LANES - ROPE), F32)
    cos_p = jnp.concatenate([c32, c32, z64], axis=1)
    sin_a = jnp.concatenate([-s32, z32, z64], axis=1)
    sin_b = jnp.concatenate([z32, s32, z64], axis=1)
    lg = jnp.log(1.0 - 2.0 ** (-5.0 - jnp.arange(heads_r, dtype=F32)))
    lgs = jnp.broadcast_to(lg[:, None, None], (heads_r, 8, LANES))

    tm = min(256, s)
    blk = min(512, s)
    t_att = min(512, s)

    u = _rms_fwd("norm_mix", xs, norm_mix_g, tm)
    proj, gathered = _mm("in_proj", u, w_mine, "nn", F32,
                         ride=_gather_exchange([shard16[n] for n in with_in_proj]))
    keep(with_in_proj, gathered)
    wq_pad = jnp.pad(full["w_q_b"].reshape(ql, heads_m, NOPE + ROPE),
                     ((0, 0), (0, 0), (0, QPAD - NOPE - ROPE))).reshape(ql, heads_m * QPAD)
    o_ret, states, ry = _ret_fwd(proj, lay, cos_r, sin_r, lgs, ret_norm_g, blk)
    y_ret = _mm("ret_out", ry, full["w_ret_o"], "nn", F32)
    cqn, ckvn, kpr = _mla_prep(proj, lay, q_a_norm_g, kv_a_norm_g, cos_p, sin_a, sin_b, tm)
    qf = _mm("q_up", cqn, wq_pad, "nn", BF16, extras=((cos_p, None), (sin_a, None), (sin_b, None)), epilogue=_q_operand)
    kv, kf = _mm("kv_up", ckvn, full["w_kv_b"], "nn", (BF16, BF16), b_by_device=True, extras=((kpr, None),),
                 epilogue=_k_operand)
    o_mla, lse, gathered = _attn_fwd(qf, kf, kv, lay, t_att, ride=_gather_exchange([shard16["w_up"]]))
    keep(("w_up",), gathered)
    gate_tile = _tile(d, 1024)
    y_mla, merged = _mm(
        "mla_out", o_mla, full["w_mla_o"], "nn", (F32, BF16), tm=512, tn=gate_tile,
        extras=((proj, lay["off"]["g_ret"] // gate_tile), (proj, lay["off"]["g_mla"] // gate_tile), y_ret),
        epilogue=lambda r, gr, gm, yr: (r, _sig(gr) * yr + _sig(gm) * r))
    h1, n2 = _mm("out_proj", merged, full["w_out"], "nn", (F32, BF16), tm=512, tn=d,
                 extras=(xs, (norm_mlp_g, "whole")), epilogue=_residual_norm)
    (z, act), gathered = _mm("mlp_up", n2, full["w_up"], "nn", (F32, BF16), b_by_device=True,
                             epilogue=lambda r: (r, jnp.square(jnp.maximum(r, 0.0))),
                             ride=_gather_exchange([shard16["w_down"]]))
    keep(("w_down",), gathered)
    dn = _mm("mlp_down", act, full["w_down"], "nn", F32, tk=4096)
    dh2, dh2_16, g_norm_f, loss_part = _final("loss_head", h1, dn, norm_f_g.reshape(1, d), tgt, tm)

    mx, my, mc = _place()
    c_arr = jnp.reshape(mc, (1,)).astype(jnp.int32)
    chip_arr = jnp.reshape(2 * mx + my, (1,)).astype(jnp.int32)
    sums, from_chips = {}, {}

    def blocks(group, grads):
        return [g if n in by_device else (_full_to_cols(g) if n in col_sharded else g.reshape((N_DEV,) + mats[n].shape))
                for n, g in zip(group, grads)]

    def pair_sums(group, mine, from_sibling):
        for n, g, r in zip(group, mine, from_sibling):
            sums[n] = _pair_sum("pair_sum_" + n, g, r, c_arr, 256)
        return [sums[n] for n in group]

    dz = _mm("mlp_down_dx", dh2_16, full["w_down"], "nt", BF16, extras=(z,),
             epilogue=lambda r, zz: (r * (2.0 * jnp.maximum(zz, 0.0)),))
    g_w_down = _mm("mlp_down_dw", act, dh2_16, "tn", BF16, tm=512, tn=d, tk=s)
    down_blocks = blocks(("w_down",), (g_w_down,))
    g_w_up, got_down = _mm("mlp_up_dw", n2, dz, "tn", BF16, tk=s, out_by_device=True,
                           ride=_sibling_exchange(down_blocks))
    dn2, got_up = _mm("mlp_up_dx", dz, full["w_up"], "nt", F32, tk=4096, b_by_device=True,
                      ride=_sibling_exchange([g_w_up]))
    mlp_sums = pair_sums(mlp, [g_w_up] + down_blocks, list(got_up) + list(got_down))
    dh1, dh1_16, g_norm_mlp = _rms_bwd("norm_mlp_bwd", dn2, h1, norm_mlp_g, dh2, tm, matmul_copy=True)
    assert lay["off"]["g_ret"] == 0 and lay["off"]["g_mla"] == d
    dy_ret, dy_mla, d_proj = _mm(
        "out_proj_dx", dh1_16, full["w_out"], "nt", (BF16, BF16, (BF16, 2, lay["total"])), tm=256, tn=d,
        extras=((proj, 0), (proj, 1), y_ret, y_mla), epilogue=_gate_grads)
    g_w_out = _mm("out_proj_dw", merged, dh1_16, "tn", BF16, tm=512, tn=d, tk=s)
    g_w_ret_o = _mm("ret_out_dw", ry, dy_ret, "tn", BF16, tm=512, tn=d, tk=s)
    g_w_mla_o = _mm("mla_out_dw", o_mla, dy_mla, "tn", BF16, tm=256, tn=d, tk=s)
    mixer = ("w_out", "w_ret_o", "w_mla_o")
    mixer_blocks = blocks(mixer, (g_w_out, g_w_ret_o, g_w_mla_o))
    d_ry, got = _mm("ret_out_dx", dy_ret, full["w_ret_o"], "nt", F32, ride=_sibling_exchange(mixer_blocks))
    mixer_sums = pair_sums(mixer, mixer_blocks, got)
    d_omla = _mm("mla_out_dx", dy_mla, full["w_mla_o"], "nt", F32)
    d_proj, g_ret_norm, got = _ret_bwd(proj, lay, cos_r, sin_r, lgs, states, d_ry, o_ret, ret_norm_g, d_proj, blk,
                                       ride=_chips_exchange(mixer_sums))
    from_chips.update(zip(mixer, got))
    delta = _attn_delta(d_omla, o_mla, lay, t_att)
    dqp, dkv, dkpe_parts, got = _attn_bwd(qf, kf, kv, lse, delta, d_omla, cos_p, sin_a, sin_b, lay, t_att,
                                          ride=_chips_exchange(mlp_sums))
    from_chips.update(zip(mlp, got))
    d_proj = _rope_key_grad(dkpe_parts, lay, d_proj, tm)
    d_cqn = _mm("q_up_dx", dqp, wq_pad, "nt", F32)
    g_wq_pad = _mm("q_up_dw", cqn, dqp, "tn", BF16)
    d_ckvn = _mm("kv_up_dx", dkv, full["w_kv_b"], "nt", F32, b_by_device=True)
    g_w_kv_b = _mm("kv_up_dw", ckvn, dkv, "tn", BF16, out_by_device=True)
    d_proj, g_q_a = _latent_norm_bwd("q_latent_bwd", proj, lay["off"]["c_q"], d_cqn, q_a_norm_g, d_proj, tm)
    d_proj, g_kv_a = _latent_norm_bwd("kv_latent_bwd", proj, lay["off"]["c_kv"], d_ckvn, kv_a_norm_g, d_proj, tm)
    g_w_mine = _mm("in_proj_dw", u, d_proj, "tn", BF16, tk=s)
    g_w_q_b = g_wq_pad.reshape(ql, heads_m, QPAD)[:, :, :NOPE + ROPE].reshape(ql, heads_m * (NOPE + ROPE))
    last = ("w_in", "w_q_b", "w_kv_b")
    last_blocks = [_mine_to_blocks(g_w_mine, lay)] + blocks(last[1:], (g_w_q_b, g_w_kv_b))
    last_sums = pair_sums(last, last_blocks, _exchange_alone("grads_to_sibling", _sibling_exchange(last_blocks)))
    du, got = _mm("in_proj_dx", d_proj, w_mine, "nt", F32, ride=_chips_exchange(last_sums))
    from_chips.update(zip(last, got))
    grad_x, g_norm_mix = _rms_bwd("norm_mix_bwd", du, xs, norm_mix_g, dh1, tm)

    upd = {n: _adamw_shard("adamw_" + n, mat_w[n], mat_m[n], mat_v[n], sums[n], from_chips[n], chip_arr, 256)
           for n in names}

    gains = [("norm_mix_g", norm_mix_g, m_norm_mix_g, v_norm_mix_g, g_norm_mix),
             ("ret_norm_g", ret_norm_g, m_ret_norm_g, v_ret_norm_g, g_ret_norm),
             ("q_a_norm_g", q_a_norm_g, m_q_a_norm_g, v_q_a_norm_g, g_q_a),
             ("kv_a_norm_g", kv_a_norm_g, m_kv_a_norm_g, v_kv_a_norm_g, g_kv_a),
             ("norm_mlp_g", norm_mlp_g, m_norm_mlp_g, v_norm_mlp_g, g_norm_mlp),
             ("norm_f_g", norm_f_g, m_norm_f_g, v_norm_f_g, g_norm_f)]
    n_rows = sum(g[1].size for g in gains) // LANES
    pad_rows = -(-(n_rows + 1) // 8) * 8 - n_rows
    tail = jnp.zeros((pad_rows, LANES), F32)
    part = jnp.concatenate([_pack_rows([g[4] for g in gains]),
                            jnp.broadcast_to(loss_part[:, :1], (1, LANES)), tail[1:]], axis=0)
    packed = [jnp.concatenate([_pack_rows([g[k] for g in gains]), tail], axis=0) for k in (1, 2, 3)]
    g_sm, d_sm, m_sm, v_sm = _small_all_reduce_adam(part, *packed)
    loss = g_sm[n_rows, 0]
    small = {}
    o = 0
    for name, w, _, _, _ in gains:
        r = w.size // LANES
        small[name] = [a[o:o + r].reshape(w.shape) for a in (g_sm, d_sm, m_sm, v_sm)]
        o += r

    order = ["norm_mix_g", "w_in", "ret_norm_g", "w_ret_o", "q_a_norm_g", "w_q_b", "kv_a_norm_g", "w_kv_b", "w_mla_o",
             "w_out", "norm_mlp_g", "w_up", "w_down", "norm_f_g"]
    outs = [loss, grad_x[None]]
    for k in range(4):
        for n in order:
            outs.append(small[n][k] if n in small else upd[n][k])
    return tuple(outs)
```

```python
import functools
import math

import jax
import jax.numpy as jnp
from jax import lax
from jax.experimental import pallas as pl
from jax.experimental.pallas import tpu as pltpu

F32 = jnp.float32
BF16 = jnp.bfloat16
MESH = pl.DeviceIdType.MESH

EPS = 1e-6
ROPE_THETA = 10000.0
CHUNK_SHIFT = 6
RET_QK = 128
RET_V = 256
NOPE = 128
ROPE = 64
VHEAD = 128
QPAD = 256
LANES = 128
N_DEV = 8
VMEM_LIMIT = 56 * 1024 * 1024

ADAM_LR = 0.001
ADAM_B1 = 0.9
ADAM_B2 = 0.999
ADAM_EPS = 1e-08
ADAM_WD = 0.01
ADAM_STEP = 10

NN = (((1,), (0,)), ((), ()))
NT = (((1,), (1,)), ((), ()))
TN = (((0,), (0,)), ((), ()))


def _dot(a, b, dims):
    return lax.dot_general(a.astype(BF16), b.astype(BF16), dims, preferred_element_type=F32)


def _tile(dim, pref):
    if dim <= pref:
        return dim
    t = (pref // LANES) * LANES
    while t >= LANES:
        if dim % t == 0:
            return t
        t -= LANES
    raise ValueError(f"no tile for {dim}")


def _params(sem):
    return pltpu.CompilerParams(dimension_semantics=sem, vmem_limit_bytes=VMEM_LIMIT)


def _sig(v):
    return 1.0 / (1.0 + jnp.exp(-v))


def _mm(name, a, b, mode, out_dtypes, *, tm=1024, tn=1024, tk=2048, extras=(), epilogue=None, ride=None,
        b_by_device=False, out_by_device=False):
    if b_by_device:
        b_cols = b.shape[2]
        b_shape = (b.shape[1], N_DEV * b_cols)
    else:
        b_shape = b.shape
    if mode == "nn":
        (m, k), (_, n) = a.shape, b_shape
    elif mode == "nt":
        (m, k), (n, _) = a.shape, b_shape
    else:
        (k, m), (_, n) = a.shape, b_shape
    tm, tn, tk = _tile(m, tm), _tile(n, tn), _tile(k, tk)
    if b_by_device and mode != "nt":
        tn = _tile(b_cols, tn)
    if out_by_device:
        tn = _tile(n // N_DEV, tn)
    nk = k // tk
    dims = {"nn": NN, "nt": NT, "tn": TN}[mode]
    a_spec = (pl.BlockSpec((tk, tm), lambda i, j, kk: (kk, i)) if mode == "tn"
              else pl.BlockSpec((tm, tk), lambda i, j, kk: (i, kk)))
    if b_by_device and mode == "nt":
        piece = min(tk, b_cols)
        n_b, per = tk // piece, b_cols // piece
        b_specs = [pl.BlockSpec((None, tn, piece),
                                lambda i, j, kk, p=p: ((kk * n_b + p) // per, j, (kk * n_b + p) % per))
                   for p in range(n_b)]
    elif b_by_device:
        per = b_cols // tn
        n_b, piece = 1, tk
        b_specs = [pl.BlockSpec((None, tk, tn), lambda i, j, kk: (j // per, kk, j % per))]
    else:
        n_b, piece = 1, tk
        b_specs = [pl.BlockSpec((tn, tk), lambda i, j, kk: (j, kk)) if mode == "nt"
                   else pl.BlockSpec((tk, tn), lambda i, j, kk: (kk, j))]
    tile_spec = pl.BlockSpec((tm, tn), lambda i, j, kk: (i, j))
    if out_by_device:
        per_out = n // N_DEV // tn
        out_spec = pl.BlockSpec((None, tm, tn), lambda i, j, kk: (j // per_out, i, j % per_out))
        out_dims = (N_DEV, m, n // N_DEV)
    else:
        out_spec, out_dims = tile_spec, (m, n)
    ex_arrays, ex_specs = [], []
    for e in extras:
        arr, off = e if isinstance(e, tuple) else (e, 0)
        ex_arrays.append(arr)
        if off == "whole":
            ex_specs.append(pl.BlockSpec(arr.shape, lambda i, j, kk, nd=arr.ndim: (0,) * nd))
        elif off is None:
            ex_specs.append(pl.BlockSpec((tm, arr.shape[1]), lambda i, j, kk: (i, 0)))
        else:
            ex_specs.append(pl.BlockSpec((tm, tn), lambda i, j, kk, off=off: (i, j + off)))
    n_ex = len(extras)
    single = not isinstance(out_dtypes, (tuple, list))
    dts = (out_dtypes,) if single else tuple(out_dtypes)
    out_specs, out_shapes = [], []
    for dt in dts:
        if isinstance(dt, tuple):
            dt, mult, width = dt
            out_specs.append(pl.BlockSpec((tm, mult * tn), lambda i, j, kk: (i, j)))
            out_shapes.append(jax.ShapeDtypeStruct((m, width), dt))
        else:
            out_specs.append(out_spec)
            out_shapes.append(jax.ShapeDtypeStruct(out_dims, dt))

    grid = (m // tm, n // tn, nk)
    r_in, r_out, r_sc = ride.counts() if ride else (0, 0, 0)
    n_acc = 1 if nk > 1 else 0

    def body(a_ref, *rest):
        b_refs, rest = rest[:n_b], rest[n_b:]
        ex, rest = rest[:n_ex], rest[n_ex:]
        ride_in, rest = rest[:r_in], rest[r_in:]
        outs, rest = rest[:len(dts)], rest[len(dts):]
        ride_out, rest = rest[:r_out], rest[r_out:]
        ride_scratch = rest[n_acc:]
        if ride:
            first, mid, last = _steps([pl.program_id(d) for d in range(3)], grid)
            ride.run(ride_in, ride_out, ride_scratch, (first, mid, None))

        def finish(r):
            vals = (r,) if epilogue is None else epilogue(r, *[e[...] for e in ex])
            for o, v in zip(outs, vals):
                o[...] = v.astype(o.dtype)

        if n_b == 1:
            part = _dot(a_ref[...], b_refs[0][...], dims)
        else:
            part = sum(_dot(a_ref[:, p * piece:(p + 1) * piece], b_refs[p][...], dims) for p in range(n_b))
        if nk == 1:
            finish(part)
        else:
            acc = rest[0]
            kk = pl.program_id(2)

            @pl.when(kk == 0)
            def _():
                acc[...] = part

            @pl.when(jnp.logical_and(kk > 0, kk < nk - 1))
            def _():
                acc[...] += part

            @pl.when(kk == nk - 1)
            def _():
                finish(acc[...] + part)

        if ride:
            ride.run(ride_in, ride_out, ride_scratch, (None, None, last))

    res = pl.pallas_call(
        body, name=name, grid=grid,
        in_specs=[a_spec] + b_specs + ex_specs + [ANY] * r_in,
        out_specs=out_specs + [ANY] * r_out,
        out_shape=out_shapes + (ride.out_shape if ride else []),
        scratch_shapes=([pltpu.VMEM((tm, tn), F32)] if nk > 1 else []) + (ride.scratch if ride else []),
        compiler_params=_params(("arbitrary",) * 3 if ride else ("parallel", "parallel", "arbitrary")),
    )(a, *[b] * n_b, *ex_arrays, *(ride.ins if ride else []))
    own = res[0] if single else res[:len(dts)]
    return (own, res[len(dts):]) if ride else own


def _rows(name, body, n_rows, tm, ins, outs, accs=(), into=None):
    in_specs, args = [], []
    for t in ins:
        if len(t) == 1:
            in_specs.append(pl.BlockSpec(t[0].shape, lambda i, nd=t[0].ndim: (0,) * nd))
        else:
            in_specs.append(pl.BlockSpec((tm, t[1]), lambda i, cb=t[2]: (i, cb)))
        args.append(t[0])
    outs = [(o + (o[0], 0))[:4] for o in outs]
    out_specs = [pl.BlockSpec((tm, w), lambda i, cb=cb: (i, cb)) for w, _, _, cb in outs]
    out_shape = [jax.ShapeDtypeStruct((n_rows, total), d) for _, d, total, _ in outs]
    aliases, kernel = {}, body
    if into is not None:
        arr, w, cb = into
        in_specs.append(ANY)
        args.append(arr)
        out_specs.append(pl.BlockSpec((tm, w), lambda i: (i, cb)))
        out_shape.append(jax.ShapeDtypeStruct(arr.shape, arr.dtype))
        aliases = {len(ins): len(outs)}
        n_in = len(ins)

        def kernel(*refs):
            body(*refs[:n_in], *refs[n_in + 1:])

    out_specs += [pl.BlockSpec((r, w), lambda i: (0, 0)) for r, w in accs]
    out_shape += [jax.ShapeDtypeStruct((r, w), F32) for r, w in accs]
    return pl.pallas_call(
        kernel, name=name, grid=(n_rows // tm,), in_specs=in_specs, out_specs=out_specs, out_shape=out_shape,
        input_output_aliases=aliases, compiler_params=_params(("arbitrary",) if accs else ("parallel",)),
    )(*args)


def _zero_first(*accs):
    @pl.when(pl.program_id(0) == 0)
    def _():
        for a in accs:
            a[...] = jnp.zeros_like(a)


def _rope64(t, cos, sin):
    return t * cos + pltpu.roll(t, RET_QK // 2, 1) * sin


def _rope32(t, cos, sin_a, sin_b):
    return t * cos + pltpu.roll(t, LANES - ROPE // 2, 1) * sin_a + pltpu.roll(t, ROPE // 2, 1) * sin_b


def _rms_fwd(name, x, g, tm):
    s, d = x.shape

    def body(x_ref, g_ref, u_ref):
        v = x_ref[...]
        r = lax.rsqrt(jnp.mean(v * v, axis=-1, keepdims=True) + EPS)
        u_ref[...] = (v * r * g_ref[...]).astype(BF16)

    return _rows(name, body, s, tm, [(x, d, 0), (g,)], [(d, BF16)])[0]


def _residual_norm(r, x, g):
    h = x + r
    return h, h * lax.rsqrt(jnp.mean(h * h, axis=-1, keepdims=True) + EPS) * g


def _gate_grads(dm, gr, gm, yr, ym):
    sr, sm = _sig(gr), _sig(gm)
    return dm * sr, dm * sm, jnp.concatenate([dm * yr * (sr * (1.0 - sr)), dm * ym * (sm * (1.0 - sm))], axis=1)


def _rms_bwd(name, dy, x, g, dres, tm, matmul_copy=False):
    s, d = x.shape

    def body(dy_ref, x_ref, g_ref, dres_ref, dx_ref, *rest):
        dg_ref = rest[-1]
        _zero_first(dg_ref)
        v, dyv = x_ref[...], dy_ref[...]
        r = lax.rsqrt(jnp.mean(v * v, axis=-1, keepdims=True) + EPS)
        xh = v * r
        dxh = dyv * g_ref[...]
        dx = dres_ref[...] + r * (dxh - xh * jnp.mean(dxh * xh, axis=-1, keepdims=True))
        dx_ref[...] = dx
        if matmul_copy:
            rest[0][...] = dx.astype(BF16)
        dg_ref[...] += jnp.sum(dyv * xh, axis=0, keepdims=True)

    return _rows(name, body, s, tm, [(dy, d, 0), (x, d, 0), (g,), (dres, d, 0)],
                 [(d, F32)] + [(d, BF16)] * matmul_copy, [(1, d)])


def _final(name, h1, dn, g, tgt, tm):
    s, d = h1.shape

    def body(h_ref, dn_ref, g_ref, t_ref, dh_ref, dh16_ref, dg_ref, loss_ref):
        _zero_first(dg_ref, loss_ref)
        v = h_ref[...] + dn_ref[...]
        r = lax.rsqrt(jnp.mean(v * v, axis=-1, keepdims=True) + EPS)
        xh = v * r
        gv = g_ref[...]
        e = xh * gv - t_ref[...]
        loss_ref[...] += 0.5 * jnp.sum(jnp.mean(e * e, axis=-1, keepdims=True))
        dy = e * (1.0 / d)
        dg_ref[...] += jnp.sum(dy * xh, axis=0, keepdims=True)
        dxh = dy * gv
        dh = r * (dxh - xh * jnp.mean(dxh * xh, axis=-1, keepdims=True))
        dh_ref[...] = dh
        dh16_ref[...] = dh.astype(BF16)

    return _rows(name, body, s, tm, [(h1, d, 0), (dn, d, 0), (g,), (tgt, d, 0)], [(d, F32), (d, BF16)],
                 [(1, d), (1, LANES)])


def _decay_mask(lg, blk):
    n = lax.broadcasted_iota(jnp.int32, (blk, blk), 0)
    m = lax.broadcasted_iota(jnp.int32, (blk, blk), 1)
    w = jnp.exp(lg * jnp.abs(n - m).astype(F32))
    return jnp.where(jnp.right_shift(m, CHUNK_SHIFT) <= jnp.right_shift(n, CHUNK_SHIFT), w, 0.0)


def _decays(lg, blk):
    pos = lax.broadcasted_iota(jnp.int32, (blk, 1), 0).astype(F32)
    return jnp.exp(lg * (pos + 1.0)), jnp.exp(lg * (blk - 1.0 - pos)), jnp.exp(lg * float(blk))


def _ret_fwd(proj, lay, cos, sin, lgs, gain, blk, ride=None):
    s = proj.shape[0]
    heads = lay["ret_heads"]
    nb = s // blk
    scale = RET_QK ** -0.5
    ride_in_specs, ride_ins, ride_out_specs, ride_out_shape, ride_scratch = _ride_args(ride)

    def body(lg_ref, qkv_ref, cos_ref, sin_ref, g_ref, o_ref, st_ref, ry_ref, state, mask):
        lg = lg_ref[0:1, 0:1]

        @pl.when(pl.program_id(1) == 0)
        def _():
            state[...] = jnp.zeros_like(state)
            mask[...] = _decay_mask(lg, blk)

        a, c, gb = _decays(lg, blk)
        q = _rope64(qkv_ref[:, :RET_QK], cos_ref[...], sin_ref[...])
        k = _rope64(qkv_ref[:, RET_QK:2 * RET_QK], cos_ref[...], sin_ref[...]) * scale
        v = qkv_ref[:, 2 * RET_QK:2 * RET_QK + RET_V]
        st = state[...]
        st_ref[...] = st
        sm = _dot(q, k, NT) * mask[...]
        o = _dot(sm, v, NN) + _dot(q * a, st, NN)
        o_ref[...] = o
        state[...] = st * gb + _dot(k * c, v, TN)
        dlt = o - jnp.mean(o, axis=-1, keepdims=True)
        rstd = lax.rsqrt(jnp.mean(dlt * dlt, axis=-1, keepdims=True) + EPS)
        rg = qkv_ref[:, 2 * RET_QK + RET_V:]
        ry_ref[...] = (dlt * rstd * g_ref[...] * (rg * _sig(rg))).astype(BF16)

    first = lay["off"]["heads"] // RET_HEAD
    res = pl.pallas_call(
        _with_ride(body, ride, (heads, nb), 0, 5, 3), name="ret_fwd", grid=(heads, nb),
        in_specs=[pl.BlockSpec((None, 8, LANES), lambda h, b: (h, 0, 0)),
                  pl.BlockSpec((blk, RET_HEAD), lambda h, b: (b, first + h)),
                  pl.BlockSpec((blk, LANES), lambda h, b: (b, 0)),
                  pl.BlockSpec((blk, LANES), lambda h, b: (b, 0)),
                  pl.BlockSpec((1, RET_V), lambda h, b: (0, h))] + ride_in_specs,
        out_specs=[pl.BlockSpec((blk, RET_V), lambda h, b: (b, h)),
                   pl.BlockSpec((None, None, RET_QK, RET_V), lambda h, b: (h, b, 0, 0)),
                   pl.BlockSpec((blk, RET_V), lambda h, b: (b, h))] + ride_out_specs,
        out_shape=[jax.ShapeDtypeStruct((s, heads * RET_V), F32),
                   jax.ShapeDtypeStruct((heads, nb, RET_QK, RET_V), F32),
                   jax.ShapeDtypeStruct((s, heads * RET_V), BF16)] + ride_out_shape,
        scratch_shapes=[pltpu.VMEM((RET_QK, RET_V), F32), pltpu.VMEM((blk, blk), F32)] + ride_scratch,
        compiler_params=_params(("arbitrary", "arbitrary") if ride else ("parallel", "arbitrary")),
    )(lgs, proj, cos, sin, gain, *ride_ins)
    return (res[0], res[1], res[2], res[3:]) if ride else res


def _ret_bwd(proj, lay, cos, sin, lgs, states, d_ry, o, gain, d_proj, blk, ride=None):
    ride_in_specs, ride_ins, ride_out_specs, ride_out_shape, ride_scratch = _ride_args(ride)
    s = proj.shape[0]
    heads = lay["ret_heads"]
    nb = s // blk
    scale = RET_QK ** -0.5

    def body(lg_ref, qkv_ref, cos_ref, sin_ref, st_ref, dry_ref, o_ref, g_ref, _, dqkv_ref, dg_ref, dstate, mask):
        lg = lg_ref[0:1, 0:1]

        @pl.when(pl.program_id(1) == 0)
        def _():
            dstate[...] = jnp.zeros_like(dstate)
            mask[...] = _decay_mask(lg, blk)
            dg_ref[...] = jnp.zeros_like(dg_ref)

        oh = o_ref[...]
        dlt = oh - jnp.mean(oh, axis=-1, keepdims=True)
        rstd = lax.rsqrt(jnp.mean(dlt * dlt, axis=-1, keepdims=True) + EPS)
        oh = dlt * rstd
        gv = g_ref[...]
        rg = qkv_ref[:, 2 * RET_QK + RET_V:]
        sg = _sig(rg)
        dry = dry_ref[...]
        dt = dry * (rg * sg)
        dqkv_ref[:, 2 * RET_QK + RET_V:] = (dry * (oh * gv) * (sg * (1.0 + rg * (1.0 - sg)))).astype(BF16)
        dg_ref[...] += jnp.sum(dt * oh, axis=0, keepdims=True)
        doh = dt * gv
        do = rstd * (doh - jnp.mean(doh, axis=-1, keepdims=True) - oh * jnp.mean(doh * oh, axis=-1, keepdims=True))

        a, c, gb = _decays(lg, blk)
        cs, sn = cos_ref[...], sin_ref[...]
        q = _rope64(qkv_ref[:, :RET_QK], cs, sn)
        k = _rope64(qkv_ref[:, RET_QK:2 * RET_QK], cs, sn) * scale
        v = qkv_ref[:, 2 * RET_QK:2 * RET_QK + RET_V]
        st = st_ref[...]
        dst = dstate[...]
        mk = mask[...]
        sm = _dot(q, k, NT) * mk
        ds = _dot(do, v, NT) * mk
        dq = _dot(ds, k, NN) + _dot(do, st, NT) * a
        dk = _dot(ds, q, TN) + _dot(v, dst, NT) * c
        dqkv_ref[:, 2 * RET_QK:2 * RET_QK + RET_V] = (_dot(sm, do, TN) + _dot(k * c, dst, NN)).astype(BF16)
        dstate[...] = dst * gb + _dot(q * a, do, TN)
        dqkv_ref[:, :RET_QK] = _rope64(dq, cs, -sn).astype(BF16)
        dqkv_ref[:, RET_QK:2 * RET_QK] = (_rope64(dk, cs, -sn) * scale).astype(BF16)

    first = lay["off"]["heads"] // RET_HEAD
    last = nb - 1
    head_tile = pl.BlockSpec((blk, RET_V), lambda h, b: (last - b, h))
    res = pl.pallas_call(
        _with_ride(body, ride, (heads, nb), 0, 9, 2), name="ret_bwd", grid=(heads, nb),
        in_specs=[pl.BlockSpec((None, 8, LANES), lambda h, b: (h, 0, 0)),
                  pl.BlockSpec((blk, RET_HEAD), lambda h, b: (last - b, first + h)),
                  pl.BlockSpec((blk, LANES), lambda h, b: (last - b, 0)),
                  pl.BlockSpec((blk, LANES), lambda h, b: (last - b, 0)),
                  pl.BlockSpec((None, None, RET_QK, RET_V), lambda h, b: (h, last - b, 0, 0)),
                  head_tile, head_tile, pl.BlockSpec((1, RET_V), lambda h, b: (0, h)), ANY] + ride_in_specs,
        out_specs=[pl.BlockSpec((blk, RET_HEAD), lambda h, b: (last - b, first + h)),
                   pl.BlockSpec((1, RET_V), lambda h, b: (0, h))] + ride_out_specs,
        out_shape=[jax.ShapeDtypeStruct(d_proj.shape, d_proj.dtype),
                   jax.ShapeDtypeStruct((1, heads * RET_V), F32)] + ride_out_shape,
        scratch_shapes=[pltpu.VMEM((RET_QK, RET_V), F32), pltpu.VMEM((blk, blk), F32)] + ride_scratch,
        input_output_aliases={8: 0},
        compiler_params=_params(("arbitrary", "arbitrary") if ride else ("parallel", "arbitrary")),
    )(lgs, proj, cos, sin, states, d_ry, o, gain, d_proj, *ride_ins)
    return (res[0], res[1], res[2:]) if ride else res[:2]


def _mla_prep(proj, lay, gq, gkv, cos, sin_a, sin_b, tm):
    s = proj.shape[0]
    ql, kl = lay["q_lora"], lay["kv_lora"]

    def body(cq_ref, ckv_ref, kpe_ref, gq_ref, gkv_ref, cos_ref, sa_ref, sb_ref, cqn_ref, ckvn_ref, kpr_ref):
        for src, gref, dst in ((cq_ref, gq_ref, cqn_ref), (ckv_ref, gkv_ref, ckvn_ref)):
            v = src[...]
            r = lax.rsqrt(jnp.mean(v * v, axis=-1, keepdims=True) + EPS)
            dst[...] = (v * r * gref[...]).astype(BF16)
        kpr_ref[...] = _rope32(kpe_ref[...], cos_ref[...], sa_ref[...], sb_ref[...]).astype(BF16)

    off = lay["off"]
    return _rows("mla_prep", body, s, tm,
                 [(proj, ql, off["c_q"] // ql), (proj, kl, off["c_kv"] // kl), (proj, LANES, off["k_pe"] // LANES),
                  (gq,), (gkv,), (cos, LANES, 0), (sin_a, LANES, 0), (sin_b, LANES, 0)],
                 [(ql, BF16), (kl, BF16), (LANES, BF16)])


def _latent_norm_bwd(name, proj, offset, d_normed, g, d_proj, tm):
    s, w = d_normed.shape

    def body(dy_ref, x_ref, g_ref, dx_ref, dg_ref):
        _zero_first(dg_ref)
        v, dy = x_ref[...], dy_ref[...]
        r = lax.rsqrt(jnp.mean(v * v, axis=-1, keepdims=True) + EPS)
        xh = v * r
        dxh = dy * g_ref[...]
        dx_ref[...] = (r * (dxh - xh * jnp.mean(dxh * xh, axis=-1, keepdims=True))).astype(BF16)
        dg_ref[...] += jnp.sum(dy * xh, axis=0, keepdims=True)

    return _rows(name, body, s, tm, [(d_normed, w, 0), (proj, w, offset // w), (g,)], [], [(1, w)],
                 into=(d_proj, w, offset // w))


def _q_operand(r, cos, sin_a, sin_b):
    qs = (NOPE + ROPE) ** -0.5 * math.log2(math.e)
    cs, sa, sb = cos * qs, sin_a * qs, sin_b * qs
    parts = []
    for lo in range(0, r.shape[1], QPAD):
        parts += [r[:, lo:lo + NOPE] * qs, _rope32(r[:, lo + NOPE:lo + QPAD], cs, sa, sb)]
    return (jnp.concatenate(parts, axis=1),)


def _k_operand(r, kpr):
    parts = []
    for lo in range(0, r.shape[1], QPAD):
        parts += [r[:, lo:lo + NOPE], kpr.astype(F32)]
    return r, jnp.concatenate(parts, axis=1)


def _rope_key_grad(parts, lay, d_proj, tm):
    s, w = parts.shape

    def body(p_ref, dkpe_ref):
        dkpe_ref[:, :LANES] = sum(p_ref[:, lo:lo + LANES] for lo in range(0, w, LANES)).astype(BF16)
        dkpe_ref[:, LANES:] = jnp.zeros((tm, LANES), BF16)

    return _rows("rope_key_grad", body, s, tm, [(parts, w, 0)], [],
                 into=(d_proj, 2 * LANES, lay["off"]["k_pe"] // (2 * LANES)))[0]


def _diag_mask(t, keys_on_rows=False):
    row = lax.broadcasted_iota(jnp.int32, (t, t), 0)
    col = lax.broadcasted_iota(jnp.int32, (t, t), 1)
    key, query = (row, col) if keys_on_rows else (col, row)
    return jnp.right_shift(key, CHUNK_SHIFT) <= jnp.right_shift(query, CHUNK_SHIFT)


def _tile_pairs(nt, by_key):
    if by_key:
        pairs = [(i, j) for j in range(nt) for i in range(j, nt)]
    else:
        pairs = [(i, j) for i in range(nt) for j in range(i + 1)]
    return (jnp.asarray([p[0] for p in pairs], jnp.int32), jnp.asarray([p[1] for p in pairs], jnp.int32))


def _head_block(heads):
    return 4 if heads % 4 == 0 else 2 if heads % 2 == 0 else 1


def _attn_fwd(qf, kf, kv, lay, t, ride=None):
    s = qf.shape[0]
    heads = lay["mla_heads"]
    hb = _head_block(heads)
    nt = s // t
    qi, kj = _tile_pairs(nt, False)
    grid = (heads // hb, int(qi.shape[0]))
    ride_in_specs, ride_ins, ride_out_specs, ride_out_shape, ride_scratch = _ride_args(ride)

    def body(qi_ref, kj_ref, q_ref, k_ref, kv_ref, o_ref, lse_ref, m_s, l_s, acc):
        p = pl.program_id(1)
        i, j = qi_ref[p], kj_ref[p]

        @pl.when(j == 0)
        def _():
            m_s[...] = jnp.full_like(m_s, -jnp.inf)
            l_s[...] = jnp.zeros_like(l_s)
            acc[...] = jnp.zeros_like(acc)

        def step(diagonal):
            ones = jnp.ones((t, LANES), BF16)
            scores = [_dot(q_ref[:, hh * QPAD:(hh + 1) * QPAD], k_ref[:, hh * QPAD:(hh + 1) * QPAD], NT)
                      for hh in range(hb)]
            for hh in range(hb):
                sc = scores[hh]
                if diagonal:
                    sc = jnp.where(_diag_mask(t), sc, -jnp.inf)
                cols = [sc[:, c * LANES:(c + 1) * LANES] for c in range(t // LANES)]
                m_old = m_s[hh]
                m_new = jnp.maximum(m_old, jnp.max(functools.reduce(jnp.maximum, cols), axis=-1, keepdims=True))
                alpha = jnp.exp2(m_old - m_new)
                pr = jnp.concatenate([jnp.exp2(c - m_new).astype(BF16) for c in cols], axis=1)
                pv = _dot(pr, jnp.concatenate([kv_ref[:, hh * QPAD + NOPE:(hh + 1) * QPAD], ones], axis=1), NN)
                l_new = alpha * l_s[hh] + pv[:, VHEAD:]
                a_new = alpha * acc[hh] + pv[:, :VHEAD]
                if diagonal:
                    o_ref[:, hh * VHEAD:(hh + 1) * VHEAD] = a_new / l_new
                    lse_ref[hh] = jnp.transpose(m_new + jnp.log2(l_new))[:1]
                else:
                    m_s[hh], l_s[hh], acc[hh] = m_new, l_new, a_new

        pl.when(j < i)(functools.partial(step, False))
        pl.when(j == i)(functools.partial(step, True))

    res = pl.pallas_call(
        _with_ride(body, ride, grid, 2, 3, 2), name="attn_fwd",
        grid_spec=pltpu.PrefetchScalarGridSpec(
            num_scalar_prefetch=2, grid=grid,
            in_specs=[pl.BlockSpec((t, hb * QPAD), lambda h, p, qi, kj: (qi[p], h)),
                      pl.BlockSpec((t, hb * QPAD), lambda h, p, qi, kj: (kj[p], h)),
                      pl.BlockSpec((t, hb * QPAD), lambda h, p, qi, kj: (kj[p], h))] + ride_in_specs,
            out_specs=[pl.BlockSpec((t, hb * VHEAD), lambda h, p, qi, kj: (qi[p], h)),
                       pl.BlockSpec((hb, 1, t), lambda h, p, qi, kj: (h, 0, qi[p]))] + ride_out_specs,
            scratch_shapes=[pltpu.VMEM((hb, t, LANES), F32), pltpu.VMEM((hb, t, LANES), F32),
                            pltpu.VMEM((hb, t, VHEAD), F32)] + ride_scratch),
        out_shape=[jax.ShapeDtypeStruct((s, heads * VHEAD), F32),
                   jax.ShapeDtypeStruct((heads, 1, s), F32)] + ride_out_shape,
        compiler_params=_params(("arbitrary", "arbitrary") if ride else ("parallel", "arbitrary")),
    )(qi, kj, qf, kf, kv, *ride_ins)
    return (res[0], res[1], res[2:]) if ride else res


def _attn_delta(d_o, o, lay, tm):
    s = o.shape[0]
    heads = lay["mla_heads"]

    def body(do_ref, o_ref, dl_ref):
        for h in range(heads):
            sl = slice(h * VHEAD, (h + 1) * VHEAD)
            dl_ref[h] = jnp.sum(jnp.transpose(do_ref[:, sl] * o_ref[:, sl]), axis=0, keepdims=True)

    tile = pl.BlockSpec((tm, heads * VHEAD), lambda i: (i, 0))
    return pl.pallas_call(
        body, name="attn_delta", grid=(s // tm,), in_specs=[tile, tile],
        out_specs=pl.BlockSpec((heads, 1, tm), lambda i: (0, 0, i)),
        out_shape=jax.ShapeDtypeStruct((heads, 1, s), F32),
        compiler_params=_params(("parallel",)),
    )(d_o, o)


def _attn_bwd(qf, kf, kv, lse, delta, d_o, cos, sin_a, sin_b, lay, t, ride=None):
    s = qf.shape[0]
    heads = lay["mla_heads"]
    hb = _head_block(heads)
    nt = s // t
    scale = (NOPE + ROPE) ** -0.5
    qi, kj = _tile_pairs(nt, True)
    grid = (heads // hb, int(qi.shape[0]))
    ride_in_specs, ride_ins, ride_out_specs, ride_out_shape, ride_scratch = _ride_args(ride)

    def body(qi_ref, kj_ref, q_ref, k_ref, kv_ref, lse_ref, dl_ref, do_ref, cos_ref, sa_ref, sb_ref,
             dqp_ref, dkv_ref, dkpe_ref, dq_acc, dk_acc, dv_acc):
        p = pl.program_id(1)
        i, j = qi_ref[p], kj_ref[p]
        rows = pl.ds(pl.multiple_of(i * t, t), t)

        def unrope(v):
            return _rope32(v, cos_ref[...], -sa_ref[...], -sb_ref[...])

        @pl.when(p == 0)
        def _():
            dq_acc[...] = jnp.zeros_like(dq_acc)

        def step(diagonal):
            for hh in range(hb):
                lo = hh * QPAD
                q, k = q_ref[:, lo:lo + QPAD], k_ref[:, lo:lo + QPAD]
                do = do_ref[:, hh * VHEAD:(hh + 1) * VHEAD]
                pr = jnp.exp2(_dot(k, q, NT) - lse_ref[hh])
                if diagonal:
                    pr = jnp.where(_diag_mask(t, keys_on_rows=True), pr, 0.0)
                dv_part = _dot(pr, do, NN)
                ds = (pr * (_dot(kv_ref[:, lo + NOPE:lo + QPAD], do, NT) - dl_ref[hh])).astype(BF16)
                dk_part = _dot(ds, q, NN)
                dq = dq_acc[rows, lo:lo + QPAD] + _dot(ds, k, TN) * scale
                if diagonal:
                    dqp_ref[:, lo:lo + NOPE] = dq[:, :NOPE].astype(BF16)
                    dqp_ref[:, lo + NOPE:lo + QPAD] = unrope(dq[:, NOPE:]).astype(BF16)
                    dk_acc[hh], dv_acc[hh] = dk_part, dv_part
                else:
                    dq_acc[rows, lo:lo + QPAD] = dq
                    dk_acc[hh] += dk_part
                    dv_acc[hh] += dv_part

        pl.when(i > j)(functools.partial(step, False))
        pl.when(i == j)(functools.partial(step, True))

        @pl.when(i == nt - 1)
        def _():
            kpe = jnp.zeros((t, LANES), F32)
            for hh in range(hb):
                lo = hh * QPAD
                dk = dk_acc[hh] * math.log(2.0)
                dkv_ref[:, lo:lo + NOPE] = dk[:, :NOPE].astype(BF16)
                dkv_ref[:, lo + NOPE:lo + QPAD] = dv_acc[hh].astype(BF16)
                kpe = kpe + dk[:, NOPE:]
            dkpe_ref[...] = unrope(kpe)

    table = pl.BlockSpec((t, LANES), lambda h, p, qi, kj: (kj[p], 0))
    res = pl.pallas_call(
        _with_ride(body, ride, grid, 2, 9, 3), name="attn_bwd",
        grid_spec=pltpu.PrefetchScalarGridSpec(
            num_scalar_prefetch=2, grid=grid,
            in_specs=[pl.BlockSpec((t, hb * QPAD), lambda h, p, qi, kj: (qi[p], h)),
                      pl.BlockSpec((t, hb * QPAD), lambda h, p, qi, kj: (kj[p], h)),
                      pl.BlockSpec((t, hb * QPAD), lambda h, p, qi, kj: (kj[p], h)),
                      pl.BlockSpec((hb, 1, t), lambda h, p, qi, kj: (h, 0, qi[p])),
                      pl.BlockSpec((hb, 1, t), lambda h, p, qi, kj: (h, 0, qi[p])),
                      pl.BlockSpec((t, hb * VHEAD), lambda h, p, qi, kj: (qi[p], h)),
                      table, table, table] + ride_in_specs,
            out_specs=[pl.BlockSpec((t, hb * QPAD), lambda h, p, qi, kj: (kj[p], h)),
                       pl.BlockSpec((t, hb * QPAD), lambda h, p, qi, kj: (kj[p], h)),
                       pl.BlockSpec((t, LANES), lambda h, p, qi, kj: (kj[p], h))] + ride_out_specs,
            scratch_shapes=[pltpu.VMEM((s, hb * QPAD), F32), pltpu.VMEM((hb, t, QPAD), F32),
                            pltpu.VMEM((hb, t, VHEAD), F32)] + ride_scratch),
        out_shape=[jax.ShapeDtypeStruct((s, heads * QPAD), BF16),
                   jax.ShapeDtypeStruct((s, heads * QPAD), BF16),
                   jax.ShapeDtypeStruct((s, heads // hb * LANES), F32)] + ride_out_shape,
        compiler_params=_params(("arbitrary", "arbitrary") if ride else ("parallel", "arbitrary")),
    )(qi, kj, qf, kf, kv, lse, delta, d_o, cos, sin_a, sin_b, *ride_ins)
    return (res[0], res[1], res[2], res[3:]) if ride else res


ANY = pl.BlockSpec(memory_space=pl.ANY)


def _place():
    return lax.axis_index("x"), lax.axis_index("y"), lax.axis_index("c")


def _other_chips(x, y):
    return [(1 - x, y), (x, 1 - y), (1 - x, 1 - y)]


class _Exchange:
    def __init__(self, ins, out_shape, scratch, phases):
        self.ins, self.out_shape, self.scratch, self.phases = list(ins), list(out_shape), list(scratch), phases

    def counts(self):
        return len(self.ins), len(self.out_shape), len(self.scratch)

    def run(self, r_in, r_out, r_scratch, conds):
        for cond, phase in zip(conds, self.phases):
            if phase is not None and cond is not None:
                pl.when(cond)(functools.partial(phase, r_in, r_out, r_scratch))


def _steps(ids, sizes):
    lin, total = 0, 1
    for i, n in zip(ids, sizes):
        lin, total = lin * n + i, total * n
    return lin == 0, lin == total // 2, lin == total - 1


def _ride_args(ride):
    if ride is None:
        return [], [], [], [], []
    n_in, n_out, _ = ride.counts()
    return [ANY] * n_in, ride.ins, [ANY] * n_out, ride.out_shape, ride.scratch


def _with_ride(body, ride, grid, n_prefetch, n_in, n_out):
    if ride is None:
        return body
    r_in, r_out, r_sc = ride.counts()

    def hosted(*refs):
        cuts = (n_prefetch, n_in, r_in, n_out, r_out)
        parts, pos = [], 0
        for n in cuts:
            parts.append(refs[pos:pos + n])
            pos += n
        pre, ins, ride_in, outs, ride_out = parts
        scratch, ride_scratch = refs[pos:len(refs) - r_sc], refs[len(refs) - r_sc:]
        first, mid, last = _steps([pl.program_id(d) for d in range(len(grid))], grid)
        ride.run(ride_in, ride_out, ride_scratch, (first, mid, None))
        body(*pre, *ins, *outs, *scratch)
        ride.run(ride_in, ride_out, ride_scratch, (None, None, last))

    return hosted


def _exchange_alone(name, ex):
    n_in, n_out, _ = ex.counts()

    def body(*refs):
        for phase in ex.phases:
            if phase is not None:
                phase(refs[:n_in], refs[n_in:n_in + n_out], refs[n_in + n_out:])

    return pl.pallas_call(
        body, name=name, in_specs=[ANY] * n_in, out_specs=[ANY] * n_out, out_shape=ex.out_shape,
        scratch_shapes=ex.scratch)(*ex.ins)


def _gather_exchange(shards):
    nw = len(shards)

    def parts(ins, outs, sems):
        send_sems, recv_sems, local_sems = sems
        x, y, c = _place()

        def slot(px, py, pc):
            return 4 * px + 2 * py + pc

        def copy(w, k, rows, to, src=None):
            return pltpu.make_async_remote_copy(
                src_ref=rows if src is None else src, dst_ref=rows, send_sem=send_sems.at[w, k],
                recv_sem=recv_sems.at[w, k], device_id=to, device_id_type=MESH)

        def plan(w, mine):
            side = c if mine else 1 - c
            half = shards[w].shape[0] // 2
            whole = lambda px, py: outs[w].at[slot(px, py, side)]
            top = lambda px, py: outs[w].at[slot(px, py, side), pl.ds(0, half)]
            bottom = lambda px, py: outs[w].at[slot(px, py, side), pl.ds(half, half)]
            xn, yn, sib = (1 - x, y, side), (x, 1 - y, side), (x, y, 1 - side)
            own = ins[w] if mine else None
            return [copy(w, 0, whole(x, y), sib, own), copy(w, 1, whole(x, y), xn, own),
                    copy(w, 2, whole(x, y), yn, own), copy(w, 3, top(1 - x, y), yn), copy(w, 4, bottom(x, 1 - y), xn),
                    copy(w, 5, whole(1 - x, y), sib), copy(w, 6, whole(x, 1 - y), sib),
                    copy(w, 7, top(1 - x, 1 - y), sib), copy(w, 8, bottom(1 - x, 1 - y), sib)]

        def arrivals(w):
            half = shards[w].shape[0] // 2
            at = lambda px, py, *rows: outs[w].at[(slot(px, py, c),) + rows]
            return {1: copy(w, 1, at(1 - x, y), (x, y, c)), 2: copy(w, 2, at(x, 1 - y), (x, y, c)),
                    3: copy(w, 3, at(1 - x, 1 - y, pl.ds(0, half)), (x, y, c)),
                    4: copy(w, 4, at(1 - x, 1 - y, pl.ds(half, half)), (x, y, c))}

        local = [pltpu.make_async_copy(ins[w], outs[w].at[slot(x, y, c)], local_sems.at[w]) for w in range(nw)]
        return plan, arrivals, local

    def start(ins, outs, sems):
        plan, _, local = parts(ins, outs, sems)
        for cp in local:
            cp.start()
        for w in range(nw):
            for k in (0, 1, 2):
                plan(w, True)[k].start()

    def middle(ins, outs, sems):
        plan, arrivals, _ = parts(ins, outs, sems)
        for landed, onward in ((1, (3, 5)), (2, (4, 6))):
            for w in range(nw):
                arrivals(w)[landed].wait_recv()
                for k in onward:
                    plan(w, True)[k].start()

    def finish(ins, outs, sems):
        plan, arrivals, local = parts(ins, outs, sems)
        for landed, onward in ((3, 7), (4, 8)):
            for w in range(nw):
                arrivals(w)[landed].wait_recv()
                plan(w, True)[onward].start()
        for w in range(nw):
            from_sibling = plan(w, False)
            for k in (0, 5, 6, 7, 8):
                from_sibling[k].wait_recv()
            for cp in plan(w, True):
                cp.wait_send()
        for cp in local:
            cp.wait()

    return _Exchange(
        shards, [jax.ShapeDtypeStruct((N_DEV,) + s.shape, s.dtype) for s in shards],
        [pltpu.SemaphoreType.DMA((nw, 9)), pltpu.SemaphoreType.DMA((nw, 9)), pltpu.SemaphoreType.DMA((nw,))],
        (start, middle, finish))


def _sibling_exchange(grads):
    nw = len(grads)

    def copies(ins, outs, sems):
        x, y, c = _place()
        return [pltpu.make_async_remote_copy(
            src_ref=ins[w].at[2 * p + (1 - c)], dst_ref=outs[w].at[p], send_sem=sems[0].at[w, p],
            recv_sem=sems[1].at[w, p], device_id=(x, y, 1 - c), device_id_type=MESH)
            for w in range(nw) for p in range(4)]

    def start(ins, outs, sems):
        for cp in copies(ins, outs, sems):
            cp.start()

    def finish(ins, outs, sems):
        for cp in copies(ins, outs, sems):
            cp.wait()

    return _Exchange(grads, [jax.ShapeDtypeStruct((4,) + g.shape[1:], g.dtype) for g in grads],
                     [pltpu.SemaphoreType.DMA((nw, 4)), pltpu.SemaphoreType.DMA((nw, 4))], (start, None, finish))


def _chips_exchange(sums):
    nw = len(sums)

    def copies(ins, outs, sems):
        x, y, c = _place()
        return [pltpu.make_async_remote_copy(
            src_ref=ins[w].at[2 * px + py], dst_ref=outs[w].at[k], send_sem=sems[0].at[w, k],
            recv_sem=sems[1].at[w, k], device_id=(px, py, c), device_id_type=MESH)
            for w in range(nw) for k, (px, py) in enumerate(_other_chips(x, y))]

    def start(ins, outs, sems):
        for cp in copies(ins, outs, sems):
            cp.start()

    def finish(ins, outs, sems):
        for cp in copies(ins, outs, sems):
            cp.wait()

    return _Exchange(sums, [jax.ShapeDtypeStruct((3,) + g.shape[1:], g.dtype) for g in sums],
                     [pltpu.SemaphoreType.DMA((nw, 3)), pltpu.SemaphoreType.DMA((nw, 3))], (start, None, finish))


def _pair_sum(name, g, got, c_arr, tr):
    _, rows, cols = g.shape
    tr = _tile_rows(rows, tr)

    def body(c_ref, a_ref, b_ref, o_ref):
        o_ref[...] = (a_ref[...].astype(F32) + b_ref[...].astype(F32)).astype(BF16)

    return pl.pallas_call(
        body, name=name,
        grid_spec=pltpu.PrefetchScalarGridSpec(
            num_scalar_prefetch=1, grid=(4, rows // tr),
            in_specs=[pl.BlockSpec((None, tr, cols), lambda p, r, cr: (2 * p + cr[0], r, 0)),
                      pl.BlockSpec((None, tr, cols), lambda p, r, cr: (p, r, 0))],
            out_specs=pl.BlockSpec((None, tr, cols), lambda p, r, cr: (p, r, 0))),
        out_shape=jax.ShapeDtypeStruct((4, rows, cols), BF16),
        compiler_params=_params(("parallel", "parallel")),
    )(c_arr, g, got)


def _tile_rows(rows, pref):
    t = min(rows, pref)
    while rows % t or t % 8:
        t -= 1
    return t


def _adam(w, g, m, v):
    m = ADAM_B1 * m + (1.0 - ADAM_B1) * g
    v = ADAM_B2 * v + (1.0 - ADAM_B2) * (g * g)
    m_hat = m / (1.0 - ADAM_B1 ** ADAM_STEP)
    v_hat = v / (1.0 - ADAM_B2 ** ADAM_STEP)
    return -ADAM_LR * (m_hat / (jnp.sqrt(v_hat) + ADAM_EPS) + ADAM_WD * w), m, v


def _adamw_shard(name, w, m, v, sums, got, chip_arr, tr):
    _, rows, cols = w.shape
    tr = _tile_rows(rows, tr)

    def body(p_ref, w_ref, m_ref, v_ref, s_ref, r_ref, g_ref, d_ref, nm_ref, nv_ref):
        g = s_ref[...].astype(F32)
        for k in range(3):
            g = g + r_ref[k].astype(F32)
        g_ref[...] = g
        d_ref[...], nm_ref[...], nv_ref[...] = _adam(w_ref[...], g, m_ref[...], v_ref[...])

    tile = pl.BlockSpec((None, tr, cols), lambda r, pr: (0, r, 0))
    return pl.pallas_call(
        body, name=name,
        grid_spec=pltpu.PrefetchScalarGridSpec(
            num_scalar_prefetch=1, grid=(rows // tr,),
            in_specs=[tile, tile, tile,
                      pl.BlockSpec((None, tr, cols), lambda r, pr: (pr[0], r, 0)),
                      pl.BlockSpec((3, tr, cols), lambda r, pr: (0, r, 0))],
            out_specs=[tile] * 4),
        out_shape=[jax.ShapeDtypeStruct((1, rows, cols), F32)] * 4,
        compiler_params=_params(("parallel",)),
    )(chip_arr, w, m, v, sums, got)


def _gains_all_reduce_adam(grads, loss_part, ws, ms, vs):
    n = len(grads)
    widths = [g.shape[1] for g in grads]
    rows, width = -(-(n + 1) // 8) * 8, max(widths)

    def body(*refs):
        g_in, loss_in = refs[:n], refs[n]
        w_in, m_in, v_in = (refs[1 + k * n:1 + (k + 1) * n] for k in (1, 2, 3))
        outs = refs[1 + 4 * n:2 + 8 * n]
        g_out, d_out, m_out, v_out = (outs[k * n:(k + 1) * n] for k in range(4))
        loss_out = outs[4 * n]
        buf, send_sems, recv_sems = refs[2 + 8 * n:]
        x, y, c = _place()
        me = 4 * x + 2 * y + c
        buf[me] = jnp.zeros((rows, width), F32)
        for r in range(n):
            buf[me, r:r + 1, :widths[r]] = g_in[r][...]
        buf[me, n:n + 1, :LANES] = loss_in[...]
        peers = [(x, y, 1 - c)] + [(px, py, pc) for px, py in _other_chips(x, y) for pc in (c, 1 - c)]
        copies = []
        for k, peer in enumerate(peers):
            cp = pltpu.make_async_remote_copy(
                src_ref=buf.at[me], dst_ref=buf.at[me], send_sem=send_sems.at[k], recv_sem=recv_sems.at[k],
                device_id=peer, device_id_type=MESH)
            cp.start()
            copies.append(cp)
        for cp in copies:
            cp.wait()
        total = buf[0]
        for k in range(1, N_DEV):
            total = total + buf[k]
        for r in range(n):
            g = total[r:r + 1, :widths[r]]
            g_out[r][...] = g
            d_out[r][...], m_out[r][...], v_out[r][...] = _adam(w_in[r][...], g, m_in[r][...], v_in[r][...])
        loss_out[...] = total[n:n + 1, :LANES]

    vm = pl.BlockSpec(memory_space=pltpu.VMEM)
    shapes = [jax.ShapeDtypeStruct((1, w), F32) for w in widths]
    res = pl.pallas_call(
        body, name="gains_all_reduce_adamw",
        in_specs=[vm] * (4 * n + 1), out_specs=[vm] * (4 * n + 1),
        out_shape=shapes * 4 + [jax.ShapeDtypeStruct((1, LANES), F32)],
        scratch_shapes=[pltpu.VMEM((N_DEV, rows, width), F32), pltpu.SemaphoreType.DMA((7,)),
                        pltpu.SemaphoreType.DMA((7,))],
    )(*grads, loss_part, *ws, *ms, *vs)
    return res[:n], res[n:2 * n], res[2 * n:3 * n], res[3 * n:4 * n], res[4 * n]


IN_ORDER = ("r_q", "r_k", "r_v", "r_g", "c_q", "c_kv", "k_pe", "g_ret", "g_mla")
RET_HEAD = 2 * RET_QK + 2 * RET_V


def _make_layout(d, vw, qw, ql, kl, mla_w):
    width = {"r_q": qw, "r_k": qw, "r_v": vw, "r_g": vw, "c_q": ql, "c_kv": kl, "k_pe": ROPE, "g_ret": d, "g_mla": d}
    src, o = {}, 0
    for n in IN_ORDER:
        src[n] = o
        o += width[n]
    heads = vw // RET_V
    off, pieces, o = {}, [], 0

    def put(name, w, s):
        nonlocal o
        off.setdefault(name, o)
        pieces.append((o, w, s))
        o += w

    for n in ("g_ret", "g_mla", "c_q"):
        put(n, width[n], src[n])
    for h in range(heads):
        put("heads", RET_QK, src["r_q"] + h * RET_QK)
        put("heads", RET_QK, src["r_k"] + h * RET_QK)
        put("heads", RET_V, src["r_v"] + h * RET_V)
        put("heads", RET_V, src["r_g"] + h * RET_V)
    for n in ("c_kv", "k_pe"):
        put(n, width[n], src[n])
    total = off["k_pe"] + 2 * LANES
    for n, blk in (("g_ret", d), ("g_mla", d), ("c_q", ql), ("heads", RET_HEAD), ("c_kv", kl), ("k_pe", 2 * LANES)):
        assert off[n] % blk == 0
    return {"off": off, "pieces": pieces, "total": total, "n_in": sum(width.values()),
            "ret_heads": heads, "mla_heads": mla_w // VHEAD, "q_lora": ql, "kv_lora": kl}


def _cols_to_full(g):
    n, r, c = g.shape
    return jnp.transpose(g, (1, 0, 2)).reshape(r, n * c)


def _full_to_cols(w):
    r, c = w.shape
    return jnp.transpose(w.reshape(r, N_DEV, c // N_DEV), (1, 0, 2))


def _w_in_to_mine(g, lay):
    _, rows, cols = g.shape
    parts, at = [], 0
    for o, w, s in lay["pieces"]:
        if o > at:
            parts.append(jnp.zeros((rows, o - at), g.dtype))
        while w > 0:
            k, a = divmod(s, cols)
            take = min(w, cols - a)
            parts.append(g[k, :, a:a + take])
            s, w, o = s + take, w - take, o + take
        at = o
    parts.append(jnp.zeros((rows, lay["total"] - at), g.dtype))
    return jnp.concatenate(parts, axis=1)


def _mine_to_blocks(g, lay):
    cols = lay["n_in"] // N_DEV
    by_src = sorted(lay["pieces"], key=lambda p: p[2])
    blocks = []
    for k in range(N_DEV):
        lo, hi, parts = k * cols, (k + 1) * cols, []
        for o, w, s in by_src:
            a, b = max(lo, s), min(hi, s + w)
            if a < b:
                parts.append(g[:, o + a - s:o + b - s])
        blocks.append(jnp.concatenate(parts, axis=1))
    return jnp.stack(blocks)


def _rope_tables(positions, half):
    inv = ROPE_THETA ** (-jnp.arange(half, dtype=F32) / half)
    ang = positions.astype(F32)[:, None] * inv
    return jnp.cos(ang), jnp.sin(ang)


def kernel(x, positions, norm_mix_g, w_in, ret_norm_g, w_ret_o, q_a_norm_g, w_q_b, kv_a_norm_g, w_kv_b, w_mla_o, w_out, norm_mlp_g, w_up, w_down, norm_f_g, loss_target, m_norm_mix_g, m_w_in, m_ret_norm_g, m_w_ret_o, m_q_a_norm_g, m_w_q_b, m_kv_a_norm_g, m_w_kv_b, m_w_mla_o, m_w_out, m_norm_mlp_g, m_w_up, m_w_down, m_norm_f_g, v_norm_mix_g, v_w_in, v_ret_norm_g, v_w_ret_o, v_q_a_norm_g, v_w_q_b, v_kv_a_norm_g, v_w_kv_b, v_w_mla_o, v_w_out, v_norm_mlp_g, v_w_up, v_w_down, v_norm_f_g):
    xs, tgt, pos = x[0], loss_target[0], positions[0]
    s, d = xs.shape
    mats = {"w_in": w_in[0], "w_ret_o": w_ret_o[0], "w_q_b": w_q_b[0], "w_kv_b": w_kv_b[0], "w_mla_o": w_mla_o[0],
            "w_out": w_out[0], "w_up": w_up[0], "w_down": w_down[0]}
    mat_w = {"w_in": w_in, "w_ret_o": w_ret_o, "w_q_b": w_q_b, "w_kv_b": w_kv_b, "w_mla_o": w_mla_o, "w_out": w_out,
             "w_up": w_up, "w_down": w_down}
    mat_m = {"w_in": m_w_in, "w_ret_o": m_w_ret_o, "w_q_b": m_w_q_b, "w_kv_b": m_w_kv_b, "w_mla_o": m_w_mla_o,
             "w_out": m_w_out, "w_up": m_w_up, "w_down": m_w_down}
    mat_v = {"w_in": v_w_in, "w_ret_o": v_w_ret_o, "w_q_b": v_w_q_b, "w_kv_b": v_w_kv_b, "w_mla_o": v_w_mla_o,
             "w_out": v_w_out, "w_up": v_w_up, "w_down": v_w_down}
    names = list(mats)
    col_sharded = ("w_in", "w_q_b", "w_kv_b", "w_up")
    vw = ret_norm_g.shape[1]
    mla_w = mats["w_mla_o"].shape[0] * N_DEV
    ql, kl = q_a_norm_g.shape[1], kv_a_norm_g.shape[1]
    n_in = mats["w_in"].shape[1] * N_DEV
    qw = (n_in - 2 * vw - ql - kl - ROPE - 2 * d) // 2
    lay = _make_layout(d, vw, qw, ql, kl, mla_w)
    assert lay["n_in"] == n_in
    heads_r, heads_m = lay["ret_heads"], lay["mla_heads"]

    shard16 = {n: mats[n].astype(BF16) for n in names}
    with_in_proj = ("w_ret_o", "w_q_b", "w_kv_b", "w_mla_o", "w_out")
    mlp = ("w_up", "w_down")
    by_device = ("w_up", "w_kv_b")
    full = {}

    def keep(group, gathered):
        for n, g in zip(group, gathered):
            if n not in by_device:
                g = _cols_to_full(g) if n in col_sharded else g.reshape(-1, g.shape[2])
            full[n] = g

    w_mine = _w_in_to_mine(_exchange_alone("gather_w_in", _gather_exchange([shard16["w_in"]]))[0], lay)

    c64, s64 = _rope_tables(pos, RET_QK // 2)
    cos_r = jnp.concatenate([c64, c64], axis=1)
    sin_r = jnp.concatenate([-s64, s64], axis=1)
    c32, s32 = _rope_tables(pos, ROPE // 2)
    z32, z64 = jnp.zeros_like(c32), jnp.zeros((s, LANES - ROPE), F32)
    cos_p = jnp.concatenate([c32, c32, z64], axis=1)
    sin_a = jnp.concatenate([-s32, z32, z64], axis=1)
    sin_b = jnp.concatenate([z32, s32, z64], axis=1)
    lg = jnp.log(1.0 - 2.0 ** (-5.0 - jnp.arange(heads_r, dtype=F32)))
    lgs = jnp.broadcast_to(lg[:, None, None], (heads_r, 8, LANES))

    tm = min(256, s)
    blk = min(512, s)
    t_att = min(512, s)

    u = _rms_fwd("norm_mix", xs, norm_mix_g, tm)
    proj, gathered = _mm("in_proj", u, w_mine, "nn", F32,
                         ride=_gather_exchange([shard16[n] for n in with_in_proj]))
    keep(with_in_proj, gathered)
    wq_pad = jnp.pad(full["w_q_b"].reshape(ql, heads_m, NOPE + ROPE),
                     ((0, 0), (0, 0), (0, QPAD - NOPE - ROPE))).reshape(ql, heads_m * QPAD)
    o_ret, states, ry = _ret_fwd(proj, lay, cos_r, sin_r, lgs, ret_norm_g, blk)
    y_ret = _mm("ret_out", ry, full["w_ret_o"], "nn", F32)
    cqn, ckvn, kpr = _mla_prep(proj, lay, q_a_norm_g, kv_a_norm_g, cos_p, sin_a, sin_b, tm)
    qf = _mm("q_up", cqn, wq_pad, "nn", BF16, extras=((cos_p, None), (sin_a, None), (sin_b, None)), epilogue=_q_operand)
    kv, kf = _mm("kv_up", ckvn, full["w_kv_b"], "nn", (BF16, BF16), b_by_device=True, extras=((kpr, None),),
                 epilogue=_k_operand)
    o_mla, lse, gathered = _attn_fwd(qf, kf, kv, lay, t_att, ride=_gather_exchange([shard16["w_up"]]))
    keep(("w_up",), gathered)
    gate_tile = _tile(d, 1024)
    y_mla, merged = _mm(
        "mla_out", o_mla, full["w_mla_o"], "nn", (F32, BF16), tm=512, tn=gate_tile,
        extras=((proj, lay["off"]["g_ret"] // gate_tile), (proj, lay["off"]["g_mla"] // gate_tile), y_ret),
        epilogue=lambda r, gr, gm, yr: (r, _sig(gr) * yr + _sig(gm) * r))
    h1, n2 = _mm("out_proj", merged, full["w_out"], "nn", (F32, BF16), tm=512, tn=d,
                 extras=(xs, (norm_mlp_g, "whole")), epilogue=_residual_norm)
    (z, act), gathered = _mm("mlp_up", n2, full["w_up"], "nn", (F32, BF16), b_by_device=True,
                             epilogue=lambda r: (r, jnp.square(jnp.maximum(r, 0.0))),
                             ride=_gather_exchange([shard16["w_down"]]))
    keep(("w_down",), gathered)
    dn = _mm("mlp_down", act, full["w_down"], "nn", F32, tk=4096)
    dh2, dh2_16, g_norm_f, loss_part = _final("loss_head", h1, dn, norm_f_g.reshape(1, d), tgt, tm)

    mx, my, mc = _place()
    c_arr = jnp.reshape(mc, (1,)).astype(jnp.int32)
    chip_arr = jnp.reshape(2 * mx + my, (1,)).astype(jnp.int32)
    sums, from_chips = {}, {}

    def blocks(group, grads):
        return [g if n in by_device else (_full_to_cols(g) if n in col_sharded else g.reshape((N_DEV,) + mats[n].shape))
                for n, g in zip(group, grads)]

    def pair_sums(group, mine, from_sibling):
        for n, g, r in zip(group, mine, from_sibling):
            sums[n] = _pair_sum("pair_sum_" + n, g, r, c_arr, 256)
        return [sums[n] for n in group]

    dz = _mm("mlp_down_dx", dh2_16, full["w_down"], "nt", BF16, extras=(z,),
             epilogue=lambda r, zz: (r * (2.0 * jnp.maximum(zz, 0.0)),))
    g_w_down = _mm("mlp_down_dw", act, dh2_16, "tn", BF16, tm=512, tn=d, tk=s)
    down_blocks = blocks(("w_down",), (g_w_down,))
    g_w_up, got_down = _mm("mlp_up_dw", n2, dz, "tn", BF16, tk=s, out_by_device=True,
                           ride=_sibling_exchange(down_blocks))
    dn2, got_up = _mm("mlp_up_dx", dz, full["w_up"], "nt", F32, tk=4096, b_by_device=True,
                      ride=_sibling_exchange([g_w_up]))
    mlp_sums = pair_sums(mlp, [g_w_up] + down_blocks, list(got_up) + list(got_down))
    dh1, dh1_16, g_norm_mlp = _rms_bwd("norm_mlp_bwd", dn2, h1, norm_mlp_g, dh2, tm, matmul_copy=True)
    assert lay["off"]["g_ret"] == 0 and lay["off"]["g_mla"] == d
    dy_ret, dy_mla, d_proj = _mm(
        "out_proj_dx", dh1_16, full["w_out"], "nt", (BF16, BF16, (BF16, 2, lay["total"])), tm=256, tn=d,
        extras=((proj, 0), (proj, 1), y_ret, y_mla), epilogue=_gate_grads)
    g_w_out = _mm("out_proj_dw", merged, dh1_16, "tn", BF16, tm=512, tn=d, tk=s)
    g_w_ret_o = _mm("ret_out_dw", ry, dy_ret, "tn", BF16, tm=512, tn=d, tk=s)
    g_w_mla_o = _mm("mla_out_dw", o_mla, dy_mla, "tn", BF16, tm=256, tn=d, tk=s)
    mixer = ("w_out", "w_ret_o", "w_mla_o")
    mixer_blocks = blocks(mixer, (g_w_out, g_w_ret_o, g_w_mla_o))
    d_ry, got = _mm("ret_out_dx", dy_ret, full["w_ret_o"], "nt", F32, ride=_sibling_exchange(mixer_blocks))
    mixer_sums = pair_sums(mixer, mixer_blocks, got)
    d_omla = _mm("mla_out_dx", dy_mla, full["w_mla_o"], "nt", F32)
    d_proj, g_ret_norm, got = _ret_bwd(proj, lay, cos_r, sin_r, lgs, states, d_ry, o_ret, ret_norm_g, d_proj, blk,
                                       ride=_chips_exchange(mixer_sums))
    from_chips.update(zip(mixer, got))
    delta = _attn_delta(d_omla, o_mla, lay, t_att)
    dqp, dkv, dkpe_parts, got = _attn_bwd(qf, kf, kv, lse, delta, d_omla, cos_p, sin_a, sin_b, lay, t_att,
                                          ride=_chips_exchange(mlp_sums))
    from_chips.update(zip(mlp, got))
    d_proj = _rope_key_grad(dkpe_parts, lay, d_proj, tm)
    d_cqn = _mm("q_up_dx", dqp, wq_pad, "nt", F32)
    g_wq_pad = _mm("q_up_dw", cqn, dqp, "tn", BF16)
    d_ckvn = _mm("kv_up_dx", dkv, full["w_kv_b"], "nt", F32, b_by_device=True)
    g_w_kv_b = _mm("kv_up_dw", ckvn, dkv, "tn", BF16, out_by_device=True)
    d_proj, g_q_a = _latent_norm_bwd("q_latent_bwd", proj, lay["off"]["c_q"], d_cqn, q_a_norm_g, d_proj, tm)
    d_proj, g_kv_a = _latent_norm_bwd("kv_latent_bwd", proj, lay["off"]["c_kv"], d_ckvn, kv_a_norm_g, d_proj, tm)
    g_w_q_b = g_wq_pad.reshape(ql, heads_m, QPAD)[:, :, :NOPE + ROPE].reshape(ql, heads_m * (NOPE + ROPE))
    latent = ("w_q_b", "w_kv_b")
    latent_blocks = blocks(latent, (g_w_q_b, g_w_kv_b))
    latent_sums = pair_sums(latent, latent_blocks,
                            _exchange_alone("latent_grads_to_sibling", _sibling_exchange(latent_blocks)))
    g_w_mine, got = _mm("in_proj_dw", u, d_proj, "tn", BF16, tk=s, ride=_chips_exchange(latent_sums))
    from_chips.update(zip(latent, got))
    last = ("w_in",)
    last_blocks = [_mine_to_blocks(g_w_mine, lay)]
    last_sums = pair_sums(last, last_blocks, _exchange_alone("grads_to_sibling", _sibling_exchange(last_blocks)))
    du, got = _mm("in_proj_dx", d_proj, w_mine, "nt", F32, ride=_chips_exchange(last_sums))
    from_chips.update(zip(last, got))
    grad_x, g_norm_mix = _rms_bwd("norm_mix_bwd", du, xs, norm_mix_g, dh1, tm)

    upd = {n: _adamw_shard("adamw_" + n, mat_w[n], mat_m[n], mat_v[n], sums[n], from_chips[n], chip_arr, 256)
           for n in names}

    gains = [("norm_mix_g", norm_mix_g, m_norm_mix_g, v_norm_mix_g, g_norm_mix),
             ("ret_norm_g", ret_norm_g, m_ret_norm_g, v_ret_norm_g, g_ret_norm),
             ("q_a_norm_g", q_a_norm_g, m_q_a_norm_g, v_q_a_norm_g, g_q_a),
             ("kv_a_norm_g", kv_a_norm_g, m_kv_a_norm_g, v_kv_a_norm_g, g_kv_a),
             ("norm_mlp_g", norm_mlp_g, m_norm_mlp_g, v_norm_mlp_g, g_norm_mlp),
             ("norm_f_g", norm_f_g, m_norm_f_g, v_norm_f_g, g_norm_f)]
    as_row = lambda a: a.reshape(1, -1)
    g_sm, d_sm, m_sm, v_sm, loss_row = _gains_all_reduce_adam(
        [g[4] for g in gains], loss_part, *[[as_row(g[k]) for g in gains] for k in (1, 2, 3)])
    loss = loss_row[0, 0]
    small = {g[0]: [a[r].reshape(g[1].shape) for a in (g_sm, d_sm, m_sm, v_sm)] for r, g in enumerate(gains)}

    order = ["norm_mix_g", "w_in", "ret_norm_g", "w_ret_o", "q_a_norm_g", "w_q_b", "kv_a_norm_g", "w_kv_b", "w_mla_o",
             "w_out", "norm_mlp_g", "w_up", "w_down", "norm_f_g"]
    outs = [loss, grad_x[None]]
    for k in range(4):
        for n in order:
            outs.append(small[n][k] if n in small else upd[n][k])
    return tuple(outs)
```

```python
import functools
import math

import jax
import jax.numpy as jnp
from jax import lax
from jax.experimental import pallas as pl
from jax.experimental.pallas import tpu as pltpu

F32 = jnp.float32
BF16 = jnp.bfloat16
MESH = pl.DeviceIdType.MESH

EPS = 1e-6
ROPE_THETA = 10000.0
CHUNK_SHIFT = 6
RET_QK = 128
RET_V = 256
NOPE = 128
ROPE = 64
VHEAD = 128
QPAD = 256
LANES = 128
N_DEV = 8
VMEM_LIMIT = 56 * 1024 * 1024

ADAM_LR = 0.001
ADAM_B1 = 0.9
ADAM_B2 = 0.999
ADAM_EPS = 1e-08
ADAM_WD = 0.01
ADAM_STEP = 10

NN = (((1,), (0,)), ((), ()))
NT = (((1,), (1,)), ((), ()))
TN = (((0,), (0,)), ((), ()))


def _dot(a, b, dims):
    return lax.dot_general(a.astype(BF16), b.astype(BF16), dims, preferred_element_type=F32)


def _tile(dim, pref):
    if dim <= pref:
        return dim
    t = (pref // LANES) * LANES
    while t >= LANES:
        if dim % t == 0:
            return t
        t -= LANES
    raise ValueError(f"no tile for {dim}")


def _params(sem):
    return pltpu.CompilerParams(dimension_semantics=sem, vmem_limit_bytes=VMEM_LIMIT)


def _sig(v):
    return 1.0 / (1.0 + jnp.exp(-v))


def _mm(name, a, b, mode, out_dtypes, *, tm=1024, tn=1024, tk=2048, extras=(), epilogue=None, ride=None,
        b_by_device=False, out_by_device=False):
    if b_by_device:
        b_cols = b.shape[2]
        b_shape = (b.shape[1], N_DEV * b_cols)
    else:
        b_shape = b.shape
    if mode == "nn":
        (m, k), (_, n) = a.shape, b_shape
    elif mode == "nt":
        (m, k), (n, _) = a.shape, b_shape
    else:
        (k, m), (_, n) = a.shape, b_shape
    tm, tn, tk = _tile(m, tm), _tile(n, tn), _tile(k, tk)
    if b_by_device and mode != "nt":
        tn = _tile(b_cols, tn)
    if out_by_device:
        tn = _tile(n // N_DEV, tn)
    nk = k // tk
    dims = {"nn": NN, "nt": NT, "tn": TN}[mode]
    a_spec = (pl.BlockSpec((tk, tm), lambda i, j, kk: (kk, i)) if mode == "tn"
              else pl.BlockSpec((tm, tk), lambda i, j, kk: (i, kk)))
    if b_by_device and mode == "nt":
        piece = min(tk, b_cols)
        n_b, per = tk // piece, b_cols // piece
        b_specs = [pl.BlockSpec((None, tn, piece),
                                lambda i, j, kk, p=p: ((kk * n_b + p) // per, j, (kk * n_b + p) % per))
                   for p in range(n_b)]
    elif b_by_device:
        per = b_cols // tn
        n_b, piece = 1, tk
        b_specs = [pl.BlockSpec((None, tk, tn), lambda i, j, kk: (j // per, kk, j % per))]
    else:
        n_b, piece = 1, tk
        b_specs = [pl.BlockSpec((tn, tk), lambda i, j, kk: (j, kk)) if mode == "nt"
                   else pl.BlockSpec((tk, tn), lambda i, j, kk: (kk, j))]
    tile_spec = pl.BlockSpec((tm, tn), lambda i, j, kk: (i, j))
    if out_by_device:
        per_out = n // N_DEV // tn
        out_spec = pl.BlockSpec((None, tm, tn), lambda i, j, kk: (j // per_out, i, j % per_out))
        out_dims = (N_DEV, m, n // N_DEV)
    else:
        out_spec, out_dims = tile_spec, (m, n)
    ex_arrays, ex_specs = [], []
    for e in extras:
        arr, off = e if isinstance(e, tuple) else (e, 0)
        ex_arrays.append(arr)
        if off == "whole":
            ex_specs.append(pl.BlockSpec(arr.shape, lambda i, j, kk, nd=arr.ndim: (0,) * nd))
        elif off is None:
            ex_specs.append(pl.BlockSpec((tm, arr.shape[1]), lambda i, j, kk: (i, 0)))
        else:
            ex_specs.append(pl.BlockSpec((tm, tn), lambda i, j, kk, off=off: (i, j + off)))
    n_ex = len(extras)
    single = not isinstance(out_dtypes, (tuple, list))
    dts = (out_dtypes,) if single else tuple(out_dtypes)
    out_specs, out_shapes = [], []
    for dt in dts:
        if isinstance(dt, tuple):
            dt, mult, width = dt
            out_specs.append(pl.BlockSpec((tm, mult * tn), lambda i, j, kk: (i, j)))
            out_shapes.append(jax.ShapeDtypeStruct((m, width), dt))
        else:
            out_specs.append(out_spec)
            out_shapes.append(jax.ShapeDtypeStruct(out_dims, dt))

    grid = (m // tm, n // tn, nk)
    r_in, r_out, r_sc = ride.counts() if ride else (0, 0, 0)
    n_acc = 1 if nk > 1 else 0

    def body(a_ref, *rest):
        b_refs, rest = rest[:n_b], rest[n_b:]
        ex, rest = rest[:n_ex], rest[n_ex:]
        ride_in, rest = rest[:r_in], rest[r_in:]
        outs, rest = rest[:len(dts)], rest[len(dts):]
        ride_out, rest = rest[:r_out], rest[r_out:]
        ride_scratch = rest[n_acc:]
        if ride:
            first, mid, last = _steps([pl.program_id(d) for d in range(3)], grid)
            ride.run(ride_in, ride_out, ride_scratch, (first, mid, None))

        def finish(r):
            vals = (r,) if epilogue is None else epilogue(r, *[e[...] for e in ex])
            for o, v in zip(outs, vals):
                o[...] = v.astype(o.dtype)

        if n_b == 1:
            part = _dot(a_ref[...], b_refs[0][...], dims)
        else:
            part = sum(_dot(a_ref[:, p * piece:(p + 1) * piece], b_refs[p][...], dims) for p in range(n_b))
        if nk == 1:
            finish(part)
        else:
            acc = rest[0]
            kk = pl.program_id(2)

            @pl.when(kk == 0)
            def _():
                acc[...] = part

            @pl.when(jnp.logical_and(kk > 0, kk < nk - 1))
            def _():
                acc[...] += part

            @pl.when(kk == nk - 1)
            def _():
                finish(acc[...] + part)

        if ride:
            ride.run(ride_in, ride_out, ride_scratch, (None, None, last))

    res = pl.pallas_call(
        body, name=name, grid=grid,
        in_specs=[a_spec] + b_specs + ex_specs + [ANY] * r_in,
        out_specs=out_specs + [ANY] * r_out,
        out_shape=out_shapes + (ride.out_shape if ride else []),
        scratch_shapes=([pltpu.VMEM((tm, tn), F32)] if nk > 1 else []) + (ride.scratch if ride else []),
        compiler_params=_params(("arbitrary",) * 3 if ride else ("parallel", "parallel", "arbitrary")),
    )(a, *[b] * n_b, *ex_arrays, *(ride.ins if ride else []))
    own = res[0] if single else res[:len(dts)]
    return (own, res[len(dts):]) if ride else own


def _rows(name, body, n_rows, tm, ins, outs, accs=(), into=None):
    in_specs, args = [], []
    for t in ins:
        if len(t) == 1:
            in_specs.append(pl.BlockSpec(t[0].shape, lambda i, nd=t[0].ndim: (0,) * nd))
        else:
            in_specs.append(pl.BlockSpec((tm, t[1]), lambda i, cb=t[2]: (i, cb)))
        args.append(t[0])
    outs = [(o + (o[0], 0))[:4] for o in outs]
    out_specs = [pl.BlockSpec((tm, w), lambda i, cb=cb: (i, cb)) for w, _, _, cb in outs]
    out_shape = [jax.ShapeDtypeStruct((n_rows, total), d) for _, d, total, _ in outs]
    aliases, kernel = {}, body
    if into is not None:
        arr, w, cb = into
        in_specs.append(ANY)
        args.append(arr)
        out_specs.append(pl.BlockSpec((tm, w), lambda i: (i, cb)))
        out_shape.append(jax.ShapeDtypeStruct(arr.shape, arr.dtype))
        aliases = {len(ins): len(outs)}
        n_in = len(ins)

        def kernel(*refs):
            body(*refs[:n_in], *refs[n_in + 1:])

    out_specs += [pl.BlockSpec((r, w), lambda i: (0, 0)) for r, w in accs]
    out_shape += [jax.ShapeDtypeStruct((r, w), F32) for r, w in accs]
    return pl.pallas_call(
        kernel, name=name, grid=(n_rows // tm,), in_specs=in_specs, out_specs=out_specs, out_shape=out_shape,
        input_output_aliases=aliases, compiler_params=_params(("arbitrary",) if accs else ("parallel",)),
    )(*args)


def _zero_first(*accs):
    @pl.when(pl.program_id(0) == 0)
    def _():
        for a in accs:
            a[...] = jnp.zeros_like(a)


def _rope64(t, cos, sin):
    return t * cos + pltpu.roll(t, RET_QK // 2, 1) * sin


def _rope32(t, cos, sin_a, sin_b):
    return t * cos + pltpu.roll(t, LANES - ROPE // 2, 1) * sin_a + pltpu.roll(t, ROPE // 2, 1) * sin_b


def _rms_fwd(name, x, g, tm):
    s, d = x.shape

    def body(x_ref, g_ref, u_ref):
        v = x_ref[...]
        r = lax.rsqrt(jnp.mean(v * v, axis=-1, keepdims=True) + EPS)
        u_ref[...] = (v * r * g_ref[...]).astype(BF16)

    return _rows(name, body, s, tm, [(x, d, 0), (g,)], [(d, BF16)])[0]


def _residual_norm(r, x, g):
    h = x + r
    return h, h * lax.rsqrt(jnp.mean(h * h, axis=-1, keepdims=True) + EPS) * g


def _gate_grads(dm, gr, gm, yr, ym):
    sr, sm = _sig(gr), _sig(gm)
    return dm * sr, dm * sm, jnp.concatenate([dm * yr * (sr * (1.0 - sr)), dm * ym * (sm * (1.0 - sm))], axis=1)


def _rms_bwd(name, dy, x, g, dres, tm, matmul_copy=False):
    s, d = x.shape

    def body(dy_ref, x_ref, g_ref, dres_ref, dx_ref, *rest):
        dg_ref = rest[-1]
        _zero_first(dg_ref)
        v, dyv = x_ref[...], dy_ref[...]
        r = lax.rsqrt(jnp.mean(v * v, axis=-1, keepdims=True) + EPS)
        xh = v * r
        dxh = dyv * g_ref[...]
        dx = dres_ref[...] + r * (dxh - xh * jnp.mean(dxh * xh, axis=-1, keepdims=True))
        dx_ref[...] = dx
        if matmul_copy:
            rest[0][...] = dx.astype(BF16)
        dg_ref[...] += jnp.sum(dyv * xh, axis=0, keepdims=True)

    return _rows(name, body, s, tm, [(dy, d, 0), (x, d, 0), (g,), (dres, d, 0)],
                 [(d, F32)] + [(d, BF16)] * matmul_copy, [(1, d)])


def _final(name, h1, dn, g, tgt, tm):
    s, d = h1.shape

    def body(h_ref, dn_ref, g_ref, t_ref, dh_ref, dh16_ref, dg_ref, loss_ref):
        _zero_first(dg_ref, loss_ref)
        v = h_ref[...] + dn_ref[...]
        r = lax.rsqrt(jnp.mean(v * v, axis=-1, keepdims=True) + EPS)
        xh = v * r
        gv = g_ref[...]
        e = xh * gv - t_ref[...]
        loss_ref[...] += 0.5 * jnp.sum(jnp.mean(e * e, axis=-1, keepdims=True))
        dy = e * (1.0 / d)
        dg_ref[...] += jnp.sum(dy * xh, axis=0, keepdims=True)
        dxh = dy * gv
        dh = r * (dxh - xh * jnp.mean(dxh * xh, axis=-1, keepdims=True))
        dh_ref[...] = dh
        dh16_ref[...] = dh.astype(BF16)

    return _rows(name, body, s, tm, [(h1, d, 0), (dn, d, 0), (g,), (tgt, d, 0)], [(d, F32), (d, BF16)],
                 [(1, d), (1, LANES)])


def _decay_mask(lg, blk):
    n = lax.broadcasted_iota(jnp.int32, (blk, blk), 0)
    m = lax.broadcasted_iota(jnp.int32, (blk, blk), 1)
    w = jnp.exp(lg * jnp.abs(n - m).astype(F32))
    return jnp.where(jnp.right_shift(m, CHUNK_SHIFT) <= jnp.right_shift(n, CHUNK_SHIFT), w, 0.0)


def _decays(lg, blk):
    pos = lax.broadcasted_iota(jnp.int32, (blk, 1), 0).astype(F32)
    return jnp.exp(lg * (pos + 1.0)), jnp.exp(lg * (blk - 1.0 - pos)), jnp.exp(lg * float(blk))


def _ret_fwd(proj, lay, cos, sin, lgs, gain, blk, ride=None):
    s = proj.shape[0]
    heads = lay["ret_heads"]
    nb = s // blk
    scale = RET_QK ** -0.5
    ride_in_specs, ride_ins, ride_out_specs, ride_out_shape, ride_scratch = _ride_args(ride)

    def body(lg_ref, qkv_ref, cos_ref, sin_ref, g_ref, o_ref, st_ref, ry_ref, state, mask):
        lg = lg_ref[0:1, 0:1]

        @pl.when(pl.program_id(1) == 0)
        def _():
            state[...] = jnp.zeros_like(state)
            mask[...] = _decay_mask(lg, blk)

        a, c, gb = _decays(lg, blk)
        q = _rope64(qkv_ref[:, :RET_QK], cos_ref[...], sin_ref[...])
        k = _rope64(qkv_ref[:, RET_QK:2 * RET_QK], cos_ref[...], sin_ref[...]) * scale
        v = qkv_ref[:, 2 * RET_QK:2 * RET_QK + RET_V]
        st = state[...]
        st_ref[...] = st
        sm = _dot(q, k, NT) * mask[...]
        o = _dot(sm, v, NN) + _dot(q * a, st, NN)
        o_ref[...] = o
        state[...] = st * gb + _dot(k * c, v, TN)
        dlt = o - jnp.mean(o, axis=-1, keepdims=True)
        rstd = lax.rsqrt(jnp.mean(dlt * dlt, axis=-1, keepdims=True) + EPS)
        rg = qkv_ref[:, 2 * RET_QK + RET_V:]
        ry_ref[...] = (dlt * rstd * g_ref[...] * (rg * _sig(rg))).astype(BF16)

    first = lay["off"]["heads"] // RET_HEAD
    res = pl.pallas_call(
        _with_ride(body, ride, (heads, nb), 0, 5, 3), name="ret_fwd", grid=(heads, nb),
        in_specs=[pl.BlockSpec((None, 8, LANES), lambda h, b: (h, 0, 0)),
                  pl.BlockSpec((blk, RET_HEAD), lambda h, b: (b, first + h)),
                  pl.BlockSpec((blk, LANES), lambda h, b: (b, 0)),
                  pl.BlockSpec((blk, LANES), lambda h, b: (b, 0)),
                  pl.BlockSpec((1, RET_V), lambda h, b: (0, h))] + ride_in_specs,
        out_specs=[pl.BlockSpec((blk, RET_V), lambda h, b: (b, h)),
                   pl.BlockSpec((None, None, RET_QK, RET_V), lambda h, b: (h, b, 0, 0)),
                   pl.BlockSpec((blk, RET_V), lambda h, b: (b, h))] + ride_out_specs,
        out_shape=[jax.ShapeDtypeStruct((s, heads * RET_V), F32),
                   jax.ShapeDtypeStruct((heads, nb, RET_QK, RET_V), F32),
                   jax.ShapeDtypeStruct((s, heads * RET_V), BF16)] + ride_out_shape,
        scratch_shapes=[pltpu.VMEM((RET_QK, RET_V), F32), pltpu.VMEM((blk, blk), F32)] + ride_scratch,
        compiler_params=_params(("arbitrary", "arbitrary") if ride else ("parallel", "arbitrary")),
    )(lgs, proj, cos, sin, gain, *ride_ins)
    return (res[0], res[1], res[2], res[3:]) if ride else res


def _ret_bwd(proj, lay, cos, sin, lgs, states, d_ry, o, gain, d_proj, blk, ride=None):
    ride_in_specs, ride_ins, ride_out_specs, ride_out_shape, ride_scratch = _ride_args(ride)
    s = proj.shape[0]
    heads = lay["ret_heads"]
    nb = s // blk
    scale = RET_QK ** -0.5

    def body(lg_ref, qkv_ref, cos_ref, sin_ref, st_ref, dry_ref, o_ref, g_ref, _, dqkv_ref, dg_ref, dstate, mask):
        lg = lg_ref[0:1, 0:1]

        @pl.when(pl.program_id(1) == 0)
        def _():
            dstate[...] = jnp.zeros_like(dstate)
            mask[...] = _decay_mask(lg, blk)
            dg_ref[...] = jnp.zeros_like(dg_ref)

        oh = o_ref[...]
        dlt = oh - jnp.mean(oh, axis=-1, keepdims=True)
        rstd = lax.rsqrt(jnp.mean(dlt * dlt, axis=-1, keepdims=True) + EPS)
        oh = dlt * rstd
        gv = g_ref[...]
        rg = qkv_ref[:, 2 * RET_QK + RET_V:]
        sg = _sig(rg)
        dry = dry_ref[...]
        dt = dry * (rg * sg)
        dqkv_ref[:, 2 * RET_QK + RET_V:] = (dry * (oh * gv) * (sg * (1.0 + rg * (1.0 - sg)))).astype(BF16)
        dg_ref[...] += jnp.sum(dt * oh, axis=0, keepdims=True)
        doh = dt * gv
        do = rstd * (doh - jnp.mean(doh, axis=-1, keepdims=True) - oh * jnp.mean(doh * oh, axis=-1, keepdims=True))

        a, c, gb = _decays(lg, blk)
        cs, sn = cos_ref[...], sin_ref[...]
        q = _rope64(qkv_ref[:, :RET_QK], cs, sn)
        k = _rope64(qkv_ref[:, RET_QK:2 * RET_QK], cs, sn) * scale
        v = qkv_ref[:, 2 * RET_QK:2 * RET_QK + RET_V]
        st = st_ref[...]
        dst = dstate[...]
        mk = mask[...]
        sm = _dot(q, k, NT) * mk
        ds = _dot(do, v, NT) * mk
        dq = _dot(ds, k, NN) + _dot(do, st, NT) * a
        dk = _dot(ds, q, TN) + _dot(v, dst, NT) * c
        dqkv_ref[:, 2 * RET_QK:2 * RET_QK + RET_V] = (_dot(sm, do, TN) + _dot(k * c, dst, NN)).astype(BF16)
        dstate[...] = dst * gb + _dot(q * a, do, TN)
        dqkv_ref[:, :RET_QK] = _rope64(dq, cs, -sn).astype(BF16)
        dqkv_ref[:, RET_QK:2 * RET_QK] = (_rope64(dk, cs, -sn) * scale).astype(BF16)

    first = lay["off"]["heads"] // RET_HEAD
    last = nb - 1
    head_tile = pl.BlockSpec((blk, RET_V), lambda h, b: (last - b, h))
    res = pl.pallas_call(
        _with_ride(body, ride, (heads, nb), 0, 9, 2), name="ret_bwd", grid=(heads, nb),
        in_specs=[pl.BlockSpec((None, 8, LANES), lambda h, b: (h, 0, 0)),
                  pl.BlockSpec((blk, RET_HEAD), lambda h, b: (last - b, first + h)),
                  pl.BlockSpec((blk, LANES), lambda h, b: (last - b, 0)),
                  pl.BlockSpec((blk, LANES), lambda h, b: (last - b, 0)),
                  pl.BlockSpec((None, None, RET_QK, RET_V), lambda h, b: (h, last - b, 0, 0)),
                  head_tile, head_tile, pl.BlockSpec((1, RET_V), lambda h, b: (0, h)), ANY] + ride_in_specs,
        out_specs=[pl.BlockSpec((blk, RET_HEAD), lambda h, b: (last - b, first + h)),
                   pl.BlockSpec((1, RET_V), lambda h, b: (0, h))] + ride_out_specs,
        out_shape=[jax.ShapeDtypeStruct(d_proj.shape, d_proj.dtype),
                   jax.ShapeDtypeStruct((1, heads * RET_V), F32)] + ride_out_shape,
        scratch_shapes=[pltpu.VMEM((RET_QK, RET_V), F32), pltpu.VMEM((blk, blk), F32)] + ride_scratch,
        input_output_aliases={8: 0},
        compiler_params=_params(("arbitrary", "arbitrary") if ride else ("parallel", "arbitrary")),
    )(lgs, proj, cos, sin, states, d_ry, o, gain, d_proj, *ride_ins)
    return (res[0], res[1], res[2:]) if ride else res[:2]


def _mla_prep(proj, lay, gq, gkv, cos, sin_a, sin_b, tm):
    s = proj.shape[0]
    ql, kl = lay["q_lora"], lay["kv_lora"]

    def body(cq_ref, ckv_ref, kpe_ref, gq_ref, gkv_ref, cos_ref, sa_ref, sb_ref, cqn_ref, ckvn_ref, kpr_ref):
        for src, gref, dst in ((cq_ref, gq_ref, cqn_ref), (ckv_ref, gkv_ref, ckvn_ref)):
            v = src[...]
            r = lax.rsqrt(jnp.mean(v * v, axis=-1, keepdims=True) + EPS)
            dst[...] = (v * r * gref[...]).astype(BF16)
        kpr_ref[...] = _rope32(kpe_ref[...], cos_ref[...], sa_ref[...], sb_ref[...]).astype(BF16)

    off = lay["off"]
    return _rows("mla_prep", body, s, tm,
                 [(proj, ql, off["c_q"] // ql), (proj, kl, off["c_kv"] // kl), (proj, LANES, off["k_pe"] // LANES),
                  (gq,), (gkv,), (cos, LANES, 0), (sin_a, LANES, 0), (sin_b, LANES, 0)],
                 [(ql, BF16), (kl, BF16), (LANES, BF16)])


def _latent_norm_bwd(name, proj, offset, d_normed, g, d_proj, tm):
    s, w = d_normed.shape

    def body(dy_ref, x_ref, g_ref, dx_ref, dg_ref):
        _zero_first(dg_ref)
        v, dy = x_ref[...], dy_ref[...]
        r = lax.rsqrt(jnp.mean(v * v, axis=-1, keepdims=True) + EPS)
        xh = v * r
        dxh = dy * g_ref[...]
        dx_ref[...] = (r * (dxh - xh * jnp.mean(dxh * xh, axis=-1, keepdims=True))).astype(BF16)
        dg_ref[...] += jnp.sum(dy * xh, axis=0, keepdims=True)

    return _rows(name, body, s, tm, [(d_normed, w, 0), (proj, w, offset // w), (g,)], [], [(1, w)],
                 into=(d_proj, w, offset // w))


def _q_operand(r, cos, sin_a, sin_b):
    qs = (NOPE + ROPE) ** -0.5 * math.log2(math.e)
    cs, sa, sb = cos * qs, sin_a * qs, sin_b * qs
    parts = []
    for lo in range(0, r.shape[1], QPAD):
        parts += [r[:, lo:lo + NOPE] * qs, _rope32(r[:, lo + NOPE:lo + QPAD], cs, sa, sb)]
    return (jnp.concatenate(parts, axis=1),)


def _k_operand(r, kpr):
    parts = []
    for lo in range(0, r.shape[1], QPAD):
        parts += [r[:, lo:lo + NOPE], kpr.astype(F32)]
    return r, jnp.concatenate(parts, axis=1)


def _rope_key_grad(parts, lay, d_proj, tm):
    s, w = parts.shape

    def body(p_ref, dkpe_ref):
        dkpe_ref[:, :LANES] = sum(p_ref[:, lo:lo + LANES] for lo in range(0, w, LANES)).astype(BF16)
        dkpe_ref[:, LANES:] = jnp.zeros((tm, LANES), BF16)

    return _rows("rope_key_grad", body, s, tm, [(parts, w, 0)], [],
                 into=(d_proj, 2 * LANES, lay["off"]["k_pe"] // (2 * LANES)))[0]


def _diag_mask(t, keys_on_rows=False):
    row = lax.broadcasted_iota(jnp.int32, (t, t), 0)
    col = lax.broadcasted_iota(jnp.int32, (t, t), 1)
    key, query = (row, col) if keys_on_rows else (col, row)
    return jnp.right_shift(key, CHUNK_SHIFT) <= jnp.right_shift(query, CHUNK_SHIFT)


def _tile_pairs(nt, by_key):
    if by_key:
        pairs = [(i, j) for j in range(nt) for i in range(j, nt)]
    else:
        pairs = [(i, j) for i in range(nt) for j in range(i + 1)]
    return (jnp.asarray([p[0] for p in pairs], jnp.int32), jnp.asarray([p[1] for p in pairs], jnp.int32))


def _head_block(heads):
    return 4 if heads % 4 == 0 else 2 if heads % 2 == 0 else 1


def _attn_fwd(qf, kf, kv, lay, t, ride=None):
    s = qf.shape[0]
    heads = lay["mla_heads"]
    hb = _head_block(heads)
    nt = s // t
    qi, kj = _tile_pairs(nt, False)
    grid = (heads // hb, int(qi.shape[0]))
    ride_in_specs, ride_ins, ride_out_specs, ride_out_shape, ride_scratch = _ride_args(ride)

    def body(qi_ref, kj_ref, q_ref, k_ref, kv_ref, o_ref, lse_ref, m_s, l_s, acc):
        p = pl.program_id(1)
        i, j = qi_ref[p], kj_ref[p]

        @pl.when(j == 0)
        def _():
            m_s[...] = jnp.full_like(m_s, -jnp.inf)
            l_s[...] = jnp.zeros_like(l_s)
            acc[...] = jnp.zeros_like(acc)

        def step(diagonal):
            ones = jnp.ones((t, LANES), BF16)
            scores = [_dot(q_ref[:, hh * QPAD:(hh + 1) * QPAD], k_ref[:, hh * QPAD:(hh + 1) * QPAD], NT)
                      for hh in range(hb)]
            for hh in range(hb):
                sc = scores[hh]
                if diagonal:
                    sc = jnp.where(_diag_mask(t), sc, -jnp.inf)
                cols = [sc[:, c * LANES:(c + 1) * LANES] for c in range(t // LANES)]
                m_old = m_s[hh]
                m_new = jnp.maximum(m_old, jnp.max(functools.reduce(jnp.maximum, cols), axis=-1, keepdims=True))
                alpha = jnp.exp2(m_old - m_new)
                pr = jnp.concatenate([jnp.exp2(c - m_new).astype(BF16) for c in cols], axis=1)
                pv = _dot(pr, jnp.concatenate([kv_ref[:, hh * QPAD + NOPE:(hh + 1) * QPAD], ones], axis=1), NN)
                l_new = alpha * l_s[hh] + pv[:, VHEAD:]
                a_new = alpha * acc[hh] + pv[:, :VHEAD]
                if diagonal:
                    o_ref[:, hh * VHEAD:(hh + 1) * VHEAD] = a_new / l_new
                    lse_ref[hh] = jnp.transpose(m_new + jnp.log2(l_new))[:1]
                else:
                    m_s[hh], l_s[hh], acc[hh] = m_new, l_new, a_new

        pl.when(j < i)(functools.partial(step, False))
        pl.when(j == i)(functools.partial(step, True))

    res = pl.pallas_call(
        _with_ride(body, ride, grid, 2, 3, 2), name="attn_fwd",
        grid_spec=pltpu.PrefetchScalarGridSpec(
            num_scalar_prefetch=2, grid=grid,
            in_specs=[pl.BlockSpec((t, hb * QPAD), lambda h, p, qi, kj: (qi[p], h)),
                      pl.BlockSpec((t, hb * QPAD), lambda h, p, qi, kj: (kj[p], h)),
                      pl.BlockSpec((t, hb * QPAD), lambda h, p, qi, kj: (kj[p], h))] + ride_in_specs,
            out_specs=[pl.BlockSpec((t, hb * VHEAD), lambda h, p, qi, kj: (qi[p], h)),
                       pl.BlockSpec((hb, 1, t), lambda h, p, qi, kj: (h, 0, qi[p]))] + ride_out_specs,
            scratch_shapes=[pltpu.VMEM((hb, t, LANES), F32), pltpu.VMEM((hb, t, LANES), F32),
                            pltpu.VMEM((hb, t, VHEAD), F32)] + ride_scratch),
        out_shape=[jax.ShapeDtypeStruct((s, heads * VHEAD), F32),
                   jax.ShapeDtypeStruct((heads, 1, s), F32)] + ride_out_shape,
        compiler_params=_params(("arbitrary", "arbitrary") if ride else ("parallel", "arbitrary")),
    )(qi, kj, qf, kf, kv, *ride_ins)
    return (res[0], res[1], res[2:]) if ride else res


def _attn_delta(d_o, o, lay, tm):
    s = o.shape[0]
    heads = lay["mla_heads"]

    def body(do_ref, o_ref, dl_ref):
        for h in range(heads):
            sl = slice(h * VHEAD, (h + 1) * VHEAD)
            dl_ref[h] = jnp.sum(jnp.transpose(do_ref[:, sl] * o_ref[:, sl]), axis=0, keepdims=True)

    tile = pl.BlockSpec((tm, heads * VHEAD), lambda i: (i, 0))
    return pl.pallas_call(
        body, name="attn_delta", grid=(s // tm,), in_specs=[tile, tile],
        out_specs=pl.BlockSpec((heads, 1, tm), lambda i: (0, 0, i)),
        out_shape=jax.ShapeDtypeStruct((heads, 1, s), F32),
        compiler_params=_params(("parallel",)),
    )(d_o, o)


def _attn_bwd(qf, kf, kv, lse, delta, d_o, cos, sin_a, sin_b, lay, t, ride=None):
    s = qf.shape[0]
    heads = lay["mla_heads"]
    hb = _head_block(heads)
    nt = s // t
    scale = (NOPE + ROPE) ** -0.5
    qi, kj = _tile_pairs(nt, True)
    grid = (heads // hb, int(qi.shape[0]))
    ride_in_specs, ride_ins, ride_out_specs, ride_out_shape, ride_scratch = _ride_args(ride)

    def body(qi_ref, kj_ref, q_ref, k_ref, kv_ref, lse_ref, dl_ref, do_ref, cos_ref, sa_ref, sb_ref,
             dqp_ref, dkv_ref, dkpe_ref, dq_acc, dk_acc, dv_acc):
        p = pl.program_id(1)
        i, j = qi_ref[p], kj_ref[p]
        rows = pl.ds(pl.multiple_of(i * t, t), t)

        def unrope(v):
            return _rope32(v, cos_ref[...], -sa_ref[...], -sb_ref[...])

        @pl.when(p == 0)
        def _():
            dq_acc[...] = jnp.zeros_like(dq_acc)

        def step(diagonal):
            for hh in range(hb):
                lo = hh * QPAD
                q, k = q_ref[:, lo:lo + QPAD], k_ref[:, lo:lo + QPAD]
                do = do_ref[:, hh * VHEAD:(hh + 1) * VHEAD]
                pr = jnp.exp2(_dot(k, q, NT) - lse_ref[hh])
                if diagonal:
                    pr = jnp.where(_diag_mask(t, keys_on_rows=True), pr, 0.0)
                dv_part = _dot(pr, do, NN)
                ds = (pr * (_dot(kv_ref[:, lo + NOPE:lo + QPAD], do, NT) - dl_ref[hh])).astype(BF16)
                dk_part = _dot(ds, q, NN)
                dq = dq_acc[rows, lo:lo + QPAD] + _dot(ds, k, TN) * scale
                if diagonal:
                    dqp_ref[:, lo:lo + NOPE] = dq[:, :NOPE].astype(BF16)
                    dqp_ref[:, lo + NOPE:lo + QPAD] = unrope(dq[:, NOPE:]).astype(BF16)
                    dk_acc[hh], dv_acc[hh] = dk_part, dv_part
                else:
                    dq_acc[rows, lo:lo + QPAD] = dq
                    dk_acc[hh] += dk_part
                    dv_acc[hh] += dv_part

        pl.when(i > j)(functools.partial(step, False))
        pl.when(i == j)(functools.partial(step, True))

        @pl.when(i == nt - 1)
        def _():
            kpe = jnp.zeros((t, LANES), F32)
            for hh in range(hb):
                lo = hh * QPAD
                dk = dk_acc[hh] * math.log(2.0)
                dkv_ref[:, lo:lo + NOPE] = dk[:, :NOPE].astype(BF16)
                dkv_ref[:, lo + NOPE:lo + QPAD] = dv_acc[hh].astype(BF16)
                kpe = kpe + dk[:, NOPE:]
            dkpe_ref[...] = unrope(kpe)

    table = pl.BlockSpec((t, LANES), lambda h, p, qi, kj: (kj[p], 0))
    res = pl.pallas_call(
        _with_ride(body, ride, grid, 2, 9, 3), name="attn_bwd",
        grid_spec=pltpu.PrefetchScalarGridSpec(
            num_scalar_prefetch=2, grid=grid,
            in_specs=[pl.BlockSpec((t, hb * QPAD), lambda h, p, qi, kj: (qi[p], h)),
                      pl.BlockSpec((t, hb * QPAD), lambda h, p, qi, kj: (kj[p], h)),
                      pl.BlockSpec((t, hb * QPAD), lambda h, p, qi, kj: (kj[p], h)),
                      pl.BlockSpec((hb, 1, t), lambda h, p, qi, kj: (h, 0, qi[p])),
                      pl.BlockSpec((hb, 1, t), lambda h, p, qi, kj: (h, 0, qi[p])),
                      pl.BlockSpec((t, hb * VHEAD), lambda h, p, qi, kj: (qi[p], h)),
                      table, table, table] + ride_in_specs,
            out_specs=[pl.BlockSpec((t, hb * QPAD), lambda h, p, qi, kj: (kj[p], h)),
                       pl.BlockSpec((t, hb * QPAD), lambda h, p, qi, kj: (kj[p], h)),
                       pl.BlockSpec((t, LANES), lambda h, p, qi, kj: (kj[p], h))] + ride_out_specs,
            scratch_shapes=[pltpu.VMEM((s, hb * QPAD), F32), pltpu.VMEM((hb, t, QPAD), F32),
                            pltpu.VMEM((hb, t, VHEAD), F32)] + ride_scratch),
        out_shape=[jax.ShapeDtypeStruct((s, heads * QPAD), BF16),
                   jax.ShapeDtypeStruct((s, heads * QPAD), BF16),
                   jax.ShapeDtypeStruct((s, heads // hb * LANES), F32)] + ride_out_shape,
        compiler_params=_params(("arbitrary", "arbitrary") if ride else ("parallel", "arbitrary")),
    )(qi, kj, qf, kf, kv, lse, delta, d_o, cos, sin_a, sin_b, *ride_ins)
    return (res[0], res[1], res[2], res[3:]) if ride else res


ANY = pl.BlockSpec(memory_space=pl.ANY)


def _place():
    return lax.axis_index("x"), lax.axis_index("y"), lax.axis_index("c")


def _other_chips(x, y):
    return [(1 - x, y), (x, 1 - y), (1 - x, 1 - y)]


class _Exchange:
    def __init__(self, ins, out_shape, scratch, phases):
        self.ins, self.out_shape, self.scratch, self.phases = list(ins), list(out_shape), list(scratch), phases

    def counts(self):
        return len(self.ins), len(self.out_shape), len(self.scratch)

    def run(self, r_in, r_out, r_scratch, conds):
        for cond, phase in zip(conds, self.phases):
            if phase is not None and cond is not None:
                pl.when(cond)(functools.partial(phase, r_in, r_out, r_scratch))


def _steps(ids, sizes):
    lin, total = 0, 1
    for i, n in zip(ids, sizes):
        lin, total = lin * n + i, total * n
    return lin == 0, lin == total // 2, lin == total - 1


def _ride_args(ride):
    if ride is None:
        return [], [], [], [], []
    n_in, n_out, _ = ride.counts()
    return [ANY] * n_in, ride.ins, [ANY] * n_out, ride.out_shape, ride.scratch


def _with_ride(body, ride, grid, n_prefetch, n_in, n_out):
    if ride is None:
        return body
    r_in, r_out, r_sc = ride.counts()

    def hosted(*refs):
        cuts = (n_prefetch, n_in, r_in, n_out, r_out)
        parts, pos = [], 0
        for n in cuts:
            parts.append(refs[pos:pos + n])
            pos += n
        pre, ins, ride_in, outs, ride_out = parts
        scratch, ride_scratch = refs[pos:len(refs) - r_sc], refs[len(refs) - r_sc:]
        first, mid, last = _steps([pl.program_id(d) for d in range(len(grid))], grid)
        ride.run(ride_in, ride_out, ride_scratch, (first, mid, None))
        body(*pre, *ins, *outs, *scratch)
        ride.run(ride_in, ride_out, ride_scratch, (None, None, last))

    return hosted


def _exchange_alone(name, ex):
    n_in, n_out, _ = ex.counts()

    def body(*refs):
        for phase in ex.phases:
            if phase is not None:
                phase(refs[:n_in], refs[n_in:n_in + n_out], refs[n_in + n_out:])

    return pl.pallas_call(
        body, name=name, in_specs=[ANY] * n_in, out_specs=[ANY] * n_out, out_shape=ex.out_shape,
        scratch_shapes=ex.scratch)(*ex.ins)


def _gather_exchange(shards):
    nw = len(shards)

    def parts(ins, outs, sems):
        send_sems, recv_sems, local_sems = sems
        x, y, c = _place()

        def slot(px, py, pc):
            return 4 * px + 2 * py + pc

        def copy(w, k, rows, to, src=None):
            return pltpu.make_async_remote_copy(
                src_ref=rows if src is None else src, dst_ref=rows, send_sem=send_sems.at[w, k],
                recv_sem=recv_sems.at[w, k], device_id=to, device_id_type=MESH)

        def plan(w, mine):
            side = c if mine else 1 - c
            half = shards[w].shape[0] // 2
            whole = lambda px, py: outs[w].at[slot(px, py, side)]
            top = lambda px, py: outs[w].at[slot(px, py, side), pl.ds(0, half)]
            bottom = lambda px, py: outs[w].at[slot(px, py, side), pl.ds(half, half)]
            xn, yn, sib = (1 - x, y, side), (x, 1 - y, side), (x, y, 1 - side)
            own = ins[w] if mine else None
            return [copy(w, 0, whole(x, y), sib, own), copy(w, 1, whole(x, y), xn, own),
                    copy(w, 2, whole(x, y), yn, own), copy(w, 3, top(1 - x, y), yn), copy(w, 4, bottom(x, 1 - y), xn),
                    copy(w, 5, whole(1 - x, y), sib), copy(w, 6, whole(x, 1 - y), sib),
                    copy(w, 7, top(1 - x, 1 - y), sib), copy(w, 8, bottom(1 - x, 1 - y), sib)]

        def arrivals(w):
            half = shards[w].shape[0] // 2
            at = lambda px, py, *rows: outs[w].at[(slot(px, py, c),) + rows]
            return {1: copy(w, 1, at(1 - x, y), (x, y, c)), 2: copy(w, 2, at(x, 1 - y), (x, y, c)),
                    3: copy(w, 3, at(1 - x, 1 - y, pl.ds(0, half)), (x, y, c)),
                    4: copy(w, 4, at(1 - x, 1 - y, pl.ds(half, half)), (x, y, c))}

        local = [pltpu.make_async_copy(ins[w], outs[w].at[slot(x, y, c)], local_sems.at[w]) for w in range(nw)]
        return plan, arrivals, local

    def start(ins, outs, sems):
        plan, _, local = parts(ins, outs, sems)
        for cp in local:
            cp.start()
        for w in range(nw):
            for k in (0, 1, 2):
                plan(w, True)[k].start()

    def middle(ins, outs, sems):
        plan, arrivals, _ = parts(ins, outs, sems)
        for landed, onward in ((1, (3, 5)), (2, (4, 6))):
            for w in range(nw):
                arrivals(w)[landed].wait_recv()
                for k in onward:
                    plan(w, True)[k].start()

    def finish(ins, outs, sems):
        plan, arrivals, local = parts(ins, outs, sems)
        for landed, onward in ((3, 7), (4, 8)):
            for w in range(nw):
                arrivals(w)[landed].wait_recv()
                plan(w, True)[onward].start()
        for w in range(nw):
            from_sibling = plan(w, False)
            for k in (0, 5, 6, 7, 8):
                from_sibling[k].wait_recv()
            for cp in plan(w, True):
                cp.wait_send()
        for cp in local:
            cp.wait()

    return _Exchange(
        shards, [jax.ShapeDtypeStruct((N_DEV,) + s.shape, s.dtype) for s in shards],
        [pltpu.SemaphoreType.DMA((nw, 9)), pltpu.SemaphoreType.DMA((nw, 9)), pltpu.SemaphoreType.DMA((nw,))],
        (start, middle, finish))


def _sibling_exchange(grads):
    nw = len(grads)

    def copies(ins, outs, sems):
        x, y, c = _place()
        return [pltpu.make_async_remote_copy(
            src_ref=ins[w].at[2 * p + (1 - c)], dst_ref=outs[w].at[p], send_sem=sems[0].at[w, p],
            recv_sem=sems[1].at[w, p], device_id=(x, y, 1 - c), device_id_type=MESH)
            for w in range(nw) for p in range(4)]

    def start(ins, outs, sems):
        for cp in copies(ins, outs, sems):
            cp.start()

    def finish(ins, outs, sems):
        for cp in copies(ins, outs, sems):
            cp.wait()

    return _Exchange(grads, [jax.ShapeDtypeStruct((4,) + g.shape[1:], g.dtype) for g in grads],
                     [pltpu.SemaphoreType.DMA((nw, 4)), pltpu.SemaphoreType.DMA((nw, 4))], (start, None, finish))


def _chips_exchange(sums):
    nw = len(sums)

    def copies(ins, outs, sems):
        x, y, c = _place()
        return [pltpu.make_async_remote_copy(
            src_ref=ins[w].at[2 * px + py], dst_ref=outs[w].at[k], send_sem=sems[0].at[w, k],
            recv_sem=sems[1].at[w, k], device_id=(px, py, c), device_id_type=MESH)
            for w in range(nw) for k, (px, py) in enumerate(_other_chips(x, y))]

    def start(ins, outs, sems):
        for cp in copies(ins, outs, sems):
            cp.start()

    def finish(ins, outs, sems):
        for cp in copies(ins, outs, sems):
            cp.wait()

    return _Exchange(sums, [jax.ShapeDtypeStruct((3,) + g.shape[1:], g.dtype) for g in sums],
                     [pltpu.SemaphoreType.DMA((nw, 3)), pltpu.SemaphoreType.DMA((nw, 3))], (start, None, finish))


def _pair_sum(name, g, got, c_arr, tr):
    _, rows, cols = g.shape
    tr = _tile_rows(rows, tr)

    def body(c_ref, a_ref, b_ref, o_ref):
        o_ref[...] = (a_ref[...].astype(F32) + b_ref[...].astype(F32)).astype(BF16)

    return pl.pallas_call(
        body, name=name,
        grid_spec=pltpu.PrefetchScalarGridSpec(
            num_scalar_prefetch=1, grid=(4, rows // tr),
            in_specs=[pl.BlockSpec((None, tr, cols), lambda p, r, cr: (2 * p + cr[0], r, 0)),
                      pl.BlockSpec((None, tr, cols), lambda p, r, cr: (p, r, 0))],
            out_specs=pl.BlockSpec((None, tr, cols), lambda p, r, cr: (p, r, 0))),
        out_shape=jax.ShapeDtypeStruct((4, rows, cols), BF16),
        compiler_params=_params(("parallel", "parallel")),
    )(c_arr, g, got)


def _tile_rows(rows, pref):
    t = min(rows, pref)
    while rows % t or t % 8:
        t -= 1
    return t


def _adam(w, g, m, v):
    m = ADAM_B1 * m + (1.0 - ADAM_B1) * g
    v = ADAM_B2 * v + (1.0 - ADAM_B2) * (g * g)
    m_hat = m / (1.0 - ADAM_B1 ** ADAM_STEP)
    v_hat = v / (1.0 - ADAM_B2 ** ADAM_STEP)
    return -ADAM_LR * (m_hat / (jnp.sqrt(v_hat) + ADAM_EPS) + ADAM_WD * w), m, v


def _adamw_shard(name, w, m, v, sums, got, chip_arr, tr):
    _, rows, cols = w.shape
    tr = _tile_rows(rows, tr)

    def body(p_ref, w_ref, m_ref, v_ref, s_ref, r_ref, g_ref, d_ref, nm_ref, nv_ref):
        g = s_ref[...].astype(F32)
        for k in range(3):
            g = g + r_ref[k].astype(F32)
        g_ref[...] = g
        d_ref[...], nm_ref[...], nv_ref[...] = _adam(w_ref[...], g, m_ref[...], v_ref[...])

    tile = pl.BlockSpec((None, tr, cols), lambda r, pr: (0, r, 0))
    return pl.pallas_call(
        body, name=name,
        grid_spec=pltpu.PrefetchScalarGridSpec(
            num_scalar_prefetch=1, grid=(rows // tr,),
            in_specs=[tile, tile, tile,
                      pl.BlockSpec((None, tr, cols), lambda r, pr: (pr[0], r, 0)),
                      pl.BlockSpec((3, tr, cols), lambda r, pr: (0, r, 0))],
            out_specs=[tile] * 4),
        out_shape=[jax.ShapeDtypeStruct((1, rows, cols), F32)] * 4,
        compiler_params=_params(("parallel",)),
    )(chip_arr, w, m, v, sums, got)


def _gains_all_reduce_adam(grads, loss_part, ws, ms, vs):
    n = len(grads)
    widths = [g.shape[1] for g in grads]
    rows, width = -(-(n + 1) // 8) * 8, max(widths)

    def body(*refs):
        g_in, loss_in = refs[:n], refs[n]
        w_in, m_in, v_in = (refs[1 + k * n:1 + (k + 1) * n] for k in (1, 2, 3))
        outs = refs[1 + 4 * n:2 + 8 * n]
        g_out, d_out, m_out, v_out = (outs[k * n:(k + 1) * n] for k in range(4))
        loss_out = outs[4 * n]
        buf, send_sems, recv_sems = refs[2 + 8 * n:]
        x, y, c = _place()
        me = 4 * x + 2 * y + c
        buf[me] = jnp.zeros((rows, width), F32)
        for r in range(n):
            buf[me, r:r + 1, :widths[r]] = g_in[r][...]
        buf[me, n:n + 1, :LANES] = loss_in[...]
        peers = [(x, y, 1 - c)] + [(px, py, pc) for px, py in _other_chips(x, y) for pc in (c, 1 - c)]
        copies = []
        for k, peer in enumerate(peers):
            cp = pltpu.make_async_remote_copy(
                src_ref=buf.at[me], dst_ref=buf.at[me], send_sem=send_sems.at[k], recv_sem=recv_sems.at[k],
                device_id=peer, device_id_type=MESH)
            cp.start()
            copies.append(cp)
        for cp in copies:
            cp.wait()
        total = buf[0]
        for k in range(1, N_DEV):
            total = total + buf[k]
        for r in range(n):
            g = total[r:r + 1, :widths[r]]
            g_out[r][...] = g
            d_out[r][...], m_out[r][...], v_out[r][...] = _adam(w_in[r][...], g, m_in[r][...], v_in[r][...])
        loss_out[...] = total[n:n + 1, :LANES]

    vm = pl.BlockSpec(memory_space=pltpu.VMEM)
    shapes = [jax.ShapeDtypeStruct((1, w), F32) for w in widths]
    res = pl.pallas_call(
        body, name="gains_all_reduce_adamw",
        in_specs=[vm] * (4 * n + 1), out_specs=[vm] * (4 * n + 1),
        out_shape=shapes * 4 + [jax.ShapeDtypeStruct((1, LANES), F32)],
        scratch_shapes=[pltpu.VMEM((N_DEV, rows, width), F32), pltpu.SemaphoreType.DMA((7,)),
                        pltpu.SemaphoreType.DMA((7,))],
    )(*grads, loss_part, *ws, *ms, *vs)
    return res[:n], res[n:2 * n], res[2 * n:3 * n], res[3 * n:4 * n], res[4 * n]


IN_ORDER = ("r_q", "r_k", "r_v", "r_g", "c_q", "c_kv", "k_pe", "g_ret", "g_mla")
RET_HEAD = 2 * RET_QK + 2 * RET_V


def _make_layout(d, vw, qw, ql, kl, mla_w):
    width = {"r_q": qw, "r_k": qw, "r_v": vw, "r_g": vw, "c_q": ql, "c_kv": kl, "k_pe": ROPE, "g_ret": d, "g_mla": d}
    src, o = {}, 0
    for n in IN_ORDER:
        src[n] = o
        o += width[n]
    heads = vw // RET_V
    off, pieces, o = {}, [], 0

    def put(name, w, s):
        nonlocal o
        off.setdefault(name, o)
        pieces.append((o, w, s))
        o += w

    for n in ("g_ret", "g_mla", "c_q"):
        put(n, width[n], src[n])
    for h in range(heads):
        put("heads", RET_QK, src["r_q"] + h * RET_QK)
        put("heads", RET_QK, src["r_k"] + h * RET_QK)
        put("heads", RET_V, src["r_v"] + h * RET_V)
        put("heads", RET_V, src["r_g"] + h * RET_V)
    for n in ("c_kv", "k_pe"):
        put(n, width[n], src[n])
    total = off["k_pe"] + 2 * LANES
    for n, blk in (("g_ret", d), ("g_mla", d), ("c_q", ql), ("heads", RET_HEAD), ("c_kv", kl), ("k_pe", 2 * LANES)):
        assert off[n] % blk == 0
    return {"off": off, "pieces": pieces, "total": total, "n_in": sum(width.values()),
            "ret_heads": heads, "mla_heads": mla_w // VHEAD, "q_lora": ql, "kv_lora": kl}


def _cols_to_full(g):
    n, r, c = g.shape
    return jnp.transpose(g, (1, 0, 2)).reshape(r, n * c)


def _full_to_cols(w):
    r, c = w.shape
    return jnp.transpose(w.reshape(r, N_DEV, c // N_DEV), (1, 0, 2))


def _w_in_to_mine(g, lay):
    _, rows, cols = g.shape
    parts, at = [], 0
    for o, w, s in lay["pieces"]:
        if o > at:
            parts.append(jnp.zeros((rows, o - at), g.dtype))
        while w > 0:
            k, a = divmod(s, cols)
            take = min(w, cols - a)
            parts.append(g[k, :, a:a + take])
            s, w, o = s + take, w - take, o + take
        at = o
    parts.append(jnp.zeros((rows, lay["total"] - at), g.dtype))
    return jnp.concatenate(parts, axis=1)


def _mine_to_blocks(g, lay):
    cols = lay["n_in"] // N_DEV
    by_src = sorted(lay["pieces"], key=lambda p: p[2])
    blocks = []
    for k in range(N_DEV):
        lo, hi, parts = k * cols, (k + 1) * cols, []
        for o, w, s in by_src:
            a, b = max(lo, s), min(hi, s + w)
            if a < b:
                parts.append(g[:, o + a - s:o + b - s])
        blocks.append(jnp.concatenate(parts, axis=1))
    return jnp.stack(blocks)


def _rope_tables(positions, half):
    inv = ROPE_THETA ** (-jnp.arange(half, dtype=F32) / half)
    ang = positions.astype(F32)[:, None] * inv
    return jnp.cos(ang), jnp.sin(ang)


def kernel(x, positions, norm_mix_g, w_in, ret_norm_g, w_ret_o, q_a_norm_g, w_q_b, kv_a_norm_g, w_kv_b, w_mla_o, w_out, norm_mlp_g, w_up, w_down, norm_f_g, loss_target, m_norm_mix_g, m_w_in, m_ret_norm_g, m_w_ret_o, m_q_a_norm_g, m_w_q_b, m_kv_a_norm_g, m_w_kv_b, m_w_mla_o, m_w_out, m_norm_mlp_g, m_w_up, m_w_down, m_norm_f_g, v_norm_mix_g, v_w_in, v_ret_norm_g, v_w_ret_o, v_q_a_norm_g, v_w_q_b, v_kv_a_norm_g, v_w_kv_b, v_w_mla_o, v_w_out, v_norm_mlp_g, v_w_up, v_w_down, v_norm_f_g):
    xs, tgt, pos = x[0], loss_target[0], positions[0]
    s, d = xs.shape
    mats = {"w_in": w_in[0], "w_ret_o": w_ret_o[0], "w_q_b": w_q_b[0], "w_kv_b": w_kv_b[0], "w_mla_o": w_mla_o[0],
            "w_out": w_out[0], "w_up": w_up[0], "w_down": w_down[0]}
    mat_w = {"w_in": w_in, "w_ret_o": w_ret_o, "w_q_b": w_q_b, "w_kv_b": w_kv_b, "w_mla_o": w_mla_o, "w_out": w_out,
             "w_up": w_up, "w_down": w_down}
    mat_m = {"w_in": m_w_in, "w_ret_o": m_w_ret_o, "w_q_b": m_w_q_b, "w_kv_b": m_w_kv_b, "w_mla_o": m_w_mla_o,
             "w_out": m_w_out, "w_up": m_w_up, "w_down": m_w_down}
    mat_v = {"w_in": v_w_in, "w_ret_o": v_w_ret_o, "w_q_b": v_w_q_b, "w_kv_b": v_w_kv_b, "w_mla_o": v_w_mla_o,
             "w_out": v_w_out, "w_up": v_w_up, "w_down": v_w_down}
    names = list(mats)
    col_sharded = ("w_in", "w_q_b", "w_kv_b", "w_up")
    vw = ret_norm_g.shape[1]
    mla_w = mats["w_mla_o"].shape[0] * N_DEV
    ql, kl = q_a_norm_g.shape[1], kv_a_norm_g.shape[1]
    n_in = mats["w_in"].shape[1] * N_DEV
    qw = (n_in - 2 * vw - ql - kl - ROPE - 2 * d) // 2
    lay = _make_layout(d, vw, qw, ql, kl, mla_w)
    assert lay["n_in"] == n_in
    heads_r, heads_m = lay["ret_heads"], lay["mla_heads"]

    shard16 = {n: mats[n].astype(BF16) for n in names}
    with_in_proj = ("w_ret_o", "w_q_b", "w_kv_b", "w_mla_o", "w_out")
    mlp = ("w_up", "w_down")
    by_device = ("w_up", "w_kv_b")
    full = {}

    def keep(group, gathered):
        for n, g in zip(group, gathered):
            if n not in by_device:
                g = _cols_to_full(g) if n in col_sharded else g.reshape(-1, g.shape[2])
            full[n] = g

    w_mine = _w_in_to_mine(_exchange_alone("gather_w_in", _gather_exchange([shard16["w_in"]]))[0], lay)

    c64, s64 = _rope_tables(pos, RET_QK // 2)
    cos_r = jnp.concatenate([c64, c64], axis=1)
    sin_r = jnp.concatenate([-s64, s64], axis=1)
    c32, s32 = _rope_tables(pos, ROPE // 2)
    z32, z64 = jnp.zeros_like(c32), jnp.zeros((s, LANES - ROPE), F32)
    cos_p = jnp.concatenate([c32, c32, z64], axis=1)
    sin_a = jnp.concatenate([-s32, z32, z64], axis=1)
    sin_b = jnp.concatenate([z32, s32, z64], axis=1)
    lg = jnp.log(1.0 - 2.0 ** (-5.0 - jnp.arange(heads_r, dtype=F32)))
    lgs = jnp.broadcast_to(lg[:, None, None], (heads_r, 8, LANES))

    tm = min(256, s)
    blk = min(512, s)
    t_att = min(512, s)

    u = _rms_fwd("norm_mix", xs, norm_mix_g, tm)
    proj, gathered = _mm("in_proj", u, w_mine, "nn", F32,
                         ride=_gather_exchange([shard16[n] for n in with_in_proj]))
    keep(with_in_proj, gathered)
    wq_pad = jnp.pad(full["w_q_b"].reshape(ql, heads_m, NOPE + ROPE),
                     ((0, 0), (0, 0), (0, QPAD - NOPE - ROPE))).reshape(ql, heads_m * QPAD)
    o_ret, states, ry = _ret_fwd(proj, lay, cos_r, sin_r, lgs, ret_norm_g, blk)
    y_ret = _mm("ret_out", ry, full["w_ret_o"], "nn", BF16)
    cqn, ckvn, kpr = _mla_prep(proj, lay, q_a_norm_g, kv_a_norm_g, cos_p, sin_a, sin_b, tm)
    qf = _mm("q_up", cqn, wq_pad, "nn", BF16, extras=((cos_p, None), (sin_a, None), (sin_b, None)), epilogue=_q_operand)
    kv, kf = _mm("kv_up", ckvn, full["w_kv_b"], "nn", (BF16, BF16), b_by_device=True, extras=((kpr, None),),
                 epilogue=_k_operand)
    o_mla, lse, gathered = _attn_fwd(qf, kf, kv, lay, t_att, ride=_gather_exchange([shard16["w_up"]]))
    keep(("w_up",), gathered)
    gate_tile = _tile(d, 1024)
    y_mla, merged = _mm(
        "mla_out", o_mla, full["w_mla_o"], "nn", (BF16, BF16), tm=512, tn=gate_tile,
        extras=((proj, lay["off"]["g_ret"] // gate_tile), (proj, lay["off"]["g_mla"] // gate_tile), y_ret),
        epilogue=lambda r, gr, gm, yr: (r, _sig(gr) * yr + _sig(gm) * r))
    h1, n2 = _mm("out_proj", merged, full["w_out"], "nn", (F32, BF16), tm=512, tn=d,
                 extras=(xs, (norm_mlp_g, "whole")), epilogue=_residual_norm)
    (act_slope, act), gathered = _mm(
        "mlp_up", n2, full["w_up"], "nn", (BF16, BF16), b_by_device=True,
        epilogue=lambda r: (2.0 * jnp.maximum(r, 0.0), jnp.square(jnp.maximum(r, 0.0))),
        ride=_gather_exchange([shard16["w_down"]]))
    keep(("w_down",), gathered)
    dn = _mm("mlp_down", act, full["w_down"], "nn", F32, tk=4096)
    dh2, dh2_16, g_norm_f, loss_part = _final("loss_head", h1, dn, norm_f_g.reshape(1, d), tgt, tm)

    mx, my, mc = _place()
    c_arr = jnp.reshape(mc, (1,)).astype(jnp.int32)
    chip_arr = jnp.reshape(2 * mx + my, (1,)).astype(jnp.int32)
    sums, from_chips = {}, {}

    def blocks(group, grads):
        return [g if n in by_device else (_full_to_cols(g) if n in col_sharded else g.reshape((N_DEV,) + mats[n].shape))
                for n, g in zip(group, grads)]

    def pair_sums(group, mine, from_sibling):
        for n, g, r in zip(group, mine, from_sibling):
            sums[n] = _pair_sum("pair_sum_" + n, g, r, c_arr, 256)
        return [sums[n] for n in group]

    dz = _mm("mlp_down_dx", dh2_16, full["w_down"], "nt", BF16, extras=(act_slope,),
             epilogue=lambda r, slope: (r * slope,))
    g_w_down = _mm("mlp_down_dw", act, dh2_16, "tn", BF16, tm=512, tn=d, tk=s)
    down_blocks = blocks(("w_down",), (g_w_down,))
    g_w_up, got_down = _mm("mlp_up_dw", n2, dz, "tn", BF16, tk=s, out_by_device=True,
                           ride=_sibling_exchange(down_blocks))
    dn2, got_up = _mm("mlp_up_dx", dz, full["w_up"], "nt", F32, tk=4096, b_by_device=True,
                      ride=_sibling_exchange([g_w_up]))
    mlp_sums = pair_sums(mlp, [g_w_up] + down_blocks, list(got_up) + list(got_down))
    dh1, dh1_16, g_norm_mlp = _rms_bwd("norm_mlp_bwd", dn2, h1, norm_mlp_g, dh2, tm, matmul_copy=True)
    assert lay["off"]["g_ret"] == 0 and lay["off"]["g_mla"] == d
    dy_ret, dy_mla, d_proj = _mm(
        "out_proj_dx", dh1_16, full["w_out"], "nt", (BF16, BF16, (BF16, 2, lay["total"])), tm=256, tn=d,
        extras=((proj, 0), (proj, 1), y_ret, y_mla), epilogue=_gate_grads)
    g_w_out = _mm("out_proj_dw", merged, dh1_16, "tn", BF16, tm=512, tn=d, tk=s)
    g_w_ret_o = _mm("ret_out_dw", ry, dy_ret, "tn", BF16, tm=512, tn=d, tk=s)
    g_w_mla_o = _mm("mla_out_dw", o_mla, dy_mla, "tn", BF16, tm=256, tn=d, tk=s)
    mixer = ("w_out", "w_ret_o", "w_mla_o")
    mixer_blocks = blocks(mixer, (g_w_out, g_w_ret_o, g_w_mla_o))
    d_ry, got = _mm("ret_out_dx", dy_ret, full["w_ret_o"], "nt", F32, ride=_sibling_exchange(mixer_blocks))
    mixer_sums = pair_sums(mixer, mixer_blocks, got)
    d_omla = _mm("mla_out_dx", dy_mla, full["w_mla_o"], "nt", F32)
    d_proj, g_ret_norm, got = _ret_bwd(proj, lay, cos_r, sin_r, lgs, states, d_ry, o_ret, ret_norm_g, d_proj, blk,
                                       ride=_chips_exchange(mixer_sums))
    from_chips.update(zip(mixer, got))
    delta = _attn_delta(d_omla, o_mla, lay, t_att)
    dqp, dkv, dkpe_parts, got = _attn_bwd(qf, kf, kv, lse, delta, d_omla, cos_p, sin_a, sin_b, lay, t_att,
                                          ride=_chips_exchange(mlp_sums))
    from_chips.update(zip(mlp, got))
    d_proj = _rope_key_grad(dkpe_parts, lay, d_proj, tm)
    d_cqn = _mm("q_up_dx", dqp, wq_pad, "nt", F32)
    g_wq_pad = _mm("q_up_dw", cqn, dqp, "tn", BF16)
    d_ckvn = _mm("kv_up_dx", dkv, full["w_kv_b"], "nt", F32, b_by_device=True)
    g_w_kv_b = _mm("kv_up_dw", ckvn, dkv, "tn", BF16, out_by_device=True)
    d_proj, g_q_a = _latent_norm_bwd("q_latent_bwd", proj, lay["off"]["c_q"], d_cqn, q_a_norm_g, d_proj, tm)
    d_proj, g_kv_a = _latent_norm_bwd("kv_latent_bwd", proj, lay["off"]["c_kv"], d_ckvn, kv_a_norm_g, d_proj, tm)
    g_w_q_b = g_wq_pad.reshape(ql, heads_m, QPAD)[:, :, :NOPE + ROPE].reshape(ql, heads_m * (NOPE + ROPE))
    latent = ("w_q_b", "w_kv_b")
    latent_blocks = blocks(latent, (g_w_q_b, g_w_kv_b))
    latent_sums = pair_sums(latent, latent_blocks,
                            _exchange_alone("latent_grads_to_sibling", _sibling_exchange(latent_blocks)))
    g_w_mine, got = _mm("in_proj_dw", u, d_proj, "tn", BF16, tk=s, ride=_chips_exchange(latent_sums))
    from_chips.update(zip(latent, got))
    last = ("w_in",)
    last_blocks = [_mine_to_blocks(g_w_mine, lay)]
    last_sums = pair_sums(last, last_blocks, _exchange_alone("grads_to_sibling", _sibling_exchange(last_blocks)))
    du, got = _mm("in_proj_dx", d_proj, w_mine, "nt", F32, ride=_chips_exchange(last_sums))
    from_chips.update(zip(last, got))
    grad_x, g_norm_mix = _rms_bwd("norm_mix_bwd", du, xs, norm_mix_g, dh1, tm)

    upd = {n: _adamw_shard("adamw_" + n, mat_w[n], mat_m[n], mat_v[n], sums[n], from_chips[n], chip_arr, 256)
           for n in names}

    gains = [("norm_mix_g", norm_mix_g, m_norm_mix_g, v_norm_mix_g, g_norm_mix),
             ("ret_norm_g", ret_norm_g, m_ret_norm_g, v_ret_norm_g, g_ret_norm),
             ("q_a_norm_g", q_a_norm_g, m_q_a_norm_g, v_q_a_norm_g, g_q_a),
             ("kv_a_norm_g", kv_a_norm_g, m_kv_a_norm_g, v_kv_a_norm_g, g_kv_a),
             ("norm_mlp_g", norm_mlp_g, m_norm_mlp_g, v_norm_mlp_g, g_norm_mlp),
             ("norm_f_g", norm_f_g, m_norm_f_g, v_norm_f_g, g_norm_f)]
    as_row = lambda a: a.reshape(1, -1)
    g_sm, d_sm, m_sm, v_sm, loss_row = _gains_all_reduce_adam(
        [g[4] for g in gains], loss_part, *[[as_row(g[k]) for g in gains] for k in (1, 2, 3)])
    loss = loss_row[0, 0]
    small = {g[0]: [a[r].reshape(g[1].shape) for a in (g_sm, d_sm, m_sm, v_sm)] for r, g in enumerate(gains)}

    order = ["norm_mix_g", "w_in", "ret_norm_g", "w_ret_o", "q_a_norm_g", "w_q_b", "kv_a_norm_g", "w_kv_b", "w_mla_o",
             "w_out", "norm_mlp_g", "w_up", "w_down", "norm_f_g"]
    outs = [loss, grad_x[None]]
    for k in range(4):
        for n in order:
            outs.append(small[n][k] if n in small else upd[n][k])
    return tuple(outs)
```

```python
import functools
import math

import jax
import jax.numpy as jnp
from jax import lax
from jax.experimental import pallas as pl
from jax.experimental.pallas import tpu as pltpu

F32 = jnp.float32
BF16 = jnp.bfloat16
MESH = pl.DeviceIdType.MESH

EPS = 1e-6
ROPE_THETA = 10000.0
CHUNK_SHIFT = 6
RET_QK = 128
RET_V = 256
NOPE = 128
ROPE = 64
VHEAD = 128
QPAD = 256
LANES = 128
N_DEV = 8
VMEM_LIMIT = 56 * 1024 * 1024

ADAM_LR = 0.001
ADAM_B1 = 0.9
ADAM_B2 = 0.999
ADAM_EPS = 1e-08
ADAM_WD = 0.01
ADAM_STEP = 10

NN = (((1,), (0,)), ((), ()))
NT = (((1,), (1,)), ((), ()))
TN = (((0,), (0,)), ((), ()))


def _dot(a, b, dims):
    return lax.dot_general(a.astype(BF16), b.astype(BF16), dims, preferred_element_type=F32)


def _tile(dim, pref):
    if dim <= pref:
        return dim
    t = (pref // LANES) * LANES
    while t >= LANES:
        if dim % t == 0:
            return t
        t -= LANES
    raise ValueError(f"no tile for {dim}")


def _params(sem):
    return pltpu.CompilerParams(dimension_semantics=sem, vmem_limit_bytes=VMEM_LIMIT)


def _sig(v):
    return 1.0 / (1.0 + jnp.exp(-v))


def _mm(name, a, b, mode, out_dtypes, *, tm=1024, tn=1024, tk=2048, extras=(), epilogue=None, ride=None,
        b_by_device=False, out_by_device=False):
    if b_by_device:
        b_cols = b.shape[2]
        b_shape = (b.shape[1], N_DEV * b_cols)
    else:
        b_shape = b.shape
    if mode == "nn":
        (m, k), (_, n) = a.shape, b_shape
    elif mode == "nt":
        (m, k), (n, _) = a.shape, b_shape
    else:
        (k, m), (_, n) = a.shape, b_shape
    tm, tn, tk = _tile(m, tm), _tile(n, tn), _tile(k, tk)
    if b_by_device and mode != "nt":
        tn = _tile(b_cols, tn)
    if out_by_device:
        tn = _tile(n // N_DEV, tn)
    nk = k // tk
    dims = {"nn": NN, "nt": NT, "tn": TN}[mode]
    a_spec = (pl.BlockSpec((tk, tm), lambda i, j, kk: (kk, i)) if mode == "tn"
              else pl.BlockSpec((tm, tk), lambda i, j, kk: (i, kk)))
    if b_by_device and mode == "nt":
        piece = min(tk, b_cols)
        n_b, per = tk // piece, b_cols // piece
        b_specs = [pl.BlockSpec((None, tn, piece),
                                lambda i, j, kk, p=p: ((kk * n_b + p) // per, j, (kk * n_b + p) % per))
                   for p in range(n_b)]
    elif b_by_device:
        per = b_cols // tn
        n_b, piece = 1, tk
        b_specs = [pl.BlockSpec((None, tk, tn), lambda i, j, kk: (j // per, kk, j % per))]
    else:
        n_b, piece = 1, tk
        b_specs = [pl.BlockSpec((tn, tk), lambda i, j, kk: (j, kk)) if mode == "nt"
                   else pl.BlockSpec((tk, tn), lambda i, j, kk: (kk, j))]
    tile_spec = pl.BlockSpec((tm, tn), lambda i, j, kk: (i, j))
    if out_by_device:
        per_out = n // N_DEV // tn
        out_spec = pl.BlockSpec((None, tm, tn), lambda i, j, kk: (j // per_out, i, j % per_out))
        out_dims = (N_DEV, m, n // N_DEV)
    else:
        out_spec, out_dims = tile_spec, (m, n)
    ex_arrays, ex_specs = [], []
    for e in extras:
        arr, off = e if isinstance(e, tuple) else (e, 0)
        ex_arrays.append(arr)
        if off == "whole":
            ex_specs.append(pl.BlockSpec(arr.shape, lambda i, j, kk, nd=arr.ndim: (0,) * nd))
        elif off is None:
            ex_specs.append(pl.BlockSpec((tm, arr.shape[1]), lambda i, j, kk: (i, 0)))
        else:
            ex_specs.append(pl.BlockSpec((tm, tn), lambda i, j, kk, off=off: (i, j + off)))
    n_ex = len(extras)
    single = not isinstance(out_dtypes, (tuple, list))
    dts = (out_dtypes,) if single else tuple(out_dtypes)
    out_specs, out_shapes = [], []
    for dt in dts:
        if isinstance(dt, tuple):
            dt, mult, width = dt
            out_specs.append(pl.BlockSpec((tm, mult * tn), lambda i, j, kk: (i, j)))
            out_shapes.append(jax.ShapeDtypeStruct((m, width), dt))
        else:
            out_specs.append(out_spec)
            out_shapes.append(jax.ShapeDtypeStruct(out_dims, dt))

    grid = (m // tm, n // tn, nk)
    r_in, r_out, r_sc = ride.counts() if ride else (0, 0, 0)
    n_acc = 1 if nk > 1 else 0

    def body(a_ref, *rest):
        b_refs, rest = rest[:n_b], rest[n_b:]
        ex, rest = rest[:n_ex], rest[n_ex:]
        ride_in, rest = rest[:r_in], rest[r_in:]
        outs, rest = rest[:len(dts)], rest[len(dts):]
        ride_out, rest = rest[:r_out], rest[r_out:]
        ride_scratch = rest[n_acc:]
        if ride:
            first, mid, last = _steps([pl.program_id(d) for d in range(3)], grid)
            ride.run(ride_in, ride_out, ride_scratch, (first, mid, None))

        def finish(r):
            vals = (r,) if epilogue is None else epilogue(r, *[e[...] for e in ex])
            for o, v in zip(outs, vals):
                o[...] = v.astype(o.dtype)

        if n_b == 1:
            part = _dot(a_ref[...], b_refs[0][...], dims)
        else:
            part = sum(_dot(a_ref[:, p * piece:(p + 1) * piece], b_refs[p][...], dims) for p in range(n_b))
        if nk == 1:
            finish(part)
        else:
            acc = rest[0]
            kk = pl.program_id(2)

            @pl.when(kk == 0)
            def _():
                acc[...] = part

            @pl.when(jnp.logical_and(kk > 0, kk < nk - 1))
            def _():
                acc[...] += part

            @pl.when(kk == nk - 1)
            def _():
                finish(acc[...] + part)

        if ride:
            ride.run(ride_in, ride_out, ride_scratch, (None, None, last))

    res = pl.pallas_call(
        body, name=name, grid=grid,
        in_specs=[a_spec] + b_specs + ex_specs + [ANY] * r_in,
        out_specs=out_specs + [ANY] * r_out,
        out_shape=out_shapes + (ride.out_shape if ride else []),
        scratch_shapes=([pltpu.VMEM((tm, tn), F32)] if nk > 1 else []) + (ride.scratch if ride else []),
        compiler_params=_params(("arbitrary",) * 3 if ride else ("parallel", "parallel", "arbitrary")),
    )(a, *[b] * n_b, *ex_arrays, *(ride.ins if ride else []))
    own = res[0] if single else res[:len(dts)]
    return (own, res[len(dts):]) if ride else own


def _rows(name, body, n_rows, tm, ins, outs, accs=(), into=None):
    in_specs, args = [], []
    for t in ins:
        if len(t) == 1:
            in_specs.append(pl.BlockSpec(t[0].shape, lambda i, nd=t[0].ndim: (0,) * nd))
        else:
            in_specs.append(pl.BlockSpec((tm, t[1]), lambda i, cb=t[2]: (i, cb)))
        args.append(t[0])
    outs = [(o + (o[0], 0))[:4] for o in outs]
    out_specs = [pl.BlockSpec((tm, w), lambda i, cb=cb: (i, cb)) for w, _, _, cb in outs]
    out_shape = [jax.ShapeDtypeStruct((n_rows, total), d) for _, d, total, _ in outs]
    aliases, kernel = {}, body
    if into is not None:
        arr, w, cb = into
        in_specs.append(ANY)
        args.append(arr)
        out_specs.append(pl.BlockSpec((tm, w), lambda i: (i, cb)))
        out_shape.append(jax.ShapeDtypeStruct(arr.shape, arr.dtype))
        aliases = {len(ins): len(outs)}
        n_in = len(ins)

        def kernel(*refs):
            body(*refs[:n_in], *refs[n_in + 1:])

    out_specs += [pl.BlockSpec((r, w), lambda i: (0, 0)) for r, w in accs]
    out_shape += [jax.ShapeDtypeStruct((r, w), F32) for r, w in accs]
    return pl.pallas_call(
        kernel, name=name, grid=(n_rows // tm,), in_specs=in_specs, out_specs=out_specs, out_shape=out_shape,
        input_output_aliases=aliases, compiler_params=_params(("arbitrary",) if accs else ("parallel",)),
    )(*args)


def _zero_first(*accs):
    @pl.when(pl.program_id(0) == 0)
    def _():
        for a in accs:
            a[...] = jnp.zeros_like(a)


def _rope64(t, cos, sin):
    return t * cos + pltpu.roll(t, RET_QK // 2, 1) * sin


def _rope32(t, cos, sin_a, sin_b):
    return t * cos + pltpu.roll(t, LANES - ROPE // 2, 1) * sin_a + pltpu.roll(t, ROPE // 2, 1) * sin_b


def _rms_fwd(name, x, g, tm):
    s, d = x.shape

    def body(x_ref, g_ref, u_ref):
        v = x_ref[...]
        r = lax.rsqrt(jnp.mean(v * v, axis=-1, keepdims=True) + EPS)
        u_ref[...] = (v * r * g_ref[...]).astype(BF16)

    return _rows(name, body, s, tm, [(x, d, 0), (g,)], [(d, BF16)])[0]


def _residual_norm(r, x, g):
    h = x + r
    return h, h * lax.rsqrt(jnp.mean(h * h, axis=-1, keepdims=True) + EPS) * g


def _gate_grads(dm, gr, gm, yr, ym):
    sr, sm = _sig(gr), _sig(gm)
    return dm * sr, dm * sm, jnp.concatenate([dm * yr * (sr * (1.0 - sr)), dm * ym * (sm * (1.0 - sm))], axis=1)


def _rms_bwd(name, dy, x, g, dres, tm, matmul_copy=False):
    s, d = x.shape

    def body(dy_ref, x_ref, g_ref, dres_ref, dx_ref, *rest):
        dg_ref = rest[-1]
        _zero_first(dg_ref)
        v, dyv = x_ref[...], dy_ref[...]
        r = lax.rsqrt(jnp.mean(v * v, axis=-1, keepdims=True) + EPS)
        xh = v * r
        dxh = dyv * g_ref[...]
        dx = dres_ref[...] + r * (dxh - xh * jnp.mean(dxh * xh, axis=-1, keepdims=True))
        dx_ref[...] = dx
        if matmul_copy:
            rest[0][...] = dx.astype(BF16)
        dg_ref[...] += jnp.sum(dyv * xh, axis=0, keepdims=True)

    return _rows(name, body, s, tm, [(dy, d, 0), (x, d, 0), (g,), (dres, d, 0)],
                 [(d, F32)] + [(d, BF16)] * matmul_copy, [(1, d)])


def _final(name, h1, dn, g, tgt, tm):
    s, d = h1.shape

    def body(h_ref, dn_ref, g_ref, t_ref, dh_ref, dh16_ref, dg_ref, loss_ref):
        _zero_first(dg_ref, loss_ref)
        v = h_ref[...] + dn_ref[...]
        r = lax.rsqrt(jnp.mean(v * v, axis=-1, keepdims=True) + EPS)
        xh = v * r
        gv = g_ref[...]
        e = xh * gv - t_ref[...]
        loss_ref[...] += 0.5 * jnp.sum(jnp.mean(e * e, axis=-1, keepdims=True))
        dy = e * (1.0 / d)
        dg_ref[...] += jnp.sum(dy * xh, axis=0, keepdims=True)
        dxh = dy * gv
        dh = r * (dxh - xh * jnp.mean(dxh * xh, axis=-1, keepdims=True))
        dh_ref[...] = dh
        dh16_ref[...] = dh.astype(BF16)

    return _rows(name, body, s, tm, [(h1, d, 0), (dn, d, 0), (g,), (tgt, d, 0)], [(d, F32), (d, BF16)],
                 [(1, d), (1, LANES)])


def _decay_mask(lg, blk):
    n = lax.broadcasted_iota(jnp.int32, (blk, blk), 0)
    m = lax.broadcasted_iota(jnp.int32, (blk, blk), 1)
    w = jnp.exp(lg * jnp.abs(n - m).astype(F32))
    return jnp.where(jnp.right_shift(m, CHUNK_SHIFT) <= jnp.right_shift(n, CHUNK_SHIFT), w, 0.0)


def _decays(lg, blk):
    pos = lax.broadcasted_iota(jnp.int32, (blk, 1), 0).astype(F32)
    return jnp.exp(lg * (pos + 1.0)), jnp.exp(lg * (blk - 1.0 - pos)), jnp.exp(lg * float(blk))


def _ret_fwd(proj, lay, cos, sin, lgs, gain, blk, ride=None):
    s = proj.shape[0]
    heads = lay["ret_heads"]
    nb = s // blk
    scale = RET_QK ** -0.5
    ride_in_specs, ride_ins, ride_out_specs, ride_out_shape, ride_scratch = _ride_args(ride)

    def body(lg_ref, qkv_ref, cos_ref, sin_ref, g_ref, o_ref, st_ref, ry_ref, state, mask):
        lg = lg_ref[0:1, 0:1]

        @pl.when(pl.program_id(1) == 0)
        def _():
            state[...] = jnp.zeros_like(state)
            mask[...] = _decay_mask(lg, blk)

        a, c, gb = _decays(lg, blk)
        q = _rope64(qkv_ref[:, :RET_QK], cos_ref[...], sin_ref[...])
        k = _rope64(qkv_ref[:, RET_QK:2 * RET_QK], cos_ref[...], sin_ref[...]) * scale
        v = qkv_ref[:, 2 * RET_QK:2 * RET_QK + RET_V]
        st = state[...]
        st_ref[...] = st
        sm = _dot(q, k, NT) * mask[...]
        o = _dot(sm, v, NN) + _dot(q * a, st, NN)
        o_ref[...] = o
        state[...] = st * gb + _dot(k * c, v, TN)
        dlt = o - jnp.mean(o, axis=-1, keepdims=True)
        rstd = lax.rsqrt(jnp.mean(dlt * dlt, axis=-1, keepdims=True) + EPS)
        rg = qkv_ref[:, 2 * RET_QK + RET_V:]
        ry_ref[...] = (dlt * rstd * g_ref[...] * (rg * _sig(rg))).astype(BF16)

    first = lay["off"]["heads"] // RET_HEAD
    res = pl.pallas_call(
        _with_ride(body, ride, (heads, nb), 0, 5, 3), name="ret_fwd", grid=(heads, nb),
        in_specs=[pl.BlockSpec((None, 8, LANES), lambda h, b: (h, 0, 0)),
                  pl.BlockSpec((blk, RET_HEAD), lambda h, b: (b, first + h)),
                  pl.BlockSpec((blk, LANES), lambda h, b: (b, 0)),
                  pl.BlockSpec((blk, LANES), lambda h, b: (b, 0)),
                  pl.BlockSpec((1, RET_V), lambda h, b: (0, h))] + ride_in_specs,
        out_specs=[pl.BlockSpec((blk, RET_V), lambda h, b: (b, h)),
                   pl.BlockSpec((None, None, RET_QK, RET_V), lambda h, b: (h, b, 0, 0)),
                   pl.BlockSpec((blk, RET_V), lambda h, b: (b, h))] + ride_out_specs,
        out_shape=[jax.ShapeDtypeStruct((s, heads * RET_V), F32),
                   jax.ShapeDtypeStruct((heads, nb, RET_QK, RET_V), F32),
                   jax.ShapeDtypeStruct((s, heads * RET_V), BF16)] + ride_out_shape,
        scratch_shapes=[pltpu.VMEM((RET_QK, RET_V), F32), pltpu.VMEM((blk, blk), F32)] + ride_scratch,
        compiler_params=_params(("arbitrary", "arbitrary") if ride else ("parallel", "arbitrary")),
    )(lgs, proj, cos, sin, gain, *ride_ins)
    return (res[0], res[1], res[2], res[3:]) if ride else res


def _ret_bwd(proj, lay, cos, sin, lgs, states, d_ry, o, gain, d_proj, blk, ride=None):
    ride_in_specs, ride_ins, ride_out_specs, ride_out_shape, ride_scratch = _ride_args(ride)
    s = proj.shape[0]
    heads = lay["ret_heads"]
    nb = s // blk
    scale = RET_QK ** -0.5

    def body(lg_ref, qkv_ref, cos_ref, sin_ref, st_ref, dry_ref, o_ref, g_ref, _, dqkv_ref, dg_ref, dstate, mask):
        lg = lg_ref[0:1, 0:1]

        @pl.when(pl.program_id(1) == 0)
        def _():
            dstate[...] = jnp.zeros_like(dstate)
            mask[...] = _decay_mask(lg, blk)
            dg_ref[...] = jnp.zeros_like(dg_ref)

        oh = o_ref[...]
        dlt = oh - jnp.mean(oh, axis=-1, keepdims=True)
        rstd = lax.rsqrt(jnp.mean(dlt * dlt, axis=-1, keepdims=True) + EPS)
        oh = dlt * rstd
        gv = g_ref[...]
        rg = qkv_ref[:, 2 * RET_QK + RET_V:]
        sg = _sig(rg)
        dry = dry_ref[...]
        dt = dry * (rg * sg)
        dqkv_ref[:, 2 * RET_QK + RET_V:] = (dry * (oh * gv) * (sg * (1.0 + rg * (1.0 - sg)))).astype(BF16)
        dg_ref[...] += jnp.sum(dt * oh, axis=0, keepdims=True)
        doh = dt * gv
        do = rstd * (doh - jnp.mean(doh, axis=-1, keepdims=True) - oh * jnp.mean(doh * oh, axis=-1, keepdims=True))

        a, c, gb = _decays(lg, blk)
        cs, sn = cos_ref[...], sin_ref[...]
        q = _rope64(qkv_ref[:, :RET_QK], cs, sn)
        k = _rope64(qkv_ref[:, RET_QK:2 * RET_QK], cs, sn) * scale
        v = qkv_ref[:, 2 * RET_QK:2 * RET_QK + RET_V]
        st = st_ref[...]
        dst = dstate[...]
        mk = mask[...]
        sm = _dot(q, k, NT) * mk
        ds = _dot(do, v, NT) * mk
        dq = _dot(ds, k, NN) + _dot(do, st, NT) * a
        dk = _dot(ds, q, TN) + _dot(v, dst, NT) * c
        dqkv_ref[:, 2 * RET_QK:2 * RET_QK + RET_V] = (_dot(sm, do, TN) + _dot(k * c, dst, NN)).astype(BF16)
        dstate[...] = dst * gb + _dot(q * a, do, TN)
        dqkv_ref[:, :RET_QK] = _rope64(dq, cs, -sn).astype(BF16)
        dqkv_ref[:, RET_QK:2 * RET_QK] = (_rope64(dk, cs, -sn) * scale).astype(BF16)

    first = lay["off"]["heads"] // RET_HEAD
    last = nb - 1
    head_tile = pl.BlockSpec((blk, RET_V), lambda h, b: (last - b, h))
    res = pl.pallas_call(
        _with_ride(body, ride, (heads, nb), 0, 9, 2), name="ret_bwd", grid=(heads, nb),
        in_specs=[pl.BlockSpec((None, 8, LANES), lambda h, b: (h, 0, 0)),
                  pl.BlockSpec((blk, RET_HEAD), lambda h, b: (last - b, first + h)),
                  pl.BlockSpec((blk, LANES), lambda h, b: (last - b, 0)),
                  pl.BlockSpec((blk, LANES), lambda h, b: (last - b, 0)),
                  pl.BlockSpec((None, None, RET_QK, RET_V), lambda h, b: (h, last - b, 0, 0)),
                  head_tile, head_tile, pl.BlockSpec((1, RET_V), lambda h, b: (0, h)), ANY] + ride_in_specs,
        out_specs=[pl.BlockSpec((blk, RET_HEAD), lambda h, b: (last - b, first + h)),
                   pl.BlockSpec((1, RET_V), lambda h, b: (0, h))] + ride_out_specs,
        out_shape=[jax.ShapeDtypeStruct(d_proj.shape, d_proj.dtype),
                   jax.ShapeDtypeStruct((1, heads * RET_V), F32)] + ride_out_shape,
        scratch_shapes=[pltpu.VMEM((RET_QK, RET_V), F32), pltpu.VMEM((blk, blk), F32)] + ride_scratch,
        input_output_aliases={8: 0},
        compiler_params=_params(("arbitrary", "arbitrary") if ride else ("parallel", "arbitrary")),
    )(lgs, proj, cos, sin, states, d_ry, o, gain, d_proj, *ride_ins)
    return (res[0], res[1], res[2:]) if ride else res[:2]


def _mla_prep(proj, lay, gq, gkv, cos, sin_a, sin_b, tm):
    s = proj.shape[0]
    ql, kl = lay["q_lora"], lay["kv_lora"]

    def body(cq_ref, ckv_ref, kpe_ref, gq_ref, gkv_ref, cos_ref, sa_ref, sb_ref, cqn_ref, ckvn_ref, kpr_ref):
        for src, gref, dst in ((cq_ref, gq_ref, cqn_ref), (ckv_ref, gkv_ref, ckvn_ref)):
            v = src[...]
            r = lax.rsqrt(jnp.mean(v * v, axis=-1, keepdims=True) + EPS)
            dst[...] = (v * r * gref[...]).astype(BF16)
        kpr_ref[...] = _rope32(kpe_ref[...], cos_ref[...], sa_ref[...], sb_ref[...]).astype(BF16)

    off = lay["off"]
    return _rows("mla_prep", body, s, tm,
                 [(proj, ql, off["c_q"] // ql), (proj, kl, off["c_kv"] // kl), (proj, LANES, off["k_pe"] // LANES),
                  (gq,), (gkv,), (cos, LANES, 0), (sin_a, LANES, 0), (sin_b, LANES, 0)],
                 [(ql, BF16), (kl, BF16), (LANES, BF16)])


def _latent_norm_bwd(name, proj, offset, d_normed, g, d_proj, tm):
    s, w = d_normed.shape

    def body(dy_ref, x_ref, g_ref, dx_ref, dg_ref):
        _zero_first(dg_ref)
        v, dy = x_ref[...], dy_ref[...]
        r = lax.rsqrt(jnp.mean(v * v, axis=-1, keepdims=True) + EPS)
        xh = v * r
        dxh = dy * g_ref[...]
        dx_ref[...] = (r * (dxh - xh * jnp.mean(dxh * xh, axis=-1, keepdims=True))).astype(BF16)
        dg_ref[...] += jnp.sum(dy * xh, axis=0, keepdims=True)

    return _rows(name, body, s, tm, [(d_normed, w, 0), (proj, w, offset // w), (g,)], [], [(1, w)],
                 into=(d_proj, w, offset // w))


def _q_operand(r, cos, sin_a, sin_b):
    qs = (NOPE + ROPE) ** -0.5 * math.log2(math.e)
    cs, sa, sb = cos * qs, sin_a * qs, sin_b * qs
    parts = []
    for lo in range(0, r.shape[1], QPAD):
        parts += [r[:, lo:lo + NOPE] * qs, _rope32(r[:, lo + NOPE:lo + QPAD], cs, sa, sb)]
    return (jnp.concatenate(parts, axis=1),)


def _k_operand(r, kpr):
    parts = []
    for lo in range(0, r.shape[1], QPAD):
        parts += [r[:, lo:lo + NOPE], kpr.astype(F32)]
    return r, jnp.concatenate(parts, axis=1)


def _rope_key_grad(parts, lay, d_proj, tm):
    s, w = parts.shape

    def body(p_ref, dkpe_ref):
        dkpe_ref[:, :LANES] = sum(p_ref[:, lo:lo + LANES] for lo in range(0, w, LANES)).astype(BF16)
        dkpe_ref[:, LANES:] = jnp.zeros((tm, LANES), BF16)

    return _rows("rope_key_grad", body, s, tm, [(parts, w, 0)], [],
                 into=(d_proj, 2 * LANES, lay["off"]["k_pe"] // (2 * LANES)))[0]


def _diag_mask(t, keys_on_rows=False):
    row = lax.broadcasted_iota(jnp.int32, (t, t), 0)
    col = lax.broadcasted_iota(jnp.int32, (t, t), 1)
    key, query = (row, col) if keys_on_rows else (col, row)
    return jnp.right_shift(key, CHUNK_SHIFT) <= jnp.right_shift(query, CHUNK_SHIFT)


def _tile_pairs(nt, by_key):
    if by_key:
        pairs = [(i, j) for j in range(nt) for i in range(j, nt)]
    else:
        pairs = [(i, j) for i in range(nt) for j in range(i + 1)]
    return (jnp.asarray([p[0] for p in pairs], jnp.int32), jnp.asarray([p[1] for p in pairs], jnp.int32))


def _head_block(heads):
    return 4 if heads % 4 == 0 else 2 if heads % 2 == 0 else 1


def _attn_fwd(qf, kf, kv, lay, t, ride=None):
    s = qf.shape[0]
    heads = lay["mla_heads"]
    hb = _head_block(heads)
    nt = s // t
    qi, kj = _tile_pairs(nt, False)
    grid = (heads // hb, int(qi.shape[0]))
    ride_in_specs, ride_ins, ride_out_specs, ride_out_shape, ride_scratch = _ride_args(ride)

    def body(qi_ref, kj_ref, q_ref, k_ref, kv_ref, o_ref, lse_ref, m_s, l_s, acc):
        p = pl.program_id(1)
        i, j = qi_ref[p], kj_ref[p]

        @pl.when(j == 0)
        def _():
            m_s[...] = jnp.full_like(m_s, -jnp.inf)
            l_s[...] = jnp.zeros_like(l_s)
            acc[...] = jnp.zeros_like(acc)

        def step(diagonal):
            ones = jnp.ones((t, LANES), BF16)
            scores = [_dot(q_ref[:, hh * QPAD:(hh + 1) * QPAD], k_ref[:, hh * QPAD:(hh + 1) * QPAD], NT)
                      for hh in range(hb)]
            for hh in range(hb):
                sc = scores[hh]
                if diagonal:
                    sc = jnp.where(_diag_mask(t), sc, -jnp.inf)
                cols = [sc[:, c * LANES:(c + 1) * LANES] for c in range(t // LANES)]
                m_old = m_s[hh]
                m_new = jnp.maximum(m_old, jnp.max(functools.reduce(jnp.maximum, cols), axis=-1, keepdims=True))
                alpha = jnp.exp2(m_old - m_new)
                pr = jnp.concatenate([jnp.exp2(c - m_new).astype(BF16) for c in cols], axis=1)
                pv = _dot(pr, jnp.concatenate([kv_ref[:, hh * QPAD + NOPE:(hh + 1) * QPAD], ones], axis=1), NN)
                l_new = alpha * l_s[hh] + pv[:, VHEAD:]
                a_new = alpha * acc[hh] + pv[:, :VHEAD]
                if diagonal:
                    o_ref[:, hh * VHEAD:(hh + 1) * VHEAD] = a_new / l_new
                    lse_ref[hh] = jnp.transpose(m_new + jnp.log2(l_new))[:1]
                else:
                    m_s[hh], l_s[hh], acc[hh] = m_new, l_new, a_new

        pl.when(j < i)(functools.partial(step, False))
        pl.when(j == i)(functools.partial(step, True))

    res = pl.pallas_call(
        _with_ride(body, ride, grid, 2, 3, 2), name="attn_fwd",
        grid_spec=pltpu.PrefetchScalarGridSpec(
            num_scalar_prefetch=2, grid=grid,
            in_specs=[pl.BlockSpec((t, hb * QPAD), lambda h, p, qi, kj: (qi[p], h)),
                      pl.BlockSpec((t, hb * QPAD), lambda h, p, qi, kj: (kj[p], h)),
                      pl.BlockSpec((t, hb * QPAD), lambda h, p, qi, kj: (kj[p], h))] + ride_in_specs,
            out_specs=[pl.BlockSpec((t, hb * VHEAD), lambda h, p, qi, kj: (qi[p], h)),
                       pl.BlockSpec((hb, 1, t), lambda h, p, qi, kj: (h, 0, qi[p]))] + ride_out_specs,
            scratch_shapes=[pltpu.VMEM((hb, t, LANES), F32), pltpu.VMEM((hb, t, LANES), F32),
                            pltpu.VMEM((hb, t, VHEAD), F32)] + ride_scratch),
        out_shape=[jax.ShapeDtypeStruct((s, heads * VHEAD), F32),
                   jax.ShapeDtypeStruct((heads, 1, s), F32)] + ride_out_shape,
        compiler_params=_params(("arbitrary", "arbitrary") if ride else ("parallel", "arbitrary")),
    )(qi, kj, qf, kf, kv, *ride_ins)
    return (res[0], res[1], res[2:]) if ride else res


def _attn_delta(d_o, o, lay, tm):
    s = o.shape[0]
    heads = lay["mla_heads"]

    def body(do_ref, o_ref, dl_ref):
        for h in range(heads):
            sl = slice(h * VHEAD, (h + 1) * VHEAD)
            dl_ref[h] = jnp.sum(jnp.transpose(do_ref[:, sl] * o_ref[:, sl]), axis=0, keepdims=True)

    tile = pl.BlockSpec((tm, heads * VHEAD), lambda i: (i, 0))
    return pl.pallas_call(
        body, name="attn_delta", grid=(s // tm,), in_specs=[tile, tile],
        out_specs=pl.BlockSpec((heads, 1, tm), lambda i: (0, 0, i)),
        out_shape=jax.ShapeDtypeStruct((heads, 1, s), F32),
        compiler_params=_params(("parallel",)),
    )(d_o, o)


def _attn_bwd(qf, kf, kv, lse, delta, d_o, cos, sin_a, sin_b, lay, t, ride=None):
    s = qf.shape[0]
    heads = lay["mla_heads"]
    hb = _head_block(heads)
    nt = s // t
    scale = (NOPE + ROPE) ** -0.5
    qi, kj = _tile_pairs(nt, True)
    grid = (heads // hb, int(qi.shape[0]))
    ride_in_specs, ride_ins, ride_out_specs, ride_out_shape, ride_scratch = _ride_args(ride)

    def body(qi_ref, kj_ref, q_ref, k_ref, kv_ref, lse_ref, dl_ref, do_ref, cos_ref, sa_ref, sb_ref,
             dqp_ref, dkv_ref, dkpe_ref, dq_acc, dk_acc, dv_acc):
        p = pl.program_id(1)
        i, j = qi_ref[p], kj_ref[p]
        rows = pl.ds(pl.multiple_of(i * t, t), t)

        def unrope(v):
            return _rope32(v, cos_ref[...], -sa_ref[...], -sb_ref[...])

        @pl.when(p == 0)
        def _():
            dq_acc[...] = jnp.zeros_like(dq_acc)

        def step(diagonal):
            for hh in range(hb):
                lo = hh * QPAD
                q, k = q_ref[:, lo:lo + QPAD], k_ref[:, lo:lo + QPAD]
                do = do_ref[:, hh * VHEAD:(hh + 1) * VHEAD]
                pr = jnp.exp2(_dot(k, q, NT) - lse_ref[hh])
                if diagonal:
                    pr = jnp.where(_diag_mask(t, keys_on_rows=True), pr, 0.0)
                dv_part = _dot(pr, do, NN)
                ds = (pr * (_dot(kv_ref[:, lo + NOPE:lo + QPAD], do, NT) - dl_ref[hh])).astype(BF16)
                dk_part = _dot(ds, q, NN)
                dq = dq_acc[rows, lo:lo + QPAD] + _dot(ds, k, TN) * scale
                if diagonal:
                    dqp_ref[:, lo:lo + NOPE] = dq[:, :NOPE].astype(BF16)
                    dqp_ref[:, lo + NOPE:lo + QPAD] = unrope(dq[:, NOPE:]).astype(BF16)
                    dk_acc[hh], dv_acc[hh] = dk_part, dv_part
                else:
                    dq_acc[rows, lo:lo + QPAD] = dq
                    dk_acc[hh] += dk_part
                    dv_acc[hh] += dv_part

        pl.when(i > j)(functools.partial(step, False))
        pl.when(i == j)(functools.partial(step, True))

        @pl.when(i == nt - 1)
        def _():
            kpe = jnp.zeros((t, LANES), F32)
            for hh in range(hb):
                lo = hh * QPAD
                dk = dk_acc[hh] * math.log(2.0)
                dkv_ref[:, lo:lo + NOPE] = dk[:, :NOPE].astype(BF16)
                dkv_ref[:, lo + NOPE:lo + QPAD] = dv_acc[hh].astype(BF16)
                kpe = kpe + dk[:, NOPE:]
            dkpe_ref[...] = unrope(kpe)

    table = pl.BlockSpec((t, LANES), lambda h, p, qi, kj: (kj[p], 0))
    res = pl.pallas_call(
        _with_ride(body, ride, grid, 2, 9, 3), name="attn_bwd",
        grid_spec=pltpu.PrefetchScalarGridSpec(
            num_scalar_prefetch=2, grid=grid,
            in_specs=[pl.BlockSpec((t, hb * QPAD), lambda h, p, qi, kj: (qi[p], h)),
                      pl.BlockSpec((t, hb * QPAD), lambda h, p, qi, kj: (kj[p], h)),
                      pl.BlockSpec((t, hb * QPAD), lambda h, p, qi, kj: (kj[p], h)),
                      pl.BlockSpec((hb, 1, t), lambda h, p, qi, kj: (h, 0, qi[p])),
                      pl.BlockSpec((hb, 1, t), lambda h, p, qi, kj: (h, 0, qi[p])),
                      pl.BlockSpec((t, hb * VHEAD), lambda h, p, qi, kj: (qi[p], h)),
                      table, table, table] + ride_in_specs,
            out_specs=[pl.BlockSpec((t, hb * QPAD), lambda h, p, qi, kj: (kj[p], h)),
                       pl.BlockSpec((t, hb * QPAD), lambda h, p, qi, kj: (kj[p], h)),
                       pl.BlockSpec((t, LANES), lambda h, p, qi, kj: (kj[p], h))] + ride_out_specs,
            scratch_shapes=[pltpu.VMEM((s, hb * QPAD), F32), pltpu.VMEM((hb, t, QPAD), F32),
                            pltpu.VMEM((hb, t, VHEAD), F32)] + ride_scratch),
        out_shape=[jax.ShapeDtypeStruct((s, heads * QPAD), BF16),
                   jax.ShapeDtypeStruct((s, heads * QPAD), BF16),
                   jax.ShapeDtypeStruct((s, heads // hb * LANES), F32)] + ride_out_shape,
        compiler_params=_params(("arbitrary", "arbitrary") if ride else ("parallel", "arbitrary")),
    )(qi, kj, qf, kf, kv, lse, delta, d_o, cos, sin_a, sin_b, *ride_ins)
    return (res[0], res[1], res[2], res[3:]) if ride else res


ANY = pl.BlockSpec(memory_space=pl.ANY)


def _place():
    return lax.axis_index("x"), lax.axis_index("y"), lax.axis_index("c")


def _other_chips(x, y):
    return [(1 - x, y), (x, 1 - y), (1 - x, 1 - y)]


class _Exchange:
    def __init__(self, ins, out_shape, scratch, phases):
        self.ins, self.out_shape, self.scratch, self.phases = list(ins), list(out_shape), list(scratch), phases

    def counts(self):
        return len(self.ins), len(self.out_shape), len(self.scratch)

    def run(self, r_in, r_out, r_scratch, conds):
        for cond, phase in zip(conds, self.phases):
            if phase is not None and cond is not None:
                pl.when(cond)(functools.partial(phase, r_in, r_out, r_scratch))


def _steps(ids, sizes):
    lin, total = 0, 1
    for i, n in zip(ids, sizes):
        lin, total = lin * n + i, total * n
    return lin == 0, lin == total // 2, lin == total - 1


def _ride_args(ride):
    if ride is None:
        return [], [], [], [], []
    n_in, n_out, _ = ride.counts()
    return [ANY] * n_in, ride.ins, [ANY] * n_out, ride.out_shape, ride.scratch


def _with_ride(body, ride, grid, n_prefetch, n_in, n_out):
    if ride is None:
        return body
    r_in, r_out, r_sc = ride.counts()

    def hosted(*refs):
        cuts = (n_prefetch, n_in, r_in, n_out, r_out)
        parts, pos = [], 0
        for n in cuts:
            parts.append(refs[pos:pos + n])
            pos += n
        pre, ins, ride_in, outs, ride_out = parts
        scratch, ride_scratch = refs[pos:len(refs) - r_sc], refs[len(refs) - r_sc:]
        first, mid, last = _steps([pl.program_id(d) for d in range(len(grid))], grid)
        ride.run(ride_in, ride_out, ride_scratch, (first, mid, None))
        body(*pre, *ins, *outs, *scratch)
        ride.run(ride_in, ride_out, ride_scratch, (None, None, last))

    return hosted


def _exchange_alone(name, ex):
    n_in, n_out, _ = ex.counts()

    def body(*refs):
        for phase in ex.phases:
            if phase is not None:
                phase(refs[:n_in], refs[n_in:n_in + n_out], refs[n_in + n_out:])

    return pl.pallas_call(
        body, name=name, in_specs=[ANY] * n_in, out_specs=[ANY] * n_out, out_shape=ex.out_shape,
        scratch_shapes=ex.scratch)(*ex.ins)


def _gather_exchange(shards):
    nw = len(shards)

    def parts(ins, outs, sems):
        send_sems, recv_sems, local_sems = sems
        x, y, c = _place()

        def slot(px, py, pc):
            return 4 * px + 2 * py + pc

        def copy(w, k, rows, to, src=None):
            return pltpu.make_async_remote_copy(
                src_ref=rows if src is None else src, dst_ref=rows, send_sem=send_sems.at[w, k],
                recv_sem=recv_sems.at[w, k], device_id=to, device_id_type=MESH)

        def plan(w, mine):
            side = c if mine else 1 - c
            half = shards[w].shape[0] // 2
            whole = lambda px, py: outs[w].at[slot(px, py, side)]
            top = lambda px, py: outs[w].at[slot(px, py, side), pl.ds(0, half)]
            bottom = lambda px, py: outs[w].at[slot(px, py, side), pl.ds(half, half)]
            xn, yn, sib = (1 - x, y, side), (x, 1 - y, side), (x, y, 1 - side)
            own = ins[w] if mine else None
            return [copy(w, 0, whole(x, y), sib, own), copy(w, 1, whole(x, y), xn, own),
                    copy(w, 2, whole(x, y), yn, own), copy(w, 3, top(1 - x, y), yn), copy(w, 4, bottom(x, 1 - y), xn),
                    copy(w, 5, whole(1 - x, y), sib), copy(w, 6, whole(x, 1 - y), sib),
                    copy(w, 7, top(1 - x, 1 - y), sib), copy(w, 8, bottom(1 - x, 1 - y), sib)]

        def arrivals(w):
            half = shards[w].shape[0] // 2
            at = lambda px, py, *rows: outs[w].at[(slot(px, py, c),) + rows]
            return {1: copy(w, 1, at(1 - x, y), (x, y, c)), 2: copy(w, 2, at(x, 1 - y), (x, y, c)),
                    3: copy(w, 3, at(1 - x, 1 - y, pl.ds(0, half)), (x, y, c)),
                    4: copy(w, 4, at(1 - x, 1 - y, pl.ds(half, half)), (x, y, c))}

        local = [pltpu.make_async_copy(ins[w], outs[w].at[slot(x, y, c)], local_sems.at[w]) for w in range(nw)]
        return plan, arrivals, local

    def start(ins, outs, sems):
        plan, _, local = parts(ins, outs, sems)
        for cp in local:
            cp.start()
        for w in range(nw):
            for k in (0, 1, 2):
                plan(w, True)[k].start()

    def middle(ins, outs, sems):
        plan, arrivals, _ = parts(ins, outs, sems)
        for landed, onward in ((1, (3, 5)), (2, (4, 6))):
            for w in range(nw):
                arrivals(w)[landed].wait_recv()
                for k in onward:
                    plan(w, True)[k].start()

    def finish(ins, outs, sems):
        plan, arrivals, local = parts(ins, outs, sems)
        for landed, onward in ((3, 7), (4, 8)):
            for w in range(nw):
                arrivals(w)[landed].wait_recv()
                plan(w, True)[onward].start()
        for w in range(nw):
            from_sibling = plan(w, False)
            for k in (0, 5, 6, 7, 8):
                from_sibling[k].wait_recv()
            for cp in plan(w, True):
                cp.wait_send()
        for cp in local:
            cp.wait()

    return _Exchange(
        shards, [jax.ShapeDtypeStruct((N_DEV,) + s.shape, s.dtype) for s in shards],
        [pltpu.SemaphoreType.DMA((nw, 9)), pltpu.SemaphoreType.DMA((nw, 9)), pltpu.SemaphoreType.DMA((nw,))],
        (start, middle, finish))


def _sibling_exchange(grads):
    nw = len(grads)

    def copies(ins, outs, sems):
        x, y, c = _place()
        return [pltpu.make_async_remote_copy(
            src_ref=ins[w].at[2 * p + (1 - c)], dst_ref=outs[w].at[p], send_sem=sems[0].at[w, p],
            recv_sem=sems[1].at[w, p], device_id=(x, y, 1 - c), device_id_type=MESH)
            for w in range(nw) for p in range(4)]

    def start(ins, outs, sems):
        for cp in copies(ins, outs, sems):
            cp.start()

    def finish(ins, outs, sems):
        for cp in copies(ins, outs, sems):
            cp.wait()

    return _Exchange(grads, [jax.ShapeDtypeStruct((4,) + g.shape[1:], g.dtype) for g in grads],
                     [pltpu.SemaphoreType.DMA((nw, 4)), pltpu.SemaphoreType.DMA((nw, 4))], (start, None, finish))


def _chips_exchange(sums):
    nw = len(sums)

    def copies(ins, outs, sems):
        x, y, c = _place()
        return [pltpu.make_async_remote_copy(
            src_ref=ins[w].at[2 * px + py], dst_ref=outs[w].at[k], send_sem=sems[0].at[w, k],
            recv_sem=sems[1].at[w, k], device_id=(px, py, c), device_id_type=MESH)
            for w in range(nw) for k, (px, py) in enumerate(_other_chips(x, y))]

    def start(ins, outs, sems):
        for cp in copies(ins, outs, sems):
            cp.start()

    def finish(ins, outs, sems):
        for cp in copies(ins, outs, sems):
            cp.wait()

    return _Exchange(sums, [jax.ShapeDtypeStruct((3,) + g.shape[1:], g.dtype) for g in sums],
                     [pltpu.SemaphoreType.DMA((nw, 3)), pltpu.SemaphoreType.DMA((nw, 3))], (start, None, finish))


def _pair_sum(name, g, got, c_arr, tr):
    _, rows, cols = g.shape
    tr = _tile_rows(rows, tr)

    def body(c_ref, a_ref, b_ref, o_ref):
        o_ref[...] = (a_ref[...].astype(F32) + b_ref[...].astype(F32)).astype(BF16)

    return pl.pallas_call(
        body, name=name,
        grid_spec=pltpu.PrefetchScalarGridSpec(
            num_scalar_prefetch=1, grid=(4, rows // tr),
            in_specs=[pl.BlockSpec((None, tr, cols), lambda p, r, cr: (2 * p + cr[0], r, 0)),
                      pl.BlockSpec((None, tr, cols), lambda p, r, cr: (p, r, 0))],
            out_specs=pl.BlockSpec((None, tr, cols), lambda p, r, cr: (p, r, 0))),
        out_shape=jax.ShapeDtypeStruct((4, rows, cols), BF16),
        compiler_params=_params(("parallel", "parallel")),
    )(c_arr, g, got)


def _tile_rows(rows, pref):
    t = min(rows, pref)
    while rows % t or t % 8:
        t -= 1
    return t


def _adam(w, g, m, v):
    m = ADAM_B1 * m + (1.0 - ADAM_B1) * g
    v = ADAM_B2 * v + (1.0 - ADAM_B2) * (g * g)
    m_hat = m / (1.0 - ADAM_B1 ** ADAM_STEP)
    v_hat = v / (1.0 - ADAM_B2 ** ADAM_STEP)
    return -ADAM_LR * (m_hat / (jnp.sqrt(v_hat) + ADAM_EPS) + ADAM_WD * w), m, v


def _adamw_shard(name, w, m, v, sums, got, chip_arr, tr, tc=None):
    _, rows, cols = w.shape
    tr = _tile_rows(rows, tr)
    tc = cols if tc is None else _tile(cols, tc)

    def body(p_ref, w_ref, m_ref, v_ref, s_ref, r_ref, g_ref, d_ref, nm_ref, nv_ref):
        g = s_ref[...].astype(F32)
        for k in range(3):
            g = g + r_ref[k].astype(F32)
        g_ref[...] = g
        d_ref[...], nm_ref[...], nv_ref[...] = _adam(w_ref[...], g, m_ref[...], v_ref[...])

    tile = pl.BlockSpec((None, tr, tc), lambda r, q, pr: (0, r, q))
    return pl.pallas_call(
        body, name=name,
        grid_spec=pltpu.PrefetchScalarGridSpec(
            num_scalar_prefetch=1, grid=(rows // tr, cols // tc),
            in_specs=[tile, tile, tile,
                      pl.BlockSpec((None, tr, tc), lambda r, q, pr: (pr[0], r, q)),
                      pl.BlockSpec((3, tr, tc), lambda r, q, pr: (0, r, q))],
            out_specs=[tile] * 4),
        out_shape=[jax.ShapeDtypeStruct((1, rows, cols), F32)] * 4,
        compiler_params=_params(("parallel", "parallel")),
    )(chip_arr, w, m, v, sums, got)


def _gains_all_reduce_adam(grads, loss_part, ws, ms, vs):
    n = len(grads)
    widths = [g.shape[1] for g in grads]
    rows, width = -(-(n + 1) // 8) * 8, max(widths)

    def body(*refs):
        g_in, loss_in = refs[:n], refs[n]
        w_in, m_in, v_in = (refs[1 + k * n:1 + (k + 1) * n] for k in (1, 2, 3))
        outs = refs[1 + 4 * n:2 + 8 * n]
        g_out, d_out, m_out, v_out = (outs[k * n:(k + 1) * n] for k in range(4))
        loss_out = outs[4 * n]
        buf, send_sems, recv_sems = refs[2 + 8 * n:]
        x, y, c = _place()
        me = 4 * x + 2 * y + c
        buf[me] = jnp.zeros((rows, width), F32)
        for r in range(n):
            buf[me, r:r + 1, :widths[r]] = g_in[r][...]
        buf[me, n:n + 1, :LANES] = loss_in[...]
        peers = [(x, y, 1 - c)] + [(px, py, pc) for px, py in _other_chips(x, y) for pc in (c, 1 - c)]
        copies = []
        for k, peer in enumerate(peers):
            cp = pltpu.make_async_remote_copy(
                src_ref=buf.at[me], dst_ref=buf.at[me], send_sem=send_sems.at[k], recv_sem=recv_sems.at[k],
                device_id=peer, device_id_type=MESH)
            cp.start()
            copies.append(cp)
        for cp in copies:
            cp.wait()
        total = buf[0]
        for k in range(1, N_DEV):
            total = total + buf[k]
        for r in range(n):
            g = total[r:r + 1, :widths[r]]
            g_out[r][...] = g
            d_out[r][...], m_out[r][...], v_out[r][...] = _adam(w_in[r][...], g, m_in[r][...], v_in[r][...])
        loss_out[...] = total[n:n + 1, :LANES]

    vm = pl.BlockSpec(memory_space=pltpu.VMEM)
    shapes = [jax.ShapeDtypeStruct((1, w), F32) for w in widths]
    res = pl.pallas_call(
        body, name="gains_all_reduce_adamw",
        in_specs=[vm] * (4 * n + 1), out_specs=[vm] * (4 * n + 1),
        out_shape=shapes * 4 + [jax.ShapeDtypeStruct((1, LANES), F32)],
        scratch_shapes=[pltpu.VMEM((N_DEV, rows, width), F32), pltpu.SemaphoreType.DMA((7,)),
                        pltpu.SemaphoreType.DMA((7,))],
    )(*grads, loss_part, *ws, *ms, *vs)
    return res[:n], res[n:2 * n], res[2 * n:3 * n], res[3 * n:4 * n], res[4 * n]


IN_ORDER = ("r_q", "r_k", "r_v", "r_g", "c_q", "c_kv", "k_pe", "g_ret", "g_mla")
RET_HEAD = 2 * RET_QK + 2 * RET_V


def _make_layout(d, vw, qw, ql, kl, mla_w):
    width = {"r_q": qw, "r_k": qw, "r_v": vw, "r_g": vw, "c_q": ql, "c_kv": kl, "k_pe": ROPE, "g_ret": d, "g_mla": d}
    src, o = {}, 0
    for n in IN_ORDER:
        src[n] = o
        o += width[n]
    heads = vw // RET_V
    off, pieces, o = {}, [], 0

    def put(name, w, s):
        nonlocal o
        off.setdefault(name, o)
        pieces.append((o, w, s))
        o += w

    for n in ("g_ret", "g_mla", "c_q"):
        put(n, width[n], src[n])
    for h in range(heads):
        put("heads", RET_QK, src["r_q"] + h * RET_QK)
        put("heads", RET_QK, src["r_k"] + h * RET_QK)
        put("heads", RET_V, src["r_v"] + h * RET_V)
        put("heads", RET_V, src["r_g"] + h * RET_V)
    for n in ("c_kv", "k_pe"):
        put(n, width[n], src[n])
    total = off["k_pe"] + 2 * LANES
    for n, blk in (("g_ret", d), ("g_mla", d), ("c_q", ql), ("heads", RET_HEAD), ("c_kv", kl), ("k_pe", 2 * LANES)):
        assert off[n] % blk == 0
    return {"off": off, "pieces": pieces, "total": total, "n_in": sum(width.values()),
            "ret_heads": heads, "mla_heads": mla_w // VHEAD, "q_lora": ql, "kv_lora": kl}


def _cols_to_full(g):
    n, r, c = g.shape
    return jnp.transpose(g, (1, 0, 2)).reshape(r, n * c)


def _full_to_cols(w):
    r, c = w.shape
    return jnp.transpose(w.reshape(r, N_DEV, c // N_DEV), (1, 0, 2))


def _w_in_to_mine(g, lay):
    _, rows, cols = g.shape
    parts, at = [], 0
    for o, w, s in lay["pieces"]:
        if o > at:
            parts.append(jnp.zeros((rows, o - at), g.dtype))
        while w > 0:
            k, a = divmod(s, cols)
            take = min(w, cols - a)
            parts.append(g[k, :, a:a + take])
            s, w, o = s + take, w - take, o + take
        at = o
    parts.append(jnp.zeros((rows, lay["total"] - at), g.dtype))
    return jnp.concatenate(parts, axis=1)


def _mine_to_blocks(g, lay):
    cols = lay["n_in"] // N_DEV
    by_src = sorted(lay["pieces"], key=lambda p: p[2])
    blocks = []
    for k in range(N_DEV):
        lo, hi, parts = k * cols, (k + 1) * cols, []
        for o, w, s in by_src:
            a, b = max(lo, s), min(hi, s + w)
            if a < b:
                parts.append(g[:, o + a - s:o + b - s].T)
        blocks.append(jnp.concatenate(parts, axis=0))
    return jnp.stack(blocks)


def _rope_tables(positions, half):
    inv = ROPE_THETA ** (-jnp.arange(half, dtype=F32) / half)
    ang = positions.astype(F32)[:, None] * inv
    return jnp.cos(ang), jnp.sin(ang)


def kernel(x, positions, norm_mix_g, w_in, ret_norm_g, w_ret_o, q_a_norm_g, w_q_b, kv_a_norm_g, w_kv_b, w_mla_o, w_out, norm_mlp_g, w_up, w_down, norm_f_g, loss_target, m_norm_mix_g, m_w_in, m_ret_norm_g, m_w_ret_o, m_q_a_norm_g, m_w_q_b, m_kv_a_norm_g, m_w_kv_b, m_w_mla_o, m_w_out, m_norm_mlp_g, m_w_up, m_w_down, m_norm_f_g, v_norm_mix_g, v_w_in, v_ret_norm_g, v_w_ret_o, v_q_a_norm_g, v_w_q_b, v_kv_a_norm_g, v_w_kv_b, v_w_mla_o, v_w_out, v_norm_mlp_g, v_w_up, v_w_down, v_norm_f_g):
    xs, tgt, pos = x[0], loss_target[0], positions[0]
    s, d = xs.shape
    mats = {"w_in": w_in[0], "w_ret_o": w_ret_o[0], "w_q_b": w_q_b[0], "w_kv_b": w_kv_b[0], "w_mla_o": w_mla_o[0],
            "w_out": w_out[0], "w_up": w_up[0], "w_down": w_down[0]}
    mat_w = {"w_in": w_in, "w_ret_o": w_ret_o, "w_q_b": w_q_b, "w_kv_b": w_kv_b, "w_mla_o": w_mla_o, "w_out": w_out,
             "w_up": w_up, "w_down": w_down}
    mat_m = {"w_in": m_w_in, "w_ret_o": m_w_ret_o, "w_q_b": m_w_q_b, "w_kv_b": m_w_kv_b, "w_mla_o": m_w_mla_o,
             "w_out": m_w_out, "w_up": m_w_up, "w_down": m_w_down}
    mat_v = {"w_in": v_w_in, "w_ret_o": v_w_ret_o, "w_q_b": v_w_q_b, "w_kv_b": v_w_kv_b, "w_mla_o": v_w_mla_o,
             "w_out": v_w_out, "w_up": v_w_up, "w_down": v_w_down}
    names = list(mats)
    col_sharded = ("w_in", "w_q_b", "w_kv_b", "w_up")
    vw = ret_norm_g.shape[1]
    mla_w = mats["w_mla_o"].shape[0] * N_DEV
    ql, kl = q_a_norm_g.shape[1], kv_a_norm_g.shape[1]
    n_in = mats["w_in"].shape[1] * N_DEV
    qw = (n_in - 2 * vw - ql - kl - ROPE - 2 * d) // 2
    lay = _make_layout(d, vw, qw, ql, kl, mla_w)
    assert lay["n_in"] == n_in
    heads_r, heads_m = lay["ret_heads"], lay["mla_heads"]

    shard16 = {n: mats[n].astype(BF16) for n in names}
    with_in_proj = ("w_ret_o", "w_q_b", "w_kv_b", "w_mla_o", "w_out")
    mlp = ("w_up", "w_down")
    by_device = ("w_up", "w_kv_b")
    full = {}

    def keep(group, gathered):
        for n, g in zip(group, gathered):
            if n not in by_device:
                g = _cols_to_full(g) if n in col_sharded else g.reshape(-1, g.shape[2])
            full[n] = g

    w_mine = _w_in_to_mine(_exchange_alone("gather_w_in", _gather_exchange([shard16["w_in"]]))[0], lay)

    c64, s64 = _rope_tables(pos, RET_QK // 2)
    cos_r = jnp.concatenate([c64, c64], axis=1)
    sin_r = jnp.concatenate([-s64, s64], axis=1)
    c32, s32 = _rope_tables(pos, ROPE // 2)
    z32, z64 = jnp.zeros_like(c32), jnp.zeros((s, LANES - ROPE), F32)
    cos_p = jnp.concatenate([c32, c32, z64], axis=1)
    sin_a = jnp.concatenate([-s32, z32, z64], axis=1)
    sin_b = jnp.concatenate([z32, s32, z64], axis=1)
    lg = jnp.log(1.0 - 2.0 ** (-5.0 - jnp.arange(heads_r, dtype=F32)))
    lgs = jnp.broadcast_to(lg[:, None, None], (heads_r, 8, LANES))

    tm = min(256, s)
    blk = min(512, s)
    t_att = min(512, s)

    u = _rms_fwd("norm_mix", xs, norm_mix_g, tm)
    proj, gathered = _mm("in_proj", u, w_mine, "nn", F32,
                         ride=_gather_exchange([shard16[n] for n in with_in_proj]))
    keep(with_in_proj, gathered)
    wq_pad = jnp.pad(full["w_q_b"].reshape(ql, heads_m, NOPE + ROPE),
                     ((0, 0), (0, 0), (0, QPAD - NOPE - ROPE))).reshape(ql, heads_m * QPAD)
    o_ret, states, ry = _ret_fwd(proj, lay, cos_r, sin_r, lgs, ret_norm_g, blk)
    y_ret = _mm("ret_out", ry, full["w_ret_o"], "nn", F32)
    cqn, ckvn, kpr = _mla_prep(proj, lay, q_a_norm_g, kv_a_norm_g, cos_p, sin_a, sin_b, tm)
    qf = _mm("q_up", cqn, wq_pad, "nn", BF16, extras=((cos_p, None), (sin_a, None), (sin_b, None)), epilogue=_q_operand)
    kv, kf = _mm("kv_up", ckvn, full["w_kv_b"], "nn", (BF16, BF16), b_by_device=True, extras=((kpr, None),),
                 epilogue=_k_operand)
    o_mla, lse, gathered = _attn_fwd(qf, kf, kv, lay, t_att, ride=_gather_exchange([shard16["w_up"]]))
    keep(("w_up",), gathered)
    gate_tile = _tile(d, 1024)
    y_mla, merged = _mm(
        "mla_out", o_mla, full["w_mla_o"], "nn", (F32, BF16), tm=512, tn=gate_tile,
        extras=((proj, lay["off"]["g_ret"] // gate_tile), (proj, lay["off"]["g_mla"] // gate_tile), y_ret),
        epilogue=lambda r, gr, gm, yr: (r, _sig(gr) * yr + _sig(gm) * r))
    h1, n2 = _mm("out_proj", merged, full["w_out"], "nn", (F32, BF16), tm=512, tn=d,
                 extras=(xs, (norm_mlp_g, "whole")), epilogue=_residual_norm)
    (z, act), gathered = _mm("mlp_up", n2, full["w_up"], "nn", (F32, BF16), b_by_device=True,
                             epilogue=lambda r: (r, jnp.square(jnp.maximum(r, 0.0))),
                             ride=_gather_exchange([shard16["w_down"]]))
    keep(("w_down",), gathered)
    dn = _mm("mlp_down", act, full["w_down"], "nn", F32, tk=4096)
    dh2, dh2_16, g_norm_f, loss_part = _final("loss_head", h1, dn, norm_f_g.reshape(1, d), tgt, tm)

    mx, my, mc = _place()
    c_arr = jnp.reshape(mc, (1,)).astype(jnp.int32)
    chip_arr = jnp.reshape(2 * mx + my, (1,)).astype(jnp.int32)
    sums, from_chips = {}, {}

    def blocks(group, grads):
        return [g if n in by_device else (_full_to_cols(g) if n in col_sharded else g.reshape((N_DEV,) + mats[n].shape))
                for n, g in zip(group, grads)]

    def pair_sums(group, mine, from_sibling):
        for n, g, r in zip(group, mine, from_sibling):
            sums[n] = _pair_sum("pair_sum_" + n, g, r, c_arr, 512)
        return [sums[n] for n in group]

    dz = _mm("mlp_down_dx", dh2_16, full["w_down"], "nt", BF16, extras=(z,),
             epilogue=lambda r, zz: (r * (2.0 * jnp.maximum(zz, 0.0)),))
    g_w_down = _mm("mlp_down_dw", act, dh2_16, "tn", BF16, tm=512, tn=d, tk=s)
    down_blocks = blocks(("w_down",), (g_w_down,))
    g_w_up, got_down = _mm("mlp_up_dw", n2, dz, "tn", BF16, tk=s, out_by_device=True,
                           ride=_sibling_exchange(down_blocks))
    dn2, got_up = _mm("mlp_up_dx", dz, full["w_up"], "nt", F32, tk=4096, b_by_device=True,
                      ride=_sibling_exchange([g_w_up]))
    mlp_sums = pair_sums(mlp, [g_w_up] + down_blocks, list(got_up) + list(got_down))
    dh1, dh1_16, g_norm_mlp = _rms_bwd("norm_mlp_bwd", dn2, h1, norm_mlp_g, dh2, tm, matmul_copy=True)
    assert lay["off"]["g_ret"] == 0 and lay["off"]["g_mla"] == d
    dy_ret, dy_mla, d_proj = _mm(
        "out_proj_dx", dh1_16, full["w_out"], "nt", (BF16, BF16, (BF16, 2, lay["total"])), tm=256, tn=d,
        extras=((proj, 0), (proj, 1), y_ret, y_mla), epilogue=_gate_grads)
    g_w_out = _mm("out_proj_dw", merged, dh1_16, "tn", BF16, tm=512, tn=d, tk=s)
    g_w_ret_o = _mm("ret_out_dw", ry, dy_ret, "tn", BF16, tm=512, tn=d, tk=s)
    g_w_mla_o = _mm("mla_out_dw", o_mla, dy_mla, "tn", BF16, tm=256, tn=d, tk=s)
    mixer = ("w_out", "w_ret_o", "w_mla_o")
    mixer_blocks = blocks(mixer, (g_w_out, g_w_ret_o, g_w_mla_o))
    d_ry, got = _mm("ret_out_dx", dy_ret, full["w_ret_o"], "nt", F32, ride=_sibling_exchange(mixer_blocks))
    mixer_sums = pair_sums(mixer, mixer_blocks, got)
    d_omla = _mm("mla_out_dx", dy_mla, full["w_mla_o"], "nt", F32)
    d_proj, g_ret_norm, got = _ret_bwd(proj, lay, cos_r, sin_r, lgs, states, d_ry, o_ret, ret_norm_g, d_proj, blk,
                                       ride=_chips_exchange(mixer_sums))
    from_chips.update(zip(mixer, got))
    delta = _attn_delta(d_omla, o_mla, lay, t_att)
    dqp, dkv, dkpe_parts, got = _attn_bwd(qf, kf, kv, lse, delta, d_omla, cos_p, sin_a, sin_b, lay, t_att,
                                          ride=_chips_exchange(mlp_sums))
    from_chips.update(zip(mlp, got))
    d_proj = _rope_key_grad(dkpe_parts, lay, d_proj, tm)
    d_cqn = _mm("q_up_dx", dqp, wq_pad, "nt", F32)
    g_wq_pad = _mm("q_up_dw", cqn, dqp, "tn", BF16)
    d_ckvn = _mm("kv_up_dx", dkv, full["w_kv_b"], "nt", F32, b_by_device=True)
    g_w_kv_b = _mm("kv_up_dw", ckvn, dkv, "tn", BF16, out_by_device=True)
    d_proj, g_q_a = _latent_norm_bwd("q_latent_bwd", proj, lay["off"]["c_q"], d_cqn, q_a_norm_g, d_proj, tm)
    d_proj, g_kv_a = _latent_norm_bwd("kv_latent_bwd", proj, lay["off"]["c_kv"], d_ckvn, kv_a_norm_g, d_proj, tm)
    g_w_q_b = g_wq_pad.reshape(ql, heads_m, QPAD)[:, :, :NOPE + ROPE].reshape(ql, heads_m * (NOPE + ROPE))
    latent = ("w_q_b", "w_kv_b")
    latent_blocks = blocks(latent, (g_w_q_b, g_w_kv_b))
    latent_sums = pair_sums(latent, latent_blocks,
                            _exchange_alone("latent_grads_to_sibling", _sibling_exchange(latent_blocks)))
    g_w_mine, got = _mm("in_proj_dw", u, d_proj, "tn", BF16, tk=s, ride=_chips_exchange(latent_sums))
    from_chips.update(zip(latent, got))
    last = ("w_in",)
    last_blocks = [_mine_to_blocks(g_w_mine, lay)]
    last_sums = pair_sums(last, last_blocks, _exchange_alone("grads_to_sibling", _sibling_exchange(last_blocks)))
    du, got = _mm("in_proj_dx", d_proj, w_mine, "nt", F32, ride=_chips_exchange(last_sums))
    from_chips.update(zip(last, got))
    grad_x, g_norm_mix = _rms_bwd("norm_mix_bwd", du, xs, norm_mix_g, dh1, tm)

    upd = {n: _adamw_shard("adamw_" + n, mat_w[n], mat_m[n], mat_v[n], sums[n], from_chips[n], chip_arr, 256)
           for n in names if n != "w_in"}
    flip = lambda a: jnp.swapaxes(a, 1, 2)
    upd["w_in"] = [flip(o) for o in _adamw_shard(
        "adamw_w_in", flip(w_in), flip(m_w_in), flip(v_w_in), sums["w_in"], from_chips["w_in"], chip_arr, 512, tc=512)]

    gains = [("norm_mix_g", norm_mix_g, m_norm_mix_g, v_norm_mix_g, g_norm_mix),
             ("ret_norm_g", ret_norm_g, m_ret_norm_g, v_ret_norm_g, g_ret_norm),
             ("q_a_norm_g", q_a_norm_g, m_q_a_norm_g, v_q_a_norm_g, g_q_a),
             ("kv_a_norm_g", kv_a_norm_g, m_kv_a_norm_g, v_kv_a_norm_g, g_kv_a),
             ("norm_mlp_g", norm_mlp_g, m_norm_mlp_g, v_norm_mlp_g, g_norm_mlp),
             ("norm_f_g", norm_f_g, m_norm_f_g, v_norm_f_g, g_norm_f)]
    as_row = lambda a: a.reshape(1, -1)
    g_sm, d_sm, m_sm, v_sm, loss_row = _gains_all_reduce_adam(
        [g[4] for g in gains], loss_part, *[[as_row(g[k]) for g in gains] for k in (1, 2, 3)])
    loss = loss_row[0, 0]
    small = {g[0]: [a[r].reshape(g[1].shape) for a in (g_sm, d_sm, m_sm, v_sm)] for r, g in enumerate(gains)}

    order = ["norm_mix_g", "w_in", "ret_norm_g", "w_ret_o", "q_a_norm_g", "w_q_b", "kv_a_norm_g", "w_kv_b", "w_mla_o",
             "w_out", "norm_mlp_g", "w_up", "w_down", "norm_f_g"]
    outs = [loss, grad_x[None]]
    for k in range(4):
        for n in order:
            outs.append(small[n][k] if n in small else upd[n][k])
    return tuple(outs)
```

```python
import functools
import math

import jax
import jax.numpy as jnp
from jax import lax
from jax.experimental import pallas as pl
from jax.experimental.pallas import tpu as pltpu

F32 = jnp.float32
BF16 = jnp.bfloat16
MESH = pl.DeviceIdType.MESH

EPS = 1e-6
ROPE_THETA = 10000.0
CHUNK_SHIFT = 6
RET_QK = 128
RET_V = 256
NOPE = 128
ROPE = 64
VHEAD = 128
QPAD = 256
LANES = 128
N_DEV = 8
VMEM_LIMIT = 56 * 1024 * 1024

ADAM_LR = 0.001
ADAM_B1 = 0.9
ADAM_B2 = 0.999
ADAM_EPS = 1e-08
ADAM_WD = 0.01
ADAM_STEP = 10

NN = (((1,), (0,)), ((), ()))
NT = (((1,), (1,)), ((), ()))
TN = (((0,), (0,)), ((), ()))


def _dot(a, b, dims):
    return lax.dot_general(a.astype(BF16), b.astype(BF16), dims, preferred_element_type=F32)


def _tile(dim, pref):
    if dim <= pref:
        return dim
    t = (pref // LANES) * LANES
    while t >= LANES:
        if dim % t == 0:
            return t
        t -= LANES
    raise ValueError(f"no tile for {dim}")


def _params(sem):
    return pltpu.CompilerParams(dimension_semantics=sem, vmem_limit_bytes=VMEM_LIMIT)


def _sig(v):
    return 1.0 / (1.0 + jnp.exp(-v))


def _mm(name, a, b, mode, out_dtypes, *, tm=1024, tn=1024, tk=2048, extras=(), epilogue=None, ride=None,
        b_by_device=False, out_by_device=False):
    if b_by_device:
        b_cols = b.shape[2]
        b_shape = (b.shape[1], N_DEV * b_cols)
    else:
        b_shape = b.shape
    if mode == "nn":
        (m, k), (_, n) = a.shape, b_shape
    elif mode == "nt":
        (m, k), (n, _) = a.shape, b_shape
    else:
        (k, m), (_, n) = a.shape, b_shape
    tm, tn, tk = _tile(m, tm), _tile(n, tn), _tile(k, tk)
    if b_by_device and mode != "nt":
        tn = _tile(b_cols, tn)
    if out_by_device:
        tn = _tile(n // N_DEV, tn)
    nk = k // tk
    dims = {"nn": NN, "nt": NT, "tn": TN}[mode]
    a_spec = (pl.BlockSpec((tk, tm), lambda i, j, kk: (kk, i)) if mode == "tn"
              else pl.BlockSpec((tm, tk), lambda i, j, kk: (i, kk)))
    if b_by_device and mode == "nt":
        piece = min(tk, b_cols)
        n_b, per = tk // piece, b_cols // piece
        b_specs = [pl.BlockSpec((None, tn, piece),
                                lambda i, j, kk, p=p: ((kk * n_b + p) // per, j, (kk * n_b + p) % per))
                   for p in range(n_b)]
    elif b_by_device:
        per = b_cols // tn
        n_b, piece = 1, tk
        b_specs = [pl.BlockSpec((None, tk, tn), lambda i, j, kk: (j // per, kk, j % per))]
    else:
        n_b, piece = 1, tk
        b_specs = [pl.BlockSpec((tn, tk), lambda i, j, kk: (j, kk)) if mode == "nt"
                   else pl.BlockSpec((tk, tn), lambda i, j, kk: (kk, j))]
    tile_spec = pl.BlockSpec((tm, tn), lambda i, j, kk: (i, j))
    if out_by_device:
        per_out = n // N_DEV // tn
        out_spec = pl.BlockSpec((None, tm, tn), lambda i, j, kk: (j // per_out, i, j % per_out))
        out_dims = (N_DEV, m, n // N_DEV)
    else:
        out_spec, out_dims = tile_spec, (m, n)
    ex_arrays, ex_specs = [], []
    for e in extras:
        arr, off = e if isinstance(e, tuple) else (e, 0)
        ex_arrays.append(arr)
        if off == "whole":
            ex_specs.append(pl.BlockSpec(arr.shape, lambda i, j, kk, nd=arr.ndim: (0,) * nd))
        elif off is None:
            ex_specs.append(pl.BlockSpec((tm, arr.shape[1]), lambda i, j, kk: (i, 0)))
        else:
            ex_specs.append(pl.BlockSpec((tm, tn), lambda i, j, kk, off=off: (i, j + off)))
    n_ex = len(extras)
    single = not isinstance(out_dtypes, (tuple, list))
    dts = (out_dtypes,) if single else tuple(out_dtypes)
    out_specs, out_shapes = [], []
    for dt in dts:
        if isinstance(dt, tuple):
            dt, mult, width = dt
            out_specs.append(pl.BlockSpec((tm, mult * tn), lambda i, j, kk: (i, j)))
            out_shapes.append(jax.ShapeDtypeStruct((m, width), dt))
        else:
            out_specs.append(out_spec)
            out_shapes.append(jax.ShapeDtypeStruct(out_dims, dt))

    grid = (m // tm, n // tn, nk)
    r_in, r_out, r_sc = ride.counts() if ride else (0, 0, 0)
    n_acc = 1 if nk > 1 else 0

    def body(a_ref, *rest):
        b_refs, rest = rest[:n_b], rest[n_b:]
        ex, rest = rest[:n_ex], rest[n_ex:]
        ride_in, rest = rest[:r_in], rest[r_in:]
        outs, rest = rest[:len(dts)], rest[len(dts):]
        ride_out, rest = rest[:r_out], rest[r_out:]
        ride_scratch = rest[n_acc:]
        if ride:
            first, mid, last = _steps([pl.program_id(d) for d in range(3)], grid)
            ride.run(ride_in, ride_out, ride_scratch, (first, mid, None))

        def finish(r):
            vals = (r,) if epilogue is None else epilogue(r, *[e[...] for e in ex])
            for o, v in zip(outs, vals):
                o[...] = v.astype(o.dtype)

        if n_b == 1:
            part = _dot(a_ref[...], b_refs[0][...], dims)
        else:
            part = sum(_dot(a_ref[:, p * piece:(p + 1) * piece], b_refs[p][...], dims) for p in range(n_b))
        if nk == 1:
            finish(part)
        else:
            acc = rest[0]
            kk = pl.program_id(2)

            @pl.when(kk == 0)
            def _():
                acc[...] = part

            @pl.when(jnp.logical_and(kk > 0, kk < nk - 1))
            def _():
                acc[...] += part

            @pl.when(kk == nk - 1)
            def _():
                finish(acc[...] + part)

        if ride:
            ride.run(ride_in, ride_out, ride_scratch, (None, None, last))

    res = pl.pallas_call(
        body, name=name, grid=grid,
        in_specs=[a_spec] + b_specs + ex_specs + [ANY] * r_in,
        out_specs=out_specs + [ANY] * r_out,
        out_shape=out_shapes + (ride.out_shape if ride else []),
        scratch_shapes=([pltpu.VMEM((tm, tn), F32)] if nk > 1 else []) + (ride.scratch if ride else []),
        compiler_params=_params(("arbitrary",) * 3 if ride else ("parallel", "parallel", "arbitrary")),
    )(a, *[b] * n_b, *ex_arrays, *(ride.ins if ride else []))
    own = res[0] if single else res[:len(dts)]
    return (own, res[len(dts):]) if ride else own


def _rows(name, body, n_rows, tm, ins, outs, accs=(), into=None):
    in_specs, args = [], []
    for t in ins:
        if len(t) == 1:
            in_specs.append(pl.BlockSpec(t[0].shape, lambda i, nd=t[0].ndim: (0,) * nd))
        else:
            in_specs.append(pl.BlockSpec((tm, t[1]), lambda i, cb=t[2]: (i, cb)))
        args.append(t[0])
    outs = [(o + (o[0], 0))[:4] for o in outs]
    out_specs = [pl.BlockSpec((tm, w), lambda i, cb=cb: (i, cb)) for w, _, _, cb in outs]
    out_shape = [jax.ShapeDtypeStruct((n_rows, total), d) for _, d, total, _ in outs]
    aliases, kernel = {}, body
    if into is not None:
        arr, w, cb = into
        in_specs.append(ANY)
        args.append(arr)
        out_specs.append(pl.BlockSpec((tm, w), lambda i: (i, cb)))
        out_shape.append(jax.ShapeDtypeStruct(arr.shape, arr.dtype))
        aliases = {len(ins): len(outs)}
        n_in = len(ins)

        def kernel(*refs):
            body(*refs[:n_in], *refs[n_in + 1:])

    out_specs += [pl.BlockSpec((r, w), lambda i: (0, 0)) for r, w in accs]
    out_shape += [jax.ShapeDtypeStruct((r, w), F32) for r, w in accs]
    return pl.pallas_call(
        kernel, name=name, grid=(n_rows // tm,), in_specs=in_specs, out_specs=out_specs, out_shape=out_shape,
        input_output_aliases=aliases, compiler_params=_params(("arbitrary",) if accs else ("parallel",)),
    )(*args)


def _zero_first(*accs):
    @pl.when(pl.program_id(0) == 0)
    def _():
        for a in accs:
            a[...] = jnp.zeros_like(a)


def _rope64(t, cos, sin):
    return t * cos + pltpu.roll(t, RET_QK // 2, 1) * sin


def _rope32(t, cos, sin_a, sin_b):
    return t * cos + pltpu.roll(t, LANES - ROPE // 2, 1) * sin_a + pltpu.roll(t, ROPE // 2, 1) * sin_b


def _rms_fwd(name, x, g, tm):
    s, d = x.shape

    def body(x_ref, g_ref, u_ref):
        v = x_ref[...]
        r = lax.rsqrt(jnp.mean(v * v, axis=-1, keepdims=True) + EPS)
        u_ref[...] = (v * r * g_ref[...]).astype(BF16)

    return _rows(name, body, s, tm, [(x, d, 0), (g,)], [(d, BF16)])[0]


def _residual_norm(r, x, g):
    h = x + r
    return h, h * lax.rsqrt(jnp.mean(h * h, axis=-1, keepdims=True) + EPS) * g


def _gate_grads(dm, gr, gm, yr, ym):
    sr, sm = _sig(gr), _sig(gm)
    return dm * sr, dm * sm, jnp.concatenate([dm * yr * (sr * (1.0 - sr)), dm * ym * (sm * (1.0 - sm))], axis=1)


def _rms_bwd(name, dy, x, g, dres, tm, matmul_copy=False):
    s, d = x.shape

    def body(dy_ref, x_ref, g_ref, dres_ref, dx_ref, *rest):
        dg_ref = rest[-1]
        _zero_first(dg_ref)
        v, dyv = x_ref[...], dy_ref[...]
        r = lax.rsqrt(jnp.mean(v * v, axis=-1, keepdims=True) + EPS)
        xh = v * r
        dxh = dyv * g_ref[...]
        dx = dres_ref[...] + r * (dxh - xh * jnp.mean(dxh * xh, axis=-1, keepdims=True))
        dx_ref[...] = dx
        if matmul_copy:
            rest[0][...] = dx.astype(BF16)
        dg_ref[...] += jnp.sum(dyv * xh, axis=0, keepdims=True)

    return _rows(name, body, s, tm, [(dy, d, 0), (x, d, 0), (g,), (dres, d, 0)],
                 [(d, F32)] + [(d, BF16)] * matmul_copy, [(1, d)])


def _final(name, h1, dn, g, tgt, tm):
    s, d = h1.shape

    def body(h_ref, dn_ref, g_ref, t_ref, dh_ref, dh16_ref, dg_ref, loss_ref):
        _zero_first(dg_ref, loss_ref)
        v = h_ref[...] + dn_ref[...]
        r = lax.rsqrt(jnp.mean(v * v, axis=-1, keepdims=True) + EPS)
        xh = v * r
        gv = g_ref[...]
        e = xh * gv - t_ref[...]
        loss_ref[...] += 0.5 * jnp.sum(jnp.mean(e * e, axis=-1, keepdims=True))
        dy = e * (1.0 / d)
        dg_ref[...] += jnp.sum(dy * xh, axis=0, keepdims=True)
        dxh = dy * gv
        dh = r * (dxh - xh * jnp.mean(dxh * xh, axis=-1, keepdims=True))
        dh_ref[...] = dh
        dh16_ref[...] = dh.astype(BF16)

    return _rows(name, body, s, tm, [(h1, d, 0), (dn, d, 0), (g,), (tgt, d, 0)], [(d, F32), (d, BF16)],
                 [(1, d), (1, LANES)])


def _decay_mask(lg, blk):
    n = lax.broadcasted_iota(jnp.int32, (blk, blk), 0)
    m = lax.broadcasted_iota(jnp.int32, (blk, blk), 1)
    w = jnp.exp(lg * jnp.abs(n - m).astype(F32))
    return jnp.where(jnp.right_shift(m, CHUNK_SHIFT) <= jnp.right_shift(n, CHUNK_SHIFT), w, 0.0)


def _decays(lg, blk):
    pos = lax.broadcasted_iota(jnp.int32, (blk, 1), 0).astype(F32)
    return jnp.exp(lg * (pos + 1.0)), jnp.exp(lg * (blk - 1.0 - pos)), jnp.exp(lg * float(blk))


def _ret_fwd(proj, lay, cos, sin, lgs, gain, blk, ride=None):
    s = proj.shape[0]
    heads = lay["ret_heads"]
    nb = s // blk
    scale = RET_QK ** -0.5
    ride_in_specs, ride_ins, ride_out_specs, ride_out_shape, ride_scratch = _ride_args(ride)

    def body(lg_ref, qkv_ref, cos_ref, sin_ref, g_ref, o_ref, st_ref, ry_ref, state, mask):
        lg = lg_ref[0:1, 0:1]

        @pl.when(pl.program_id(1) == 0)
        def _():
            state[...] = jnp.zeros_like(state)
            mask[...] = _decay_mask(lg, blk)

        a, c, gb = _decays(lg, blk)
        q = _rope64(qkv_ref[:, :RET_QK], cos_ref[...], sin_ref[...])
        k = _rope64(qkv_ref[:, RET_QK:2 * RET_QK], cos_ref[...], sin_ref[...]) * scale
        v = qkv_ref[:, 2 * RET_QK:2 * RET_QK + RET_V]
        st = state[...]
        st_ref[...] = st
        sm = _dot(q, k, NT) * mask[...]
        o = _dot(sm, v, NN) + _dot(q * a, st, NN)
        o_ref[...] = o
        state[...] = st * gb + _dot(k * c, v, TN)
        dlt = o - jnp.mean(o, axis=-1, keepdims=True)
        rstd = lax.rsqrt(jnp.mean(dlt * dlt, axis=-1, keepdims=True) + EPS)
        rg = qkv_ref[:, 2 * RET_QK + RET_V:]
        ry_ref[...] = (dlt * rstd * g_ref[...] * (rg * _sig(rg))).astype(BF16)

    first = lay["off"]["heads"] // RET_HEAD
    res = pl.pallas_call(
        _with_ride(body, ride, (heads, nb), 0, 5, 3), name="ret_fwd", grid=(heads, nb),
        in_specs=[pl.BlockSpec((None, 8, LANES), lambda h, b: (h, 0, 0)),
                  pl.BlockSpec((blk, RET_HEAD), lambda h, b: (b, first + h)),
                  pl.BlockSpec((blk, LANES), lambda h, b: (b, 0)),
                  pl.BlockSpec((blk, LANES), lambda h, b: (b, 0)),
                  pl.BlockSpec((1, RET_V), lambda h, b: (0, h))] + ride_in_specs,
        out_specs=[pl.BlockSpec((blk, RET_V), lambda h, b: (b, h)),
                   pl.BlockSpec((None, None, RET_QK, RET_V), lambda h, b: (h, b, 0, 0)),
                   pl.BlockSpec((blk, RET_V), lambda h, b: (b, h))] + ride_out_specs,
        out_shape=[jax.ShapeDtypeStruct((s, heads * RET_V), F32),
                   jax.ShapeDtypeStruct((heads, nb, RET_QK, RET_V), F32),
                   jax.ShapeDtypeStruct((s, heads * RET_V), BF16)] + ride_out_shape,
        scratch_shapes=[pltpu.VMEM((RET_QK, RET_V), F32), pltpu.VMEM((blk, blk), F32)] + ride_scratch,
        compiler_params=_params(("arbitrary", "arbitrary") if ride else ("parallel", "arbitrary")),
    )(lgs, proj, cos, sin, gain, *ride_ins)
    return (res[0], res[1], res[2], res[3:]) if ride else res


def _ret_bwd(proj, lay, cos, sin, lgs, states, d_ry, o, gain, d_proj, blk, ride=None):
    ride_in_specs, ride_ins, ride_out_specs, ride_out_shape, ride_scratch = _ride_args(ride)
    s = proj.shape[0]
    heads = lay["ret_heads"]
    nb = s // blk
    scale = RET_QK ** -0.5

    def body(lg_ref, qkv_ref, cos_ref, sin_ref, st_ref, dry_ref, o_ref, g_ref, _, dqkv_ref, dg_ref, dstate, mask):
        lg = lg_ref[0:1, 0:1]

        @pl.when(pl.program_id(1) == 0)
        def _():
            dstate[...] = jnp.zeros_like(dstate)
            mask[...] = _decay_mask(lg, blk)
            dg_ref[...] = jnp.zeros_like(dg_ref)

        oh = o_ref[...]
        dlt = oh - jnp.mean(oh, axis=-1, keepdims=True)
        rstd = lax.rsqrt(jnp.mean(dlt * dlt, axis=-1, keepdims=True) + EPS)
        oh = dlt * rstd
        gv = g_ref[...]
        rg = qkv_ref[:, 2 * RET_QK + RET_V:]
        sg = _sig(rg)
        dry = dry_ref[...]
        dt = dry * (rg * sg)
        dqkv_ref[:, 2 * RET_QK + RET_V:] = (dry * (oh * gv) * (sg * (1.0 + rg * (1.0 - sg)))).astype(BF16)
        dg_ref[...] += jnp.sum(dt * oh, axis=0, keepdims=True)
        doh = dt * gv
        do = rstd * (doh - jnp.mean(doh, axis=-1, keepdims=True) - oh * jnp.mean(doh * oh, axis=-1, keepdims=True))

        a, c, gb = _decays(lg, blk)
        cs, sn = cos_ref[...], sin_ref[...]
        q = _rope64(qkv_ref[:, :RET_QK], cs, sn)
        k = _rope64(qkv_ref[:, RET_QK:2 * RET_QK], cs, sn) * scale
        v = qkv_ref[:, 2 * RET_QK:2 * RET_QK + RET_V]
        st = st_ref[...]
        dst = dstate[...]
        mk = mask[...]
        sm = _dot(q, k, NT) * mk
        ds = _dot(do, v, NT) * mk
        dq = _dot(ds, k, NN) + _dot(do, st, NT) * a
        dk = _dot(ds, q, TN) + _dot(v, dst, NT) * c
        dqkv_ref[:, 2 * RET_QK:2 * RET_QK + RET_V] = (_dot(sm, do, TN) + _dot(k * c, dst, NN)).astype(BF16)
        dstate[...] = dst * gb + _dot(q * a, do, TN)
        dqkv_ref[:, :RET_QK] = _rope64(dq, cs, -sn).astype(BF16)
        dqkv_ref[:, RET_QK:2 * RET_QK] = (_rope64(dk, cs, -sn) * scale).astype(BF16)

    first = lay["off"]["heads"] // RET_HEAD
    last = nb - 1
    head_tile = pl.BlockSpec((blk, RET_V), lambda h, b: (last - b, h))
    res = pl.pallas_call(
        _with_ride(body, ride, (heads, nb), 0, 9, 2), name="ret_bwd", grid=(heads, nb),
        in_specs=[pl.BlockSpec((None, 8, LANES), lambda h, b: (h, 0, 0)),
                  pl.BlockSpec((blk, RET_HEAD), lambda h, b: (last - b, first + h)),
                  pl.BlockSpec((blk, LANES), lambda h, b: (last - b, 0)),
                  pl.BlockSpec((blk, LANES), lambda h, b: (last - b, 0)),
                  pl.BlockSpec((None, None, RET_QK, RET_V), lambda h, b: (h, last - b, 0, 0)),
                  head_tile, head_tile, pl.BlockSpec((1, RET_V), lambda h, b: (0, h)), ANY] + ride_in_specs,
        out_specs=[pl.BlockSpec((blk, RET_HEAD), lambda h, b: (last - b, first + h)),
                   pl.BlockSpec((1, RET_V), lambda h, b: (0, h))] + ride_out_specs,
        out_shape=[jax.ShapeDtypeStruct(d_proj.shape, d_proj.dtype),
                   jax.ShapeDtypeStruct((1, heads * RET_V), F32)] + ride_out_shape,
        scratch_shapes=[pltpu.VMEM((RET_QK, RET_V), F32), pltpu.VMEM((blk, blk), F32)] + ride_scratch,
        input_output_aliases={8: 0},
        compiler_params=_params(("arbitrary", "arbitrary") if ride else ("parallel", "arbitrary")),
    )(lgs, proj, cos, sin, states, d_ry, o, gain, d_proj, *ride_ins)
    return (res[0], res[1], res[2:]) if ride else res[:2]


def _mla_prep(proj, lay, gq, gkv, cos, sin_a, sin_b, tm):
    s = proj.shape[0]
    ql, kl = lay["q_lora"], lay["kv_lora"]

    def body(cq_ref, ckv_ref, kpe_ref, gq_ref, gkv_ref, cos_ref, sa_ref, sb_ref, cqn_ref, ckvn_ref, kpr_ref):
        for src, gref, dst in ((cq_ref, gq_ref, cqn_ref), (ckv_ref, gkv_ref, ckvn_ref)):
            v = src[...]
            r = lax.rsqrt(jnp.mean(v * v, axis=-1, keepdims=True) + EPS)
            dst[...] = (v * r * gref[...]).astype(BF16)
        kpr_ref[...] = _rope32(kpe_ref[...], cos_ref[...], sa_ref[...], sb_ref[...]).astype(BF16)

    off = lay["off"]
    return _rows("mla_prep", body, s, tm,
                 [(proj, ql, off["c_q"] // ql), (proj, kl, off["c_kv"] // kl), (proj, LANES, off["k_pe"] // LANES),
                  (gq,), (gkv,), (cos, LANES, 0), (sin_a, LANES, 0), (sin_b, LANES, 0)],
                 [(ql, BF16), (kl, BF16), (LANES, BF16)])


def _latent_norm_bwd(name, proj, offset, d_normed, g, d_proj, tm):
    s, w = d_normed.shape

    def body(dy_ref, x_ref, g_ref, dx_ref, dg_ref):
        _zero_first(dg_ref)
        v, dy = x_ref[...], dy_ref[...]
        r = lax.rsqrt(jnp.mean(v * v, axis=-1, keepdims=True) + EPS)
        xh = v * r
        dxh = dy * g_ref[...]
        dx_ref[...] = (r * (dxh - xh * jnp.mean(dxh * xh, axis=-1, keepdims=True))).astype(BF16)
        dg_ref[...] += jnp.sum(dy * xh, axis=0, keepdims=True)

    return _rows(name, body, s, tm, [(d_normed, w, 0), (proj, w, offset // w), (g,)], [], [(1, w)],
                 into=(d_proj, w, offset // w))


def _q_operand(r, cos, sin_a, sin_b):
    qs = (NOPE + ROPE) ** -0.5 * math.log2(math.e)
    cs, sa, sb = cos * qs, sin_a * qs, sin_b * qs
    parts = []
    for lo in range(0, r.shape[1], QPAD):
        parts += [r[:, lo:lo + NOPE] * qs, _rope32(r[:, lo + NOPE:lo + QPAD], cs, sa, sb)]
    return (jnp.concatenate(parts, axis=1),)


def _k_operand(r, kpr):
    parts = []
    for lo in range(0, r.shape[1], QPAD):
        parts += [r[:, lo:lo + NOPE], kpr.astype(F32)]
    return r, jnp.concatenate(parts, axis=1)


def _rope_key_grad(parts, lay, d_proj, tm):
    s, w = parts.shape

    def body(p_ref, dkpe_ref):
        dkpe_ref[:, :LANES] = sum(p_ref[:, lo:lo + LANES] for lo in range(0, w, LANES)).astype(BF16)
        dkpe_ref[:, LANES:] = jnp.zeros((tm, LANES), BF16)

    return _rows("rope_key_grad", body, s, tm, [(parts, w, 0)], [],
                 into=(d_proj, 2 * LANES, lay["off"]["k_pe"] // (2 * LANES)))[0]


def _diag_mask(t, keys_on_rows=False):
    row = lax.broadcasted_iota(jnp.int32, (t, t), 0)
    col = lax.broadcasted_iota(jnp.int32, (t, t), 1)
    key, query = (row, col) if keys_on_rows else (col, row)
    return jnp.right_shift(key, CHUNK_SHIFT) <= jnp.right_shift(query, CHUNK_SHIFT)


def _tile_pairs(nt, by_key):
    if by_key:
        pairs = [(i, j) for j in range(nt) for i in range(j, nt)]
    else:
        pairs = [(i, j) for i in range(nt) for j in range(i + 1)]
    return (jnp.asarray([p[0] for p in pairs], jnp.int32), jnp.asarray([p[1] for p in pairs], jnp.int32))


def _head_block(heads):
    return 4 if heads % 4 == 0 else 2 if heads % 2 == 0 else 1


def _attn_fwd(qf, kf, kv, lay, t, ride=None):
    s = qf.shape[0]
    heads = lay["mla_heads"]
    hb = _head_block(heads)
    nt = s // t
    qi, kj = _tile_pairs(nt, False)
    grid = (heads // hb, int(qi.shape[0]))
    ride_in_specs, ride_ins, ride_out_specs, ride_out_shape, ride_scratch = _ride_args(ride)

    def body(qi_ref, kj_ref, q_ref, k_ref, kv_ref, o_ref, lse_ref, m_s, l_s, acc):
        p = pl.program_id(1)
        i, j = qi_ref[p], kj_ref[p]

        @pl.when(j == 0)
        def _():
            m_s[...] = jnp.full_like(m_s, -jnp.inf)
            l_s[...] = jnp.zeros_like(l_s)
            acc[...] = jnp.zeros_like(acc)

        def step(diagonal):
            ones = jnp.ones((t, LANES), BF16)
            scores = [_dot(q_ref[:, hh * QPAD:(hh + 1) * QPAD], k_ref[:, hh * QPAD:(hh + 1) * QPAD], NT)
                      for hh in range(hb)]
            for hh in range(hb):
                sc = scores[hh]
                if diagonal:
                    sc = jnp.where(_diag_mask(t), sc, -jnp.inf)
                cols = [sc[:, c * LANES:(c + 1) * LANES] for c in range(t // LANES)]
                m_old = m_s[hh]
                m_new = jnp.maximum(m_old, jnp.max(functools.reduce(jnp.maximum, cols), axis=-1, keepdims=True))
                alpha = jnp.exp2(m_old - m_new)
                pr = jnp.concatenate([jnp.exp2(c - m_new).astype(BF16) for c in cols], axis=1)
                pv = _dot(pr, jnp.concatenate([kv_ref[:, hh * QPAD + NOPE:(hh + 1) * QPAD], ones], axis=1), NN)
                l_new = alpha * l_s[hh] + pv[:, VHEAD:]
                a_new = alpha * acc[hh] + pv[:, :VHEAD]
                if diagonal:
                    o_ref[:, hh * VHEAD:(hh + 1) * VHEAD] = a_new / l_new
                    lse_ref[hh] = jnp.transpose(m_new + jnp.log2(l_new))[:1]
                else:
                    m_s[hh], l_s[hh], acc[hh] = m_new, l_new, a_new

        pl.when(j < i)(functools.partial(step, False))
        pl.when(j == i)(functools.partial(step, True))

    res = pl.pallas_call(
        _with_ride(body, ride, grid, 2, 3, 2), name="attn_fwd",
        grid_spec=pltpu.PrefetchScalarGridSpec(
            num_scalar_prefetch=2, grid=grid,
            in_specs=[pl.BlockSpec((t, hb * QPAD), lambda h, p, qi, kj: (qi[p], h)),
                      pl.BlockSpec((t, hb * QPAD), lambda h, p, qi, kj: (kj[p], h)),
                      pl.BlockSpec((t, hb * QPAD), lambda h, p, qi, kj: (kj[p], h))] + ride_in_specs,
            out_specs=[pl.BlockSpec((t, hb * VHEAD), lambda h, p, qi, kj: (qi[p], h)),
                       pl.BlockSpec((hb, 1, t), lambda h, p, qi, kj: (h, 0, qi[p]))] + ride_out_specs,
            scratch_shapes=[pltpu.VMEM((hb, t, LANES), F32), pltpu.VMEM((hb, t, LANES), F32),
                            pltpu.VMEM((hb, t, VHEAD), F32)] + ride_scratch),
        out_shape=[jax.ShapeDtypeStruct((s, heads * VHEAD), F32),
                   jax.ShapeDtypeStruct((heads, 1, s), F32)] + ride_out_shape,
        compiler_params=_params(("arbitrary", "arbitrary") if ride else ("parallel", "arbitrary")),
    )(qi, kj, qf, kf, kv, *ride_ins)
    return (res[0], res[1], res[2:]) if ride else res


def _attn_delta(d_o, o, lay, tm):
    s = o.shape[0]
    heads = lay["mla_heads"]

    def body(do_ref, o_ref, dl_ref):
        for h in range(heads):
            sl = slice(h * VHEAD, (h + 1) * VHEAD)
            dl_ref[h] = jnp.sum(jnp.transpose(do_ref[:, sl] * o_ref[:, sl]), axis=0, keepdims=True)

    tile = pl.BlockSpec((tm, heads * VHEAD), lambda i: (i, 0))
    return pl.pallas_call(
        body, name="attn_delta", grid=(s // tm,), in_specs=[tile, tile],
        out_specs=pl.BlockSpec((heads, 1, tm), lambda i: (0, 0, i)),
        out_shape=jax.ShapeDtypeStruct((heads, 1, s), F32),
        compiler_params=_params(("parallel",)),
    )(d_o, o)


def _attn_bwd(qf, kf, kv, lse, delta, d_o, cos, sin_a, sin_b, lay, t, ride=None):
    s = qf.shape[0]
    heads = lay["mla_heads"]
    hb = _head_block(heads)
    nt = s // t
    scale = (NOPE + ROPE) ** -0.5
    qi, kj = _tile_pairs(nt, True)
    grid = (heads // hb, int(qi.shape[0]))
    ride_in_specs, ride_ins, ride_out_specs, ride_out_shape, ride_scratch = _ride_args(ride)

    def body(qi_ref, kj_ref, q_ref, k_ref, kv_ref, lse_ref, dl_ref, do_ref, cos_ref, sa_ref, sb_ref,
             dqp_ref, dkv_ref, dkpe_ref, dq_acc, dk_acc, dv_acc):
        p = pl.program_id(1)
        i, j = qi_ref[p], kj_ref[p]
        rows = pl.ds(pl.multiple_of(i * t, t), t)

        def unrope(v):
            return _rope32(v, cos_ref[...], -sa_ref[...], -sb_ref[...])

        @pl.when(p == 0)
        def _():
            dq_acc[...] = jnp.zeros_like(dq_acc)

        def step(diagonal):
            for hh in range(hb):
                lo = hh * QPAD
                q, k = q_ref[:, lo:lo + QPAD], k_ref[:, lo:lo + QPAD]
                do = do_ref[:, hh * VHEAD:(hh + 1) * VHEAD]
                pr = jnp.exp2(_dot(k, q, NT) - lse_ref[hh])
                if diagonal:
                    pr = jnp.where(_diag_mask(t, keys_on_rows=True), pr, 0.0)
                dv_part = _dot(pr, do, NN)
                ds = (pr * (_dot(kv_ref[:, lo + NOPE:lo + QPAD], do, NT) - dl_ref[hh])).astype(BF16)
                dk_part = _dot(ds, q, NN)
                dq = dq_acc[rows, lo:lo + QPAD] + _dot(ds, k, TN) * scale
                if diagonal:
                    dqp_ref[:, lo:lo + NOPE] = dq[:, :NOPE].astype(BF16)
                    dqp_ref[:, lo + NOPE:lo + QPAD] = unrope(dq[:, NOPE:]).astype(BF16)
                    dk_acc[hh], dv_acc[hh] = dk_part, dv_part
                else:
                    dq_acc[rows, lo:lo + QPAD] = dq
                    dk_acc[hh] += dk_part
                    dv_acc[hh] += dv_part

        pl.when(i > j)(functools.partial(step, False))
        pl.when(i == j)(functools.partial(step, True))

        @pl.when(i == nt - 1)
        def _():
            kpe = jnp.zeros((t, LANES), F32)
            for hh in range(hb):
                lo = hh * QPAD
                dk = dk_acc[hh] * math.log(2.0)
                dkv_ref[:, lo:lo + NOPE] = dk[:, :NOPE].astype(BF16)
                dkv_ref[:, lo + NOPE:lo + QPAD] = dv_acc[hh].astype(BF16)
                kpe = kpe + dk[:, NOPE:]
            dkpe_ref[...] = unrope(kpe)

    table = pl.BlockSpec((t, LANES), lambda h, p, qi, kj: (kj[p], 0))
    res = pl.pallas_call(
        _with_ride(body, ride, grid, 2, 9, 3), name="attn_bwd",
        grid_spec=pltpu.PrefetchScalarGridSpec(
            num_scalar_prefetch=2, grid=grid,
            in_specs=[pl.BlockSpec((t, hb * QPAD), lambda h, p, qi, kj: (qi[p], h)),
                      pl.BlockSpec((t, hb * QPAD), lambda h, p, qi, kj: (kj[p], h)),
                      pl.BlockSpec((t, hb * QPAD), lambda h, p, qi, kj: (kj[p], h)),
                      pl.BlockSpec((hb, 1, t), lambda h, p, qi, kj: (h, 0, qi[p])),
                      pl.BlockSpec((hb, 1, t), lambda h, p, qi, kj: (h, 0, qi[p])),
                      pl.BlockSpec((t, hb * VHEAD), lambda h, p, qi, kj: (qi[p], h)),
                      table, table, table] + ride_in_specs,
            out_specs=[pl.BlockSpec((t, hb * QPAD), lambda h, p, qi, kj: (kj[p], h)),
                       pl.BlockSpec((t, hb * QPAD), lambda h, p, qi, kj: (kj[p], h)),
                       pl.BlockSpec((t, LANES), lambda h, p, qi, kj: (kj[p], h))] + ride_out_specs,
            scratch_shapes=[pltpu.VMEM((s, hb * QPAD), F32), pltpu.VMEM((hb, t, QPAD), F32),
                            pltpu.VMEM((hb, t, VHEAD), F32)] + ride_scratch),
        out_shape=[jax.ShapeDtypeStruct((s, heads * QPAD), BF16),
                   jax.ShapeDtypeStruct((s, heads * QPAD), BF16),
                   jax.ShapeDtypeStruct((s, heads // hb * LANES), F32)] + ride_out_shape,
        compiler_params=_params(("arbitrary", "arbitrary") if ride else ("parallel", "arbitrary")),
    )(qi, kj, qf, kf, kv, lse, delta, d_o, cos, sin_a, sin_b, *ride_ins)
    return (res[0], res[1], res[2], res[3:]) if ride else res


ANY = pl.BlockSpec(memory_space=pl.ANY)


def _place():
    return lax.axis_index("x"), lax.axis_index("y"), lax.axis_index("c")


def _other_chips(x, y):
    return [(1 - x, y), (x, 1 - y), (1 - x, 1 - y)]


class _Exchange:
    def __init__(self, ins, out_shape, scratch, phases):
        self.ins, self.out_shape, self.scratch, self.phases = list(ins), list(out_shape), list(scratch), phases

    def counts(self):
        return len(self.ins), len(self.out_shape), len(self.scratch)

    def run(self, r_in, r_out, r_scratch, conds):
        for cond, phase in zip(conds, self.phases):
            if phase is not None and cond is not None:
                pl.when(cond)(functools.partial(phase, r_in, r_out, r_scratch))


def _steps(ids, sizes):
    lin, total = 0, 1
    for i, n in zip(ids, sizes):
        lin, total = lin * n + i, total * n
    return lin == 0, lin == total // 2, lin == total - 1


def _ride_args(ride):
    if ride is None:
        return [], [], [], [], []
    n_in, n_out, _ = ride.counts()
    return [ANY] * n_in, ride.ins, [ANY] * n_out, ride.out_shape, ride.scratch


def _with_ride(body, ride, grid, n_prefetch, n_in, n_out):
    if ride is None:
        return body
    r_in, r_out, r_sc = ride.counts()

    def hosted(*refs):
        cuts = (n_prefetch, n_in, r_in, n_out, r_out)
        parts, pos = [], 0
        for n in cuts:
            parts.append(refs[pos:pos + n])
            pos += n
        pre, ins, ride_in, outs, ride_out = parts
        scratch, ride_scratch = refs[pos:len(refs) - r_sc], refs[len(refs) - r_sc:]
        first, mid, last = _steps([pl.program_id(d) for d in range(len(grid))], grid)
        ride.run(ride_in, ride_out, ride_scratch, (first, mid, None))
        body(*pre, *ins, *outs, *scratch)
        ride.run(ride_in, ride_out, ride_scratch, (None, None, last))

    return hosted


def _exchange_alone(name, ex):
    n_in, n_out, _ = ex.counts()

    def body(*refs):
        for phase in ex.phases:
            if phase is not None:
                phase(refs[:n_in], refs[n_in:n_in + n_out], refs[n_in + n_out:])

    return pl.pallas_call(
        body, name=name, in_specs=[ANY] * n_in, out_specs=[ANY] * n_out, out_shape=ex.out_shape,
        scratch_shapes=ex.scratch)(*ex.ins)


def _gather_exchange(shards):
    nw = len(shards)

    def parts(ins, outs, sems):
        send_sems, recv_sems, local_sems = sems
        x, y, c = _place()

        def slot(px, py, pc):
            return 4 * px + 2 * py + pc

        def copy(w, k, rows, to, src=None):
            return pltpu.make_async_remote_copy(
                src_ref=rows if src is None else src, dst_ref=rows, send_sem=send_sems.at[w, k],
                recv_sem=recv_sems.at[w, k], device_id=to, device_id_type=MESH)

        def halves(w):
            rows = shards[w].shape[0]
            cut = -(-(rows // 2) // 16) * 16
            return pl.ds(0, cut), pl.ds(cut, rows - cut)

        def plan(w, mine):
            side = c if mine else 1 - c
            upper, lower = halves(w)
            whole = lambda px, py: outs[w].at[slot(px, py, side)]
            top = lambda px, py: outs[w].at[slot(px, py, side), upper]
            bottom = lambda px, py: outs[w].at[slot(px, py, side), lower]
            xn, yn, sib = (1 - x, y, side), (x, 1 - y, side), (x, y, 1 - side)
            own = ins[w] if mine else None
            return [copy(w, 0, whole(x, y), sib, own), copy(w, 1, whole(x, y), xn, own),
                    copy(w, 2, whole(x, y), yn, own), copy(w, 3, top(1 - x, y), yn), copy(w, 4, bottom(x, 1 - y), xn),
                    copy(w, 5, whole(1 - x, y), sib), copy(w, 6, whole(x, 1 - y), sib),
                    copy(w, 7, top(1 - x, 1 - y), sib), copy(w, 8, bottom(1 - x, 1 - y), sib)]

        def arrivals(w):
            upper, lower = halves(w)
            at = lambda px, py, *rows: outs[w].at[(slot(px, py, c),) + rows]
            return {1: copy(w, 1, at(1 - x, y), (x, y, c)), 2: copy(w, 2, at(x, 1 - y), (x, y, c)),
                    3: copy(w, 3, at(1 - x, 1 - y, upper), (x, y, c)),
                    4: copy(w, 4, at(1 - x, 1 - y, lower), (x, y, c))}

        local = [pltpu.make_async_copy(ins[w], outs[w].at[slot(x, y, c)], local_sems.at[w]) for w in range(nw)]
        return plan, arrivals, local

    def start(ins, outs, sems):
        plan, _, local = parts(ins, outs, sems)
        for cp in local:
            cp.start()
        for w in range(nw):
            for k in (0, 1, 2):
                plan(w, True)[k].start()

    def middle(ins, outs, sems):
        plan, arrivals, _ = parts(ins, outs, sems)
        for landed, onward in ((1, (3, 5)), (2, (4, 6))):
            for w in range(nw):
                arrivals(w)[landed].wait_recv()
                for k in onward:
                    plan(w, True)[k].start()

    def finish(ins, outs, sems):
        plan, arrivals, local = parts(ins, outs, sems)
        for landed, onward in ((3, 7), (4, 8)):
            for w in range(nw):
                arrivals(w)[landed].wait_recv()
                plan(w, True)[onward].start()
        for w in range(nw):
            from_sibling = plan(w, False)
            for k in (0, 5, 6, 7, 8):
                from_sibling[k].wait_recv()
            for cp in plan(w, True):
                cp.wait_send()
        for cp in local:
            cp.wait()

    return _Exchange(
        shards, [jax.ShapeDtypeStruct((N_DEV,) + s.shape, s.dtype) for s in shards],
        [pltpu.SemaphoreType.DMA((nw, 9)), pltpu.SemaphoreType.DMA((nw, 9)), pltpu.SemaphoreType.DMA((nw,))],
        (start, middle, finish))


def _sibling_exchange(grads):
    nw = len(grads)

    def copies(ins, outs, sems):
        x, y, c = _place()
        return [pltpu.make_async_remote_copy(
            src_ref=ins[w].at[2 * p + (1 - c)], dst_ref=outs[w].at[p], send_sem=sems[0].at[w, p],
            recv_sem=sems[1].at[w, p], device_id=(x, y, 1 - c), device_id_type=MESH)
            for w in range(nw) for p in range(4)]

    def start(ins, outs, sems):
        for cp in copies(ins, outs, sems):
            cp.start()

    def finish(ins, outs, sems):
        for cp in copies(ins, outs, sems):
            cp.wait()

    return _Exchange(grads, [jax.ShapeDtypeStruct((4,) + g.shape[1:], g.dtype) for g in grads],
                     [pltpu.SemaphoreType.DMA((nw, 4)), pltpu.SemaphoreType.DMA((nw, 4))], (start, None, finish))


def _chips_exchange(sums):
    nw = len(sums)

    def copies(ins, outs, sems):
        x, y, c = _place()
        return [pltpu.make_async_remote_copy(
            src_ref=ins[w].at[2 * px + py], dst_ref=outs[w].at[k], send_sem=sems[0].at[w, k],
            recv_sem=sems[1].at[w, k], device_id=(px, py, c), device_id_type=MESH)
            for w in range(nw) for k, (px, py) in enumerate(_other_chips(x, y))]

    def start(ins, outs, sems):
        for cp in copies(ins, outs, sems):
            cp.start()

    def finish(ins, outs, sems):
        for cp in copies(ins, outs, sems):
            cp.wait()

    return _Exchange(sums, [jax.ShapeDtypeStruct((3,) + g.shape[1:], g.dtype) for g in sums],
                     [pltpu.SemaphoreType.DMA((nw, 3)), pltpu.SemaphoreType.DMA((nw, 3))], (start, None, finish))


def _pair_sum(name, g, got, c_arr, tr):
    _, rows, cols = g.shape
    tr = _tile_rows(rows, tr)

    def body(c_ref, a_ref, b_ref, o_ref):
        o_ref[...] = (a_ref[...].astype(F32) + b_ref[...].astype(F32)).astype(BF16)

    return pl.pallas_call(
        body, name=name,
        grid_spec=pltpu.PrefetchScalarGridSpec(
            num_scalar_prefetch=1, grid=(4, rows // tr),
            in_specs=[pl.BlockSpec((None, tr, cols), lambda p, r, cr: (2 * p + cr[0], r, 0)),
                      pl.BlockSpec((None, tr, cols), lambda p, r, cr: (p, r, 0))],
            out_specs=pl.BlockSpec((None, tr, cols), lambda p, r, cr: (p, r, 0))),
        out_shape=jax.ShapeDtypeStruct((4, rows, cols), BF16),
        compiler_params=_params(("parallel", "parallel")),
    )(c_arr, g, got)


def _tile_rows(rows, pref):
    t = min(rows, pref)
    while rows % t or t % 8:
        t -= 1
    return t


def _adam(w, g, m, v):
    m = ADAM_B1 * m + (1.0 - ADAM_B1) * g
    v = ADAM_B2 * v + (1.0 - ADAM_B2) * (g * g)
    m_hat = m / (1.0 - ADAM_B1 ** ADAM_STEP)
    v_hat = v / (1.0 - ADAM_B2 ** ADAM_STEP)
    return -ADAM_LR * (m_hat / (jnp.sqrt(v_hat) + ADAM_EPS) + ADAM_WD * w), m, v


def _adamw_shard(name, w, m, v, sums, got, chip_arr, tr, tc=None):
    _, rows, cols = w.shape
    tr = _tile_rows(rows, tr)
    tc = cols if tc is None else _tile(cols, tc)

    def body(p_ref, w_ref, m_ref, v_ref, s_ref, r_ref, g_ref, d_ref, nm_ref, nv_ref):
        g = s_ref[...].astype(F32)
        for k in range(3):
            g = g + r_ref[k].astype(F32)
        g_ref[...] = g
        d_ref[...], nm_ref[...], nv_ref[...] = _adam(w_ref[...], g, m_ref[...], v_ref[...])

    tile = pl.BlockSpec((None, tr, tc), lambda r, q, pr: (0, r, q))
    return pl.pallas_call(
        body, name=name,
        grid_spec=pltpu.PrefetchScalarGridSpec(
            num_scalar_prefetch=1, grid=(rows // tr, cols // tc),
            in_specs=[tile, tile, tile,
                      pl.BlockSpec((None, tr, tc), lambda r, q, pr: (pr[0], r, q)),
                      pl.BlockSpec((3, tr, tc), lambda r, q, pr: (0, r, q))],
            out_specs=[tile] * 4),
        out_shape=[jax.ShapeDtypeStruct((1, rows, cols), F32)] * 4,
        compiler_params=_params(("parallel", "parallel")),
    )(chip_arr, w, m, v, sums, got)


def _gains_all_reduce_adam(grads, loss_part, ws, ms, vs):
    n = len(grads)
    widths = [g.shape[1] for g in grads]
    rows, width = -(-(n + 1) // 8) * 8, max(widths)

    def body(*refs):
        g_in, loss_in = refs[:n], refs[n]
        w_in, m_in, v_in = (refs[1 + k * n:1 + (k + 1) * n] for k in (1, 2, 3))
        outs = refs[1 + 4 * n:2 + 8 * n]
        g_out, d_out, m_out, v_out = (outs[k * n:(k + 1) * n] for k in range(4))
        loss_out = outs[4 * n]
        buf, send_sems, recv_sems = refs[2 + 8 * n:]
        x, y, c = _place()
        me = 4 * x + 2 * y + c
        buf[me] = jnp.zeros((rows, width), F32)
        for r in range(n):
            buf[me, r:r + 1, :widths[r]] = g_in[r][...]
        buf[me, n:n + 1, :LANES] = loss_in[...]
        peers = [(x, y, 1 - c)] + [(px, py, pc) for px, py in _other_chips(x, y) for pc in (c, 1 - c)]
        copies = []
        for k, peer in enumerate(peers):
            cp = pltpu.make_async_remote_copy(
                src_ref=buf.at[me], dst_ref=buf.at[me], send_sem=send_sems.at[k], recv_sem=recv_sems.at[k],
                device_id=peer, device_id_type=MESH)
            cp.start()
            copies.append(cp)
        for cp in copies:
            cp.wait()
        total = buf[0]
        for k in range(1, N_DEV):
            total = total + buf[k]
        for r in range(n):
            g = total[r:r + 1, :widths[r]]
            g_out[r][...] = g
            d_out[r][...], m_out[r][...], v_out[r][...] = _adam(w_in[r][...], g, m_in[r][...], v_in[r][...])
        loss_out[...] = total[n:n + 1, :LANES]

    vm = pl.BlockSpec(memory_space=pltpu.VMEM)
    shapes = [jax.ShapeDtypeStruct((1, w), F32) for w in widths]
    res = pl.pallas_call(
        body, name="gains_all_reduce_adamw",
        in_specs=[vm] * (4 * n + 1), out_specs=[vm] * (4 * n + 1),
        out_shape=shapes * 4 + [jax.ShapeDtypeStruct((1, LANES), F32)],
        scratch_shapes=[pltpu.VMEM((N_DEV, rows, width), F32), pltpu.SemaphoreType.DMA((7,)),
                        pltpu.SemaphoreType.DMA((7,))],
    )(*grads, loss_part, *ws, *ms, *vs)
    return res[:n], res[n:2 * n], res[2 * n:3 * n], res[3 * n:4 * n], res[4 * n]


IN_ORDER = ("r_q", "r_k", "r_v", "r_g", "c_q", "c_kv", "k_pe", "g_ret", "g_mla")
RET_HEAD = 2 * RET_QK + 2 * RET_V


def _make_layout(d, vw, qw, ql, kl, mla_w):
    width = {"r_q": qw, "r_k": qw, "r_v": vw, "r_g": vw, "c_q": ql, "c_kv": kl, "k_pe": ROPE, "g_ret": d, "g_mla": d}
    src, o = {}, 0
    for n in IN_ORDER:
        src[n] = o
        o += width[n]
    heads = vw // RET_V
    off, pieces, o = {}, [], 0

    def put(name, w, s):
        nonlocal o
        off.setdefault(name, o)
        pieces.append((o, w, s))
        o += w

    for n in ("g_ret", "g_mla", "c_q"):
        put(n, width[n], src[n])
    for h in range(heads):
        put("heads", RET_QK, src["r_q"] + h * RET_QK)
        put("heads", RET_QK, src["r_k"] + h * RET_QK)
        put("heads", RET_V, src["r_v"] + h * RET_V)
        put("heads", RET_V, src["r_g"] + h * RET_V)
    for n in ("c_kv", "k_pe"):
        put(n, width[n], src[n])
    total = off["k_pe"] + 2 * LANES
    for n, blk in (("g_ret", d), ("g_mla", d), ("c_q", ql), ("heads", RET_HEAD), ("c_kv", kl), ("k_pe", 2 * LANES)):
        assert off[n] % blk == 0
    return {"off": off, "pieces": pieces, "total": total, "n_in": sum(width.values()),
            "ret_heads": heads, "mla_heads": mla_w // VHEAD, "q_lora": ql, "kv_lora": kl}


def _cols_to_full(g):
    n, r, c = g.shape
    return jnp.transpose(g, (1, 0, 2)).reshape(r, n * c)


def _full_to_cols(w):
    r, c = w.shape
    return jnp.transpose(w.reshape(r, N_DEV, c // N_DEV), (1, 0, 2))


def _w_in_to_mine(g, lay):
    _, cols, d = g.shape
    parts, at = [], 0
    for o, w, s in lay["pieces"]:
        if o > at:
            parts.append(jnp.zeros((o - at, d), g.dtype))
        while w > 0:
            k, a = divmod(s, cols)
            take = min(w, cols - a)
            parts.append(g[k, a:a + take])
            s, w, o = s + take, w - take, o + take
        at = o
    parts.append(jnp.zeros((lay["total"] - at, d), g.dtype))
    return jnp.concatenate(parts, axis=0)


def _mine_to_blocks(g, lay):
    cols = lay["n_in"] // N_DEV
    by_src = sorted(lay["pieces"], key=lambda p: p[2])
    blocks = []
    for k in range(N_DEV):
        lo, hi, parts = k * cols, (k + 1) * cols, []
        for o, w, s in by_src:
            a, b = max(lo, s), min(hi, s + w)
            if a < b:
                parts.append(g[o + a - s:o + b - s])
        blocks.append(jnp.concatenate(parts, axis=0))
    return jnp.stack(blocks)


def _rope_tables(positions, half):
    inv = ROPE_THETA ** (-jnp.arange(half, dtype=F32) / half)
    ang = positions.astype(F32)[:, None] * inv
    return jnp.cos(ang), jnp.sin(ang)


def kernel(x, positions, norm_mix_g, w_in, ret_norm_g, w_ret_o, q_a_norm_g, w_q_b, kv_a_norm_g, w_kv_b, w_mla_o, w_out, norm_mlp_g, w_up, w_down, norm_f_g, loss_target, m_norm_mix_g, m_w_in, m_ret_norm_g, m_w_ret_o, m_q_a_norm_g, m_w_q_b, m_kv_a_norm_g, m_w_kv_b, m_w_mla_o, m_w_out, m_norm_mlp_g, m_w_up, m_w_down, m_norm_f_g, v_norm_mix_g, v_w_in, v_ret_norm_g, v_w_ret_o, v_q_a_norm_g, v_w_q_b, v_kv_a_norm_g, v_w_kv_b, v_w_mla_o, v_w_out, v_norm_mlp_g, v_w_up, v_w_down, v_norm_f_g):
    xs, tgt, pos = x[0], loss_target[0], positions[0]
    s, d = xs.shape
    mats = {"w_in": w_in[0], "w_ret_o": w_ret_o[0], "w_q_b": w_q_b[0], "w_kv_b": w_kv_b[0], "w_mla_o": w_mla_o[0],
            "w_out": w_out[0], "w_up": w_up[0], "w_down": w_down[0]}
    mat_w = {"w_in": w_in, "w_ret_o": w_ret_o, "w_q_b": w_q_b, "w_kv_b": w_kv_b, "w_mla_o": w_mla_o, "w_out": w_out,
             "w_up": w_up, "w_down": w_down}
    mat_m = {"w_in": m_w_in, "w_ret_o": m_w_ret_o, "w_q_b": m_w_q_b, "w_kv_b": m_w_kv_b, "w_mla_o": m_w_mla_o,
             "w_out": m_w_out, "w_up": m_w_up, "w_down": m_w_down}
    mat_v = {"w_in": v_w_in, "w_ret_o": v_w_ret_o, "w_q_b": v_w_q_b, "w_kv_b": v_w_kv_b, "w_mla_o": v_w_mla_o,
             "w_out": v_w_out, "w_up": v_w_up, "w_down": v_w_down}
    names = list(mats)
    col_sharded = ("w_in", "w_q_b", "w_kv_b", "w_up")
    vw = ret_norm_g.shape[1]
    mla_w = mats["w_mla_o"].shape[0] * N_DEV
    ql, kl = q_a_norm_g.shape[1], kv_a_norm_g.shape[1]
    n_in = mats["w_in"].shape[1] * N_DEV
    qw = (n_in - 2 * vw - ql - kl - ROPE - 2 * d) // 2
    lay = _make_layout(d, vw, qw, ql, kl, mla_w)
    assert lay["n_in"] == n_in
    heads_r, heads_m = lay["ret_heads"], lay["mla_heads"]

    shard16 = {n: mats[n].astype(BF16) for n in names}
    shard16["w_in"] = jnp.swapaxes(w_in, 1, 2)[0].astype(BF16)
    with_in_proj = ("w_ret_o", "w_q_b", "w_kv_b", "w_mla_o", "w_out")
    mlp = ("w_up", "w_down")
    by_device = ("w_up", "w_kv_b")
    full = {}

    def keep(group, gathered):
        for n, g in zip(group, gathered):
            if n not in by_device:
                g = _cols_to_full(g) if n in col_sharded else g.reshape(-1, g.shape[2])
            full[n] = g

    w_mine = _w_in_to_mine(_exchange_alone("gather_w_in", _gather_exchange([shard16["w_in"]]))[0], lay)

    c64, s64 = _rope_tables(pos, RET_QK // 2)
    cos_r = jnp.concatenate([c64, c64], axis=1)
    sin_r = jnp.concatenate([-s64, s64], axis=1)
    c32, s32 = _rope_tables(pos, ROPE // 2)
    z32, z64 = jnp.zeros_like(c32), jnp.zeros((s, LANES - ROPE), F32)
    cos_p = jnp.concatenate([c32, c32, z64], axis=1)
    sin_a = jnp.concatenate([-s32, z32, z64], axis=1)
    sin_b = jnp.concatenate([z32, s32, z64], axis=1)
    lg = jnp.log(1.0 - 2.0 ** (-5.0 - jnp.arange(heads_r, dtype=F32)))
    lgs = jnp.broadcast_to(lg[:, None, None], (heads_r, 8, LANES))

    tm = min(256, s)
    blk = min(512, s)
    t_att = min(512, s)

    u = _rms_fwd("norm_mix", xs, norm_mix_g, tm)
    proj, gathered = _mm("in_proj", u, w_mine, "nt", F32, tn=1280,
                         ride=_gather_exchange([shard16[n] for n in with_in_proj]))
    keep(with_in_proj, gathered)
    wq_pad = jnp.pad(full["w_q_b"].reshape(ql, heads_m, NOPE + ROPE),
                     ((0, 0), (0, 0), (0, QPAD - NOPE - ROPE))).reshape(ql, heads_m * QPAD)
    o_ret, states, ry = _ret_fwd(proj, lay, cos_r, sin_r, lgs, ret_norm_g, blk)
    y_ret = _mm("ret_out", ry, full["w_ret_o"], "nn", F32)
    cqn, ckvn, kpr = _mla_prep(proj, lay, q_a_norm_g, kv_a_norm_g, cos_p, sin_a, sin_b, tm)
    qf = _mm("q_up", cqn, wq_pad, "nn", BF16, extras=((cos_p, None), (sin_a, None), (sin_b, None)), epilogue=_q_operand)
    kv, kf = _mm("kv_up", ckvn, full["w_kv_b"], "nn", (BF16, BF16), b_by_device=True, extras=((kpr, None),),
                 epilogue=_k_operand)
    o_mla, lse, gathered = _attn_fwd(qf, kf, kv, lay, t_att, ride=_gather_exchange([shard16["w_up"]]))
    keep(("w_up",), gathered)
    gate_tile = _tile(d, 1024)
    y_mla, merged = _mm(
        "mla_out", o_mla, full["w_mla_o"], "nn", (F32, BF16), tm=512, tn=gate_tile,
        extras=((proj, lay["off"]["g_ret"] // gate_tile), (proj, lay["off"]["g_mla"] // gate_tile), y_ret),
        epilogue=lambda r, gr, gm, yr: (r, _sig(gr) * yr + _sig(gm) * r))
    h1, n2 = _mm("out_proj", merged, full["w_out"], "nn", (F32, BF16), tm=512, tn=d,
                 extras=(xs, (norm_mlp_g, "whole")), epilogue=_residual_norm)
    (z, act), gathered = _mm("mlp_up", n2, full["w_up"], "nn", (F32, BF16), b_by_device=True,
                             epilogue=lambda r: (r, jnp.square(jnp.maximum(r, 0.0))),
                             ride=_gather_exchange([shard16["w_down"]]))
    keep(("w_down",), gathered)
    dn = _mm("mlp_down", act, full["w_down"], "nn", F32, tk=4096)
    dh2, dh2_16, g_norm_f, loss_part = _final("loss_head", h1, dn, norm_f_g.reshape(1, d), tgt, tm)

    mx, my, mc = _place()
    c_arr = jnp.reshape(mc, (1,)).astype(jnp.int32)
    chip_arr = jnp.reshape(2 * mx + my, (1,)).astype(jnp.int32)
    sums, from_chips = {}, {}

    def blocks(group, grads):
        return [g if n in by_device else (_full_to_cols(g) if n in col_sharded else g.reshape((N_DEV,) + mats[n].shape))
                for n, g in zip(group, grads)]

    def pair_sums(group, mine, from_sibling):
        for n, g, r in zip(group, mine, from_sibling):
            sums[n] = _pair_sum("pair_sum_" + n, g, r, c_arr, 512)
        return [sums[n] for n in group]

    dz = _mm("mlp_down_dx", dh2_16, full["w_down"], "nt", BF16, extras=(z,),
             epilogue=lambda r, zz: (r * (2.0 * jnp.maximum(zz, 0.0)),))
    g_w_down = _mm("mlp_down_dw", act, dh2_16, "tn", BF16, tm=512, tn=d, tk=s)
    down_blocks = blocks(("w_down",), (g_w_down,))
    g_w_up, got_down = _mm("mlp_up_dw", n2, dz, "tn", BF16, tk=s, out_by_device=True,
                           ride=_sibling_exchange(down_blocks))
    dn2, got_up = _mm("mlp_up_dx", dz, full["w_up"], "nt", F32, tk=4096, b_by_device=True,
                      ride=_sibling_exchange([g_w_up]))
    mlp_sums = pair_sums(mlp, [g_w_up] + down_blocks, list(got_up) + list(got_down))
    dh1, dh1_16, g_norm_mlp = _rms_bwd("norm_mlp_bwd", dn2, h1, norm_mlp_g, dh2, tm, matmul_copy=True)
    assert lay["off"]["g_ret"] == 0 and lay["off"]["g_mla"] == d
    dy_ret, dy_mla, d_proj = _mm(
        "out_proj_dx", dh1_16, full["w_out"], "nt", (BF16, BF16, (BF16, 2, lay["total"])), tm=256, tn=d,
        extras=((proj, 0), (proj, 1), y_ret, y_mla), epilogue=_gate_grads)
    g_w_out = _mm("out_proj_dw", merged, dh1_16, "tn", BF16, tm=512, tn=d, tk=s)
    g_w_ret_o = _mm("ret_out_dw", ry, dy_ret, "tn", BF16, tm=512, tn=d, tk=s)
    g_w_mla_o = _mm("mla_out_dw", o_mla, dy_mla, "tn", BF16, tm=256, tn=d, tk=s)
    mixer = ("w_out", "w_ret_o", "w_mla_o")
    mixer_blocks = blocks(mixer, (g_w_out, g_w_ret_o, g_w_mla_o))
    d_ry, got = _mm("ret_out_dx", dy_ret, full["w_ret_o"], "nt", F32, ride=_sibling_exchange(mixer_blocks))
    mixer_sums = pair_sums(mixer, mixer_blocks, got)
    d_omla = _mm("mla_out_dx", dy_mla, full["w_mla_o"], "nt", F32)
    d_proj, g_ret_norm, got = _ret_bwd(proj, lay, cos_r, sin_r, lgs, states, d_ry, o_ret, ret_norm_g, d_proj, blk,
                                       ride=_chips_exchange(mixer_sums))
    from_chips.update(zip(mixer, got))
    delta = _attn_delta(d_omla, o_mla, lay, t_att)
    dqp, dkv, dkpe_parts, got = _attn_bwd(qf, kf, kv, lse, delta, d_omla, cos_p, sin_a, sin_b, lay, t_att,
                                          ride=_chips_exchange(mlp_sums))
    from_chips.update(zip(mlp, got))
    d_proj = _rope_key_grad(dkpe_parts, lay, d_proj, tm)
    d_cqn = _mm("q_up_dx", dqp, wq_pad, "nt", F32)
    g_wq_pad = _mm("q_up_dw", cqn, dqp, "tn", BF16)
    d_ckvn = _mm("kv_up_dx", dkv, full["w_kv_b"], "nt", F32, b_by_device=True)
    g_w_kv_b = _mm("kv_up_dw", ckvn, dkv, "tn", BF16, out_by_device=True)
    d_proj, g_q_a = _latent_norm_bwd("q_latent_bwd", proj, lay["off"]["c_q"], d_cqn, q_a_norm_g, d_proj, tm)
    d_proj, g_kv_a = _latent_norm_bwd("kv_latent_bwd", proj, lay["off"]["c_kv"], d_ckvn, kv_a_norm_g, d_proj, tm)
    g_w_q_b = g_wq_pad.reshape(ql, heads_m, QPAD)[:, :, :NOPE + ROPE].reshape(ql, heads_m * (NOPE + ROPE))
    latent = ("w_q_b", "w_kv_b")
    latent_blocks = blocks(latent, (g_w_q_b, g_w_kv_b))
    latent_sums = pair_sums(latent, latent_blocks,
                            _exchange_alone("latent_grads_to_sibling", _sibling_exchange(latent_blocks)))
    g_w_mine, got = _mm("in_proj_dw", d_proj, u, "tn", BF16, tm=768, tk=s, ride=_chips_exchange(latent_sums))
    from_chips.update(zip(latent, got))
    last = ("w_in",)
    last_blocks = [_mine_to_blocks(g_w_mine, lay)]
    last_sums = pair_sums(last, last_blocks, _exchange_alone("grads_to_sibling", _sibling_exchange(last_blocks)))
    du, got = _mm("in_proj_dx", d_proj, w_mine, "nn", F32, ride=_chips_exchange(last_sums))
    from_chips.update(zip(last, got))
    grad_x, g_norm_mix = _rms_bwd("norm_mix_bwd", du, xs, norm_mix_g, dh1, tm)

    upd = {n: _adamw_shard("adamw_" + n, mat_w[n], mat_m[n], mat_v[n], sums[n], from_chips[n], chip_arr, 256)
           for n in names if n != "w_in"}
    flip = lambda a: jnp.swapaxes(a, 1, 2)
    upd["w_in"] = [flip(o) for o in _adamw_shard(
        "adamw_w_in", flip(w_in), flip(m_w_in), flip(v_w_in), sums["w_in"], from_chips["w_in"], chip_arr, 512, tc=512)]

    gains = [("norm_mix_g", norm_mix_g, m_norm_mix_g, v_norm_mix_g, g_norm_mix),
             ("ret_norm_g", ret_norm_g, m_ret_norm_g, v_ret_norm_g, g_ret_norm),
             ("q_a_norm_g", q_a_norm_g, m_q_a_norm_g, v_q_a_norm_g, g_q_a),
             ("kv_a_norm_g", kv_a_norm_g, m_kv_a_norm_g, v_kv_a_norm_g, g_kv_a),
             ("norm_mlp_g", norm_mlp_g, m_norm_mlp_g, v_norm_mlp_g, g_norm_mlp),
             ("norm_f_g", norm_f_g, m_norm_f_g, v_norm_f_g, g_norm_f)]
    as_row = lambda a: a.reshape(1, -1)
    g_sm, d_sm, m_sm, v_sm, loss_row = _gains_all_reduce_adam(
        [g[4] for g in gains], loss_part, *[[as_row(g[k]) for g in gains] for k in (1, 2, 3)])
    loss = loss_row[0, 0]
    small = {g[0]: [a[r].reshape(g[1].shape) for a in (g_sm, d_sm, m_sm, v_sm)] for r, g in enumerate(gains)}

    order = ["norm_mix_g", "w_in", "ret_norm_g", "w_ret_o", "q_a_norm_g", "w_q_b", "kv_a_norm_g", "w_kv_b", "w_mla_o",
             "w_out", "norm_mlp_g", "w_up", "w_down", "norm_f_g"]
    outs = [loss, grad_x[None]]
    for k in range(4):
        for n in order:
            outs.append(small[n][k] if n in small else upd[n][k])
    return tuple(outs)
```

```python
import functools
import math

import jax
import jax.numpy as jnp
from jax import lax
from jax.experimental import pallas as pl
from jax.experimental.pallas import tpu as pltpu

F32 = jnp.float32
BF16 = jnp.bfloat16
MESH = pl.DeviceIdType.MESH

EPS = 1e-6
ROPE_THETA = 10000.0
CHUNK_SHIFT = 6
RET_QK = 128
RET_V = 256
NOPE = 128
ROPE = 64
VHEAD = 128
QPAD = 256
LANES = 128
N_DEV = 8
VMEM_LIMIT = 56 * 1024 * 1024

ADAM_LR = 0.001
ADAM_B1 = 0.9
ADAM_B2 = 0.999
ADAM_EPS = 1e-08
ADAM_WD = 0.01
ADAM_STEP = 10

NN = (((1,), (0,)), ((), ()))
NT = (((1,), (1,)), ((), ()))
TN = (((0,), (0,)), ((), ()))


def _dot(a, b, dims):
    return lax.dot_general(a.astype(BF16), b.astype(BF16), dims, preferred_element_type=F32)


def _tile(dim, pref):
    if dim <= pref:
        return dim
    t = (pref // LANES) * LANES
    while t >= LANES:
        if dim % t == 0:
            return t
        t -= LANES
    raise ValueError(f"no tile for {dim}")


def _params(sem):
    return pltpu.CompilerParams(dimension_semantics=sem, vmem_limit_bytes=VMEM_LIMIT)


def _sig(v):
    return 1.0 / (1.0 + jnp.exp(-v))


def _mm(name, a, b, mode, out_dtypes, *, tm=1024, tn=1024, tk=2048, extras=(), epilogue=None, ride=None,
        b_by_device=False, out_by_device=False):
    if b_by_device:
        b_cols = b.shape[2]
        b_shape = (b.shape[1], N_DEV * b_cols)
    else:
        b_shape = b.shape
    if mode == "nn":
        (m, k), (_, n) = a.shape, b_shape
    elif mode == "nt":
        (m, k), (n, _) = a.shape, b_shape
    else:
        (k, m), (_, n) = a.shape, b_shape
    tm, tn, tk = _tile(m, tm), _tile(n, tn), _tile(k, tk)
    if b_by_device and mode != "nt":
        tn = _tile(b_cols, tn)
    if out_by_device:
        tn = _tile(n // N_DEV, tn)
    nk = k // tk
    dims = {"nn": NN, "nt": NT, "tn": TN}[mode]
    a_spec = (pl.BlockSpec((tk, tm), lambda i, j, kk: (kk, i)) if mode == "tn"
              else pl.BlockSpec((tm, tk), lambda i, j, kk: (i, kk)))
    if b_by_device and mode == "nt":
        piece = min(tk, b_cols)
        n_b, per = tk // piece, b_cols // piece
        b_specs = [pl.BlockSpec((None, tn, piece),
                                lambda i, j, kk, p=p: ((kk * n_b + p) // per, j, (kk * n_b + p) % per))
                   for p in range(n_b)]
    elif b_by_device:
        per = b_cols // tn
        n_b, piece = 1, tk
        b_specs = [pl.BlockSpec((None, tk, tn), lambda i, j, kk: (j // per, kk, j % per))]
    else:
        n_b, piece = 1, tk
        b_specs = [pl.BlockSpec((tn, tk), lambda i, j, kk: (j, kk)) if mode == "nt"
                   else pl.BlockSpec((tk, tn), lambda i, j, kk: (kk, j))]
    tile_spec = pl.BlockSpec((tm, tn), lambda i, j, kk: (i, j))
    if out_by_device:
        per_out = n // N_DEV // tn
        out_spec = pl.BlockSpec((None, tm, tn), lambda i, j, kk: (j // per_out, i, j % per_out))
        out_dims = (N_DEV, m, n // N_DEV)
    else:
        out_spec, out_dims = tile_spec, (m, n)
    ex_arrays, ex_specs = [], []
    for e in extras:
        arr, off = e if isinstance(e, tuple) else (e, 0)
        ex_arrays.append(arr)
        if off == "whole":
            ex_specs.append(pl.BlockSpec(arr.shape, lambda i, j, kk, nd=arr.ndim: (0,) * nd))
        elif off is None:
            ex_specs.append(pl.BlockSpec((tm, arr.shape[1]), lambda i, j, kk: (i, 0)))
        else:
            ex_specs.append(pl.BlockSpec((tm, tn), lambda i, j, kk, off=off: (i, j + off)))
    n_ex = len(extras)
    single = not isinstance(out_dtypes, (tuple, list))
    dts = (out_dtypes,) if single else tuple(out_dtypes)
    out_specs, out_shapes = [], []
    for dt in dts:
        if isinstance(dt, tuple):
            dt, mult, width = dt
            out_specs.append(pl.BlockSpec((tm, mult * tn), lambda i, j, kk: (i, j)))
            out_shapes.append(jax.ShapeDtypeStruct((m, width), dt))
        else:
            out_specs.append(out_spec)
            out_shapes.append(jax.ShapeDtypeStruct(out_dims, dt))

    grid = (m // tm, n // tn, nk)
    r_in, r_out, r_sc = ride.counts() if ride else (0, 0, 0)
    n_acc = 1 if nk > 1 else 0

    def body(a_ref, *rest):
        b_refs, rest = rest[:n_b], rest[n_b:]
        ex, rest = rest[:n_ex], rest[n_ex:]
        ride_in, rest = rest[:r_in], rest[r_in:]
        outs, rest = rest[:len(dts)], rest[len(dts):]
        ride_out, rest = rest[:r_out], rest[r_out:]
        ride_scratch = rest[n_acc:]
        if ride:
            first, mid, last = _steps([pl.program_id(d) for d in range(3)], grid)
            ride.run(ride_in, ride_out, ride_scratch, (first, mid, None))

        def finish(r):
            vals = (r,) if epilogue is None else epilogue(r, *[e[...] for e in ex])
            for o, v in zip(outs, vals):
                o[...] = v.astype(o.dtype)

        if n_b == 1:
            part = _dot(a_ref[...], b_refs[0][...], dims)
        else:
            part = sum(_dot(a_ref[:, p * piece:(p + 1) * piece], b_refs[p][...], dims) for p in range(n_b))
        if nk == 1:
            finish(part)
        else:
            acc = rest[0]
            kk = pl.program_id(2)

            @pl.when(kk == 0)
            def _():
                acc[...] = part

            @pl.when(jnp.logical_and(kk > 0, kk < nk - 1))
            def _():
                acc[...] += part

            @pl.when(kk == nk - 1)
            def _():
                finish(acc[...] + part)

        if ride:
            ride.run(ride_in, ride_out, ride_scratch, (None, None, last))

    res = pl.pallas_call(
        body, name=name, grid=grid,
        in_specs=[a_spec] + b_specs + ex_specs + [ANY] * r_in,
        out_specs=out_specs + [ANY] * r_out,
        out_shape=out_shapes + (ride.out_shape if ride else []),
        scratch_shapes=([pltpu.VMEM((tm, tn), F32)] if nk > 1 else []) + (ride.scratch if ride else []),
        compiler_params=_params(("arbitrary",) * 3 if ride else ("parallel", "parallel", "arbitrary")),
    )(a, *[b] * n_b, *ex_arrays, *(ride.ins if ride else []))
    own = res[0] if single else res[:len(dts)]
    return (own, res[len(dts):]) if ride else own


def _rows(name, body, n_rows, tm, ins, outs, accs=(), into=None):
    in_specs, args = [], []
    for t in ins:
        if len(t) == 1:
            in_specs.append(pl.BlockSpec(t[0].shape, lambda i, nd=t[0].ndim: (0,) * nd))
        else:
            in_specs.append(pl.BlockSpec((tm, t[1]), lambda i, cb=t[2]: (i, cb)))
        args.append(t[0])
    outs = [(o + (o[0], 0))[:4] for o in outs]
    out_specs = [pl.BlockSpec((tm, w), lambda i, cb=cb: (i, cb)) for w, _, _, cb in outs]
    out_shape = [jax.ShapeDtypeStruct((n_rows, total), d) for _, d, total, _ in outs]
    aliases, kernel = {}, body
    if into is not None:
        arr, w, cb = into
        in_specs.append(ANY)
        args.append(arr)
        out_specs.append(pl.BlockSpec((tm, w), lambda i: (i, cb)))
        out_shape.append(jax.ShapeDtypeStruct(arr.shape, arr.dtype))
        aliases = {len(ins): len(outs)}
        n_in = len(ins)

        def kernel(*refs):
            body(*refs[:n_in], *refs[n_in + 1:])

    out_specs += [pl.BlockSpec((r, w), lambda i: (0, 0)) for r, w in accs]
    out_shape += [jax.ShapeDtypeStruct((r, w), F32) for r, w in accs]
    return pl.pallas_call(
        kernel, name=name, grid=(n_rows // tm,), in_specs=in_specs, out_specs=out_specs, out_shape=out_shape,
        input_output_aliases=aliases, compiler_params=_params(("arbitrary",) if accs else ("parallel",)),
    )(*args)


def _zero_first(*accs):
    @pl.when(pl.program_id(0) == 0)
    def _():
        for a in accs:
            a[...] = jnp.zeros_like(a)


def _rope64(t, cos, sin):
    return t * cos + pltpu.roll(t, RET_QK // 2, 1) * sin


def _rope32(t, cos, sin_a, sin_b):
    return t * cos + pltpu.roll(t, LANES - ROPE // 2, 1) * sin_a + pltpu.roll(t, ROPE // 2, 1) * sin_b


def _rms_fwd(name, x, g, tm):
    s, d = x.shape

    def body(x_ref, g_ref, u_ref):
        v = x_ref[...]
        r = lax.rsqrt(jnp.mean(v * v, axis=-1, keepdims=True) + EPS)
        u_ref[...] = (v * r * g_ref[...]).astype(BF16)

    return _rows(name, body, s, tm, [(x, d, 0), (g,)], [(d, BF16)])[0]


def _residual_norm(r, x, g):
    h = x + r
    return h, h * lax.rsqrt(jnp.mean(h * h, axis=-1, keepdims=True) + EPS) * g


def _gate_grads(dm, gr, gm, yr, ym):
    sr, sm = _sig(gr), _sig(gm)
    return dm * sr, dm * sm, jnp.concatenate([dm * yr * (sr * (1.0 - sr)), dm * ym * (sm * (1.0 - sm))], axis=1)


def _rms_bwd(name, dy, x, g, dres, tm, matmul_copy=False):
    s, d = x.shape

    def body(dy_ref, x_ref, g_ref, dres_ref, dx_ref, *rest):
        dg_ref = rest[-1]
        _zero_first(dg_ref)
        v, dyv = x_ref[...], dy_ref[...]
        r = lax.rsqrt(jnp.mean(v * v, axis=-1, keepdims=True) + EPS)
        xh = v * r
        dxh = dyv * g_ref[...]
        dx = dres_ref[...] + r * (dxh - xh * jnp.mean(dxh * xh, axis=-1, keepdims=True))
        dx_ref[...] = dx
        if matmul_copy:
            rest[0][...] = dx.astype(BF16)
        dg_ref[...] += jnp.sum(dyv * xh, axis=0, keepdims=True)

    return _rows(name, body, s, tm, [(dy, d, 0), (x, d, 0), (g,), (dres, d, 0)],
                 [(d, F32)] + [(d, BF16)] * matmul_copy, [(1, d)])


def _final(name, h1, dn, g, tgt, tm):
    s, d = h1.shape

    def body(h_ref, dn_ref, g_ref, t_ref, dh_ref, dh16_ref, dg_ref, loss_ref):
        _zero_first(dg_ref, loss_ref)
        v = h_ref[...] + dn_ref[...]
        r = lax.rsqrt(jnp.mean(v * v, axis=-1, keepdims=True) + EPS)
        xh = v * r
        gv = g_ref[...]
        e = xh * gv - t_ref[...]
        loss_ref[...] += 0.5 * jnp.sum(jnp.mean(e * e, axis=-1, keepdims=True))
        dy = e * (1.0 / d)
        dg_ref[...] += jnp.sum(dy * xh, axis=0, keepdims=True)
        dxh = dy * gv
        dh = r * (dxh - xh * jnp.mean(dxh * xh, axis=-1, keepdims=True))
        dh_ref[...] = dh
        dh16_ref[...] = dh.astype(BF16)

    return _rows(name, body, s, tm, [(h1, d, 0), (dn, d, 0), (g,), (tgt, d, 0)], [(d, F32), (d, BF16)],
                 [(1, d), (1, LANES)])


def _decay_mask(lg, blk):
    n = lax.broadcasted_iota(jnp.int32, (blk, blk), 0)
    m = lax.broadcasted_iota(jnp.int32, (blk, blk), 1)
    w = jnp.exp(lg * jnp.abs(n - m).astype(F32))
    return jnp.where(jnp.right_shift(m, CHUNK_SHIFT) <= jnp.right_shift(n, CHUNK_SHIFT), w, 0.0)


def _decays(lg, blk):
    pos = lax.broadcasted_iota(jnp.int32, (blk, 1), 0).astype(F32)
    return jnp.exp(lg * (pos + 1.0)), jnp.exp(lg * (blk - 1.0 - pos)), jnp.exp(lg * float(blk))


def _ret_fwd(proj, lay, cos, sin, lgs, gain, blk, ride=None):
    s = proj.shape[0]
    heads = lay["ret_heads"]
    nb = s // blk
    scale = RET_QK ** -0.5
    ride_in_specs, ride_ins, ride_out_specs, ride_out_shape, ride_scratch = _ride_args(ride)

    def body(lg_ref, qkv_ref, cos_ref, sin_ref, g_ref, o_ref, st_ref, ry_ref, state, mask):
        lg = lg_ref[0:1, 0:1]

        @pl.when(pl.program_id(1) == 0)
        def _():
            state[...] = jnp.zeros_like(state)
            mask[...] = _decay_mask(lg, blk)

        a, c, gb = _decays(lg, blk)
        q = _rope64(qkv_ref[:, :RET_QK], cos_ref[...], sin_ref[...])
        k = _rope64(qkv_ref[:, RET_QK:2 * RET_QK], cos_ref[...], sin_ref[...]) * scale
        v = qkv_ref[:, 2 * RET_QK:2 * RET_QK + RET_V]
        st = state[...]
        st_ref[...] = st
        sm = _dot(q, k, NT) * mask[...]
        o = _dot(sm, v, NN) + _dot(q * a, st, NN)
        o_ref[...] = o
        state[...] = st * gb + _dot(k * c, v, TN)
        dlt = o - jnp.mean(o, axis=-1, keepdims=True)
        rstd = lax.rsqrt(jnp.mean(dlt * dlt, axis=-1, keepdims=True) + EPS)
        rg = qkv_ref[:, 2 * RET_QK + RET_V:]
        ry_ref[...] = (dlt * rstd * g_ref[...] * (rg * _sig(rg))).astype(BF16)

    first = lay["off"]["heads"] // RET_HEAD
    res = pl.pallas_call(
        _with_ride(body, ride, (heads, nb), 0, 5, 3), name="ret_fwd", grid=(heads, nb),
        in_specs=[pl.BlockSpec((None, 8, LANES), lambda h, b: (h, 0, 0)),
                  pl.BlockSpec((blk, RET_HEAD), lambda h, b: (b, first + h)),
                  pl.BlockSpec((blk, LANES), lambda h, b: (b, 0)),
                  pl.BlockSpec((blk, LANES), lambda h, b: (b, 0)),
                  pl.BlockSpec((1, RET_V), lambda h, b: (0, h))] + ride_in_specs,
        out_specs=[pl.BlockSpec((blk, RET_V), lambda h, b: (b, h)),
                   pl.BlockSpec((None, None, RET_QK, RET_V), lambda h, b: (h, b, 0, 0)),
                   pl.BlockSpec((blk, RET_V), lambda h, b: (b, h))] + ride_out_specs,
        out_shape=[jax.ShapeDtypeStruct((s, heads * RET_V), F32),
                   jax.ShapeDtypeStruct((heads, nb, RET_QK, RET_V), F32),
                   jax.ShapeDtypeStruct((s, heads * RET_V), BF16)] + ride_out_shape,
        scratch_shapes=[pltpu.VMEM((RET_QK, RET_V), F32), pltpu.VMEM((blk, blk), F32)] + ride_scratch,
        compiler_params=_params(("arbitrary", "arbitrary") if ride else ("parallel", "arbitrary")),
    )(lgs, proj, cos, sin, gain, *ride_ins)
    return (res[0], res[1], res[2], res[3:]) if ride else res


def _ret_bwd(proj, lay, cos, sin, lgs, states, d_ry, o, gain, d_proj, blk, ride=None):
    ride_in_specs, ride_ins, ride_out_specs, ride_out_shape, ride_scratch = _ride_args(ride)
    s = proj.shape[0]
    heads = lay["ret_heads"]
    nb = s // blk
    scale = RET_QK ** -0.5

    def body(lg_ref, qkv_ref, cos_ref, sin_ref, st_ref, dry_ref, o_ref, g_ref, _, dqkv_ref, dg_ref, dstate, mask):
        lg = lg_ref[0:1, 0:1]

        @pl.when(pl.program_id(1) == 0)
        def _():
            dstate[...] = jnp.zeros_like(dstate)
            mask[...] = _decay_mask(lg, blk)
            dg_ref[...] = jnp.zeros_like(dg_ref)

        oh = o_ref[...]
        dlt = oh - jnp.mean(oh, axis=-1, keepdims=True)
        rstd = lax.rsqrt(jnp.mean(dlt * dlt, axis=-1, keepdims=True) + EPS)
        oh = dlt * rstd
        gv = g_ref[...]
        rg = qkv_ref[:, 2 * RET_QK + RET_V:]
        sg = _sig(rg)
        dry = dry_ref[...]
        dt = dry * (rg * sg)
        dqkv_ref[:, 2 * RET_QK + RET_V:] = (dry * (oh * gv) * (sg * (1.0 + rg * (1.0 - sg)))).astype(BF16)
        dg_ref[...] += jnp.sum(dt * oh, axis=0, keepdims=True)
        doh = dt * gv
        do = rstd * (doh - jnp.mean(doh, axis=-1, keepdims=True) - oh * jnp.mean(doh * oh, axis=-1, keepdims=True))

        a, c, gb = _decays(lg, blk)
        cs, sn = cos_ref[...], sin_ref[...]
        q = _rope64(qkv_ref[:, :RET_QK], cs, sn)
        k = _rope64(qkv_ref[:, RET_QK:2 * RET_QK], cs, sn) * scale
        v = qkv_ref[:, 2 * RET_QK:2 * RET_QK + RET_V]
        st = st_ref[...]
        dst = dstate[...]
        mk = mask[...]
        sm = _dot(q, k, NT) * mk
        ds = _dot(do, v, NT) * mk
        dq = _dot(ds, k, NN) + _dot(do, st, NT) * a
        dk = _dot(ds, q, TN) + _dot(v, dst, NT) * c
        dqkv_ref[:, 2 * RET_QK:2 * RET_QK + RET_V] = (_dot(sm, do, TN) + _dot(k * c, dst, NN)).astype(BF16)
        dstate[...] = dst * gb + _dot(q * a, do, TN)
        dqkv_ref[:, :RET_QK] = _rope64(dq, cs, -sn).astype(BF16)
        dqkv_ref[:, RET_QK:2 * RET_QK] = (_rope64(dk, cs, -sn) * scale).astype(BF16)

    first = lay["off"]["heads"] // RET_HEAD
    last = nb - 1
    head_tile = pl.BlockSpec((blk, RET_V), lambda h, b: (last - b, h))
    res = pl.pallas_call(
        _with_ride(body, ride, (heads, nb), 0, 9, 2), name="ret_bwd", grid=(heads, nb),
        in_specs=[pl.BlockSpec((None, 8, LANES), lambda h, b: (h, 0, 0)),
                  pl.BlockSpec((blk, RET_HEAD), lambda h, b: (last - b, first + h)),
                  pl.BlockSpec((blk, LANES), lambda h, b: (last - b, 0)),
                  pl.BlockSpec((blk, LANES), lambda h, b: (last - b, 0)),
                  pl.BlockSpec((None, None, RET_QK, RET_V), lambda h, b: (h, last - b, 0, 0)),
                  head_tile, head_tile, pl.BlockSpec((1, RET_V), lambda h, b: (0, h)), ANY] + ride_in_specs,
        out_specs=[pl.BlockSpec((blk, RET_HEAD), lambda h, b: (last - b, first + h)),
                   pl.BlockSpec((1, RET_V), lambda h, b: (0, h))] + ride_out_specs,
        out_shape=[jax.ShapeDtypeStruct(d_proj.shape, d_proj.dtype),
                   jax.ShapeDtypeStruct((1, heads * RET_V), F32)] + ride_out_shape,
        scratch_shapes=[pltpu.VMEM((RET_QK, RET_V), F32), pltpu.VMEM((blk, blk), F32)] + ride_scratch,
        input_output_aliases={8: 0},
        compiler_params=_params(("arbitrary", "arbitrary") if ride else ("parallel", "arbitrary")),
    )(lgs, proj, cos, sin, states, d_ry, o, gain, d_proj, *ride_ins)
    return (res[0], res[1], res[2:]) if ride else res[:2]


def _mla_prep(proj, lay, gq, gkv, cos, sin_a, sin_b, tm):
    s = proj.shape[0]
    ql, kl = lay["q_lora"], lay["kv_lora"]

    def body(cq_ref, ckv_ref, kpe_ref, gq_ref, gkv_ref, cos_ref, sa_ref, sb_ref, cqn_ref, ckvn_ref, kpr_ref):
        for src, gref, dst in ((cq_ref, gq_ref, cqn_ref), (ckv_ref, gkv_ref, ckvn_ref)):
            v = src[...]
            r = lax.rsqrt(jnp.mean(v * v, axis=-1, keepdims=True) + EPS)
            dst[...] = (v * r * gref[...]).astype(BF16)
        kpr_ref[...] = _rope32(kpe_ref[...], cos_ref[...], sa_ref[...], sb_ref[...]).astype(BF16)

    off = lay["off"]
    return _rows("mla_prep", body, s, tm,
                 [(proj, ql, off["c_q"] // ql), (proj, kl, off["c_kv"] // kl), (proj, LANES, off["k_pe"] // LANES),
                  (gq,), (gkv,), (cos, LANES, 0), (sin_a, LANES, 0), (sin_b, LANES, 0)],
                 [(ql, BF16), (kl, BF16), (LANES, BF16)])


def _latent_norm_bwd(name, proj, offset, d_normed, g, d_proj, tm):
    s, w = d_normed.shape

    def body(dy_ref, x_ref, g_ref, dx_ref, dg_ref):
        _zero_first(dg_ref)
        v, dy = x_ref[...], dy_ref[...]
        r = lax.rsqrt(jnp.mean(v * v, axis=-1, keepdims=True) + EPS)
        xh = v * r
        dxh = dy * g_ref[...]
        dx_ref[...] = (r * (dxh - xh * jnp.mean(dxh * xh, axis=-1, keepdims=True))).astype(BF16)
        dg_ref[...] += jnp.sum(dy * xh, axis=0, keepdims=True)

    return _rows(name, body, s, tm, [(d_normed, w, 0), (proj, w, offset // w), (g,)], [], [(1, w)],
                 into=(d_proj, w, offset // w))


def _q_operand(r, cos, sin_a, sin_b):
    qs = (NOPE + ROPE) ** -0.5 * math.log2(math.e)
    cs, sa, sb = cos * qs, sin_a * qs, sin_b * qs
    parts = []
    for lo in range(0, r.shape[1], QPAD):
        parts += [r[:, lo:lo + NOPE] * qs, _rope32(r[:, lo + NOPE:lo + QPAD], cs, sa, sb)]
    return (jnp.concatenate(parts, axis=1),)


def _k_operand(r, kpr):
    parts = []
    for lo in range(0, r.shape[1], QPAD):
        parts += [r[:, lo:lo + NOPE], kpr.astype(F32)]
    return r, jnp.concatenate(parts, axis=1)


def _rope_key_grad(parts, lay, d_proj, tm):
    s, w = parts.shape

    def body(p_ref, dkpe_ref):
        dkpe_ref[:, :LANES] = sum(p_ref[:, lo:lo + LANES] for lo in range(0, w, LANES)).astype(BF16)
        dkpe_ref[:, LANES:] = jnp.zeros((tm, LANES), BF16)

    return _rows("rope_key_grad", body, s, tm, [(parts, w, 0)], [],
                 into=(d_proj, 2 * LANES, lay["off"]["k_pe"] // (2 * LANES)))[0]


def _diag_mask(t, keys_on_rows=False):
    row = lax.broadcasted_iota(jnp.int32, (t, t), 0)
    col = lax.broadcasted_iota(jnp.int32, (t, t), 1)
    key, query = (row, col) if keys_on_rows else (col, row)
    return jnp.right_shift(key, CHUNK_SHIFT) <= jnp.right_shift(query, CHUNK_SHIFT)


def _tile_pairs(nt, by_key):
    if by_key:
        pairs = [(i, j) for j in range(nt) for i in range(j, nt)]
    else:
        pairs = [(i, j) for i in range(nt) for j in range(i + 1)]
    return (jnp.asarray([p[0] for p in pairs], jnp.int32), jnp.asarray([p[1] for p in pairs], jnp.int32))


def _head_block(heads):
    return 4 if heads % 4 == 0 else 2 if heads % 2 == 0 else 1


def _attn_fwd(qf, kf, kv, lay, t, ride=None):
    s = qf.shape[0]
    heads = lay["mla_heads"]
    hb = _head_block(heads)
    nt = s // t
    qi, kj = _tile_pairs(nt, False)
    grid = (heads // hb, int(qi.shape[0]))
    ride_in_specs, ride_ins, ride_out_specs, ride_out_shape, ride_scratch = _ride_args(ride)

    def body(qi_ref, kj_ref, q_ref, k_ref, kv_ref, o_ref, lse_ref, m_s, l_s, acc):
        p = pl.program_id(1)
        i, j = qi_ref[p], kj_ref[p]

        @pl.when(j == 0)
        def _():
            m_s[...] = jnp.full_like(m_s, -jnp.inf)
            l_s[...] = jnp.zeros_like(l_s)
            acc[...] = jnp.zeros_like(acc)

        def step(diagonal):
            ones = jnp.ones((t, LANES), BF16)
            scores = [_dot(q_ref[:, hh * QPAD:(hh + 1) * QPAD], k_ref[:, hh * QPAD:(hh + 1) * QPAD], NT)
                      for hh in range(hb)]
            for hh in range(hb):
                sc = scores[hh]
                if diagonal:
                    sc = jnp.where(_diag_mask(t), sc, -jnp.inf)
                cols = [sc[:, c * LANES:(c + 1) * LANES] for c in range(t // LANES)]
                m_old = m_s[hh]
                m_new = jnp.maximum(m_old, jnp.max(functools.reduce(jnp.maximum, cols), axis=-1, keepdims=True))
                alpha = jnp.exp2(m_old - m_new)
                pr = jnp.concatenate([jnp.exp2(c - m_new).astype(BF16) for c in cols], axis=1)
                pv = _dot(pr, jnp.concatenate([kv_ref[:, hh * QPAD + NOPE:(hh + 1) * QPAD], ones], axis=1), NN)
                l_new = alpha * l_s[hh] + pv[:, VHEAD:]
                a_new = alpha * acc[hh] + pv[:, :VHEAD]
                if diagonal:
                    o_ref[:, hh * VHEAD:(hh + 1) * VHEAD] = a_new / l_new
                    lse_ref[hh] = jnp.transpose(m_new + jnp.log2(l_new))[:1]
                else:
                    m_s[hh], l_s[hh], acc[hh] = m_new, l_new, a_new

        pl.when(j < i)(functools.partial(step, False))
        pl.when(j == i)(functools.partial(step, True))

    res = pl.pallas_call(
        _with_ride(body, ride, grid, 2, 3, 2), name="attn_fwd",
        grid_spec=pltpu.PrefetchScalarGridSpec(
            num_scalar_prefetch=2, grid=grid,
            in_specs=[pl.BlockSpec((t, hb * QPAD), lambda h, p, qi, kj: (qi[p], h)),
                      pl.BlockSpec((t, hb * QPAD), lambda h, p, qi, kj: (kj[p], h)),
                      pl.BlockSpec((t, hb * QPAD), lambda h, p, qi, kj: (kj[p], h))] + ride_in_specs,
            out_specs=[pl.BlockSpec((t, hb * VHEAD), lambda h, p, qi, kj: (qi[p], h)),
                       pl.BlockSpec((hb, 1, t), lambda h, p, qi, kj: (h, 0, qi[p]))] + ride_out_specs,
            scratch_shapes=[pltpu.VMEM((hb, t, LANES), F32), pltpu.VMEM((hb, t, LANES), F32),
                            pltpu.VMEM((hb, t, VHEAD), F32)] + ride_scratch),
        out_shape=[jax.ShapeDtypeStruct((s, heads * VHEAD), F32),
                   jax.ShapeDtypeStruct((heads, 1, s), F32)] + ride_out_shape,
        compiler_params=_params(("arbitrary", "arbitrary") if ride else ("parallel", "arbitrary")),
    )(qi, kj, qf, kf, kv, *ride_ins)
    return (res[0], res[1], res[2:]) if ride else res


def _attn_delta(d_o, o, lay, tm):
    s = o.shape[0]
    heads = lay["mla_heads"]

    def body(do_ref, o_ref, dl_ref):
        for h in range(heads):
            sl = slice(h * VHEAD, (h + 1) * VHEAD)
            dl_ref[h] = jnp.sum(jnp.transpose(do_ref[:, sl] * o_ref[:, sl]), axis=0, keepdims=True)

    tile = pl.BlockSpec((tm, heads * VHEAD), lambda i: (i, 0))
    return pl.pallas_call(
        body, name="attn_delta", grid=(s // tm,), in_specs=[tile, tile],
        out_specs=pl.BlockSpec((heads, 1, tm), lambda i: (0, 0, i)),
        out_shape=jax.ShapeDtypeStruct((heads, 1, s), F32),
        compiler_params=_params(("parallel",)),
    )(d_o, o)


def _attn_bwd(qf, kf, kv, lse, delta, d_o, cos, sin_a, sin_b, lay, t, ride=None):
    s = qf.shape[0]
    heads = lay["mla_heads"]
    hb = _head_block(heads)
    nt = s // t
    scale = (NOPE + ROPE) ** -0.5
    qi, kj = _tile_pairs(nt, True)
    grid = (heads // hb, int(qi.shape[0]))
    ride_in_specs, ride_ins, ride_out_specs, ride_out_shape, ride_scratch = _ride_args(ride)

    def body(qi_ref, kj_ref, q_ref, k_ref, kv_ref, lse_ref, dl_ref, do_ref, cos_ref, sa_ref, sb_ref,
             dqp_ref, dkv_ref, dkpe_ref, dq_acc, dk_acc, dv_acc):
        p = pl.program_id(1)
        i, j = qi_ref[p], kj_ref[p]
        rows = pl.ds(pl.multiple_of(i * t, t), t)

        def unrope(v):
            return _rope32(v, cos_ref[...], -sa_ref[...], -sb_ref[...])

        @pl.when(p == 0)
        def _():
            dq_acc[...] = jnp.zeros_like(dq_acc)

        def step(diagonal):
            for hh in range(hb):
                lo = hh * QPAD
                q, k = q_ref[:, lo:lo + QPAD], k_ref[:, lo:lo + QPAD]
                do = do_ref[:, hh * VHEAD:(hh + 1) * VHEAD]
                pr = jnp.exp2(_dot(k, q, NT) - lse_ref[hh])
                if diagonal:
                    pr = jnp.where(_diag_mask(t, keys_on_rows=True), pr, 0.0)
                dv_part = _dot(pr, do, NN)
                ds = (pr * (_dot(kv_ref[:, lo + NOPE:lo + QPAD], do, NT) - dl_ref[hh])).astype(BF16)
                dk_part = _dot(ds, q, NN)
                dq = dq_acc[rows, lo:lo + QPAD] + _dot(ds, k, TN) * scale
                if diagonal:
                    dqp_ref[:, lo:lo + NOPE] = dq[:, :NOPE].astype(BF16)
                    dqp_ref[:, lo + NOPE:lo + QPAD] = unrope(dq[:, NOPE:]).astype(BF16)
                    dk_acc[hh], dv_acc[hh] = dk_part, dv_part
                else:
                    dq_acc[rows, lo:lo + QPAD] = dq
                    dk_acc[hh] += dk_part
                    dv_acc[hh] += dv_part

        pl.when(i > j)(functools.partial(step, False))
        pl.when(i == j)(functools.partial(step, True))

        @pl.when(i == nt - 1)
        def _():
            kpe = jnp.zeros((t, LANES), F32)
            for hh in range(hb):
                lo = hh * QPAD
                dk = dk_acc[hh] * math.log(2.0)
                dkv_ref[:, lo:lo + NOPE] = dk[:, :NOPE].astype(BF16)
                dkv_ref[:, lo + NOPE:lo + QPAD] = dv_acc[hh].astype(BF16)
                kpe = kpe + dk[:, NOPE:]
            dkpe_ref[...] = unrope(kpe)

    table = pl.BlockSpec((t, LANES), lambda h, p, qi, kj: (kj[p], 0))
    res = pl.pallas_call(
        _with_ride(body, ride, grid, 2, 9, 3), name="attn_bwd",
        grid_spec=pltpu.PrefetchScalarGridSpec(
            num_scalar_prefetch=2, grid=grid,
            in_specs=[pl.BlockSpec((t, hb * QPAD), lambda h, p, qi, kj: (qi[p], h)),
                      pl.BlockSpec((t, hb * QPAD), lambda h, p, qi, kj: (kj[p], h)),
                      pl.BlockSpec((t, hb * QPAD), lambda h, p, qi, kj: (kj[p], h)),
                      pl.BlockSpec((hb, 1, t), lambda h, p, qi, kj: (h, 0, qi[p])),
                      pl.BlockSpec((hb, 1, t), lambda h, p, qi, kj: (h, 0, qi[p])),
                      pl.BlockSpec((t, hb * VHEAD), lambda h, p, qi, kj: (qi[p], h)),
                      table, table, table] + ride_in_specs,
            out_specs=[pl.BlockSpec((t, hb * QPAD), lambda h, p, qi, kj: (kj[p], h)),
                       pl.BlockSpec((t, hb * QPAD), lambda h, p, qi, kj: (kj[p], h)),
                       pl.BlockSpec((t, LANES), lambda h, p, qi, kj: (kj[p], h))] + ride_out_specs,
            scratch_shapes=[pltpu.VMEM((s, hb * QPAD), F32), pltpu.VMEM((hb, t, QPAD), F32),
                            pltpu.VMEM((hb, t, VHEAD), F32)] + ride_scratch),
        out_shape=[jax.ShapeDtypeStruct((s, heads * QPAD), BF16),
                   jax.ShapeDtypeStruct((s, heads * QPAD), BF16),
                   jax.ShapeDtypeStruct((s, heads // hb * LANES), F32)] + ride_out_shape,
        compiler_params=_params(("arbitrary", "arbitrary") if ride else ("parallel", "arbitrary")),
    )(qi, kj, qf, kf, kv, lse, delta, d_o, cos, sin_a, sin_b, *ride_ins)
    return (res[0], res[1], res[2], res[3:]) if ride else res


ANY = pl.BlockSpec(memory_space=pl.ANY)


def _place():
    return lax.axis_index("x"), lax.axis_index("y"), lax.axis_index("c")


def _other_chips(x, y):
    return [(1 - x, y), (x, 1 - y), (1 - x, 1 - y)]


class _Exchange:
    def __init__(self, ins, out_shape, scratch, phases):
        self.ins, self.out_shape, self.scratch, self.phases = list(ins), list(out_shape), list(scratch), phases

    def counts(self):
        return len(self.ins), len(self.out_shape), len(self.scratch)

    def run(self, r_in, r_out, r_scratch, conds):
        for cond, phase in zip(conds, self.phases):
            if phase is not None and cond is not None:
                pl.when(cond)(functools.partial(phase, r_in, r_out, r_scratch))


def _steps(ids, sizes):
    lin, total = 0, 1
    for i, n in zip(ids, sizes):
        lin, total = lin * n + i, total * n
    return lin == 0, lin == (5 * total) // 8, lin == total - 1


def _ride_args(ride):
    if ride is None:
        return [], [], [], [], []
    n_in, n_out, _ = ride.counts()
    return [ANY] * n_in, ride.ins, [ANY] * n_out, ride.out_shape, ride.scratch


def _with_ride(body, ride, grid, n_prefetch, n_in, n_out):
    if ride is None:
        return body
    r_in, r_out, r_sc = ride.counts()

    def hosted(*refs):
        cuts = (n_prefetch, n_in, r_in, n_out, r_out)
        parts, pos = [], 0
        for n in cuts:
            parts.append(refs[pos:pos + n])
            pos += n
        pre, ins, ride_in, outs, ride_out = parts
        scratch, ride_scratch = refs[pos:len(refs) - r_sc], refs[len(refs) - r_sc:]
        first, mid, last = _steps([pl.program_id(d) for d in range(len(grid))], grid)
        ride.run(ride_in, ride_out, ride_scratch, (first, mid, None))
        body(*pre, *ins, *outs, *scratch)
        ride.run(ride_in, ride_out, ride_scratch, (None, None, last))

    return hosted


def _exchange_alone(name, ex):
    n_in, n_out, _ = ex.counts()

    def body(*refs):
        for phase in ex.phases:
            if phase is not None:
                phase(refs[:n_in], refs[n_in:n_in + n_out], refs[n_in + n_out:])

    return pl.pallas_call(
        body, name=name, in_specs=[ANY] * n_in, out_specs=[ANY] * n_out, out_shape=ex.out_shape,
        scratch_shapes=ex.scratch)(*ex.ins)


def _gather_exchange(shards):
    nw = len(shards)

    def parts(ins, outs, sems):
        send_sems, recv_sems, local_sems = sems
        x, y, c = _place()

        def slot(px, py, pc):
            return 4 * px + 2 * py + pc

        def copy(w, k, rows, to, src=None):
            return pltpu.make_async_remote_copy(
                src_ref=rows if src is None else src, dst_ref=rows, send_sem=send_sems.at[w, k],
                recv_sem=recv_sems.at[w, k], device_id=to, device_id_type=MESH)

        def halves(w):
            rows = shards[w].shape[0]
            cut = -(-(rows // 2) // 16) * 16
            return pl.ds(0, cut), pl.ds(cut, rows - cut)

        def plan(w, mine):
            side = c if mine else 1 - c
            upper, lower = halves(w)
            whole = lambda px, py: outs[w].at[slot(px, py, side)]
            top = lambda px, py: outs[w].at[slot(px, py, side), upper]
            bottom = lambda px, py: outs[w].at[slot(px, py, side), lower]
            xn, yn, sib = (1 - x, y, side), (x, 1 - y, side), (x, y, 1 - side)
            own = ins[w] if mine else None
            return [copy(w, 0, whole(x, y), sib, own), copy(w, 1, whole(x, y), xn, own),
                    copy(w, 2, whole(x, y), yn, own), copy(w, 3, top(1 - x, y), yn), copy(w, 4, bottom(x, 1 - y), xn),
                    copy(w, 5, whole(1 - x, y), sib), copy(w, 6, whole(x, 1 - y), sib),
                    copy(w, 7, top(1 - x, 1 - y), sib), copy(w, 8, bottom(1 - x, 1 - y), sib)]

        def arrivals(w):
            upper, lower = halves(w)
            at = lambda px, py, *rows: outs[w].at[(slot(px, py, c),) + rows]
            return {1: copy(w, 1, at(1 - x, y), (x, y, c)), 2: copy(w, 2, at(x, 1 - y), (x, y, c)),
                    3: copy(w, 3, at(1 - x, 1 - y, upper), (x, y, c)),
                    4: copy(w, 4, at(1 - x, 1 - y, lower), (x, y, c))}

        local = [pltpu.make_async_copy(ins[w], outs[w].at[slot(x, y, c)], local_sems.at[w]) for w in range(nw)]
        return plan, arrivals, local

    def start(ins, outs, sems):
        plan, _, local = parts(ins, outs, sems)
        for cp in local:
            cp.start()
        for w in range(nw):
            for k in (0, 1, 2):
                plan(w, True)[k].start()

    def middle(ins, outs, sems):
        plan, arrivals, _ = parts(ins, outs, sems)
        for landed, onward in ((1, (3, 5)), (2, (4, 6))):
            for w in range(nw):
                arrivals(w)[landed].wait_recv()
                for k in onward:
                    plan(w, True)[k].start()

    def finish(ins, outs, sems):
        plan, arrivals, local = parts(ins, outs, sems)
        for landed, onward in ((3, 7), (4, 8)):
            for w in range(nw):
                arrivals(w)[landed].wait_recv()
                plan(w, True)[onward].start()
        for w in range(nw):
            from_sibling = plan(w, False)
            for k in (0, 5, 6, 7, 8):
                from_sibling[k].wait_recv()
            for cp in plan(w, True):
                cp.wait_send()
        for cp in local:
            cp.wait()

    return _Exchange(
        shards, [jax.ShapeDtypeStruct((N_DEV,) + s.shape, s.dtype) for s in shards],
        [pltpu.SemaphoreType.DMA((nw, 9)), pltpu.SemaphoreType.DMA((nw, 9)), pltpu.SemaphoreType.DMA((nw,))],
        (start, middle, finish))


def _sibling_exchange(grads):
    nw = len(grads)

    def copies(ins, outs, sems):
        x, y, c = _place()
        return [pltpu.make_async_remote_copy(
            src_ref=ins[w].at[2 * p + (1 - c)], dst_ref=outs[w].at[p], send_sem=sems[0].at[w, p],
            recv_sem=sems[1].at[w, p], device_id=(x, y, 1 - c), device_id_type=MESH)
            for w in range(nw) for p in range(4)]

    def start(ins, outs, sems):
        for cp in copies(ins, outs, sems):
            cp.start()

    def finish(ins, outs, sems):
        for cp in copies(ins, outs, sems):
            cp.wait()

    return _Exchange(grads, [jax.ShapeDtypeStruct((4,) + g.shape[1:], g.dtype) for g in grads],
                     [pltpu.SemaphoreType.DMA((nw, 4)), pltpu.SemaphoreType.DMA((nw, 4))], (start, None, finish))


def _chips_exchange(sums):
    nw = len(sums)

    def copies(ins, outs, sems):
        x, y, c = _place()
        return [pltpu.make_async_remote_copy(
            src_ref=ins[w].at[2 * px + py], dst_ref=outs[w].at[k], send_sem=sems[0].at[w, k],
            recv_sem=sems[1].at[w, k], device_id=(px, py, c), device_id_type=MESH)
            for w in range(nw) for k, (px, py) in enumerate(_other_chips(x, y))]

    def start(ins, outs, sems):
        for cp in copies(ins, outs, sems):
            cp.start()

    def finish(ins, outs, sems):
        for cp in copies(ins, outs, sems):
            cp.wait()

    return _Exchange(sums, [jax.ShapeDtypeStruct((3,) + g.shape[1:], g.dtype) for g in sums],
                     [pltpu.SemaphoreType.DMA((nw, 3)), pltpu.SemaphoreType.DMA((nw, 3))], (start, None, finish))


def _pair_sum(name, g, got, c_arr, tr):
    _, rows, cols = g.shape
    tr = _tile_rows(rows, tr)

    def body(c_ref, a_ref, b_ref, o_ref):
        o_ref[...] = (a_ref[...].astype(F32) + b_ref[...].astype(F32)).astype(BF16)

    return pl.pallas_call(
        body, name=name,
        grid_spec=pltpu.PrefetchScalarGridSpec(
            num_scalar_prefetch=1, grid=(4, rows // tr),
            in_specs=[pl.BlockSpec((None, tr, cols), lambda p, r, cr: (2 * p + cr[0], r, 0)),
                      pl.BlockSpec((None, tr, cols), lambda p, r, cr: (p, r, 0))],
            out_specs=pl.BlockSpec((None, tr, cols), lambda p, r, cr: (p, r, 0))),
        out_shape=jax.ShapeDtypeStruct((4, rows, cols), BF16),
        compiler_params=_params(("parallel", "parallel")),
    )(c_arr, g, got)


def _tile_rows(rows, pref):
    t = min(rows, pref)
    while rows % t or t % 8:
        t -= 1
    return t


def _adam(w, g, m, v):
    m = ADAM_B1 * m + (1.0 - ADAM_B1) * g
    v = ADAM_B2 * v + (1.0 - ADAM_B2) * (g * g)
    m_hat = m / (1.0 - ADAM_B1 ** ADAM_STEP)
    v_hat = v / (1.0 - ADAM_B2 ** ADAM_STEP)
    return -ADAM_LR * (m_hat / (jnp.sqrt(v_hat) + ADAM_EPS) + ADAM_WD * w), m, v


def _adamw_shard(name, w, m, v, sums, got, chip_arr, tr, tc=None):
    _, rows, cols = w.shape
    tr = _tile_rows(rows, tr)
    tc = cols if tc is None else _tile(cols, tc)

    def body(p_ref, w_ref, m_ref, v_ref, s_ref, r_ref, g_ref, d_ref, nm_ref, nv_ref):
        g = s_ref[...].astype(F32)
        for k in range(3):
            g = g + r_ref[k].astype(F32)
        g_ref[...] = g
        d_ref[...], nm_ref[...], nv_ref[...] = _adam(w_ref[...], g, m_ref[...], v_ref[...])

    tile = pl.BlockSpec((None, tr, tc), lambda r, q, pr: (0, r, q))
    return pl.pallas_call(
        body, name=name,
        grid_spec=pltpu.PrefetchScalarGridSpec(
            num_scalar_prefetch=1, grid=(rows // tr, cols // tc),
            in_specs=[tile, tile, tile,
                      pl.BlockSpec((None, tr, tc), lambda r, q, pr: (pr[0], r, q)),
                      pl.BlockSpec((3, tr, tc), lambda r, q, pr: (0, r, q))],
            out_specs=[tile] * 4),
        out_shape=[jax.ShapeDtypeStruct((1, rows, cols), F32)] * 4,
        compiler_params=_params(("parallel", "parallel")),
    )(chip_arr, w, m, v, sums, got)


def _gains_all_reduce_adam(grads, loss_part, ws, ms, vs):
    n = len(grads)
    widths = [g.shape[1] for g in grads]
    rows, width = -(-(n + 1) // 8) * 8, max(widths)

    def body(*refs):
        g_in, loss_in = refs[:n], refs[n]
        w_in, m_in, v_in = (refs[1 + k * n:1 + (k + 1) * n] for k in (1, 2, 3))
        outs = refs[1 + 4 * n:2 + 8 * n]
        g_out, d_out, m_out, v_out = (outs[k * n:(k + 1) * n] for k in range(4))
        loss_out = outs[4 * n]
        buf, send_sems, recv_sems = refs[2 + 8 * n:]
        x, y, c = _place()
        me = 4 * x + 2 * y + c
        buf[me] = jnp.zeros((rows, width), F32)
        for r in range(n):
            buf[me, r:r + 1, :widths[r]] = g_in[r][...]
        buf[me, n:n + 1, :LANES] = loss_in[...]
        peers = [(x, y, 1 - c)] + [(px, py, pc) for px, py in _other_chips(x, y) for pc in (c, 1 - c)]
        copies = []
        for k, peer in enumerate(peers):
            cp = pltpu.make_async_remote_copy(
                src_ref=buf.at[me], dst_ref=buf.at[me], send_sem=send_sems.at[k], recv_sem=recv_sems.at[k],
                device_id=peer, device_id_type=MESH)
            cp.start()
            copies.append(cp)
        for cp in copies:
            cp.wait()
        total = buf[0]
        for k in range(1, N_DEV):
            total = total + buf[k]
        for r in range(n):
            g = total[r:r + 1, :widths[r]]
            g_out[r][...] = g
            d_out[r][...], m_out[r][...], v_out[r][...] = _adam(w_in[r][...], g, m_in[r][...], v_in[r][...])
        loss_out[...] = total[n:n + 1, :LANES]

    vm = pl.BlockSpec(memory_space=pltpu.VMEM)
    shapes = [jax.ShapeDtypeStruct((1, w), F32) for w in widths]
    res = pl.pallas_call(
        body, name="gains_all_reduce_adamw",
        in_specs=[vm] * (4 * n + 1), out_specs=[vm] * (4 * n + 1),
        out_shape=shapes * 4 + [jax.ShapeDtypeStruct((1, LANES), F32)],
        scratch_shapes=[pltpu.VMEM((N_DEV, rows, width), F32), pltpu.SemaphoreType.DMA((7,)),
                        pltpu.SemaphoreType.DMA((7,))],
    )(*grads, loss_part, *ws, *ms, *vs)
    return res[:n], res[n:2 * n], res[2 * n:3 * n], res[3 * n:4 * n], res[4 * n]


IN_ORDER = ("r_q", "r_k", "r_v", "r_g", "c_q", "c_kv", "k_pe", "g_ret", "g_mla")
RET_HEAD = 2 * RET_QK + 2 * RET_V


def _make_layout(d, vw, qw, ql, kl, mla_w):
    width = {"r_q": qw, "r_k": qw, "r_v": vw, "r_g": vw, "c_q": ql, "c_kv": kl, "k_pe": ROPE, "g_ret": d, "g_mla": d}
    src, o = {}, 0
    for n in IN_ORDER:
        src[n] = o
        o += width[n]
    heads = vw // RET_V
    off, pieces, o = {}, [], 0

    def put(name, w, s):
        nonlocal o
        off.setdefault(name, o)
        pieces.append((o, w, s))
        o += w

    for n in ("g_ret", "g_mla", "c_q"):
        put(n, width[n], src[n])
    for h in range(heads):
        put("heads", RET_QK, src["r_q"] + h * RET_QK)
        put("heads", RET_QK, src["r_k"] + h * RET_QK)
        put("heads", RET_V, src["r_v"] + h * RET_V)
        put("heads", RET_V, src["r_g"] + h * RET_V)
    for n in ("c_kv", "k_pe"):
        put(n, width[n], src[n])
    total = off["k_pe"] + 2 * LANES
    for n, blk in (("g_ret", d), ("g_mla", d), ("c_q", ql), ("heads", RET_HEAD), ("c_kv", kl), ("k_pe", 2 * LANES)):
        assert off[n] % blk == 0
    return {"off": off, "pieces": pieces, "total": total, "n_in": sum(width.values()),
            "ret_heads": heads, "mla_heads": mla_w // VHEAD, "q_lora": ql, "kv_lora": kl}


def _cols_to_full(g):
    n, r, c = g.shape
    return jnp.transpose(g, (1, 0, 2)).reshape(r, n * c)


def _full_to_cols(w):
    r, c = w.shape
    return jnp.transpose(w.reshape(r, N_DEV, c // N_DEV), (1, 0, 2))


def _w_in_to_mine(g, lay):
    _, cols, d = g.shape
    parts, at = [], 0
    for o, w, s in lay["pieces"]:
        if o > at:
            parts.append(jnp.zeros((o - at, d), g.dtype))
        while w > 0:
            k, a = divmod(s, cols)
            take = min(w, cols - a)
            parts.append(g[k, a:a + take])
            s, w, o = s + take, w - take, o + take
        at = o
    parts.append(jnp.zeros((lay["total"] - at, d), g.dtype))
    return jnp.concatenate(parts, axis=0)


def _mine_to_blocks(g, lay):
    cols = lay["n_in"] // N_DEV
    by_src = sorted(lay["pieces"], key=lambda p: p[2])
    blocks = []
    for k in range(N_DEV):
        lo, hi, parts = k * cols, (k + 1) * cols, []
        for o, w, s in by_src:
            a, b = max(lo, s), min(hi, s + w)
            if a < b:
                parts.append(g[o + a - s:o + b - s])
        blocks.append(jnp.concatenate(parts, axis=0))
    return jnp.stack(blocks)


def _rope_tables(positions, half):
    inv = ROPE_THETA ** (-jnp.arange(half, dtype=F32) / half)
    ang = positions.astype(F32)[:, None] * inv
    return jnp.cos(ang), jnp.sin(ang)


def kernel(x, positions, norm_mix_g, w_in, ret_norm_g, w_ret_o, q_a_norm_g, w_q_b, kv_a_norm_g, w_kv_b, w_mla_o, w_out, norm_mlp_g, w_up, w_down, norm_f_g, loss_target, m_norm_mix_g, m_w_in, m_ret_norm_g, m_w_ret_o, m_q_a_norm_g, m_w_q_b, m_kv_a_norm_g, m_w_kv_b, m_w_mla_o, m_w_out, m_norm_mlp_g, m_w_up, m_w_down, m_norm_f_g, v_norm_mix_g, v_w_in, v_ret_norm_g, v_w_ret_o, v_q_a_norm_g, v_w_q_b, v_kv_a_norm_g, v_w_kv_b, v_w_mla_o, v_w_out, v_norm_mlp_g, v_w_up, v_w_down, v_norm_f_g):
    xs, tgt, pos = x[0], loss_target[0], positions[0]
    s, d = xs.shape
    mats = {"w_in": w_in[0], "w_ret_o": w_ret_o[0], "w_q_b": w_q_b[0], "w_kv_b": w_kv_b[0], "w_mla_o": w_mla_o[0],
            "w_out": w_out[0], "w_up": w_up[0], "w_down": w_down[0]}
    mat_w = {"w_in": w_in, "w_ret_o": w_ret_o, "w_q_b": w_q_b, "w_kv_b": w_kv_b, "w_mla_o": w_mla_o, "w_out": w_out,
             "w_up": w_up, "w_down": w_down}
    mat_m = {"w_in": m_w_in, "w_ret_o": m_w_ret_o, "w_q_b": m_w_q_b, "w_kv_b": m_w_kv_b, "w_mla_o": m_w_mla_o,
             "w_out": m_w_out, "w_up": m_w_up, "w_down": m_w_down}
    mat_v = {"w_in": v_w_in, "w_ret_o": v_w_ret_o, "w_q_b": v_w_q_b, "w_kv_b": v_w_kv_b, "w_mla_o": v_w_mla_o,
             "w_out": v_w_out, "w_up": v_w_up, "w_down": v_w_down}
    names = list(mats)
    col_sharded = ("w_in", "w_q_b", "w_kv_b", "w_up")
    vw = ret_norm_g.shape[1]
    mla_w = mats["w_mla_o"].shape[0] * N_DEV
    ql, kl = q_a_norm_g.shape[1], kv_a_norm_g.shape[1]
    n_in = mats["w_in"].shape[1] * N_DEV
    qw = (n_in - 2 * vw - ql - kl - ROPE - 2 * d) // 2
    lay = _make_layout(d, vw, qw, ql, kl, mla_w)
    assert lay["n_in"] == n_in
    heads_r, heads_m = lay["ret_heads"], lay["mla_heads"]

    shard16 = {n: mats[n].astype(BF16) for n in names}
    shard16["w_in"] = jnp.swapaxes(w_in, 1, 2)[0].astype(BF16)
    with_in_proj = ("w_ret_o", "w_q_b", "w_kv_b", "w_mla_o", "w_out")
    mlp = ("w_up", "w_down")
    by_device = ("w_up", "w_kv_b")
    full = {}

    def keep(group, gathered):
        for n, g in zip(group, gathered):
            if n not in by_device:
                g = _cols_to_full(g) if n in col_sharded else g.reshape(-1, g.shape[2])
            full[n] = g

    w_mine = _w_in_to_mine(_exchange_alone("gather_w_in", _gather_exchange([shard16["w_in"]]))[0], lay)

    c64, s64 = _rope_tables(pos, RET_QK // 2)
    cos_r = jnp.concatenate([c64, c64], axis=1)
    sin_r = jnp.concatenate([-s64, s64], axis=1)
    c32, s32 = _rope_tables(pos, ROPE // 2)
    z32, z64 = jnp.zeros_like(c32), jnp.zeros((s, LANES - ROPE), F32)
    cos_p = jnp.concatenate([c32, c32, z64], axis=1)
    sin_a = jnp.concatenate([-s32, z32, z64], axis=1)
    sin_b = jnp.concatenate([z32, s32, z64], axis=1)
    lg = jnp.log(1.0 - 2.0 ** (-5.0 - jnp.arange(heads_r, dtype=F32)))
    lgs = jnp.broadcast_to(lg[:, None, None], (heads_r, 8, LANES))

    tm = min(256, s)
    blk = min(512, s)
    t_att = min(512, s)

    u = _rms_fwd("norm_mix", xs, norm_mix_g, tm)
    proj, gathered = _mm("in_proj", u, w_mine, "nt", F32, tn=2304,
                         ride=_gather_exchange([shard16[n] for n in with_in_proj]))
    keep(with_in_proj, gathered)
    wq_pad = jnp.pad(full["w_q_b"].reshape(ql, heads_m, NOPE + ROPE),
                     ((0, 0), (0, 0), (0, QPAD - NOPE - ROPE))).reshape(ql, heads_m * QPAD)
    o_ret, states, ry = _ret_fwd(proj, lay, cos_r, sin_r, lgs, ret_norm_g, blk)
    y_ret = _mm("ret_out", ry, full["w_ret_o"], "nn", F32)
    cqn, ckvn, kpr = _mla_prep(proj, lay, q_a_norm_g, kv_a_norm_g, cos_p, sin_a, sin_b, tm)
    qf = _mm("q_up", cqn, wq_pad, "nn", BF16, extras=((cos_p, None), (sin_a, None), (sin_b, None)), epilogue=_q_operand)
    kv, kf = _mm("kv_up", ckvn, full["w_kv_b"], "nn", (BF16, BF16), b_by_device=True, extras=((kpr, None),),
                 epilogue=_k_operand)
    o_mla, lse, gathered = _attn_fwd(qf, kf, kv, lay, t_att, ride=_gather_exchange([shard16["w_up"]]))
    keep(("w_up",), gathered)
    gate_tile = _tile(d, 1024)
    y_mla, merged = _mm(
        "mla_out", o_mla, full["w_mla_o"], "nn", (F32, BF16), tm=512, tn=gate_tile,
        extras=((proj, lay["off"]["g_ret"] // gate_tile), (proj, lay["off"]["g_mla"] // gate_tile), y_ret),
        epilogue=lambda r, gr, gm, yr: (r, _sig(gr) * yr + _sig(gm) * r))
    h1, n2 = _mm("out_proj", merged, full["w_out"], "nn", (F32, BF16), tm=512, tn=d,
                 extras=(xs, (norm_mlp_g, "whole")), epilogue=_residual_norm)
    (z, act), gathered = _mm("mlp_up", n2, full["w_up"], "nn", (F32, BF16), b_by_device=True,
                             epilogue=lambda r: (r, jnp.square(jnp.maximum(r, 0.0))),
                             ride=_gather_exchange([shard16["w_down"]]))
    keep(("w_down",), gathered)
    dn = _mm("mlp_down", act, full["w_down"], "nn", F32, tk=4096)
    dh2, dh2_16, g_norm_f, loss_part = _final("loss_head", h1, dn, norm_f_g.reshape(1, d), tgt, tm)

    mx, my, mc = _place()
    c_arr = jnp.reshape(mc, (1,)).astype(jnp.int32)
    chip_arr = jnp.reshape(2 * mx + my, (1,)).astype(jnp.int32)
    sums, from_chips = {}, {}

    def blocks(group, grads):
        return [g if n in by_device else (_full_to_cols(g) if n in col_sharded else g.reshape((N_DEV,) + mats[n].shape))
                for n, g in zip(group, grads)]

    def pair_sums(group, mine, from_sibling):
        for n, g, r in zip(group, mine, from_sibling):
            sums[n] = _pair_sum("pair_sum_" + n, g, r, c_arr, 512)
        return [sums[n] for n in group]

    dz = _mm("mlp_down_dx", dh2_16, full["w_down"], "nt", BF16, extras=(z,),
             epilogue=lambda r, zz: (r * (2.0 * jnp.maximum(zz, 0.0)),))
    g_w_down = _mm("mlp_down_dw", act, dh2_16, "tn", BF16, tm=512, tn=d, tk=s)
    down_blocks = blocks(("w_down",), (g_w_down,))
    g_w_up, got_down = _mm("mlp_up_dw", n2, dz, "tn", BF16, tk=s, out_by_device=True,
                           ride=_sibling_exchange(down_blocks))
    dn2, got_up = _mm("mlp_up_dx", dz, full["w_up"], "nt", F32, tk=4096, b_by_device=True,
                      ride=_sibling_exchange([g_w_up]))
    mlp_sums = pair_sums(mlp, [g_w_up] + down_blocks, list(got_up) + list(got_down))
    dh1, dh1_16, g_norm_mlp = _rms_bwd("norm_mlp_bwd", dn2, h1, norm_mlp_g, dh2, tm, matmul_copy=True)
    assert lay["off"]["g_ret"] == 0 and lay["off"]["g_mla"] == d
    dy_ret, dy_mla, d_proj = _mm(
        "out_proj_dx", dh1_16, full["w_out"], "nt", (BF16, BF16, (BF16, 2, lay["total"])), tm=256, tn=d,
        extras=((proj, 0), (proj, 1), y_ret, y_mla), epilogue=_gate_grads)
    g_w_out = _mm("out_proj_dw", merged, dh1_16, "tn", BF16, tm=512, tn=d, tk=s)
    g_w_ret_o = _mm("ret_out_dw", ry, dy_ret, "tn", BF16, tm=512, tn=d, tk=s)
    g_w_mla_o = _mm("mla_out_dw", o_mla, dy_mla, "tn", BF16, tm=256, tn=d, tk=s)
    mixer = ("w_out", "w_ret_o", "w_mla_o")
    mixer_blocks = blocks(mixer, (g_w_out, g_w_ret_o, g_w_mla_o))
    d_ry, got = _mm("ret_out_dx", dy_ret, full["w_ret_o"], "nt", F32, ride=_sibling_exchange(mixer_blocks))
    mixer_sums = pair_sums(mixer, mixer_blocks, got)
    d_omla = _mm("mla_out_dx", dy_mla, full["w_mla_o"], "nt", F32)
    d_proj, g_ret_norm, got = _ret_bwd(proj, lay, cos_r, sin_r, lgs, states, d_ry, o_ret, ret_norm_g, d_proj, blk,
                                       ride=_chips_exchange(mixer_sums))
    from_chips.update(zip(mixer, got))
    delta = _attn_delta(d_omla, o_mla, lay, t_att)
    dqp, dkv, dkpe_parts, got = _attn_bwd(qf, kf, kv, lse, delta, d_omla, cos_p, sin_a, sin_b, lay, t_att,
                                          ride=_chips_exchange(mlp_sums))
    from_chips.update(zip(mlp, got))
    d_proj = _rope_key_grad(dkpe_parts, lay, d_proj, tm)
    d_cqn = _mm("q_up_dx", dqp, wq_pad, "nt", F32)
    g_wq_pad = _mm("q_up_dw", cqn, dqp, "tn", BF16)
    d_ckvn = _mm("kv_up_dx", dkv, full["w_kv_b"], "nt", F32, b_by_device=True)
    g_w_kv_b = _mm("kv_up_dw", ckvn, dkv, "tn", BF16, out_by_device=True)
    d_proj, g_q_a = _latent_norm_bwd("q_latent_bwd", proj, lay["off"]["c_q"], d_cqn, q_a_norm_g, d_proj, tm)
    d_proj, g_kv_a = _latent_norm_bwd("kv_latent_bwd", proj, lay["off"]["c_kv"], d_ckvn, kv_a_norm_g, d_proj, tm)
    g_w_q_b = g_wq_pad.reshape(ql, heads_m, QPAD)[:, :, :NOPE + ROPE].reshape(ql, heads_m * (NOPE + ROPE))
    latent = ("w_q_b", "w_kv_b")
    latent_blocks = blocks(latent, (g_w_q_b, g_w_kv_b))
    latent_sums = pair_sums(latent, latent_blocks,
                            _exchange_alone("latent_grads_to_sibling", _sibling_exchange(latent_blocks)))
    g_w_mine, got = _mm("in_proj_dw", d_proj, u, "tn", BF16, tm=768, tk=s, ride=_chips_exchange(latent_sums))
    from_chips.update(zip(latent, got))
    last = ("w_in",)
    last_blocks = [_mine_to_blocks(g_w_mine, lay)]
    last_sums = pair_sums(last, last_blocks, _exchange_alone("grads_to_sibling", _sibling_exchange(last_blocks)))
    du, got = _mm("in_proj_dx", d_proj, w_mine, "nn", F32, ride=_chips_exchange(last_sums))
    from_chips.update(zip(last, got))
    grad_x, g_norm_mix = _rms_bwd("norm_mix_bwd", du, xs, norm_mix_g, dh1, tm)

    upd = {n: _adamw_shard("adamw_" + n, mat_w[n], mat_m[n], mat_v[n], sums[n], from_chips[n], chip_arr, 256)
           for n in names if n != "w_in"}
    flip = lambda a: jnp.swapaxes(a, 1, 2)
    upd["w_in"] = [flip(o) for o in _adamw_shard(
        "adamw_w_in", flip(w_in), flip(m_w_in), flip(v_w_in), sums["w_in"], from_chips["w_in"], chip_arr, 512, tc=512)]

    gains = [("norm_mix_g", norm_mix_g, m_norm_mix_g, v_norm_mix_g, g_norm_mix),
             ("ret_norm_g", ret_norm_g, m_ret_norm_g, v_ret_norm_g, g_ret_norm),
             ("q_a_norm_g", q_a_norm_g, m_q_a_norm_g, v_q_a_norm_g, g_q_a),
             ("kv_a_norm_g", kv_a_norm_g, m_kv_a_norm_g, v_kv_a_norm_g, g_kv_a),
             ("norm_mlp_g", norm_mlp_g, m_norm_mlp_g, v_norm_mlp_g, g_norm_mlp),
             ("norm_f_g", norm_f_g, m_norm_f_g, v_norm_f_g, g_norm_f)]
    as_row = lambda a: a.reshape(1, -1)
    g_sm, d_sm, m_sm, v_sm, loss_row = _gains_all_reduce_adam(
        [g[4] for g in gains], loss_part, *[[as_row(g[k]) for g in gains] for k in (1, 2, 3)])
    loss = loss_row[0, 0]
    small = {g[0]: [a[r].reshape(g[1].shape) for a in (g_sm, d_sm, m_sm, v_sm)] for r, g in enumerate(gains)}

    order = ["norm_mix_g", "w_in", "ret_norm_g", "w_ret_o", "q_a_norm_g", "w_q_b", "kv_a_norm_g", "w_kv_b", "w_mla_o",
             "w_out", "norm_mlp_g", "w_up", "w_down", "norm_f_g"]
    outs = [loss, grad_x[None]]
    for k in range(4):
        for n in order:
            outs.append(small[n][k] if n in small else upd[n][k])
    return tuple(outs)
```

```python
import functools
import math

import jax
import jax.numpy as jnp
from jax import lax
from jax.experimental import pallas as pl
from jax.experimental.pallas import tpu as pltpu

F32 = jnp.float32
BF16 = jnp.bfloat16
MESH = pl.DeviceIdType.MESH

EPS = 1e-6
ROPE_THETA = 10000.0
CHUNK_SHIFT = 6
RET_QK = 128
RET_V = 256
NOPE = 128
ROPE = 64
VHEAD = 128
QPAD = 256
LANES = 128
N_DEV = 8
VMEM_LIMIT = 56 * 1024 * 1024

ADAM_LR = 0.001
ADAM_B1 = 0.9
ADAM_B2 = 0.999
ADAM_EPS = 1e-08
ADAM_WD = 0.01
ADAM_STEP = 10

NN = (((1,), (0,)), ((), ()))
NT = (((1,), (1,)), ((), ()))
TN = (((0,), (0,)), ((), ()))


def _dot(a, b, dims):
    return lax.dot_general(a.astype(BF16), b.astype(BF16), dims, preferred_element_type=F32)


def _tile(dim, pref):
    if dim <= pref:
        return dim
    t = (pref // LANES) * LANES
    while t >= LANES:
        if dim % t == 0:
            return t
        t -= LANES
    raise ValueError(f"no tile for {dim}")


def _params(sem):
    return pltpu.CompilerParams(dimension_semantics=sem, vmem_limit_bytes=VMEM_LIMIT)


def _sig(v):
    return 1.0 / (1.0 + jnp.exp(-v))


def _mm(name, a, b, mode, out_dtypes, *, tm=1024, tn=1024, tk=2048, extras=(), epilogue=None, ride=None,
        b_by_device=False, out_by_device=False):
    if b_by_device:
        b_cols = b.shape[2]
        b_shape = (b.shape[1], N_DEV * b_cols)
    else:
        b_shape = b.shape
    if mode == "nn":
        (m, k), (_, n) = a.shape, b_shape
    elif mode == "nt":
        (m, k), (n, _) = a.shape, b_shape
    else:
        (k, m), (_, n) = a.shape, b_shape
    tm, tn, tk = _tile(m, tm), _tile(n, tn), _tile(k, tk)
    if b_by_device and mode != "nt":
        tn = _tile(b_cols, tn)
    if out_by_device:
        tn = _tile(n // N_DEV, tn)
    nk = k // tk
    dims = {"nn": NN, "nt": NT, "tn": TN}[mode]
    a_spec = (pl.BlockSpec((tk, tm), lambda i, j, kk: (kk, i)) if mode == "tn"
              else pl.BlockSpec((tm, tk), lambda i, j, kk: (i, kk)))
    if b_by_device and mode == "nt":
        piece = min(tk, b_cols)
        n_b, per = tk // piece, b_cols // piece
        b_specs = [pl.BlockSpec((None, tn, piece),
                                lambda i, j, kk, p=p: ((kk * n_b + p) // per, j, (kk * n_b + p) % per))
                   for p in range(n_b)]
    elif b_by_device:
        per = b_cols // tn
        n_b, piece = 1, tk
        b_specs = [pl.BlockSpec((None, tk, tn), lambda i, j, kk: (j // per, kk, j % per))]
    else:
        n_b, piece = 1, tk
        b_specs = [pl.BlockSpec((tn, tk), lambda i, j, kk: (j, kk)) if mode == "nt"
                   else pl.BlockSpec((tk, tn), lambda i, j, kk: (kk, j))]
    tile_spec = pl.BlockSpec((tm, tn), lambda i, j, kk: (i, j))
    if out_by_device:
        per_out = n // N_DEV // tn
        out_spec = pl.BlockSpec((None, tm, tn), lambda i, j, kk: (j // per_out, i, j % per_out))
        out_dims = (N_DEV, m, n // N_DEV)
    else:
        out_spec, out_dims = tile_spec, (m, n)
    ex_arrays, ex_specs = [], []
    for e in extras:
        arr, off = e if isinstance(e, tuple) else (e, 0)
        ex_arrays.append(arr)
        if off == "whole":
            ex_specs.append(pl.BlockSpec(arr.shape, lambda i, j, kk, nd=arr.ndim: (0,) * nd))
        elif off is None:
            ex_specs.append(pl.BlockSpec((tm, arr.shape[1]), lambda i, j, kk: (i, 0)))
        else:
            ex_specs.append(pl.BlockSpec((tm, tn), lambda i, j, kk, off=off: (i, j + off)))
    n_ex = len(extras)
    single = not isinstance(out_dtypes, (tuple, list))
    dts = (out_dtypes,) if single else tuple(out_dtypes)
    out_specs, out_shapes = [], []
    for dt in dts:
        if isinstance(dt, tuple):
            dt, mult, width = dt
            out_specs.append(pl.BlockSpec((tm, mult * tn), lambda i, j, kk: (i, j)))
            out_shapes.append(jax.ShapeDtypeStruct((m, width), dt))
        else:
            out_specs.append(out_spec)
            out_shapes.append(jax.ShapeDtypeStruct(out_dims, dt))

    grid = (m // tm, n // tn, nk)
    r_in, r_out, r_sc = ride.counts() if ride else (0, 0, 0)
    n_acc = 1 if nk > 1 else 0

    def body(a_ref, *rest):
        b_refs, rest = rest[:n_b], rest[n_b:]
        ex, rest = rest[:n_ex], rest[n_ex:]
        ride_in, rest = rest[:r_in], rest[r_in:]
        outs, rest = rest[:len(dts)], rest[len(dts):]
        ride_out, rest = rest[:r_out], rest[r_out:]
        ride_scratch = rest[n_acc:]
        if ride:
            first, mid, last = _steps([pl.program_id(d) for d in range(3)], grid)
            ride.run(ride_in, ride_out, ride_scratch, (first, mid, None))

        def finish(r):
            vals = (r,) if epilogue is None else epilogue(r, *[e[...] for e in ex])
            for o, v in zip(outs, vals):
                o[...] = v.astype(o.dtype)

        if n_b == 1:
            part = _dot(a_ref[...], b_refs[0][...], dims)
        else:
            part = sum(_dot(a_ref[:, p * piece:(p + 1) * piece], b_refs[p][...], dims) for p in range(n_b))
        if nk == 1:
            finish(part)
        else:
            acc = rest[0]
            kk = pl.program_id(2)

            @pl.when(kk == 0)
            def _():
                acc[...] = part

            @pl.when(jnp.logical_and(kk > 0, kk < nk - 1))
            def _():
                acc[...] += part

            @pl.when(kk == nk - 1)
            def _():
                finish(acc[...] + part)

        if ride:
            ride.run(ride_in, ride_out, ride_scratch, (None, None, last))

    res = pl.pallas_call(
        body, name=name, grid=grid,
        in_specs=[a_spec] + b_specs + ex_specs + [ANY] * r_in,
        out_specs=out_specs + [ANY] * r_out,
        out_shape=out_shapes + (ride.out_shape if ride else []),
        scratch_shapes=([pltpu.VMEM((tm, tn), F32)] if nk > 1 else []) + (ride.scratch if ride else []),
        compiler_params=_params(("arbitrary",) * 3 if ride else ("parallel", "parallel", "arbitrary")),
    )(a, *[b] * n_b, *ex_arrays, *(ride.ins if ride else []))
    own = res[0] if single else res[:len(dts)]
    return (own, res[len(dts):]) if ride else own


def _rows(name, body, n_rows, tm, ins, outs, accs=(), into=None):
    in_specs, args = [], []
    for t in ins:
        if len(t) == 1:
            in_specs.append(pl.BlockSpec(t[0].shape, lambda i, nd=t[0].ndim: (0,) * nd))
        else:
            in_specs.append(pl.BlockSpec((tm, t[1]), lambda i, cb=t[2]: (i, cb)))
        args.append(t[0])
    outs = [(o + (o[0], 0))[:4] for o in outs]
    out_specs = [pl.BlockSpec((tm, w), lambda i, cb=cb: (i, cb)) for w, _, _, cb in outs]
    out_shape = [jax.ShapeDtypeStruct((n_rows, total), d) for _, d, total, _ in outs]
    aliases, kernel = {}, body
    if into is not None:
        arr, w, cb = into
        in_specs.append(ANY)
        args.append(arr)
        out_specs.append(pl.BlockSpec((tm, w), lambda i: (i, cb)))
        out_shape.append(jax.ShapeDtypeStruct(arr.shape, arr.dtype))
        aliases = {len(ins): len(outs)}
        n_in = len(ins)

        def kernel(*refs):
            body(*refs[:n_in], *refs[n_in + 1:])

    out_specs += [pl.BlockSpec((r, w), lambda i: (0, 0)) for r, w in accs]
    out_shape += [jax.ShapeDtypeStruct((r, w), F32) for r, w in accs]
    return pl.pallas_call(
        kernel, name=name, grid=(n_rows // tm,), in_specs=in_specs, out_specs=out_specs, out_shape=out_shape,
        input_output_aliases=aliases, compiler_params=_params(("arbitrary",) if accs else ("parallel",)),
    )(*args)


def _zero_first(*accs):
    @pl.when(pl.program_id(0) == 0)
    def _():
        for a in accs:
            a[...] = jnp.zeros_like(a)


def _rope64(t, cos, sin):
    return t * cos + pltpu.roll(t, RET_QK // 2, 1) * sin


def _rope32(t, cos, sin_a, sin_b):
    return t * cos + pltpu.roll(t, LANES - ROPE // 2, 1) * sin_a + pltpu.roll(t, ROPE // 2, 1) * sin_b


def _rms_fwd(name, x, g, tm):
    s, d = x.shape

    def body(x_ref, g_ref, u_ref):
        v = x_ref[...]
        r = lax.rsqrt(jnp.mean(v * v, axis=-1, keepdims=True) + EPS)
        u_ref[...] = (v * r * g_ref[...]).astype(BF16)

    return _rows(name, body, s, tm, [(x, d, 0), (g,)], [(d, BF16)])[0]


def _residual_norm(r, x, g):
    h = x + r
    return h, h * lax.rsqrt(jnp.mean(h * h, axis=-1, keepdims=True) + EPS) * g


def _gate_grads(dm, gr, gm, yr, ym):
    sr, sm = _sig(gr), _sig(gm)
    return dm * sr, dm * sm, jnp.concatenate([dm * yr * (sr * (1.0 - sr)), dm * ym * (sm * (1.0 - sm))], axis=1)


def _rms_bwd(name, dy, x, g, dres, tm, matmul_copy=False):
    s, d = x.shape

    def body(dy_ref, x_ref, g_ref, dres_ref, dx_ref, *rest):
        dg_ref = rest[-1]
        _zero_first(dg_ref)
        v, dyv = x_ref[...], dy_ref[...]
        r = lax.rsqrt(jnp.mean(v * v, axis=-1, keepdims=True) + EPS)
        xh = v * r
        dxh = dyv * g_ref[...]
        dx = dres_ref[...] + r * (dxh - xh * jnp.mean(dxh * xh, axis=-1, keepdims=True))
        dx_ref[...] = dx
        if matmul_copy:
            rest[0][...] = dx.astype(BF16)
        dg_ref[...] += jnp.sum(dyv * xh, axis=0, keepdims=True)

    return _rows(name, body, s, tm, [(dy, d, 0), (x, d, 0), (g,), (dres, d, 0)],
                 [(d, F32)] + [(d, BF16)] * matmul_copy, [(1, d)])


def _final(name, h1, dn, g, tgt, tm):
    s, d = h1.shape

    def body(h_ref, dn_ref, g_ref, t_ref, dh_ref, dh16_ref, dg_ref, loss_ref):
        _zero_first(dg_ref, loss_ref)
        v = h_ref[...] + dn_ref[...]
        r = lax.rsqrt(jnp.mean(v * v, axis=-1, keepdims=True) + EPS)
        xh = v * r
        gv = g_ref[...]
        e = xh * gv - t_ref[...]
        loss_ref[...] += 0.5 * jnp.sum(jnp.mean(e * e, axis=-1, keepdims=True))
        dy = e * (1.0 / d)
        dg_ref[...] += jnp.sum(dy * xh, axis=0, keepdims=True)
        dxh = dy * gv
        dh = r * (dxh - xh * jnp.mean(dxh * xh, axis=-1, keepdims=True))
        dh_ref[...] = dh
        dh16_ref[...] = dh.astype(BF16)

    return _rows(name, body, s, tm, [(h1, d, 0), (dn, d, 0), (g,), (tgt, d, 0)], [(d, F32), (d, BF16)],
                 [(1, d), (1, LANES)])


def _decay_mask(lg, blk):
    n = lax.broadcasted_iota(jnp.int32, (blk, blk), 0)
    m = lax.broadcasted_iota(jnp.int32, (blk, blk), 1)
    w = jnp.exp(lg * jnp.abs(n - m).astype(F32))
    return jnp.where(jnp.right_shift(m, CHUNK_SHIFT) <= jnp.right_shift(n, CHUNK_SHIFT), w, 0.0)


def _decays(lg, blk):
    pos = lax.broadcasted_iota(jnp.int32, (blk, 1), 0).astype(F32)
    return jnp.exp(lg * (pos + 1.0)), jnp.exp(lg * (blk - 1.0 - pos)), jnp.exp(lg * float(blk))


def _ret_fwd(proj, lay, cos, sin, lgs, gain, blk, ride=None):
    s = proj.shape[0]
    heads = lay["ret_heads"]
    nb = s // blk
    scale = RET_QK ** -0.5
    ride_in_specs, ride_ins, ride_out_specs, ride_out_shape, ride_scratch = _ride_args(ride)

    def body(lg_ref, qkv_ref, cos_ref, sin_ref, g_ref, o_ref, st_ref, ry_ref, state, mask):
        lg = lg_ref[0:1, 0:1]

        @pl.when(pl.program_id(1) == 0)
        def _():
            state[...] = jnp.zeros_like(state)
            mask[...] = _decay_mask(lg, blk)

        a, c, gb = _decays(lg, blk)
        q = _rope64(qkv_ref[:, :RET_QK], cos_ref[...], sin_ref[...])
        k = _rope64(qkv_ref[:, RET_QK:2 * RET_QK], cos_ref[...], sin_ref[...]) * scale
        v = qkv_ref[:, 2 * RET_QK:2 * RET_QK + RET_V]
        st = state[...]
        st_ref[...] = st
        sm = _dot(q, k, NT) * mask[...]
        o = _dot(sm, v, NN) + _dot(q * a, st, NN)
        o_ref[...] = o
        state[...] = st * gb + _dot(k * c, v, TN)
        dlt = o - jnp.mean(o, axis=-1, keepdims=True)
        rstd = lax.rsqrt(jnp.mean(dlt * dlt, axis=-1, keepdims=True) + EPS)
        rg = qkv_ref[:, 2 * RET_QK + RET_V:]
        ry_ref[...] = (dlt * rstd * g_ref[...] * (rg * _sig(rg))).astype(BF16)

    first = lay["off"]["heads"] // RET_HEAD
    res = pl.pallas_call(
        _with_ride(body, ride, (heads, nb), 0, 5, 3), name="ret_fwd", grid=(heads, nb),
        in_specs=[pl.BlockSpec((None, 8, LANES), lambda h, b: (h, 0, 0)),
                  pl.BlockSpec((blk, RET_HEAD), lambda h, b: (b, first + h)),
                  pl.BlockSpec((blk, LANES), lambda h, b: (b, 0)),
                  pl.BlockSpec((blk, LANES), lambda h, b: (b, 0)),
                  pl.BlockSpec((1, RET_V), lambda h, b: (0, h))] + ride_in_specs,
        out_specs=[pl.BlockSpec((blk, RET_V), lambda h, b: (b, h)),
                   pl.BlockSpec((None, None, RET_QK, RET_V), lambda h, b: (h, b, 0, 0)),
                   pl.BlockSpec((blk, RET_V), lambda h, b: (b, h))] + ride_out_specs,
        out_shape=[jax.ShapeDtypeStruct((s, heads * RET_V), F32),
                   jax.ShapeDtypeStruct((heads, nb, RET_QK, RET_V), F32),
                   jax.ShapeDtypeStruct((s, heads * RET_V), BF16)] + ride_out_shape,
        scratch_shapes=[pltpu.VMEM((RET_QK, RET_V), F32), pltpu.VMEM((blk, blk), F32)] + ride_scratch,
        compiler_params=_params(("arbitrary", "arbitrary") if ride else ("parallel", "arbitrary")),
    )(lgs, proj, cos, sin, gain, *ride_ins)
    return (res[0], res[1], res[2], res[3:]) if ride else res


def _ret_bwd(proj, lay, cos, sin, lgs, states, d_ry, o, gain, d_proj, blk, ride=None):
    ride_in_specs, ride_ins, ride_out_specs, ride_out_shape, ride_scratch = _ride_args(ride)
    s = proj.shape[0]
    heads = lay["ret_heads"]
    nb = s // blk
    scale = RET_QK ** -0.5

    def body(lg_ref, qkv_ref, cos_ref, sin_ref, st_ref, dry_ref, o_ref, g_ref, _, dqkv_ref, dg_ref, dstate, mask):
        lg = lg_ref[0:1, 0:1]

        @pl.when(pl.program_id(1) == 0)
        def _():
            dstate[...] = jnp.zeros_like(dstate)
            mask[...] = _decay_mask(lg, blk)
            dg_ref[...] = jnp.zeros_like(dg_ref)

        oh = o_ref[...]
        dlt = oh - jnp.mean(oh, axis=-1, keepdims=True)
        rstd = lax.rsqrt(jnp.mean(dlt * dlt, axis=-1, keepdims=True) + EPS)
        oh = dlt * rstd
        gv = g_ref[...]
        rg = qkv_ref[:, 2 * RET_QK + RET_V:]
        sg = _sig(rg)
        dry = dry_ref[...]
        dt = dry * (rg * sg)
        dqkv_ref[:, 2 * RET_QK + RET_V:] = (dry * (oh * gv) * (sg * (1.0 + rg * (1.0 - sg)))).astype(BF16)
        dg_ref[...] += jnp.sum(dt * oh, axis=0, keepdims=True)
        doh = dt * gv
        do = rstd * (doh - jnp.mean(doh, axis=-1, keepdims=True) - oh * jnp.mean(doh * oh, axis=-1, keepdims=True))

        a, c, gb = _decays(lg, blk)
        cs, sn = cos_ref[...], sin_ref[...]
        q = _rope64(qkv_ref[:, :RET_QK], cs, sn)
        k = _rope64(qkv_ref[:, RET_QK:2 * RET_QK], cs, sn) * scale
        v = qkv_ref[:, 2 * RET_QK:2 * RET_QK + RET_V]
        st = st_ref[...]
        dst = dstate[...]
        mk = mask[...]
        sm = _dot(q, k, NT) * mk
        ds = _dot(do, v, NT) * mk
        dq = _dot(ds, k, NN) + _dot(do, st, NT) * a
        dk = _dot(ds, q, TN) + _dot(v, dst, NT) * c
        dqkv_ref[:, 2 * RET_QK:2 * RET_QK + RET_V] = (_dot(sm, do, TN) + _dot(k * c, dst, NN)).astype(BF16)
        dstate[...] = dst * gb + _dot(q * a, do, TN)
        dqkv_ref[:, :RET_QK] = _rope64(dq, cs, -sn).astype(BF16)
        dqkv_ref[:, RET_QK:2 * RET_QK] = (_rope64(dk, cs, -sn) * scale).astype(BF16)

    first = lay["off"]["heads"] // RET_HEAD
    last = nb - 1
    head_tile = pl.BlockSpec((blk, RET_V), lambda h, b: (last - b, h))
    res = pl.pallas_call(
        _with_ride(body, ride, (heads, nb), 0, 9, 2), name="ret_bwd", grid=(heads, nb),
        in_specs=[pl.BlockSpec((None, 8, LANES), lambda h, b: (h, 0, 0)),
                  pl.BlockSpec((blk, RET_HEAD), lambda h, b: (last - b, first + h)),
                  pl.BlockSpec((blk, LANES), lambda h, b: (last - b, 0)),
                  pl.BlockSpec((blk, LANES), lambda h, b: (last - b, 0)),
                  pl.BlockSpec((None, None, RET_QK, RET_V), lambda h, b: (h, last - b, 0, 0)),
                  head_tile, head_tile, pl.BlockSpec((1, RET_V), lambda h, b: (0, h)), ANY] + ride_in_specs,
        out_specs=[pl.BlockSpec((blk, RET_HEAD), lambda h, b: (last - b, first + h)),
                   pl.BlockSpec((1, RET_V), lambda h, b: (0, h))] + ride_out_specs,
        out_shape=[jax.ShapeDtypeStruct(d_proj.shape, d_proj.dtype),
                   jax.ShapeDtypeStruct((1, heads * RET_V), F32)] + ride_out_shape,
        scratch_shapes=[pltpu.VMEM((RET_QK, RET_V), F32), pltpu.VMEM((blk, blk), F32)] + ride_scratch,
        input_output_aliases={8: 0},
        compiler_params=_params(("arbitrary", "arbitrary") if ride else ("parallel", "arbitrary")),
    )(lgs, proj, cos, sin, states, d_ry, o, gain, d_proj, *ride_ins)
    return (res[0], res[1], res[2:]) if ride else res[:2]


def _mla_prep(proj, lay, gq, gkv, cos, sin_a, sin_b, tm):
    s = proj.shape[0]
    ql, kl = lay["q_lora"], lay["kv_lora"]

    def body(cq_ref, ckv_ref, kpe_ref, gq_ref, gkv_ref, cos_ref, sa_ref, sb_ref, cqn_ref, ckvn_ref, kpr_ref):
        for src, gref, dst in ((cq_ref, gq_ref, cqn_ref), (ckv_ref, gkv_ref, ckvn_ref)):
            v = src[...]
            r = lax.rsqrt(jnp.mean(v * v, axis=-1, keepdims=True) + EPS)
            dst[...] = (v * r * gref[...]).astype(BF16)
        kpr_ref[...] = _rope32(kpe_ref[...], cos_ref[...], sa_ref[...], sb_ref[...]).astype(BF16)

    off = lay["off"]
    return _rows("mla_prep", body, s, tm,
                 [(proj, ql, off["c_q"] // ql), (proj, kl, off["c_kv"] // kl), (proj, LANES, off["k_pe"] // LANES),
                  (gq,), (gkv,), (cos, LANES, 0), (sin_a, LANES, 0), (sin_b, LANES, 0)],
                 [(ql, BF16), (kl, BF16), (LANES, BF16)])


def _latent_norm_bwd(name, proj, offset, d_normed, g, d_proj, tm):
    s, w = d_normed.shape

    def body(dy_ref, x_ref, g_ref, dx_ref, dg_ref):
        _zero_first(dg_ref)
        v, dy = x_ref[...], dy_ref[...]
        r = lax.rsqrt(jnp.mean(v * v, axis=-1, keepdims=True) + EPS)
        xh = v * r
        dxh = dy * g_ref[...]
        dx_ref[...] = (r * (dxh - xh * jnp.mean(dxh * xh, axis=-1, keepdims=True))).astype(BF16)
        dg_ref[...] += jnp.sum(dy * xh, axis=0, keepdims=True)

    return _rows(name, body, s, tm, [(d_normed, w, 0), (proj, w, offset // w), (g,)], [], [(1, w)],
                 into=(d_proj, w, offset // w))


def _q_operand(r, cos, sin_a, sin_b):
    qs = (NOPE + ROPE) ** -0.5 * math.log2(math.e)
    cs, sa, sb = cos * qs, sin_a * qs, sin_b * qs
    parts = []
    for lo in range(0, r.shape[1], QPAD):
        parts += [r[:, lo:lo + NOPE] * qs, _rope32(r[:, lo + NOPE:lo + QPAD], cs, sa, sb)]
    return (jnp.concatenate(parts, axis=1),)


def _k_operand(r, kpr):
    parts = []
    for lo in range(0, r.shape[1], QPAD):
        parts += [r[:, lo:lo + NOPE], kpr.astype(F32)]
    return r, jnp.concatenate(parts, axis=1)


def _rope_key_grad(parts, lay, d_proj, tm):
    s, w = parts.shape

    def body(p_ref, dkpe_ref):
        dkpe_ref[:, :LANES] = sum(p_ref[:, lo:lo + LANES] for lo in range(0, w, LANES)).astype(BF16)
        dkpe_ref[:, LANES:] = jnp.zeros((tm, LANES), BF16)

    return _rows("rope_key_grad", body, s, tm, [(parts, w, 0)], [],
                 into=(d_proj, 2 * LANES, lay["off"]["k_pe"] // (2 * LANES)))[0]


def _diag_mask(t, keys_on_rows=False):
    row = lax.broadcasted_iota(jnp.int32, (t, t), 0)
    col = lax.broadcasted_iota(jnp.int32, (t, t), 1)
    key, query = (row, col) if keys_on_rows else (col, row)
    return jnp.right_shift(key, CHUNK_SHIFT) <= jnp.right_shift(query, CHUNK_SHIFT)


def _tile_pairs(nt, by_key):
    if by_key:
        pairs = [(i, j) for j in range(nt) for i in range(j, nt)]
    else:
        pairs = [(i, j) for i in range(nt) for j in range(i + 1)]
    return (jnp.asarray([p[0] for p in pairs], jnp.int32), jnp.asarray([p[1] for p in pairs], jnp.int32))


def _head_block(heads):
    return 4 if heads % 4 == 0 else 2 if heads % 2 == 0 else 1


def _attn_fwd(qf, kf, kv, lay, t, ride=None):
    s = qf.shape[0]
    heads = lay["mla_heads"]
    hb = 8 if heads % 8 == 0 else _head_block(heads)
    nt = s // t
    qi, kj = _tile_pairs(nt, False)
    grid = (heads // hb, int(qi.shape[0]))
    ride_in_specs, ride_ins, ride_out_specs, ride_out_shape, ride_scratch = _ride_args(ride)

    def body(qi_ref, kj_ref, q_ref, k_ref, kv_ref, o_ref, lse_ref, m_s, l_s, acc):
        p = pl.program_id(1)
        i, j = qi_ref[p], kj_ref[p]

        @pl.when(j == 0)
        def _():
            m_s[...] = jnp.full_like(m_s, -jnp.inf)
            l_s[...] = jnp.zeros_like(l_s)
            acc[...] = jnp.zeros_like(acc)

        def step(diagonal):
            ones = jnp.ones((t, LANES), BF16)
            scores = [_dot(q_ref[:, hh * QPAD:(hh + 1) * QPAD], k_ref[:, hh * QPAD:(hh + 1) * QPAD], NT)
                      for hh in range(hb)]
            for hh in range(hb):
                sc = scores[hh]
                if diagonal:
                    sc = jnp.where(_diag_mask(t), sc, -jnp.inf)
                cols = [sc[:, c * LANES:(c + 1) * LANES] for c in range(t // LANES)]
                m_old = m_s[hh]
                m_new = jnp.maximum(m_old, jnp.max(functools.reduce(jnp.maximum, cols), axis=-1, keepdims=True))
                alpha = jnp.exp2(m_old - m_new)
                pr = jnp.concatenate([jnp.exp2(c - m_new).astype(BF16) for c in cols], axis=1)
                pv = _dot(pr, jnp.concatenate([kv_ref[:, hh * QPAD + NOPE:(hh + 1) * QPAD], ones], axis=1), NN)
                l_new = alpha * l_s[hh] + pv[:, VHEAD:]
                a_new = alpha * acc[hh] + pv[:, :VHEAD]
                if diagonal:
                    o_ref[:, hh * VHEAD:(hh + 1) * VHEAD] = a_new / l_new
                    lse_ref[hh] = jnp.transpose(m_new + jnp.log2(l_new))[:1]
                else:
                    m_s[hh], l_s[hh], acc[hh] = m_new, l_new, a_new

        pl.when(j < i)(functools.partial(step, False))
        pl.when(j == i)(functools.partial(step, True))

    res = pl.pallas_call(
        _with_ride(body, ride, grid, 2, 3, 2), name="attn_fwd",
        grid_spec=pltpu.PrefetchScalarGridSpec(
            num_scalar_prefetch=2, grid=grid,
            in_specs=[pl.BlockSpec((t, hb * QPAD), lambda h, p, qi, kj: (qi[p], h)),
                      pl.BlockSpec((t, hb * QPAD), lambda h, p, qi, kj: (kj[p], h)),
                      pl.BlockSpec((t, hb * QPAD), lambda h, p, qi, kj: (kj[p], h))] + ride_in_specs,
            out_specs=[pl.BlockSpec((t, hb * VHEAD), lambda h, p, qi, kj: (qi[p], h)),
                       pl.BlockSpec((hb, 1, t), lambda h, p, qi, kj: (h, 0, qi[p]))] + ride_out_specs,
            scratch_shapes=[pltpu.VMEM((hb, t, LANES), F32), pltpu.VMEM((hb, t, LANES), F32),
                            pltpu.VMEM((hb, t, VHEAD), F32)] + ride_scratch),
        out_shape=[jax.ShapeDtypeStruct((s, heads * VHEAD), F32),
                   jax.ShapeDtypeStruct((heads, 1, s), F32)] + ride_out_shape,
        compiler_params=_params(("arbitrary", "arbitrary") if ride else ("parallel", "arbitrary")),
    )(qi, kj, qf, kf, kv, *ride_ins)
    return (res[0], res[1], res[2:]) if ride else res


def _attn_delta(d_o, o, lay, tm):
    s = o.shape[0]
    heads = lay["mla_heads"]

    def body(do_ref, o_ref, dl_ref):
        for h in range(heads):
            sl = slice(h * VHEAD, (h + 1) * VHEAD)
            dl_ref[h] = jnp.sum(jnp.transpose(do_ref[:, sl] * o_ref[:, sl]), axis=0, keepdims=True)

    tile = pl.BlockSpec((tm, heads * VHEAD), lambda i: (i, 0))
    return pl.pallas_call(
        body, name="attn_delta", grid=(s // tm,), in_specs=[tile, tile],
        out_specs=pl.BlockSpec((heads, 1, tm), lambda i: (0, 0, i)),
        out_shape=jax.ShapeDtypeStruct((heads, 1, s), F32),
        compiler_params=_params(("parallel",)),
    )(d_o, o)


def _attn_bwd(qf, kf, kv, lse, delta, d_o, cos, sin_a, sin_b, lay, t, ride=None):
    s = qf.shape[0]
    heads = lay["mla_heads"]
    hb = _head_block(heads)
    nt = s // t
    scale = (NOPE + ROPE) ** -0.5
    qi, kj = _tile_pairs(nt, True)
    grid = (heads // hb, int(qi.shape[0]))
    ride_in_specs, ride_ins, ride_out_specs, ride_out_shape, ride_scratch = _ride_args(ride)

    def body(qi_ref, kj_ref, q_ref, k_ref, kv_ref, lse_ref, dl_ref, do_ref, cos_ref, sa_ref, sb_ref,
             dqp_ref, dkv_ref, dkpe_ref, dq_acc, dk_acc, dv_acc):
        p = pl.program_id(1)
        i, j = qi_ref[p], kj_ref[p]
        rows = pl.ds(pl.multiple_of(i * t, t), t)

        def unrope(v):
            return _rope32(v, cos_ref[...], -sa_ref[...], -sb_ref[...])

        @pl.when(p == 0)
        def _():
            dq_acc[...] = jnp.zeros_like(dq_acc)

        def step(diagonal):
            for hh in range(hb):
                lo = hh * QPAD
                q, k = q_ref[:, lo:lo + QPAD], k_ref[:, lo:lo + QPAD]
                do = do_ref[:, hh * VHEAD:(hh + 1) * VHEAD]
                pr = jnp.exp2(_dot(k, q, NT) - lse_ref[hh])
                if diagonal:
                    pr = jnp.where(_diag_mask(t, keys_on_rows=True), pr, 0.0)
                dv_part = _dot(pr, do, NN)
                ds = (pr * (_dot(kv_ref[:, lo + NOPE:lo + QPAD], do, NT) - dl_ref[hh])).astype(BF16)
                dk_part = _dot(ds, q, NN)
                dq = dq_acc[rows, lo:lo + QPAD] + _dot(ds, k, TN) * scale
                if diagonal:
                    dqp_ref[:, lo:lo + NOPE] = dq[:, :NOPE].astype(BF16)
                    dqp_ref[:, lo + NOPE:lo + QPAD] = unrope(dq[:, NOPE:]).astype(BF16)
                    dk_acc[hh], dv_acc[hh] = dk_part, dv_part
                else:
                    dq_acc[rows, lo:lo + QPAD] = dq
                    dk_acc[hh] += dk_part
                    dv_acc[hh] += dv_part

        pl.when(i > j)(functools.partial(step, False))
        pl.when(i == j)(functools.partial(step, True))

        @pl.when(i == nt - 1)
        def _():
            kpe = jnp.zeros((t, LANES), F32)
            for hh in range(hb):
                lo = hh * QPAD
                dk = dk_acc[hh] * math.log(2.0)
                dkv_ref[:, lo:lo + NOPE] = dk[:, :NOPE].astype(BF16)
                dkv_ref[:, lo + NOPE:lo + QPAD] = dv_acc[hh].astype(BF16)
                kpe = kpe + dk[:, NOPE:]
            dkpe_ref[...] = unrope(kpe)

    table = pl.BlockSpec((t, LANES), lambda h, p, qi, kj: (kj[p], 0))
    res = pl.pallas_call(
        _with_ride(body, ride, grid, 2, 9, 3), name="attn_bwd",
        grid_spec=pltpu.PrefetchScalarGridSpec(
            num_scalar_prefetch=2, grid=grid,
            in_specs=[pl.BlockSpec((t, hb * QPAD), lambda h, p, qi, kj: (qi[p], h)),
                      pl.BlockSpec((t, hb * QPAD), lambda h, p, qi, kj: (kj[p], h)),
                      pl.BlockSpec((t, hb * QPAD), lambda h, p, qi, kj: (kj[p], h)),
                      pl.BlockSpec((hb, 1, t), lambda h, p, qi, kj: (h, 0, qi[p])),
                      pl.BlockSpec((hb, 1, t), lambda h, p, qi, kj: (h, 0, qi[p])),
                      pl.BlockSpec((t, hb * VHEAD), lambda h, p, qi, kj: (qi[p], h)),
                      table, table, table] + ride_in_specs,
            out_specs=[pl.BlockSpec((t, hb * QPAD), lambda h, p, qi, kj: (kj[p], h)),
                       pl.BlockSpec((t, hb * QPAD), lambda h, p, qi, kj: (kj[p], h)),
                       pl.BlockSpec((t, LANES), lambda h, p, qi, kj: (kj[p], h))] + ride_out_specs,
            scratch_shapes=[pltpu.VMEM((s, hb * QPAD), F32), pltpu.VMEM((hb, t, QPAD), F32),
                            pltpu.VMEM((hb, t, VHEAD), F32)] + ride_scratch),
        out_shape=[jax.ShapeDtypeStruct((s, heads * QPAD), BF16),
                   jax.ShapeDtypeStruct((s, heads * QPAD), BF16),
                   jax.ShapeDtypeStruct((s, heads // hb * LANES), F32)] + ride_out_shape,
        compiler_params=_params(("arbitrary", "arbitrary") if ride else ("parallel", "arbitrary")),
    )(qi, kj, qf, kf, kv, lse, delta, d_o, cos, sin_a, sin_b, *ride_ins)
    return (res[0], res[1], res[2], res[3:]) if ride else res


ANY = pl.BlockSpec(memory_space=pl.ANY)


def _place():
    return lax.axis_index("x"), lax.axis_index("y"), lax.axis_index("c")


def _other_chips(x, y):
    return [(1 - x, y), (x, 1 - y), (1 - x, 1 - y)]


class _Exchange:
    def __init__(self, ins, out_shape, scratch, phases):
        self.ins, self.out_shape, self.scratch, self.phases = list(ins), list(out_shape), list(scratch), phases

    def counts(self):
        return len(self.ins), len(self.out_shape), len(self.scratch)

    def run(self, r_in, r_out, r_scratch, conds):
        for cond, phase in zip(conds, self.phases):
            if phase is not None and cond is not None:
                pl.when(cond)(functools.partial(phase, r_in, r_out, r_scratch))


def _steps(ids, sizes):
    lin, total = 0, 1
    for i, n in zip(ids, sizes):
        lin, total = lin * n + i, total * n
    return lin == 0, lin == total // 2, lin == total - 1


def _ride_args(ride):
    if ride is None:
        return [], [], [], [], []
    n_in, n_out, _ = ride.counts()
    return [ANY] * n_in, ride.ins, [ANY] * n_out, ride.out_shape, ride.scratch


def _with_ride(body, ride, grid, n_prefetch, n_in, n_out):
    if ride is None:
        return body
    r_in, r_out, r_sc = ride.counts()

    def hosted(*refs):
        cuts = (n_prefetch, n_in, r_in, n_out, r_out)
        parts, pos = [], 0
        for n in cuts:
            parts.append(refs[pos:pos + n])
            pos += n
        pre, ins, ride_in, outs, ride_out = parts
        scratch, ride_scratch = refs[pos:len(refs) - r_sc], refs[len(refs) - r_sc:]
        first, mid, last = _steps([pl.program_id(d) for d in range(len(grid))], grid)
        ride.run(ride_in, ride_out, ride_scratch, (first, mid, None))
        body(*pre, *ins, *outs, *scratch)
        ride.run(ride_in, ride_out, ride_scratch, (None, None, last))

    return hosted


def _exchange_alone(name, ex):
    n_in, n_out, _ = ex.counts()

    def body(*refs):
        for phase in ex.phases:
            if phase is not None:
                phase(refs[:n_in], refs[n_in:n_in + n_out], refs[n_in + n_out:])

    return pl.pallas_call(
        body, name=name, in_specs=[ANY] * n_in, out_specs=[ANY] * n_out, out_shape=ex.out_shape,
        scratch_shapes=ex.scratch)(*ex.ins)


def _gather_exchange(shards):
    nw = len(shards)

    def parts(ins, outs, sems):
        send_sems, recv_sems, local_sems = sems
        x, y, c = _place()

        def slot(px, py, pc):
            return 4 * px + 2 * py + pc

        def copy(w, k, rows, to, src=None):
            return pltpu.make_async_remote_copy(
                src_ref=rows if src is None else src, dst_ref=rows, send_sem=send_sems.at[w, k],
                recv_sem=recv_sems.at[w, k], device_id=to, device_id_type=MESH)

        def halves(w):
            rows = shards[w].shape[0]
            cut = -(-(rows // 2) // 16) * 16
            return pl.ds(0, cut), pl.ds(cut, rows - cut)

        def plan(w, mine):
            side = c if mine else 1 - c
            upper, lower = halves(w)
            whole = lambda px, py: outs[w].at[slot(px, py, side)]
            top = lambda px, py: outs[w].at[slot(px, py, side), upper]
            bottom = lambda px, py: outs[w].at[slot(px, py, side), lower]
            xn, yn, sib = (1 - x, y, side), (x, 1 - y, side), (x, y, 1 - side)
            own = ins[w] if mine else None
            return [copy(w, 0, whole(x, y), sib, own), copy(w, 1, whole(x, y), xn, own),
                    copy(w, 2, whole(x, y), yn, own), copy(w, 3, top(1 - x, y), yn), copy(w, 4, bottom(x, 1 - y), xn),
                    copy(w, 5, whole(1 - x, y), sib), copy(w, 6, whole(x, 1 - y), sib),
                    copy(w, 7, top(1 - x, 1 - y), sib), copy(w, 8, bottom(1 - x, 1 - y), sib)]

        def arrivals(w):
            upper, lower = halves(w)
            at = lambda px, py, *rows: outs[w].at[(slot(px, py, c),) + rows]
            return {1: copy(w, 1, at(1 - x, y), (x, y, c)), 2: copy(w, 2, at(x, 1 - y), (x, y, c)),
                    3: copy(w, 3, at(1 - x, 1 - y, upper), (x, y, c)),
                    4: copy(w, 4, at(1 - x, 1 - y, lower), (x, y, c))}

        local = [pltpu.make_async_copy(ins[w], outs[w].at[slot(x, y, c)], local_sems.at[w]) for w in range(nw)]
        return plan, arrivals, local

    def start(ins, outs, sems):
        plan, _, local = parts(ins, outs, sems)
        for cp in local:
            cp.start()
        for w in range(nw):
            for k in (0, 1, 2):
                plan(w, True)[k].start()

    def middle(ins, outs, sems):
        plan, arrivals, _ = parts(ins, outs, sems)
        for landed, onward in ((1, (3, 5)), (2, (4, 6))):
            for w in range(nw):
                arrivals(w)[landed].wait_recv()
                for k in onward:
                    plan(w, True)[k].start()

    def finish(ins, outs, sems):
        plan, arrivals, local = parts(ins, outs, sems)
        for landed, onward in ((3, 7), (4, 8)):
            for w in range(nw):
                arrivals(w)[landed].wait_recv()
                plan(w, True)[onward].start()
        for w in range(nw):
            from_sibling = plan(w, False)
            for k in (0, 5, 6, 7, 8):
                from_sibling[k].wait_recv()
            for cp in plan(w, True):
                cp.wait_send()
        for cp in local:
            cp.wait()

    return _Exchange(
        shards, [jax.ShapeDtypeStruct((N_DEV,) + s.shape, s.dtype) for s in shards],
        [pltpu.SemaphoreType.DMA((nw, 9)), pltpu.SemaphoreType.DMA((nw, 9)), pltpu.SemaphoreType.DMA((nw,))],
        (start, middle, finish))


def _sibling_exchange(grads):
    nw = len(grads)

    def copies(ins, outs, sems):
        x, y, c = _place()
        return [pltpu.make_async_remote_copy(
            src_ref=ins[w].at[2 * p + (1 - c)], dst_ref=outs[w].at[p], send_sem=sems[0].at[w, p],
            recv_sem=sems[1].at[w, p], device_id=(x, y, 1 - c), device_id_type=MESH)
            for w in range(nw) for p in range(4)]

    def start(ins, outs, sems):
        for cp in copies(ins, outs, sems):
            cp.start()

    def finish(ins, outs, sems):
        for cp in copies(ins, outs, sems):
            cp.wait()

    return _Exchange(grads, [jax.ShapeDtypeStruct((4,) + g.shape[1:], g.dtype) for g in grads],
                     [pltpu.SemaphoreType.DMA((nw, 4)), pltpu.SemaphoreType.DMA((nw, 4))], (start, None, finish))


def _chips_exchange(sums):
    nw = len(sums)

    def copies(ins, outs, sems):
        x, y, c = _place()
        return [pltpu.make_async_remote_copy(
            src_ref=ins[w].at[2 * px + py], dst_ref=outs[w].at[k], send_sem=sems[0].at[w, k],
            recv_sem=sems[1].at[w, k], device_id=(px, py, c), device_id_type=MESH)
            for w in range(nw) for k, (px, py) in enumerate(_other_chips(x, y))]

    def start(ins, outs, sems):
        for cp in copies(ins, outs, sems):
            cp.start()

    def finish(ins, outs, sems):
        for cp in copies(ins, outs, sems):
            cp.wait()

    return _Exchange(sums, [jax.ShapeDtypeStruct((3,) + g.shape[1:], g.dtype) for g in sums],
                     [pltpu.SemaphoreType.DMA((nw, 3)), pltpu.SemaphoreType.DMA((nw, 3))], (start, None, finish))


def _pair_sum(name, g, got, c_arr, tr):
    _, rows, cols = g.shape
    tr = _tile_rows(rows, tr)

    def body(c_ref, a_ref, b_ref, o_ref):
        o_ref[...] = (a_ref[...].astype(F32) + b_ref[...].astype(F32)).astype(BF16)

    return pl.pallas_call(
        body, name=name,
        grid_spec=pltpu.PrefetchScalarGridSpec(
            num_scalar_prefetch=1, grid=(4, rows // tr),
            in_specs=[pl.BlockSpec((None, tr, cols), lambda p, r, cr: (2 * p + cr[0], r, 0)),
                      pl.BlockSpec((None, tr, cols), lambda p, r, cr: (p, r, 0))],
            out_specs=pl.BlockSpec((None, tr, cols), lambda p, r, cr: (p, r, 0))),
        out_shape=jax.ShapeDtypeStruct((4, rows, cols), BF16),
        compiler_params=_params(("parallel", "parallel")),
    )(c_arr, g, got)


def _tile_rows(rows, pref):
    t = min(rows, pref)
    while rows % t or t % 8:
        t -= 1
    return t


def _adam(w, g, m, v):
    m = ADAM_B1 * m + (1.0 - ADAM_B1) * g
    v = ADAM_B2 * v + (1.0 - ADAM_B2) * (g * g)
    m_hat = m / (1.0 - ADAM_B1 ** ADAM_STEP)
    v_hat = v / (1.0 - ADAM_B2 ** ADAM_STEP)
    return -ADAM_LR * (m_hat / (jnp.sqrt(v_hat) + ADAM_EPS) + ADAM_WD * w), m, v


def _adamw_shard(name, w, m, v, sums, got, chip_arr, tr, tc=None):
    _, rows, cols = w.shape
    tr = _tile_rows(rows, tr)
    tc = cols if tc is None else _tile(cols, tc)

    def body(p_ref, w_ref, m_ref, v_ref, s_ref, r_ref, g_ref, d_ref, nm_ref, nv_ref):
        g = s_ref[...].astype(F32)
        for k in range(3):
            g = g + r_ref[k].astype(F32)
        g_ref[...] = g
        d_ref[...], nm_ref[...], nv_ref[...] = _adam(w_ref[...], g, m_ref[...], v_ref[...])

    tile = pl.BlockSpec((None, tr, tc), lambda r, q, pr: (0, r, q))
    return pl.pallas_call(
        body, name=name,
        grid_spec=pltpu.PrefetchScalarGridSpec(
            num_scalar_prefetch=1, grid=(rows // tr, cols // tc),
            in_specs=[tile, tile, tile,
                      pl.BlockSpec((None, tr, tc), lambda r, q, pr: (pr[0], r, q)),
                      pl.BlockSpec((3, tr, tc), lambda r, q, pr: (0, r, q))],
            out_specs=[tile] * 4),
        out_shape=[jax.ShapeDtypeStruct((1, rows, cols), F32)] * 4,
        compiler_params=_params(("parallel", "parallel")),
    )(chip_arr, w, m, v, sums, got)


def _gains_all_reduce_adam(grads, loss_part, ws, ms, vs):
    n = len(grads)
    widths = [g.shape[1] for g in grads]
    rows, width = -(-(n + 1) // 8) * 8, max(widths)

    def body(*refs):
        g_in, loss_in = refs[:n], refs[n]
        w_in, m_in, v_in = (refs[1 + k * n:1 + (k + 1) * n] for k in (1, 2, 3))
        outs = refs[1 + 4 * n:2 + 8 * n]
        g_out, d_out, m_out, v_out = (outs[k * n:(k + 1) * n] for k in range(4))
        loss_out = outs[4 * n]
        buf, send_sems, recv_sems = refs[2 + 8 * n:]
        x, y, c = _place()
        me = 4 * x + 2 * y + c
        buf[me] = jnp.zeros((rows, width), F32)
        for r in range(n):
            buf[me, r:r + 1, :widths[r]] = g_in[r][...]
        buf[me, n:n + 1, :LANES] = loss_in[...]
        peers = [(x, y, 1 - c)] + [(px, py, pc) for px, py in _other_chips(x, y) for pc in (c, 1 - c)]
        copies = []
        for k, peer in enumerate(peers):
            cp = pltpu.make_async_remote_copy(
                src_ref=buf.at[me], dst_ref=buf.at[me], send_sem=send_sems.at[k], recv_sem=recv_sems.at[k],
                device_id=peer, device_id_type=MESH)
            cp.start()
            copies.append(cp)
        for cp in copies:
            cp.wait()
        total = buf[0]
        for k in range(1, N_DEV):
            total = total + buf[k]
        for r in range(n):
            g = total[r:r + 1, :widths[r]]
            g_out[r][...] = g
            d_out[r][...], m_out[r][...], v_out[r][...] = _adam(w_in[r][...], g, m_in[r][...], v_in[r][...])
        loss_out[...] = total[n:n + 1, :LANES]

    vm = pl.BlockSpec(memory_space=pltpu.VMEM)
    shapes = [jax.ShapeDtypeStruct((1, w), F32) for w in widths]
    res = pl.pallas_call(
        body, name="gains_all_reduce_adamw",
        in_specs=[vm] * (4 * n + 1), out_specs=[vm] * (4 * n + 1),
        out_shape=shapes * 4 + [jax.ShapeDtypeStruct((1, LANES), F32)],
        scratch_shapes=[pltpu.VMEM((N_DEV, rows, width), F32), pltpu.SemaphoreType.DMA((7,)),
                        pltpu.SemaphoreType.DMA((7,))],
    )(*grads, loss_part, *ws, *ms, *vs)
    return res[:n], res[n:2 * n], res[2 * n:3 * n], res[3 * n:4 * n], res[4 * n]


IN_ORDER = ("r_q", "r_k", "r_v", "r_g", "c_q", "c_kv", "k_pe", "g_ret", "g_mla")
RET_HEAD = 2 * RET_QK + 2 * RET_V


def _make_layout(d, vw, qw, ql, kl, mla_w):
    width = {"r_q": qw, "r_k": qw, "r_v": vw, "r_g": vw, "c_q": ql, "c_kv": kl, "k_pe": ROPE, "g_ret": d, "g_mla": d}
    src, o = {}, 0
    for n in IN_ORDER:
        src[n] = o
        o += width[n]
    heads = vw // RET_V
    off, pieces, o = {}, [], 0

    def put(name, w, s):
        nonlocal o
        off.setdefault(name, o)
        pieces.append((o, w, s))
        o += w

    for n in ("g_ret", "g_mla", "c_q"):
        put(n, width[n], src[n])
    for h in range(heads):
        put("heads", RET_QK, src["r_q"] + h * RET_QK)
        put("heads", RET_QK, src["r_k"] + h * RET_QK)
        put("heads", RET_V, src["r_v"] + h * RET_V)
        put("heads", RET_V, src["r_g"] + h * RET_V)
    for n in ("c_kv", "k_pe"):
        put(n, width[n], src[n])
    total = off["k_pe"] + 2 * LANES
    for n, blk in (("g_ret", d), ("g_mla", d), ("c_q", ql), ("heads", RET_HEAD), ("c_kv", kl), ("k_pe", 2 * LANES)):
        assert off[n] % blk == 0
    return {"off": off, "pieces": pieces, "total": total, "n_in": sum(width.values()),
            "ret_heads": heads, "mla_heads": mla_w // VHEAD, "q_lora": ql, "kv_lora": kl}


def _cols_to_full(g):
    n, r, c = g.shape
    return jnp.transpose(g, (1, 0, 2)).reshape(r, n * c)


def _full_to_cols(w):
    r, c = w.shape
    return jnp.transpose(w.reshape(r, N_DEV, c // N_DEV), (1, 0, 2))


def _w_in_to_mine(g, lay):
    _, cols, d = g.shape
    parts, at = [], 0
    for o, w, s in lay["pieces"]:
        if o > at:
            parts.append(jnp.zeros((o - at, d), g.dtype))
        while w > 0:
            k, a = divmod(s, cols)
            take = min(w, cols - a)
            parts.append(g[k, a:a + take])
            s, w, o = s + take, w - take, o + take
        at = o
    parts.append(jnp.zeros((lay["total"] - at, d), g.dtype))
    return jnp.concatenate(parts, axis=0)


def _mine_to_blocks(g, lay):
    cols = lay["n_in"] // N_DEV
    by_src = sorted(lay["pieces"], key=lambda p: p[2])
    blocks = []
    for k in range(N_DEV):
        lo, hi, parts = k * cols, (k + 1) * cols, []
        for o, w, s in by_src:
            a, b = max(lo, s), min(hi, s + w)
            if a < b:
                parts.append(g[o + a - s:o + b - s])
        blocks.append(jnp.concatenate(parts, axis=0))
    return jnp.stack(blocks)


def _rope_tables(positions, half):
    inv = ROPE_THETA ** (-jnp.arange(half, dtype=F32) / half)
    ang = positions.astype(F32)[:, None] * inv
    return jnp.cos(ang), jnp.sin(ang)


def kernel(x, positions, norm_mix_g, w_in, ret_norm_g, w_ret_o, q_a_norm_g, w_q_b, kv_a_norm_g, w_kv_b, w_mla_o, w_out, norm_mlp_g, w_up, w_down, norm_f_g, loss_target, m_norm_mix_g, m_w_in, m_ret_norm_g, m_w_ret_o, m_q_a_norm_g, m_w_q_b, m_kv_a_norm_g, m_w_kv_b, m_w_mla_o, m_w_out, m_norm_mlp_g, m_w_up, m_w_down, m_norm_f_g, v_norm_mix_g, v_w_in, v_ret_norm_g, v_w_ret_o, v_q_a_norm_g, v_w_q_b, v_kv_a_norm_g, v_w_kv_b, v_w_mla_o, v_w_out, v_norm_mlp_g, v_w_up, v_w_down, v_norm_f_g):
    xs, tgt, pos = x[0], loss_target[0], positions[0]
    s, d = xs.shape
    mats = {"w_in": w_in[0], "w_ret_o": w_ret_o[0], "w_q_b": w_q_b[0], "w_kv_b": w_kv_b[0], "w_mla_o": w_mla_o[0],
            "w_out": w_out[0], "w_up": w_up[0], "w_down": w_down[0]}
    mat_w = {"w_in": w_in, "w_ret_o": w_ret_o, "w_q_b": w_q_b, "w_kv_b": w_kv_b, "w_mla_o": w_mla_o, "w_out": w_out,
             "w_up": w_up, "w_down": w_down}
    mat_m = {"w_in": m_w_in, "w_ret_o": m_w_ret_o, "w_q_b": m_w_q_b, "w_kv_b": m_w_kv_b, "w_mla_o": m_w_mla_o,
             "w_out": m_w_out, "w_up": m_w_up, "w_down": m_w_down}
    mat_v = {"w_in": v_w_in, "w_ret_o": v_w_ret_o, "w_q_b": v_w_q_b, "w_kv_b": v_w_kv_b, "w_mla_o": v_w_mla_o,
             "w_out": v_w_out, "w_up": v_w_up, "w_down": v_w_down}
    names = list(mats)
    col_sharded = ("w_in", "w_q_b", "w_kv_b", "w_up")
    vw = ret_norm_g.shape[1]
    mla_w = mats["w_mla_o"].shape[0] * N_DEV
    ql, kl = q_a_norm_g.shape[1], kv_a_norm_g.shape[1]
    n_in = mats["w_in"].shape[1] * N_DEV
    qw = (n_in - 2 * vw - ql - kl - ROPE - 2 * d) // 2
    lay = _make_layout(d, vw, qw, ql, kl, mla_w)
    assert lay["n_in"] == n_in
    heads_r, heads_m = lay["ret_heads"], lay["mla_heads"]

    shard16 = {n: mats[n].astype(BF16) for n in names}
    shard16["w_in"] = jnp.swapaxes(w_in, 1, 2)[0].astype(BF16)
    with_in_proj = ("w_ret_o", "w_q_b", "w_kv_b", "w_mla_o", "w_out")
    mlp = ("w_up", "w_down")
    by_device = ("w_up", "w_kv_b")
    full = {}

    def keep(group, gathered):
        for n, g in zip(group, gathered):
            if n not in by_device:
                g = _cols_to_full(g) if n in col_sharded else g.reshape(-1, g.shape[2])
            full[n] = g

    w_mine = _w_in_to_mine(_exchange_alone("gather_w_in", _gather_exchange([shard16["w_in"]]))[0], lay)

    c64, s64 = _rope_tables(pos, RET_QK // 2)
    cos_r = jnp.concatenate([c64, c64], axis=1)
    sin_r = jnp.concatenate([-s64, s64], axis=1)
    c32, s32 = _rope_tables(pos, ROPE // 2)
    z32, z64 = jnp.zeros_like(c32), jnp.zeros((s, LANES - ROPE), F32)
    cos_p = jnp.concatenate([c32, c32, z64], axis=1)
    sin_a = jnp.concatenate([-s32, z32, z64], axis=1)
    sin_b = jnp.concatenate([z32, s32, z64], axis=1)
    lg = jnp.log(1.0 - 2.0 ** (-5.0 - jnp.arange(heads_r, dtype=F32)))
    lgs = jnp.broadcast_to(lg[:, None, None], (heads_r, 8, LANES))

    tm = min(256, s)
    blk = min(512, s)
    t_att = min(512, s)

    u = _rms_fwd("norm_mix", xs, norm_mix_g, tm)
    proj, gathered = _mm("in_proj", u, w_mine, "nt", F32, tn=1280,
                         ride=_gather_exchange([shard16[n] for n in with_in_proj]))
    keep(with_in_proj, gathered)
    wq_pad = jnp.pad(full["w_q_b"].reshape(ql, heads_m, NOPE + ROPE),
                     ((0, 0), (0, 0), (0, QPAD - NOPE - ROPE))).reshape(ql, heads_m * QPAD)
    o_ret, states, ry = _ret_fwd(proj, lay, cos_r, sin_r, lgs, ret_norm_g, blk)
    y_ret = _mm("ret_out", ry, full["w_ret_o"], "nn", F32)
    cqn, ckvn, kpr = _mla_prep(proj, lay, q_a_norm_g, kv_a_norm_g, cos_p, sin_a, sin_b, tm)
    qf = _mm("q_up", cqn, wq_pad, "nn", BF16, extras=((cos_p, None), (sin_a, None), (sin_b, None)), epilogue=_q_operand)
    kv, kf = _mm("kv_up", ckvn, full["w_kv_b"], "nn", (BF16, BF16), b_by_device=True, extras=((kpr, None),),
                 epilogue=_k_operand)
    o_mla, lse, gathered = _attn_fwd(qf, kf, kv, lay, t_att, ride=_gather_exchange([shard16["w_up"]]))
    keep(("w_up",), gathered)
    gate_tile = _tile(d, 1024)
    y_mla, merged = _mm(
        "mla_out", o_mla, full["w_mla_o"], "nn", (F32, BF16), tm=512, tn=gate_tile,
        extras=((proj, lay["off"]["g_ret"] // gate_tile), (proj, lay["off"]["g_mla"] // gate_tile), y_ret),
        epilogue=lambda r, gr, gm, yr: (r, _sig(gr) * yr + _sig(gm) * r))
    h1, n2 = _mm("out_proj", merged, full["w_out"], "nn", (F32, BF16), tm=512, tn=d,
                 extras=(xs, (norm_mlp_g, "whole")), epilogue=_residual_norm)
    (z, act), gathered = _mm("mlp_up", n2, full["w_up"], "nn", (F32, BF16), b_by_device=True,
                             epilogue=lambda r: (r, jnp.square(jnp.maximum(r, 0.0))),
                             ride=_gather_exchange([shard16["w_down"]]))
    keep(("w_down",), gathered)
    dn = _mm("mlp_down", act, full["w_down"], "nn", F32, tk=4096)
    dh2, dh2_16, g_norm_f, loss_part = _final("loss_head", h1, dn, norm_f_g.reshape(1, d), tgt, tm)

    mx, my, mc = _place()
    c_arr = jnp.reshape(mc, (1,)).astype(jnp.int32)
    chip_arr = jnp.reshape(2 * mx + my, (1,)).astype(jnp.int32)
    sums, from_chips = {}, {}

    def blocks(group, grads):
        return [g if n in by_device else (_full_to_cols(g) if n in col_sharded else g.reshape((N_DEV,) + mats[n].shape))
                for n, g in zip(group, grads)]

    def pair_sums(group, mine, from_sibling):
        for n, g, r in zip(group, mine, from_sibling):
            sums[n] = _pair_sum("pair_sum_" + n, g, r, c_arr, 512)
        return [sums[n] for n in group]

    dz = _mm("mlp_down_dx", dh2_16, full["w_down"], "nt", BF16, extras=(z,),
             epilogue=lambda r, zz: (r * (2.0 * jnp.maximum(zz, 0.0)),))
    g_w_down = _mm("mlp_down_dw", act, dh2_16, "tn", BF16, tm=512, tn=d, tk=s)
    down_blocks = blocks(("w_down",), (g_w_down,))
    g_w_up, got_down = _mm("mlp_up_dw", n2, dz, "tn", BF16, tk=s, out_by_device=True,
                           ride=_sibling_exchange(down_blocks))
    dn2, got_up = _mm("mlp_up_dx", dz, full["w_up"], "nt", F32, tk=4096, b_by_device=True,
                      ride=_sibling_exchange([g_w_up]))
    mlp_sums = pair_sums(mlp, [g_w_up] + down_blocks, list(got_up) + list(got_down))
    dh1, dh1_16, g_norm_mlp = _rms_bwd("norm_mlp_bwd", dn2, h1, norm_mlp_g, dh2, tm, matmul_copy=True)
    assert lay["off"]["g_ret"] == 0 and lay["off"]["g_mla"] == d
    dy_ret, dy_mla, d_proj = _mm(
        "out_proj_dx", dh1_16, full["w_out"], "nt", (BF16, BF16, (BF16, 2, lay["total"])), tm=256, tn=d,
        extras=((proj, 0), (proj, 1), y_ret, y_mla), epilogue=_gate_grads)
    g_w_out = _mm("out_proj_dw", merged, dh1_16, "tn", BF16, tm=512, tn=d, tk=s)
    g_w_ret_o = _mm("ret_out_dw", ry, dy_ret, "tn", BF16, tm=512, tn=d, tk=s)
    g_w_mla_o = _mm("mla_out_dw", o_mla, dy_mla, "tn", BF16, tm=256, tn=d, tk=s)
    mixer = ("w_out", "w_ret_o", "w_mla_o")
    mixer_blocks = blocks(mixer, (g_w_out, g_w_ret_o, g_w_mla_o))
    d_ry, got = _mm("ret_out_dx", dy_ret, full["w_ret_o"], "nt", F32, ride=_sibling_exchange(mixer_blocks))
    mixer_sums = pair_sums(mixer, mixer_blocks, got)
    d_omla = _mm("mla_out_dx", dy_mla, full["w_mla_o"], "nt", F32)
    d_proj, g_ret_norm, got = _ret_bwd(proj, lay, cos_r, sin_r, lgs, states, d_ry, o_ret, ret_norm_g, d_proj, blk,
                                       ride=_chips_exchange(mixer_sums))
    from_chips.update(zip(mixer, got))
    delta = _attn_delta(d_omla, o_mla, lay, t_att)
    dqp, dkv, dkpe_parts, got = _attn_bwd(qf, kf, kv, lse, delta, d_omla, cos_p, sin_a, sin_b, lay, t_att,
                                          ride=_chips_exchange(mlp_sums))
    from_chips.update(zip(mlp, got))
    d_proj = _rope_key_grad(dkpe_parts, lay, d_proj, tm)
    d_cqn = _mm("q_up_dx", dqp, wq_pad, "nt", F32)
    g_wq_pad = _mm("q_up_dw", cqn, dqp, "tn", BF16)
    d_ckvn = _mm("kv_up_dx", dkv, full["w_kv_b"], "nt", F32, b_by_device=True)
    g_w_kv_b = _mm("kv_up_dw", ckvn, dkv, "tn", BF16, out_by_device=True)
    d_proj, g_q_a = _latent_norm_bwd("q_latent_bwd", proj, lay["off"]["c_q"], d_cqn, q_a_norm_g, d_proj, tm)
    d_proj, g_kv_a = _latent_norm_bwd("kv_latent_bwd", proj, lay["off"]["c_kv"], d_ckvn, kv_a_norm_g, d_proj, tm)
    g_w_q_b = g_wq_pad.reshape(ql, heads_m, QPAD)[:, :, :NOPE + ROPE].reshape(ql, heads_m * (NOPE + ROPE))
    latent = ("w_q_b", "w_kv_b")
    latent_blocks = blocks(latent, (g_w_q_b, g_w_kv_b))
    latent_sums = pair_sums(latent, latent_blocks,
                            _exchange_alone("latent_grads_to_sibling", _sibling_exchange(latent_blocks)))
    g_w_mine, got = _mm("in_proj_dw", d_proj, u, "tn", BF16, tm=768, tk=s, ride=_chips_exchange(latent_sums))
    from_chips.update(zip(latent, got))
    last = ("w_in",)
    last_blocks = [_mine_to_blocks(g_w_mine, lay)]
    last_sums = pair_sums(last, last_blocks, _exchange_alone("grads_to_sibling", _sibling_exchange(last_blocks)))
    du, got = _mm("in_proj_dx", d_proj, w_mine, "nn", F32, ride=_chips_exchange(last_sums))
    from_chips.update(zip(last, got))
    grad_x, g_norm_mix = _rms_bwd("norm_mix_bwd", du, xs, norm_mix_g, dh1, tm)

    upd = {n: _adamw_shard("adamw_" + n, mat_w[n], mat_m[n], mat_v[n], sums[n], from_chips[n], chip_arr, 256)
           for n in names if n != "w_in"}
    flip = lambda a: jnp.swapaxes(a, 1, 2)
    upd["w_in"] = [flip(o) for o in _adamw_shard(
        "adamw_w_in", flip(w_in), flip(m_w_in), flip(v_w_in), sums["w_in"], from_chips["w_in"], chip_arr, 512, tc=512)]

    gains = [("norm_mix_g", norm_mix_g, m_norm_mix_g, v_norm_mix_g, g_norm_mix),
             ("ret_norm_g", ret_norm_g, m_ret_norm_g, v_ret_norm_g, g_ret_norm),
             ("q_a_norm_g", q_a_norm_g, m_q_a_norm_g, v_q_a_norm_g, g_q_a),
             ("kv_a_norm_g", kv_a_norm_g, m_kv_a_norm_g, v_kv_a_norm_g, g_kv_a),
             ("norm_mlp_g", norm_mlp_g, m_norm_mlp_g, v_norm_mlp_g, g_norm_mlp),
             ("norm_f_g", norm_f_g, m_norm_f_g, v_norm_f_g, g_norm_f)]
    as_row = lambda a: a.reshape(1, -1)
    g_sm, d_sm, m_sm, v_sm, loss_row = _gains_all_reduce_adam(
        [g[4] for g in gains], loss_part, *[[as_row(g[k]) for g in gains] for k in (1, 2, 3)])
    loss = loss_row[0, 0]
    small = {g[0]: [a[r].reshape(g[1].shape) for a in (g_sm, d_sm, m_sm, v_sm)] for r, g in enumerate(gains)}

    order = ["norm_mix_g", "w_in", "ret_norm_g", "w_ret_o", "q_a_norm_g", "w_q_b", "kv_a_norm_g", "w_kv_b", "w_mla_o",
             "w_out", "norm_mlp_g", "w_up", "w_down", "norm_f_g"]
    outs = [loss, grad_x[None]]
    for k in range(4):
        for n in order:
            outs.append(small[n][k] if n in small else upd[n][k])
    return tuple(outs)
```

```python
import functools
import math

import jax
import jax.numpy as jnp
from jax import lax
from jax.experimental import pallas as pl
from jax.experimental.pallas import tpu as pltpu

F32 = jnp.float32
BF16 = jnp.bfloat16
MESH = pl.DeviceIdType.MESH

EPS = 1e-6
ROPE_THETA = 10000.0
CHUNK_SHIFT = 6
RET_QK = 128
RET_V = 256
NOPE = 128
ROPE = 64
VHEAD = 128
QPAD = 256
LANES = 128
N_DEV = 8
VMEM_LIMIT = 56 * 1024 * 1024

ADAM_LR = 0.001
ADAM_B1 = 0.9
ADAM_B2 = 0.999
ADAM_EPS = 1e-08
ADAM_WD = 0.01
ADAM_STEP = 10

NN = (((1,), (0,)), ((), ()))
NT = (((1,), (1,)), ((), ()))
TN = (((0,), (0,)), ((), ()))


def _dot(a, b, dims):
    return lax.dot_general(a.astype(BF16), b.astype(BF16), dims, preferred_element_type=F32)


def _tile(dim, pref):
    if dim <= pref:
        return dim
    t = (pref // LANES) * LANES
    while t >= LANES:
        if dim % t == 0:
            return t
        t -= LANES
    raise ValueError(f"no tile for {dim}")


def _params(sem):
    return pltpu.CompilerParams(dimension_semantics=sem, vmem_limit_bytes=VMEM_LIMIT)


def _sig(v):
    return 1.0 / (1.0 + jnp.exp(-v))


def _mm(name, a, b, mode, out_dtypes, *, tm=1024, tn=1024, tk=2048, extras=(), epilogue=None, ride=None,
        b_by_device=False, out_by_device=False):
    if b_by_device:
        b_cols = b.shape[2]
        b_shape = (b.shape[1], N_DEV * b_cols)
    else:
        b_shape = b.shape
    if mode == "nn":
        (m, k), (_, n) = a.shape, b_shape
    elif mode == "nt":
        (m, k), (n, _) = a.shape, b_shape
    else:
        (k, m), (_, n) = a.shape, b_shape
    tm, tn, tk = _tile(m, tm), _tile(n, tn), _tile(k, tk)
    if b_by_device and mode != "nt":
        tn = _tile(b_cols, tn)
    if out_by_device:
        tn = _tile(n // N_DEV, tn)
    nk = k // tk
    dims = {"nn": NN, "nt": NT, "tn": TN}[mode]
    a_spec = (pl.BlockSpec((tk, tm), lambda i, j, kk: (kk, i)) if mode == "tn"
              else pl.BlockSpec((tm, tk), lambda i, j, kk: (i, kk)))
    if b_by_device and mode == "nt":
        piece = min(tk, b_cols)
        n_b, per = tk // piece, b_cols // piece
        b_specs = [pl.BlockSpec((None, tn, piece),
                                lambda i, j, kk, p=p: ((kk * n_b + p) // per, j, (kk * n_b + p) % per))
                   for p in range(n_b)]
    elif b_by_device:
        per = b_cols // tn
        n_b, piece = 1, tk
        b_specs = [pl.BlockSpec((None, tk, tn), lambda i, j, kk: (j // per, kk, j % per))]
    else:
        n_b, piece = 1, tk
        b_specs = [pl.BlockSpec((tn, tk), lambda i, j, kk: (j, kk)) if mode == "nt"
                   else pl.BlockSpec((tk, tn), lambda i, j, kk: (kk, j))]
    tile_spec = pl.BlockSpec((tm, tn), lambda i, j, kk: (i, j))
    if out_by_device:
        per_out = n // N_DEV // tn
        out_spec = pl.BlockSpec((None, tm, tn), lambda i, j, kk: (j // per_out, i, j % per_out))
        out_dims = (N_DEV, m, n // N_DEV)
    else:
        out_spec, out_dims = tile_spec, (m, n)
    ex_arrays, ex_specs = [], []
    for e in extras:
        arr, off = e if isinstance(e, tuple) else (e, 0)
        ex_arrays.append(arr)
        if off == "whole":
            ex_specs.append(pl.BlockSpec(arr.shape, lambda i, j, kk, nd=arr.ndim: (0,) * nd))
        elif off is None:
            ex_specs.append(pl.BlockSpec((tm, arr.shape[1]), lambda i, j, kk: (i, 0)))
        else:
            ex_specs.append(pl.BlockSpec((tm, tn), lambda i, j, kk, off=off: (i, j + off)))
    n_ex = len(extras)
    single = not isinstance(out_dtypes, (tuple, list))
    dts = (out_dtypes,) if single else tuple(out_dtypes)
    out_specs, out_shapes = [], []
    for dt in dts:
        if isinstance(dt, tuple):
            dt, mult, width = dt
            out_specs.append(pl.BlockSpec((tm, mult * tn), lambda i, j, kk: (i, j)))
            out_shapes.append(jax.ShapeDtypeStruct((m, width), dt))
        else:
            out_specs.append(out_spec)
            out_shapes.append(jax.ShapeDtypeStruct(out_dims, dt))

    grid = (m // tm, n // tn, nk)
    r_in, r_out, r_sc = ride.counts() if ride else (0, 0, 0)
    n_acc = 1 if nk > 1 else 0

    def body(a_ref, *rest):
        b_refs, rest = rest[:n_b], rest[n_b:]
        ex, rest = rest[:n_ex], rest[n_ex:]
        ride_in, rest = rest[:r_in], rest[r_in:]
        outs, rest = rest[:len(dts)], rest[len(dts):]
        ride_out, rest = rest[:r_out], rest[r_out:]
        ride_scratch = rest[n_acc:]
        if ride:
            first, mid, last = _steps([pl.program_id(d) for d in range(3)], grid)
            ride.run(ride_in, ride_out, ride_scratch, (first, mid, None))

        def finish(r):
            vals = (r,) if epilogue is None else epilogue(r, *[e[...] for e in ex])
            for o, v in zip(outs, vals):
                o[...] = v.astype(o.dtype)

        if n_b == 1:
            part = _dot(a_ref[...], b_refs[0][...], dims)
        else:
            part = sum(_dot(a_ref[:, p * piece:(p + 1) * piece], b_refs[p][...], dims) for p in range(n_b))
        if nk == 1:
            finish(part)
        else:
            acc = rest[0]
            kk = pl.program_id(2)

            @pl.when(kk == 0)
            def _():
                acc[...] = part

            @pl.when(jnp.logical_and(kk > 0, kk < nk - 1))
            def _():
                acc[...] += part

            @pl.when(kk == nk - 1)
            def _():
                finish(acc[...] + part)

        if ride:
            ride.run(ride_in, ride_out, ride_scratch, (None, None, last))

    res = pl.pallas_call(
        body, name=name, grid=grid,
        in_specs=[a_spec] + b_specs + ex_specs + [ANY] * r_in,
        out_specs=out_specs + [ANY] * r_out,
        out_shape=out_shapes + (ride.out_shape if ride else []),
        scratch_shapes=([pltpu.VMEM((tm, tn), F32)] if nk > 1 else []) + (ride.scratch if ride else []),
        compiler_params=_params(("arbitrary",) * 3 if ride else ("parallel", "parallel", "arbitrary")),
    )(a, *[b] * n_b, *ex_arrays, *(ride.ins if ride else []))
    own = res[0] if single else res[:len(dts)]
    return (own, res[len(dts):]) if ride else own


def _rows(name, body, n_rows, tm, ins, outs, accs=(), into=None):
    in_specs, args = [], []
    for t in ins:
        if len(t) == 1:
            in_specs.append(pl.BlockSpec(t[0].shape, lambda i, nd=t[0].ndim: (0,) * nd))
        else:
            in_specs.append(pl.BlockSpec((tm, t[1]), lambda i, cb=t[2]: (i, cb)))
        args.append(t[0])
    outs = [(o + (o[0], 0))[:4] for o in outs]
    out_specs = [pl.BlockSpec((tm, w), lambda i, cb=cb: (i, cb)) for w, _, _, cb in outs]
    out_shape = [jax.ShapeDtypeStruct((n_rows, total), d) for _, d, total, _ in outs]
    aliases, kernel = {}, body
    if into is not None:
        arr, w, cb = into
        in_specs.append(ANY)
        args.append(arr)
        out_specs.append(pl.BlockSpec((tm, w), lambda i: (i, cb)))
        out_shape.append(jax.ShapeDtypeStruct(arr.shape, arr.dtype))
        aliases = {len(ins): len(outs)}
        n_in = len(ins)

        def kernel(*refs):
            body(*refs[:n_in], *refs[n_in + 1:])

    out_specs += [pl.BlockSpec((r, w), lambda i: (0, 0)) for r, w in accs]
    out_shape += [jax.ShapeDtypeStruct((r, w), F32) for r, w in accs]
    return pl.pallas_call(
        kernel, name=name, grid=(n_rows // tm,), in_specs=in_specs, out_specs=out_specs, out_shape=out_shape,
        input_output_aliases=aliases, compiler_params=_params(("arbitrary",) if accs else ("parallel",)),
    )(*args)


def _zero_first(*accs):
    @pl.when(pl.program_id(0) == 0)
    def _():
        for a in accs:
            a[...] = jnp.zeros_like(a)


def _rope64(t, cos, sin):
    return t * cos + pltpu.roll(t, RET_QK // 2, 1) * sin


def _rope32(t, cos, sin_a, sin_b):
    return t * cos + pltpu.roll(t, LANES - ROPE // 2, 1) * sin_a + pltpu.roll(t, ROPE // 2, 1) * sin_b


def _rms_fwd(name, x, g, tm):
    s, d = x.shape

    def body(x_ref, g_ref, u_ref):
        v = x_ref[...]
        r = lax.rsqrt(jnp.mean(v * v, axis=-1, keepdims=True) + EPS)
        u_ref[...] = (v * r * g_ref[...]).astype(BF16)

    return _rows(name, body, s, tm, [(x, d, 0), (g,)], [(d, BF16)])[0]


def _residual_norm(r, x, g):
    h = x + r
    return h, h * lax.rsqrt(jnp.mean(h * h, axis=-1, keepdims=True) + EPS) * g


def _gate_grads(dm, gr, gm, yr, ym):
    sr, sm = _sig(gr), _sig(gm)
    return dm * sr, dm * sm, jnp.concatenate([dm * yr * (sr * (1.0 - sr)), dm * ym * (sm * (1.0 - sm))], axis=1)


def _rms_bwd(name, dy, x, g, dres, tm, matmul_copy=False):
    s, d = x.shape

    def body(dy_ref, x_ref, g_ref, dres_ref, dx_ref, *rest):
        dg_ref = rest[-1]
        _zero_first(dg_ref)
        v, dyv = x_ref[...], dy_ref[...]
        r = lax.rsqrt(jnp.mean(v * v, axis=-1, keepdims=True) + EPS)
        xh = v * r
        dxh = dyv * g_ref[...]
        dx = dres_ref[...] + r * (dxh - xh * jnp.mean(dxh * xh, axis=-1, keepdims=True))
        dx_ref[...] = dx
        if matmul_copy:
            rest[0][...] = dx.astype(BF16)
        dg_ref[...] += jnp.sum(dyv * xh, axis=0, keepdims=True)

    return _rows(name, body, s, tm, [(dy, d, 0), (x, d, 0), (g,), (dres, d, 0)],
                 [(d, F32)] + [(d, BF16)] * matmul_copy, [(1, d)])


def _final(name, h1, dn, g, tgt, tm):
    s, d = h1.shape

    def body(h_ref, dn_ref, g_ref, t_ref, dh_ref, dh16_ref, dg_ref, loss_ref):
        _zero_first(dg_ref, loss_ref)
        v = h_ref[...] + dn_ref[...]
        r = lax.rsqrt(jnp.mean(v * v, axis=-1, keepdims=True) + EPS)
        xh = v * r
        gv = g_ref[...]
        e = xh * gv - t_ref[...]
        loss_ref[...] += 0.5 * jnp.sum(jnp.mean(e * e, axis=-1, keepdims=True))
        dy = e * (1.0 / d)
        dg_ref[...] += jnp.sum(dy * xh, axis=0, keepdims=True)
        dxh = dy * gv
        dh = r * (dxh - xh * jnp.mean(dxh * xh, axis=-1, keepdims=True))
        dh_ref[...] = dh
        dh16_ref[...] = dh.astype(BF16)

    return _rows(name, body, s, tm, [(h1, d, 0), (dn, d, 0), (g,), (tgt, d, 0)], [(d, F32), (d, BF16)],
                 [(1, d), (1, LANES)])


def _decay_mask(lg, blk):
    n = lax.broadcasted_iota(jnp.int32, (blk, blk), 0)
    m = lax.broadcasted_iota(jnp.int32, (blk, blk), 1)
    w = jnp.exp(lg * jnp.abs(n - m).astype(F32))
    return jnp.where(jnp.right_shift(m, CHUNK_SHIFT) <= jnp.right_shift(n, CHUNK_SHIFT), w, 0.0)


def _decays(lg, blk):
    pos = lax.broadcasted_iota(jnp.int32, (blk, 1), 0).astype(F32)
    return jnp.exp(lg * (pos + 1.0)), jnp.exp(lg * (blk - 1.0 - pos)), jnp.exp(lg * float(blk))


def _ret_fwd(proj, lay, cos, sin, lgs, gain, blk, ride=None):
    s = proj.shape[0]
    heads = lay["ret_heads"]
    nb = s // blk
    scale = RET_QK ** -0.5
    ride_in_specs, ride_ins, ride_out_specs, ride_out_shape, ride_scratch = _ride_args(ride)

    def body(lg_ref, qkv_ref, cos_ref, sin_ref, g_ref, o_ref, st_ref, ry_ref, state, mask):
        lg = lg_ref[0:1, 0:1]

        @pl.when(pl.program_id(1) == 0)
        def _():
            state[...] = jnp.zeros_like(state)
            mask[...] = _decay_mask(lg, blk)

        a, c, gb = _decays(lg, blk)
        q = _rope64(qkv_ref[:, :RET_QK], cos_ref[...], sin_ref[...])
        k = _rope64(qkv_ref[:, RET_QK:2 * RET_QK], cos_ref[...], sin_ref[...]) * scale
        v = qkv_ref[:, 2 * RET_QK:2 * RET_QK + RET_V]
        st = state[...]
        st_ref[...] = st
        sm = _dot(q, k, NT) * mask[...]
        o = _dot(sm, v, NN) + _dot(q * a, st, NN)
        o_ref[...] = o
        state[...] = st * gb + _dot(k * c, v, TN)
        dlt = o - jnp.mean(o, axis=-1, keepdims=True)
        rstd = lax.rsqrt(jnp.mean(dlt * dlt, axis=-1, keepdims=True) + EPS)
        rg = qkv_ref[:, 2 * RET_QK + RET_V:]
        ry_ref[...] = (dlt * rstd * g_ref[...] * (rg * _sig(rg))).astype(BF16)

    first = lay["off"]["heads"] // RET_HEAD
    res = pl.pallas_call(
        _with_ride(body, ride, (heads, nb), 0, 5, 3), name="ret_fwd", grid=(heads, nb),
        in_specs=[pl.BlockSpec((None, 8, LANES), lambda h, b: (h, 0, 0)),
                  pl.BlockSpec((blk, RET_HEAD), lambda h, b: (b, first + h)),
                  pl.BlockSpec((blk, LANES), lambda h, b: (b, 0)),
                  pl.BlockSpec((blk, LANES), lambda h, b: (b, 0)),
                  pl.BlockSpec((1, RET_V), lambda h, b: (0, h))] + ride_in_specs,
        out_specs=[pl.BlockSpec((blk, RET_V), lambda h, b: (b, h)),
                   pl.BlockSpec((None, None, RET_QK, RET_V), lambda h, b: (h, b, 0, 0)),
                   pl.BlockSpec((blk, RET_V), lambda h, b: (b, h))] + ride_out_specs,
        out_shape=[jax.ShapeDtypeStruct((s, heads * RET_V), F32),
                   jax.ShapeDtypeStruct((heads, nb, RET_QK, RET_V), F32),
                   jax.ShapeDtypeStruct((s, heads * RET_V), BF16)] + ride_out_shape,
        scratch_shapes=[pltpu.VMEM((RET_QK, RET_V), F32), pltpu.VMEM((blk, blk), F32)] + ride_scratch,
        compiler_params=_params(("arbitrary", "arbitrary") if ride else ("parallel", "arbitrary")),
    )(lgs, proj, cos, sin, gain, *ride_ins)
    return (res[0], res[1], res[2], res[3:]) if ride else res


def _ret_bwd(proj, lay, cos, sin, lgs, states, d_ry, o, gain, d_proj, blk, ride=None):
    ride_in_specs, ride_ins, ride_out_specs, ride_out_shape, ride_scratch = _ride_args(ride)
    s = proj.shape[0]
    heads = lay["ret_heads"]
    nb = s // blk
    scale = RET_QK ** -0.5

    def body(lg_ref, qkv_ref, cos_ref, sin_ref, st_ref, dry_ref, o_ref, g_ref, _, dqkv_ref, dg_ref, dstate, mask):
        lg = lg_ref[0:1, 0:1]

        @pl.when(pl.program_id(1) == 0)
        def _():
            dstate[...] = jnp.zeros_like(dstate)
            mask[...] = _decay_mask(lg, blk)
            dg_ref[...] = jnp.zeros_like(dg_ref)

        oh = o_ref[...]
        dlt = oh - jnp.mean(oh, axis=-1, keepdims=True)
        rstd = lax.rsqrt(jnp.mean(dlt * dlt, axis=-1, keepdims=True) + EPS)
        oh = dlt * rstd
        gv = g_ref[...]
        rg = qkv_ref[:, 2 * RET_QK + RET_V:]
        sg = _sig(rg)
        dry = dry_ref[...]
        dt = dry * (rg * sg)
        dqkv_ref[:, 2 * RET_QK + RET_V:] = (dry * (oh * gv) * (sg * (1.0 + rg * (1.0 - sg)))).astype(BF16)
        dg_ref[...] += jnp.sum(dt * oh, axis=0, keepdims=True)
        doh = dt * gv
        do = rstd * (doh - jnp.mean(doh, axis=-1, keepdims=True) - oh * jnp.mean(doh * oh, axis=-1, keepdims=True))

        a, c, gb = _decays(lg, blk)
        cs, sn = cos_ref[...], sin_ref[...]
        q = _rope64(qkv_ref[:, :RET_QK], cs, sn)
        k = _rope64(qkv_ref[:, RET_QK:2 * RET_QK], cs, sn) * scale
        v = qkv_ref[:, 2 * RET_QK:2 * RET_QK + RET_V]
        st = st_ref[...]
        dst = dstate[...]
        mk = mask[...]
        sm = _dot(q, k, NT) * mk
        ds = _dot(do, v, NT) * mk
        dq = _dot(ds, k, NN) + _dot(do, st, NT) * a
        dk = _dot(ds, q, TN) + _dot(v, dst, NT) * c
        dqkv_ref[:, 2 * RET_QK:2 * RET_QK + RET_V] = (_dot(sm, do, TN) + _dot(k * c, dst, NN)).astype(BF16)
        dstate[...] = dst * gb + _dot(q * a, do, TN)
        dqkv_ref[:, :RET_QK] = _rope64(dq, cs, -sn).astype(BF16)
        dqkv_ref[:, RET_QK:2 * RET_QK] = (_rope64(dk, cs, -sn) * scale).astype(BF16)

    first = lay["off"]["heads"] // RET_HEAD
    last = nb - 1
    head_tile = pl.BlockSpec((blk, RET_V), lambda h, b: (last - b, h))
    res = pl.pallas_call(
        _with_ride(body, ride, (heads, nb), 0, 9, 2), name="ret_bwd", grid=(heads, nb),
        in_specs=[pl.BlockSpec((None, 8, LANES), lambda h, b: (h, 0, 0)),
                  pl.BlockSpec((blk, RET_HEAD), lambda h, b: (last - b, first + h)),
                  pl.BlockSpec((blk, LANES), lambda h, b: (last - b, 0)),
                  pl.BlockSpec((blk, LANES), lambda h, b: (last - b, 0)),
                  pl.BlockSpec((None, None, RET_QK, RET_V), lambda h, b: (h, last - b, 0, 0)),
                  head_tile, head_tile, pl.BlockSpec((1, RET_V), lambda h, b: (0, h)), ANY] + ride_in_specs,
        out_specs=[pl.BlockSpec((blk, RET_HEAD), lambda h, b: (last - b, first + h)),
                   pl.BlockSpec((1, RET_V), lambda h, b: (0, h))] + ride_out_specs,
        out_shape=[jax.ShapeDtypeStruct(d_proj.shape, d_proj.dtype),
                   jax.ShapeDtypeStruct((1, heads * RET_V), F32)] + ride_out_shape,
        scratch_shapes=[pltpu.VMEM((RET_QK, RET_V), F32), pltpu.VMEM((blk, blk), F32)] + ride_scratch,
        input_output_aliases={8: 0},
        compiler_params=_params(("arbitrary", "arbitrary") if ride else ("parallel", "arbitrary")),
    )(lgs, proj, cos, sin, states, d_ry, o, gain, d_proj, *ride_ins)
    return (res[0], res[1], res[2:]) if ride else res[:2]


def _mla_prep(proj, lay, gq, gkv, cos, sin_a, sin_b, tm):
    s = proj.shape[0]
    ql, kl = lay["q_lora"], lay["kv_lora"]

    def body(cq_ref, ckv_ref, kpe_ref, gq_ref, gkv_ref, cos_ref, sa_ref, sb_ref, cqn_ref, ckvn_ref, kpr_ref):
        for src, gref, dst in ((cq_ref, gq_ref, cqn_ref), (ckv_ref, gkv_ref, ckvn_ref)):
            v = src[...]
            r = lax.rsqrt(jnp.mean(v * v, axis=-1, keepdims=True) + EPS)
            dst[...] = (v * r * gref[...]).astype(BF16)
        kpr_ref[...] = _rope32(kpe_ref[...], cos_ref[...], sa_ref[...], sb_ref[...]).astype(BF16)

    off = lay["off"]
    return _rows("mla_prep", body, s, tm,
                 [(proj, ql, off["c_q"] // ql), (proj, kl, off["c_kv"] // kl), (proj, LANES, off["k_pe"] // LANES),
                  (gq,), (gkv,), (cos, LANES, 0), (sin_a, LANES, 0), (sin_b, LANES, 0)],
                 [(ql, BF16), (kl, BF16), (LANES, BF16)])


def _latent_norm_bwd(name, proj, offset, d_normed, g, d_proj, tm):
    s, w = d_normed.shape

    def body(dy_ref, x_ref, g_ref, dx_ref, dg_ref):
        _zero_first(dg_ref)
        v, dy = x_ref[...], dy_ref[...]
        r = lax.rsqrt(jnp.mean(v * v, axis=-1, keepdims=True) + EPS)
        xh = v * r
        dxh = dy * g_ref[...]
        dx_ref[...] = (r * (dxh - xh * jnp.mean(dxh * xh, axis=-1, keepdims=True))).astype(BF16)
        dg_ref[...] += jnp.sum(dy * xh, axis=0, keepdims=True)

    return _rows(name, body, s, tm, [(d_normed, w, 0), (proj, w, offset // w), (g,)], [], [(1, w)],
                 into=(d_proj, w, offset // w))


def _q_operand(r, cos, sin_a, sin_b):
    qs = (NOPE + ROPE) ** -0.5 * math.log2(math.e)
    cs, sa, sb = cos * qs, sin_a * qs, sin_b * qs
    parts = []
    for lo in range(0, r.shape[1], QPAD):
        parts += [r[:, lo:lo + NOPE] * qs, _rope32(r[:, lo + NOPE:lo + QPAD], cs, sa, sb)]
    return (jnp.concatenate(parts, axis=1),)


def _k_operand(r, kpr):
    parts = []
    for lo in range(0, r.shape[1], QPAD):
        parts += [r[:, lo:lo + NOPE], kpr.astype(F32)]
    return r, jnp.concatenate(parts, axis=1)


def _rope_key_grad(parts, lay, d_proj, tm):
    s, w = parts.shape

    def body(p_ref, dkpe_ref):
        dkpe_ref[:, :LANES] = sum(p_ref[:, lo:lo + LANES] for lo in range(0, w, LANES)).astype(BF16)
        dkpe_ref[:, LANES:] = jnp.zeros((tm, LANES), BF16)

    return _rows("rope_key_grad", body, s, tm, [(parts, w, 0)], [],
                 into=(d_proj, 2 * LANES, lay["off"]["k_pe"] // (2 * LANES)))[0]


def _diag_mask(t, keys_on_rows=False):
    row = lax.broadcasted_iota(jnp.int32, (t, t), 0)
    col = lax.broadcasted_iota(jnp.int32, (t, t), 1)
    key, query = (row, col) if keys_on_rows else (col, row)
    return jnp.right_shift(key, CHUNK_SHIFT) <= jnp.right_shift(query, CHUNK_SHIFT)


def _tile_pairs(nt, by_key):
    if by_key:
        pairs = [(i, j) for j in range(nt) for i in range(j, nt)]
    else:
        pairs = [(i, j) for i in range(nt) for j in range(i + 1)]
    return (jnp.asarray([p[0] for p in pairs], jnp.int32), jnp.asarray([p[1] for p in pairs], jnp.int32))


def _head_block(heads):
    return 4 if heads % 4 == 0 else 2 if heads % 2 == 0 else 1


def _attn_fwd(qf, kf, kv, lay, t, ride=None):
    s = qf.shape[0]
    heads = lay["mla_heads"]
    hb = 8 if heads % 8 == 0 else _head_block(heads)
    nt = s // t
    qi, kj = _tile_pairs(nt, False)
    grid = (heads // hb, int(qi.shape[0]))
    ride_in_specs, ride_ins, ride_out_specs, ride_out_shape, ride_scratch = _ride_args(ride)

    def body(qi_ref, kj_ref, q_ref, k_ref, kv_ref, o_ref, lse_ref, m_s, l_s, acc):
        p = pl.program_id(1)
        i, j = qi_ref[p], kj_ref[p]

        @pl.when(j == 0)
        def _():
            m_s[...] = jnp.full_like(m_s, -jnp.inf)
            l_s[...] = jnp.zeros_like(l_s)
            acc[...] = jnp.zeros_like(acc)

        def step(diagonal):
            ones = jnp.ones((t, LANES), BF16)
            scores = [_dot(q_ref[:, hh * QPAD:(hh + 1) * QPAD], k_ref[:, hh * QPAD:(hh + 1) * QPAD], NT)
                      for hh in range(hb)]
            for hh in range(hb):
                sc = scores[hh]
                if diagonal:
                    sc = jnp.where(_diag_mask(t), sc, -jnp.inf)
                cols = [sc[:, c * LANES:(c + 1) * LANES] for c in range(t // LANES)]
                m_old = m_s[hh]
                m_new = jnp.maximum(m_old, jnp.max(functools.reduce(jnp.maximum, cols), axis=-1, keepdims=True))
                alpha = jnp.exp2(m_old - m_new)
                pr = jnp.concatenate([jnp.exp2(c - m_new).astype(BF16) for c in cols], axis=1)
                pv = _dot(pr, jnp.concatenate([kv_ref[:, hh * QPAD + NOPE:(hh + 1) * QPAD], ones], axis=1), NN)
                l_new = alpha * l_s[hh] + pv[:, VHEAD:]
                a_new = alpha * acc[hh] + pv[:, :VHEAD]
                if diagonal:
                    o_ref[:, hh * VHEAD:(hh + 1) * VHEAD] = a_new / l_new
                    lse_ref[hh] = jnp.transpose(m_new + jnp.log2(l_new))[:1]
                else:
                    m_s[hh], l_s[hh], acc[hh] = m_new, l_new, a_new

        pl.when(j < i)(functools.partial(step, False))
        pl.when(j == i)(functools.partial(step, True))

    res = pl.pallas_call(
        _with_ride(body, ride, grid, 2, 3, 2), name="attn_fwd",
        grid_spec=pltpu.PrefetchScalarGridSpec(
            num_scalar_prefetch=2, grid=grid,
            in_specs=[pl.BlockSpec((t, hb * QPAD), lambda h, p, qi, kj: (qi[p], h)),
                      pl.BlockSpec((t, hb * QPAD), lambda h, p, qi, kj: (kj[p], h)),
                      pl.BlockSpec((t, hb * QPAD), lambda h, p, qi, kj: (kj[p], h))] + ride_in_specs,
            out_specs=[pl.BlockSpec((t, hb * VHEAD), lambda h, p, qi, kj: (qi[p], h)),
                       pl.BlockSpec((hb, 1, t), lambda h, p, qi, kj: (h, 0, qi[p]))] + ride_out_specs,
            scratch_shapes=[pltpu.VMEM((hb, t, LANES), F32), pltpu.VMEM((hb, t, LANES), F32),
                            pltpu.VMEM((hb, t, VHEAD), F32)] + ride_scratch),
        out_shape=[jax.ShapeDtypeStruct((s, heads * VHEAD), F32),
                   jax.ShapeDtypeStruct((heads, 1, s), F32)] + ride_out_shape,
        compiler_params=_params(("arbitrary", "arbitrary") if ride else ("parallel", "arbitrary")),
    )(qi, kj, qf, kf, kv, *ride_ins)
    return (res[0], res[1], res[2:]) if ride else res


def _attn_delta(d_o, o, lay, tm):
    s = o.shape[0]
    heads = lay["mla_heads"]

    def body(do_ref, o_ref, dl_ref):
        for h in range(heads):
            sl = slice(h * VHEAD, (h + 1) * VHEAD)
            dl_ref[h] = jnp.sum(jnp.transpose(do_ref[:, sl] * o_ref[:, sl]), axis=0, keepdims=True)

    tile = pl.BlockSpec((tm, heads * VHEAD), lambda i: (i, 0))
    return pl.pallas_call(
        body, name="attn_delta", grid=(s // tm,), in_specs=[tile, tile],
        out_specs=pl.BlockSpec((heads, 1, tm), lambda i: (0, 0, i)),
        out_shape=jax.ShapeDtypeStruct((heads, 1, s), F32),
        compiler_params=_params(("parallel",)),
    )(d_o, o)


def _attn_bwd(qf, kf, kv, lse, delta, d_o, cos, sin_a, sin_b, lay, t, ride=None):
    s = qf.shape[0]
    heads = lay["mla_heads"]
    hb = _head_block(heads)
    nt = s // t
    scale = (NOPE + ROPE) ** -0.5
    qi, kj = _tile_pairs(nt, True)
    grid = (heads // hb, int(qi.shape[0]))
    ride_in_specs, ride_ins, ride_out_specs, ride_out_shape, ride_scratch = _ride_args(ride)

    def body(qi_ref, kj_ref, q_ref, k_ref, kv_ref, lse_ref, dl_ref, do_ref, cos_ref, sa_ref, sb_ref,
             dqp_ref, dkv_ref, dkpe_ref, dq_acc, dk_acc, dv_acc):
        p = pl.program_id(1)
        i, j = qi_ref[p], kj_ref[p]
        rows = pl.ds(pl.multiple_of(i * t, t), t)

        def unrope(v):
            return _rope32(v, cos_ref[...], -sa_ref[...], -sb_ref[...])

        @pl.when(p == 0)
        def _():
            dq_acc[...] = jnp.zeros_like(dq_acc)

        def step(diagonal):
            for hh in range(hb):
                lo = hh * QPAD
                q, k = q_ref[:, lo:lo + QPAD], k_ref[:, lo:lo + QPAD]
                do = do_ref[:, hh * VHEAD:(hh + 1) * VHEAD]
                pr = jnp.exp2(_dot(k, q, NT) - lse_ref[hh])
                if diagonal:
                    pr = jnp.where(_diag_mask(t, keys_on_rows=True), pr, 0.0)
                dv_part = _dot(pr, do, NN)
                ds = (pr * (_dot(kv_ref[:, lo + NOPE:lo + QPAD], do, NT) - dl_ref[hh])).astype(BF16)
                dk_part = _dot(ds, q, NN)
                dq = dq_acc[rows, lo:lo + QPAD] + _dot(ds, k, TN) * scale
                if diagonal:
                    dqp_ref[:, lo:lo + NOPE] = dq[:, :NOPE].astype(BF16)
                    dqp_ref[:, lo + NOPE:lo + QPAD] = unrope(dq[:, NOPE:]).astype(BF16)
                    dk_acc[hh], dv_acc[hh] = dk_part, dv_part
                else:
                    dq_acc[rows, lo:lo + QPAD] = dq
                    dk_acc[hh] += dk_part
                    dv_acc[hh] += dv_part

        pl.when(i > j)(functools.partial(step, False))
        pl.when(i == j)(functools.partial(step, True))

        @pl.when(i == nt - 1)
        def _():
            kpe = jnp.zeros((t, LANES), F32)
            for hh in range(hb):
                lo = hh * QPAD
                dk = dk_acc[hh] * math.log(2.0)
                dkv_ref[:, lo:lo + NOPE] = dk[:, :NOPE].astype(BF16)
                dkv_ref[:, lo + NOPE:lo + QPAD] = dv_acc[hh].astype(BF16)
                kpe = kpe + dk[:, NOPE:]
            dkpe_ref[...] = unrope(kpe)

    table = pl.BlockSpec((t, LANES), lambda h, p, qi, kj: (kj[p], 0))
    res = pl.pallas_call(
        _with_ride(body, ride, grid, 2, 9, 3), name="attn_bwd",
        grid_spec=pltpu.PrefetchScalarGridSpec(
            num_scalar_prefetch=2, grid=grid,
            in_specs=[pl.BlockSpec((t, hb * QPAD), lambda h, p, qi, kj: (qi[p], h)),
                      pl.BlockSpec((t, hb * QPAD), lambda h, p, qi, kj: (kj[p], h)),
                      pl.BlockSpec((t, hb * QPAD), lambda h, p, qi, kj: (kj[p], h)),
                      pl.BlockSpec((hb, 1, t), lambda h, p, qi, kj: (h, 0, qi[p])),
                      pl.BlockSpec((hb, 1, t), lambda h, p, qi, kj: (h, 0, qi[p])),
                      pl.BlockSpec((t, hb * VHEAD), lambda h, p, qi, kj: (qi[p], h)),
                      table, table, table] + ride_in_specs,
            out_specs=[pl.BlockSpec((t, hb * QPAD), lambda h, p, qi, kj: (kj[p], h)),
                       pl.BlockSpec((t, hb * QPAD), lambda h, p, qi, kj: (kj[p], h)),
                       pl.BlockSpec((t, LANES), lambda h, p, qi, kj: (kj[p], h))] + ride_out_specs,
            scratch_shapes=[pltpu.VMEM((s, hb * QPAD), F32), pltpu.VMEM((hb, t, QPAD), F32),
                            pltpu.VMEM((hb, t, VHEAD), F32)] + ride_scratch),
        out_shape=[jax.ShapeDtypeStruct((s, heads * QPAD), BF16),
                   jax.ShapeDtypeStruct((s, heads * QPAD), BF16),
                   jax.ShapeDtypeStruct((s, heads // hb * LANES), F32)] + ride_out_shape,
        compiler_params=_params(("arbitrary", "arbitrary") if ride else ("parallel", "arbitrary")),
    )(qi, kj, qf, kf, kv, lse, delta, d_o, cos, sin_a, sin_b, *ride_ins)
    return (res[0], res[1], res[2], res[3:]) if ride else res


ANY = pl.BlockSpec(memory_space=pl.ANY)


def _place():
    return lax.axis_index("x"), lax.axis_index("y"), lax.axis_index("c")


def _other_chips(x, y):
    return [(1 - x, y), (x, 1 - y), (1 - x, 1 - y)]


class _Exchange:
    def __init__(self, ins, out_shape, scratch, phases):
        self.ins, self.out_shape, self.scratch, self.phases = list(ins), list(out_shape), list(scratch), phases

    def counts(self):
        return len(self.ins), len(self.out_shape), len(self.scratch)

    def run(self, r_in, r_out, r_scratch, conds):
        for cond, phase in zip(conds, self.phases):
            if phase is not None and cond is not None:
                pl.when(cond)(functools.partial(phase, r_in, r_out, r_scratch))


def _steps(ids, sizes):
    lin, total = 0, 1
    for i, n in zip(ids, sizes):
        lin, total = lin * n + i, total * n
    return lin == 0, lin == total // 2, lin == total - 1


def _ride_args(ride):
    if ride is None:
        return [], [], [], [], []
    n_in, n_out, _ = ride.counts()
    return [ANY] * n_in, ride.ins, [ANY] * n_out, ride.out_shape, ride.scratch


def _with_ride(body, ride, grid, n_prefetch, n_in, n_out):
    if ride is None:
        return body
    r_in, r_out, r_sc = ride.counts()

    def hosted(*refs):
        cuts = (n_prefetch, n_in, r_in, n_out, r_out)
        parts, pos = [], 0
        for n in cuts:
            parts.append(refs[pos:pos + n])
            pos += n
        pre, ins, ride_in, outs, ride_out = parts
        scratch, ride_scratch = refs[pos:len(refs) - r_sc], refs[len(refs) - r_sc:]
        first, mid, last = _steps([pl.program_id(d) for d in range(len(grid))], grid)
        ride.run(ride_in, ride_out, ride_scratch, (first, mid, None))
        body(*pre, *ins, *outs, *scratch)
        ride.run(ride_in, ride_out, ride_scratch, (None, None, last))

    return hosted


def _exchange_alone(name, ex):
    n_in, n_out, _ = ex.counts()

    def body(*refs):
        for phase in ex.phases:
            if phase is not None:
                phase(refs[:n_in], refs[n_in:n_in + n_out], refs[n_in + n_out:])

    return pl.pallas_call(
        body, name=name, in_specs=[ANY] * n_in, out_specs=[ANY] * n_out, out_shape=ex.out_shape,
        scratch_shapes=ex.scratch)(*ex.ins)


def _gather_exchange(shards):
    nw = len(shards)

    def parts(ins, outs, sems):
        send_sems, recv_sems, local_sems = sems
        x, y, c = _place()

        def slot(px, py, pc):
            return 4 * px + 2 * py + pc

        def copy(w, k, rows, to, src=None):
            return pltpu.make_async_remote_copy(
                src_ref=rows if src is None else src, dst_ref=rows, send_sem=send_sems.at[w, k],
                recv_sem=recv_sems.at[w, k], device_id=to, device_id_type=MESH)

        def halves(w):
            rows = shards[w].shape[0]
            cut = -(-(rows // 2) // 16) * 16
            return pl.ds(0, cut), pl.ds(cut, rows - cut)

        def plan(w, mine):
            side = c if mine else 1 - c
            upper, lower = halves(w)
            whole = lambda px, py: outs[w].at[slot(px, py, side)]
            top = lambda px, py: outs[w].at[slot(px, py, side), upper]
            bottom = lambda px, py: outs[w].at[slot(px, py, side), lower]
            xn, yn, sib = (1 - x, y, side), (x, 1 - y, side), (x, y, 1 - side)
            own = ins[w] if mine else None
            return [copy(w, 0, whole(x, y), sib, own), copy(w, 1, whole(x, y), xn, own),
                    copy(w, 2, whole(x, y), yn, own), copy(w, 3, top(1 - x, y), yn), copy(w, 4, bottom(x, 1 - y), xn),
                    copy(w, 5, whole(1 - x, y), sib), copy(w, 6, whole(x, 1 - y), sib),
                    copy(w, 7, top(1 - x, 1 - y), sib), copy(w, 8, bottom(1 - x, 1 - y), sib)]

        def arrivals(w):
            upper, lower = halves(w)
            at = lambda px, py, *rows: outs[w].at[(slot(px, py, c),) + rows]
            return {1: copy(w, 1, at(1 - x, y), (x, y, c)), 2: copy(w, 2, at(x, 1 - y), (x, y, c)),
                    3: copy(w, 3, at(1 - x, 1 - y, upper), (x, y, c)),
                    4: copy(w, 4, at(1 - x, 1 - y, lower), (x, y, c))}

        local = [pltpu.make_async_copy(ins[w], outs[w].at[slot(x, y, c)], local_sems.at[w]) for w in range(nw)]
        return plan, arrivals, local

    def start(ins, outs, sems):
        plan, _, local = parts(ins, outs, sems)
        for cp in local:
            cp.start()
        for w in range(nw):
            for k in (0, 1, 2):
                plan(w, True)[k].start()

    def middle(ins, outs, sems):
        plan, arrivals, _ = parts(ins, outs, sems)
        for landed, onward in ((1, (3, 5)), (2, (4, 6))):
            for w in range(nw):
                arrivals(w)[landed].wait_recv()
                for k in onward:
                    plan(w, True)[k].start()

    def finish(ins, outs, sems):
        plan, arrivals, local = parts(ins, outs, sems)
        for landed, onward in ((3, 7), (4, 8)):
            for w in range(nw):
                arrivals(w)[landed].wait_recv()
                plan(w, True)[onward].start()
        for w in range(nw):
            from_sibling = plan(w, False)
            for k in (0, 5, 6, 7, 8):
                from_sibling[k].wait_recv()
            for cp in plan(w, True):
                cp.wait_send()
        for cp in local:
            cp.wait()

    return _Exchange(
        shards, [jax.ShapeDtypeStruct((N_DEV,) + s.shape, s.dtype) for s in shards],
        [pltpu.SemaphoreType.DMA((nw, 9)), pltpu.SemaphoreType.DMA((nw, 9)), pltpu.SemaphoreType.DMA((nw,))],
        (start, middle, finish))


def _sibling_exchange(grads):
    nw = len(grads)

    def copies(ins, outs, sems):
        x, y, c = _place()
        return [pltpu.make_async_remote_copy(
            src_ref=ins[w].at[2 * p + (1 - c)], dst_ref=outs[w].at[p], send_sem=sems[0].at[w, p],
            recv_sem=sems[1].at[w, p], device_id=(x, y, 1 - c), device_id_type=MESH)
            for w in range(nw) for p in range(4)]

    def start(ins, outs, sems):
        for cp in copies(ins, outs, sems):
            cp.start()

    def finish(ins, outs, sems):
        for cp in copies(ins, outs, sems):
            cp.wait()

    return _Exchange(grads, [jax.ShapeDtypeStruct((4,) + g.shape[1:], g.dtype) for g in grads],
                     [pltpu.SemaphoreType.DMA((nw, 4)), pltpu.SemaphoreType.DMA((nw, 4))], (start, None, finish))


def _chips_exchange(sums):
    nw = len(sums)

    def copies(ins, outs, sems):
        x, y, c = _place()
        return [pltpu.make_async_remote_copy(
            src_ref=ins[w].at[2 * px + py], dst_ref=outs[w].at[k], send_sem=sems[0].at[w, k],
            recv_sem=sems[1].at[w, k], device_id=(px, py, c), device_id_type=MESH)
            for w in range(nw) for k, (px, py) in enumerate(_other_chips(x, y))]

    def start(ins, outs, sems):
        for cp in copies(ins, outs, sems):
            cp.start()

    def finish(ins, outs, sems):
        for cp in copies(ins, outs, sems):
            cp.wait()

    return _Exchange(sums, [jax.ShapeDtypeStruct((3,) + g.shape[1:], g.dtype) for g in sums],
                     [pltpu.SemaphoreType.DMA((nw, 3)), pltpu.SemaphoreType.DMA((nw, 3))], (start, None, finish))


def _pair_sum(name, g, got, c_arr, tr):
    _, rows, cols = g.shape
    tr = _tile_rows(rows, tr)

    def body(c_ref, a_ref, b_ref, o_ref):
        o_ref[...] = (a_ref[...].astype(F32) + b_ref[...].astype(F32)).astype(BF16)

    return pl.pallas_call(
        body, name=name,
        grid_spec=pltpu.PrefetchScalarGridSpec(
            num_scalar_prefetch=1, grid=(4, rows // tr),
            in_specs=[pl.BlockSpec((None, tr, cols), lambda p, r, cr: (2 * p + cr[0], r, 0)),
                      pl.BlockSpec((None, tr, cols), lambda p, r, cr: (p, r, 0))],
            out_specs=pl.BlockSpec((None, tr, cols), lambda p, r, cr: (p, r, 0))),
        out_shape=jax.ShapeDtypeStruct((4, rows, cols), BF16),
        compiler_params=_params(("parallel", "parallel")),
    )(c_arr, g, got)


def _tile_rows(rows, pref):
    t = min(rows, pref)
    while rows % t or t % 8:
        t -= 1
    return t


def _adam(w, g, m, v):
    m = ADAM_B1 * m + (1.0 - ADAM_B1) * g
    v = ADAM_B2 * v + (1.0 - ADAM_B2) * (g * g)
    m_hat = m / (1.0 - ADAM_B1 ** ADAM_STEP)
    v_hat = v / (1.0 - ADAM_B2 ** ADAM_STEP)
    return -ADAM_LR * (m_hat / (jnp.sqrt(v_hat) + ADAM_EPS) + ADAM_WD * w), m, v


def _adamw_shard(name, w, m, v, sums, got, chip_arr, tr, tc=None, ride=None):
    _, rows, cols = w.shape
    tr = _tile_rows(rows, tr)
    tc = cols if tc is None else _tile(cols, tc)
    grid = (rows // tr, cols // tc)
    ride_in_specs, ride_ins, ride_out_specs, ride_out_shape, ride_scratch = _ride_args(ride)

    def body(p_ref, w_ref, m_ref, v_ref, s_ref, r_ref, g_ref, d_ref, nm_ref, nv_ref):
        g = s_ref[...].astype(F32)
        for k in range(3):
            g = g + r_ref[k].astype(F32)
        g_ref[...] = g
        d_ref[...], nm_ref[...], nv_ref[...] = _adam(w_ref[...], g, m_ref[...], v_ref[...])

    tile = pl.BlockSpec((None, tr, tc), lambda r, q, pr: (0, r, q))
    res = pl.pallas_call(
        _with_ride(body, ride, grid, 1, 5, 4), name=name,
        grid_spec=pltpu.PrefetchScalarGridSpec(
            num_scalar_prefetch=1, grid=grid,
            in_specs=[tile, tile, tile,
                      pl.BlockSpec((None, tr, tc), lambda r, q, pr: (pr[0], r, q)),
                      pl.BlockSpec((3, tr, tc), lambda r, q, pr: (0, r, q))] + ride_in_specs,
            out_specs=[tile] * 4 + ride_out_specs, scratch_shapes=ride_scratch),
        out_shape=[jax.ShapeDtypeStruct((1, rows, cols), F32)] * 4 + ride_out_shape,
        compiler_params=_params(("arbitrary", "arbitrary") if ride else ("parallel", "parallel")),
    )(chip_arr, w, m, v, sums, got, *ride_ins)
    return (res[:4], res[4:]) if ride else res


def _gains_all_reduce_adam(grads, loss_part, ws, ms, vs):
    n = len(grads)
    widths = [g.shape[1] for g in grads]
    rows, width = -(-(n + 1) // 8) * 8, max(widths)

    def body(*refs):
        g_in, loss_in = refs[:n], refs[n]
        w_in, m_in, v_in = (refs[1 + k * n:1 + (k + 1) * n] for k in (1, 2, 3))
        outs = refs[1 + 4 * n:2 + 8 * n]
        g_out, d_out, m_out, v_out = (outs[k * n:(k + 1) * n] for k in range(4))
        loss_out = outs[4 * n]
        buf, send_sems, recv_sems = refs[2 + 8 * n:]
        x, y, c = _place()
        me = 4 * x + 2 * y + c
        buf[me] = jnp.zeros((rows, width), F32)
        for r in range(n):
            buf[me, r:r + 1, :widths[r]] = g_in[r][...]
        buf[me, n:n + 1, :LANES] = loss_in[...]
        peers = [(x, y, 1 - c)] + [(px, py, pc) for px, py in _other_chips(x, y) for pc in (c, 1 - c)]
        copies = []
        for k, peer in enumerate(peers):
            cp = pltpu.make_async_remote_copy(
                src_ref=buf.at[me], dst_ref=buf.at[me], send_sem=send_sems.at[k], recv_sem=recv_sems.at[k],
                device_id=peer, device_id_type=MESH)
            cp.start()
            copies.append(cp)
        for cp in copies:
            cp.wait()
        total = buf[0]
        for k in range(1, N_DEV):
            total = total + buf[k]
        for r in range(n):
            g = total[r:r + 1, :widths[r]]
            g_out[r][...] = g
            d_out[r][...], m_out[r][...], v_out[r][...] = _adam(w_in[r][...], g, m_in[r][...], v_in[r][...])
        loss_out[...] = total[n:n + 1, :LANES]

    vm = pl.BlockSpec(memory_space=pltpu.VMEM)
    shapes = [jax.ShapeDtypeStruct((1, w), F32) for w in widths]
    res = pl.pallas_call(
        body, name="gains_all_reduce_adamw",
        in_specs=[vm] * (4 * n + 1), out_specs=[vm] * (4 * n + 1),
        out_shape=shapes * 4 + [jax.ShapeDtypeStruct((1, LANES), F32)],
        scratch_shapes=[pltpu.VMEM((N_DEV, rows, width), F32), pltpu.SemaphoreType.DMA((7,)),
                        pltpu.SemaphoreType.DMA((7,))],
    )(*grads, loss_part, *ws, *ms, *vs)
    return res[:n], res[n:2 * n], res[2 * n:3 * n], res[3 * n:4 * n], res[4 * n]


IN_ORDER = ("r_q", "r_k", "r_v", "r_g", "c_q", "c_kv", "k_pe", "g_ret", "g_mla")
RET_HEAD = 2 * RET_QK + 2 * RET_V


def _make_layout(d, vw, qw, ql, kl, mla_w):
    width = {"r_q": qw, "r_k": qw, "r_v": vw, "r_g": vw, "c_q": ql, "c_kv": kl, "k_pe": ROPE, "g_ret": d, "g_mla": d}
    src, o = {}, 0
    for n in IN_ORDER:
        src[n] = o
        o += width[n]
    heads = vw // RET_V
    off, pieces, o = {}, [], 0

    def put(name, w, s):
        nonlocal o
        off.setdefault(name, o)
        pieces.append((o, w, s))
        o += w

    for n in ("g_ret", "g_mla", "c_q"):
        put(n, width[n], src[n])
    for h in range(heads):
        put("heads", RET_QK, src["r_q"] + h * RET_QK)
        put("heads", RET_QK, src["r_k"] + h * RET_QK)
        put("heads", RET_V, src["r_v"] + h * RET_V)
        put("heads", RET_V, src["r_g"] + h * RET_V)
    for n in ("c_kv", "k_pe"):
        put(n, width[n], src[n])
    total = off["k_pe"] + 2 * LANES
    for n, blk in (("g_ret", d), ("g_mla", d), ("c_q", ql), ("heads", RET_HEAD), ("c_kv", kl), ("k_pe", 2 * LANES)):
        assert off[n] % blk == 0
    return {"off": off, "pieces": pieces, "total": total, "n_in": sum(width.values()),
            "ret_heads": heads, "mla_heads": mla_w // VHEAD, "q_lora": ql, "kv_lora": kl}


def _cols_to_full(g):
    n, r, c = g.shape
    return jnp.transpose(g, (1, 0, 2)).reshape(r, n * c)


def _full_to_cols(w):
    r, c = w.shape
    return jnp.transpose(w.reshape(r, N_DEV, c // N_DEV), (1, 0, 2))


def _w_in_to_mine(g, lay):
    _, cols, d = g.shape
    parts, at = [], 0
    for o, w, s in lay["pieces"]:
        if o > at:
            parts.append(jnp.zeros((o - at, d), g.dtype))
        while w > 0:
            k, a = divmod(s, cols)
            take = min(w, cols - a)
            parts.append(g[k, a:a + take])
            s, w, o = s + take, w - take, o + take
        at = o
    parts.append(jnp.zeros((lay["total"] - at, d), g.dtype))
    return jnp.concatenate(parts, axis=0)


def _mine_to_blocks(g, lay):
    cols = lay["n_in"] // N_DEV
    by_src = sorted(lay["pieces"], key=lambda p: p[2])
    blocks = []
    for k in range(N_DEV):
        lo, hi, parts = k * cols, (k + 1) * cols, []
        for o, w, s in by_src:
            a, b = max(lo, s), min(hi, s + w)
            if a < b:
                parts.append(g[o + a - s:o + b - s])
        blocks.append(jnp.concatenate(parts, axis=0))
    return jnp.stack(blocks)


def _rope_tables(positions, half):
    inv = ROPE_THETA ** (-jnp.arange(half, dtype=F32) / half)
    ang = positions.astype(F32)[:, None] * inv
    return jnp.cos(ang), jnp.sin(ang)


def kernel(x, positions, norm_mix_g, w_in, ret_norm_g, w_ret_o, q_a_norm_g, w_q_b, kv_a_norm_g, w_kv_b, w_mla_o, w_out, norm_mlp_g, w_up, w_down, norm_f_g, loss_target, m_norm_mix_g, m_w_in, m_ret_norm_g, m_w_ret_o, m_q_a_norm_g, m_w_q_b, m_kv_a_norm_g, m_w_kv_b, m_w_mla_o, m_w_out, m_norm_mlp_g, m_w_up, m_w_down, m_norm_f_g, v_norm_mix_g, v_w_in, v_ret_norm_g, v_w_ret_o, v_q_a_norm_g, v_w_q_b, v_kv_a_norm_g, v_w_kv_b, v_w_mla_o, v_w_out, v_norm_mlp_g, v_w_up, v_w_down, v_norm_f_g):
    xs, tgt, pos = x[0], loss_target[0], positions[0]
    s, d = xs.shape
    mats = {"w_in": w_in[0], "w_ret_o": w_ret_o[0], "w_q_b": w_q_b[0], "w_kv_b": w_kv_b[0], "w_mla_o": w_mla_o[0],
            "w_out": w_out[0], "w_up": w_up[0], "w_down": w_down[0]}
    mat_w = {"w_in": w_in, "w_ret_o": w_ret_o, "w_q_b": w_q_b, "w_kv_b": w_kv_b, "w_mla_o": w_mla_o, "w_out": w_out,
             "w_up": w_up, "w_down": w_down}
    mat_m = {"w_in": m_w_in, "w_ret_o": m_w_ret_o, "w_q_b": m_w_q_b, "w_kv_b": m_w_kv_b, "w_mla_o": m_w_mla_o,
             "w_out": m_w_out, "w_up": m_w_up, "w_down": m_w_down}
    mat_v = {"w_in": v_w_in, "w_ret_o": v_w_ret_o, "w_q_b": v_w_q_b, "w_kv_b": v_w_kv_b, "w_mla_o": v_w_mla_o,
             "w_out": v_w_out, "w_up": v_w_up, "w_down": v_w_down}
    names = list(mats)
    col_sharded = ("w_in", "w_q_b", "w_kv_b", "w_up")
    vw = ret_norm_g.shape[1]
    mla_w = mats["w_mla_o"].shape[0] * N_DEV
    ql, kl = q_a_norm_g.shape[1], kv_a_norm_g.shape[1]
    n_in = mats["w_in"].shape[1] * N_DEV
    qw = (n_in - 2 * vw - ql - kl - ROPE - 2 * d) // 2
    lay = _make_layout(d, vw, qw, ql, kl, mla_w)
    assert lay["n_in"] == n_in
    heads_r, heads_m = lay["ret_heads"], lay["mla_heads"]

    shard16 = {n: mats[n].astype(BF16) for n in names}
    shard16["w_in"] = jnp.swapaxes(w_in, 1, 2)[0].astype(BF16)
    with_in_proj = ("w_ret_o", "w_q_b", "w_kv_b", "w_mla_o", "w_out")
    mlp = ("w_up", "w_down")
    by_device = ("w_up", "w_kv_b")
    full = {}

    def keep(group, gathered):
        for n, g in zip(group, gathered):
            if n not in by_device:
                g = _cols_to_full(g) if n in col_sharded else g.reshape(-1, g.shape[2])
            full[n] = g

    w_mine = _w_in_to_mine(_exchange_alone("gather_w_in", _gather_exchange([shard16["w_in"]]))[0], lay)

    c64, s64 = _rope_tables(pos, RET_QK // 2)
    cos_r = jnp.concatenate([c64, c64], axis=1)
    sin_r = jnp.concatenate([-s64, s64], axis=1)
    c32, s32 = _rope_tables(pos, ROPE // 2)
    z32, z64 = jnp.zeros_like(c32), jnp.zeros((s, LANES - ROPE), F32)
    cos_p = jnp.concatenate([c32, c32, z64], axis=1)
    sin_a = jnp.concatenate([-s32, z32, z64], axis=1)
    sin_b = jnp.concatenate([z32, s32, z64], axis=1)
    lg = jnp.log(1.0 - 2.0 ** (-5.0 - jnp.arange(heads_r, dtype=F32)))
    lgs = jnp.broadcast_to(lg[:, None, None], (heads_r, 8, LANES))

    tm = min(256, s)
    blk = min(512, s)
    t_att = min(512, s)

    u = _rms_fwd("norm_mix", xs, norm_mix_g, tm)
    proj, gathered = _mm("in_proj", u, w_mine, "nt", F32, tn=1280,
                         ride=_gather_exchange([shard16[n] for n in with_in_proj]))
    keep(with_in_proj, gathered)
    wq_pad = jnp.pad(full["w_q_b"].reshape(ql, heads_m, NOPE + ROPE),
                     ((0, 0), (0, 0), (0, QPAD - NOPE - ROPE))).reshape(ql, heads_m * QPAD)
    o_ret, states, ry = _ret_fwd(proj, lay, cos_r, sin_r, lgs, ret_norm_g, blk)
    y_ret = _mm("ret_out", ry, full["w_ret_o"], "nn", F32)
    cqn, ckvn, kpr = _mla_prep(proj, lay, q_a_norm_g, kv_a_norm_g, cos_p, sin_a, sin_b, tm)
    qf = _mm("q_up", cqn, wq_pad, "nn", BF16, extras=((cos_p, None), (sin_a, None), (sin_b, None)), epilogue=_q_operand)
    kv, kf = _mm("kv_up", ckvn, full["w_kv_b"], "nn", (BF16, BF16), b_by_device=True, extras=((kpr, None),),
                 epilogue=_k_operand)
    o_mla, lse, gathered = _attn_fwd(qf, kf, kv, lay, t_att, ride=_gather_exchange([shard16["w_up"]]))
    keep(("w_up",), gathered)
    gate_tile = _tile(d, 1024)
    y_mla, merged = _mm(
        "mla_out", o_mla, full["w_mla_o"], "nn", (F32, BF16), tm=512, tn=gate_tile,
        extras=((proj, lay["off"]["g_ret"] // gate_tile), (proj, lay["off"]["g_mla"] // gate_tile), y_ret),
        epilogue=lambda r, gr, gm, yr: (r, _sig(gr) * yr + _sig(gm) * r))
    h1, n2 = _mm("out_proj", merged, full["w_out"], "nn", (F32, BF16), tm=512, tn=d,
                 extras=(xs, (norm_mlp_g, "whole")), epilogue=_residual_norm)
    (z, act), gathered = _mm("mlp_up", n2, full["w_up"], "nn", (F32, BF16), b_by_device=True,
                             epilogue=lambda r: (r, jnp.square(jnp.maximum(r, 0.0))),
                             ride=_gather_exchange([shard16["w_down"]]))
    keep(("w_down",), gathered)
    dn = _mm("mlp_down", act, full["w_down"], "nn", F32, tk=4096)
    dh2, dh2_16, g_norm_f, loss_part = _final("loss_head", h1, dn, norm_f_g.reshape(1, d), tgt, tm)

    mx, my, mc = _place()
    c_arr = jnp.reshape(mc, (1,)).astype(jnp.int32)
    chip_arr = jnp.reshape(2 * mx + my, (1,)).astype(jnp.int32)
    sums, from_chips = {}, {}

    def blocks(group, grads):
        return [g if n in by_device else (_full_to_cols(g) if n in col_sharded else g.reshape((N_DEV,) + mats[n].shape))
                for n, g in zip(group, grads)]

    def pair_sums(group, mine, from_sibling):
        for n, g, r in zip(group, mine, from_sibling):
            sums[n] = _pair_sum("pair_sum_" + n, g, r, c_arr, 512)
        return [sums[n] for n in group]

    dz = _mm("mlp_down_dx", dh2_16, full["w_down"], "nt", BF16, extras=(z,),
             epilogue=lambda r, zz: (r * (2.0 * jnp.maximum(zz, 0.0)),))
    g_w_down = _mm("mlp_down_dw", act, dh2_16, "tn", BF16, tm=512, tn=d, tk=s)
    down_blocks = blocks(("w_down",), (g_w_down,))
    g_w_up, got_down = _mm("mlp_up_dw", n2, dz, "tn", BF16, tk=s, out_by_device=True,
                           ride=_sibling_exchange(down_blocks))
    dn2, got_up = _mm("mlp_up_dx", dz, full["w_up"], "nt", F32, tk=4096, b_by_device=True,
                      ride=_sibling_exchange([g_w_up]))
    mlp_sums = pair_sums(mlp, [g_w_up] + down_blocks, list(got_up) + list(got_down))
    dh1, dh1_16, g_norm_mlp = _rms_bwd("norm_mlp_bwd", dn2, h1, norm_mlp_g, dh2, tm, matmul_copy=True)
    assert lay["off"]["g_ret"] == 0 and lay["off"]["g_mla"] == d
    dy_ret, dy_mla, d_proj = _mm(
        "out_proj_dx", dh1_16, full["w_out"], "nt", (BF16, BF16, (BF16, 2, lay["total"])), tm=256, tn=d,
        extras=((proj, 0), (proj, 1), y_ret, y_mla), epilogue=_gate_grads)
    g_w_out = _mm("out_proj_dw", merged, dh1_16, "tn", BF16, tm=512, tn=d, tk=s)
    g_w_ret_o = _mm("ret_out_dw", ry, dy_ret, "tn", BF16, tm=512, tn=d, tk=s)
    g_w_mla_o = _mm("mla_out_dw", o_mla, dy_mla, "tn", BF16, tm=256, tn=d, tk=s)
    mixer = ("w_out", "w_ret_o", "w_mla_o")
    mixer_blocks = blocks(mixer, (g_w_out, g_w_ret_o, g_w_mla_o))
    d_ry, got = _mm("ret_out_dx", dy_ret, full["w_ret_o"], "nt", F32, ride=_sibling_exchange(mixer_blocks))
    mixer_sums = pair_sums(mixer, mixer_blocks, got)
    d_omla = _mm("mla_out_dx", dy_mla, full["w_mla_o"], "nt", F32)
    d_proj, g_ret_norm, got = _ret_bwd(proj, lay, cos_r, sin_r, lgs, states, d_ry, o_ret, ret_norm_g, d_proj, blk,
                                       ride=_chips_exchange(mixer_sums))
    from_chips.update(zip(mixer, got))
    delta = _attn_delta(d_omla, o_mla, lay, t_att)
    dqp, dkv, dkpe_parts, got = _attn_bwd(qf, kf, kv, lse, delta, d_omla, cos_p, sin_a, sin_b, lay, t_att,
                                          ride=_chips_exchange(mlp_sums))
    from_chips.update(zip(mlp, got))
    d_proj = _rope_key_grad(dkpe_parts, lay, d_proj, tm)
    d_cqn = _mm("q_up_dx", dqp, wq_pad, "nt", F32)
    g_wq_pad = _mm("q_up_dw", cqn, dqp, "tn", BF16)
    d_ckvn = _mm("kv_up_dx", dkv, full["w_kv_b"], "nt", F32, b_by_device=True)
    g_w_kv_b = _mm("kv_up_dw", ckvn, dkv, "tn", BF16, out_by_device=True)
    d_proj, g_q_a = _latent_norm_bwd("q_latent_bwd", proj, lay["off"]["c_q"], d_cqn, q_a_norm_g, d_proj, tm)
    d_proj, g_kv_a = _latent_norm_bwd("kv_latent_bwd", proj, lay["off"]["c_kv"], d_ckvn, kv_a_norm_g, d_proj, tm)
    g_w_q_b = g_wq_pad.reshape(ql, heads_m, QPAD)[:, :, :NOPE + ROPE].reshape(ql, heads_m * (NOPE + ROPE))
    latent = ("w_q_b", "w_kv_b")
    latent_blocks = blocks(latent, (g_w_q_b, g_w_kv_b))
    latent_sums = pair_sums(latent, latent_blocks,
                            _exchange_alone("latent_grads_to_sibling", _sibling_exchange(latent_blocks)))
    g_w_mine, got = _mm("in_proj_dw", d_proj, u, "tn", BF16, tm=768, tk=s, ride=_chips_exchange(latent_sums))
    from_chips.update(zip(latent, got))
    last = ("w_in",)
    last_blocks = [_mine_to_blocks(g_w_mine, lay)]
    upd = {}
    upd["w_up"], got = _adamw_shard("adamw_w_up", mat_w["w_up"], mat_m["w_up"], mat_v["w_up"], sums["w_up"],
                                    from_chips["w_up"], chip_arr, 256, ride=_sibling_exchange(last_blocks))
    last_sums = pair_sums(last, last_blocks, got)
    du, got = _mm("in_proj_dx", d_proj, w_mine, "nn", F32, ride=_chips_exchange(last_sums))
    from_chips.update(zip(last, got))
    grad_x, g_norm_mix = _rms_bwd("norm_mix_bwd", du, xs, norm_mix_g, dh1, tm)

    upd.update({n: _adamw_shard("adamw_" + n, mat_w[n], mat_m[n], mat_v[n], sums[n], from_chips[n], chip_arr, 256)
                for n in names if n not in ("w_in", "w_up")})
    flip = lambda a: jnp.swapaxes(a, 1, 2)
    upd["w_in"] = [flip(o) for o in _adamw_shard(
        "adamw_w_in", flip(w_in), flip(m_w_in), flip(v_w_in), sums["w_in"], from_chips["w_in"], chip_arr, 512, tc=512)]

    gains = [("norm_mix_g", norm_mix_g, m_norm_mix_g, v_norm_mix_g, g_norm_mix),
             ("ret_norm_g", ret_norm_g, m_ret_norm_g, v_ret_norm_g, g_ret_norm),
             ("q_a_norm_g", q_a_norm_g, m_q_a_norm_g, v_q_a_norm_g, g_q_a),
             ("kv_a_norm_g", kv_a_norm_g, m_kv_a_norm_g, v_kv_a_norm_g, g_kv_a),
             ("norm_mlp_g", norm_mlp_g, m_norm_mlp_g, v_norm_mlp_g, g_norm_mlp),
             ("norm_f_g", norm_f_g, m_norm_f_g, v_norm_f_g, g_norm_f)]
    as_row = lambda a: a.reshape(1, -1)
    g_sm, d_sm, m_sm, v_sm, loss_row = _gains_all_reduce_adam(
        [g[4] for g in gains], loss_part, *[[as_row(g[k]) for g in gains] for k in (1, 2, 3)])
    loss = loss_row[0, 0]
    small = {g[0]: [a[r].reshape(g[1].shape) for a in (g_sm, d_sm, m_sm, v_sm)] for r, g in enumerate(gains)}

    order = ["norm_mix_g", "w_in", "ret_norm_g", "w_ret_o", "q_a_norm_g", "w_q_b", "kv_a_norm_g", "w_kv_b", "w_mla_o",
             "w_out", "norm_mlp_g", "w_up", "w_down", "norm_f_g"]
    outs = [loss, grad_x[None]]
    for k in range(4):
        for n in order:
            outs.append(small[n][k] if n in small else upd[n][k])
    return tuple(outs)
```
